```python
import math
import jax, jax.numpy as jnp
from jax import lax
import numpy as np

D_MODEL = 2048
BATCH = 16
SEQ = 256
DEPTH = 4
DEC_BATCH = 2
DEC_SEQ = 4096
PAST_LEN = 512

GRID_W = 64
N_MIXERS = 3
N_MLA = (DEPTH + 2) // 3
N_HYENA = (DEPTH + 1) // 3
N_FNET = DEPTH // 3

MLA_HEADS = 16
QK_NOPE = 128
QK_ROPE = 64
V_DIM = 128
Q_RANK = 512
KV_RANK = 512
ROPE_THETA = 10000.0
AXIS_ROPE = QK_ROPE // 2
Q_BLOCK = 128

HY_ORDER = 2
HY_DIRS = 2
HY_CONV = 3
HY_BANDS = 16
HY_EMB = 1 + 2 * HY_BANDS
HY_FW = 64
HY_SHIFT = 0.05
HY_DECAY_MIN = math.log(100.0) / 1.5
HY_DECAY_MAX = math.log(100.0) / 0.3

FNET_GROUPS = 8
FNET_CG = D_MODEL // FNET_GROUPS

D_FF = -(-8 * D_MODEL // (3 * 256)) * 256

DN_ALPHA = (2 * DEPTH) ** 0.25
DN_BETA = (8 * DEPTH) ** -0.25
LN_EPS = 1e-5
RMS_EPS = 1e-6
N_MOD = 6

kernel_name = 'hybrid_mla_hyena_fnet_diffusion_step'


def layer_norm(x, g, b):
    xf = x.astype(jnp.float32)
    mu = jnp.mean(xf, axis=-1, keepdims=True)
    var = jnp.mean(jnp.square(xf - mu), axis=-1, keepdims=True)
    return ((xf - mu) * lax.rsqrt(var + LN_EPS) * g + b).astype(x.dtype)


def rms_norm(x, g):
    xf = x.astype(jnp.float32)
    ms = jnp.mean(jnp.square(xf), axis=-1, keepdims=True)
    return (xf * lax.rsqrt(ms + RMS_EPS) * g).astype(x.dtype)


def modulation(cond, w, b):
    m = (jax.nn.silu(cond) @ w + b).reshape(cond.shape[:-1] + (N_MOD, 1, D_MODEL))
    return jnp.moveaxis(m, -3, 0)


def post_norm(h, delta, g, b):
    return layer_norm(DN_ALPHA * h + delta, g, b)


def swiglu(u, w_gate, w_up, w_down):
    return (jax.nn.silu(u @ w_gate) * (u @ w_up)) @ w_down


def grid_rope_tables(rows):
    row = jnp.repeat(jnp.arange(rows), GRID_W).astype(jnp.float32)
    col = jnp.tile(jnp.arange(GRID_W), rows).astype(jnp.float32)
    inv = ROPE_THETA ** (-jnp.arange(0, AXIS_ROPE, 2, dtype=jnp.float32) / AXIS_ROPE)
    ang = jnp.concatenate([row[:, None] * inv, col[:, None] * inv], axis=-1)
    return jnp.cos(ang), jnp.sin(ang)


def apply_rope(x, cos, sin):
    xf = x.astype(jnp.float32)
    x1, x2 = xf[..., 0::2], xf[..., 1::2]
    out = jnp.stack([x1 * cos - x2 * sin, x1 * sin + x2 * cos], axis=-1)
    return out.reshape(x.shape).astype(x.dtype)


def mla_queries(u, w_dq, q_norm, w_uq):
    b, l, _ = u.shape
    q = (rms_norm(u @ w_dq, q_norm) @ w_uq).reshape(b, l, MLA_HEADS, QK_NOPE + QK_ROPE)
    return q[..., :QK_NOPE], q[..., QK_NOPE:]


def mla_compress(u, w_dkv, kv_norm, w_kr):
    return rms_norm(u @ w_dkv, kv_norm), u @ w_kr


def mla_expand(ckv, w_ukv):
    b, l, _ = ckv.shape
    kv = (ckv @ w_ukv).reshape(b, l, MLA_HEADS, QK_NOPE + V_DIM)
    return kv[..., :QK_NOPE], kv[..., QK_NOPE:]


def mla_attention(q_nope, q_rope, k_nope, k_rope, v):
    b, lq = q_nope.shape[0], q_nope.shape[1]
    nb = lq // Q_BLOCK
    scale = (QK_NOPE + QK_ROPE) ** -0.5

    def to_blocks(t):
        return jnp.moveaxis(t.reshape((b, nb, Q_BLOCK) + t.shape[2:]), 1, 0)

    def block(qs):
        qn, qr = qs
        s = jnp.einsum('bqhd,bkhd->bhqk', qn, k_nope) + jnp.einsum('bqhr,bkr->bhqk', qr, k_rope)
        p = jax.nn.softmax(s.astype(jnp.float32) * scale, axis=-1).astype(v.dtype)
        return jnp.einsum('bhqk,bkhd->bqhd', p, v)

    o = lax.map(block, (to_blocks(q_nope), to_blocks(q_rope)))
    return jnp.moveaxis(o, 0, 1).reshape(b, lq, MLA_HEADS * V_DIM)


def mla_context(u, w_dq, q_norm, w_uq, w_dkv, kv_norm, w_kr, w_ukv, w_o):
    qn, qr = mla_queries(u, w_dq, q_norm, w_uq)
    ckv, krope = mla_compress(u, w_dkv, kv_norm, w_kr)
    kn, v = mla_expand(ckv, w_ukv)
    return mla_attention(qn, qr, kn, krope, v) @ w_o, ckv, krope


def mla_latent(u, ckv_ctx, krope_ctx, cos, sin, w_dq, q_norm, w_uq, w_dkv, kv_norm, w_kr, w_ukv, w_o):
    qn, qr = mla_queries(u, w_dq, q_norm, w_uq)
    qr = apply_rope(qr, cos[:, None, :], sin[:, None, :])
    ckv, krope = mla_compress(u, w_dkv, kv_norm, w_kr)
    krope = apply_rope(krope, cos, sin)
    ckv_all = jnp.concatenate([ckv, ckv_ctx.astype(ckv.dtype)], axis=1)
    krope_all = jnp.concatenate([krope, krope_ctx.astype(krope.dtype)], axis=1)
    kn, v = mla_expand(ckv_all, w_ukv)
    return mla_attention(qn, qr, kn, krope_all, v) @ w_o


def short_conv(x, w, b):
    l = x.shape[1]
    xp = jnp.pad(x, ((0, 0), (HY_CONV // 2, HY_CONV // 2), (0, 0)))
    return sum(w[k] * xp[:, k:k + l] for k in range(HY_CONV)) + b


def hyena_filters(l, f_w1, f_b1, f_freq1, f_w2, f_b2, f_freq2, f_w3, log_decay):
    t = jnp.linspace(0.0, 1.0, l, dtype=jnp.float32)[:, None]
    t_idx = jnp.arange(l, dtype=jnp.float32)[:, None]
    bands = jnp.linspace(1e-4, HY_BANDS - 1, HY_BANDS, dtype=jnp.float32)
    w = 2.0 * math.pi * t_idx * bands / l
    feat = jnp.concatenate([t, jnp.cos(w), -jnp.sin(w)], axis=-1).astype(f_w1.dtype)
    h = jnp.sin(f_freq1 * (feat @ f_w1 + f_b1))
    h = jnp.sin(f_freq2 * (h @ f_w2 + f_b2))
    h = (h @ f_w3).astype(jnp.float32).reshape(l, HY_DIRS, HY_ORDER, D_MODEL)
    decay = jnp.exp(log_decay.astype(jnp.float32))
    h = h * (jnp.exp(-t[:, :, None, None] * decay) + HY_SHIFT)
    h = h * lax.rsqrt(jnp.sum(h * h, axis=(0, 1), keepdims=True) + 1e-12)
    fwd, bwd = h[:, 0], h[:, 1]
    k = jnp.concatenate([fwd, jnp.zeros((1, HY_ORDER, D_MODEL), jnp.float32), bwd[1:][::-1]], axis=0)
    return jnp.fft.rfft(k, axis=0)


def long_conv(z, kf, skip):
    l = z.shape[1]
    zf = z.astype(jnp.float32)
    y = jnp.fft.irfft(jnp.fft.rfft(zf, n=2 * l, axis=1) * kf, n=2 * l, axis=1)[:, :l]
    return (y + skip.astype(jnp.float32) * zf).astype(z.dtype)


def hyena(u, w_in, b_in, conv_w, conv_b, f_w1, f_b1, f_freq1, f_w2, f_b2, f_freq2, f_w3,
          log_decay, skip, w_out, b_out):
    proj = short_conv(u @ w_in + b_in, conv_w, conv_b)
    v, x1, x2 = jnp.split(proj, 3, axis=-1)
    kf = hyena_filters(u.shape[1], f_w1, f_b1, f_freq1, f_w2, f_b2, f_freq2, f_w3, log_decay)
    z = v
    for o, gate in enumerate((x1, x2)):
        z = gate * long_conv(z, kf[:, o], skip[o])
    return z @ w_out + b_out


def fnet(u, w_out, b_out):
    b, l, d = u.shape
    g = u.astype(jnp.float32).reshape(b, l, FNET_GROUPS, FNET_CG)
    f = jnp.fft.fft2(g, axes=(1, 3), norm='ortho').real
    return f.reshape(b, l, d).astype(u.dtype) @ w_out + b_out


def setup_inputs(seed: int = 0) -> dict:
    key = jax.random.key(seed)
    ks = iter(jax.random.split(key, 64))
    D = D_MODEL

    def nrm(shape, scale=1.0):
        return jax.random.normal(next(ks), shape, jnp.float32) * scale

    inp = {}
    inp['x_prompt'] = nrm((BATCH, SEQ, D))
    inp['x_sample'] = nrm((DEC_BATCH, DEC_SEQ, D))
    inp['c'] = nrm((DEC_BATCH, D))
    inp['cache_ckv'] = nrm((DEC_BATCH, N_MLA, PAST_LEN, KV_RANK))
    inp['cache_krope'] = nrm((DEC_BATCH, N_MLA, PAST_LEN, QK_ROPE))
    inp['c_ctx'] = nrm((D,))
    inp['ada_w'] = nrm((DEPTH, D, N_MOD * D), D ** -0.5)
    inp['ada_b'] = nrm((DEPTH, N_MOD * D), 0.02)
    inp['ln_g'] = 1.0 + nrm((DEPTH, 2, D), 0.05)
    inp['ln_b'] = nrm((DEPTH, 2, D), 0.02)
    inp['ffn_w_gate'] = nrm((DEPTH, D, D_FF), D ** -0.5)
    inp['ffn_w_up'] = nrm((DEPTH, D, D_FF), D ** -0.5)
    inp['ffn_w_down'] = nrm((DEPTH, D_FF, D), DN_BETA * D_FF ** -0.5)
    inp['mla_w_dq'] = nrm((N_MLA, D, Q_RANK), D ** -0.5)
    inp['mla_q_norm'] = 1.0 + nrm((N_MLA, Q_RANK), 0.05)
    inp['mla_w_uq'] = nrm((N_MLA, Q_RANK, MLA_HEADS * (QK_NOPE + QK_ROPE)), Q_RANK ** -0.5)
    inp['mla_w_dkv'] = nrm((N_MLA, D, KV_RANK), D ** -0.5)
    inp['mla_kv_norm'] = 1.0 + nrm((N_MLA, KV_RANK), 0.05)
    inp['mla_w_kr'] = nrm((N_MLA, D, QK_ROPE), D ** -0.5)
    inp['mla_w_ukv'] = nrm((N_MLA, KV_RANK, MLA_HEADS * (QK_NOPE + V_DIM)), KV_RANK ** -0.5)
    inp['mla_w_o'] = nrm((N_MLA, MLA_HEADS * V_DIM, D), DN_BETA * (MLA_HEADS * V_DIM) ** -0.5)
    inp['hy_w_in'] = nrm((N_HYENA, D, 3 * D), D ** -0.5)
    inp['hy_b_in'] = nrm((N_HYENA, 3 * D), 0.02)
    inp['hy_conv_w'] = nrm((N_HYENA, HY_CONV, 3 * D), HY_CONV ** -0.5)
    inp['hy_conv_b'] = nrm((N_HYENA, 3 * D), 0.02)
    inp['hy_f_w1'] = nrm((N_HYENA, HY_EMB, HY_FW), HY_EMB ** -0.5)
    inp['hy_f_b1'] = nrm((N_HYENA, HY_FW), 0.1)
    inp['hy_f_freq1'] = 1.0 + nrm((N_HYENA, HY_FW), 0.1)
    inp['hy_f_w2'] = nrm((N_HYENA, HY_FW, HY_FW), HY_FW ** -0.5)
    inp['hy_f_b2'] = nrm((N_HYENA, HY_FW), 0.1)
    inp['hy_f_freq2'] = 1.0 + nrm((N_HYENA, HY_FW), 0.1)
    inp['hy_f_w3'] = nrm((N_HYENA, HY_FW, HY_DIRS * HY_ORDER * D), HY_FW ** -0.5)
    inp['hy_log_decay'] = jax.random.uniform(next(ks), (N_HYENA, HY_DIRS, HY_ORDER, D), jnp.float32,
                                             minval=math.log(HY_DECAY_MIN), maxval=math.log(HY_DECAY_MAX))
    inp['hy_skip'] = nrm((N_HYENA, HY_ORDER, D), 1.0)
    inp['hy_w_out'] = nrm((N_HYENA, D, D), DN_BETA * D ** -0.5)
    inp['hy_b_out'] = nrm((N_HYENA, D), 0.02)
    inp['fn_w_out'] = nrm((N_FNET, D, D), DN_BETA * D ** -0.5)
    inp['fn_b_out'] = nrm((N_FNET, D), 0.02)
    return inp


def reference(x_prompt, x_sample, c, cache_ckv, cache_krope, c_ctx,
              ada_w, ada_b, ln_g, ln_b, ffn_w_gate, ffn_w_up, ffn_w_down,
              mla_w_dq, mla_q_norm, mla_w_uq, mla_w_dkv, mla_kv_norm, mla_w_kr, mla_w_ukv, mla_w_o,
              hy_w_in, hy_b_in, hy_conv_w, hy_conv_b, hy_f_w1, hy_f_b1, hy_f_freq1,
              hy_f_w2, hy_f_b2, hy_f_freq2, hy_f_w3, hy_log_decay, hy_skip, hy_w_out, hy_b_out,
              fn_w_out, fn_b_out):
    rows = x_sample.shape[1] // GRID_W
    cos, sin = grid_rope_tables(rows)
    h_ctx, h_lat = x_prompt, x_sample
    ckv_states, krope_states = [], []
    for i in range(DEPTH):
        kind, j = i % N_MIXERS, i // N_MIXERS
        m_ctx = modulation(c_ctx, ada_w[i], ada_b[i])
        m_lat = modulation(c, ada_w[i], ada_b[i])
        u_ctx = h_ctx * (1.0 + m_ctx[1]) + m_ctx[0]
        u_lat = h_lat * (1.0 + m_lat[1]) + m_lat[0]
        if kind == 0:
            mp = (mla_w_dq[j], mla_q_norm[j], mla_w_uq[j], mla_w_dkv[j], mla_kv_norm[j],
                  mla_w_kr[j], mla_w_ukv[j], mla_w_o[j])
            o_ctx, ckv, krope = mla_context(u_ctx, *mp)
            ckv_states.append(ckv)
            krope_states.append(krope)
            o_lat = mla_latent(u_lat, cache_ckv[:, j], cache_krope[:, j], cos, sin, *mp)
        elif kind == 1:
            hp = (hy_w_in[j], hy_b_in[j], hy_conv_w[j], hy_conv_b[j], hy_f_w1[j], hy_f_b1[j],
                  hy_f_freq1[j], hy_f_w2[j], hy_f_b2[j], hy_f_freq2[j], hy_f_w3[j],
                  hy_log_decay[j], hy_skip[j], hy_w_out[j], hy_b_out[j])
            o_ctx = hyena(u_ctx, *hp)
            o_lat = hyena(u_lat, *hp)
        else:
            o_ctx = fnet(u_ctx, fn_w_out[j], fn_b_out[j])
            o_lat = fnet(u_lat, fn_w_out[j], fn_b_out[j])
        h_ctx = post_norm(h_ctx, m_ctx[2] * o_ctx, ln_g[i, 0], ln_b[i, 0])
        h_lat = post_norm(h_lat, m_lat[2] * o_lat, ln_g[i, 0], ln_b[i, 0])
        f_ctx = swiglu(h_ctx * (1.0 + m_ctx[4]) + m_ctx[3], ffn_w_gate[i], ffn_w_up[i], ffn_w_down[i])
        f_lat = swiglu(h_lat * (1.0 + m_lat[4]) + m_lat[3], ffn_w_gate[i], ffn_w_up[i], ffn_w_down[i])
        h_ctx = post_norm(h_ctx, m_ctx[5] * f_ctx, ln_g[i, 1], ln_b[i, 1])
        h_lat = post_norm(h_lat, m_lat[5] * f_lat, ln_g[i, 1], ln_b[i, 1])
    state_ckv = jnp.stack(ckv_states, axis=1)
    state_krope = jnp.stack(krope_states, axis=1)
    return (h_ctx, h_lat, state_ckv, state_krope)
```

```python
import functools
import math

import jax
import jax.numpy as jnp
from jax import lax
from jax.experimental import pallas as pl
from jax.experimental.pallas import tpu as pltpu

F32 = jnp.float32
BF16 = jnp.bfloat16

D_MODEL = 2048
BATCH = 16
SEQ = 256
DEPTH = 4
DEC_BATCH = 2
DEC_SEQ = 4096
PAST_LEN = 512
GRID_W = 64
N_MIXERS = 3
MLA_HEADS = 16
QK_NOPE = 128
QK_ROPE = 64
V_DIM = 128
Q_RANK = 512
KV_RANK = 512
ROPE_THETA = 10000.0
AXIS_ROPE = QK_ROPE // 2
HY_ORDER = 2
HY_DIRS = 2
HY_CONV = 3
HY_BANDS = 16
HY_EMB = 1 + 2 * HY_BANDS
HY_FW = 64
HY_SHIFT = 0.05
FNET_GROUPS = 8
FNET_CG = D_MODEL // FNET_GROUPS
D_FF = -(-8 * D_MODEL // (3 * 256)) * 256
DN_ALPHA = (2 * DEPTH) ** 0.25
LN_EPS = 1e-5
RMS_EPS = 1e-6
N_MOD = 6

ROWS_CTX = BATCH * SEQ
ROWS_LAT = DEC_BATCH * DEC_SEQ
ROWS = ROWS_CTX + ROWS_LAT
N_COND = 1 + DEC_BATCH
COND_PAD = 8
HEAD_W = QK_NOPE + 2 * QK_ROPE
ATT_SCALE = (QK_NOPE + QK_ROPE) ** -0.5

V7X_VMEM_BYTES = 64 * 2 ** 20
VMEM_CAP_BYTES = V7X_VMEM_BYTES * 7 // 8
VMEM_FLOOR_BYTES = 32 * 2 ** 20


def _vmem_limit(pipelined, resident=0, temps=0):
    est = 2 * pipelined + resident + temps
    return int(min(max(est, VMEM_FLOOR_BYTES), VMEM_CAP_BYTES))


def _params(semantics, vmem):
    return pltpu.CompilerParams(dimension_semantics=semantics, vmem_limit_bytes=vmem)


def _nbytes(shape, dtype):
    return math.prod(shape) * jnp.dtype(dtype).itemsize


def _group_of_tile(i, tm):
    n_ctx = ROWS_CTX // tm
    return jnp.where(i < n_ctx, 0, 1 + (i - n_ctx) // (DEC_SEQ // tm))


def _mod_spec(which, tm):
    return pl.BlockSpec((None, 1, D_MODEL),
                        lambda i, *_: (which * COND_PAD + _group_of_tile(i, tm), 0, 0))


def _row_spec(width=D_MODEL):
    return pl.BlockSpec((1, width), lambda *_: (0, 0))


def _modulate(h, shift, scale):
    return h * (1.0 + scale) + shift


def _post_norm(h, delta, g, b):
    z = DN_ALPHA * h + delta
    mu = jnp.mean(z, axis=-1, keepdims=True)
    zc = z - mu
    var = jnp.mean(zc * zc, axis=-1, keepdims=True)
    return zc * lax.rsqrt(var + LN_EPS) * g + b


def _rms_norm(x, g):
    ms = jnp.mean(x * x, axis=-1, keepdims=True)
    return x * lax.rsqrt(ms + RMS_EPS) * g


def _dot(a, b):
    return jnp.dot(a, b, preferred_element_type=F32)


def _modvec_kernel(c_ref, w_ref, b_ref, o_ref):
    a = jax.nn.silu(c_ref[...]).astype(BF16)
    o_ref[...] = _dot(a, w_ref[...].astype(BF16)) + b_ref[...]


def _modulation_vectors(cond, ada_w, ada_b):
    tn = 1024
    n = N_MOD * D_MODEL
    out = pl.pallas_call(
        _modvec_kernel,
        grid=(DEPTH, n // tn),
        in_specs=[pl.BlockSpec((COND_PAD, D_MODEL), lambda l, j: (0, 0)),
                  pl.BlockSpec((None, D_MODEL, tn), lambda l, j: (l, 0, j)),
                  pl.BlockSpec((None, 1, tn), lambda l, j: (l, 0, j))],
        out_specs=pl.BlockSpec((None, COND_PAD, tn), lambda l, j: (l, 0, j)),
        out_shape=jax.ShapeDtypeStruct((DEPTH, COND_PAD, n), F32),
        compiler_params=_params(("parallel", "parallel"),
                                _vmem_limit(_nbytes((D_MODEL, tn), F32),
                                            temps=_nbytes((D_MODEL, tn), BF16))),
        name="modvec",
    )(cond, ada_w, ada_b.reshape(DEPTH, 1, n))
    out = out.reshape(DEPTH, COND_PAD, N_MOD, D_MODEL).transpose(0, 2, 1, 3)
    return out.reshape(DEPTH, N_MOD * COND_PAD, 1, D_MODEL)


def _mod_mm_kernel(h_ref, sh_ref, sc_ref, w_ref, b_ref, o_ref, u_ref):
    @pl.when(pl.program_id(1) == 0)
    def _():
        u_ref[...] = _modulate(h_ref[...], sh_ref[...], sc_ref[...]).astype(BF16)

    o_ref[...] = (_dot(u_ref[...], w_ref[...]) + b_ref[...]).astype(o_ref.dtype)


def _mod_mm(h, mods, w, b, which_shift, out_dtype=F32, tm=512, tn=512):
    k, n = w.shape
    return pl.pallas_call(
        _mod_mm_kernel,
        grid=(ROWS // tm, n // tn),
        in_specs=[pl.BlockSpec((tm, k), lambda i, j: (i, 0)),
                  _mod_spec(which_shift, tm), _mod_spec(which_shift + 1, tm),
                  pl.BlockSpec((k, tn), lambda i, j: (0, j)),
                  pl.BlockSpec((1, tn), lambda i, j: (0, j))],
        out_specs=pl.BlockSpec((tm, tn), lambda i, j: (i, j)),
        out_shape=jax.ShapeDtypeStruct((ROWS, n), out_dtype),
        scratch_shapes=[pltpu.VMEM((tm, k), BF16)],
        compiler_params=_params(("parallel", "arbitrary"),
                                _vmem_limit(_nbytes((tm, k), F32) + _nbytes((k, tn), BF16)
                                            + _nbytes((tm, tn), F32),
                                            resident=_nbytes((tm, k), BF16),
                                            temps=_nbytes((tm, k), F32))),
        name="mod_mm",
    )(h, mods, mods, w, b.reshape(1, n))


def _mm_postnorm_kernel(a_ref, w_ref, bias_ref, h_ref, gate_ref, g_ref, b_ref, o_ref):
    y = _dot(a_ref[...], w_ref[...]) + bias_ref[...]
    o_ref[...] = _post_norm(h_ref[...], gate_ref[...] * y, g_ref[...], b_ref[...])


def _mm_postnorm(a, w, bias, h, mods, which_gate, ln_g, ln_b, tm=256):
    k = a.shape[1]
    return pl.pallas_call(
        _mm_postnorm_kernel,
        grid=(ROWS // tm,),
        in_specs=[pl.BlockSpec((tm, k), lambda i: (i, 0)),
                  pl.BlockSpec((k, D_MODEL), lambda i: (0, 0)),
                  _row_spec(),
                  pl.BlockSpec((tm, D_MODEL), lambda i: (i, 0)),
                  _mod_spec(which_gate, tm), _row_spec(), _row_spec()],
        out_specs=pl.BlockSpec((tm, D_MODEL), lambda i: (i, 0)),
        out_shape=jax.ShapeDtypeStruct((ROWS, D_MODEL), F32),
        compiler_params=_params(("parallel",),
                                _vmem_limit(_nbytes((tm, k), BF16) + _nbytes((k, D_MODEL), BF16)
                                            + 2 * _nbytes((tm, D_MODEL), F32),
                                            temps=3 * _nbytes((tm, D_MODEL), F32))),
        name="mm_postnorm",
    )(a, w, bias.reshape(1, D_MODEL), h, mods, ln_g.reshape(1, D_MODEL), ln_b.reshape(1, D_MODEL))


def _ffn_kernel(h_ref, sh_ref, sc_ref, gate_ref, g_ref, b_ref, wg_ref, wu_ref, wd_ref,
                o_ref, u_ref):
    f = pl.program_id(1)

    @pl.when(f == 0)
    def _():
        u_ref[...] = _modulate(h_ref[...], sh_ref[...], sc_ref[...]).astype(BF16)
        o_ref[...] = jnp.zeros_like(o_ref)

    u = u_ref[...]
    act = (jax.nn.silu(_dot(u, wg_ref[...])) * _dot(u, wu_ref[...])).astype(BF16)
    o_ref[...] += _dot(act, wd_ref[...])

    @pl.when(f == pl.num_programs(1) - 1)
    def _():
        o_ref[...] = _post_norm(h_ref[...], gate_ref[...] * o_ref[...], g_ref[...], b_ref[...])


def _ffn(h, mods, w_gate, w_up, w_down, ln_g, ln_b, tm=512, tf=512):
    return pl.pallas_call(
        _ffn_kernel,
        grid=(ROWS // tm, D_FF // tf),
        in_specs=[pl.BlockSpec((tm, D_MODEL), lambda i, f: (i, 0)),
                  _mod_spec(3, tm), _mod_spec(4, tm), _mod_spec(5, tm),
                  _row_spec(), _row_spec(),
                  pl.BlockSpec((D_MODEL, tf), lambda i, f: (0, f)),
                  pl.BlockSpec((D_MODEL, tf), lambda i, f: (0, f)),
                  pl.BlockSpec((tf, D_MODEL), lambda i, f: (f, 0))],
        out_specs=pl.BlockSpec((tm, D_MODEL), lambda i, f: (i, 0)),
        out_shape=jax.ShapeDtypeStruct((ROWS, D_MODEL), F32),
        scratch_shapes=[pltpu.VMEM((tm, D_MODEL), BF16)],
        compiler_params=_params(("parallel", "arbitrary"),
                                _vmem_limit(2 * _nbytes((tm, D_MODEL), F32)
                                            + 3 * _nbytes((D_MODEL, tf), BF16),
                                            resident=_nbytes((tm, D_MODEL), BF16),
                                            temps=3 * _nbytes((tm, D_MODEL), F32))),
        name="ffn",
    )(h, mods, mods, mods, ln_g.reshape(1, D_MODEL), ln_b.reshape(1, D_MODEL),
      w_gate, w_up, w_down)


def _mla_down_kernel(h_ref, sh_ref, sc_ref, wdq_ref, wdkv_ref, wkr_ref, qn_ref, kvn_ref,
                     rope_ref, cq_ref, ckv_ref, kr_ref, kr2_ref):
    u = _modulate(h_ref[...], sh_ref[...], sc_ref[...]).astype(BF16)
    cq_ref[...] = _rms_norm(_dot(u, wdq_ref[...]), qn_ref[...]).astype(BF16)
    ckv_ref[...] = _rms_norm(_dot(u, wdkv_ref[...]), kvn_ref[...])
    t = _dot(u, wkr_ref[...])
    kr_ref[...] = t[:, :QK_ROPE]
    v = t * rope_ref[...]
    kr2_ref[...] = (v + pltpu.roll(v, QK_ROPE, 1)).astype(BF16)


def _mla_down(h, mods, w_dq, w_dkv, w_kr2, q_norm, kv_norm, rope_tab, tm=512):
    row = lambda width: pl.BlockSpec((tm, width), lambda i: (i, 0))
    full = lambda shape: pl.BlockSpec(shape, lambda i: (0, 0))
    return pl.pallas_call(
        _mla_down_kernel,
        grid=(ROWS // tm,),
        in_specs=[row(D_MODEL), _mod_spec(0, tm), _mod_spec(1, tm),
                  full((D_MODEL, Q_RANK)), full((D_MODEL, KV_RANK)), full((D_MODEL, 2 * QK_ROPE)),
                  _row_spec(Q_RANK), _row_spec(KV_RANK), row(2 * QK_ROPE)],
        out_specs=[row(Q_RANK), row(KV_RANK), row(QK_ROPE), row(2 * QK_ROPE)],
        out_shape=[jax.ShapeDtypeStruct((ROWS, Q_RANK), BF16),
                   jax.ShapeDtypeStruct((ROWS, KV_RANK), F32),
                   jax.ShapeDtypeStruct((ROWS, QK_ROPE), F32),
                   jax.ShapeDtypeStruct((ROWS, 2 * QK_ROPE), BF16)],
        compiler_params=_params(("parallel",),
                                _vmem_limit(_nbytes((tm, D_MODEL), F32)
                                            + _nbytes((D_MODEL, Q_RANK + KV_RANK + 2 * QK_ROPE), BF16)
                                            + 3 * _nbytes((tm, KV_RANK), F32),
                                            temps=2 * _nbytes((tm, D_MODEL), F32))),
        name="mla_down",
    )(h, mods, mods, w_dq, w_dkv, w_kr2, q_norm.reshape(1, Q_RANK), kv_norm.reshape(1, KV_RANK),
      rope_tab)


def _q_up_kernel(cq_ref, w_ref, rope_ref, q_ref):
    r = _dot(cq_ref[...], w_ref[...])
    q_ref[:, :QK_NOPE] = r[:, :QK_NOPE].astype(BF16)
    q_ref[:, QK_NOPE:] = (r[:, QK_NOPE:] * rope_ref[...]).astype(BF16)


def _q_up(cq, w_q, rope_tab, tm=512):
    return pl.pallas_call(
        _q_up_kernel,
        grid=(ROWS // tm, MLA_HEADS),
        in_specs=[pl.BlockSpec((tm, Q_RANK), lambda i, h: (i, 0)),
                  pl.BlockSpec((Q_RANK, HEAD_W), lambda i, h: (0, h)),
                  pl.BlockSpec((tm, 2 * QK_ROPE), lambda i, h: (i, 0))],
        out_specs=pl.BlockSpec((None, tm, HEAD_W), lambda i, h: (h, i, 0)),
        out_shape=jax.ShapeDtypeStruct((MLA_HEADS, ROWS, HEAD_W), BF16),
        compiler_params=_params(("parallel", "arbitrary"), VMEM_FLOOR_BYTES),
        name="q_up",
    )(cq, w_q, rope_tab)


def _kv_expand_kernel(ckv_ref, kr2_ref, w_ref, k_ref, v_ref):
    r = _dot(ckv_ref[...].astype(BF16), w_ref[...])
    k_ref[:, :QK_NOPE] = r[:, :QK_NOPE].astype(BF16)
    k_ref[:, QK_NOPE:] = kr2_ref[...]
    v_ref[...] = r[:, QK_NOPE:].astype(BF16)


def _kv_expand(ckv, kr2, w_ukv, tm=512):
    rows = ckv.shape[0]
    return pl.pallas_call(
        _kv_expand_kernel,
        grid=(rows // tm, MLA_HEADS),
        in_specs=[pl.BlockSpec((tm, KV_RANK), lambda i, h: (i, 0)),
                  pl.BlockSpec((tm, 2 * QK_ROPE), lambda i, h: (i, 0)),
                  pl.BlockSpec((KV_RANK, QK_NOPE + V_DIM), lambda i, h: (0, h))],
        out_specs=[pl.BlockSpec((None, tm, HEAD_W), lambda i, h: (h, i, 0)),
                   pl.BlockSpec((None, tm, V_DIM), lambda i, h: (h, i, 0))],
        out_shape=[jax.ShapeDtypeStruct((MLA_HEADS, rows, HEAD_W), BF16),
                   jax.ShapeDtypeStruct((MLA_HEADS, rows, V_DIM), BF16)],
        compiler_params=_params(("parallel", "arbitrary"), VMEM_FLOOR_BYTES),
        name="kv_expand",
    )(ckv, kr2, w_ukv)


def _scores(q, k):
    return lax.dot_general(q, k, (((1,), (1,)), ((), ())), preferred_element_type=F32) * ATT_SCALE


def _attn_ctx_kernel(q_ref, k_ref, v_ref, o_ref):
    s = _scores(q_ref[...], k_ref[...])
    p = jnp.exp(s - jnp.max(s, axis=-1, keepdims=True))
    o = _dot(p.astype(BF16), v_ref[...])
    o_ref[...] = (o / jnp.sum(p, axis=-1, keepdims=True)).astype(BF16)


def _attn_lat_kernel(q_ref, k1_ref, v1_ref, k2_ref, v2_ref, prev_ref, o_ref):
    del prev_ref
    q = q_ref[...]
    s1 = _scores(q, k1_ref[...])
    s2 = _scores(q, k2_ref[...])
    m = jnp.maximum(jnp.max(s1, axis=-1, keepdims=True), jnp.max(s2, axis=-1, keepdims=True))
    p1 = jnp.exp(s1 - m)
    p2 = jnp.exp(s2 - m)
    o = _dot(p1.astype(BF16), v1_ref[...]) + _dot(p2.astype(BF16), v2_ref[...])
    l = jnp.sum(p1, axis=-1, keepdims=True) + jnp.sum(p2, axis=-1, keepdims=True)
    o_ref[...] = (o / l).astype(BF16)


def _attention(q, k_tok, v_tok, k_cache, v_cache, tq=256):
    out_shape = jax.ShapeDtypeStruct((ROWS, MLA_HEADS * V_DIM), BF16)
    o = pl.pallas_call(
        _attn_ctx_kernel,
        grid=(MLA_HEADS, BATCH),
        in_specs=[pl.BlockSpec((None, SEQ, HEAD_W), lambda h, s: (h, s, 0)),
                  pl.BlockSpec((None, SEQ, HEAD_W), lambda h, s: (h, s, 0)),
                  pl.BlockSpec((None, SEQ, V_DIM), lambda h, s: (h, s, 0))],
        out_specs=pl.BlockSpec((SEQ, V_DIM), lambda h, s: (s, h)),
        out_shape=out_shape,
        compiler_params=_params(("parallel", "parallel"), VMEM_FLOOR_BYTES),
        name="attn_ctx",
    )(q, k_tok, v_tok)

    lat0 = ROWS_CTX // DEC_SEQ
    q0 = ROWS_CTX // tq
    nq = DEC_SEQ // tq
    return pl.pallas_call(
        _attn_lat_kernel,
        grid=(DEC_BATCH, MLA_HEADS, nq),
        in_specs=[pl.BlockSpec((None, tq, HEAD_W), lambda b, h, i: (h, q0 + b * nq + i, 0)),
                  pl.BlockSpec((None, DEC_SEQ, HEAD_W), lambda b, h, i: (h, lat0 + b, 0)),
                  pl.BlockSpec((None, DEC_SEQ, V_DIM), lambda b, h, i: (h, lat0 + b, 0)),
                  pl.BlockSpec((None, PAST_LEN, HEAD_W), lambda b, h, i: (h, b, 0)),
                  pl.BlockSpec((None, PAST_LEN, V_DIM), lambda b, h, i: (h, b, 0)),
                  pl.BlockSpec(memory_space=pl.ANY)],
        out_specs=pl.BlockSpec((tq, V_DIM), lambda b, h, i: (q0 + b * nq + i, h)),
        out_shape=out_shape,
        input_output_aliases={5: 0},
        compiler_params=_params(("parallel", "parallel", "arbitrary"),
                                _vmem_limit(_nbytes((DEC_SEQ + PAST_LEN, HEAD_W + V_DIM), BF16),
                                            temps=3 * _nbytes((tq, DEC_SEQ + PAST_LEN), F32))),
        name="attn_lat",
    )(q, k_tok, v_tok, k_cache, v_cache, o)


def _mm_kernel(a_ref, b_ref, o_ref, acc_ref):
    k = pl.program_id(2)

    @pl.when(k == 0)
    def _():
        acc_ref[...] = jnp.zeros_like(acc_ref)

    acc_ref[...] += _dot(a_ref[...], b_ref[...])

    @pl.when(k == pl.num_programs(2) - 1)
    def _():
        o_ref[...] = acc_ref[...].astype(o_ref.dtype)


def _mm(a, b, out_dtype=F32, tm=512, tn=512, tk=512):
    m, kk = a.shape
    n = b.shape[1]
    tm, tn, tk = min(tm, m), min(tn, n), min(tk, kk)
    return pl.pallas_call(
        _mm_kernel,
        grid=(m // tm, n // tn, kk // tk),
        in_specs=[pl.BlockSpec((tm, tk), lambda i, j, k: (i, k)),
                  pl.BlockSpec((tk, tn), lambda i, j, k: (k, j))],
        out_specs=pl.BlockSpec((tm, tn), lambda i, j, k: (i, j)),
        out_shape=jax.ShapeDtypeStruct((m, n), out_dtype),
        scratch_shapes=[pltpu.VMEM((tm, tn), F32)],
        compiler_params=_params(("parallel", "parallel", "arbitrary"), VMEM_FLOOR_BYTES),
        name="mm",
    )(a, b)


def _seq_tiles(length):
    if length >= 1024:
        return 1024, 512, 512
    return length, D_MODEL, length


def _dft_fwd_kernel(c_ref, s_ref, z_ref, kr_ref, ks_ref, yr_ref, ys_ref, accr_ref, accs_ref):
    k = pl.program_id(3)

    @pl.when(k == 0)
    def _():
        accr_ref[...] = jnp.zeros_like(accr_ref)
        accs_ref[...] = jnp.zeros_like(accs_ref)

    z = z_ref[...].astype(BF16)
    accr_ref[...] += _dot(c_ref[...], z)
    accs_ref[...] += _dot(s_ref[...], z)

    @pl.when(k == pl.num_programs(3) - 1)
    def _():
        zr, zs, kr, ks = accr_ref[...], accs_ref[...], kr_ref[...], ks_ref[...]
        yr_ref[...] = (zr * kr - zs * ks).astype(BF16)
        ys_ref[...] = (zr * ks + zs * kr).astype(BF16)


def _dft_fwd(tabs, z, z_col0, kr, ks, order, n_seq, length):
    tf, tn, tk = _seq_tiles(length)
    nb, nf, nk = D_MODEL // tn, length // tf, length // tk
    zc0, kc0 = z_col0 // tn, order * nb
    out = jax.ShapeDtypeStruct((n_seq * length, D_MODEL), BF16)
    return pl.pallas_call(
        _dft_fwd_kernel,
        grid=(n_seq, nf, nb, nk),
        in_specs=[pl.BlockSpec((tf, tk), lambda s, f, n, k: (f, k)),
                  pl.BlockSpec((tf, tk), lambda s, f, n, k: (f, k)),
                  pl.BlockSpec((tk, tn), lambda s, f, n, k: (s * nk + k, zc0 + n)),
                  pl.BlockSpec((tf, tn), lambda s, f, n, k: (f, kc0 + n)),
                  pl.BlockSpec((tf, tn), lambda s, f, n, k: (f, kc0 + n))],
        out_specs=[pl.BlockSpec((tf, tn), lambda s, f, n, k: (s * nf + f, n)),
                   pl.BlockSpec((tf, tn), lambda s, f, n, k: (s * nf + f, n))],
        out_shape=[out, out],
        scratch_shapes=[pltpu.VMEM((tf, tn), F32), pltpu.VMEM((tf, tn), F32)],
        compiler_params=_params(("parallel", "parallel", "parallel", "arbitrary"),
                                _vmem_limit(2 * _nbytes((tf, tk), BF16) + _nbytes((tk, tn), F32)
                                            + 2 * _nbytes((tf, tn), F32) + 2 * _nbytes((tf, tn), BF16),
                                            resident=2 * _nbytes((tf, tn), F32),
                                            temps=4 * _nbytes((tf, tn), F32))),
        name="dft_fwd",
    )(tabs["c"], tabs["s"], z, kr, ks)


def _dft_inv_kernel(inv_len, ct_ref, st_ref, yr_ref, ys_ref, z_ref, gate_ref, skip_ref,
                    o_ref, acc_ref):
    k = pl.program_id(3)

    @pl.when(k == 0)
    def _():
        acc_ref[...] = jnp.zeros_like(acc_ref)

    acc_ref[...] += _dot(ct_ref[...], yr_ref[...]) + _dot(st_ref[...], ys_ref[...])

    @pl.when(k == pl.num_programs(3) - 1)
    def _():
        y = acc_ref[...] * inv_len + skip_ref[...] * z_ref[...]
        o_ref[...] = (gate_ref[...] * y).astype(o_ref.dtype)


def _dft_inv(tabs, yr, ys, z, z_col0, gate, gate_col0, skip, order, n_seq, length, out_dtype):
    tt, tn, tk = _seq_tiles(length)
    nb, nt, nk = D_MODEL // tn, length // tt, length // tk
    zc0, gc0 = z_col0 // tn, gate_col0 // tn
    return pl.pallas_call(
        functools.partial(_dft_inv_kernel, 1.0 / length),
        grid=(n_seq, nt, nb, nk),
        in_specs=[pl.BlockSpec((tt, tk), lambda s, t, n, k: (t, k)),
                  pl.BlockSpec((tt, tk), lambda s, t, n, k: (t, k)),
                  pl.BlockSpec((tk, tn), lambda s, t, n, k: (s * nk + k, n)),
                  pl.BlockSpec((tk, tn), lambda s, t, n, k: (s * nk + k, n)),
                  pl.BlockSpec((tt, tn), lambda s, t, n, k: (s * nt + t, zc0 + n)),
                  pl.BlockSpec((tt, tn), lambda s, t, n, k: (s * nt + t, gc0 + n)),
                  pl.BlockSpec((None, 1, tn), lambda s, t, n, k: (order, 0, n))],
        out_specs=pl.BlockSpec((tt, tn), lambda s, t, n, k: (s * nt + t, n)),
        out_shape=jax.ShapeDtypeStruct((n_seq * length, D_MODEL), out_dtype),
        scratch_shapes=[pltpu.VMEM((tt, tn), F32)],
        compiler_params=_params(("parallel", "parallel", "parallel", "arbitrary"),
                                _vmem_limit(2 * _nbytes((tt, tk), BF16) + 2 * _nbytes((tk, tn), BF16)
                                            + 3 * _nbytes((tt, tn), F32),
                                            resident=_nbytes((tt, tn), F32),
                                            temps=3 * _nbytes((tt, tn), F32))),
        name="dft_inv",
    )(tabs["ct"], tabs["st"], yr, ys, z, gate, skip)


def _shortconv_kernel(x_ref, w_ref, b_ref, o_ref):
    x = x_ref[...]
    n = x.shape[0]
    row = lax.broadcasted_iota(jnp.int32, x.shape, 0)
    before = jnp.where(row == 0, 0.0, pltpu.roll(x, 1, 0))
    after = jnp.where(row == n - 1, 0.0, pltpu.roll(x, n - 1, 0))
    o_ref[...] = w_ref[0:1, :] * before + w_ref[1:2, :] * x + w_ref[2:3, :] * after + b_ref[...]


def _shortconv(x, w, b, row0, n_seq, length):
    width = x.shape[1]
    tc = 256 if length >= 1024 else 2048
    s0 = row0 // length
    return pl.pallas_call(
        _shortconv_kernel,
        grid=(n_seq, width // tc),
        in_specs=[pl.BlockSpec((length, tc), lambda s, j: (s0 + s, j)),
                  pl.BlockSpec((HY_CONV, tc), lambda s, j: (0, j)),
                  pl.BlockSpec((1, tc), lambda s, j: (0, j))],
        out_specs=pl.BlockSpec((length, tc), lambda s, j: (s, j)),
        out_shape=jax.ShapeDtypeStruct((n_seq * length, width), F32),
        compiler_params=_params(("parallel", "parallel"),
                                _vmem_limit(2 * _nbytes((length, tc), F32),
                                            temps=4 * _nbytes((length, tc), F32))),
        name="shortconv",
    )(x, w, b.reshape(1, width))


def _filter_mlp_kernel(feat_ref, t_ref, w1_ref, b1_ref, fr1_ref, w2_ref, b2_ref, fr2_ref,
                       w3_ref, decay_ref, h_ref, ss_ref):
    x = jnp.sin(fr1_ref[...] * (_dot(feat_ref[...].astype(BF16), w1_ref[...].astype(BF16))
                                + b1_ref[...]))
    x = jnp.sin(fr2_ref[...] * (_dot(x.astype(BF16), w2_ref[...].astype(BF16)) + b2_ref[...]))
    h = _dot(x.astype(BF16), w3_ref[...].astype(BF16))
    h = h * (jnp.exp(-t_ref[...] * jnp.exp(decay_ref[...])) + HY_SHIFT)
    h_ref[...] = h

    @pl.when(pl.program_id(0) == 0)
    def _():
        ss_ref[...] = jnp.zeros_like(ss_ref)

    ss_ref[...] += jnp.sum(h * h, axis=0, keepdims=True)


def _filter_combine_kernel(hf_ref, hb_ref, ssf_ref, ssb_ref, a_ref, b_ref):
    norm = lax.rsqrt(ssf_ref[...] + ssb_ref[...] + 1e-12)
    fwd = hf_ref[...] * norm
    bwd = hb_ref[...] * norm
    row = lax.broadcasted_iota(jnp.int32, bwd.shape, 0) + pl.program_id(0) * bwd.shape[0]
    bwd = jnp.where(row == 0, 0.0, bwd)
    a_ref[...] = (fwd + bwd).astype(BF16)
    b_ref[...] = (fwd - bwd).astype(BF16)


def _hyena_filter_spectra(tabs, length, f_w1, f_b1, f_freq1, f_w2, f_b2, f_freq2, f_w3, log_decay):
    t = jnp.linspace(0.0, 1.0, length, dtype=F32)[:, None]
    t_idx = jnp.arange(length, dtype=F32)[:, None]
    bands = jnp.linspace(1e-4, HY_BANDS - 1, HY_BANDS, dtype=F32)
    w = 2.0 * math.pi * t_idx * bands / length
    feat = jnp.concatenate([t, jnp.cos(w), -jnp.sin(w)], axis=-1)
    emb_pad = 128
    feat = jnp.pad(feat, ((0, 0), (0, emb_pad - HY_EMB)))
    w1 = jnp.pad(f_w1, ((0, emb_pad - HY_EMB), (0, 0)))
    n_all = HY_DIRS * HY_ORDER * D_MODEL
    n_dir = HY_ORDER * D_MODEL
    tm = 256
    full = lambda shape: pl.BlockSpec(shape, lambda i: (0, 0))
    h, ss = pl.pallas_call(
        _filter_mlp_kernel,
        grid=(length // tm,),
        in_specs=[pl.BlockSpec((tm, emb_pad), lambda i: (i, 0)),
                  pl.BlockSpec((tm, 1), lambda i: (i, 0)),
                  full((emb_pad, HY_FW)), full((1, HY_FW)), full((1, HY_FW)),
                  full((HY_FW, HY_FW)), full((1, HY_FW)), full((1, HY_FW)),
                  full((HY_FW, n_all)), full((1, n_all))],
        out_specs=[pl.BlockSpec((tm, n_all), lambda i: (i, 0)), full((1, n_all))],
        out_shape=[jax.ShapeDtypeStruct((length, n_all), F32),
                   jax.ShapeDtypeStruct((1, n_all), F32)],
        compiler_params=_params(("arbitrary",),
                                _vmem_limit(_nbytes((tm, n_all), F32) + _nbytes((HY_FW, n_all), F32),
                                            temps=3 * _nbytes((tm, n_all), F32))),
        name="filter_mlp",
    )(feat, t, w1, f_b1.reshape(1, HY_FW), f_freq1.reshape(1, HY_FW), f_w2,
      f_b2.reshape(1, HY_FW), f_freq2.reshape(1, HY_FW), f_w3,
      log_decay.reshape(1, n_all))

    tn = 1024
    nb = n_dir // tn
    comb = jax.ShapeDtypeStruct((length, n_dir), BF16)
    a, b = pl.pallas_call(
        _filter_combine_kernel,
        grid=(length // tm, nb),
        in_specs=[pl.BlockSpec((tm, tn), lambda i, j: (i, j)),
                  pl.BlockSpec((tm, tn), lambda i, j: (i, nb + j)),
                  pl.BlockSpec((1, tn), lambda i, j: (0, j)),
                  pl.BlockSpec((1, tn), lambda i, j: (0, nb + j))],
        out_specs=[pl.BlockSpec((tm, tn), lambda i, j: (i, j)),
                   pl.BlockSpec((tm, tn), lambda i, j: (i, j))],
        out_shape=[comb, comb],
        compiler_params=_params(("parallel", "parallel"), VMEM_FLOOR_BYTES),
        name="filter_combine",
    )(h, h, ss, ss)
    return _mm(tabs["c"], a), _mm(tabs["s"], b)


def _odd_dft_tables(length):
    idx = jnp.arange(length, dtype=jnp.int32)
    q = ((2 * idx[:, None] + 1) * idx[None, :]) % (4 * length)
    ang = q.astype(F32) * (2.0 * math.pi / (4 * length))
    qt = (idx[:, None] * (2 * idx[None, :] + 1)) % (4 * length)
    angt = qt.astype(F32) * (2.0 * math.pi / (4 * length))
    return {"c": jnp.cos(ang).astype(BF16), "s": jnp.sin(ang).astype(BF16),
            "ct": jnp.cos(angt).astype(BF16), "st": jnp.sin(angt).astype(BF16)}


def _dft_tables(length):
    idx = jnp.arange(length, dtype=jnp.int32)
    q = (idx[:, None] * idx[None, :]) % length
    ang = q.astype(F32) * (2.0 * math.pi / length)
    return jnp.cos(ang).astype(BF16), (-jnp.sin(ang)).astype(BF16)


def _hyena_mix(proj, conv_w, conv_b, kr, ks, skip, tabs, row0, n_seq, length):
    pc = _shortconv(proj, conv_w, conv_b, row0, n_seq, length)
    v0, x1, x2 = 0, D_MODEL, 2 * D_MODEL
    yr, ys = _dft_fwd(tabs, pc, v0, kr, ks, 0, n_seq, length)
    z1 = _dft_inv(tabs, yr, ys, pc, v0, pc, x1, skip, 0, n_seq, length, F32)
    yr, ys = _dft_fwd(tabs, z1, 0, kr, ks, 1, n_seq, length)
    return _dft_inv(tabs, yr, ys, z1, 0, pc, x2, skip, 1, n_seq, length, BF16)


def _fnet_chan_kernel(h_ref, sh_ref, sc_ref, w_ref, p_ref, q_ref):
    u = _modulate(h_ref[...], sh_ref[...], sc_ref[...]).astype(BF16)
    r = _dot(u, w_ref[...])
    p_ref[...] = r[:, :FNET_CG].astype(BF16)
    q_ref[...] = r[:, FNET_CG:].astype(BF16)


def _fnet_chan(h, mods, w_cs, tm=512):
    mod = lambda which: pl.BlockSpec(
        (None, 1, FNET_CG), lambda i, g: (which * COND_PAD + _group_of_tile(i, tm), 0, g))
    out = jax.ShapeDtypeStruct((ROWS, D_MODEL), BF16)
    return pl.pallas_call(
        _fnet_chan_kernel,
        grid=(ROWS // tm, FNET_GROUPS),
        in_specs=[pl.BlockSpec((tm, FNET_CG), lambda i, g: (i, g)), mod(0), mod(1),
                  pl.BlockSpec((FNET_CG, 2 * FNET_CG), lambda i, g: (0, 0))],
        out_specs=[pl.BlockSpec((tm, FNET_CG), lambda i, g: (i, g)),
                   pl.BlockSpec((tm, FNET_CG), lambda i, g: (i, g))],
        out_shape=[out, out],
        compiler_params=_params(("parallel", "parallel"), VMEM_FLOOR_BYTES),
        name="fnet_chan",
    )(h, mods, mods, w_cs)


def _fnet_pos_kernel(scale, c_ref, ns_ref, p_ref, q_ref, prev_ref, o_ref, acc_ref):
    del prev_ref
    k = pl.program_id(3)

    @pl.when(k == 0)
    def _():
        acc_ref[...] = jnp.zeros_like(acc_ref)

    acc_ref[...] += _dot(c_ref[...], p_ref[...]) + _dot(ns_ref[...], q_ref[...])

    @pl.when(k == pl.num_programs(3) - 1)
    def _():
        o_ref[...] = (acc_ref[...] * scale).astype(o_ref.dtype)


def _fnet_pos(c_tab, ns_tab, p, q, prev, row0, n_seq, length):
    tt, tn, tk = _seq_tiles(length)
    nb, nt, nk = D_MODEL // tn, length // tt, length // tk
    t0, k0 = row0 // tt, row0 // tk
    scale = (length * FNET_CG) ** -0.5
    return pl.pallas_call(
        functools.partial(_fnet_pos_kernel, scale),
        grid=(n_seq, nt, nb, nk),
        in_specs=[pl.BlockSpec((tt, tk), lambda s, t, n, k: (t, k)),
                  pl.BlockSpec((tt, tk), lambda s, t, n, k: (t, k)),
                  pl.BlockSpec((tk, tn), lambda s, t, n, k: (k0 + s * nk + k, n)),
                  pl.BlockSpec((tk, tn), lambda s, t, n, k: (k0 + s * nk + k, n)),
                  pl.BlockSpec(memory_space=pl.ANY)],
        out_specs=pl.BlockSpec((tt, tn), lambda s, t, n, k: (t0 + s * nt + t, n)),
        out_shape=jax.ShapeDtypeStruct((ROWS, D_MODEL), BF16),
        scratch_shapes=[pltpu.VMEM((tt, tn), F32)],
        input_output_aliases={4: 0},
        compiler_params=_params(("parallel", "parallel", "parallel", "arbitrary"),
                                _vmem_limit(2 * _nbytes((tt, tk), BF16) + 2 * _nbytes((tk, tn), BF16)
                                            + _nbytes((tt, tn), BF16),
                                            resident=_nbytes((tt, tn), F32),
                                            temps=2 * _nbytes((tt, tn), F32))),
        name="fnet_pos",
    )(c_tab, ns_tab, p, q, prev)


def _rope_table():
    rows = DEC_SEQ // GRID_W
    row = jnp.repeat(jnp.arange(rows), GRID_W).astype(F32)
    col = jnp.tile(jnp.arange(GRID_W), rows).astype(F32)
    inv = ROPE_THETA ** (-jnp.arange(0, AXIS_ROPE, 2, dtype=F32) / AXIS_ROPE)
    ang = jnp.concatenate([row[:, None] * inv, col[:, None] * inv], axis=-1)
    cos = jnp.repeat(jnp.cos(ang), 2, axis=-1)
    sin = jnp.repeat(jnp.sin(ang), 2, axis=-1)
    lat = jnp.tile(jnp.concatenate([cos, sin], axis=-1), (DEC_BATCH, 1))
    ctx = jnp.concatenate([jnp.ones((ROWS_CTX, QK_ROPE), F32), jnp.zeros((ROWS_CTX, QK_ROPE), F32)],
                          axis=-1)
    return jnp.concatenate([ctx, lat], axis=0)


def _pair_rotated(w):
    pairs = w.reshape(w.shape[:-1] + (QK_ROPE // 2, 2))
    return jnp.stack([-pairs[..., 1], pairs[..., 0]], axis=-1).reshape(w.shape)


def kernel(x_prompt, x_sample, c, cache_ckv, cache_krope, c_ctx, ada_w, ada_b, ln_g, ln_b, ffn_w_gate, ffn_w_up, ffn_w_down, mla_w_dq, mla_q_norm, mla_w_uq, mla_w_dkv, mla_kv_norm, mla_w_kr, mla_w_ukv, mla_w_o, hy_w_in, hy_b_in, hy_conv_w, hy_conv_b, hy_f_w1, hy_f_b1, hy_f_freq1, hy_f_w2, hy_f_b2, hy_f_freq2, hy_f_w3, hy_log_decay, hy_skip, hy_w_out, hy_b_out, fn_w_out, fn_b_out):
    assert x_prompt.shape == (BATCH, SEQ, D_MODEL) and x_sample.shape == (DEC_BATCH, DEC_SEQ, D_MODEL)
    assert ROWS_CTX % DEC_SEQ == 0 and SEQ == FNET_CG

    h = jnp.concatenate([x_prompt.reshape(ROWS_CTX, D_MODEL), x_sample.reshape(ROWS_LAT, D_MODEL)])
    cond = jnp.concatenate([c_ctx[None, :], c, jnp.zeros((COND_PAD - N_COND, D_MODEL), F32)])
    mods_all = _modulation_vectors(cond, ada_w, ada_b)
    zero_bias = jnp.zeros((D_MODEL,), F32)
    rope_tab = None
    ckv_states, krope_states = [], []

    for i in range(DEPTH):
        kind, j = i % N_MIXERS, i // N_MIXERS
        mods = mods_all[i]
        if kind == 0:
            if rope_tab is None:
                rope_tab = _rope_table()
            w_kr2 = jnp.concatenate([mla_w_kr[j], _pair_rotated(mla_w_kr[j])], axis=-1).astype(BF16)
            wq = mla_w_uq[j].reshape(Q_RANK, MLA_HEADS, QK_NOPE + QK_ROPE)
            w_q = jnp.concatenate([wq, _pair_rotated(wq[..., QK_NOPE:])], axis=-1)
            w_q = w_q.reshape(Q_RANK, MLA_HEADS * HEAD_W).astype(BF16)
            w_ukv = mla_w_ukv[j].astype(BF16)
            cq, ckv, kr, kr2 = _mla_down(h, mods, mla_w_dq[j].astype(BF16), mla_w_dkv[j].astype(BF16),
                                         w_kr2, mla_q_norm[j], mla_kv_norm[j], rope_tab)
            ckv_states.append(ckv[:ROWS_CTX].reshape(BATCH, SEQ, KV_RANK))
            krope_states.append(kr[:ROWS_CTX].reshape(BATCH, SEQ, QK_ROPE))
            q = _q_up(cq, w_q, rope_tab)
            k_tok, v_tok = _kv_expand(ckv, kr2, w_ukv)
            kc = cache_krope[:, j].reshape(DEC_BATCH * PAST_LEN, QK_ROPE).astype(BF16)
            k_cache, v_cache = _kv_expand(cache_ckv[:, j].reshape(DEC_BATCH * PAST_LEN, KV_RANK),
                                          jnp.concatenate([kc, kc], axis=-1), w_ukv)
            o = _attention(q, k_tok, v_tok, k_cache, v_cache)
            h = _mm_postnorm(o, mla_w_o[j].astype(BF16), zero_bias, h, mods, 2, ln_g[i, 0], ln_b[i, 0])
        elif kind == 1:
            proj = _mod_mm(h, mods, hy_w_in[j].astype(BF16), hy_b_in[j], 0)
            fp = (hy_f_w1[j], hy_f_b1[j], hy_f_freq1[j], hy_f_w2[j], hy_f_b2[j], hy_f_freq2[j],
                  hy_f_w3[j], hy_log_decay[j])
            skip = hy_skip[j].reshape(HY_ORDER, 1, D_MODEL)
            outs = []
            for row0, n_seq, length in ((0, BATCH, SEQ), (ROWS_CTX, DEC_BATCH, DEC_SEQ)):
                tabs = _odd_dft_tables(length)
                kr_f, ks_f = _hyena_filter_spectra(tabs, length, *fp)
                outs.append(_hyena_mix(proj, hy_conv_w[j], hy_conv_b[j], kr_f, ks_f, skip, tabs,
                                       row0, n_seq, length))
            h = _mm_postnorm(jnp.concatenate(outs), hy_w_out[j].astype(BF16), hy_b_out[j], h, mods, 2,
                             ln_g[i, 0], ln_b[i, 0])
        else:
            c_ch, ns_ch = _dft_tables(FNET_CG)
            p, q = _fnet_chan(h, mods, jnp.concatenate([c_ch, -ns_ch], axis=-1))
            f = _fnet_pos(c_ch, ns_ch, p, q, jnp.zeros((ROWS, D_MODEL), BF16), 0, BATCH, SEQ)
            c_pos, ns_pos = _dft_tables(DEC_SEQ)
            f = _fnet_pos(c_pos, ns_pos, p, q, f, ROWS_CTX, DEC_BATCH, DEC_SEQ)
            h = _mm_postnorm(f, fn_w_out[j].astype(BF16), fn_b_out[j], h, mods, 2,
                             ln_g[i, 0], ln_b[i, 0])
        h = _ffn(h, mods, ffn_w_gate[i].astype(BF16), ffn_w_up[i].astype(BF16),
                 ffn_w_down[i].astype(BF16), ln_g[i, 1], ln_b[i, 1])

    y_prompt = h[:ROWS_CTX].reshape(BATCH, SEQ, D_MODEL)
    y_sample = h[ROWS_CTX:].reshape(DEC_BATCH, DEC_SEQ, D_MODEL)
    return (y_prompt, y_sample, jnp.stack(ckv_states, axis=1), jnp.stack(krope_states, axis=1))
```

```python
import functools
import math

import jax
import jax.numpy as jnp
from jax import lax
from jax.experimental import pallas as pl
from jax.experimental.pallas import tpu as pltpu

F32 = jnp.float32
BF16 = jnp.bfloat16

D_MODEL = 2048
BATCH = 16
SEQ = 256
DEPTH = 4
DEC_BATCH = 2
DEC_SEQ = 4096
PAST_LEN = 512
GRID_W = 64
N_MIXERS = 3
MLA_HEADS = 16
QK_NOPE = 128
QK_ROPE = 64
V_DIM = 128
Q_RANK = 512
KV_RANK = 512
ROPE_THETA = 10000.0
AXIS_ROPE = QK_ROPE // 2
HY_ORDER = 2
HY_DIRS = 2
HY_CONV = 3
HY_BANDS = 16
HY_EMB = 1 + 2 * HY_BANDS
HY_FW = 64
HY_SHIFT = 0.05
FNET_GROUPS = 8
FNET_CG = D_MODEL // FNET_GROUPS
D_FF = -(-8 * D_MODEL // (3 * 256)) * 256
DN_ALPHA = (2 * DEPTH) ** 0.25
LN_EPS = 1e-5
RMS_EPS = 1e-6
N_MOD = 6

ROWS_CTX = BATCH * SEQ
ROWS_LAT = DEC_BATCH * DEC_SEQ
ROWS = ROWS_CTX + ROWS_LAT
N_COND = 1 + DEC_BATCH
COND_PAD = 8
HEAD_W = QK_NOPE + 2 * QK_ROPE
ATT_SCALE = (QK_NOPE + QK_ROPE) ** -0.5

V7X_VMEM_BYTES = 64 * 2 ** 20
VMEM_CAP_BYTES = V7X_VMEM_BYTES * 7 // 8
VMEM_FLOOR_BYTES = 32 * 2 ** 20


def _vmem_limit(pipelined, resident=0, temps=0):
    est = 2 * pipelined + resident + temps
    return int(min(max(est, VMEM_FLOOR_BYTES), VMEM_CAP_BYTES))


def _params(semantics, vmem):
    return pltpu.CompilerParams(dimension_semantics=semantics, vmem_limit_bytes=vmem)


def _nbytes(shape, dtype):
    return math.prod(shape) * jnp.dtype(dtype).itemsize


def _group_of_tile(i, tm):
    n_ctx = ROWS_CTX // tm
    return jnp.where(i < n_ctx, 0, 1 + (i - n_ctx) // (DEC_SEQ // tm))


def _mod_spec(which, tm):
    return pl.BlockSpec((None, 1, D_MODEL),
                        lambda i, *_: (which * COND_PAD + _group_of_tile(i, tm), 0, 0))


def _row_spec(width=D_MODEL):
    return pl.BlockSpec((1, width), lambda *_: (0, 0))


def _modulate(h, shift, scale):
    return h * (1.0 + scale) + shift


def _post_norm(h, delta, g, b):
    z = DN_ALPHA * h + delta
    mu = jnp.mean(z, axis=-1, keepdims=True)
    zc = z - mu
    var = jnp.mean(zc * zc, axis=-1, keepdims=True)
    return zc * lax.rsqrt(var + LN_EPS) * g + b


def _rms_norm(x, g):
    ms = jnp.mean(x * x, axis=-1, keepdims=True)
    return x * lax.rsqrt(ms + RMS_EPS) * g


def _dot(a, b):
    return jnp.dot(a, b, preferred_element_type=F32)


def _modvec_kernel(c_ref, w_ref, b_ref, o_ref):
    a = jax.nn.silu(c_ref[...]).astype(BF16)
    o_ref[...] = _dot(a, w_ref[...].astype(BF16)) + b_ref[...]


def _modulation_vectors(cond, ada_w, ada_b):
    tn = 1024
    n = N_MOD * D_MODEL
    out = pl.pallas_call(
        _modvec_kernel,
        grid=(DEPTH, n // tn),
        in_specs=[pl.BlockSpec((COND_PAD, D_MODEL), lambda l, j: (0, 0)),
                  pl.BlockSpec((None, D_MODEL, tn), lambda l, j: (l, 0, j)),
                  pl.BlockSpec((None, 1, tn), lambda l, j: (l, 0, j))],
        out_specs=pl.BlockSpec((None, COND_PAD, tn), lambda l, j: (l, 0, j)),
        out_shape=jax.ShapeDtypeStruct((DEPTH, COND_PAD, n), F32),
        compiler_params=_params(("parallel", "parallel"),
                                _vmem_limit(_nbytes((D_MODEL, tn), F32),
                                            temps=_nbytes((D_MODEL, tn), BF16))),
        name="modvec",
    )(cond, ada_w, ada_b.reshape(DEPTH, 1, n))
    out = out.reshape(DEPTH, COND_PAD, N_MOD, D_MODEL).transpose(0, 2, 1, 3)
    return out.reshape(DEPTH, N_MOD * COND_PAD, 1, D_MODEL)


def _mod_mm_kernel(h_ref, sh_ref, sc_ref, w_ref, b_ref, o_ref, u_ref):
    @pl.when(pl.program_id(1) == 0)
    def _():
        u_ref[...] = _modulate(h_ref[...], sh_ref[...], sc_ref[...]).astype(BF16)

    o_ref[...] = (_dot(u_ref[...], w_ref[...]) + b_ref[...]).astype(o_ref.dtype)


def _mod_mm(h, mods, w, b, which_shift, out_dtype=F32, tm=512, tn=512):
    k, n = w.shape
    return pl.pallas_call(
        _mod_mm_kernel,
        grid=(ROWS // tm, n // tn),
        in_specs=[pl.BlockSpec((tm, k), lambda i, j: (i, 0)),
                  _mod_spec(which_shift, tm), _mod_spec(which_shift + 1, tm),
                  pl.BlockSpec((k, tn), lambda i, j: (0, j)),
                  pl.BlockSpec((1, tn), lambda i, j: (0, j))],
        out_specs=pl.BlockSpec((tm, tn), lambda i, j: (i, j)),
        out_shape=jax.ShapeDtypeStruct((ROWS, n), out_dtype),
        scratch_shapes=[pltpu.VMEM((tm, k), BF16)],
        compiler_params=_params(("parallel", "arbitrary"),
                                _vmem_limit(_nbytes((tm, k), F32) + _nbytes((k, tn), BF16)
                                            + _nbytes((tm, tn), F32),
                                            resident=_nbytes((tm, k), BF16),
                                            temps=_nbytes((tm, k), F32))),
        name="mod_mm",
    )(h, mods, mods, w, b.reshape(1, n))


def _mm_postnorm_kernel(a_ref, w_ref, bias_ref, h_ref, gate_ref, g_ref, b_ref, o_ref):
    y = _dot(a_ref[...], w_ref[...]) + bias_ref[...]
    o_ref[...] = _post_norm(h_ref[...], gate_ref[...] * y, g_ref[...], b_ref[...])


def _mm_postnorm(a, w, bias, h, mods, which_gate, ln_g, ln_b, tm=256):
    k = a.shape[1]
    return pl.pallas_call(
        _mm_postnorm_kernel,
        grid=(ROWS // tm,),
        in_specs=[pl.BlockSpec((tm, k), lambda i: (i, 0)),
                  pl.BlockSpec((k, D_MODEL), lambda i: (0, 0)),
                  _row_spec(),
                  pl.BlockSpec((tm, D_MODEL), lambda i: (i, 0)),
                  _mod_spec(which_gate, tm), _row_spec(), _row_spec()],
        out_specs=pl.BlockSpec((tm, D_MODEL), lambda i: (i, 0)),
        out_shape=jax.ShapeDtypeStruct((ROWS, D_MODEL), F32),
        compiler_params=_params(("parallel",),
                                _vmem_limit(_nbytes((tm, k), BF16) + _nbytes((k, D_MODEL), BF16)
                                            + 2 * _nbytes((tm, D_MODEL), F32),
                                            temps=3 * _nbytes((tm, D_MODEL), F32))),
        name="mm_postnorm",
    )(a, w, bias.reshape(1, D_MODEL), h, mods, ln_g.reshape(1, D_MODEL), ln_b.reshape(1, D_MODEL))


def _ffn_kernel(h_ref, sh_ref, sc_ref, gate_ref, g_ref, b_ref, wg_ref, wu_ref, wd_ref,
                o_ref, u_ref):
    f = pl.program_id(1)

    @pl.when(f == 0)
    def _():
        u_ref[...] = _modulate(h_ref[...], sh_ref[...], sc_ref[...]).astype(BF16)
        o_ref[...] = jnp.zeros_like(o_ref)

    u = u_ref[...]
    act = (jax.nn.silu(_dot(u, wg_ref[...])) * _dot(u, wu_ref[...])).astype(BF16)
    o_ref[...] += _dot(act, wd_ref[...])

    @pl.when(f == pl.num_programs(1) - 1)
    def _():
        o_ref[...] = _post_norm(h_ref[...], gate_ref[...] * o_ref[...], g_ref[...], b_ref[...])


def _ffn(h, mods, w_gate, w_up, w_down, ln_g, ln_b, tm=512, tf=512):
    return pl.pallas_call(
        _ffn_kernel,
        grid=(ROWS // tm, D_FF // tf),
        in_specs=[pl.BlockSpec((tm, D_MODEL), lambda i, f: (i, 0)),
                  _mod_spec(3, tm), _mod_spec(4, tm), _mod_spec(5, tm),
                  _row_spec(), _row_spec(),
                  pl.BlockSpec((D_MODEL, tf), lambda i, f: (0, f)),
                  pl.BlockSpec((D_MODEL, tf), lambda i, f: (0, f)),
                  pl.BlockSpec((tf, D_MODEL), lambda i, f: (f, 0))],
        out_specs=pl.BlockSpec((tm, D_MODEL), lambda i, f: (i, 0)),
        out_shape=jax.ShapeDtypeStruct((ROWS, D_MODEL), F32),
        scratch_shapes=[pltpu.VMEM((tm, D_MODEL), BF16)],
        compiler_params=_params(("parallel", "arbitrary"),
                                _vmem_limit(2 * _nbytes((tm, D_MODEL), F32)
                                            + 3 * _nbytes((D_MODEL, tf), BF16),
                                            resident=_nbytes((tm, D_MODEL), BF16),
                                            temps=3 * _nbytes((tm, D_MODEL), F32))),
        name="ffn",
    )(h, mods, mods, mods, ln_g.reshape(1, D_MODEL), ln_b.reshape(1, D_MODEL),
      w_gate, w_up, w_down)


def _mla_down_kernel(h_ref, sh_ref, sc_ref, wdq_ref, wdkv_ref, wkr_ref, qn_ref, kvn_ref,
                     rope_ref, cq_ref, ckv_ref, kr_ref, kr2_ref):
    u = _modulate(h_ref[...], sh_ref[...], sc_ref[...]).astype(BF16)
    cq_ref[...] = _rms_norm(_dot(u, wdq_ref[...]), qn_ref[...]).astype(BF16)
    ckv_ref[...] = _rms_norm(_dot(u, wdkv_ref[...]), kvn_ref[...])
    t = _dot(u, wkr_ref[...])
    kr_ref[...] = t[:, :QK_ROPE]
    v = t * rope_ref[...]
    kr2_ref[...] = (v + pltpu.roll(v, QK_ROPE, 1)).astype(BF16)


def _mla_down(h, mods, w_dq, w_dkv, w_kr2, q_norm, kv_norm, rope_tab, tm=512):
    row = lambda width: pl.BlockSpec((tm, width), lambda i: (i, 0))
    full = lambda shape: pl.BlockSpec(shape, lambda i: (0, 0))
    return pl.pallas_call(
        _mla_down_kernel,
        grid=(ROWS // tm,),
        in_specs=[row(D_MODEL), _mod_spec(0, tm), _mod_spec(1, tm),
                  full((D_MODEL, Q_RANK)), full((D_MODEL, KV_RANK)), full((D_MODEL, 2 * QK_ROPE)),
                  _row_spec(Q_RANK), _row_spec(KV_RANK), row(2 * QK_ROPE)],
        out_specs=[row(Q_RANK), row(KV_RANK), row(QK_ROPE), row(2 * QK_ROPE)],
        out_shape=[jax.ShapeDtypeStruct((ROWS, Q_RANK), BF16),
                   jax.ShapeDtypeStruct((ROWS, KV_RANK), F32),
                   jax.ShapeDtypeStruct((ROWS, QK_ROPE), F32),
                   jax.ShapeDtypeStruct((ROWS, 2 * QK_ROPE), BF16)],
        compiler_params=_params(("parallel",),
                                _vmem_limit(_nbytes((tm, D_MODEL), F32)
                                            + _nbytes((D_MODEL, Q_RANK + KV_RANK + 2 * QK_ROPE), BF16)
                                            + 3 * _nbytes((tm, KV_RANK), F32),
                                            temps=2 * _nbytes((tm, D_MODEL), F32))),
        name="mla_down",
    )(h, mods, mods, w_dq, w_dkv, w_kr2, q_norm.reshape(1, Q_RANK), kv_norm.reshape(1, KV_RANK),
      rope_tab)


NT_DIMS = (((1,), (1,)), ((), ()))


def _q_up_kernel(cq_ref, w_ref, rope_ref, q_ref):
    cq, tab = cq_ref[...], rope_ref[...]
    for h in range(MLA_HEADS):
        r = _dot(cq, w_ref[:, h * HEAD_W:(h + 1) * HEAD_W])
        q_ref[h, :, :QK_NOPE] = r[:, :QK_NOPE].astype(BF16)
        q_ref[h, :, QK_NOPE:] = (r[:, QK_NOPE:] * tab).astype(BF16)


def _q_up(cq, w_q, rope_tab, tm=512):
    return pl.pallas_call(
        _q_up_kernel,
        grid=(ROWS // tm,),
        in_specs=[pl.BlockSpec((tm, Q_RANK), lambda i: (i, 0)),
                  pl.BlockSpec((Q_RANK, MLA_HEADS * HEAD_W), lambda i: (0, 0)),
                  pl.BlockSpec((tm, 2 * QK_ROPE), lambda i: (i, 0))],
        out_specs=pl.BlockSpec((MLA_HEADS, tm, HEAD_W), lambda i: (0, i, 0)),
        out_shape=jax.ShapeDtypeStruct((MLA_HEADS, ROWS, HEAD_W), BF16),
        compiler_params=_params(("parallel",),
                                _vmem_limit(_nbytes((Q_RANK + tm, MLA_HEADS * HEAD_W), BF16))),
        name="q_up",
    )(cq, w_q, rope_tab)


def _kv_expand_kernel(ckv_ref, kr2_ref, wk_ref, wvt_ref, k_ref, vt_ref):
    c = ckv_ref[...].astype(BF16)
    vt_ref[...] = lax.dot_general(wvt_ref[...], c, NT_DIMS,
                                  preferred_element_type=F32).astype(BF16)
    kr2 = kr2_ref[...]
    pair_w = 2 * QK_NOPE
    for g in range(MLA_HEADS // 2):
        r = _dot(c, wk_ref[:, g * pair_w:(g + 1) * pair_w]).astype(BF16)
        for e in range(2):
            k_ref[2 * g + e, :, :QK_NOPE] = r[:, e * QK_NOPE:(e + 1) * QK_NOPE]
            k_ref[2 * g + e, :, QK_NOPE:] = kr2


def _kv_expand(ckv, kr2, w_k, w_vt, tm=512):
    rows = ckv.shape[0]
    return pl.pallas_call(
        _kv_expand_kernel,
        grid=(rows // tm,),
        in_specs=[pl.BlockSpec((tm, KV_RANK), lambda i: (i, 0)),
                  pl.BlockSpec((tm, 2 * QK_ROPE), lambda i: (i, 0)),
                  pl.BlockSpec((KV_RANK, MLA_HEADS * QK_NOPE), lambda i: (0, 0)),
                  pl.BlockSpec((MLA_HEADS * V_DIM, KV_RANK), lambda i: (0, 0))],
        out_specs=[pl.BlockSpec((MLA_HEADS, tm, HEAD_W), lambda i: (0, i, 0)),
                   pl.BlockSpec((MLA_HEADS * V_DIM, tm), lambda i: (0, i))],
        out_shape=[jax.ShapeDtypeStruct((MLA_HEADS, rows, HEAD_W), BF16),
                   jax.ShapeDtypeStruct((MLA_HEADS * V_DIM, rows), BF16)],
        compiler_params=_params(("parallel",),
                                _vmem_limit(_nbytes((tm, MLA_HEADS * (HEAD_W + V_DIM)), BF16)
                                            + 2 * _nbytes((KV_RANK, MLA_HEADS * V_DIM), BF16),
                                            temps=_nbytes((MLA_HEADS * V_DIM, tm), F32))),
        name="kv_expand",
    )(ckv, kr2, w_k, w_vt)


ATT_CHUNK = 512
LOG2E_SCALE = ATT_SCALE * math.log2(math.e)


def _attn_chunk(q, k, vt, carry):
    s = lax.dot_general(k, q, NT_DIMS, preferred_element_type=F32) * LOG2E_SCALE
    m_new = jnp.max(s, axis=0, keepdims=True)
    if carry is not None:
        m_old, l_old, acc_old = carry
        m_new = jnp.maximum(m_old, m_new)
    p = jnp.exp2(s - m_new)
    l = jnp.sum(p, axis=0, keepdims=True)
    acc = _dot(vt, p.astype(BF16))
    if carry is not None:
        alpha = jnp.exp2(m_old - m_new)
        l = alpha * l_old + l
        acc = alpha * acc_old + acc
    return m_new, l, acc


def _attn_ctx_kernel(q_ref, k_ref, vt_ref, o_ref):
    for h in range(MLA_HEADS):
        _, l, acc = _attn_chunk(q_ref[h], k_ref[h], vt_ref[h * V_DIM:(h + 1) * V_DIM, :], None)
        o_ref[:, h * V_DIM:(h + 1) * V_DIM] = (acc / l).T.astype(BF16)


def _attn_lat_kernel(q_ref, k_ref, vt_ref, kc_ref, vtc_ref, prev_ref, o_ref):
    del prev_ref
    q = q_ref[...]
    carry = None
    for c in range(DEC_SEQ // ATT_CHUNK):
        rows = slice(c * ATT_CHUNK, (c + 1) * ATT_CHUNK)
        carry = _attn_chunk(q, k_ref[rows, :], vt_ref[:, rows], carry)
    _, l, acc = _attn_chunk(q, kc_ref[...], vtc_ref[...], carry)
    o_ref[...] = (acc / l).T.astype(BF16)


def _attention(q, k_tok, vt_tok, k_cache, vt_cache, tq=512):
    assert PAST_LEN == ATT_CHUNK
    out_shape = jax.ShapeDtypeStruct((ROWS, MLA_HEADS * V_DIM), BF16)
    o = pl.pallas_call(
        _attn_ctx_kernel,
        grid=(BATCH,),
        in_specs=[pl.BlockSpec((MLA_HEADS, SEQ, HEAD_W), lambda s: (0, s, 0)),
                  pl.BlockSpec((MLA_HEADS, SEQ, HEAD_W), lambda s: (0, s, 0)),
                  pl.BlockSpec((MLA_HEADS * V_DIM, SEQ), lambda s: (0, s))],
        out_specs=pl.BlockSpec((SEQ, MLA_HEADS * V_DIM), lambda s: (s, 0)),
        out_shape=out_shape,
        compiler_params=_params(("parallel",), VMEM_FLOOR_BYTES),
        name="attn_ctx",
    )(q, k_tok, vt_tok)

    lat0 = ROWS_CTX // DEC_SEQ
    q0 = ROWS_CTX // tq
    nq = DEC_SEQ // tq
    return pl.pallas_call(
        _attn_lat_kernel,
        grid=(DEC_BATCH, MLA_HEADS, nq),
        in_specs=[pl.BlockSpec((None, tq, HEAD_W), lambda b, h, i: (h, q0 + b * nq + i, 0)),
                  pl.BlockSpec((None, DEC_SEQ, HEAD_W), lambda b, h, i: (h, lat0 + b, 0)),
                  pl.BlockSpec((V_DIM, DEC_SEQ), lambda b, h, i: (h, lat0 + b)),
                  pl.BlockSpec((None, PAST_LEN, HEAD_W), lambda b, h, i: (h, b, 0)),
                  pl.BlockSpec((V_DIM, PAST_LEN), lambda b, h, i: (h, b)),
                  pl.BlockSpec(memory_space=pl.ANY)],
        out_specs=pl.BlockSpec((tq, V_DIM), lambda b, h, i: (q0 + b * nq + i, h)),
        out_shape=out_shape,
        input_output_aliases={5: 0},
        compiler_params=_params(("parallel", "parallel", "arbitrary"),
                                _vmem_limit(_nbytes((DEC_SEQ + PAST_LEN, HEAD_W + V_DIM), BF16),
                                            temps=8 * _nbytes((ATT_CHUNK, tq), F32))),
        name="attn_lat",
    )(q, k_tok, vt_tok, k_cache, vt_cache, o)


def _mm_kernel(a_ref, b_ref, o_ref, acc_ref):
    k = pl.program_id(2)

    @pl.when(k == 0)
    def _():
        acc_ref[...] = jnp.zeros_like(acc_ref)

    acc_ref[...] += _dot(a_ref[...], b_ref[...])

    @pl.when(k == pl.num_programs(2) - 1)
    def _():
        o_ref[...] = acc_ref[...].astype(o_ref.dtype)


def _mm(a, b, out_dtype=F32, tm=1024, tn=1024, tk=512):
    m, kk = a.shape
    n = b.shape[1]
    tm, tn, tk = min(tm, m), min(tn, n), min(tk, kk)
    return pl.pallas_call(
        _mm_kernel,
        grid=(m // tm, n // tn, kk // tk),
        in_specs=[pl.BlockSpec((tm, tk), lambda i, j, k: (i, k)),
                  pl.BlockSpec((tk, tn), lambda i, j, k: (k, j))],
        out_specs=pl.BlockSpec((tm, tn), lambda i, j, k: (i, j)),
        out_shape=jax.ShapeDtypeStruct((m, n), out_dtype),
        scratch_shapes=[pltpu.VMEM((tm, tn), F32)],
        compiler_params=_params(("parallel", "parallel", "arbitrary"), VMEM_FLOOR_BYTES),
        name="mm",
    )(a, b)


def _seq_tiles(length):
    if length >= 1024:
        return 1024, 512, 512
    return length, D_MODEL, length


def _dft_fwd_kernel(c_ref, s_ref, z_ref, kr_ref, ks_ref, yr_ref, ys_ref, accr_ref, accs_ref):
    k = pl.program_id(3)

    @pl.when(k == 0)
    def _():
        accr_ref[...] = jnp.zeros_like(accr_ref)
        accs_ref[...] = jnp.zeros_like(accs_ref)

    z = z_ref[...].astype(BF16)
    accr_ref[...] += _dot(c_ref[...], z)
    accs_ref[...] += _dot(s_ref[...], z)

    @pl.when(k == pl.num_programs(3) - 1)
    def _():
        zr, zs, kr, ks = accr_ref[...], accs_ref[...], kr_ref[...], ks_ref[...]
        yr_ref[...] = (zr * kr - zs * ks).astype(BF16)
        ys_ref[...] = (zr * ks + zs * kr).astype(BF16)


def _dft_fwd(tabs, z, z_col0, kr, ks, order, n_seq, length):
    tf, tn, tk = _seq_tiles(length)
    nb, nf, nk = D_MODEL // tn, length // tf, length // tk
    zc0, kc0 = z_col0 // tn, order * nb
    out = jax.ShapeDtypeStruct((n_seq * length, D_MODEL), BF16)
    return pl.pallas_call(
        _dft_fwd_kernel,
        grid=(n_seq, nf, nb, nk),
        in_specs=[pl.BlockSpec((tf, tk), lambda s, f, n, k: (f, k)),
                  pl.BlockSpec((tf, tk), lambda s, f, n, k: (f, k)),
                  pl.BlockSpec((tk, tn), lambda s, f, n, k: (s * nk + k, zc0 + n)),
                  pl.BlockSpec((tf, tn), lambda s, f, n, k: (f, kc0 + n)),
                  pl.BlockSpec((tf, tn), lambda s, f, n, k: (f, kc0 + n))],
        out_specs=[pl.BlockSpec((tf, tn), lambda s, f, n, k: (s * nf + f, n)),
                   pl.BlockSpec((tf, tn), lambda s, f, n, k: (s * nf + f, n))],
        out_shape=[out, out],
        scratch_shapes=[pltpu.VMEM((tf, tn), F32), pltpu.VMEM((tf, tn), F32)],
        compiler_params=_params(("parallel", "parallel", "parallel", "arbitrary"),
                                _vmem_limit(2 * _nbytes((tf, tk), BF16) + _nbytes((tk, tn), F32)
                                            + 2 * _nbytes((tf, tn), F32) + 2 * _nbytes((tf, tn), BF16),
                                            resident=2 * _nbytes((tf, tn), F32),
                                            temps=4 * _nbytes((tf, tn), F32))),
        name="dft_fwd",
    )(tabs["c"], tabs["s"], z, kr, ks)


def _dft_inv_kernel(inv_len, ct_ref, st_ref, yr_ref, ys_ref, z_ref, gate_ref, skip_ref,
                    o_ref, acc_ref):
    k = pl.program_id(3)

    @pl.when(k == 0)
    def _():
        acc_ref[...] = jnp.zeros_like(acc_ref)

    acc_ref[...] += _dot(ct_ref[...], yr_ref[...]) + _dot(st_ref[...], ys_ref[...])

    @pl.when(k == pl.num_programs(3) - 1)
    def _():
        y = acc_ref[...] * inv_len + skip_ref[...] * z_ref[...]
        o_ref[...] = (gate_ref[...] * y).astype(o_ref.dtype)


def _dft_inv(tabs, yr, ys, z, z_col0, gate, gate_col0, skip, order, n_seq, length, out_dtype):
    tt, tn, tk = _seq_tiles(length)
    nb, nt, nk = D_MODEL // tn, length // tt, length // tk
    zc0, gc0 = z_col0 // tn, gate_col0 // tn
    return pl.pallas_call(
        functools.partial(_dft_inv_kernel, 1.0 / length),
        grid=(n_seq, nt, nb, nk),
        in_specs=[pl.BlockSpec((tt, tk), lambda s, t, n, k: (t, k)),
                  pl.BlockSpec((tt, tk), lambda s, t, n, k: (t, k)),
                  pl.BlockSpec((tk, tn), lambda s, t, n, k: (s * nk + k, n)),
                  pl.BlockSpec((tk, tn), lambda s, t, n, k: (s * nk + k, n)),
                  pl.BlockSpec((tt, tn), lambda s, t, n, k: (s * nt + t, zc0 + n)),
                  pl.BlockSpec((tt, tn), lambda s, t, n, k: (s * nt + t, gc0 + n)),
                  pl.BlockSpec((None, 1, tn), lambda s, t, n, k: (order, 0, n))],
        out_specs=pl.BlockSpec((tt, tn), lambda s, t, n, k: (s * nt + t, n)),
        out_shape=jax.ShapeDtypeStruct((n_seq * length, D_MODEL), out_dtype),
        scratch_shapes=[pltpu.VMEM((tt, tn), F32)],
        compiler_params=_params(("parallel", "parallel", "parallel", "arbitrary"),
                                _vmem_limit(2 * _nbytes((tt, tk), BF16) + 2 * _nbytes((tk, tn), BF16)
                                            + 3 * _nbytes((tt, tn), F32),
                                            resident=_nbytes((tt, tn), F32),
                                            temps=3 * _nbytes((tt, tn), F32))),
        name="dft_inv",
    )(tabs["ct"], tabs["st"], yr, ys, z, gate, skip)


def _shortconv_kernel(x_ref, w_ref, b_ref, o_ref):
    x = x_ref[...]
    n = x.shape[0]
    row = lax.broadcasted_iota(jnp.int32, x.shape, 0)
    before = jnp.where(row == 0, 0.0, pltpu.roll(x, 1, 0))
    after = jnp.where(row == n - 1, 0.0, pltpu.roll(x, n - 1, 0))
    o_ref[...] = w_ref[0:1, :] * before + w_ref[1:2, :] * x + w_ref[2:3, :] * after + b_ref[...]


def _shortconv(x, w, b, row0, n_seq, length):
    width = x.shape[1]
    tc = 256 if length >= 1024 else 2048
    s0 = row0 // length
    return pl.pallas_call(
        _shortconv_kernel,
        grid=(n_seq, width // tc),
        in_specs=[pl.BlockSpec((length, tc), lambda s, j: (s0 + s, j)),
                  pl.BlockSpec((HY_CONV, tc), lambda s, j: (0, j)),
                  pl.BlockSpec((1, tc), lambda s, j: (0, j))],
        out_specs=pl.BlockSpec((length, tc), lambda s, j: (s, j)),
        out_shape=jax.ShapeDtypeStruct((n_seq * length, width), F32),
        compiler_params=_params(("parallel", "parallel"),
                                _vmem_limit(2 * _nbytes((length, tc), F32),
                                            temps=4 * _nbytes((length, tc), F32))),
        name="shortconv",
    )(x, w, b.reshape(1, width))


def _filter_mlp_kernel(feat_ref, t_ref, w1_ref, b1_ref, fr1_ref, w2_ref, b2_ref, fr2_ref,
                       w3_ref, decay_ref, h_ref, ss_ref):
    x = jnp.sin(fr1_ref[...] * (_dot(feat_ref[...].astype(BF16), w1_ref[...].astype(BF16))
                                + b1_ref[...]))
    x = jnp.sin(fr2_ref[...] * (_dot(x.astype(BF16), w2_ref[...].astype(BF16)) + b2_ref[...]))
    h = _dot(x.astype(BF16), w3_ref[...].astype(BF16))
    h = h * (jnp.exp(-t_ref[...] * jnp.exp(decay_ref[...])) + HY_SHIFT)
    h_ref[...] = h

    @pl.when(pl.program_id(0) == 0)
    def _():
        ss_ref[...] = jnp.zeros_like(ss_ref)

    ss_ref[...] += jnp.sum(h * h, axis=0, keepdims=True)


def _filter_combine_kernel(hf_ref, hb_ref, ssf_ref, ssb_ref, a_ref, b_ref):
    norm = lax.rsqrt(ssf_ref[...] + ssb_ref[...] + 1e-12)
    fwd = hf_ref[...] * norm
    bwd = hb_ref[...] * norm
    row = lax.broadcasted_iota(jnp.int32, bwd.shape, 0) + pl.program_id(0) * bwd.shape[0]
    bwd = jnp.where(row == 0, 0.0, bwd)
    a_ref[...] = (fwd + bwd).astype(BF16)
    b_ref[...] = (fwd - bwd).astype(BF16)


def _hyena_filter_spectra(tabs, length, f_w1, f_b1, f_freq1, f_w2, f_b2, f_freq2, f_w3, log_decay):
    t = jnp.linspace(0.0, 1.0, length, dtype=F32)[:, None]
    t_idx = jnp.arange(length, dtype=F32)[:, None]
    bands = jnp.linspace(1e-4, HY_BANDS - 1, HY_BANDS, dtype=F32)
    w = 2.0 * math.pi * t_idx * bands / length
    feat = jnp.concatenate([t, jnp.cos(w), -jnp.sin(w)], axis=-1)
    emb_pad = 128
    feat = jnp.pad(feat, ((0, 0), (0, emb_pad - HY_EMB)))
    w1 = jnp.pad(f_w1, ((0, emb_pad - HY_EMB), (0, 0)))
    n_all = HY_DIRS * HY_ORDER * D_MODEL
    n_dir = HY_ORDER * D_MODEL
    tm = 256
    full = lambda shape: pl.BlockSpec(shape, lambda i: (0, 0))
    h, ss = pl.pallas_call(
        _filter_mlp_kernel,
        grid=(length // tm,),
        in_specs=[pl.BlockSpec((tm, emb_pad), lambda i: (i, 0)),
                  pl.BlockSpec((tm, 1), lambda i: (i, 0)),
                  full((emb_pad, HY_FW)), full((1, HY_FW)), full((1, HY_FW)),
                  full((HY_FW, HY_FW)), full((1, HY_FW)), full((1, HY_FW)),
                  full((HY_FW, n_all)), full((1, n_all))],
        out_specs=[pl.BlockSpec((tm, n_all), lambda i: (i, 0)), full((1, n_all))],
        out_shape=[jax.ShapeDtypeStruct((length, n_all), F32),
                   jax.ShapeDtypeStruct((1, n_all), F32)],
        compiler_params=_params(("arbitrary",),
                                _vmem_limit(_nbytes((tm, n_all), F32) + _nbytes((HY_FW, n_all), F32),
                                            temps=3 * _nbytes((tm, n_all), F32))),
        name="filter_mlp",
    )(feat, t, w1, f_b1.reshape(1, HY_FW), f_freq1.reshape(1, HY_FW), f_w2,
      f_b2.reshape(1, HY_FW), f_freq2.reshape(1, HY_FW), f_w3,
      log_decay.reshape(1, n_all))

    tn = 1024
    nb = n_dir // tn
    comb = jax.ShapeDtypeStruct((length, n_dir), BF16)
    a, b = pl.pallas_call(
        _filter_combine_kernel,
        grid=(length // tm, nb),
        in_specs=[pl.BlockSpec((tm, tn), lambda i, j: (i, j)),
                  pl.BlockSpec((tm, tn), lambda i, j: (i, nb + j)),
                  pl.BlockSpec((1, tn), lambda i, j: (0, j)),
                  pl.BlockSpec((1, tn), lambda i, j: (0, nb + j))],
        out_specs=[pl.BlockSpec((tm, tn), lambda i, j: (i, j)),
                   pl.BlockSpec((tm, tn), lambda i, j: (i, j))],
        out_shape=[comb, comb],
        compiler_params=_params(("parallel", "parallel"), VMEM_FLOOR_BYTES),
        name="filter_combine",
    )(h, h, ss, ss)
    return _mm(tabs["c"], a), _mm(tabs["s"], b)


def _cis_product(row_hi, row_lo, period):
    def cis(phase):
        ang = (phase % period).astype(F32) * (2.0 * math.pi / period)
        return jnp.cos(ang)[:, :, None], jnp.sin(ang)[:, :, None]
    (c1, s1), (c0, s0) = cis(row_hi), cis(row_lo)
    c0, s0 = jnp.swapaxes(c0, 1, 2), jnp.swapaxes(s0, 1, 2)
    rows = row_hi.shape[0]
    return ((c1 * c0 - s1 * s0).reshape(rows, -1), (s1 * c0 + c1 * s0).reshape(rows, -1))


def _odd_dft_tables(length):
    split = 1 << (length.bit_length() // 2)
    r = jnp.arange(length, dtype=jnp.int32)[:, None]
    hi = jnp.arange(length // split, dtype=jnp.int32)[None, :] * split
    lo = jnp.arange(split, dtype=jnp.int32)[None, :]
    c, s = _cis_product((2 * r + 1) * hi, (2 * r + 1) * lo, 4 * length)
    ct, st = _cis_product(r * (2 * hi), r * (2 * lo + 1), 4 * length)
    return {"c": c.astype(BF16), "s": s.astype(BF16), "ct": ct.astype(BF16), "st": st.astype(BF16)}


def _dft_tables(length):
    split = 1 << (length.bit_length() // 2)
    r = jnp.arange(length, dtype=jnp.int32)[:, None]
    hi = jnp.arange(length // split, dtype=jnp.int32)[None, :] * split
    lo = jnp.arange(split, dtype=jnp.int32)[None, :]
    c, s = _cis_product(r * hi, r * lo, length)
    return c.astype(BF16), (-s).astype(BF16)


def _hyena_mix(proj, conv_w, conv_b, kr, ks, skip, tabs, row0, n_seq, length):
    pc = _shortconv(proj, conv_w, conv_b, row0, n_seq, length)
    v0, x1, x2 = 0, D_MODEL, 2 * D_MODEL
    yr, ys = _dft_fwd(tabs, pc, v0, kr, ks, 0, n_seq, length)
    z1 = _dft_inv(tabs, yr, ys, pc, v0, pc, x1, skip, 0, n_seq, length, F32)
    yr, ys = _dft_fwd(tabs, z1, 0, kr, ks, 1, n_seq, length)
    return _dft_inv(tabs, yr, ys, z1, 0, pc, x2, skip, 1, n_seq, length, BF16)


def _fnet_chan_kernel(h_ref, sh_ref, sc_ref, w_ref, p_ref, q_ref):
    u = _modulate(h_ref[...], sh_ref[...], sc_ref[...]).astype(BF16)
    r = _dot(u, w_ref[...])
    p_ref[...] = r[:, :FNET_CG].astype(BF16)
    q_ref[...] = r[:, FNET_CG:].astype(BF16)


def _fnet_chan(h, mods, w_cs, tm=512):
    mod = lambda which: pl.BlockSpec(
        (None, 1, FNET_CG), lambda i, g: (which * COND_PAD + _group_of_tile(i, tm), 0, g))
    out = jax.ShapeDtypeStruct((ROWS, D_MODEL), BF16)
    return pl.pallas_call(
        _fnet_chan_kernel,
        grid=(ROWS // tm, FNET_GROUPS),
        in_specs=[pl.BlockSpec((tm, FNET_CG), lambda i, g: (i, g)), mod(0), mod(1),
                  pl.BlockSpec((FNET_CG, 2 * FNET_CG), lambda i, g: (0, 0))],
        out_specs=[pl.BlockSpec((tm, FNET_CG), lambda i, g: (i, g)),
                   pl.BlockSpec((tm, FNET_CG), lambda i, g: (i, g))],
        out_shape=[out, out],
        compiler_params=_params(("parallel", "parallel"), VMEM_FLOOR_BYTES),
        name="fnet_chan",
    )(h, mods, mods, w_cs)


def _fnet_pos_kernel(scale, c_ref, ns_ref, p_ref, q_ref, prev_ref, o_ref, acc_ref):
    del prev_ref
    k = pl.program_id(3)

    @pl.when(k == 0)
    def _():
        acc_ref[...] = jnp.zeros_like(acc_ref)

    acc_ref[...] += _dot(c_ref[...], p_ref[...]) + _dot(ns_ref[...], q_ref[...])

    @pl.when(k == pl.num_programs(3) - 1)
    def _():
        o_ref[...] = (acc_ref[...] * scale).astype(o_ref.dtype)


def _fnet_pos(c_tab, ns_tab, p, q, prev, row0, n_seq, length):
    tt, tn, tk = _seq_tiles(length)
    nb, nt, nk = D_MODEL // tn, length // tt, length // tk
    t0, k0 = row0 // tt, row0 // tk
    scale = (length * FNET_CG) ** -0.5
    return pl.pallas_call(
        functools.partial(_fnet_pos_kernel, scale),
        grid=(n_seq, nt, nb, nk),
        in_specs=[pl.BlockSpec((tt, tk), lambda s, t, n, k: (t, k)),
                  pl.BlockSpec((tt, tk), lambda s, t, n, k: (t, k)),
                  pl.BlockSpec((tk, tn), lambda s, t, n, k: (k0 + s * nk + k, n)),
                  pl.BlockSpec((tk, tn), lambda s, t, n, k: (k0 + s * nk + k, n)),
                  pl.BlockSpec(memory_space=pl.ANY)],
        out_specs=pl.BlockSpec((tt, tn), lambda s, t, n, k: (t0 + s * nt + t, n)),
        out_shape=jax.ShapeDtypeStruct((ROWS, D_MODEL), BF16),
        scratch_shapes=[pltpu.VMEM((tt, tn), F32)],
        input_output_aliases={4: 0},
        compiler_params=_params(("parallel", "parallel", "parallel", "arbitrary"),
                                _vmem_limit(2 * _nbytes((tt, tk), BF16) + 2 * _nbytes((tk, tn), BF16)
                                            + _nbytes((tt, tn), BF16),
                                            resident=_nbytes((tt, tn), F32),
                                            temps=2 * _nbytes((tt, tn), F32))),
        name="fnet_pos",
    )(c_tab, ns_tab, p, q, prev)


def _rope_table():
    rows = DEC_SEQ // GRID_W
    row = jnp.repeat(jnp.arange(rows), GRID_W).astype(F32)
    col = jnp.tile(jnp.arange(GRID_W), rows).astype(F32)
    inv = ROPE_THETA ** (-jnp.arange(0, AXIS_ROPE, 2, dtype=F32) / AXIS_ROPE)
    ang = jnp.concatenate([row[:, None] * inv, col[:, None] * inv], axis=-1)
    cos = jnp.repeat(jnp.cos(ang), 2, axis=-1)
    sin = jnp.repeat(jnp.sin(ang), 2, axis=-1)
    lat = jnp.tile(jnp.concatenate([cos, sin], axis=-1), (DEC_BATCH, 1))
    ctx = jnp.concatenate([jnp.ones((ROWS_CTX, QK_ROPE), F32), jnp.zeros((ROWS_CTX, QK_ROPE), F32)],
                          axis=-1)
    return jnp.concatenate([ctx, lat], axis=0)


def _pair_rotated(w):
    pairs = w.reshape(w.shape[:-1] + (QK_ROPE // 2, 2))
    return jnp.stack([-pairs[..., 1], pairs[..., 0]], axis=-1).reshape(w.shape)


def kernel(x_prompt, x_sample, c, cache_ckv, cache_krope, c_ctx, ada_w, ada_b, ln_g, ln_b, ffn_w_gate, ffn_w_up, ffn_w_down, mla_w_dq, mla_q_norm, mla_w_uq, mla_w_dkv, mla_kv_norm, mla_w_kr, mla_w_ukv, mla_w_o, hy_w_in, hy_b_in, hy_conv_w, hy_conv_b, hy_f_w1, hy_f_b1, hy_f_freq1, hy_f_w2, hy_f_b2, hy_f_freq2, hy_f_w3, hy_log_decay, hy_skip, hy_w_out, hy_b_out, fn_w_out, fn_b_out):
    assert x_prompt.shape == (BATCH, SEQ, D_MODEL) and x_sample.shape == (DEC_BATCH, DEC_SEQ, D_MODEL)
    assert ROWS_CTX % DEC_SEQ == 0 and SEQ == FNET_CG

    h = jnp.concatenate([x_prompt.reshape(ROWS_CTX, D_MODEL), x_sample.reshape(ROWS_LAT, D_MODEL)])
    cond = jnp.concatenate([c_ctx[None, :], c, jnp.zeros((COND_PAD - N_COND, D_MODEL), F32)])
    mods_all = _modulation_vectors(cond, ada_w, ada_b)
    zero_bias = jnp.zeros((D_MODEL,), F32)
    rope_tab = None
    ckv_states, krope_states = [], []

    for i in range(DEPTH):
        kind, j = i % N_MIXERS, i // N_MIXERS
        mods = mods_all[i]
        if kind == 0:
            if rope_tab is None:
                rope_tab = _rope_table()
            w_kr2 = jnp.concatenate([mla_w_kr[j], _pair_rotated(mla_w_kr[j])], axis=-1).astype(BF16)
            wq = mla_w_uq[j].reshape(Q_RANK, MLA_HEADS, QK_NOPE + QK_ROPE)
            w_q = jnp.concatenate([wq, _pair_rotated(wq[..., QK_NOPE:])], axis=-1)
            w_q = w_q.reshape(Q_RANK, MLA_HEADS * HEAD_W).astype(BF16)
            w_ukv = mla_w_ukv[j].reshape(KV_RANK, MLA_HEADS, QK_NOPE + V_DIM)
            w_k = w_ukv[..., :QK_NOPE].reshape(KV_RANK, MLA_HEADS * QK_NOPE).astype(BF16)
            w_vt = w_ukv[..., QK_NOPE:].reshape(KV_RANK, MLA_HEADS * V_DIM).T.astype(BF16)
            cq, ckv, kr, kr2 = _mla_down(h, mods, mla_w_dq[j].astype(BF16), mla_w_dkv[j].astype(BF16),
                                         w_kr2, mla_q_norm[j], mla_kv_norm[j], rope_tab)
            ckv_states.append(ckv[:ROWS_CTX].reshape(BATCH, SEQ, KV_RANK))
            krope_states.append(kr[:ROWS_CTX].reshape(BATCH, SEQ, QK_ROPE))
            q = _q_up(cq, w_q, rope_tab)
            k_tok, vt_tok = _kv_expand(ckv, kr2, w_k, w_vt)
            kc = cache_krope[:, j].reshape(DEC_BATCH * PAST_LEN, QK_ROPE).astype(BF16)
            k_cache, vt_cache = _kv_expand(cache_ckv[:, j].reshape(DEC_BATCH * PAST_LEN, KV_RANK),
                                           jnp.concatenate([kc, kc], axis=-1), w_k, w_vt)
            o = _attention(q, k_tok, vt_tok, k_cache, vt_cache)
            h = _mm_postnorm(o, mla_w_o[j].astype(BF16), zero_bias, h, mods, 2, ln_g[i, 0], ln_b[i, 0])
        elif kind == 1:
            proj = _mod_mm(h, mods, hy_w_in[j].astype(BF16), hy_b_in[j], 0)
            fp = (hy_f_w1[j], hy_f_b1[j], hy_f_freq1[j], hy_f_w2[j], hy_f_b2[j], hy_f_freq2[j],
                  hy_f_w3[j], hy_log_decay[j])
            skip = hy_skip[j].reshape(HY_ORDER, 1, D_MODEL)
            outs = []
            for row0, n_seq, length in ((0, BATCH, SEQ), (ROWS_CTX, DEC_BATCH, DEC_SEQ)):
                tabs = _odd_dft_tables(length)
                kr_f, ks_f = _hyena_filter_spectra(tabs, length, *fp)
                outs.append(_hyena_mix(proj, hy_conv_w[j], hy_conv_b[j], kr_f, ks_f, skip, tabs,
                                       row0, n_seq, length))
            h = _mm_postnorm(jnp.concatenate(outs), hy_w_out[j].astype(BF16), hy_b_out[j], h, mods, 2,
                             ln_g[i, 0], ln_b[i, 0])
        else:
            c_ch, ns_ch = _dft_tables(FNET_CG)
            p, q = _fnet_chan(h, mods, jnp.concatenate([c_ch, -ns_ch], axis=-1))
            f = _fnet_pos(c_ch, ns_ch, p, q, jnp.zeros((ROWS, D_MODEL), BF16), 0, BATCH, SEQ)
            c_pos, ns_pos = _dft_tables(DEC_SEQ)
            f = _fnet_pos(c_pos, ns_pos, p, q, f, ROWS_CTX, DEC_BATCH, DEC_SEQ)
            h = _mm_postnorm(f, fn_w_out[j].astype(BF16), fn_b_out[j], h, mods, 2,
                             ln_g[i, 0], ln_b[i, 0])
        h = _ffn(h, mods, ffn_w_gate[i].astype(BF16), ffn_w_up[i].astype(BF16),
                 ffn_w_down[i].astype(BF16), ln_g[i, 1], ln_b[i, 1])

    y_prompt = h[:ROWS_CTX].reshape(BATCH, SEQ, D_MODEL)
    y_sample = h[ROWS_CTX:].reshape(DEC_BATCH, DEC_SEQ, D_MODEL)
    return (y_prompt, y_sample, jnp.stack(ckv_states, axis=1), jnp.stack(krope_states, axis=1))
```

```python
import functools
import math

import jax
import jax.numpy as jnp
from jax import lax
from jax.experimental import pallas as pl
from jax.experimental.pallas import tpu as pltpu

F32 = jnp.float32
BF16 = jnp.bfloat16

D_MODEL = 2048
BATCH = 16
SEQ = 256
DEPTH = 4
DEC_BATCH = 2
DEC_SEQ = 4096
PAST_LEN = 512
GRID_W = 64
N_MIXERS = 3
MLA_HEADS = 16
QK_NOPE = 128
QK_ROPE = 64
V_DIM = 128
Q_RANK = 512
KV_RANK = 512
ROPE_THETA = 10000.0
AXIS_ROPE = QK_ROPE // 2
HY_ORDER = 2
HY_DIRS = 2
HY_CONV = 3
HY_BANDS = 16
HY_EMB = 1 + 2 * HY_BANDS
HY_FW = 64
HY_SHIFT = 0.05
FNET_GROUPS = 8
FNET_CG = D_MODEL // FNET_GROUPS
D_FF = -(-8 * D_MODEL // (3 * 256)) * 256
DN_ALPHA = (2 * DEPTH) ** 0.25
LN_EPS = 1e-5
RMS_EPS = 1e-6
N_MOD = 6

ROWS_CTX = BATCH * SEQ
ROWS_LAT = DEC_BATCH * DEC_SEQ
ROWS = ROWS_CTX + ROWS_LAT
N_COND = 1 + DEC_BATCH
COND_PAD = 8
HEAD_W = QK_NOPE + 2 * QK_ROPE
ATT_SCALE = (QK_NOPE + QK_ROPE) ** -0.5

V7X_VMEM_BYTES = 64 * 2 ** 20
VMEM_CAP_BYTES = V7X_VMEM_BYTES * 7 // 8
VMEM_FLOOR_BYTES = 32 * 2 ** 20


def _vmem_limit(pipelined, resident=0, temps=0):
    est = 2 * pipelined + resident + temps
    return int(min(max(est, VMEM_FLOOR_BYTES), VMEM_CAP_BYTES))


def _params(semantics, vmem):
    return pltpu.CompilerParams(dimension_semantics=semantics, vmem_limit_bytes=vmem)


def _nbytes(shape, dtype):
    return math.prod(shape) * jnp.dtype(dtype).itemsize


def _group_of_tile(i, tm):
    n_ctx = ROWS_CTX // tm
    return jnp.where(i < n_ctx, 0, 1 + (i - n_ctx) // (DEC_SEQ // tm))


def _mod_spec(which, tm):
    return pl.BlockSpec((None, 1, D_MODEL),
                        lambda i, *_: (which * COND_PAD + _group_of_tile(i, tm), 0, 0))


def _row_spec(width=D_MODEL):
    return pl.BlockSpec((1, width), lambda *_: (0, 0))


def _modulate(h, shift, scale):
    return h * (1.0 + scale) + shift


def _post_norm(h, delta, g, b):
    z = DN_ALPHA * h + delta
    mu = jnp.mean(z, axis=-1, keepdims=True)
    zc = z - mu
    var = jnp.mean(zc * zc, axis=-1, keepdims=True)
    return zc * lax.rsqrt(var + LN_EPS) * g + b


def _rms_norm(x, g):
    ms = jnp.mean(x * x, axis=-1, keepdims=True)
    return x * lax.rsqrt(ms + RMS_EPS) * g


def _dot(a, b):
    return jnp.dot(a, b, preferred_element_type=F32)


def _modvec_kernel(c_ref, w_ref, b_ref, o_ref):
    a = jax.nn.silu(c_ref[...]).astype(BF16)
    o_ref[...] = _dot(a, w_ref[...].astype(BF16)) + b_ref[...]


def _modulation_vectors(cond, ada_w, ada_b):
    tn = 1024
    n = N_MOD * D_MODEL
    out = pl.pallas_call(
        _modvec_kernel,
        grid=(DEPTH, n // tn),
        in_specs=[pl.BlockSpec((COND_PAD, D_MODEL), lambda l, j: (0, 0)),
                  pl.BlockSpec((None, D_MODEL, tn), lambda l, j: (l, 0, j)),
                  pl.BlockSpec((None, 1, tn), lambda l, j: (l, 0, j))],
        out_specs=pl.BlockSpec((None, COND_PAD, tn), lambda l, j: (l, 0, j)),
        out_shape=jax.ShapeDtypeStruct((DEPTH, COND_PAD, n), F32),
        compiler_params=_params(("parallel", "parallel"),
                                _vmem_limit(_nbytes((D_MODEL, tn), F32),
                                            temps=_nbytes((D_MODEL, tn), BF16))),
        name="modvec",
    )(cond, ada_w, ada_b.reshape(DEPTH, 1, n))
    out = out.reshape(DEPTH, COND_PAD, N_MOD, D_MODEL).transpose(0, 2, 1, 3)
    return out.reshape(DEPTH, N_MOD * COND_PAD, 1, D_MODEL)


def _mod_mm_kernel(h_ref, sh_ref, sc_ref, w_ref, b_ref, o_ref, u_ref):
    @pl.when(pl.program_id(1) == 0)
    def _():
        u_ref[...] = _modulate(h_ref[...], sh_ref[...], sc_ref[...]).astype(BF16)

    o_ref[...] = (_dot(u_ref[...], w_ref[...]) + b_ref[...]).astype(o_ref.dtype)


def _mod_mm(h, mods, w, b, which_shift, out_dtype=F32, tm=512, tn=1024):
    k, n = w.shape
    return pl.pallas_call(
        _mod_mm_kernel,
        grid=(ROWS // tm, n // tn),
        in_specs=[pl.BlockSpec((tm, k), lambda i, j: (i, 0)),
                  _mod_spec(which_shift, tm), _mod_spec(which_shift + 1, tm),
                  pl.BlockSpec((k, tn), lambda i, j: (0, j)),
                  pl.BlockSpec((1, tn), lambda i, j: (0, j))],
        out_specs=pl.BlockSpec((tm, tn), lambda i, j: (i, j)),
        out_shape=jax.ShapeDtypeStruct((ROWS, n), out_dtype),
        scratch_shapes=[pltpu.VMEM((tm, k), BF16)],
        compiler_params=_params(("parallel", "arbitrary"),
                                _vmem_limit(_nbytes((tm, k), F32) + _nbytes((k, tn), BF16)
                                            + _nbytes((tm, tn), F32),
                                            resident=_nbytes((tm, k), BF16),
                                            temps=_nbytes((tm, k), F32))),
        name="mod_mm",
    )(h, mods, mods, w, b.reshape(1, n))


def _mm_postnorm_kernel(a_ref, w_ref, bias_ref, h_ref, gate_ref, g_ref, b_ref, o_ref):
    y = _dot(a_ref[...], w_ref[...]) + bias_ref[...]
    o_ref[...] = _post_norm(h_ref[...], gate_ref[...] * y, g_ref[...], b_ref[...])


def _mm_postnorm(a, w, bias, h, mods, which_gate, ln_g, ln_b, tm=256):
    k = a.shape[1]
    return pl.pallas_call(
        _mm_postnorm_kernel,
        grid=(ROWS // tm,),
        in_specs=[pl.BlockSpec((tm, k), lambda i: (i, 0)),
                  pl.BlockSpec((k, D_MODEL), lambda i: (0, 0)),
                  _row_spec(),
                  pl.BlockSpec((tm, D_MODEL), lambda i: (i, 0)),
                  _mod_spec(which_gate, tm), _row_spec(), _row_spec()],
        out_specs=pl.BlockSpec((tm, D_MODEL), lambda i: (i, 0)),
        out_shape=jax.ShapeDtypeStruct((ROWS, D_MODEL), F32),
        compiler_params=_params(("parallel",),
                                _vmem_limit(_nbytes((tm, k), BF16) + _nbytes((k, D_MODEL), BF16)
                                            + 2 * _nbytes((tm, D_MODEL), F32),
                                            temps=3 * _nbytes((tm, D_MODEL), F32))),
        name="mm_postnorm",
    )(a, w, bias.reshape(1, D_MODEL), h, mods, ln_g.reshape(1, D_MODEL), ln_b.reshape(1, D_MODEL))


def _ffn_kernel(h_ref, sh_ref, sc_ref, gate_ref, g_ref, b_ref, wg_ref, wu_ref, wd_ref,
                o_ref, u_ref):
    f = pl.program_id(1)

    @pl.when(f == 0)
    def _():
        u_ref[...] = _modulate(h_ref[...], sh_ref[...], sc_ref[...]).astype(BF16)
        o_ref[...] = jnp.zeros_like(o_ref)

    u = u_ref[...]
    act = (jax.nn.silu(_dot(u, wg_ref[...])) * _dot(u, wu_ref[...])).astype(BF16)
    o_ref[...] += _dot(act, wd_ref[...])

    @pl.when(f == pl.num_programs(1) - 1)
    def _():
        o_ref[...] = _post_norm(h_ref[...], gate_ref[...] * o_ref[...], g_ref[...], b_ref[...])


def _ffn(h, mods, w_gate, w_up, w_down, ln_g, ln_b, tm=512, tf=512):
    return pl.pallas_call(
        _ffn_kernel,
        grid=(ROWS // tm, D_FF // tf),
        in_specs=[pl.BlockSpec((tm, D_MODEL), lambda i, f: (i, 0)),
                  _mod_spec(3, tm), _mod_spec(4, tm), _mod_spec(5, tm),
                  _row_spec(), _row_spec(),
                  pl.BlockSpec((D_MODEL, tf), lambda i, f: (0, f)),
                  pl.BlockSpec((D_MODEL, tf), lambda i, f: (0, f)),
                  pl.BlockSpec((tf, D_MODEL), lambda i, f: (f, 0))],
        out_specs=pl.BlockSpec((tm, D_MODEL), lambda i, f: (i, 0)),
        out_shape=jax.ShapeDtypeStruct((ROWS, D_MODEL), F32),
        scratch_shapes=[pltpu.VMEM((tm, D_MODEL), BF16)],
        compiler_params=_params(("parallel", "arbitrary"),
                                _vmem_limit(2 * _nbytes((tm, D_MODEL), F32)
                                            + 3 * _nbytes((D_MODEL, tf), BF16),
                                            resident=_nbytes((tm, D_MODEL), BF16),
                                            temps=3 * _nbytes((tm, D_MODEL), F32))),
        name="ffn",
    )(h, mods, mods, mods, ln_g.reshape(1, D_MODEL), ln_b.reshape(1, D_MODEL),
      w_gate, w_up, w_down)


def _mla_down_kernel(h_ref, sh_ref, sc_ref, wdq_ref, wdkv_ref, wkr_ref, qn_ref, kvn_ref,
                     rope_ref, cq_ref, ckv_ref, kr_ref, kr2_ref):
    u = _modulate(h_ref[...], sh_ref[...], sc_ref[...]).astype(BF16)
    cq_ref[...] = _rms_norm(_dot(u, wdq_ref[...]), qn_ref[...]).astype(BF16)
    ckv_ref[...] = _rms_norm(_dot(u, wdkv_ref[...]), kvn_ref[...])
    t = _dot(u, wkr_ref[...])
    kr_ref[...] = t[:, :QK_ROPE]
    v = t * rope_ref[...]
    kr2_ref[...] = (v + pltpu.roll(v, QK_ROPE, 1)).astype(BF16)


def _mla_down(h, mods, w_dq, w_dkv, w_kr2, q_norm, kv_norm, rope_tab, tm=512):
    row = lambda width: pl.BlockSpec((tm, width), lambda i: (i, 0))
    full = lambda shape: pl.BlockSpec(shape, lambda i: (0, 0))
    return pl.pallas_call(
        _mla_down_kernel,
        grid=(ROWS // tm,),
        in_specs=[row(D_MODEL), _mod_spec(0, tm), _mod_spec(1, tm),
                  full((D_MODEL, Q_RANK)), full((D_MODEL, KV_RANK)), full((D_MODEL, 2 * QK_ROPE)),
                  _row_spec(Q_RANK), _row_spec(KV_RANK), row(2 * QK_ROPE)],
        out_specs=[row(Q_RANK), row(KV_RANK), row(QK_ROPE), row(2 * QK_ROPE)],
        out_shape=[jax.ShapeDtypeStruct((ROWS, Q_RANK), BF16),
                   jax.ShapeDtypeStruct((ROWS, KV_RANK), F32),
                   jax.ShapeDtypeStruct((ROWS, QK_ROPE), F32),
                   jax.ShapeDtypeStruct((ROWS, 2 * QK_ROPE), BF16)],
        compiler_params=_params(("parallel",),
                                _vmem_limit(_nbytes((tm, D_MODEL), F32)
                                            + _nbytes((D_MODEL, Q_RANK + KV_RANK + 2 * QK_ROPE), BF16)
                                            + 3 * _nbytes((tm, KV_RANK), F32),
                                            temps=2 * _nbytes((tm, D_MODEL), F32))),
        name="mla_down",
    )(h, mods, mods, w_dq, w_dkv, w_kr2, q_norm.reshape(1, Q_RANK), kv_norm.reshape(1, KV_RANK),
      rope_tab)


NT_DIMS = (((1,), (1,)), ((), ()))


def _q_up_kernel(cq_ref, w_ref, rope_ref, q_ref):
    cq, tab = cq_ref[...], rope_ref[...] * LOG2E_SCALE
    for h in range(MLA_HEADS):
        r = _dot(cq, w_ref[:, h * HEAD_W:(h + 1) * HEAD_W])
        q_ref[h, :, :QK_NOPE] = (r[:, :QK_NOPE] * LOG2E_SCALE).astype(BF16)
        q_ref[h, :, QK_NOPE:] = (r[:, QK_NOPE:] * tab).astype(BF16)


def _q_up(cq, w_q, rope_tab, tm=512):
    return pl.pallas_call(
        _q_up_kernel,
        grid=(ROWS // tm,),
        in_specs=[pl.BlockSpec((tm, Q_RANK), lambda i: (i, 0)),
                  pl.BlockSpec((Q_RANK, MLA_HEADS * HEAD_W), lambda i: (0, 0)),
                  pl.BlockSpec((tm, 2 * QK_ROPE), lambda i: (i, 0))],
        out_specs=pl.BlockSpec((MLA_HEADS, tm, HEAD_W), lambda i: (0, i, 0)),
        out_shape=jax.ShapeDtypeStruct((MLA_HEADS, ROWS, HEAD_W), BF16),
        compiler_params=_params(("parallel",),
                                _vmem_limit(_nbytes((Q_RANK + tm, MLA_HEADS * HEAD_W), BF16))),
        name="q_up",
    )(cq, w_q, rope_tab)


def _kv_expand_kernel(ckv_ref, kr2_ref, wk_ref, wvt_ref, k_ref, vt_ref):
    c = ckv_ref[...].astype(BF16)
    vt_ref[...] = lax.dot_general(wvt_ref[...], c, NT_DIMS,
                                  preferred_element_type=F32).astype(BF16)
    kr2 = kr2_ref[...]
    pair_w = 2 * QK_NOPE
    for g in range(MLA_HEADS // 2):
        r = _dot(c, wk_ref[:, g * pair_w:(g + 1) * pair_w]).astype(BF16)
        for e in range(2):
            k_ref[2 * g + e, :, :QK_NOPE] = r[:, e * QK_NOPE:(e + 1) * QK_NOPE]
            k_ref[2 * g + e, :, QK_NOPE:] = kr2


def _kv_expand(ckv, kr2, w_k, w_vt, tm=512):
    rows = ckv.shape[0]
    return pl.pallas_call(
        _kv_expand_kernel,
        grid=(rows // tm,),
        in_specs=[pl.BlockSpec((tm, KV_RANK), lambda i: (i, 0)),
                  pl.BlockSpec((tm, 2 * QK_ROPE), lambda i: (i, 0)),
                  pl.BlockSpec((KV_RANK, MLA_HEADS * QK_NOPE), lambda i: (0, 0)),
                  pl.BlockSpec((MLA_HEADS * V_DIM, KV_RANK), lambda i: (0, 0))],
        out_specs=[pl.BlockSpec((MLA_HEADS, tm, HEAD_W), lambda i: (0, i, 0)),
                   pl.BlockSpec((MLA_HEADS * V_DIM, tm), lambda i: (0, i))],
        out_shape=[jax.ShapeDtypeStruct((MLA_HEADS, rows, HEAD_W), BF16),
                   jax.ShapeDtypeStruct((MLA_HEADS * V_DIM, rows), BF16)],
        compiler_params=_params(("parallel",),
                                _vmem_limit(_nbytes((tm, MLA_HEADS * (HEAD_W + V_DIM)), BF16)
                                            + 2 * _nbytes((KV_RANK, MLA_HEADS * V_DIM), BF16),
                                            temps=_nbytes((MLA_HEADS * V_DIM, tm), F32))),
        name="kv_expand",
    )(ckv, kr2, w_k, w_vt)


ATT_CHUNK = 512
LOG2E_SCALE = ATT_SCALE * math.log2(math.e)


def _attn_scores(q, k):
    return lax.dot_general(k, q, NT_DIMS, preferred_element_type=F32)


def _attn_values(s, vt, carry):
    m = jnp.max(s, axis=0, keepdims=True)
    if carry is not None:
        m_old, l_old, acc_old = carry
        m = jnp.maximum(m_old, m)
    p = jnp.exp2(s - m)
    l = jnp.sum(p, axis=0, keepdims=True)
    acc = _dot(vt, p.astype(BF16))
    if carry is not None:
        alpha = jnp.exp2(m_old - m)
        l = alpha * l_old + l
        acc = alpha * acc_old + acc
    return m, l, acc


def _attn_ctx_kernel(q_ref, k_ref, vt_ref, o_ref):
    for h in range(MLA_HEADS):
        _, l, acc = _attn_values(_attn_scores(q_ref[h], k_ref[h]),
                                 vt_ref[h * V_DIM:(h + 1) * V_DIM, :], None)
        o_ref[:, h * V_DIM:(h + 1) * V_DIM] = (acc / l).T.astype(BF16)


def _attn_lat_kernel(q_ref, k_ref, vt_ref, kc_ref, vtc_ref, prev_ref, o_ref):
    del prev_ref
    q = q_ref[...]
    n_tok = DEC_SEQ // ATT_CHUNK
    rows = lambda c: slice(c * ATT_CHUNK, (c + 1) * ATT_CHUNK)
    carry = None
    s = _attn_scores(q, k_ref[rows(0), :])
    for c in range(n_tok):
        k_next = k_ref[rows(c + 1), :] if c + 1 < n_tok else kc_ref[...]
        s_next = _attn_scores(q, k_next)
        carry = _attn_values(s, vt_ref[:, rows(c)], carry)
        s = s_next
    _, l, acc = _attn_values(s, vtc_ref[...], carry)
    o_ref[...] = (acc / l).T.astype(BF16)


def _attention(q, k_tok, vt_tok, k_cache, vt_cache, tq=512):
    assert PAST_LEN == ATT_CHUNK
    out_shape = jax.ShapeDtypeStruct((ROWS, MLA_HEADS * V_DIM), BF16)
    o = pl.pallas_call(
        _attn_ctx_kernel,
        grid=(BATCH,),
        in_specs=[pl.BlockSpec((MLA_HEADS, SEQ, HEAD_W), lambda s: (0, s, 0)),
                  pl.BlockSpec((MLA_HEADS, SEQ, HEAD_W), lambda s: (0, s, 0)),
                  pl.BlockSpec((MLA_HEADS * V_DIM, SEQ), lambda s: (0, s))],
        out_specs=pl.BlockSpec((SEQ, MLA_HEADS * V_DIM), lambda s: (s, 0)),
        out_shape=out_shape,
        compiler_params=_params(("parallel",), VMEM_FLOOR_BYTES),
        name="attn_ctx",
    )(q, k_tok, vt_tok)

    lat0 = ROWS_CTX // DEC_SEQ
    q0 = ROWS_CTX // tq
    nq = DEC_SEQ // tq
    return pl.pallas_call(
        _attn_lat_kernel,
        grid=(DEC_BATCH, MLA_HEADS, nq),
        in_specs=[pl.BlockSpec((None, tq, HEAD_W), lambda b, h, i: (h, q0 + b * nq + i, 0)),
                  pl.BlockSpec((None, DEC_SEQ, HEAD_W), lambda b, h, i: (h, lat0 + b, 0)),
                  pl.BlockSpec((V_DIM, DEC_SEQ), lambda b, h, i: (h, lat0 + b)),
                  pl.BlockSpec((None, PAST_LEN, HEAD_W), lambda b, h, i: (h, b, 0)),
                  pl.BlockSpec((V_DIM, PAST_LEN), lambda b, h, i: (h, b)),
                  pl.BlockSpec(memory_space=pl.ANY)],
        out_specs=pl.BlockSpec((tq, V_DIM), lambda b, h, i: (q0 + b * nq + i, h)),
        out_shape=out_shape,
        input_output_aliases={5: 0},
        compiler_params=_params(("parallel", "parallel", "arbitrary"),
                                _vmem_limit(_nbytes((DEC_SEQ + PAST_LEN, HEAD_W + V_DIM), BF16),
                                            temps=8 * _nbytes((ATT_CHUNK, tq), F32))),
        name="attn_lat",
    )(q, k_tok, vt_tok, k_cache, vt_cache, o)


def _mm_kernel(a_ref, b_ref, o_ref, acc_ref):
    k = pl.program_id(2)

    @pl.when(k == 0)
    def _():
        acc_ref[...] = jnp.zeros_like(acc_ref)

    acc_ref[...] += _dot(a_ref[...], b_ref[...])

    @pl.when(k == pl.num_programs(2) - 1)
    def _():
        o_ref[...] = acc_ref[...].astype(o_ref.dtype)


def _mm(a, b, out_dtype=F32, tm=1024, tn=1024, tk=512):
    m, kk = a.shape
    n = b.shape[1]
    tm, tn, tk = min(tm, m), min(tn, n), min(tk, kk)
    return pl.pallas_call(
        _mm_kernel,
        grid=(m // tm, n // tn, kk // tk),
        in_specs=[pl.BlockSpec((tm, tk), lambda i, j, k: (i, k)),
                  pl.BlockSpec((tk, tn), lambda i, j, k: (k, j))],
        out_specs=pl.BlockSpec((tm, tn), lambda i, j, k: (i, j)),
        out_shape=jax.ShapeDtypeStruct((m, n), out_dtype),
        scratch_shapes=[pltpu.VMEM((tm, tn), F32)],
        compiler_params=_params(("parallel", "parallel", "arbitrary"), VMEM_FLOOR_BYTES),
        name="mm",
    )(a, b)


def _seq_tiles(length):
    if length >= 1024:
        return 1024, 512, 1024
    return length, D_MODEL, length


def _dft_fwd_kernel(c_ref, s_ref, z_ref, kr_ref, ks_ref, yr_ref, ys_ref, accr_ref, accs_ref):
    k = pl.program_id(3)

    @pl.when(k == 0)
    def _():
        accr_ref[...] = jnp.zeros_like(accr_ref)
        accs_ref[...] = jnp.zeros_like(accs_ref)

    z = z_ref[...].astype(BF16)
    accr_ref[...] += _dot(c_ref[...], z)
    accs_ref[...] += _dot(s_ref[...], z)

    @pl.when(k == pl.num_programs(3) - 1)
    def _():
        zr, zs, kr, ks = accr_ref[...], accs_ref[...], kr_ref[...], ks_ref[...]
        yr_ref[...] = (zr * kr - zs * ks).astype(BF16)
        ys_ref[...] = (zr * ks + zs * kr).astype(BF16)


def _dft_fwd(tabs, z, z_col0, kr, ks, order, n_seq, length):
    tf, tn, tk = _seq_tiles(length)
    nb, nf, nk = D_MODEL // tn, length // tf, length // tk
    zc0, kc0 = z_col0 // tn, order * nb
    out = jax.ShapeDtypeStruct((n_seq * length, D_MODEL), BF16)
    return pl.pallas_call(
        _dft_fwd_kernel,
        grid=(n_seq, nf, nb, nk),
        in_specs=[pl.BlockSpec((tf, tk), lambda s, f, n, k: (f, k)),
                  pl.BlockSpec((tf, tk), lambda s, f, n, k: (f, k)),
                  pl.BlockSpec((tk, tn), lambda s, f, n, k: (s * nk + k, zc0 + n)),
                  pl.BlockSpec((tf, tn), lambda s, f, n, k: (f, kc0 + n)),
                  pl.BlockSpec((tf, tn), lambda s, f, n, k: (f, kc0 + n))],
        out_specs=[pl.BlockSpec((tf, tn), lambda s, f, n, k: (s * nf + f, n)),
                   pl.BlockSpec((tf, tn), lambda s, f, n, k: (s * nf + f, n))],
        out_shape=[out, out],
        scratch_shapes=[pltpu.VMEM((tf, tn), F32), pltpu.VMEM((tf, tn), F32)],
        compiler_params=_params(("parallel", "parallel", "parallel", "arbitrary"),
                                _vmem_limit(2 * _nbytes((tf, tk), BF16) + _nbytes((tk, tn), F32)
                                            + 2 * _nbytes((tf, tn), F32) + 2 * _nbytes((tf, tn), BF16),
                                            resident=2 * _nbytes((tf, tn), F32),
                                            temps=4 * _nbytes((tf, tn), F32))),
        name="dft_fwd",
    )(tabs["c"], tabs["s"], z, kr, ks)


def _dft_inv_kernel(inv_len, ct_ref, st_ref, yr_ref, ys_ref, z_ref, gate_ref, skip_ref,
                    o_ref, acc_ref):
    k = pl.program_id(3)

    @pl.when(k == 0)
    def _():
        acc_ref[...] = jnp.zeros_like(acc_ref)

    acc_ref[...] += _dot(ct_ref[...], yr_ref[...]) + _dot(st_ref[...], ys_ref[...])

    @pl.when(k == pl.num_programs(3) - 1)
    def _():
        y = acc_ref[...] * inv_len + skip_ref[...] * z_ref[...]
        o_ref[...] = (gate_ref[...] * y).astype(o_ref.dtype)


def _dft_inv(tabs, yr, ys, z, z_col0, gate, gate_col0, skip, order, n_seq, length, out_dtype):
    tt, tn, tk = _seq_tiles(length)
    nb, nt, nk = D_MODEL // tn, length // tt, length // tk
    zc0, gc0 = z_col0 // tn, gate_col0 // tn
    return pl.pallas_call(
        functools.partial(_dft_inv_kernel, 1.0 / length),
        grid=(n_seq, nt, nb, nk),
        in_specs=[pl.BlockSpec((tt, tk), lambda s, t, n, k: (t, k)),
                  pl.BlockSpec((tt, tk), lambda s, t, n, k: (t, k)),
                  pl.BlockSpec((tk, tn), lambda s, t, n, k: (s * nk + k, n)),
                  pl.BlockSpec((tk, tn), lambda s, t, n, k: (s * nk + k, n)),
                  pl.BlockSpec((tt, tn), lambda s, t, n, k: (s * nt + t, zc0 + n)),
                  pl.BlockSpec((tt, tn), lambda s, t, n, k: (s * nt + t, gc0 + n)),
                  pl.BlockSpec((None, 1, tn), lambda s, t, n, k: (order, 0, n))],
        out_specs=pl.BlockSpec((tt, tn), lambda s, t, n, k: (s * nt + t, n)),
        out_shape=jax.ShapeDtypeStruct((n_seq * length, D_MODEL), out_dtype),
        scratch_shapes=[pltpu.VMEM((tt, tn), F32)],
        compiler_params=_params(("parallel", "parallel", "parallel", "arbitrary"),
                                _vmem_limit(2 * _nbytes((tt, tk), BF16) + 2 * _nbytes((tk, tn), BF16)
                                            + 3 * _nbytes((tt, tn), F32),
                                            resident=_nbytes((tt, tn), F32),
                                            temps=3 * _nbytes((tt, tn), F32))),
        name="dft_inv",
    )(tabs["ct"], tabs["st"], yr, ys, z, gate, skip)


def _shortconv_kernel(x_ref, w_ref, b_ref, o_ref):
    x = x_ref[...]
    n = x.shape[0]
    row = lax.broadcasted_iota(jnp.int32, x.shape, 0)
    before = jnp.where(row == 0, 0.0, pltpu.roll(x, 1, 0))
    after = jnp.where(row == n - 1, 0.0, pltpu.roll(x, n - 1, 0))
    o_ref[...] = w_ref[0:1, :] * before + w_ref[1:2, :] * x + w_ref[2:3, :] * after + b_ref[...]


def _shortconv(x, w, b, row0, n_seq, length):
    width = x.shape[1]
    tc = 256 if length >= 1024 else 2048
    s0 = row0 // length
    return pl.pallas_call(
        _shortconv_kernel,
        grid=(n_seq, width // tc),
        in_specs=[pl.BlockSpec((length, tc), lambda s, j: (s0 + s, j)),
                  pl.BlockSpec((HY_CONV, tc), lambda s, j: (0, j)),
                  pl.BlockSpec((1, tc), lambda s, j: (0, j))],
        out_specs=pl.BlockSpec((length, tc), lambda s, j: (s, j)),
        out_shape=jax.ShapeDtypeStruct((n_seq * length, width), F32),
        compiler_params=_params(("parallel", "parallel"),
                                _vmem_limit(2 * _nbytes((length, tc), F32),
                                            temps=4 * _nbytes((length, tc), F32))),
        name="shortconv",
    )(x, w, b.reshape(1, width))


def _filter_mlp_kernel(feat_ref, t_ref, w1_ref, b1_ref, fr1_ref, w2_ref, b2_ref, fr2_ref,
                       w3_ref, decay_ref, h_ref, ss_ref):
    x = jnp.sin(fr1_ref[...] * (_dot(feat_ref[...].astype(BF16), w1_ref[...].astype(BF16))
                                + b1_ref[...]))
    x = jnp.sin(fr2_ref[...] * (_dot(x.astype(BF16), w2_ref[...].astype(BF16)) + b2_ref[...]))
    h = _dot(x.astype(BF16), w3_ref[...].astype(BF16))
    h = h * (jnp.exp(-t_ref[...] * jnp.exp(decay_ref[...])) + HY_SHIFT)
    h_ref[...] = h

    @pl.when(pl.program_id(0) == 0)
    def _():
        ss_ref[...] = jnp.zeros_like(ss_ref)

    ss_ref[...] += jnp.sum(h * h, axis=0, keepdims=True)


def _filter_combine_kernel(hf_ref, hb_ref, ssf_ref, ssb_ref, a_ref, b_ref):
    norm = lax.rsqrt(ssf_ref[...] + ssb_ref[...] + 1e-12)
    fwd = hf_ref[...] * norm
    bwd = hb_ref[...] * norm
    row = lax.broadcasted_iota(jnp.int32, bwd.shape, 0) + pl.program_id(0) * bwd.shape[0]
    bwd = jnp.where(row == 0, 0.0, bwd)
    a_ref[...] = (fwd + bwd).astype(BF16)
    b_ref[...] = (fwd - bwd).astype(BF16)


def _hyena_filter_spectra(tabs, length, f_w1, f_b1, f_freq1, f_w2, f_b2, f_freq2, f_w3, log_decay):
    t = jnp.linspace(0.0, 1.0, length, dtype=F32)[:, None]
    t_idx = jnp.arange(length, dtype=F32)[:, None]
    bands = jnp.linspace(1e-4, HY_BANDS - 1, HY_BANDS, dtype=F32)
    w = 2.0 * math.pi * t_idx * bands / length
    feat = jnp.concatenate([t, jnp.cos(w), -jnp.sin(w)], axis=-1)
    emb_pad = 128
    feat = jnp.pad(feat, ((0, 0), (0, emb_pad - HY_EMB)))
    w1 = jnp.pad(f_w1, ((0, emb_pad - HY_EMB), (0, 0)))
    n_all = HY_DIRS * HY_ORDER * D_MODEL
    n_dir = HY_ORDER * D_MODEL
    tm = 256
    full = lambda shape: pl.BlockSpec(shape, lambda i: (0, 0))
    h, ss = pl.pallas_call(
        _filter_mlp_kernel,
        grid=(length // tm,),
        in_specs=[pl.BlockSpec((tm, emb_pad), lambda i: (i, 0)),
                  pl.BlockSpec((tm, 1), lambda i: (i, 0)),
                  full((emb_pad, HY_FW)), full((1, HY_FW)), full((1, HY_FW)),
                  full((HY_FW, HY_FW)), full((1, HY_FW)), full((1, HY_FW)),
                  full((HY_FW, n_all)), full((1, n_all))],
        out_specs=[pl.BlockSpec((tm, n_all), lambda i: (i, 0)), full((1, n_all))],
        out_shape=[jax.ShapeDtypeStruct((length, n_all), F32),
                   jax.ShapeDtypeStruct((1, n_all), F32)],
        compiler_params=_params(("arbitrary",),
                                _vmem_limit(_nbytes((tm, n_all), F32) + _nbytes((HY_FW, n_all), F32),
                                            temps=3 * _nbytes((tm, n_all), F32))),
        name="filter_mlp",
    )(feat, t, w1, f_b1.reshape(1, HY_FW), f_freq1.reshape(1, HY_FW), f_w2,
      f_b2.reshape(1, HY_FW), f_freq2.reshape(1, HY_FW), f_w3,
      log_decay.reshape(1, n_all))

    tn = 1024
    nb = n_dir // tn
    comb = jax.ShapeDtypeStruct((length, n_dir), BF16)
    a, b = pl.pallas_call(
        _filter_combine_kernel,
        grid=(length // tm, nb),
        in_specs=[pl.BlockSpec((tm, tn), lambda i, j: (i, j)),
                  pl.BlockSpec((tm, tn), lambda i, j: (i, nb + j)),
                  pl.BlockSpec((1, tn), lambda i, j: (0, j)),
                  pl.BlockSpec((1, tn), lambda i, j: (0, nb + j))],
        out_specs=[pl.BlockSpec((tm, tn), lambda i, j: (i, j)),
                   pl.BlockSpec((tm, tn), lambda i, j: (i, j))],
        out_shape=[comb, comb],
        compiler_params=_params(("parallel", "parallel"), VMEM_FLOOR_BYTES),
        name="filter_combine",
    )(h, h, ss, ss)
    return _mm(tabs["c"], a), _mm(tabs["s"], b)


def _cis_product(row_hi, row_lo, period):
    def cis(phase):
        ang = (phase % period).astype(F32) * (2.0 * math.pi / period)
        return jnp.cos(ang)[:, :, None], jnp.sin(ang)[:, :, None]
    (c1, s1), (c0, s0) = cis(row_hi), cis(row_lo)
    c0, s0 = jnp.swapaxes(c0, 1, 2), jnp.swapaxes(s0, 1, 2)
    rows = row_hi.shape[0]
    return ((c1 * c0 - s1 * s0).reshape(rows, -1), (s1 * c0 + c1 * s0).reshape(rows, -1))


def _odd_dft_tables(length):
    split = 1 << (length.bit_length() // 2)
    r = jnp.arange(length, dtype=jnp.int32)[:, None]
    hi = jnp.arange(length // split, dtype=jnp.int32)[None, :] * split
    lo = jnp.arange(split, dtype=jnp.int32)[None, :]
    c, s = _cis_product((2 * r + 1) * hi, (2 * r + 1) * lo, 4 * length)
    ct, st = _cis_product(r * (2 * hi), r * (2 * lo + 1), 4 * length)
    return {"c": c.astype(BF16), "s": s.astype(BF16), "ct": ct.astype(BF16), "st": st.astype(BF16)}


def _dft_tables(length):
    split = 1 << (length.bit_length() // 2)
    r = jnp.arange(length, dtype=jnp.int32)[:, None]
    hi = jnp.arange(length // split, dtype=jnp.int32)[None, :] * split
    lo = jnp.arange(split, dtype=jnp.int32)[None, :]
    c, s = _cis_product(r * hi, r * lo, length)
    return c.astype(BF16), (-s).astype(BF16)


def _hyena_mix(proj, conv_w, conv_b, kr, ks, skip, tabs, row0, n_seq, length):
    pc = _shortconv(proj, conv_w, conv_b, row0, n_seq, length)
    v0, x1, x2 = 0, D_MODEL, 2 * D_MODEL
    yr, ys = _dft_fwd(tabs, pc, v0, kr, ks, 0, n_seq, length)
    z1 = _dft_inv(tabs, yr, ys, pc, v0, pc, x1, skip, 0, n_seq, length, F32)
    yr, ys = _dft_fwd(tabs, z1, 0, kr, ks, 1, n_seq, length)
    return _dft_inv(tabs, yr, ys, z1, 0, pc, x2, skip, 1, n_seq, length, BF16)


def _fnet_chan_kernel(h_ref, sh_ref, sc_ref, w_ref, p_ref, q_ref):
    u = _modulate(h_ref[...], sh_ref[...], sc_ref[...]).astype(BF16)
    r = _dot(u, w_ref[...])
    p_ref[...] = r[:, :FNET_CG].astype(BF16)
    q_ref[...] = r[:, FNET_CG:].astype(BF16)


def _fnet_chan(h, mods, w_cs, tm=512):
    mod = lambda which: pl.BlockSpec(
        (None, 1, FNET_CG), lambda i, g: (which * COND_PAD + _group_of_tile(i, tm), 0, g))
    out = jax.ShapeDtypeStruct((ROWS, D_MODEL), BF16)
    return pl.pallas_call(
        _fnet_chan_kernel,
        grid=(ROWS // tm, FNET_GROUPS),
        in_specs=[pl.BlockSpec((tm, FNET_CG), lambda i, g: (i, g)), mod(0), mod(1),
                  pl.BlockSpec((FNET_CG, 2 * FNET_CG), lambda i, g: (0, 0))],
        out_specs=[pl.BlockSpec((tm, FNET_CG), lambda i, g: (i, g)),
                   pl.BlockSpec((tm, FNET_CG), lambda i, g: (i, g))],
        out_shape=[out, out],
        compiler_params=_params(("parallel", "parallel"), VMEM_FLOOR_BYTES),
        name="fnet_chan",
    )(h, mods, mods, w_cs)


def _fnet_pos_kernel(scale, c_ref, ns_ref, p_ref, q_ref, prev_ref, o_ref, acc_ref):
    del prev_ref
    k = pl.program_id(3)

    @pl.when(k == 0)
    def _():
        acc_ref[...] = jnp.zeros_like(acc_ref)

    acc_ref[...] += _dot(c_ref[...], p_ref[...]) + _dot(ns_ref[...], q_ref[...])

    @pl.when(k == pl.num_programs(3) - 1)
    def _():
        o_ref[...] = (acc_ref[...] * scale).astype(o_ref.dtype)


def _fnet_pos(c_tab, ns_tab, p, q, prev, row0, n_seq, length):
    tt, tn, tk = _seq_tiles(length)
    nb, nt, nk = D_MODEL // tn, length // tt, length // tk
    t0, k0 = row0 // tt, row0 // tk
    scale = (length * FNET_CG) ** -0.5
    return pl.pallas_call(
        functools.partial(_fnet_pos_kernel, scale),
        grid=(n_seq, nt, nb, nk),
        in_specs=[pl.BlockSpec((tt, tk), lambda s, t, n, k: (t, k)),
                  pl.BlockSpec((tt, tk), lambda s, t, n, k: (t, k)),
                  pl.BlockSpec((tk, tn), lambda s, t, n, k: (k0 + s * nk + k, n)),
                  pl.BlockSpec((tk, tn), lambda s, t, n, k: (k0 + s * nk + k, n)),
                  pl.BlockSpec(memory_space=pl.ANY)],
        out_specs=pl.BlockSpec((tt, tn), lambda s, t, n, k: (t0 + s * nt + t, n)),
        out_shape=jax.ShapeDtypeStruct((ROWS, D_MODEL), BF16),
        scratch_shapes=[pltpu.VMEM((tt, tn), F32)],
        input_output_aliases={4: 0},
        compiler_params=_params(("parallel", "parallel", "parallel", "arbitrary"),
                                _vmem_limit(2 * _nbytes((tt, tk), BF16) + 2 * _nbytes((tk, tn), BF16)
                                            + _nbytes((tt, tn), BF16),
                                            resident=_nbytes((tt, tn), F32),
                                            temps=2 * _nbytes((tt, tn), F32))),
        name="fnet_pos",
    )(c_tab, ns_tab, p, q, prev)


def _rope_table():
    rows = DEC_SEQ // GRID_W
    row = jnp.repeat(jnp.arange(rows), GRID_W).astype(F32)
    col = jnp.tile(jnp.arange(GRID_W), rows).astype(F32)
    inv = ROPE_THETA ** (-jnp.arange(0, AXIS_ROPE, 2, dtype=F32) / AXIS_ROPE)
    ang = jnp.concatenate([row[:, None] * inv, col[:, None] * inv], axis=-1)
    cos = jnp.repeat(jnp.cos(ang), 2, axis=-1)
    sin = jnp.repeat(jnp.sin(ang), 2, axis=-1)
    lat = jnp.tile(jnp.concatenate([cos, sin], axis=-1), (DEC_BATCH, 1))
    ctx = jnp.concatenate([jnp.ones((ROWS_CTX, QK_ROPE), F32), jnp.zeros((ROWS_CTX, QK_ROPE), F32)],
                          axis=-1)
    return jnp.concatenate([ctx, lat], axis=0)


def _pair_rotated(w):
    pairs = w.reshape(w.shape[:-1] + (QK_ROPE // 2, 2))
    return jnp.stack([-pairs[..., 1], pairs[..., 0]], axis=-1).reshape(w.shape)


def kernel(x_prompt, x_sample, c, cache_ckv, cache_krope, c_ctx, ada_w, ada_b, ln_g, ln_b, ffn_w_gate, ffn_w_up, ffn_w_down, mla_w_dq, mla_q_norm, mla_w_uq, mla_w_dkv, mla_kv_norm, mla_w_kr, mla_w_ukv, mla_w_o, hy_w_in, hy_b_in, hy_conv_w, hy_conv_b, hy_f_w1, hy_f_b1, hy_f_freq1, hy_f_w2, hy_f_b2, hy_f_freq2, hy_f_w3, hy_log_decay, hy_skip, hy_w_out, hy_b_out, fn_w_out, fn_b_out):
    assert x_prompt.shape == (BATCH, SEQ, D_MODEL) and x_sample.shape == (DEC_BATCH, DEC_SEQ, D_MODEL)
    assert ROWS_CTX % DEC_SEQ == 0 and SEQ == FNET_CG

    h = jnp.concatenate([x_prompt.reshape(ROWS_CTX, D_MODEL), x_sample.reshape(ROWS_LAT, D_MODEL)])
    cond = jnp.concatenate([c_ctx[None, :], c, jnp.zeros((COND_PAD - N_COND, D_MODEL), F32)])
    mods_all = _modulation_vectors(cond, ada_w, ada_b)
    zero_bias = jnp.zeros((D_MODEL,), F32)
    rope_tab = None
    ckv_states, krope_states = [], []

    for i in range(DEPTH):
        kind, j = i % N_MIXERS, i // N_MIXERS
        mods = mods_all[i]
        if kind == 0:
            if rope_tab is None:
                rope_tab = _rope_table()
            w_kr2 = jnp.concatenate([mla_w_kr[j], _pair_rotated(mla_w_kr[j])], axis=-1).astype(BF16)
            wq = mla_w_uq[j].reshape(Q_RANK, MLA_HEADS, QK_NOPE + QK_ROPE)
            w_q = jnp.concatenate([wq, _pair_rotated(wq[..., QK_NOPE:])], axis=-1)
            w_q = w_q.reshape(Q_RANK, MLA_HEADS * HEAD_W).astype(BF16)
            w_ukv = mla_w_ukv[j].reshape(KV_RANK, MLA_HEADS, QK_NOPE + V_DIM)
            w_k = w_ukv[..., :QK_NOPE].reshape(KV_RANK, MLA_HEADS * QK_NOPE).astype(BF16)
            w_vt = w_ukv[..., QK_NOPE:].reshape(KV_RANK, MLA_HEADS * V_DIM).T.astype(BF16)
            cq, ckv, kr, kr2 = _mla_down(h, mods, mla_w_dq[j].astype(BF16), mla_w_dkv[j].astype(BF16),
                                         w_kr2, mla_q_norm[j], mla_kv_norm[j], rope_tab)
            ckv_states.append(ckv[:ROWS_CTX].reshape(BATCH, SEQ, KV_RANK))
            krope_states.append(kr[:ROWS_CTX].reshape(BATCH, SEQ, QK_ROPE))
            q = _q_up(cq, w_q, rope_tab)
            k_tok, vt_tok = _kv_expand(ckv, kr2, w_k, w_vt)
            kc = cache_krope[:, j].reshape(DEC_BATCH * PAST_LEN, QK_ROPE).astype(BF16)
            k_cache, vt_cache = _kv_expand(cache_ckv[:, j].reshape(DEC_BATCH * PAST_LEN, KV_RANK),
                                           jnp.concatenate([kc, kc], axis=-1), w_k, w_vt)
            o = _attention(q, k_tok, vt_tok, k_cache, vt_cache)
            h = _mm_postnorm(o, mla_w_o[j].astype(BF16), zero_bias, h, mods, 2, ln_g[i, 0], ln_b[i, 0])
        elif kind == 1:
            proj = _mod_mm(h, mods, hy_w_in[j].astype(BF16), hy_b_in[j], 0)
            fp = (hy_f_w1[j], hy_f_b1[j], hy_f_freq1[j], hy_f_w2[j], hy_f_b2[j], hy_f_freq2[j],
                  hy_f_w3[j], hy_log_decay[j])
            skip = hy_skip[j].reshape(HY_ORDER, 1, D_MODEL)
            outs = []
            for row0, n_seq, length in ((0, BATCH, SEQ), (ROWS_CTX, DEC_BATCH, DEC_SEQ)):
                tabs = _odd_dft_tables(length)
                kr_f, ks_f = _hyena_filter_spectra(tabs, length, *fp)
                outs.append(_hyena_mix(proj, hy_conv_w[j], hy_conv_b[j], kr_f, ks_f, skip, tabs,
                                       row0, n_seq, length))
            h = _mm_postnorm(jnp.concatenate(outs), hy_w_out[j].astype(BF16), hy_b_out[j], h, mods, 2,
                             ln_g[i, 0], ln_b[i, 0])
        else:
            c_ch, ns_ch = _dft_tables(FNET_CG)
            p, q = _fnet_chan(h, mods, jnp.concatenate([c_ch, -ns_ch], axis=-1))
            f = _fnet_pos(c_ch, ns_ch, p, q, jnp.zeros((ROWS, D_MODEL), BF16), 0, BATCH, SEQ)
            c_pos, ns_pos = _dft_tables(DEC_SEQ)
            f = _fnet_pos(c_pos, ns_pos, p, q, f, ROWS_CTX, DEC_BATCH, DEC_SEQ)
            h = _mm_postnorm(f, fn_w_out[j].astype(BF16), fn_b_out[j], h, mods, 2,
                             ln_g[i, 0], ln_b[i, 0])
        h = _ffn(h, mods, ffn_w_gate[i].astype(BF16), ffn_w_up[i].astype(BF16),
                 ffn_w_down[i].astype(BF16), ln_g[i, 1], ln_b[i, 1])

    y_prompt = h[:ROWS_CTX].reshape(BATCH, SEQ, D_MODEL)
    y_sample = h[ROWS_CTX:].reshape(DEC_BATCH, DEC_SEQ, D_MODEL)
    return (y_prompt, y_sample, jnp.stack(ckv_states, axis=1), jnp.stack(krope_states, axis=1))
```

```python
import functools
import math

import jax
import jax.numpy as jnp
from jax import lax
from jax.experimental import pallas as pl
from jax.experimental.pallas import tpu as pltpu

F32 = jnp.float32
BF16 = jnp.bfloat16

D_MODEL = 2048
BATCH = 16
SEQ = 256
DEPTH = 4
DEC_BATCH = 2
DEC_SEQ = 4096
PAST_LEN = 512
GRID_W = 64
N_MIXERS = 3
MLA_HEADS = 16
QK_NOPE = 128
QK_ROPE = 64
V_DIM = 128
Q_RANK = 512
KV_RANK = 512
ROPE_THETA = 10000.0
AXIS_ROPE = QK_ROPE // 2
HY_ORDER = 2
HY_DIRS = 2
HY_CONV = 3
HY_BANDS = 16
HY_EMB = 1 + 2 * HY_BANDS
HY_FW = 64
HY_SHIFT = 0.05
FNET_GROUPS = 8
FNET_CG = D_MODEL // FNET_GROUPS
D_FF = -(-8 * D_MODEL // (3 * 256)) * 256
DN_ALPHA = (2 * DEPTH) ** 0.25
LN_EPS = 1e-5
RMS_EPS = 1e-6
N_MOD = 6

ROWS_CTX = BATCH * SEQ
ROWS_LAT = DEC_BATCH * DEC_SEQ
ROWS = ROWS_CTX + ROWS_LAT
N_COND = 1 + DEC_BATCH
COND_PAD = 8
HEAD_W = QK_NOPE + 2 * QK_ROPE
ATT_SCALE = (QK_NOPE + QK_ROPE) ** -0.5

V7X_VMEM_BYTES = 64 * 2 ** 20
VMEM_CAP_BYTES = V7X_VMEM_BYTES * 7 // 8
VMEM_FLOOR_BYTES = 32 * 2 ** 20


def _vmem_limit(pipelined, resident=0, temps=0):
    est = 2 * pipelined + resident + temps
    return int(min(max(est, VMEM_FLOOR_BYTES), VMEM_CAP_BYTES))


def _params(semantics, vmem):
    return pltpu.CompilerParams(dimension_semantics=semantics, vmem_limit_bytes=vmem)


def _nbytes(shape, dtype):
    return math.prod(shape) * jnp.dtype(dtype).itemsize


def _group_of_tile(i, tm):
    n_ctx = ROWS_CTX // tm
    return jnp.where(i < n_ctx, 0, 1 + (i - n_ctx) // (DEC_SEQ // tm))


def _mod_spec(which, tm):
    return pl.BlockSpec((None, 1, D_MODEL),
                        lambda i, *_: (which * COND_PAD + _group_of_tile(i, tm), 0, 0))


def _row_spec(width=D_MODEL):
    return pl.BlockSpec((1, width), lambda *_: (0, 0))


def _modulate(h, shift, scale):
    return h * (1.0 + scale) + shift


def _post_norm(h, delta, g, b):
    z = DN_ALPHA * h + delta
    mu = jnp.mean(z, axis=-1, keepdims=True)
    zc = z - mu
    var = jnp.mean(zc * zc, axis=-1, keepdims=True)
    return zc * lax.rsqrt(var + LN_EPS) * g + b


def _rms_norm(x, g):
    ms = jnp.mean(x * x, axis=-1, keepdims=True)
    return x * lax.rsqrt(ms + RMS_EPS) * g


def _dot(a, b):
    return jnp.dot(a, b, preferred_element_type=F32)


def _modvec_kernel(c_ref, w_ref, b_ref, o_ref):
    a = jax.nn.silu(c_ref[...]).astype(BF16)
    o_ref[...] = _dot(a, w_ref[...].astype(BF16)) + b_ref[...]


def _modulation_vectors(cond, ada_w, ada_b):
    tn = 1024
    n = N_MOD * D_MODEL
    out = pl.pallas_call(
        _modvec_kernel,
        grid=(DEPTH, n // tn),
        in_specs=[pl.BlockSpec((COND_PAD, D_MODEL), lambda l, j: (0, 0)),
                  pl.BlockSpec((None, D_MODEL, tn), lambda l, j: (l, 0, j)),
                  pl.BlockSpec((None, 1, tn), lambda l, j: (l, 0, j))],
        out_specs=pl.BlockSpec((None, COND_PAD, tn), lambda l, j: (l, 0, j)),
        out_shape=jax.ShapeDtypeStruct((DEPTH, COND_PAD, n), F32),
        compiler_params=_params(("parallel", "parallel"),
                                _vmem_limit(_nbytes((D_MODEL, tn), F32),
                                            temps=_nbytes((D_MODEL, tn), BF16))),
        name="modvec",
    )(cond, ada_w, ada_b.reshape(DEPTH, 1, n))
    out = out.reshape(DEPTH, COND_PAD, N_MOD, D_MODEL).transpose(0, 2, 1, 3)
    return out.reshape(DEPTH, N_MOD * COND_PAD, 1, D_MODEL)


def _mod_mm_kernel(h_ref, sh_ref, sc_ref, w_ref, b_ref, o_ref, u_ref):
    @pl.when(pl.program_id(1) == 0)
    def _():
        u_ref[...] = _modulate(h_ref[...], sh_ref[...], sc_ref[...]).astype(BF16)

    o_ref[...] = (_dot(u_ref[...], w_ref[...]) + b_ref[...]).astype(o_ref.dtype)


def _mod_mm(h, mods, w, b, which_shift, out_dtype=F32, tm=512, tn=1024):
    k, n = w.shape
    return pl.pallas_call(
        _mod_mm_kernel,
        grid=(ROWS // tm, n // tn),
        in_specs=[pl.BlockSpec((tm, k), lambda i, j: (i, 0)),
                  _mod_spec(which_shift, tm), _mod_spec(which_shift + 1, tm),
                  pl.BlockSpec((k, tn), lambda i, j: (0, j)),
                  pl.BlockSpec((1, tn), lambda i, j: (0, j))],
        out_specs=pl.BlockSpec((tm, tn), lambda i, j: (i, j)),
        out_shape=jax.ShapeDtypeStruct((ROWS, n), out_dtype),
        scratch_shapes=[pltpu.VMEM((tm, k), BF16)],
        compiler_params=_params(("parallel", "arbitrary"),
                                _vmem_limit(_nbytes((tm, k), F32) + _nbytes((k, tn), BF16)
                                            + _nbytes((tm, tn), F32),
                                            resident=_nbytes((tm, k), BF16),
                                            temps=_nbytes((tm, k), F32))),
        name="mod_mm",
    )(h, mods, mods, w, b.reshape(1, n))


def _mm_postnorm_kernel(a_ref, w_ref, bias_ref, h_ref, gate_ref, g_ref, b_ref, o_ref):
    y = _dot(a_ref[...], w_ref[...]) + bias_ref[...]
    o_ref[...] = _post_norm(h_ref[...], gate_ref[...] * y, g_ref[...], b_ref[...])


def _mm_postnorm(a, w, bias, h, mods, which_gate, ln_g, ln_b, tm=256):
    k = a.shape[1]
    return pl.pallas_call(
        _mm_postnorm_kernel,
        grid=(ROWS // tm,),
        in_specs=[pl.BlockSpec((tm, k), lambda i: (i, 0)),
                  pl.BlockSpec((k, D_MODEL), lambda i: (0, 0)),
                  _row_spec(),
                  pl.BlockSpec((tm, D_MODEL), lambda i: (i, 0)),
                  _mod_spec(which_gate, tm), _row_spec(), _row_spec()],
        out_specs=pl.BlockSpec((tm, D_MODEL), lambda i: (i, 0)),
        out_shape=jax.ShapeDtypeStruct((ROWS, D_MODEL), F32),
        compiler_params=_params(("parallel",),
                                _vmem_limit(_nbytes((tm, k), BF16) + _nbytes((k, D_MODEL), BF16)
                                            + 2 * _nbytes((tm, D_MODEL), F32),
                                            temps=3 * _nbytes((tm, D_MODEL), F32))),
        name="mm_postnorm",
    )(a, w, bias.reshape(1, D_MODEL), h, mods, ln_g.reshape(1, D_MODEL), ln_b.reshape(1, D_MODEL))


def _ffn_kernel(h_ref, sh_ref, sc_ref, gate_ref, g_ref, b_ref, wg_ref, wu_ref, wd_ref,
                o_ref, u_ref):
    f = pl.program_id(1)

    @pl.when(f == 0)
    def _():
        u_ref[...] = _modulate(h_ref[...], sh_ref[...], sc_ref[...]).astype(BF16)
        o_ref[...] = jnp.zeros_like(o_ref)

    u = u_ref[...]
    act = (jax.nn.silu(_dot(u, wg_ref[...])) * _dot(u, wu_ref[...])).astype(BF16)
    o_ref[...] += _dot(act, wd_ref[...])

    @pl.when(f == pl.num_programs(1) - 1)
    def _():
        o_ref[...] = _post_norm(h_ref[...], gate_ref[...] * o_ref[...], g_ref[...], b_ref[...])


def _ffn(h, mods, w_gate, w_up, w_down, ln_g, ln_b, tm=512, tf=512):
    return pl.pallas_call(
        _ffn_kernel,
        grid=(ROWS // tm, D_FF // tf),
        in_specs=[pl.BlockSpec((tm, D_MODEL), lambda i, f: (i, 0)),
                  _mod_spec(3, tm), _mod_spec(4, tm), _mod_spec(5, tm),
                  _row_spec(), _row_spec(),
                  pl.BlockSpec((D_MODEL, tf), lambda i, f: (0, f)),
                  pl.BlockSpec((D_MODEL, tf), lambda i, f: (0, f)),
                  pl.BlockSpec((tf, D_MODEL), lambda i, f: (f, 0))],
        out_specs=pl.BlockSpec((tm, D_MODEL), lambda i, f: (i, 0)),
        out_shape=jax.ShapeDtypeStruct((ROWS, D_MODEL), F32),
        scratch_shapes=[pltpu.VMEM((tm, D_MODEL), BF16)],
        compiler_params=_params(("parallel", "arbitrary"),
                                _vmem_limit(2 * _nbytes((tm, D_MODEL), F32)
                                            + 3 * _nbytes((D_MODEL, tf), BF16),
                                            resident=_nbytes((tm, D_MODEL), BF16),
                                            temps=3 * _nbytes((tm, D_MODEL), F32))),
        name="ffn",
    )(h, mods, mods, mods, ln_g.reshape(1, D_MODEL), ln_b.reshape(1, D_MODEL),
      w_gate, w_up, w_down)


def _mla_down_kernel(h_ref, sh_ref, sc_ref, wdq_ref, wdkv_ref, wkr_ref, qn_ref, kvn_ref,
                     rope_ref, cq_ref, ckv_ref, kr_ref, kr2_ref):
    u = _modulate(h_ref[...], sh_ref[...], sc_ref[...]).astype(BF16)
    cq_ref[...] = _rms_norm(_dot(u, wdq_ref[...]), qn_ref[...]).astype(BF16)
    ckv_ref[...] = _rms_norm(_dot(u, wdkv_ref[...]), kvn_ref[...])
    t = _dot(u, wkr_ref[...])
    kr_ref[...] = t[:, :QK_ROPE]
    v = t * rope_ref[...]
    kr2_ref[...] = (v + pltpu.roll(v, QK_ROPE, 1)).astype(BF16)


def _mla_down(h, mods, w_dq, w_dkv, w_kr2, q_norm, kv_norm, rope_tab, tm=512):
    row = lambda width: pl.BlockSpec((tm, width), lambda i: (i, 0))
    full = lambda shape: pl.BlockSpec(shape, lambda i: (0, 0))
    return pl.pallas_call(
        _mla_down_kernel,
        grid=(ROWS // tm,),
        in_specs=[row(D_MODEL), _mod_spec(0, tm), _mod_spec(1, tm),
                  full((D_MODEL, Q_RANK)), full((D_MODEL, KV_RANK)), full((D_MODEL, 2 * QK_ROPE)),
                  _row_spec(Q_RANK), _row_spec(KV_RANK), row(2 * QK_ROPE)],
        out_specs=[row(Q_RANK), row(KV_RANK), row(QK_ROPE), row(2 * QK_ROPE)],
        out_shape=[jax.ShapeDtypeStruct((ROWS, Q_RANK), BF16),
                   jax.ShapeDtypeStruct((ROWS, KV_RANK), F32),
                   jax.ShapeDtypeStruct((ROWS, QK_ROPE), F32),
                   jax.ShapeDtypeStruct((ROWS, 2 * QK_ROPE), BF16)],
        compiler_params=_params(("parallel",),
                                _vmem_limit(_nbytes((tm, D_MODEL), F32)
                                            + _nbytes((D_MODEL, Q_RANK + KV_RANK + 2 * QK_ROPE), BF16)
                                            + 3 * _nbytes((tm, KV_RANK), F32),
                                            temps=2 * _nbytes((tm, D_MODEL), F32))),
        name="mla_down",
    )(h, mods, mods, w_dq, w_dkv, w_kr2, q_norm.reshape(1, Q_RANK), kv_norm.reshape(1, KV_RANK),
      rope_tab)


NT_DIMS = (((1,), (1,)), ((), ()))


def _q_up_kernel(cq_ref, w_ref, rope_ref, q_ref):
    cq, tab = cq_ref[...], rope_ref[...] * LOG2E_SCALE
    for h in range(MLA_HEADS):
        r = _dot(cq, w_ref[:, h * HEAD_W:(h + 1) * HEAD_W])
        q_ref[h, :, :QK_NOPE] = (r[:, :QK_NOPE] * LOG2E_SCALE).astype(BF16)
        q_ref[h, :, QK_NOPE:] = (r[:, QK_NOPE:] * tab).astype(BF16)


def _q_up(cq, w_q, rope_tab, tm=512):
    return pl.pallas_call(
        _q_up_kernel,
        grid=(ROWS // tm,),
        in_specs=[pl.BlockSpec((tm, Q_RANK), lambda i: (i, 0)),
                  pl.BlockSpec((Q_RANK, MLA_HEADS * HEAD_W), lambda i: (0, 0)),
                  pl.BlockSpec((tm, 2 * QK_ROPE), lambda i: (i, 0))],
        out_specs=pl.BlockSpec((MLA_HEADS, tm, HEAD_W), lambda i: (0, i, 0)),
        out_shape=jax.ShapeDtypeStruct((MLA_HEADS, ROWS, HEAD_W), BF16),
        compiler_params=_params(("parallel",),
                                _vmem_limit(_nbytes((Q_RANK + tm, MLA_HEADS * HEAD_W), BF16))),
        name="q_up",
    )(cq, w_q, rope_tab)


def _kv_expand_kernel(ckv_ref, kr2_ref, wk_ref, wvt_ref, k_ref, vt_ref):
    c = ckv_ref[...].astype(BF16)
    vt_ref[...] = lax.dot_general(wvt_ref[...], c, NT_DIMS,
                                  preferred_element_type=F32).astype(BF16)
    kr2 = kr2_ref[...]
    pair_w = 2 * QK_NOPE
    for g in range(MLA_HEADS // 2):
        r = _dot(c, wk_ref[:, g * pair_w:(g + 1) * pair_w]).astype(BF16)
        for e in range(2):
            k_ref[2 * g + e, :, :QK_NOPE] = r[:, e * QK_NOPE:(e + 1) * QK_NOPE]
            k_ref[2 * g + e, :, QK_NOPE:] = kr2


def _kv_expand(ckv, kr2, w_k, w_vt, tm=512):
    rows = ckv.shape[0]
    return pl.pallas_call(
        _kv_expand_kernel,
        grid=(rows // tm,),
        in_specs=[pl.BlockSpec((tm, KV_RANK), lambda i: (i, 0)),
                  pl.BlockSpec((tm, 2 * QK_ROPE), lambda i: (i, 0)),
                  pl.BlockSpec((KV_RANK, MLA_HEADS * QK_NOPE), lambda i: (0, 0)),
                  pl.BlockSpec((MLA_HEADS * V_DIM, KV_RANK), lambda i: (0, 0))],
        out_specs=[pl.BlockSpec((MLA_HEADS, tm, HEAD_W), lambda i: (0, i, 0)),
                   pl.BlockSpec((MLA_HEADS * V_DIM, tm), lambda i: (0, i))],
        out_shape=[jax.ShapeDtypeStruct((MLA_HEADS, rows, HEAD_W), BF16),
                   jax.ShapeDtypeStruct((MLA_HEADS * V_DIM, rows), BF16)],
        compiler_params=_params(("parallel",),
                                _vmem_limit(_nbytes((tm, MLA_HEADS * (HEAD_W + V_DIM)), BF16)
                                            + 2 * _nbytes((KV_RANK, MLA_HEADS * V_DIM), BF16),
                                            temps=_nbytes((MLA_HEADS * V_DIM, tm), F32))),
        name="kv_expand",
    )(ckv, kr2, w_k, w_vt)


ATT_CHUNK = 512
LOG2E_SCALE = ATT_SCALE * math.log2(math.e)


def _attn_scores(q, k):
    return lax.dot_general(k, q, NT_DIMS, preferred_element_type=F32)


def _attn_values(s, vt, carry):
    m = jnp.max(s, axis=0, keepdims=True)
    if carry is not None:
        m_old, l_old, acc_old = carry
        m = jnp.maximum(m_old, m)
    p = jnp.exp2(s - m)
    l = jnp.sum(p, axis=0, keepdims=True)
    acc = _dot(vt, p.astype(BF16))
    if carry is not None:
        alpha = jnp.exp2(m_old - m)
        l = alpha * l_old + l
        acc = alpha * acc_old + acc
    return m, l, acc


def _attn_ctx_kernel(q_ref, k_ref, vt_ref, o_ref):
    for h in range(MLA_HEADS):
        _, l, acc = _attn_values(_attn_scores(q_ref[h], k_ref[h]),
                                 vt_ref[h * V_DIM:(h + 1) * V_DIM, :], None)
        o_ref[:, h * V_DIM:(h + 1) * V_DIM] = (acc / l).T.astype(BF16)


def _attn_lat_kernel(q_ref, k_ref, vt_ref, kc_ref, vtc_ref, prev_ref, o_ref):
    del prev_ref
    q = q_ref[...]
    n_tok = DEC_SEQ // ATT_CHUNK
    rows = lambda c: slice(c * ATT_CHUNK, (c + 1) * ATT_CHUNK)
    carry = None
    s = _attn_scores(q, k_ref[rows(0), :])
    for c in range(n_tok):
        k_next = k_ref[rows(c + 1), :] if c + 1 < n_tok else kc_ref[...]
        s_next = _attn_scores(q, k_next)
        carry = _attn_values(s, vt_ref[:, rows(c)], carry)
        s = s_next
    _, l, acc = _attn_values(s, vtc_ref[...], carry)
    o_ref[...] = (acc / l).T.astype(BF16)


def _attention(q, k_tok, vt_tok, k_cache, vt_cache, tq=512):
    assert PAST_LEN == ATT_CHUNK
    out_shape = jax.ShapeDtypeStruct((ROWS, MLA_HEADS * V_DIM), BF16)
    o = pl.pallas_call(
        _attn_ctx_kernel,
        grid=(BATCH,),
        in_specs=[pl.BlockSpec((MLA_HEADS, SEQ, HEAD_W), lambda s: (0, s, 0)),
                  pl.BlockSpec((MLA_HEADS, SEQ, HEAD_W), lambda s: (0, s, 0)),
                  pl.BlockSpec((MLA_HEADS * V_DIM, SEQ), lambda s: (0, s))],
        out_specs=pl.BlockSpec((SEQ, MLA_HEADS * V_DIM), lambda s: (s, 0)),
        out_shape=out_shape,
        compiler_params=_params(("parallel",), VMEM_FLOOR_BYTES),
        name="attn_ctx",
    )(q, k_tok, vt_tok)

    lat0 = ROWS_CTX // DEC_SEQ
    q0 = ROWS_CTX // tq
    nq = DEC_SEQ // tq
    return pl.pallas_call(
        _attn_lat_kernel,
        grid=(DEC_BATCH, MLA_HEADS, nq),
        in_specs=[pl.BlockSpec((None, tq, HEAD_W), lambda b, h, i: (h, q0 + b * nq + i, 0)),
                  pl.BlockSpec((None, DEC_SEQ, HEAD_W), lambda b, h, i: (h, lat0 + b, 0)),
                  pl.BlockSpec((V_DIM, DEC_SEQ), lambda b, h, i: (h, lat0 + b)),
                  pl.BlockSpec((None, PAST_LEN, HEAD_W), lambda b, h, i: (h, b, 0)),
                  pl.BlockSpec((V_DIM, PAST_LEN), lambda b, h, i: (h, b)),
                  pl.BlockSpec(memory_space=pl.ANY)],
        out_specs=pl.BlockSpec((tq, V_DIM), lambda b, h, i: (q0 + b * nq + i, h)),
        out_shape=out_shape,
        input_output_aliases={5: 0},
        compiler_params=_params(("parallel", "parallel", "arbitrary"),
                                _vmem_limit(_nbytes((DEC_SEQ + PAST_LEN, HEAD_W + V_DIM), BF16),
                                            temps=8 * _nbytes((ATT_CHUNK, tq), F32))),
        name="attn_lat",
    )(q, k_tok, vt_tok, k_cache, vt_cache, o)


def _mm_kernel(a_ref, b_ref, o_ref, acc_ref):
    k = pl.program_id(2)

    @pl.when(k == 0)
    def _():
        acc_ref[...] = jnp.zeros_like(acc_ref)

    acc_ref[...] += _dot(a_ref[...], b_ref[...])

    @pl.when(k == pl.num_programs(2) - 1)
    def _():
        o_ref[...] = acc_ref[...].astype(o_ref.dtype)


def _mm(a, b, out_dtype=F32, tm=1024, tn=1024, tk=512):
    m, kk = a.shape
    n = b.shape[1]
    tm, tn, tk = min(tm, m), min(tn, n), min(tk, kk)
    return pl.pallas_call(
        _mm_kernel,
        grid=(m // tm, n // tn, kk // tk),
        in_specs=[pl.BlockSpec((tm, tk), lambda i, j, k: (i, k)),
                  pl.BlockSpec((tk, tn), lambda i, j, k: (k, j))],
        out_specs=pl.BlockSpec((tm, tn), lambda i, j, k: (i, j)),
        out_shape=jax.ShapeDtypeStruct((m, n), out_dtype),
        scratch_shapes=[pltpu.VMEM((tm, tn), F32)],
        compiler_params=_params(("parallel", "parallel", "arbitrary"), VMEM_FLOOR_BYTES),
        name="mm",
    )(a, b)


def _seq_tiles(length):
    if length >= 1024:
        return 1024, 512, 1024
    return length, D_MODEL, length


def _dft_fwd_kernel(c_ref, s_ref, z_ref, kr_ref, ks_ref, yr_ref, ys_ref, accr_ref, accs_ref):
    k = pl.program_id(3)

    @pl.when(k == 0)
    def _():
        accr_ref[...] = jnp.zeros_like(accr_ref)
        accs_ref[...] = jnp.zeros_like(accs_ref)

    z = z_ref[...].astype(BF16)
    accr_ref[...] += _dot(c_ref[...], z)
    accs_ref[...] += _dot(s_ref[...], z)

    @pl.when(k == pl.num_programs(3) - 1)
    def _():
        zr, zs, kr, ks = accr_ref[...], accs_ref[...], kr_ref[...], ks_ref[...]
        yr_ref[...] = (zr * kr - zs * ks).astype(BF16)
        ys_ref[...] = (zr * ks + zs * kr).astype(BF16)


def _dft_fwd(tabs, z, z_col0, kr, ks, order, n_seq, length):
    tf, tn, tk = _seq_tiles(length)
    nb, nf, nk = D_MODEL // tn, length // tf, length // tk
    zc0, kc0 = z_col0 // tn, order * nb
    out = jax.ShapeDtypeStruct((n_seq * length, D_MODEL), BF16)
    return pl.pallas_call(
        _dft_fwd_kernel,
        grid=(n_seq, nf, nb, nk),
        in_specs=[pl.BlockSpec((tf, tk), lambda s, f, n, k: (f, k)),
                  pl.BlockSpec((tf, tk), lambda s, f, n, k: (f, k)),
                  pl.BlockSpec((tk, tn), lambda s, f, n, k: (s * nk + k, zc0 + n)),
                  pl.BlockSpec((tf, tn), lambda s, f, n, k: (f, kc0 + n)),
                  pl.BlockSpec((tf, tn), lambda s, f, n, k: (f, kc0 + n))],
        out_specs=[pl.BlockSpec((tf, tn), lambda s, f, n, k: (s * nf + f, n)),
                   pl.BlockSpec((tf, tn), lambda s, f, n, k: (s * nf + f, n))],
        out_shape=[out, out],
        scratch_shapes=[pltpu.VMEM((tf, tn), F32), pltpu.VMEM((tf, tn), F32)],
        compiler_params=_params(("parallel", "parallel", "parallel", "arbitrary"),
                                _vmem_limit(2 * _nbytes((tf, tk), BF16) + _nbytes((tk, tn), F32)
                                            + 2 * _nbytes((tf, tn), F32) + 2 * _nbytes((tf, tn), BF16),
                                            resident=2 * _nbytes((tf, tn), F32),
                                            temps=4 * _nbytes((tf, tn), F32))),
        name="dft_fwd",
    )(tabs["c"], tabs["s"], z, kr, ks)


def _dft_inv_kernel(inv_len, ct_ref, st_ref, yr_ref, ys_ref, z_ref, gate_ref, skip_ref,
                    o_ref, acc_ref):
    k = pl.program_id(3)

    @pl.when(k == 0)
    def _():
        acc_ref[...] = jnp.zeros_like(acc_ref)

    acc_ref[...] += _dot(ct_ref[...], yr_ref[...]) + _dot(st_ref[...], ys_ref[...])

    @pl.when(k == pl.num_programs(3) - 1)
    def _():
        y = acc_ref[...] * inv_len + skip_ref[...] * z_ref[...]
        o_ref[...] = (gate_ref[...] * y).astype(o_ref.dtype)


def _dft_inv(tabs, yr, ys, z, z_col0, gate, gate_col0, skip, order, n_seq, length, out_dtype):
    tt, tn, tk = _seq_tiles(length)
    nb, nt, nk = D_MODEL // tn, length // tt, length // tk
    zc0, gc0 = z_col0 // tn, gate_col0 // tn
    return pl.pallas_call(
        functools.partial(_dft_inv_kernel, 1.0 / length),
        grid=(n_seq, nt, nb, nk),
        in_specs=[pl.BlockSpec((tt, tk), lambda s, t, n, k: (t, k)),
                  pl.BlockSpec((tt, tk), lambda s, t, n, k: (t, k)),
                  pl.BlockSpec((tk, tn), lambda s, t, n, k: (s * nk + k, n)),
                  pl.BlockSpec((tk, tn), lambda s, t, n, k: (s * nk + k, n)),
                  pl.BlockSpec((tt, tn), lambda s, t, n, k: (s * nt + t, zc0 + n)),
                  pl.BlockSpec((tt, tn), lambda s, t, n, k: (s * nt + t, gc0 + n)),
                  pl.BlockSpec((None, 1, tn), lambda s, t, n, k: (order, 0, n))],
        out_specs=pl.BlockSpec((tt, tn), lambda s, t, n, k: (s * nt + t, n)),
        out_shape=jax.ShapeDtypeStruct((n_seq * length, D_MODEL), out_dtype),
        scratch_shapes=[pltpu.VMEM((tt, tn), F32)],
        compiler_params=_params(("parallel", "parallel", "parallel", "arbitrary"),
                                _vmem_limit(2 * _nbytes((tt, tk), BF16) + 2 * _nbytes((tk, tn), BF16)
                                            + 3 * _nbytes((tt, tn), F32),
                                            resident=_nbytes((tt, tn), F32),
                                            temps=3 * _nbytes((tt, tn), F32))),
        name="dft_inv",
    )(tabs["ct"], tabs["st"], yr, ys, z, gate, skip)


TW_LANES = 128


def _lane_tile(x, width):
    return jnp.tile(x, (1, width // x.shape[-1]))


def _ct_stage_a_kernel(n_in, width, *refs):
    x_refs, (fa_ref, cw_ref, sw_ref, br_ref, bi_ref) = refs[:n_in], refs[n_in:]
    fa = fa_ref[...]
    half = fa.shape[0] // 2
    for j in range(cw_ref.shape[0]):
        cols = slice(j * width, (j + 1) * width)
        x = jnp.concatenate([r[:, cols] for r in x_refs], axis=0).astype(BF16)
        a = _dot(fa, x)
        ar, ai = a[:half], a[half:]
        cw, sw = _lane_tile(cw_ref[j], width), _lane_tile(sw_ref[j], width)
        br_ref[:, cols] = (ar * cw + ai * sw).astype(BF16)
        bi_ref[:, cols] = (ai * cw - ar * sw).astype(BF16)


def _ct_stage_a(xs, n_seq, fa, cw, sw, width, tn2):
    n2, n1, _ = cw.shape
    tc = tn2 * width
    in_specs, blocks = [], 0
    for arr, prefix in xs:
        rows_in = arr.shape[-2]
        lead = (None,) * (arr.ndim - 2)
        in_specs.append(pl.BlockSpec(lead + (rows_in, tc), lambda s, i, prefix=prefix: prefix(s) + (0, i)))
        blocks += _nbytes((rows_in, tc), arr.dtype)
    tw = pl.BlockSpec((tn2, n1, TW_LANES), lambda s, i: (i, 0, 0))
    in_specs += [pl.BlockSpec(fa.shape, lambda s, i: (0, 0)), tw, tw]
    out = jax.ShapeDtypeStruct((n_seq, n1, n2 * width), BF16)
    out_spec = pl.BlockSpec((None, n1, tc), lambda s, i: (s, 0, i))
    return pl.pallas_call(
        functools.partial(_ct_stage_a_kernel, len(xs), width),
        grid=(n_seq, n2 // tn2),
        in_specs=in_specs,
        out_specs=[out_spec, out_spec],
        out_shape=[out, out],
        compiler_params=_params(("parallel", "parallel"),
                                _vmem_limit(blocks + 2 * _nbytes((n1, tc), BF16),
                                            temps=6 * _nbytes((2 * n1, width), F32))),
        name="ct_stage_a",
    )(*[arr for arr, _ in xs], fa, cw, sw)


def _ct_mid_kernel(br_ref, bi_ref, pr_ref, pi_ref, mr_ref, mi_ref, fb_ref, fbi_ref, cw_ref, sw_ref,
                   vr_ref, vi_ref):
    fb, fbi = fb_ref[...], fbi_ref[...]
    half = fb.shape[0] // 2
    width = br_ref.shape[-1]
    for j in range(br_ref.shape[0]):
        stack = lambda re_ref, im_ref: jnp.concatenate([re_ref[j], im_ref[j]], axis=0)
        kr = _dot(fb[:half], stack(pr_ref, pi_ref))
        ki = _dot(fb[half:], stack(mr_ref, mi_ref))
        x = _dot(fb, stack(br_ref, bi_ref))
        xr, xi = x[:half], x[half:]
        y = jnp.concatenate([xr * kr - xi * ki, xr * ki + xi * kr], axis=0).astype(BF16)
        v = _dot(fbi, y)
        vr, vi = v[:half], v[half:]
        cw, sw = _lane_tile(cw_ref[j], width), _lane_tile(sw_ref[j], width)
        vr_ref[j] = (vr * cw - vi * sw).astype(BF16)
        vi_ref[j] = (vi * cw + vr * sw).astype(BF16)


def _ct_mid(br, bi, filt, order, fb, fbi, cw, sw, tk1=4, td=1024):
    n1, n2, d = br.shape
    nd = d // td
    data = pl.BlockSpec((tk1, n2, td), lambda i, j: (i, 0, j))
    coef = pl.BlockSpec((tk1, n2, td), lambda i, j: (i, 0, order * nd + j))
    mat = pl.BlockSpec(fb.shape, lambda i, j: (0, 0))
    tw = pl.BlockSpec((tk1, n2, TW_LANES), lambda i, j: (i, 0, 0))
    out = jax.ShapeDtypeStruct((n1, n2, d), BF16)
    return pl.pallas_call(
        _ct_mid_kernel,
        grid=(n1 // tk1, nd),
        in_specs=[data, data, coef, coef, coef, coef, mat, mat, tw, tw],
        out_specs=[data, data],
        out_shape=[out, out],
        compiler_params=_params(("parallel", "parallel"),
                                _vmem_limit(8 * _nbytes((tk1, n2, td), BF16),
                                            temps=10 * _nbytes((2 * n2, td), F32))),
        name="ct_mid",
    )(br, bi, *filt, fb, fbi, cw, sw)


def _ct_inv_a_kernel(scale, width, vr_ref, vi_ref, fai_ref, z0_ref, z1_ref, g0_ref, g1_ref, skip_ref,
                     o_ref):
    fai = fai_ref[...]
    half = fai.shape[0] // 2
    skip = skip_ref[...]
    for j in range(vr_ref.shape[1] // width):
        cols = slice(j * width, (j + 1) * width)
        y = _dot(fai, jnp.concatenate([vr_ref[:, cols], vi_ref[:, cols]], axis=0)) * scale
        o_ref[0, :, cols] = (g0_ref[:, cols] * (y[:half] + skip * z0_ref[:, cols])).astype(o_ref.dtype)
        o_ref[1, :, cols] = (g1_ref[:, cols] * (y[half:] + skip * z1_ref[:, cols])).astype(o_ref.dtype)


def _ct_inv_a(vr, vi, fai, z, z_which, gate, gate_which, skip, order, out_dtype, tn2=4):
    n1, cols = vr.shape
    rows = fai.shape[0] // 2
    tc = tn2 * D_MODEL
    pair = lambda which, b: pl.BlockSpec((None, None, rows, tc), lambda i: (which, b, 0, i))
    return pl.pallas_call(
        functools.partial(_ct_inv_a_kernel, 1.0 / (2 * DEC_SEQ), D_MODEL),
        grid=(cols // tc,),
        in_specs=[pl.BlockSpec((n1, tc), lambda i: (0, i)), pl.BlockSpec((n1, tc), lambda i: (0, i)),
                  pl.BlockSpec(fai.shape, lambda i: (0, 0)),
                  pair(z_which, 0), pair(z_which, 1), pair(gate_which, 0), pair(gate_which, 1),
                  pl.BlockSpec((None, 1, D_MODEL), lambda i: (order, 0, 0))],
        out_specs=pl.BlockSpec((2, rows, tc), lambda i: (0, 0, i)),
        out_shape=jax.ShapeDtypeStruct((2, rows, cols), out_dtype),
        compiler_params=_params(("parallel",),
                                _vmem_limit(2 * _nbytes((n1, tc), BF16) + 6 * _nbytes((rows, tc), F32),
                                            temps=4 * _nbytes((2 * rows, D_MODEL), F32))),
        name="ct_inv_a",
    )(vr, vi, fai, z, z, gate, gate, skip)


def _ct_real_b_kernel(scale, br_ref, bi_ref, fb_ref, o_ref):
    fb = fb_ref[...]
    for j in range(br_ref.shape[0]):
        o_ref[j] = (_dot(fb, jnp.concatenate([br_ref[j], bi_ref[j]], axis=0)) * scale).astype(o_ref.dtype)


def _ct_real_b(br, bi, fb_re, scale, tk1=8):
    n_seq, n1, n2, d = br.shape
    blk = pl.BlockSpec((None, tk1, n2, d), lambda s, i: (s, i, 0, 0))
    return pl.pallas_call(
        functools.partial(_ct_real_b_kernel, scale),
        grid=(n_seq, n1 // tk1),
        in_specs=[blk, blk, pl.BlockSpec(fb_re.shape, lambda s, i: (0, 0))],
        out_specs=blk,
        out_shape=jax.ShapeDtypeStruct(br.shape, BF16),
        compiler_params=_params(("parallel", "parallel"),
                                _vmem_limit(3 * _nbytes((tk1, n2, d), BF16),
                                            temps=2 * _nbytes((n2, d), F32))),
        name="ct_real_b",
    )(br, bi, fb_re)


def _cos_sin(num, den):
    ang = (num % den).astype(F32) * (2.0 * math.pi / den)
    return jnp.cos(ang), jnp.sin(ang)


def _ct_tables(n1, n2):
    i1 = jnp.arange(n1, dtype=jnp.int32)
    i2 = jnp.arange(n2, dtype=jnp.int32)
    c1, s1 = _cos_sin(i1[:, None] * i1[None, :], n1)
    c2, s2 = _cos_sin(i2[:, None] * i2[None, :], n2)
    cw, sw = _cos_sin(i2[:, None] * i1[None, :], n1 * n2)
    lanes = lambda t: jnp.broadcast_to(t[:, :, None], t.shape + (TW_LANES,))
    return {"c1": c1, "s1": s1, "c2": c2, "s2": s2,
            "cw_a": lanes(cw), "sw_a": lanes(sw), "cw_b": lanes(cw.T), "sw_b": lanes(sw.T)}


def _shortconv_kernel(x_ref, w_ref, b_ref, o_ref):
    x = x_ref[...]
    n = x.shape[0]
    row = lax.broadcasted_iota(jnp.int32, x.shape, 0)
    before = jnp.where(row == 0, 0.0, pltpu.roll(x, 1, 0))
    after = jnp.where(row == n - 1, 0.0, pltpu.roll(x, n - 1, 0))
    o_ref[...] = w_ref[0:1, :] * before + w_ref[1:2, :] * x + w_ref[2:3, :] * after + b_ref[...]


def _shortconv(x, w, b, row0, n_seq, length, split):
    width = x.shape[1]
    tc = 256 if length >= 1024 else 2048
    s0 = row0 // length
    rows = n_seq * length
    if split:
        per = D_MODEL // tc
        out_spec = pl.BlockSpec((None, length, tc), lambda s, j: (j // per, s, j % per))
        out_shape = jax.ShapeDtypeStruct((width // D_MODEL, rows, D_MODEL), F32)
    else:
        out_spec = pl.BlockSpec((length, tc), lambda s, j: (s, j))
        out_shape = jax.ShapeDtypeStruct((rows, width), F32)
    return pl.pallas_call(
        _shortconv_kernel,
        grid=(n_seq, width // tc),
        in_specs=[pl.BlockSpec((length, tc), lambda s, j: (s0 + s, j)),
                  pl.BlockSpec((HY_CONV, tc), lambda s, j: (0, j)),
                  pl.BlockSpec((1, tc), lambda s, j: (0, j))],
        out_specs=out_spec,
        out_shape=out_shape,
        compiler_params=_params(("parallel", "parallel"),
                                _vmem_limit(2 * _nbytes((length, tc), F32),
                                            temps=4 * _nbytes((length, tc), F32))),
        name="shortconv",
    )(x, w, b.reshape(1, width))


def _filter_mlp_kernel(feat_ref, t_ref, w1_ref, b1_ref, fr1_ref, w2_ref, b2_ref, fr2_ref,
                       w3_ref, decay_ref, h_ref, ss_ref):
    x = jnp.sin(fr1_ref[...] * (_dot(feat_ref[...].astype(BF16), w1_ref[...].astype(BF16))
                                + b1_ref[...]))
    x = jnp.sin(fr2_ref[...] * (_dot(x.astype(BF16), w2_ref[...].astype(BF16)) + b2_ref[...]))
    h = _dot(x.astype(BF16), w3_ref[...].astype(BF16))
    h = h * (jnp.exp(-t_ref[...] * jnp.exp(decay_ref[...])) + HY_SHIFT)
    h_ref[...] = h

    @pl.when(pl.program_id(0) == 0)
    def _():
        ss_ref[...] = jnp.zeros_like(ss_ref)

    ss_ref[...] += jnp.sum(h * h, axis=0, keepdims=True)


def _filter_combine_kernel(hf_ref, hb_ref, ssf_ref, ssb_ref, a_ref, b_ref):
    norm = lax.rsqrt(ssf_ref[...] + ssb_ref[...] + 1e-12)
    fwd = hf_ref[...] * norm
    bwd = hb_ref[...] * norm
    row = lax.broadcasted_iota(jnp.int32, bwd.shape, 0) + pl.program_id(0) * bwd.shape[0]
    bwd = jnp.where(row == 0, 0.0, bwd)
    a_ref[...] = (fwd + bwd).astype(BF16)
    b_ref[...] = (fwd - bwd).astype(BF16)


def _hyena_filters(length, f_w1, f_b1, f_freq1, f_w2, f_b2, f_freq2, f_w3, log_decay):
    t = jnp.linspace(0.0, 1.0, length, dtype=F32)[:, None]
    t_idx = jnp.arange(length, dtype=F32)[:, None]
    bands = jnp.linspace(1e-4, HY_BANDS - 1, HY_BANDS, dtype=F32)
    w = 2.0 * math.pi * t_idx * bands / length
    feat = jnp.concatenate([t, jnp.cos(w), -jnp.sin(w)], axis=-1)
    emb_pad = 128
    feat = jnp.pad(feat, ((0, 0), (0, emb_pad - HY_EMB)))
    w1 = jnp.pad(f_w1, ((0, emb_pad - HY_EMB), (0, 0)))
    n_all = HY_DIRS * HY_ORDER * D_MODEL
    n_dir = HY_ORDER * D_MODEL
    tm = 256
    full = lambda shape: pl.BlockSpec(shape, lambda i: (0, 0))
    h, ss = pl.pallas_call(
        _filter_mlp_kernel,
        grid=(length // tm,),
        in_specs=[pl.BlockSpec((tm, emb_pad), lambda i: (i, 0)),
                  pl.BlockSpec((tm, 1), lambda i: (i, 0)),
                  full((emb_pad, HY_FW)), full((1, HY_FW)), full((1, HY_FW)),
                  full((HY_FW, HY_FW)), full((1, HY_FW)), full((1, HY_FW)),
                  full((HY_FW, n_all)), full((1, n_all))],
        out_specs=[pl.BlockSpec((tm, n_all), lambda i: (i, 0)), full((1, n_all))],
        out_shape=[jax.ShapeDtypeStruct((length, n_all), F32),
                   jax.ShapeDtypeStruct((1, n_all), F32)],
        compiler_params=_params(("arbitrary",),
                                _vmem_limit(_nbytes((tm, n_all), F32) + _nbytes((HY_FW, n_all), F32),
                                            temps=3 * _nbytes((tm, n_all), F32))),
        name="filter_mlp",
    )(feat, t, w1, f_b1.reshape(1, HY_FW), f_freq1.reshape(1, HY_FW), f_w2,
      f_b2.reshape(1, HY_FW), f_freq2.reshape(1, HY_FW), f_w3,
      log_decay.reshape(1, n_all))

    tn = 1024
    nb = n_dir // tn
    comb = jax.ShapeDtypeStruct((length, n_dir), BF16)
    a, b = pl.pallas_call(
        _filter_combine_kernel,
        grid=(length // tm, nb),
        in_specs=[pl.BlockSpec((tm, tn), lambda i, j: (i, j)),
                  pl.BlockSpec((tm, tn), lambda i, j: (i, nb + j)),
                  pl.BlockSpec((1, tn), lambda i, j: (0, j)),
                  pl.BlockSpec((1, tn), lambda i, j: (0, nb + j))],
        out_specs=[pl.BlockSpec((tm, tn), lambda i, j: (i, j)),
                   pl.BlockSpec((tm, tn), lambda i, j: (i, j))],
        out_shape=[comb, comb],
        compiler_params=_params(("parallel", "parallel"), VMEM_FLOOR_BYTES),
        name="filter_combine",
    )(h, h, ss, ss)
    return a, b


def _cis_product(row_hi, row_lo, period):
    def cis(phase):
        ang = (phase % period).astype(F32) * (2.0 * math.pi / period)
        return jnp.cos(ang)[:, :, None], jnp.sin(ang)[:, :, None]
    (c1, s1), (c0, s0) = cis(row_hi), cis(row_lo)
    c0, s0 = jnp.swapaxes(c0, 1, 2), jnp.swapaxes(s0, 1, 2)
    rows = row_hi.shape[0]
    return ((c1 * c0 - s1 * s0).reshape(rows, -1), (s1 * c0 + c1 * s0).reshape(rows, -1))


def _odd_dft_tables(length):
    split = 1 << (length.bit_length() // 2)
    r = jnp.arange(length, dtype=jnp.int32)[:, None]
    hi = jnp.arange(length // split, dtype=jnp.int32)[None, :] * split
    lo = jnp.arange(split, dtype=jnp.int32)[None, :]
    c, s = _cis_product((2 * r + 1) * hi, (2 * r + 1) * lo, 4 * length)
    ct, st = _cis_product(r * (2 * hi), r * (2 * lo + 1), 4 * length)
    return {"c": c.astype(BF16), "s": s.astype(BF16), "ct": ct.astype(BF16), "st": st.astype(BF16)}


def _dft_tables(length):
    split = 1 << (length.bit_length() // 2)
    r = jnp.arange(length, dtype=jnp.int32)[:, None]
    hi = jnp.arange(length // split, dtype=jnp.int32)[None, :] * split
    lo = jnp.arange(split, dtype=jnp.int32)[None, :]
    c, s = _cis_product(r * hi, r * lo, length)
    return c.astype(BF16), (-s).astype(BF16)


def _hyena_mix_dense(proj, conv_w, conv_b, filt_p, filt_m, skip, row0, n_seq, length):
    tabs = _odd_dft_tables(length)
    kr, ks = _mm(tabs["c"], filt_p), _mm(tabs["s"], filt_m)
    pc = _shortconv(proj, conv_w, conv_b, row0, n_seq, length, split=False)
    v0, x1, x2 = 0, D_MODEL, 2 * D_MODEL
    yr, ys = _dft_fwd(tabs, pc, v0, kr, ks, 0, n_seq, length)
    z1 = _dft_inv(tabs, yr, ys, pc, v0, pc, x1, skip, 0, n_seq, length, F32)
    yr, ys = _dft_fwd(tabs, z1, 0, kr, ks, 1, n_seq, length)
    return _dft_inv(tabs, yr, ys, z1, 0, pc, x2, skip, 1, n_seq, length, BF16)


HY_N1, HY_N2 = 64, 128


def _hyena_mix_pair(proj, conv_w, conv_b, filt_p, filt_m, skip, row0):
    assert DEC_BATCH == 2 and HY_N1 * HY_N2 == 2 * DEC_SEQ
    n1, n2, rows_in = HY_N1, HY_N2, HY_N1 // 2
    t = _ct_tables(n1, n2)
    c_in, s_in = t["c1"][:, :rows_in], t["s1"][:, :rows_in]
    fa = jnp.block([[c_in, s_in], [-s_in, c_in]]).astype(BF16)
    fa_real = jnp.concatenate([c_in, -s_in], axis=0).astype(BF16)
    fb = jnp.block([[t["c2"], t["s2"]], [-t["s2"], t["c2"]]]).astype(BF16)
    fbi = jnp.block([[t["c2"], -t["s2"]], [t["s2"], t["c2"]]]).astype(BF16)
    c_out, s_out = t["c1"][:rows_in], t["s1"][:rows_in]
    fai = jnp.block([[c_out, -s_out], [s_out, c_out]]).astype(BF16)
    first = lambda s: (0,)
    n_filt = HY_ORDER * D_MODEL
    filt = []
    for f in (filt_p, filt_m):
        re, im = _ct_stage_a([(f.reshape(1, rows_in, n2 * n_filt), first)], 1, fa_real,
                             t["cw_a"], t["sw_a"], n_filt, tn2=4)
        filt += [re.reshape(n1, n2, n_filt), im.reshape(n1, n2, n_filt)]

    pc = _shortconv(proj, conv_w, conv_b, row0, DEC_BATCH, DEC_SEQ, split=True)
    pc = pc.reshape(3, DEC_BATCH, rows_in, n2 * D_MODEL)
    z, out_dtypes = pc, (F32, BF16)
    for order in range(HY_ORDER):
        br, bi = _ct_stage_a([(z, lambda s: (0, 0)), (z, lambda s: (0, 1))], 1, fa,
                             t["cw_a"], t["sw_a"], D_MODEL, tn2=8)
        vr, vi = _ct_mid(br.reshape(n1, n2, D_MODEL), bi.reshape(n1, n2, D_MODEL), filt, order,
                         fb, fbi, t["cw_b"], t["sw_b"])
        z = _ct_inv_a(vr.reshape(n1, n2 * D_MODEL), vi.reshape(n1, n2 * D_MODEL), fai, z, 0,
                      pc, 1 + order, skip, order, out_dtypes[order])
        z = z.reshape(1, DEC_BATCH, rows_in, n2 * D_MODEL)
    return z.reshape(ROWS_LAT, D_MODEL)


def _fnet_chan_kernel(h_ref, sh_ref, sc_ref, w_ref, p_ref, q_ref):
    u = _modulate(h_ref[...], sh_ref[...], sc_ref[...]).astype(BF16)
    r = _dot(u, w_ref[...])
    p_ref[...] = r[:, :FNET_CG].astype(BF16)
    q_ref[...] = r[:, FNET_CG:].astype(BF16)


def _fnet_chan(h, mods, w_cs, tm=512):
    mod = lambda which: pl.BlockSpec(
        (None, 1, FNET_CG), lambda i, g: (which * COND_PAD + _group_of_tile(i, tm), 0, g))
    out = jax.ShapeDtypeStruct((ROWS, D_MODEL), BF16)
    return pl.pallas_call(
        _fnet_chan_kernel,
        grid=(ROWS // tm, FNET_GROUPS),
        in_specs=[pl.BlockSpec((tm, FNET_CG), lambda i, g: (i, g)), mod(0), mod(1),
                  pl.BlockSpec((FNET_CG, 2 * FNET_CG), lambda i, g: (0, 0))],
        out_specs=[pl.BlockSpec((tm, FNET_CG), lambda i, g: (i, g)),
                   pl.BlockSpec((tm, FNET_CG), lambda i, g: (i, g))],
        out_shape=[out, out],
        compiler_params=_params(("parallel", "parallel"), VMEM_FLOOR_BYTES),
        name="fnet_chan",
    )(h, mods, mods, w_cs)


def _fnet_pos_kernel(scale, c_ref, ns_ref, p_ref, q_ref, o_ref, acc_ref):
    k = pl.program_id(3)

    @pl.when(k == 0)
    def _():
        acc_ref[...] = jnp.zeros_like(acc_ref)

    acc_ref[...] += _dot(c_ref[...], p_ref[...]) + _dot(ns_ref[...], q_ref[...])

    @pl.when(k == pl.num_programs(3) - 1)
    def _():
        o_ref[...] = (acc_ref[...] * scale).astype(o_ref.dtype)


def _fnet_pos(c_tab, ns_tab, p, q, n_seq, length):
    tt, tn, tk = _seq_tiles(length)
    nb, nt, nk = D_MODEL // tn, length // tt, length // tk
    scale = (length * FNET_CG) ** -0.5
    return pl.pallas_call(
        functools.partial(_fnet_pos_kernel, scale),
        grid=(n_seq, nt, nb, nk),
        in_specs=[pl.BlockSpec((tt, tk), lambda s, t, n, k: (t, k)),
                  pl.BlockSpec((tt, tk), lambda s, t, n, k: (t, k)),
                  pl.BlockSpec((tk, tn), lambda s, t, n, k: (s * nk + k, n)),
                  pl.BlockSpec((tk, tn), lambda s, t, n, k: (s * nk + k, n))],
        out_specs=pl.BlockSpec((tt, tn), lambda s, t, n, k: (s * nt + t, n)),
        out_shape=jax.ShapeDtypeStruct((n_seq * length, D_MODEL), BF16),
        scratch_shapes=[pltpu.VMEM((tt, tn), F32)],
        compiler_params=_params(("parallel", "parallel", "parallel", "arbitrary"),
                                _vmem_limit(2 * _nbytes((tt, tk), BF16) + 2 * _nbytes((tk, tn), BF16)
                                            + _nbytes((tt, tn), BF16),
                                            resident=_nbytes((tt, tn), F32),
                                            temps=2 * _nbytes((tt, tn), F32))),
        name="fnet_pos",
    )(c_tab, ns_tab, p, q)


FN_N1, FN_N2 = 64, 64


def _fnet_pos_factored(p, q):
    assert FN_N1 * FN_N2 == DEC_SEQ and ROWS % DEC_SEQ == 0
    n1, n2 = FN_N1, FN_N2
    t = _ct_tables(n1, n2)
    fa = jnp.block([[t["c1"], -t["s1"]], [-t["s1"], -t["c1"]]]).astype(BF16)
    fb_re = jnp.concatenate([t["c2"], t["s2"]], axis=1).astype(BF16)
    lat0 = ROWS_CTX // DEC_SEQ
    view = lambda x: x.reshape(ROWS // DEC_SEQ, n1, n2 * D_MODEL)
    seq = lambda s: (lat0 + s,)
    br, bi = _ct_stage_a([(view(p), seq), (view(q), seq)], DEC_BATCH, fa, t["cw_a"], t["sw_a"],
                         D_MODEL, tn2=8)
    shape = (DEC_BATCH, n1, n2, D_MODEL)
    f = _ct_real_b(br.reshape(shape), bi.reshape(shape), fb_re, (DEC_SEQ * FNET_CG) ** -0.5)
    return jnp.swapaxes(f, 1, 2).reshape(ROWS_LAT, D_MODEL)


def _rope_table():
    rows = DEC_SEQ // GRID_W
    row = jnp.repeat(jnp.arange(rows), GRID_W).astype(F32)
    col = jnp.tile(jnp.arange(GRID_W), rows).astype(F32)
    inv = ROPE_THETA ** (-jnp.arange(0, AXIS_ROPE, 2, dtype=F32) / AXIS_ROPE)
    ang = jnp.concatenate([row[:, None] * inv, col[:, None] * inv], axis=-1)
    cos = jnp.repeat(jnp.cos(ang), 2, axis=-1)
    sin = jnp.repeat(jnp.sin(ang), 2, axis=-1)
    lat = jnp.tile(jnp.concatenate([cos, sin], axis=-1), (DEC_BATCH, 1))
    ctx = jnp.concatenate([jnp.ones((ROWS_CTX, QK_ROPE), F32), jnp.zeros((ROWS_CTX, QK_ROPE), F32)],
                          axis=-1)
    return jnp.concatenate([ctx, lat], axis=0)


def _pair_rotated(w):
    pairs = w.reshape(w.shape[:-1] + (QK_ROPE // 2, 2))
    return jnp.stack([-pairs[..., 1], pairs[..., 0]], axis=-1).reshape(w.shape)


def kernel(x_prompt, x_sample, c, cache_ckv, cache_krope, c_ctx, ada_w, ada_b, ln_g, ln_b, ffn_w_gate, ffn_w_up, ffn_w_down, mla_w_dq, mla_q_norm, mla_w_uq, mla_w_dkv, mla_kv_norm, mla_w_kr, mla_w_ukv, mla_w_o, hy_w_in, hy_b_in, hy_conv_w, hy_conv_b, hy_f_w1, hy_f_b1, hy_f_freq1, hy_f_w2, hy_f_b2, hy_f_freq2, hy_f_w3, hy_log_decay, hy_skip, hy_w_out, hy_b_out, fn_w_out, fn_b_out):
    assert x_prompt.shape == (BATCH, SEQ, D_MODEL) and x_sample.shape == (DEC_BATCH, DEC_SEQ, D_MODEL)
    assert ROWS_CTX % DEC_SEQ == 0 and SEQ == FNET_CG

    h = jnp.concatenate([x_prompt.reshape(ROWS_CTX, D_MODEL), x_sample.reshape(ROWS_LAT, D_MODEL)])
    cond = jnp.concatenate([c_ctx[None, :], c, jnp.zeros((COND_PAD - N_COND, D_MODEL), F32)])
    mods_all = _modulation_vectors(cond, ada_w, ada_b)
    zero_bias = jnp.zeros((D_MODEL,), F32)
    rope_tab = None
    ckv_states, krope_states = [], []

    for i in range(DEPTH):
        kind, j = i % N_MIXERS, i // N_MIXERS
        mods = mods_all[i]
        if kind == 0:
            if rope_tab is None:
                rope_tab = _rope_table()
            w_kr2 = jnp.concatenate([mla_w_kr[j], _pair_rotated(mla_w_kr[j])], axis=-1).astype(BF16)
            wq = mla_w_uq[j].reshape(Q_RANK, MLA_HEADS, QK_NOPE + QK_ROPE)
            w_q = jnp.concatenate([wq, _pair_rotated(wq[..., QK_NOPE:])], axis=-1)
            w_q = w_q.reshape(Q_RANK, MLA_HEADS * HEAD_W).astype(BF16)
            w_ukv = mla_w_ukv[j].reshape(KV_RANK, MLA_HEADS, QK_NOPE + V_DIM)
            w_k = w_ukv[..., :QK_NOPE].reshape(KV_RANK, MLA_HEADS * QK_NOPE).astype(BF16)
            w_vt = w_ukv[..., QK_NOPE:].reshape(KV_RANK, MLA_HEADS * V_DIM).T.astype(BF16)
            cq, ckv, kr, kr2 = _mla_down(h, mods, mla_w_dq[j].astype(BF16), mla_w_dkv[j].astype(BF16),
                                         w_kr2, mla_q_norm[j], mla_kv_norm[j], rope_tab)
            ckv_states.append(ckv[:ROWS_CTX].reshape(BATCH, SEQ, KV_RANK))
            krope_states.append(kr[:ROWS_CTX].reshape(BATCH, SEQ, QK_ROPE))
            q = _q_up(cq, w_q, rope_tab)
            k_tok, vt_tok = _kv_expand(ckv, kr2, w_k, w_vt)
            kc = cache_krope[:, j].reshape(DEC_BATCH * PAST_LEN, QK_ROPE).astype(BF16)
            k_cache, vt_cache = _kv_expand(cache_ckv[:, j].reshape(DEC_BATCH * PAST_LEN, KV_RANK),
                                           jnp.concatenate([kc, kc], axis=-1), w_k, w_vt)
            o = _attention(q, k_tok, vt_tok, k_cache, vt_cache)
            h = _mm_postnorm(o, mla_w_o[j].astype(BF16), zero_bias, h, mods, 2, ln_g[i, 0], ln_b[i, 0])
        elif kind == 1:
            proj = _mod_mm(h, mods, hy_w_in[j].astype(BF16), hy_b_in[j], 0)
            fp = (hy_f_w1[j], hy_f_b1[j], hy_f_freq1[j], hy_f_w2[j], hy_f_b2[j], hy_f_freq2[j],
                  hy_f_w3[j], hy_log_decay[j])
            skip = hy_skip[j].reshape(HY_ORDER, 1, D_MODEL)
            conv = (hy_conv_w[j], hy_conv_b[j])
            z_ctx = _hyena_mix_dense(proj, *conv, *_hyena_filters(SEQ, *fp), skip, 0, BATCH, SEQ)
            z_lat = _hyena_mix_pair(proj, *conv, *_hyena_filters(DEC_SEQ, *fp), skip, ROWS_CTX)
            h = _mm_postnorm(jnp.concatenate([z_ctx, z_lat]), hy_w_out[j].astype(BF16), hy_b_out[j],
                             h, mods, 2, ln_g[i, 0], ln_b[i, 0])
        else:
            c_ch, ns_ch = _dft_tables(FNET_CG)
            p, q = _fnet_chan(h, mods, jnp.concatenate([c_ch, -ns_ch], axis=-1))
            f_ctx = _fnet_pos(c_ch, ns_ch, p, q, BATCH, SEQ)
            f_lat = _fnet_pos_factored(p, q)
            h = _mm_postnorm(jnp.concatenate([f_ctx, f_lat]), fn_w_out[j].astype(BF16), fn_b_out[j],
                             h, mods, 2, ln_g[i, 0], ln_b[i, 0])
        h = _ffn(h, mods, ffn_w_gate[i].astype(BF16), ffn_w_up[i].astype(BF16),
                 ffn_w_down[i].astype(BF16), ln_g[i, 1], ln_b[i, 1])

    y_prompt = h[:ROWS_CTX].reshape(BATCH, SEQ, D_MODEL)
    y_sample = h[ROWS_CTX:].reshape(DEC_BATCH, DEC_SEQ, D_MODEL)
    return (y_prompt, y_sample, jnp.stack(ckv_states, axis=1), jnp.stack(krope_states, axis=1))
```

```python
import functools
import math

import jax
import jax.numpy as jnp
from jax import lax
from jax.experimental import pallas as pl
from jax.experimental.pallas import tpu as pltpu

F32 = jnp.float32
BF16 = jnp.bfloat16

D_MODEL = 2048
BATCH = 16
SEQ = 256
DEPTH = 4
DEC_BATCH = 2
DEC_SEQ = 4096
PAST_LEN = 512
GRID_W = 64
N_MIXERS = 3
MLA_HEADS = 16
QK_NOPE = 128
QK_ROPE = 64
V_DIM = 128
Q_RANK = 512
KV_RANK = 512
ROPE_THETA = 10000.0
AXIS_ROPE = QK_ROPE // 2
HY_ORDER = 2
HY_DIRS = 2
HY_CONV = 3
HY_BANDS = 16
HY_EMB = 1 + 2 * HY_BANDS
HY_FW = 64
HY_SHIFT = 0.05
FNET_GROUPS = 8
FNET_CG = D_MODEL // FNET_GROUPS
D_FF = -(-8 * D_MODEL // (3 * 256)) * 256
DN_ALPHA = (2 * DEPTH) ** 0.25
LN_EPS = 1e-5
RMS_EPS = 1e-6
N_MOD = 6

ROWS_CTX = BATCH * SEQ
ROWS_LAT = DEC_BATCH * DEC_SEQ
ROWS = ROWS_CTX + ROWS_LAT
N_COND = 1 + DEC_BATCH
COND_PAD = 8
HEAD_W = QK_NOPE + 2 * QK_ROPE
ATT_SCALE = (QK_NOPE + QK_ROPE) ** -0.5

V7X_VMEM_BYTES = 64 * 2 ** 20
VMEM_CAP_BYTES = V7X_VMEM_BYTES * 7 // 8
VMEM_FLOOR_BYTES = 32 * 2 ** 20


def _vmem_limit(pipelined, resident=0, temps=0):
    est = 2 * pipelined + resident + temps
    return int(min(max(est, VMEM_FLOOR_BYTES), VMEM_CAP_BYTES))


def _params(semantics, vmem):
    return pltpu.CompilerParams(dimension_semantics=semantics, vmem_limit_bytes=vmem)


def _nbytes(shape, dtype):
    return math.prod(shape) * jnp.dtype(dtype).itemsize


def _group_of_tile(i, tm):
    n_ctx = ROWS_CTX // tm
    return jnp.where(i < n_ctx, 0, 1 + (i - n_ctx) // (DEC_SEQ // tm))


def _mod_spec(which, tm):
    return pl.BlockSpec((None, 1, D_MODEL),
                        lambda i, *_: (which * COND_PAD + _group_of_tile(i, tm), 0, 0))


def _row_spec(width=D_MODEL):
    return pl.BlockSpec((1, width), lambda *_: (0, 0))


def _modulate(h, shift, scale):
    return h * (1.0 + scale) + shift


def _post_norm(h, delta, g, b):
    z = DN_ALPHA * h + delta
    mu = jnp.mean(z, axis=-1, keepdims=True)
    zc = z - mu
    var = jnp.mean(zc * zc, axis=-1, keepdims=True)
    return zc * lax.rsqrt(var + LN_EPS) * g + b


def _rms_norm(x, g):
    ms = jnp.mean(x * x, axis=-1, keepdims=True)
    return x * lax.rsqrt(ms + RMS_EPS) * g


def _dot(a, b):
    return jnp.dot(a, b, preferred_element_type=F32)


def _modvec_kernel(c_ref, w_ref, b_ref, o_ref):
    a = jax.nn.silu(c_ref[...]).astype(BF16)
    o_ref[...] = _dot(a, w_ref[...].astype(BF16)) + b_ref[...]


def _modulation_vectors(cond, ada_w, ada_b):
    tn = 1024
    n = N_MOD * D_MODEL
    out = pl.pallas_call(
        _modvec_kernel,
        grid=(DEPTH, n // tn),
        in_specs=[pl.BlockSpec((COND_PAD, D_MODEL), lambda l, j: (0, 0)),
                  pl.BlockSpec((None, D_MODEL, tn), lambda l, j: (l, 0, j)),
                  pl.BlockSpec((None, 1, tn), lambda l, j: (l, 0, j))],
        out_specs=pl.BlockSpec((None, COND_PAD, tn), lambda l, j: (l, 0, j)),
        out_shape=jax.ShapeDtypeStruct((DEPTH, COND_PAD, n), F32),
        compiler_params=_params(("parallel", "parallel"),
                                _vmem_limit(_nbytes((D_MODEL, tn), F32),
                                            temps=_nbytes((D_MODEL, tn), BF16))),
        name="modvec",
    )(cond, ada_w, ada_b.reshape(DEPTH, 1, n))
    out = out.reshape(DEPTH, COND_PAD, N_MOD, D_MODEL).transpose(0, 2, 1, 3)
    return out.reshape(DEPTH, N_MOD * COND_PAD, 1, D_MODEL)


def _mod_mm_kernel(h_ref, sh_ref, sc_ref, w_ref, b_ref, o_ref, u_ref):
    @pl.when(pl.program_id(1) == 0)
    def _():
        u_ref[...] = _modulate(h_ref[...], sh_ref[...], sc_ref[...]).astype(BF16)

    o_ref[...] = (_dot(u_ref[...], w_ref[...]) + b_ref[...]).astype(o_ref.dtype)


def _mod_mm(h, mods, w, b, which_shift, out_dtype=F32, tm=512, tn=1024):
    k, n = w.shape
    return pl.pallas_call(
        _mod_mm_kernel,
        grid=(ROWS // tm, n // tn),
        in_specs=[pl.BlockSpec((tm, k), lambda i, j: (i, 0)),
                  _mod_spec(which_shift, tm), _mod_spec(which_shift + 1, tm),
                  pl.BlockSpec((k, tn), lambda i, j: (0, j)),
                  pl.BlockSpec((1, tn), lambda i, j: (0, j))],
        out_specs=pl.BlockSpec((tm, tn), lambda i, j: (i, j)),
        out_shape=jax.ShapeDtypeStruct((ROWS, n), out_dtype),
        scratch_shapes=[pltpu.VMEM((tm, k), BF16)],
        compiler_params=_params(("parallel", "arbitrary"),
                                _vmem_limit(_nbytes((tm, k), F32) + _nbytes((k, tn), BF16)
                                            + _nbytes((tm, tn), F32),
                                            resident=_nbytes((tm, k), BF16),
                                            temps=_nbytes((tm, k), F32))),
        name="mod_mm",
    )(h, mods, mods, w, b.reshape(1, n))


def _mm_postnorm_kernel(a_ref, w_ref, bias_ref, h_ref, gate_ref, g_ref, b_ref, o_ref):
    y = _dot(a_ref[...], w_ref[...]) + bias_ref[...]
    o_ref[...] = _post_norm(h_ref[...], gate_ref[...] * y, g_ref[...], b_ref[...])


def _mm_postnorm(a, w, bias, h, mods, which_gate, ln_g, ln_b, tm=256):
    k = a.shape[1]
    return pl.pallas_call(
        _mm_postnorm_kernel,
        grid=(ROWS // tm,),
        in_specs=[pl.BlockSpec((tm, k), lambda i: (i, 0)),
                  pl.BlockSpec((k, D_MODEL), lambda i: (0, 0)),
                  _row_spec(),
                  pl.BlockSpec((tm, D_MODEL), lambda i: (i, 0)),
                  _mod_spec(which_gate, tm), _row_spec(), _row_spec()],
        out_specs=pl.BlockSpec((tm, D_MODEL), lambda i: (i, 0)),
        out_shape=jax.ShapeDtypeStruct((ROWS, D_MODEL), F32),
        compiler_params=_params(("parallel",),
                                _vmem_limit(_nbytes((tm, k), BF16) + _nbytes((k, D_MODEL), BF16)
                                            + 2 * _nbytes((tm, D_MODEL), F32),
                                            temps=3 * _nbytes((tm, D_MODEL), F32))),
        name="mm_postnorm",
    )(a, w, bias.reshape(1, D_MODEL), h, mods, ln_g.reshape(1, D_MODEL), ln_b.reshape(1, D_MODEL))


def _ffn_kernel(h_ref, sh_ref, sc_ref, gate_ref, g_ref, b_ref, wg_ref, wu_ref, wd_ref,
                o_ref, u_ref):
    f = pl.program_id(1)

    @pl.when(f == 0)
    def _():
        u_ref[...] = _modulate(h_ref[...], sh_ref[...], sc_ref[...]).astype(BF16)
        o_ref[...] = jnp.zeros_like(o_ref)

    u = u_ref[...]
    act = (jax.nn.silu(_dot(u, wg_ref[...])) * _dot(u, wu_ref[...])).astype(BF16)
    o_ref[...] += _dot(act, wd_ref[...])

    @pl.when(f == pl.num_programs(1) - 1)
    def _():
        o_ref[...] = _post_norm(h_ref[...], gate_ref[...] * o_ref[...], g_ref[...], b_ref[...])


def _ffn(h, mods, w_gate, w_up, w_down, ln_g, ln_b, tm=512, tf=512):
    return pl.pallas_call(
        _ffn_kernel,
        grid=(ROWS // tm, D_FF // tf),
        in_specs=[pl.BlockSpec((tm, D_MODEL), lambda i, f: (i, 0)),
                  _mod_spec(3, tm), _mod_spec(4, tm), _mod_spec(5, tm),
                  _row_spec(), _row_spec(),
                  pl.BlockSpec((D_MODEL, tf), lambda i, f: (0, f)),
                  pl.BlockSpec((D_MODEL, tf), lambda i, f: (0, f)),
                  pl.BlockSpec((tf, D_MODEL), lambda i, f: (f, 0))],
        out_specs=pl.BlockSpec((tm, D_MODEL), lambda i, f: (i, 0)),
        out_shape=jax.ShapeDtypeStruct((ROWS, D_MODEL), F32),
        scratch_shapes=[pltpu.VMEM((tm, D_MODEL), BF16)],
        compiler_params=_params(("parallel", "arbitrary"),
                                _vmem_limit(2 * _nbytes((tm, D_MODEL), F32)
                                            + 3 * _nbytes((D_MODEL, tf), BF16),
                                            resident=_nbytes((tm, D_MODEL), BF16),
                                            temps=3 * _nbytes((tm, D_MODEL), F32))),
        name="ffn",
    )(h, mods, mods, mods, ln_g.reshape(1, D_MODEL), ln_b.reshape(1, D_MODEL),
      w_gate, w_up, w_down)


def _mla_down_kernel(h_ref, sh_ref, sc_ref, wdq_ref, wdkv_ref, wkr_ref, qn_ref, kvn_ref,
                     rope_ref, cq_ref, ckv_ref, kr_ref, kr2_ref):
    u = _modulate(h_ref[...], sh_ref[...], sc_ref[...]).astype(BF16)
    cq_ref[...] = _rms_norm(_dot(u, wdq_ref[...]), qn_ref[...]).astype(BF16)
    ckv_ref[...] = _rms_norm(_dot(u, wdkv_ref[...]), kvn_ref[...])
    t = _dot(u, wkr_ref[...])
    kr_ref[...] = t[:, :QK_ROPE]
    v = t * rope_ref[...]
    kr2_ref[...] = (v + pltpu.roll(v, QK_ROPE, 1)).astype(BF16)


def _mla_down(h, mods, w_dq, w_dkv, w_kr2, q_norm, kv_norm, rope_tab, tm=512):
    row = lambda width: pl.BlockSpec((tm, width), lambda i: (i, 0))
    full = lambda shape: pl.BlockSpec(shape, lambda i: (0, 0))
    return pl.pallas_call(
        _mla_down_kernel,
        grid=(ROWS // tm,),
        in_specs=[row(D_MODEL), _mod_spec(0, tm), _mod_spec(1, tm),
                  full((D_MODEL, Q_RANK)), full((D_MODEL, KV_RANK)), full((D_MODEL, 2 * QK_ROPE)),
                  _row_spec(Q_RANK), _row_spec(KV_RANK), row(2 * QK_ROPE)],
        out_specs=[row(Q_RANK), row(KV_RANK), row(QK_ROPE), row(2 * QK_ROPE)],
        out_shape=[jax.ShapeDtypeStruct((ROWS, Q_RANK), BF16),
                   jax.ShapeDtypeStruct((ROWS, KV_RANK), F32),
                   jax.ShapeDtypeStruct((ROWS, QK_ROPE), F32),
                   jax.ShapeDtypeStruct((ROWS, 2 * QK_ROPE), BF16)],
        compiler_params=_params(("parallel",),
                                _vmem_limit(_nbytes((tm, D_MODEL), F32)
                                            + _nbytes((D_MODEL, Q_RANK + KV_RANK + 2 * QK_ROPE), BF16)
                                            + 3 * _nbytes((tm, KV_RANK), F32),
                                            temps=2 * _nbytes((tm, D_MODEL), F32))),
        name="mla_down",
    )(h, mods, mods, w_dq, w_dkv, w_kr2, q_norm.reshape(1, Q_RANK), kv_norm.reshape(1, KV_RANK),
      rope_tab)


NT_DIMS = (((1,), (1,)), ((), ()))


def _q_up_kernel(cq_ref, w_ref, rope_ref, q_ref):
    cq, tab = cq_ref[...], rope_ref[...] * LOG2E_SCALE
    for h in range(MLA_HEADS):
        r = _dot(cq, w_ref[:, h * HEAD_W:(h + 1) * HEAD_W])
        q_ref[h, :, :QK_NOPE] = (r[:, :QK_NOPE] * LOG2E_SCALE).astype(BF16)
        q_ref[h, :, QK_NOPE:] = (r[:, QK_NOPE:] * tab).astype(BF16)


def _q_up(cq, w_q, rope_tab, tm=512):
    return pl.pallas_call(
        _q_up_kernel,
        grid=(ROWS // tm,),
        in_specs=[pl.BlockSpec((tm, Q_RANK), lambda i: (i, 0)),
                  pl.BlockSpec((Q_RANK, MLA_HEADS * HEAD_W), lambda i: (0, 0)),
                  pl.BlockSpec((tm, 2 * QK_ROPE), lambda i: (i, 0))],
        out_specs=pl.BlockSpec((MLA_HEADS, tm, HEAD_W), lambda i: (0, i, 0)),
        out_shape=jax.ShapeDtypeStruct((MLA_HEADS, ROWS, HEAD_W), BF16),
        compiler_params=_params(("parallel",),
                                _vmem_limit(_nbytes((Q_RANK + tm, MLA_HEADS * HEAD_W), BF16))),
        name="q_up",
    )(cq, w_q, rope_tab)


def _kv_expand_kernel(ckv_ref, kr2_ref, wk_ref, wvt_ref, k_ref, vt_ref):
    c = ckv_ref[...].astype(BF16)
    vt_ref[...] = lax.dot_general(wvt_ref[...], c, NT_DIMS,
                                  preferred_element_type=F32).astype(BF16)
    kr2 = kr2_ref[...]
    pair_w = 2 * QK_NOPE
    for g in range(MLA_HEADS // 2):
        r = _dot(c, wk_ref[:, g * pair_w:(g + 1) * pair_w]).astype(BF16)
        for e in range(2):
            k_ref[2 * g + e, :, :QK_NOPE] = r[:, e * QK_NOPE:(e + 1) * QK_NOPE]
            k_ref[2 * g + e, :, QK_NOPE:] = kr2


def _kv_expand(ckv, kr2, w_k, w_vt, tm=512):
    rows = ckv.shape[0]
    return pl.pallas_call(
        _kv_expand_kernel,
        grid=(rows // tm,),
        in_specs=[pl.BlockSpec((tm, KV_RANK), lambda i: (i, 0)),
                  pl.BlockSpec((tm, 2 * QK_ROPE), lambda i: (i, 0)),
                  pl.BlockSpec((KV_RANK, MLA_HEADS * QK_NOPE), lambda i: (0, 0)),
                  pl.BlockSpec((MLA_HEADS * V_DIM, KV_RANK), lambda i: (0, 0))],
        out_specs=[pl.BlockSpec((MLA_HEADS, tm, HEAD_W), lambda i: (0, i, 0)),
                   pl.BlockSpec((MLA_HEADS * V_DIM, tm), lambda i: (0, i))],
        out_shape=[jax.ShapeDtypeStruct((MLA_HEADS, rows, HEAD_W), BF16),
                   jax.ShapeDtypeStruct((MLA_HEADS * V_DIM, rows), BF16)],
        compiler_params=_params(("parallel",),
                                _vmem_limit(_nbytes((tm, MLA_HEADS * (HEAD_W + V_DIM)), BF16)
                                            + 2 * _nbytes((KV_RANK, MLA_HEADS * V_DIM), BF16),
                                            temps=_nbytes((MLA_HEADS * V_DIM, tm), F32))),
        name="kv_expand",
    )(ckv, kr2, w_k, w_vt)


ATT_CHUNK = 512
LOG2E_SCALE = ATT_SCALE * math.log2(math.e)


def _attn_scores(q, k):
    return lax.dot_general(k, q, NT_DIMS, preferred_element_type=F32)


def _attn_values(s, vt, carry):
    m = jnp.max(s, axis=0, keepdims=True)
    if carry is not None:
        m_old, l_old, acc_old = carry
        m = jnp.maximum(m_old, m)
    p = jnp.exp2(s - m)
    l = jnp.sum(p, axis=0, keepdims=True)
    acc = _dot(vt, p.astype(BF16))
    if carry is not None:
        alpha = jnp.exp2(m_old - m)
        l = alpha * l_old + l
        acc = alpha * acc_old + acc
    return m, l, acc


def _attn_ctx_kernel(q_ref, k_ref, vt_ref, o_ref):
    for h in range(MLA_HEADS):
        _, l, acc = _attn_values(_attn_scores(q_ref[h], k_ref[h]),
                                 vt_ref[h * V_DIM:(h + 1) * V_DIM, :], None)
        o_ref[:, h * V_DIM:(h + 1) * V_DIM] = (acc / l).T.astype(BF16)


def _attn_lat_kernel(q_ref, k_ref, vt_ref, kc_ref, vtc_ref, prev_ref, o_ref):
    del prev_ref
    q = q_ref[...]
    n_tok = DEC_SEQ // ATT_CHUNK
    rows = lambda c: slice(c * ATT_CHUNK, (c + 1) * ATT_CHUNK)
    carry = None
    s = _attn_scores(q, k_ref[rows(0), :])
    for c in range(n_tok):
        k_next = k_ref[rows(c + 1), :] if c + 1 < n_tok else kc_ref[...]
        s_next = _attn_scores(q, k_next)
        carry = _attn_values(s, vt_ref[:, rows(c)], carry)
        s = s_next
    _, l, acc = _attn_values(s, vtc_ref[...], carry)
    o_ref[...] = (acc / l).T.astype(BF16)


def _attention(q, k_tok, vt_tok, k_cache, vt_cache, tq=512):
    assert PAST_LEN == ATT_CHUNK
    out_shape = jax.ShapeDtypeStruct((ROWS, MLA_HEADS * V_DIM), BF16)
    o = pl.pallas_call(
        _attn_ctx_kernel,
        grid=(BATCH,),
        in_specs=[pl.BlockSpec((MLA_HEADS, SEQ, HEAD_W), lambda s: (0, s, 0)),
                  pl.BlockSpec((MLA_HEADS, SEQ, HEAD_W), lambda s: (0, s, 0)),
                  pl.BlockSpec((MLA_HEADS * V_DIM, SEQ), lambda s: (0, s))],
        out_specs=pl.BlockSpec((SEQ, MLA_HEADS * V_DIM), lambda s: (s, 0)),
        out_shape=out_shape,
        compiler_params=_params(("parallel",), VMEM_FLOOR_BYTES),
        name="attn_ctx",
    )(q, k_tok, vt_tok)

    lat0 = ROWS_CTX // DEC_SEQ
    q0 = ROWS_CTX // tq
    nq = DEC_SEQ // tq
    return pl.pallas_call(
        _attn_lat_kernel,
        grid=(DEC_BATCH, MLA_HEADS, nq),
        in_specs=[pl.BlockSpec((None, tq, HEAD_W), lambda b, h, i: (h, q0 + b * nq + i, 0)),
                  pl.BlockSpec((None, DEC_SEQ, HEAD_W), lambda b, h, i: (h, lat0 + b, 0)),
                  pl.BlockSpec((V_DIM, DEC_SEQ), lambda b, h, i: (h, lat0 + b)),
                  pl.BlockSpec((None, PAST_LEN, HEAD_W), lambda b, h, i: (h, b, 0)),
                  pl.BlockSpec((V_DIM, PAST_LEN), lambda b, h, i: (h, b)),
                  pl.BlockSpec(memory_space=pl.ANY)],
        out_specs=pl.BlockSpec((tq, V_DIM), lambda b, h, i: (q0 + b * nq + i, h)),
        out_shape=out_shape,
        input_output_aliases={5: 0},
        compiler_params=_params(("parallel", "parallel", "arbitrary"),
                                _vmem_limit(_nbytes((DEC_SEQ + PAST_LEN, HEAD_W + V_DIM), BF16),
                                            temps=8 * _nbytes((ATT_CHUNK, tq), F32))),
        name="attn_lat",
    )(q, k_tok, vt_tok, k_cache, vt_cache, o)


def _mm_kernel(a_ref, b_ref, o_ref, acc_ref):
    k = pl.program_id(2)

    @pl.when(k == 0)
    def _():
        acc_ref[...] = jnp.zeros_like(acc_ref)

    acc_ref[...] += _dot(a_ref[...], b_ref[...].astype(BF16))

    @pl.when(k == pl.num_programs(2) - 1)
    def _():
        o_ref[...] = acc_ref[...].astype(o_ref.dtype)


def _mm(a, b, out_dtype=F32, tm=1024, tn=1024, tk=512):
    m, kk = a.shape
    n = b.shape[1]
    tm, tn, tk = min(tm, m), min(tn, n), min(tk, kk)
    return pl.pallas_call(
        _mm_kernel,
        grid=(m // tm, n // tn, kk // tk),
        in_specs=[pl.BlockSpec((tm, tk), lambda i, j, k: (i, k)),
                  pl.BlockSpec((tk, tn), lambda i, j, k: (k, j))],
        out_specs=pl.BlockSpec((tm, tn), lambda i, j, k: (i, j)),
        out_shape=jax.ShapeDtypeStruct((m, n), out_dtype),
        scratch_shapes=[pltpu.VMEM((tm, tn), F32)],
        compiler_params=_params(("parallel", "parallel", "arbitrary"), VMEM_FLOOR_BYTES),
        name="mm",
    )(a, b)


def _seq_tiles(length):
    if length >= 1024:
        return 1024, 512, 1024
    return length, D_MODEL, length


def _dft_fwd_kernel(c_ref, s_ref, z_ref, kr_ref, ks_ref, yr_ref, ys_ref, accr_ref, accs_ref):
    k = pl.program_id(3)

    @pl.when(k == 0)
    def _():
        accr_ref[...] = jnp.zeros_like(accr_ref)
        accs_ref[...] = jnp.zeros_like(accs_ref)

    z = z_ref[...].astype(BF16)
    accr_ref[...] += _dot(c_ref[...], z)
    accs_ref[...] += _dot(s_ref[...], z)

    @pl.when(k == pl.num_programs(3) - 1)
    def _():
        zr, zs, kr, ks = accr_ref[...], accs_ref[...], kr_ref[...], ks_ref[...]
        yr_ref[...] = (zr * kr - zs * ks).astype(BF16)
        ys_ref[...] = (zr * ks + zs * kr).astype(BF16)


def _dft_fwd(tabs, z, z_col0, kr, ks, order, n_seq, length):
    tf, tn, tk = _seq_tiles(length)
    nb, nf, nk = D_MODEL // tn, length // tf, length // tk
    zc0, kc0 = z_col0 // tn, order * nb
    out = jax.ShapeDtypeStruct((n_seq * length, D_MODEL), BF16)
    return pl.pallas_call(
        _dft_fwd_kernel,
        grid=(n_seq, nf, nb, nk),
        in_specs=[pl.BlockSpec((tf, tk), lambda s, f, n, k: (f, k)),
                  pl.BlockSpec((tf, tk), lambda s, f, n, k: (f, k)),
                  pl.BlockSpec((tk, tn), lambda s, f, n, k: (s * nk + k, zc0 + n)),
                  pl.BlockSpec((tf, tn), lambda s, f, n, k: (f, kc0 + n)),
                  pl.BlockSpec((tf, tn), lambda s, f, n, k: (f, kc0 + n))],
        out_specs=[pl.BlockSpec((tf, tn), lambda s, f, n, k: (s * nf + f, n)),
                   pl.BlockSpec((tf, tn), lambda s, f, n, k: (s * nf + f, n))],
        out_shape=[out, out],
        scratch_shapes=[pltpu.VMEM((tf, tn), F32), pltpu.VMEM((tf, tn), F32)],
        compiler_params=_params(("parallel", "parallel", "parallel", "arbitrary"),
                                _vmem_limit(2 * _nbytes((tf, tk), BF16) + _nbytes((tk, tn), F32)
                                            + 2 * _nbytes((tf, tn), F32) + 2 * _nbytes((tf, tn), BF16),
                                            resident=2 * _nbytes((tf, tn), F32),
                                            temps=4 * _nbytes((tf, tn), F32))),
        name="dft_fwd",
    )(tabs["c"], tabs["s"], z, kr, ks)


def _dft_inv_kernel(inv_len, ct_ref, st_ref, yr_ref, ys_ref, z_ref, gate_ref, skip_ref,
                    o_ref, acc_ref):
    k = pl.program_id(3)

    @pl.when(k == 0)
    def _():
        acc_ref[...] = jnp.zeros_like(acc_ref)

    acc_ref[...] += _dot(ct_ref[...], yr_ref[...]) + _dot(st_ref[...], ys_ref[...])

    @pl.when(k == pl.num_programs(3) - 1)
    def _():
        y = acc_ref[...] * inv_len + skip_ref[...] * z_ref[...]
        o_ref[...] = (gate_ref[...] * y).astype(o_ref.dtype)


def _dft_inv(tabs, yr, ys, z, z_col0, gate, gate_col0, skip, order, n_seq, length, out_dtype):
    tt, tn, tk = _seq_tiles(length)
    nb, nt, nk = D_MODEL // tn, length // tt, length // tk
    zc0, gc0 = z_col0 // tn, gate_col0 // tn
    return pl.pallas_call(
        functools.partial(_dft_inv_kernel, 1.0 / length),
        grid=(n_seq, nt, nb, nk),
        in_specs=[pl.BlockSpec((tt, tk), lambda s, t, n, k: (t, k)),
                  pl.BlockSpec((tt, tk), lambda s, t, n, k: (t, k)),
                  pl.BlockSpec((tk, tn), lambda s, t, n, k: (s * nk + k, n)),
                  pl.BlockSpec((tk, tn), lambda s, t, n, k: (s * nk + k, n)),
                  pl.BlockSpec((tt, tn), lambda s, t, n, k: (s * nt + t, zc0 + n)),
                  pl.BlockSpec((tt, tn), lambda s, t, n, k: (s * nt + t, gc0 + n)),
                  pl.BlockSpec((None, 1, tn), lambda s, t, n, k: (order, 0, n))],
        out_specs=pl.BlockSpec((tt, tn), lambda s, t, n, k: (s * nt + t, n)),
        out_shape=jax.ShapeDtypeStruct((n_seq * length, D_MODEL), out_dtype),
        scratch_shapes=[pltpu.VMEM((tt, tn), F32)],
        compiler_params=_params(("parallel", "parallel", "parallel", "arbitrary"),
                                _vmem_limit(2 * _nbytes((tt, tk), BF16) + 2 * _nbytes((tk, tn), BF16)
                                            + 3 * _nbytes((tt, tn), F32),
                                            resident=_nbytes((tt, tn), F32),
                                            temps=3 * _nbytes((tt, tn), F32))),
        name="dft_inv",
    )(tabs["ct"], tabs["st"], yr, ys, z, gate, skip)


TW_LANES = 128
CT_ROWS = 16
CT_COLS = 1024


def _lane_tile(x, width):
    return jnp.tile(x, (1, width // x.shape[-1]))


def _ct_stage_a_kernel(n_in, *refs):
    x_refs, (fa_ref, cw_ref, sw_ref, br_ref, bi_ref, sr_ref, si_ref) = refs[:n_in], refs[n_in:]
    fa = fa_ref[...]
    half = fa.shape[0] // 2
    width = br_ref.shape[-1]
    for j in range(cw_ref.shape[0]):
        x = jnp.concatenate([r[:, j, :] for r in x_refs], axis=0).astype(BF16)
        a = _dot(fa, x)
        ar, ai = a[:half], a[half:]
        cw, sw = _lane_tile(cw_ref[j], width), _lane_tile(sw_ref[j], width)
        sr_ref[:, j, :] = ar * cw + ai * sw
        si_ref[:, j, :] = ai * cw - ar * sw
    br_ref[...] = sr_ref[...].astype(BF16)
    bi_ref[...] = si_ref[...].astype(BF16)


def _ct_stage_a(xs, n_seq, fa, cw, sw):
    n2, n1, _ = cw.shape
    width = xs[0][0].shape[-1]
    tn2, tw = CT_ROWS, CT_COLS
    in_specs, blocks = [], 0
    for arr, prefix in xs:
        rows_in = arr.shape[-3]
        lead = (None,) * (arr.ndim - 3)
        in_specs.append(pl.BlockSpec(lead + (rows_in, tn2, tw),
                                     lambda s, i, c, prefix=prefix: prefix(s) + (0, i, c)))
        blocks += _nbytes((rows_in, tn2, tw), arr.dtype)
    twid = pl.BlockSpec((tn2, n1, TW_LANES), lambda s, i, c: (i, 0, 0))
    in_specs += [pl.BlockSpec(fa.shape, lambda s, i, c: (0, 0)), twid, twid]
    out = jax.ShapeDtypeStruct((n_seq, n1, n2, width), BF16)
    out_spec = pl.BlockSpec((None, n1, tn2, tw), lambda s, i, c: (s, 0, i, c))
    stage = pltpu.VMEM((n1, tn2, tw), F32)
    return pl.pallas_call(
        functools.partial(_ct_stage_a_kernel, len(xs)),
        grid=(n_seq, n2 // tn2, width // tw),
        in_specs=in_specs,
        out_specs=[out_spec, out_spec],
        out_shape=[out, out],
        scratch_shapes=[stage, stage],
        compiler_params=_params(("parallel", "parallel", "parallel"),
                                _vmem_limit(blocks + 2 * _nbytes((n1, tn2, tw), BF16),
                                            resident=2 * _nbytes((n1, tn2, tw), F32),
                                            temps=6 * _nbytes((2 * n1, tw), F32))),
        name="ct_stage_a",
    )(*[arr for arr, _ in xs], fa, cw, sw)


def _ct_mid_kernel(br_ref, bi_ref, pr_ref, pi_ref, mr_ref, mi_ref, fb_ref, fbi_ref, cw_ref, sw_ref,
                   vr_ref, vi_ref):
    fb, fbi = fb_ref[...], fbi_ref[...]
    half = fb.shape[0] // 2
    width = br_ref.shape[-1]
    for j in range(br_ref.shape[0]):
        stack = lambda re_ref, im_ref: jnp.concatenate([re_ref[j], im_ref[j]], axis=0)
        kr = _dot(fb[:half], stack(pr_ref, pi_ref))
        ki = _dot(fb[half:], stack(mr_ref, mi_ref))
        x = _dot(fb, stack(br_ref, bi_ref))
        xr, xi = x[:half], x[half:]
        y = jnp.concatenate([xr * kr - xi * ki, xr * ki + xi * kr], axis=0).astype(BF16)
        v = _dot(fbi, y)
        vr, vi = v[:half], v[half:]
        cw, sw = _lane_tile(cw_ref[j], width), _lane_tile(sw_ref[j], width)
        vr_ref[j] = (vr * cw - vi * sw).astype(BF16)
        vi_ref[j] = (vi * cw + vr * sw).astype(BF16)


def _ct_mid(br, bi, filt, order, fb, fbi, cw, sw, tk1=4, td=1024):
    n1, n2, d = br.shape
    nd = d // td
    data = pl.BlockSpec((tk1, n2, td), lambda i, j: (i, 0, j))
    coef = pl.BlockSpec((tk1, n2, td), lambda i, j: (i, 0, order * nd + j))
    mat = pl.BlockSpec(fb.shape, lambda i, j: (0, 0))
    tw = pl.BlockSpec((tk1, n2, TW_LANES), lambda i, j: (i, 0, 0))
    out = jax.ShapeDtypeStruct((n1, n2, d), BF16)
    return pl.pallas_call(
        _ct_mid_kernel,
        grid=(n1 // tk1, nd),
        in_specs=[data, data, coef, coef, coef, coef, mat, mat, tw, tw],
        out_specs=[data, data],
        out_shape=[out, out],
        compiler_params=_params(("parallel", "parallel"),
                                _vmem_limit(8 * _nbytes((tk1, n2, td), BF16),
                                            temps=10 * _nbytes((2 * n2, td), F32))),
        name="ct_mid",
    )(br, bi, *filt, fb, fbi, cw, sw)


def _ct_inv_a_kernel(scale, vr_ref, vi_ref, fai_ref, z0_ref, z1_ref, g0_ref, g1_ref, skip_ref,
                     o_ref, sr_ref, si_ref, so_ref):
    fai = fai_ref[...]
    half = fai.shape[0] // 2
    skip = skip_ref[...]
    sr_ref[...] = vr_ref[...].astype(F32)
    si_ref[...] = vi_ref[...].astype(F32)
    for j in range(vr_ref.shape[1]):
        v = jnp.concatenate([sr_ref[:, j, :], si_ref[:, j, :]], axis=0).astype(BF16)
        y = _dot(fai, v) * scale
        so_ref[0, :, j, :] = g0_ref[:, j, :] * (y[:half] + skip * z0_ref[:, j, :])
        so_ref[1, :, j, :] = g1_ref[:, j, :] * (y[half:] + skip * z1_ref[:, j, :])
    o_ref[...] = so_ref[...].astype(o_ref.dtype)


def _ct_inv_a(vr, vi, fai, z, z_which, gate, gate_which, skip, order, out_dtype):
    n1, n2, d = vr.shape
    rows = fai.shape[0] // 2
    tn2, tw = CT_ROWS, CT_COLS // 2
    nc = d // tw
    spec = pl.BlockSpec((n1, tn2, tw), lambda i, c: (0, i, c))
    pair = lambda which, b: pl.BlockSpec((None, None, rows, tn2, tw), lambda i, c: (which, b, 0, i, c))
    return pl.pallas_call(
        functools.partial(_ct_inv_a_kernel, 1.0 / (n1 * n2)),
        grid=(n2 // tn2, nc),
        in_specs=[spec, spec, pl.BlockSpec(fai.shape, lambda i, c: (0, 0)),
                  pair(z_which, 0), pair(z_which, 1), pair(gate_which, 0), pair(gate_which, 1),
                  pl.BlockSpec((None, 1, tw), lambda i, c: (order, 0, c))],
        out_specs=pl.BlockSpec((2, rows, tn2, tw), lambda i, c: (0, 0, i, c)),
        out_shape=jax.ShapeDtypeStruct((2, rows, n2, d), out_dtype),
        scratch_shapes=[pltpu.VMEM((n1, tn2, tw), F32), pltpu.VMEM((n1, tn2, tw), F32),
                        pltpu.VMEM((2, rows, tn2, tw), F32)],
        compiler_params=_params(("parallel", "parallel"),
                                _vmem_limit(2 * _nbytes((n1, tn2, tw), BF16)
                                            + 6 * _nbytes((rows, tn2, tw), F32),
                                            resident=3 * _nbytes((n1, tn2, tw), F32),
                                            temps=4 * _nbytes((n1, tn2, tw), F32))),
        name="ct_inv_a",
    )(vr, vi, fai, z, z, gate, gate, skip)


def _ct_real_b_kernel(scale, br_ref, bi_ref, fb_ref, o_ref, so_ref):
    fb = fb_ref[...]
    for j in range(br_ref.shape[0]):
        so_ref[:, j, :] = _dot(fb, jnp.concatenate([br_ref[j], bi_ref[j]], axis=0)) * scale
    o_ref[...] = so_ref[...].astype(o_ref.dtype)


def _ct_real_b(br, bi, fb_re, scale):
    n_seq, n1, n2, d = br.shape
    tk1, tw = CT_ROWS, CT_COLS
    blk = pl.BlockSpec((None, tk1, n2, tw), lambda s, i, c: (s, i, 0, c))
    return pl.pallas_call(
        functools.partial(_ct_real_b_kernel, scale),
        grid=(n_seq, n1 // tk1, d // tw),
        in_specs=[blk, blk, pl.BlockSpec(fb_re.shape, lambda s, i, c: (0, 0))],
        out_specs=pl.BlockSpec((None, n2, tk1, tw), lambda s, i, c: (s, 0, i, c)),
        out_shape=jax.ShapeDtypeStruct((n_seq, n2, n1, d), BF16),
        scratch_shapes=[pltpu.VMEM((n2, tk1, tw), F32)],
        compiler_params=_params(("parallel", "parallel", "parallel"),
                                _vmem_limit(3 * _nbytes((tk1, n2, tw), BF16),
                                            resident=_nbytes((n2, tk1, tw), F32),
                                            temps=2 * _nbytes((n2, tk1, tw), F32))),
        name="ct_real_b",
    )(br, bi, fb_re)


def _cos_sin(num, den):
    ang = (num % den).astype(F32) * (2.0 * math.pi / den)
    return jnp.cos(ang), jnp.sin(ang)


def _ct_tables(n1, n2):
    i1 = jnp.arange(n1, dtype=jnp.int32)
    i2 = jnp.arange(n2, dtype=jnp.int32)
    c1, s1 = _cos_sin(i1[:, None] * i1[None, :], n1)
    c2, s2 = _cos_sin(i2[:, None] * i2[None, :], n2)
    cw, sw = _cos_sin(i2[:, None] * i1[None, :], n1 * n2)
    lanes = lambda t: jnp.broadcast_to(t[:, :, None], t.shape + (TW_LANES,))
    return {"c1": c1, "s1": s1, "c2": c2, "s2": s2,
            "cw_a": lanes(cw), "sw_a": lanes(sw), "cw_b": lanes(cw.T), "sw_b": lanes(sw.T)}


def _shortconv_kernel(x_ref, w_ref, b_ref, o_ref):
    x = x_ref[...]
    n = x.shape[0]
    row = lax.broadcasted_iota(jnp.int32, x.shape, 0)
    before = jnp.where(row == 0, 0.0, pltpu.roll(x, 1, 0))
    after = jnp.where(row == n - 1, 0.0, pltpu.roll(x, n - 1, 0))
    o_ref[...] = w_ref[0:1, :] * before + w_ref[1:2, :] * x + w_ref[2:3, :] * after + b_ref[...]


def _shortconv(x, w, b, row0, n_seq, length, split):
    width = x.shape[1]
    tc = 256 if length >= 1024 else 2048
    s0 = row0 // length
    rows = n_seq * length
    if split:
        per = D_MODEL // tc
        out_spec = pl.BlockSpec((None, length, tc), lambda s, j: (j // per, s, j % per))
        out_shape = jax.ShapeDtypeStruct((width // D_MODEL, rows, D_MODEL), F32)
    else:
        out_spec = pl.BlockSpec((length, tc), lambda s, j: (s, j))
        out_shape = jax.ShapeDtypeStruct((rows, width), F32)
    return pl.pallas_call(
        _shortconv_kernel,
        grid=(n_seq, width // tc),
        in_specs=[pl.BlockSpec((length, tc), lambda s, j: (s0 + s, j)),
                  pl.BlockSpec((HY_CONV, tc), lambda s, j: (0, j)),
                  pl.BlockSpec((1, tc), lambda s, j: (0, j))],
        out_specs=out_spec,
        out_shape=out_shape,
        compiler_params=_params(("parallel", "parallel"),
                                _vmem_limit(2 * _nbytes((length, tc), F32),
                                            temps=4 * _nbytes((length, tc), F32))),
        name="shortconv",
    )(x, w, b.reshape(1, width))


def _filter_mlp_kernel(feat_ref, t_ref, w1_ref, b1_ref, fr1_ref, w2_ref, b2_ref, fr2_ref,
                       w3_ref, decay_ref, h_ref, ss_ref):
    x = jnp.sin(fr1_ref[...] * (_dot(feat_ref[...].astype(BF16), w1_ref[...].astype(BF16))
                                + b1_ref[...]))
    x = jnp.sin(fr2_ref[...] * (_dot(x.astype(BF16), w2_ref[...].astype(BF16)) + b2_ref[...]))
    h = _dot(x.astype(BF16), w3_ref[...].astype(BF16))
    h = h * (jnp.exp(-t_ref[...] * jnp.exp(decay_ref[...])) + HY_SHIFT)
    h_ref[...] = h

    @pl.when(pl.program_id(0) == 0)
    def _():
        ss_ref[...] = jnp.zeros_like(ss_ref)

    ss_ref[...] += jnp.sum(h * h, axis=0, keepdims=True)


def _filter_combine_kernel(hf_ref, hb_ref, ssf_ref, ssb_ref, a_ref, b_ref):
    norm = lax.rsqrt(ssf_ref[...] + ssb_ref[...] + 1e-12)
    fwd = hf_ref[...] * norm
    bwd = hb_ref[...] * norm
    row = lax.broadcasted_iota(jnp.int32, bwd.shape, 0) + pl.program_id(0) * bwd.shape[0]
    bwd = jnp.where(row == 0, 0.0, bwd)
    a_ref[...] = fwd + bwd
    b_ref[...] = fwd - bwd


def _hyena_filters(length, f_w1, f_b1, f_freq1, f_w2, f_b2, f_freq2, f_w3, log_decay):
    t = jnp.linspace(0.0, 1.0, length, dtype=F32)[:, None]
    t_idx = jnp.arange(length, dtype=F32)[:, None]
    bands = jnp.linspace(1e-4, HY_BANDS - 1, HY_BANDS, dtype=F32)
    w = 2.0 * math.pi * t_idx * bands / length
    feat = jnp.concatenate([t, jnp.cos(w), -jnp.sin(w)], axis=-1)
    emb_pad = 128
    feat = jnp.pad(feat, ((0, 0), (0, emb_pad - HY_EMB)))
    w1 = jnp.pad(f_w1, ((0, emb_pad - HY_EMB), (0, 0)))
    n_all = HY_DIRS * HY_ORDER * D_MODEL
    n_dir = HY_ORDER * D_MODEL
    tm = 256
    full = lambda shape: pl.BlockSpec(shape, lambda i: (0, 0))
    h, ss = pl.pallas_call(
        _filter_mlp_kernel,
        grid=(length // tm,),
        in_specs=[pl.BlockSpec((tm, emb_pad), lambda i: (i, 0)),
                  pl.BlockSpec((tm, 1), lambda i: (i, 0)),
                  full((emb_pad, HY_FW)), full((1, HY_FW)), full((1, HY_FW)),
                  full((HY_FW, HY_FW)), full((1, HY_FW)), full((1, HY_FW)),
                  full((HY_FW, n_all)), full((1, n_all))],
        out_specs=[pl.BlockSpec((tm, n_all), lambda i: (i, 0)), full((1, n_all))],
        out_shape=[jax.ShapeDtypeStruct((length, n_all), F32),
                   jax.ShapeDtypeStruct((1, n_all), F32)],
        compiler_params=_params(("arbitrary",),
                                _vmem_limit(_nbytes((tm, n_all), F32) + _nbytes((HY_FW, n_all), F32),
                                            temps=3 * _nbytes((tm, n_all), F32))),
        name="filter_mlp",
    )(feat, t, w1, f_b1.reshape(1, HY_FW), f_freq1.reshape(1, HY_FW), f_w2,
      f_b2.reshape(1, HY_FW), f_freq2.reshape(1, HY_FW), f_w3,
      log_decay.reshape(1, n_all))

    tn = 1024
    nb = n_dir // tn
    comb = jax.ShapeDtypeStruct((length, n_dir), F32)
    a, b = pl.pallas_call(
        _filter_combine_kernel,
        grid=(length // tm, nb),
        in_specs=[pl.BlockSpec((tm, tn), lambda i, j: (i, j)),
                  pl.BlockSpec((tm, tn), lambda i, j: (i, nb + j)),
                  pl.BlockSpec((1, tn), lambda i, j: (0, j)),
                  pl.BlockSpec((1, tn), lambda i, j: (0, nb + j))],
        out_specs=[pl.BlockSpec((tm, tn), lambda i, j: (i, j)),
                   pl.BlockSpec((tm, tn), lambda i, j: (i, j))],
        out_shape=[comb, comb],
        compiler_params=_params(("parallel", "parallel"), VMEM_FLOOR_BYTES),
        name="filter_combine",
    )(h, h, ss, ss)
    return a, b


def _cis_product(row_hi, row_lo, period):
    def cis(phase):
        ang = (phase % period).astype(F32) * (2.0 * math.pi / period)
        return jnp.cos(ang)[:, :, None], jnp.sin(ang)[:, :, None]
    (c1, s1), (c0, s0) = cis(row_hi), cis(row_lo)
    c0, s0 = jnp.swapaxes(c0, 1, 2), jnp.swapaxes(s0, 1, 2)
    rows = row_hi.shape[0]
    return ((c1 * c0 - s1 * s0).reshape(rows, -1), (s1 * c0 + c1 * s0).reshape(rows, -1))


def _odd_dft_tables(length):
    split = 1 << (length.bit_length() // 2)
    r = jnp.arange(length, dtype=jnp.int32)[:, None]
    hi = jnp.arange(length // split, dtype=jnp.int32)[None, :] * split
    lo = jnp.arange(split, dtype=jnp.int32)[None, :]
    c, s = _cis_product((2 * r + 1) * hi, (2 * r + 1) * lo, 4 * length)
    ct, st = _cis_product(r * (2 * hi), r * (2 * lo + 1), 4 * length)
    return {"c": c.astype(BF16), "s": s.astype(BF16), "ct": ct.astype(BF16), "st": st.astype(BF16)}


def _dft_tables(length):
    split = 1 << (length.bit_length() // 2)
    r = jnp.arange(length, dtype=jnp.int32)[:, None]
    hi = jnp.arange(length // split, dtype=jnp.int32)[None, :] * split
    lo = jnp.arange(split, dtype=jnp.int32)[None, :]
    c, s = _cis_product(r * hi, r * lo, length)
    return c.astype(BF16), (-s).astype(BF16)


def _hyena_mix_dense(proj, conv_w, conv_b, filt_p, filt_m, skip, row0, n_seq, length):
    tabs = _odd_dft_tables(length)
    kr, ks = _mm(tabs["c"], filt_p), _mm(tabs["s"], filt_m)
    pc = _shortconv(proj, conv_w, conv_b, row0, n_seq, length, split=False)
    v0, x1, x2 = 0, D_MODEL, 2 * D_MODEL
    yr, ys = _dft_fwd(tabs, pc, v0, kr, ks, 0, n_seq, length)
    z1 = _dft_inv(tabs, yr, ys, pc, v0, pc, x1, skip, 0, n_seq, length, F32)
    yr, ys = _dft_fwd(tabs, z1, 0, kr, ks, 1, n_seq, length)
    return _dft_inv(tabs, yr, ys, z1, 0, pc, x2, skip, 1, n_seq, length, BF16)


HY_N1, HY_N2 = 64, 128


def _hyena_mix_pair(proj, conv_w, conv_b, filt_p, filt_m, skip, row0):
    assert DEC_BATCH == 2 and HY_N1 * HY_N2 == 2 * DEC_SEQ
    n1, n2, rows_in = HY_N1, HY_N2, HY_N1 // 2
    t = _ct_tables(n1, n2)
    c_in, s_in = t["c1"][:, :rows_in], t["s1"][:, :rows_in]
    fa = jnp.block([[c_in, s_in], [-s_in, c_in]]).astype(BF16)
    fa_real = jnp.concatenate([c_in, -s_in], axis=0).astype(BF16)
    fb = jnp.block([[t["c2"], t["s2"]], [-t["s2"], t["c2"]]]).astype(BF16)
    fbi = jnp.block([[t["c2"], -t["s2"]], [t["s2"], t["c2"]]]).astype(BF16)
    c_out, s_out = t["c1"][:rows_in], t["s1"][:rows_in]
    fai = jnp.block([[c_out, -s_out], [s_out, c_out]]).astype(BF16)
    n_filt = HY_ORDER * D_MODEL
    filt = []
    for f in (filt_p, filt_m):
        re, im = _ct_stage_a([(f.reshape(rows_in, n2, n_filt), lambda s: ())], 1, fa_real,
                             t["cw_a"], t["sw_a"])
        filt += [re[0], im[0]]

    pc = _shortconv(proj, conv_w, conv_b, row0, DEC_BATCH, DEC_SEQ, split=True)
    pc = pc.reshape(3, DEC_BATCH, rows_in, n2, D_MODEL)
    z, out_dtypes = pc, (F32, BF16)
    for order in range(HY_ORDER):
        br, bi = _ct_stage_a([(z, lambda s: (0, 0)), (z, lambda s: (0, 1))], 1, fa,
                             t["cw_a"], t["sw_a"])
        vr, vi = _ct_mid(br[0], bi[0], filt, order, fb, fbi, t["cw_b"], t["sw_b"])
        z = _ct_inv_a(vr, vi, fai, z, 0, pc, 1 + order, skip, order, out_dtypes[order])[None]
    return z.reshape(ROWS_LAT, D_MODEL)


def _fnet_chan_kernel(h_ref, sh_ref, sc_ref, w_ref, p_ref, q_ref):
    u = _modulate(h_ref[...], sh_ref[...], sc_ref[...]).astype(BF16)
    r = _dot(u, w_ref[...])
    p_ref[...] = r[:, :FNET_CG]
    q_ref[...] = r[:, FNET_CG:]


def _fnet_chan(h, mods, w_cs, tm=512):
    mod = lambda which: pl.BlockSpec(
        (None, 1, FNET_CG), lambda i, g: (which * COND_PAD + _group_of_tile(i, tm), 0, g))
    out = jax.ShapeDtypeStruct((ROWS, D_MODEL), F32)
    return pl.pallas_call(
        _fnet_chan_kernel,
        grid=(ROWS // tm, FNET_GROUPS),
        in_specs=[pl.BlockSpec((tm, FNET_CG), lambda i, g: (i, g)), mod(0), mod(1),
                  pl.BlockSpec((FNET_CG, 2 * FNET_CG), lambda i, g: (0, 0))],
        out_specs=[pl.BlockSpec((tm, FNET_CG), lambda i, g: (i, g)),
                   pl.BlockSpec((tm, FNET_CG), lambda i, g: (i, g))],
        out_shape=[out, out],
        compiler_params=_params(("parallel", "parallel"), VMEM_FLOOR_BYTES),
        name="fnet_chan",
    )(h, mods, mods, w_cs)


def _fnet_pos_kernel(scale, c_ref, ns_ref, p_ref, q_ref, o_ref, acc_ref):
    k = pl.program_id(3)

    @pl.when(k == 0)
    def _():
        acc_ref[...] = jnp.zeros_like(acc_ref)

    acc_ref[...] += (_dot(c_ref[...], p_ref[...].astype(BF16))
                     + _dot(ns_ref[...], q_ref[...].astype(BF16)))

    @pl.when(k == pl.num_programs(3) - 1)
    def _():
        o_ref[...] = (acc_ref[...] * scale).astype(o_ref.dtype)


def _fnet_pos(c_tab, ns_tab, p, q, n_seq, length):
    tt, tn, tk = _seq_tiles(length)
    nb, nt, nk = D_MODEL // tn, length // tt, length // tk
    scale = (length * FNET_CG) ** -0.5
    return pl.pallas_call(
        functools.partial(_fnet_pos_kernel, scale),
        grid=(n_seq, nt, nb, nk),
        in_specs=[pl.BlockSpec((tt, tk), lambda s, t, n, k: (t, k)),
                  pl.BlockSpec((tt, tk), lambda s, t, n, k: (t, k)),
                  pl.BlockSpec((tk, tn), lambda s, t, n, k: (s * nk + k, n)),
                  pl.BlockSpec((tk, tn), lambda s, t, n, k: (s * nk + k, n))],
        out_specs=pl.BlockSpec((tt, tn), lambda s, t, n, k: (s * nt + t, n)),
        out_shape=jax.ShapeDtypeStruct((n_seq * length, D_MODEL), BF16),
        scratch_shapes=[pltpu.VMEM((tt, tn), F32)],
        compiler_params=_params(("parallel", "parallel", "parallel", "arbitrary"),
                                _vmem_limit(2 * _nbytes((tt, tk), BF16) + 2 * _nbytes((tk, tn), BF16)
                                            + _nbytes((tt, tn), BF16),
                                            resident=_nbytes((tt, tn), F32),
                                            temps=2 * _nbytes((tt, tn), F32))),
        name="fnet_pos",
    )(c_tab, ns_tab, p, q)


FN_N1, FN_N2 = 64, 64


def _fnet_pos_factored(p, q):
    assert FN_N1 * FN_N2 == DEC_SEQ and ROWS % DEC_SEQ == 0
    n1, n2 = FN_N1, FN_N2
    t = _ct_tables(n1, n2)
    fa = jnp.block([[t["c1"], -t["s1"]], [-t["s1"], -t["c1"]]]).astype(BF16)
    fb_re = jnp.concatenate([t["c2"], t["s2"]], axis=1).astype(BF16)
    lat0 = ROWS_CTX // DEC_SEQ
    view = lambda x: x.reshape(ROWS // DEC_SEQ, n1, n2, D_MODEL)
    seq = lambda s: (lat0 + s,)
    br, bi = _ct_stage_a([(view(p), seq), (view(q), seq)], DEC_BATCH, fa, t["cw_a"], t["sw_a"])
    f = _ct_real_b(br, bi, fb_re, (DEC_SEQ * FNET_CG) ** -0.5)
    return f.reshape(ROWS_LAT, D_MODEL)


def _rope_table():
    rows = DEC_SEQ // GRID_W
    row = jnp.repeat(jnp.arange(rows), GRID_W).astype(F32)
    col = jnp.tile(jnp.arange(GRID_W), rows).astype(F32)
    inv = ROPE_THETA ** (-jnp.arange(0, AXIS_ROPE, 2, dtype=F32) / AXIS_ROPE)
    ang = jnp.concatenate([row[:, None] * inv, col[:, None] * inv], axis=-1)
    cos = jnp.repeat(jnp.cos(ang), 2, axis=-1)
    sin = jnp.repeat(jnp.sin(ang), 2, axis=-1)
    lat = jnp.tile(jnp.concatenate([cos, sin], axis=-1), (DEC_BATCH, 1))
    ctx = jnp.concatenate([jnp.ones((ROWS_CTX, QK_ROPE), F32), jnp.zeros((ROWS_CTX, QK_ROPE), F32)],
                          axis=-1)
    return jnp.concatenate([ctx, lat], axis=0)


def _pair_rotated(w):
    pairs = w.reshape(w.shape[:-1] + (QK_ROPE // 2, 2))
    return jnp.stack([-pairs[..., 1], pairs[..., 0]], axis=-1).reshape(w.shape)


def kernel(x_prompt, x_sample, c, cache_ckv, cache_krope, c_ctx, ada_w, ada_b, ln_g, ln_b, ffn_w_gate, ffn_w_up, ffn_w_down, mla_w_dq, mla_q_norm, mla_w_uq, mla_w_dkv, mla_kv_norm, mla_w_kr, mla_w_ukv, mla_w_o, hy_w_in, hy_b_in, hy_conv_w, hy_conv_b, hy_f_w1, hy_f_b1, hy_f_freq1, hy_f_w2, hy_f_b2, hy_f_freq2, hy_f_w3, hy_log_decay, hy_skip, hy_w_out, hy_b_out, fn_w_out, fn_b_out):
    assert x_prompt.shape == (BATCH, SEQ, D_MODEL) and x_sample.shape == (DEC_BATCH, DEC_SEQ, D_MODEL)
    assert ROWS_CTX % DEC_SEQ == 0 and SEQ == FNET_CG

    h = jnp.concatenate([x_prompt.reshape(ROWS_CTX, D_MODEL), x_sample.reshape(ROWS_LAT, D_MODEL)])
    cond = jnp.concatenate([c_ctx[None, :], c, jnp.zeros((COND_PAD - N_COND, D_MODEL), F32)])
    mods_all = _modulation_vectors(cond, ada_w, ada_b)
    zero_bias = jnp.zeros((D_MODEL,), F32)
    rope_tab = None
    ckv_states, krope_states = [], []

    for i in range(DEPTH):
        kind, j = i % N_MIXERS, i // N_MIXERS
        mods = mods_all[i]
        if kind == 0:
            if rope_tab is None:
                rope_tab = _rope_table()
            w_kr2 = jnp.concatenate([mla_w_kr[j], _pair_rotated(mla_w_kr[j])], axis=-1).astype(BF16)
            wq = mla_w_uq[j].reshape(Q_RANK, MLA_HEADS, QK_NOPE + QK_ROPE)
            w_q = jnp.concatenate([wq, _pair_rotated(wq[..., QK_NOPE:])], axis=-1)
            w_q = w_q.reshape(Q_RANK, MLA_HEADS * HEAD_W).astype(BF16)
            w_ukv = mla_w_ukv[j].reshape(KV_RANK, MLA_HEADS, QK_NOPE + V_DIM)
            w_k = w_ukv[..., :QK_NOPE].reshape(KV_RANK, MLA_HEADS * QK_NOPE).astype(BF16)
            w_vt = w_ukv[..., QK_NOPE:].reshape(KV_RANK, MLA_HEADS * V_DIM).T.astype(BF16)
            cq, ckv, kr, kr2 = _mla_down(h, mods, mla_w_dq[j].astype(BF16), mla_w_dkv[j].astype(BF16),
                                         w_kr2, mla_q_norm[j], mla_kv_norm[j], rope_tab)
            ckv_states.append(ckv[:ROWS_CTX].reshape(BATCH, SEQ, KV_RANK))
            krope_states.append(kr[:ROWS_CTX].reshape(BATCH, SEQ, QK_ROPE))
            q = _q_up(cq, w_q, rope_tab)
            k_tok, vt_tok = _kv_expand(ckv, kr2, w_k, w_vt)
            kc = cache_krope[:, j].reshape(DEC_BATCH * PAST_LEN, QK_ROPE).astype(BF16)
            k_cache, vt_cache = _kv_expand(cache_ckv[:, j].reshape(DEC_BATCH * PAST_LEN, KV_RANK),
                                           jnp.concatenate([kc, kc], axis=-1), w_k, w_vt)
            o = _attention(q, k_tok, vt_tok, k_cache, vt_cache)
            h = _mm_postnorm(o, mla_w_o[j].astype(BF16), zero_bias, h, mods, 2, ln_g[i, 0], ln_b[i, 0])
        elif kind == 1:
            proj = _mod_mm(h, mods, hy_w_in[j].astype(BF16), hy_b_in[j], 0)
            fp = (hy_f_w1[j], hy_f_b1[j], hy_f_freq1[j], hy_f_w2[j], hy_f_b2[j], hy_f_freq2[j],
                  hy_f_w3[j], hy_log_decay[j])
            skip = hy_skip[j].reshape(HY_ORDER, 1, D_MODEL)
            conv = (hy_conv_w[j], hy_conv_b[j])
            z_ctx = _hyena_mix_dense(proj, *conv, *_hyena_filters(SEQ, *fp), skip, 0, BATCH, SEQ)
            z_lat = _hyena_mix_pair(proj, *conv, *_hyena_filters(DEC_SEQ, *fp), skip, ROWS_CTX)
            h = _mm_postnorm(jnp.concatenate([z_ctx, z_lat]), hy_w_out[j].astype(BF16), hy_b_out[j],
                             h, mods, 2, ln_g[i, 0], ln_b[i, 0])
        else:
            c_ch, ns_ch = _dft_tables(FNET_CG)
            p, q = _fnet_chan(h, mods, jnp.concatenate([c_ch, -ns_ch], axis=-1))
            f_ctx = _fnet_pos(c_ch, ns_ch, p, q, BATCH, SEQ)
            f_lat = _fnet_pos_factored(p, q)
            h = _mm_postnorm(jnp.concatenate([f_ctx, f_lat]), fn_w_out[j].astype(BF16), fn_b_out[j],
                             h, mods, 2, ln_g[i, 0], ln_b[i, 0])
        h = _ffn(h, mods, ffn_w_gate[i].astype(BF16), ffn_w_up[i].astype(BF16),
                 ffn_w_down[i].astype(BF16), ln_g[i, 1], ln_b[i, 1])

    y_prompt = h[:ROWS_CTX].reshape(BATCH, SEQ, D_MODEL)
    y_sample = h[ROWS_CTX:].reshape(DEC_BATCH, DEC_SEQ, D_MODEL)
    return (y_prompt, y_sample, jnp.stack(ckv_states, axis=1), jnp.stack(krope_states, axis=1))
```

```python
import functools
import math

import jax
import jax.numpy as jnp
from jax import lax
from jax.experimental import pallas as pl
from jax.experimental.pallas import tpu as pltpu

F32 = jnp.float32
BF16 = jnp.bfloat16

D_MODEL = 2048
BATCH = 16
SEQ = 256
DEPTH = 4
DEC_BATCH = 2
DEC_SEQ = 4096
PAST_LEN = 512
GRID_W = 64
N_MIXERS = 3
MLA_HEADS = 16
QK_NOPE = 128
QK_ROPE = 64
V_DIM = 128
Q_RANK = 512
KV_RANK = 512
ROPE_THETA = 10000.0
AXIS_ROPE = QK_ROPE // 2
HY_ORDER = 2
HY_DIRS = 2
HY_CONV = 3
HY_BANDS = 16
HY_EMB = 1 + 2 * HY_BANDS
HY_FW = 64
HY_SHIFT = 0.05
FNET_GROUPS = 8
FNET_CG = D_MODEL // FNET_GROUPS
D_FF = -(-8 * D_MODEL // (3 * 256)) * 256
DN_ALPHA = (2 * DEPTH) ** 0.25
LN_EPS = 1e-5
RMS_EPS = 1e-6
N_MOD = 6

ROWS_CTX = BATCH * SEQ
ROWS_LAT = DEC_BATCH * DEC_SEQ
ROWS = ROWS_CTX + ROWS_LAT
N_COND = 1 + DEC_BATCH
COND_PAD = 8
HEAD_W = QK_NOPE + 2 * QK_ROPE
ATT_SCALE = (QK_NOPE + QK_ROPE) ** -0.5

V7X_VMEM_BYTES = 64 * 2 ** 20
VMEM_CAP_BYTES = V7X_VMEM_BYTES * 7 // 8
VMEM_FLOOR_BYTES = 32 * 2 ** 20


def _vmem_limit(pipelined, resident=0, temps=0):
    est = 2 * pipelined + resident + temps
    return int(min(max(est, VMEM_FLOOR_BYTES), VMEM_CAP_BYTES))


def _params(semantics, vmem):
    return pltpu.CompilerParams(dimension_semantics=semantics, vmem_limit_bytes=vmem)


def _nbytes(shape, dtype):
    return math.prod(shape) * jnp.dtype(dtype).itemsize


def _group_of_tile(i, tm):
    n_ctx = ROWS_CTX // tm
    return jnp.where(i < n_ctx, 0, 1 + (i - n_ctx) // (DEC_SEQ // tm))


def _mod_spec(which, tm):
    return pl.BlockSpec((None, 1, D_MODEL),
                        lambda i, *_: (which * COND_PAD + _group_of_tile(i, tm), 0, 0))


def _row_spec(width=D_MODEL):
    return pl.BlockSpec((1, width), lambda *_: (0, 0))


def _modulate(h, shift, scale):
    return h * (1.0 + scale) + shift


def _post_norm(h, delta, g, b):
    z = DN_ALPHA * h + delta
    mu = jnp.mean(z, axis=-1, keepdims=True)
    zc = z - mu
    var = jnp.mean(zc * zc, axis=-1, keepdims=True)
    return zc * lax.rsqrt(var + LN_EPS) * g + b


def _rms_norm(x, g):
    ms = jnp.mean(x * x, axis=-1, keepdims=True)
    return x * lax.rsqrt(ms + RMS_EPS) * g


def _dot(a, b):
    return jnp.dot(a, b, preferred_element_type=F32)


def _modvec_kernel(c_ref, w_ref, b_ref, o_ref):
    a = jax.nn.silu(c_ref[...]).astype(BF16)
    o_ref[...] = _dot(a, w_ref[...].astype(BF16)) + b_ref[...]


def _modulation_vectors(cond, ada_w, ada_b):
    tn = 1024
    n = N_MOD * D_MODEL
    out = pl.pallas_call(
        _modvec_kernel,
        grid=(DEPTH, n // tn),
        in_specs=[pl.BlockSpec((COND_PAD, D_MODEL), lambda l, j: (0, 0)),
                  pl.BlockSpec((None, D_MODEL, tn), lambda l, j: (l, 0, j)),
                  pl.BlockSpec((None, 1, tn), lambda l, j: (l, 0, j))],
        out_specs=pl.BlockSpec((None, COND_PAD, tn), lambda l, j: (l, 0, j)),
        out_shape=jax.ShapeDtypeStruct((DEPTH, COND_PAD, n), F32),
        compiler_params=_params(("parallel", "parallel"),
                                _vmem_limit(_nbytes((D_MODEL, tn), F32),
                                            temps=_nbytes((D_MODEL, tn), BF16))),
        name="modvec",
    )(cond, ada_w, ada_b.reshape(DEPTH, 1, n))
    out = out.reshape(DEPTH, COND_PAD, N_MOD, D_MODEL).transpose(0, 2, 1, 3)
    return out.reshape(DEPTH, N_MOD * COND_PAD, 1, D_MODEL)


def _mod_mm_kernel(h_ref, sh_ref, sc_ref, w_ref, b_ref, o_ref, u_ref):
    @pl.when(pl.program_id(1) == 0)
    def _():
        u_ref[...] = _modulate(h_ref[...], sh_ref[...], sc_ref[...]).astype(BF16)

    o_ref[...] = (_dot(u_ref[...], w_ref[...]) + b_ref[...]).astype(o_ref.dtype)


def _mod_mm(h, mods, w, b, which_shift, out_dtype=F32, tm=512, tn=1024):
    k, n = w.shape
    return pl.pallas_call(
        _mod_mm_kernel,
        grid=(ROWS // tm, n // tn),
        in_specs=[pl.BlockSpec((tm, k), lambda i, j: (i, 0)),
                  _mod_spec(which_shift, tm), _mod_spec(which_shift + 1, tm),
                  pl.BlockSpec((k, tn), lambda i, j: (0, j)),
                  pl.BlockSpec((1, tn), lambda i, j: (0, j))],
        out_specs=pl.BlockSpec((tm, tn), lambda i, j: (i, j)),
        out_shape=jax.ShapeDtypeStruct((ROWS, n), out_dtype),
        scratch_shapes=[pltpu.VMEM((tm, k), BF16)],
        compiler_params=_params(("parallel", "arbitrary"),
                                _vmem_limit(_nbytes((tm, k), F32) + _nbytes((k, tn), BF16)
                                            + _nbytes((tm, tn), F32),
                                            resident=_nbytes((tm, k), BF16),
                                            temps=_nbytes((tm, k), F32))),
        name="mod_mm",
    )(h, mods, mods, w, b.reshape(1, n))


def _mm_postnorm_kernel(a_ref, w_ref, bias_ref, h_ref, gate_ref, g_ref, b_ref, o_ref):
    y = _dot(a_ref[...], w_ref[...]) + bias_ref[...]
    o_ref[...] = _post_norm(h_ref[...], gate_ref[...] * y, g_ref[...], b_ref[...])


def _mm_postnorm(a, w, bias, h, mods, which_gate, ln_g, ln_b, tm=256):
    k = a.shape[1]
    return pl.pallas_call(
        _mm_postnorm_kernel,
        grid=(ROWS // tm,),
        in_specs=[pl.BlockSpec((tm, k), lambda i: (i, 0)),
                  pl.BlockSpec((k, D_MODEL), lambda i: (0, 0)),
                  _row_spec(),
                  pl.BlockSpec((tm, D_MODEL), lambda i: (i, 0)),
                  _mod_spec(which_gate, tm), _row_spec(), _row_spec()],
        out_specs=pl.BlockSpec((tm, D_MODEL), lambda i: (i, 0)),
        out_shape=jax.ShapeDtypeStruct((ROWS, D_MODEL), F32),
        compiler_params=_params(("parallel",),
                                _vmem_limit(_nbytes((tm, k), BF16) + _nbytes((k, D_MODEL), BF16)
                                            + 2 * _nbytes((tm, D_MODEL), F32),
                                            temps=3 * _nbytes((tm, D_MODEL), F32))),
        name="mm_postnorm",
    )(a, w, bias.reshape(1, D_MODEL), h, mods, ln_g.reshape(1, D_MODEL), ln_b.reshape(1, D_MODEL))


def _ffn_kernel(h_ref, sh_ref, sc_ref, gate_ref, g_ref, b_ref, wg_ref, wu_ref, wd_ref,
                o_ref, u_ref):
    f = pl.program_id(1)

    @pl.when(f == 0)
    def _():
        u_ref[...] = _modulate(h_ref[...], sh_ref[...], sc_ref[...]).astype(BF16)
        o_ref[...] = jnp.zeros_like(o_ref)

    u = u_ref[...]
    act = (jax.nn.silu(_dot(u, wg_ref[...])) * _dot(u, wu_ref[...])).astype(BF16)
    o_ref[...] += _dot(act, wd_ref[...])

    @pl.when(f == pl.num_programs(1) - 1)
    def _():
        o_ref[...] = _post_norm(h_ref[...], gate_ref[...] * o_ref[...], g_ref[...], b_ref[...])


def _ffn(h, mods, w_gate, w_up, w_down, ln_g, ln_b, tm=512, tf=512):
    return pl.pallas_call(
        _ffn_kernel,
        grid=(ROWS // tm, D_FF // tf),
        in_specs=[pl.BlockSpec((tm, D_MODEL), lambda i, f: (i, 0)),
                  _mod_spec(3, tm), _mod_spec(4, tm), _mod_spec(5, tm),
                  _row_spec(), _row_spec(),
                  pl.BlockSpec((D_MODEL, tf), lambda i, f: (0, f)),
                  pl.BlockSpec((D_MODEL, tf), lambda i, f: (0, f)),
                  pl.BlockSpec((tf, D_MODEL), lambda i, f: (f, 0))],
        out_specs=pl.BlockSpec((tm, D_MODEL), lambda i, f: (i, 0)),
        out_shape=jax.ShapeDtypeStruct((ROWS, D_MODEL), F32),
        scratch_shapes=[pltpu.VMEM((tm, D_MODEL), BF16)],
        compiler_params=_params(("parallel", "arbitrary"),
                                _vmem_limit(2 * _nbytes((tm, D_MODEL), F32)
                                            + 3 * _nbytes((D_MODEL, tf), BF16),
                                            resident=_nbytes((tm, D_MODEL), BF16),
                                            temps=3 * _nbytes((tm, D_MODEL), F32))),
        name="ffn",
    )(h, mods, mods, mods, ln_g.reshape(1, D_MODEL), ln_b.reshape(1, D_MODEL),
      w_gate, w_up, w_down)


def _mla_down_kernel(h_ref, sh_ref, sc_ref, wdq_ref, wdkv_ref, wkr_ref, qn_ref, kvn_ref,
                     rope_ref, cq_ref, ckv_ref, kr_ref, kr2_ref):
    u = _modulate(h_ref[...], sh_ref[...], sc_ref[...]).astype(BF16)
    cq_ref[...] = _rms_norm(_dot(u, wdq_ref[...]), qn_ref[...]).astype(BF16)
    ckv_ref[...] = _rms_norm(_dot(u, wdkv_ref[...]), kvn_ref[...])
    t = _dot(u, wkr_ref[...])
    kr_ref[...] = t[:, :QK_ROPE]
    v = t * rope_ref[...]
    kr2_ref[...] = (v + pltpu.roll(v, QK_ROPE, 1)).astype(BF16)


def _mla_down(h, mods, w_dq, w_dkv, w_kr2, q_norm, kv_norm, rope_tab, tm=512):
    row = lambda width: pl.BlockSpec((tm, width), lambda i: (i, 0))
    full = lambda shape: pl.BlockSpec(shape, lambda i: (0, 0))
    return pl.pallas_call(
        _mla_down_kernel,
        grid=(ROWS // tm,),
        in_specs=[row(D_MODEL), _mod_spec(0, tm), _mod_spec(1, tm),
                  full((D_MODEL, Q_RANK)), full((D_MODEL, KV_RANK)), full((D_MODEL, 2 * QK_ROPE)),
                  _row_spec(Q_RANK), _row_spec(KV_RANK), row(2 * QK_ROPE)],
        out_specs=[row(Q_RANK), row(KV_RANK), row(QK_ROPE), row(2 * QK_ROPE)],
        out_shape=[jax.ShapeDtypeStruct((ROWS, Q_RANK), BF16),
                   jax.ShapeDtypeStruct((ROWS, KV_RANK), F32),
                   jax.ShapeDtypeStruct((ROWS, QK_ROPE), F32),
                   jax.ShapeDtypeStruct((ROWS, 2 * QK_ROPE), BF16)],
        compiler_params=_params(("parallel",),
                                _vmem_limit(_nbytes((tm, D_MODEL), F32)
                                            + _nbytes((D_MODEL, Q_RANK + KV_RANK + 2 * QK_ROPE), BF16)
                                            + 3 * _nbytes((tm, KV_RANK), F32),
                                            temps=2 * _nbytes((tm, D_MODEL), F32))),
        name="mla_down",
    )(h, mods, mods, w_dq, w_dkv, w_kr2, q_norm.reshape(1, Q_RANK), kv_norm.reshape(1, KV_RANK),
      rope_tab)


NT_DIMS = (((1,), (1,)), ((), ()))


def _q_up_kernel(cq_ref, w_ref, rope_ref, q_ref):
    cq, tab = cq_ref[...], rope_ref[...] * LOG2E_SCALE
    for h in range(MLA_HEADS):
        r = _dot(cq, w_ref[:, h * HEAD_W:(h + 1) * HEAD_W])
        q_ref[h, :, :QK_NOPE] = (r[:, :QK_NOPE] * LOG2E_SCALE).astype(BF16)
        q_ref[h, :, QK_NOPE:] = (r[:, QK_NOPE:] * tab).astype(BF16)


def _q_up(cq, w_q, rope_tab, tm=512):
    return pl.pallas_call(
        _q_up_kernel,
        grid=(ROWS // tm,),
        in_specs=[pl.BlockSpec((tm, Q_RANK), lambda i: (i, 0)),
                  pl.BlockSpec((Q_RANK, MLA_HEADS * HEAD_W), lambda i: (0, 0)),
                  pl.BlockSpec((tm, 2 * QK_ROPE), lambda i: (i, 0))],
        out_specs=pl.BlockSpec((MLA_HEADS, tm, HEAD_W), lambda i: (0, i, 0)),
        out_shape=jax.ShapeDtypeStruct((MLA_HEADS, ROWS, HEAD_W), BF16),
        compiler_params=_params(("parallel",),
                                _vmem_limit(_nbytes((Q_RANK + tm, MLA_HEADS * HEAD_W), BF16))),
        name="q_up",
    )(cq, w_q, rope_tab)


def _kv_expand_kernel(ckv_ref, kr2_ref, wk_ref, wvt_ref, k_ref, vt_ref):
    c = ckv_ref[...].astype(BF16)
    vt_ref[...] = lax.dot_general(wvt_ref[...], c, NT_DIMS,
                                  preferred_element_type=F32).astype(BF16)
    kr2 = kr2_ref[...]
    pair_w = 2 * QK_NOPE
    for g in range(MLA_HEADS // 2):
        r = _dot(c, wk_ref[:, g * pair_w:(g + 1) * pair_w]).astype(BF16)
        for e in range(2):
            k_ref[2 * g + e, :, :QK_NOPE] = r[:, e * QK_NOPE:(e + 1) * QK_NOPE]
            k_ref[2 * g + e, :, QK_NOPE:] = kr2


def _kv_expand(ckv, kr2, w_k, w_vt, tm=512):
    rows = ckv.shape[0]
    return pl.pallas_call(
        _kv_expand_kernel,
        grid=(rows // tm,),
        in_specs=[pl.BlockSpec((tm, KV_RANK), lambda i: (i, 0)),
                  pl.BlockSpec((tm, 2 * QK_ROPE), lambda i: (i, 0)),
                  pl.BlockSpec((KV_RANK, MLA_HEADS * QK_NOPE), lambda i: (0, 0)),
                  pl.BlockSpec((MLA_HEADS * V_DIM, KV_RANK), lambda i: (0, 0))],
        out_specs=[pl.BlockSpec((MLA_HEADS, tm, HEAD_W), lambda i: (0, i, 0)),
                   pl.BlockSpec((MLA_HEADS * V_DIM, tm), lambda i: (0, i))],
        out_shape=[jax.ShapeDtypeStruct((MLA_HEADS, rows, HEAD_W), BF16),
                   jax.ShapeDtypeStruct((MLA_HEADS * V_DIM, rows), BF16)],
        compiler_params=_params(("parallel",),
                                _vmem_limit(_nbytes((tm, MLA_HEADS * (HEAD_W + V_DIM)), BF16)
                                            + 2 * _nbytes((KV_RANK, MLA_HEADS * V_DIM), BF16),
                                            temps=_nbytes((MLA_HEADS * V_DIM, tm), F32))),
        name="kv_expand",
    )(ckv, kr2, w_k, w_vt)


ATT_CHUNK = 512
LOG2E_SCALE = ATT_SCALE * math.log2(math.e)


def _attn_scores(q, k):
    return lax.dot_general(k, q, NT_DIMS, preferred_element_type=F32)


def _attn_values(s, vt, carry):
    m = jnp.max(s, axis=0, keepdims=True)
    if carry is not None:
        m_old, l_old, acc_old = carry
        m = jnp.maximum(m_old, m)
    p = jnp.exp2(s - m)
    l = jnp.sum(p, axis=0, keepdims=True)
    acc = _dot(vt, p.astype(BF16))
    if carry is not None:
        alpha = jnp.exp2(m_old - m)
        l = alpha * l_old + l
        acc = alpha * acc_old + acc
    return m, l, acc


def _attn_ctx_kernel(q_ref, k_ref, vt_ref, o_ref):
    for h in range(MLA_HEADS):
        _, l, acc = _attn_values(_attn_scores(q_ref[h], k_ref[h]),
                                 vt_ref[h * V_DIM:(h + 1) * V_DIM, :], None)
        o_ref[:, h * V_DIM:(h + 1) * V_DIM] = (acc / l).T.astype(BF16)


def _attn_lat_kernel(q_ref, k_ref, vt_ref, kc_ref, vtc_ref, prev_ref, o_ref):
    del prev_ref
    q = q_ref[...]
    n_tok = DEC_SEQ // ATT_CHUNK
    rows = lambda c: slice(c * ATT_CHUNK, (c + 1) * ATT_CHUNK)
    carry = None
    s = _attn_scores(q, k_ref[rows(0), :])
    for c in range(n_tok):
        k_next = k_ref[rows(c + 1), :] if c + 1 < n_tok else kc_ref[...]
        s_next = _attn_scores(q, k_next)
        carry = _attn_values(s, vt_ref[:, rows(c)], carry)
        s = s_next
    _, l, acc = _attn_values(s, vtc_ref[...], carry)
    o_ref[...] = (acc / l).T.astype(BF16)


def _attention(q, k_tok, vt_tok, k_cache, vt_cache, tq=512):
    assert PAST_LEN == ATT_CHUNK
    out_shape = jax.ShapeDtypeStruct((ROWS, MLA_HEADS * V_DIM), BF16)
    o = pl.pallas_call(
        _attn_ctx_kernel,
        grid=(BATCH,),
        in_specs=[pl.BlockSpec((MLA_HEADS, SEQ, HEAD_W), lambda s: (0, s, 0)),
                  pl.BlockSpec((MLA_HEADS, SEQ, HEAD_W), lambda s: (0, s, 0)),
                  pl.BlockSpec((MLA_HEADS * V_DIM, SEQ), lambda s: (0, s))],
        out_specs=pl.BlockSpec((SEQ, MLA_HEADS * V_DIM), lambda s: (s, 0)),
        out_shape=out_shape,
        compiler_params=_params(("parallel",), VMEM_FLOOR_BYTES),
        name="attn_ctx",
    )(q, k_tok, vt_tok)

    lat0 = ROWS_CTX // DEC_SEQ
    q0 = ROWS_CTX // tq
    nq = DEC_SEQ // tq
    return pl.pallas_call(
        _attn_lat_kernel,
        grid=(DEC_BATCH, MLA_HEADS, nq),
        in_specs=[pl.BlockSpec((None, tq, HEAD_W), lambda b, h, i: (h, q0 + b * nq + i, 0)),
                  pl.BlockSpec((None, DEC_SEQ, HEAD_W), lambda b, h, i: (h, lat0 + b, 0)),
                  pl.BlockSpec((V_DIM, DEC_SEQ), lambda b, h, i: (h, lat0 + b)),
                  pl.BlockSpec((None, PAST_LEN, HEAD_W), lambda b, h, i: (h, b, 0)),
                  pl.BlockSpec((V_DIM, PAST_LEN), lambda b, h, i: (h, b)),
                  pl.BlockSpec(memory_space=pl.ANY)],
        out_specs=pl.BlockSpec((tq, V_DIM), lambda b, h, i: (q0 + b * nq + i, h)),
        out_shape=out_shape,
        input_output_aliases={5: 0},
        compiler_params=_params(("parallel", "parallel", "arbitrary"),
                                _vmem_limit(_nbytes((DEC_SEQ + PAST_LEN, HEAD_W + V_DIM), BF16),
                                            temps=8 * _nbytes((ATT_CHUNK, tq), F32))),
        name="attn_lat",
    )(q, k_tok, vt_tok, k_cache, vt_cache, o)


def _mm_kernel(a_ref, b_ref, o_ref, acc_ref):
    k = pl.program_id(2)

    @pl.when(k == 0)
    def _():
        acc_ref[...] = jnp.zeros_like(acc_ref)

    acc_ref[...] += _dot(a_ref[...], b_ref[...].astype(BF16))

    @pl.when(k == pl.num_programs(2) - 1)
    def _():
        o_ref[...] = acc_ref[...].astype(o_ref.dtype)


def _mm(a, b, out_dtype=F32, tm=1024, tn=1024, tk=512):
    m, kk = a.shape
    n = b.shape[1]
    tm, tn, tk = min(tm, m), min(tn, n), min(tk, kk)
    return pl.pallas_call(
        _mm_kernel,
        grid=(m // tm, n // tn, kk // tk),
        in_specs=[pl.BlockSpec((tm, tk), lambda i, j, k: (i, k)),
                  pl.BlockSpec((tk, tn), lambda i, j, k: (k, j))],
        out_specs=pl.BlockSpec((tm, tn), lambda i, j, k: (i, j)),
        out_shape=jax.ShapeDtypeStruct((m, n), out_dtype),
        scratch_shapes=[pltpu.VMEM((tm, tn), F32)],
        compiler_params=_params(("parallel", "parallel", "arbitrary"), VMEM_FLOOR_BYTES),
        name="mm",
    )(a, b)


def _seq_tiles(length):
    if length >= 1024:
        return 1024, 512, 1024
    return length, D_MODEL, length


def _dft_fwd_kernel(c_ref, s_ref, z_ref, kr_ref, ks_ref, yr_ref, ys_ref, accr_ref, accs_ref):
    k = pl.program_id(3)

    @pl.when(k == 0)
    def _():
        accr_ref[...] = jnp.zeros_like(accr_ref)
        accs_ref[...] = jnp.zeros_like(accs_ref)

    z = z_ref[...].astype(BF16)
    accr_ref[...] += _dot(c_ref[...], z)
    accs_ref[...] += _dot(s_ref[...], z)

    @pl.when(k == pl.num_programs(3) - 1)
    def _():
        zr, zs, kr, ks = accr_ref[...], accs_ref[...], kr_ref[...], ks_ref[...]
        yr_ref[...] = (zr * kr - zs * ks).astype(BF16)
        ys_ref[...] = (zr * ks + zs * kr).astype(BF16)


def _dft_fwd(tabs, z, z_col0, kr, ks, order, n_seq, length):
    tf, tn, tk = _seq_tiles(length)
    nb, nf, nk = D_MODEL // tn, length // tf, length // tk
    zc0, kc0 = z_col0 // tn, order * nb
    out = jax.ShapeDtypeStruct((n_seq * length, D_MODEL), BF16)
    return pl.pallas_call(
        _dft_fwd_kernel,
        grid=(n_seq, nf, nb, nk),
        in_specs=[pl.BlockSpec((tf, tk), lambda s, f, n, k: (f, k)),
                  pl.BlockSpec((tf, tk), lambda s, f, n, k: (f, k)),
                  pl.BlockSpec((tk, tn), lambda s, f, n, k: (s * nk + k, zc0 + n)),
                  pl.BlockSpec((tf, tn), lambda s, f, n, k: (f, kc0 + n)),
                  pl.BlockSpec((tf, tn), lambda s, f, n, k: (f, kc0 + n))],
        out_specs=[pl.BlockSpec((tf, tn), lambda s, f, n, k: (s * nf + f, n)),
                   pl.BlockSpec((tf, tn), lambda s, f, n, k: (s * nf + f, n))],
        out_shape=[out, out],
        scratch_shapes=[pltpu.VMEM((tf, tn), F32), pltpu.VMEM((tf, tn), F32)],
        compiler_params=_params(("parallel", "parallel", "parallel", "arbitrary"),
                                _vmem_limit(2 * _nbytes((tf, tk), BF16) + _nbytes((tk, tn), F32)
                                            + 2 * _nbytes((tf, tn), F32) + 2 * _nbytes((tf, tn), BF16),
                                            resident=2 * _nbytes((tf, tn), F32),
                                            temps=4 * _nbytes((tf, tn), F32))),
        name="dft_fwd",
    )(tabs["c"], tabs["s"], z, kr, ks)


def _dft_inv_kernel(inv_len, ct_ref, st_ref, yr_ref, ys_ref, z_ref, gate_ref, skip_ref,
                    o_ref, acc_ref):
    k = pl.program_id(3)

    @pl.when(k == 0)
    def _():
        acc_ref[...] = jnp.zeros_like(acc_ref)

    acc_ref[...] += _dot(ct_ref[...], yr_ref[...]) + _dot(st_ref[...], ys_ref[...])

    @pl.when(k == pl.num_programs(3) - 1)
    def _():
        y = acc_ref[...] * inv_len + skip_ref[...] * z_ref[...]
        o_ref[...] = (gate_ref[...] * y).astype(o_ref.dtype)


def _dft_inv(tabs, yr, ys, z, z_col0, gate, gate_col0, skip, order, n_seq, length, out_dtype):
    tt, tn, tk = _seq_tiles(length)
    nb, nt, nk = D_MODEL // tn, length // tt, length // tk
    zc0, gc0 = z_col0 // tn, gate_col0 // tn
    return pl.pallas_call(
        functools.partial(_dft_inv_kernel, 1.0 / length),
        grid=(n_seq, nt, nb, nk),
        in_specs=[pl.BlockSpec((tt, tk), lambda s, t, n, k: (t, k)),
                  pl.BlockSpec((tt, tk), lambda s, t, n, k: (t, k)),
                  pl.BlockSpec((tk, tn), lambda s, t, n, k: (s * nk + k, n)),
                  pl.BlockSpec((tk, tn), lambda s, t, n, k: (s * nk + k, n)),
                  pl.BlockSpec((tt, tn), lambda s, t, n, k: (s * nt + t, zc0 + n)),
                  pl.BlockSpec((tt, tn), lambda s, t, n, k: (s * nt + t, gc0 + n)),
                  pl.BlockSpec((None, 1, tn), lambda s, t, n, k: (order, 0, n))],
        out_specs=pl.BlockSpec((tt, tn), lambda s, t, n, k: (s * nt + t, n)),
        out_shape=jax.ShapeDtypeStruct((n_seq * length, D_MODEL), out_dtype),
        scratch_shapes=[pltpu.VMEM((tt, tn), F32)],
        compiler_params=_params(("parallel", "parallel", "parallel", "arbitrary"),
                                _vmem_limit(2 * _nbytes((tt, tk), BF16) + 2 * _nbytes((tk, tn), BF16)
                                            + 3 * _nbytes((tt, tn), F32),
                                            resident=_nbytes((tt, tn), F32),
                                            temps=3 * _nbytes((tt, tn), F32))),
        name="dft_inv",
    )(tabs["ct"], tabs["st"], yr, ys, z, gate, skip)


TW_LANES = 128
SUB = 8
CT_ROWS = 2 * SUB
CT_COLS = 1024


def _lane_tile(x, width):
    return jnp.tile(x, (1, width // x.shape[-1]))


def _kron_sub(f):
    return jnp.kron(f, jnp.eye(SUB, dtype=f.dtype))


def _sub_rows(x, h):
    part = x[:, h * SUB:(h + 1) * SUB, :]
    return part.reshape(part.shape[0] * SUB, part.shape[2])


def _from_sub_rows(parts):
    split = [p.reshape(p.shape[0] // SUB, SUB, p.shape[1]) for p in parts]
    return jnp.concatenate(split, axis=1)


def _ct_stage_a_kernel(n_in, *refs):
    x_refs, (fa_ref, cw_ref, sw_ref, br_ref, bi_ref) = refs[:n_in], refs[n_in:]
    fa = fa_ref[...]
    half = fa.shape[0] // 2
    width = br_ref.shape[-1]
    xs = [r[...] for r in x_refs]
    b_re, b_im = [], []
    for h in range(CT_ROWS // SUB):
        x = jnp.concatenate([_sub_rows(x, h) for x in xs], axis=0).astype(BF16)
        a = _dot(fa, x)
        ar, ai = a[:half], a[half:]
        cw, sw = _lane_tile(cw_ref[h], width), _lane_tile(sw_ref[h], width)
        b_re.append(ar * cw + ai * sw)
        b_im.append(ai * cw - ar * sw)
    br_ref[...] = _from_sub_rows(b_re).astype(BF16)
    bi_ref[...] = _from_sub_rows(b_im).astype(BF16)


def _ct_stage_a(xs, n_seq, fa, cw, sw):
    n1 = fa.shape[0] // (2 * SUB)
    n2 = cw.shape[0] * CT_ROWS
    width = xs[0][0].shape[-1]
    tn2, tw = CT_ROWS, CT_COLS
    in_specs, blocks = [], 0
    for arr, prefix in xs:
        rows_in = arr.shape[-3]
        lead = (None,) * (arr.ndim - 3)
        in_specs.append(pl.BlockSpec(lead + (rows_in, tn2, tw),
                                     lambda s, i, c, prefix=prefix: prefix(s) + (0, i, c)))
        blocks += _nbytes((rows_in, tn2, tw), arr.dtype)
    twid = pl.BlockSpec((None,) + cw.shape[1:], lambda s, i, c: (i, 0, 0, 0))
    in_specs += [pl.BlockSpec(fa.shape, lambda s, i, c: (0, 0)), twid, twid]
    out = jax.ShapeDtypeStruct((n_seq, n1, n2, width), BF16)
    out_spec = pl.BlockSpec((None, n1, tn2, tw), lambda s, i, c: (s, 0, i, c))
    return pl.pallas_call(
        functools.partial(_ct_stage_a_kernel, len(xs)),
        grid=(n_seq, n2 // tn2, width // tw),
        in_specs=in_specs,
        out_specs=[out_spec, out_spec],
        out_shape=[out, out],
        compiler_params=_params(("parallel", "parallel", "parallel"),
                                _vmem_limit(blocks + 2 * _nbytes((n1, tn2, tw), BF16)
                                            + _nbytes(fa.shape, BF16),
                                            temps=8 * _nbytes((n1, tn2, tw), F32))),
        name="ct_stage_a",
    )(*[arr for arr, _ in xs], fa, cw, sw)


def _ct_mid_kernel(br_ref, bi_ref, pr_ref, pi_ref, mr_ref, mi_ref, fb_ref, fbi_ref, cw_ref, sw_ref,
                   vr_ref, vi_ref):
    fb, fbi = fb_ref[...], fbi_ref[...]
    half = fb.shape[0] // 2
    width = br_ref.shape[-1]
    for j in range(br_ref.shape[0]):
        stack = lambda re_ref, im_ref: jnp.concatenate([re_ref[j], im_ref[j]], axis=0)
        kr = _dot(fb[:half], stack(pr_ref, pi_ref))
        ki = _dot(fb[half:], stack(mr_ref, mi_ref))
        x = _dot(fb, stack(br_ref, bi_ref))
        xr, xi = x[:half], x[half:]
        y = jnp.concatenate([xr * kr - xi * ki, xr * ki + xi * kr], axis=0).astype(BF16)
        v = _dot(fbi, y)
        vr, vi = v[:half], v[half:]
        cw, sw = _lane_tile(cw_ref[j], width), _lane_tile(sw_ref[j], width)
        vr_ref[j] = (vr * cw - vi * sw).astype(BF16)
        vi_ref[j] = (vi * cw + vr * sw).astype(BF16)


def _ct_mid(br, bi, filt, order, fb, fbi, cw, sw, tk1=4, td=1024):
    n1, n2, d = br.shape
    nd = d // td
    data = pl.BlockSpec((tk1, n2, td), lambda i, j: (i, 0, j))
    coef = pl.BlockSpec((tk1, n2, td), lambda i, j: (i, 0, order * nd + j))
    mat = pl.BlockSpec(fb.shape, lambda i, j: (0, 0))
    tw = pl.BlockSpec((tk1, n2, TW_LANES), lambda i, j: (i, 0, 0))
    out = jax.ShapeDtypeStruct((n1, n2, d), BF16)
    return pl.pallas_call(
        _ct_mid_kernel,
        grid=(n1 // tk1, nd),
        in_specs=[data, data, coef, coef, coef, coef, mat, mat, tw, tw],
        out_specs=[data, data],
        out_shape=[out, out],
        compiler_params=_params(("parallel", "parallel"),
                                _vmem_limit(8 * _nbytes((tk1, n2, td), BF16),
                                            temps=10 * _nbytes((2 * n2, td), F32))),
        name="ct_mid",
    )(br, bi, *filt, fb, fbi, cw, sw)


def _ct_inv_a_kernel(scale, vr_ref, vi_ref, fai_ref, z0_ref, z1_ref, g0_ref, g1_ref, skip_ref,
                     o_ref):
    fai = fai_ref[...]
    half = fai.shape[0] // 2
    skip = skip_ref[...]
    vr, vi = vr_ref[...].astype(F32), vi_ref[...].astype(F32)
    zs, gs = (z0_ref[...], z1_ref[...]), (g0_ref[...], g1_ref[...])
    outs = ([], [])
    for h in range(CT_ROWS // SUB):
        v = jnp.concatenate([_sub_rows(vr, h), _sub_rows(vi, h)], axis=0).astype(BF16)
        y = _dot(fai, v) * scale
        for b, yb in enumerate((y[:half], y[half:])):
            outs[b].append(_sub_rows(gs[b], h) * (yb + skip * _sub_rows(zs[b], h)))
    for b in range(2):
        o_ref[b] = _from_sub_rows(outs[b]).astype(o_ref.dtype)


def _ct_inv_a(vr, vi, fai, z, z_which, gate, gate_which, skip, order, out_dtype):
    n1, n2, d = vr.shape
    rows = fai.shape[0] // (2 * SUB)
    tn2, tw = CT_ROWS, CT_COLS // 2
    spec = pl.BlockSpec((n1, tn2, tw), lambda i, c: (0, i, c))
    pair = lambda which, b: pl.BlockSpec((None, None, rows, tn2, tw), lambda i, c: (which, b, 0, i, c))
    return pl.pallas_call(
        functools.partial(_ct_inv_a_kernel, 1.0 / (n1 * n2)),
        grid=(n2 // tn2, d // tw),
        in_specs=[spec, spec, pl.BlockSpec(fai.shape, lambda i, c: (0, 0)),
                  pair(z_which, 0), pair(z_which, 1), pair(gate_which, 0), pair(gate_which, 1),
                  pl.BlockSpec((None, 1, tw), lambda i, c: (order, 0, c))],
        out_specs=pl.BlockSpec((2, rows, tn2, tw), lambda i, c: (0, 0, i, c)),
        out_shape=jax.ShapeDtypeStruct((2, rows, n2, d), out_dtype),
        compiler_params=_params(("parallel", "parallel"),
                                _vmem_limit(2 * _nbytes((n1, tn2, tw), BF16)
                                            + 6 * _nbytes((rows, tn2, tw), F32)
                                            + _nbytes(fai.shape, BF16),
                                            temps=8 * _nbytes((n1, tn2, tw), F32))),
        name="ct_inv_a",
    )(vr, vi, fai, z, z, gate, gate, skip)


def _ct_real_b_kernel(scale, br_ref, bi_ref, fb_ref, o_ref, so_ref):
    fb = fb_ref[...]
    for j in range(br_ref.shape[0]):
        so_ref[:, j, :] = _dot(fb, jnp.concatenate([br_ref[j], bi_ref[j]], axis=0)) * scale
    o_ref[...] = so_ref[...].astype(o_ref.dtype)


def _ct_real_b(br, bi, fb_re, scale):
    n_seq, n1, n2, d = br.shape
    tk1, tw = CT_ROWS, CT_COLS
    blk = pl.BlockSpec((None, tk1, n2, tw), lambda s, i, c: (s, i, 0, c))
    return pl.pallas_call(
        functools.partial(_ct_real_b_kernel, scale),
        grid=(n_seq, n1 // tk1, d // tw),
        in_specs=[blk, blk, pl.BlockSpec(fb_re.shape, lambda s, i, c: (0, 0))],
        out_specs=pl.BlockSpec((None, n2, tk1, tw), lambda s, i, c: (s, 0, i, c)),
        out_shape=jax.ShapeDtypeStruct((n_seq, n2, n1, d), BF16),
        scratch_shapes=[pltpu.VMEM((n2, tk1, tw), F32)],
        compiler_params=_params(("parallel", "parallel", "parallel"),
                                _vmem_limit(3 * _nbytes((tk1, n2, tw), BF16),
                                            resident=_nbytes((n2, tk1, tw), F32),
                                            temps=2 * _nbytes((n2, tk1, tw), F32))),
        name="ct_real_b",
    )(br, bi, fb_re)


def _cos_sin(num, den):
    ang = (num % den).astype(F32) * (2.0 * math.pi / den)
    return jnp.cos(ang), jnp.sin(ang)


def _ct_tables(n1, n2):
    i1 = jnp.arange(n1, dtype=jnp.int32)
    i2 = jnp.arange(n2, dtype=jnp.int32)
    c1, s1 = _cos_sin(i1[:, None] * i1[None, :], n1)
    c2, s2 = _cos_sin(i2[:, None] * i2[None, :], n2)
    cw, sw = _cos_sin(i2[:, None] * i1[None, :], n1 * n2)
    lanes = lambda t: jnp.broadcast_to(t[..., None], t.shape + (TW_LANES,))

    def stage_a_rows(t):
        t = t.reshape(n2 // CT_ROWS, CT_ROWS // SUB, SUB, n1)
        return lanes(jnp.swapaxes(t, 2, 3).reshape(n2 // CT_ROWS, CT_ROWS // SUB, n1 * SUB))

    return {"c1": c1, "s1": s1, "c2": c2, "s2": s2,
            "cw_a": stage_a_rows(cw), "sw_a": stage_a_rows(sw),
            "cw_b": lanes(cw.T), "sw_b": lanes(sw.T)}


def _shortconv_kernel(x_ref, w_ref, b_ref, o_ref):
    x = x_ref[...]
    n = x.shape[0]
    row = lax.broadcasted_iota(jnp.int32, x.shape, 0)
    before = jnp.where(row == 0, 0.0, pltpu.roll(x, 1, 0))
    after = jnp.where(row == n - 1, 0.0, pltpu.roll(x, n - 1, 0))
    o_ref[...] = w_ref[0:1, :] * before + w_ref[1:2, :] * x + w_ref[2:3, :] * after + b_ref[...]


def _shortconv(x, w, b, row0, n_seq, length, split):
    width = x.shape[1]
    tc = 256 if length >= 1024 else 2048
    s0 = row0 // length
    rows = n_seq * length
    if split:
        per = D_MODEL // tc
        out_spec = pl.BlockSpec((None, length, tc), lambda s, j: (j // per, s, j % per))
        out_shape = jax.ShapeDtypeStruct((width // D_MODEL, rows, D_MODEL), F32)
    else:
        out_spec = pl.BlockSpec((length, tc), lambda s, j: (s, j))
        out_shape = jax.ShapeDtypeStruct((rows, width), F32)
    return pl.pallas_call(
        _shortconv_kernel,
        grid=(n_seq, width // tc),
        in_specs=[pl.BlockSpec((length, tc), lambda s, j: (s0 + s, j)),
                  pl.BlockSpec((HY_CONV, tc), lambda s, j: (0, j)),
                  pl.BlockSpec((1, tc), lambda s, j: (0, j))],
        out_specs=out_spec,
        out_shape=out_shape,
        compiler_params=_params(("parallel", "parallel"),
                                _vmem_limit(2 * _nbytes((length, tc), F32),
                                            temps=4 * _nbytes((length, tc), F32))),
        name="shortconv",
    )(x, w, b.reshape(1, width))


def _filter_mlp_kernel(feat_ref, t_ref, w1_ref, b1_ref, fr1_ref, w2_ref, b2_ref, fr2_ref,
                       w3_ref, decay_ref, h_ref, ss_ref):
    x = jnp.sin(fr1_ref[...] * (_dot(feat_ref[...].astype(BF16), w1_ref[...].astype(BF16))
                                + b1_ref[...]))
    x = jnp.sin(fr2_ref[...] * (_dot(x.astype(BF16), w2_ref[...].astype(BF16)) + b2_ref[...]))
    h = _dot(x.astype(BF16), w3_ref[...].astype(BF16))
    h = h * (jnp.exp(-t_ref[...] * jnp.exp(decay_ref[...])) + HY_SHIFT)
    h_ref[...] = h

    @pl.when(pl.program_id(0) == 0)
    def _():
        ss_ref[...] = jnp.zeros_like(ss_ref)

    ss_ref[...] += jnp.sum(h * h, axis=0, keepdims=True)


def _filter_combine_kernel(hf_ref, hb_ref, ssf_ref, ssb_ref, a_ref, b_ref):
    norm = lax.rsqrt(ssf_ref[...] + ssb_ref[...] + 1e-12)
    fwd = hf_ref[...] * norm
    bwd = hb_ref[...] * norm
    row = lax.broadcasted_iota(jnp.int32, bwd.shape, 0) + pl.program_id(0) * bwd.shape[0]
    bwd = jnp.where(row == 0, 0.0, bwd)
    a_ref[...] = fwd + bwd
    b_ref[...] = fwd - bwd


def _hyena_filters(length, f_w1, f_b1, f_freq1, f_w2, f_b2, f_freq2, f_w3, log_decay):
    t = jnp.linspace(0.0, 1.0, length, dtype=F32)[:, None]
    t_idx = jnp.arange(length, dtype=F32)[:, None]
    bands = jnp.linspace(1e-4, HY_BANDS - 1, HY_BANDS, dtype=F32)
    w = 2.0 * math.pi * t_idx * bands / length
    feat = jnp.concatenate([t, jnp.cos(w), -jnp.sin(w)], axis=-1)
    emb_pad = 128
    feat = jnp.pad(feat, ((0, 0), (0, emb_pad - HY_EMB)))
    w1 = jnp.pad(f_w1, ((0, emb_pad - HY_EMB), (0, 0)))
    n_all = HY_DIRS * HY_ORDER * D_MODEL
    n_dir = HY_ORDER * D_MODEL
    tm = 256
    full = lambda shape: pl.BlockSpec(shape, lambda i: (0, 0))
    h, ss = pl.pallas_call(
        _filter_mlp_kernel,
        grid=(length // tm,),
        in_specs=[pl.BlockSpec((tm, emb_pad), lambda i: (i, 0)),
                  pl.BlockSpec((tm, 1), lambda i: (i, 0)),
                  full((emb_pad, HY_FW)), full((1, HY_FW)), full((1, HY_FW)),
                  full((HY_FW, HY_FW)), full((1, HY_FW)), full((1, HY_FW)),
                  full((HY_FW, n_all)), full((1, n_all))],
        out_specs=[pl.BlockSpec((tm, n_all), lambda i: (i, 0)), full((1, n_all))],
        out_shape=[jax.ShapeDtypeStruct((length, n_all), F32),
                   jax.ShapeDtypeStruct((1, n_all), F32)],
        compiler_params=_params(("arbitrary",),
                                _vmem_limit(_nbytes((tm, n_all), F32) + _nbytes((HY_FW, n_all), F32),
                                            temps=3 * _nbytes((tm, n_all), F32))),
        name="filter_mlp",
    )(feat, t, w1, f_b1.reshape(1, HY_FW), f_freq1.reshape(1, HY_FW), f_w2,
      f_b2.reshape(1, HY_FW), f_freq2.reshape(1, HY_FW), f_w3,
      log_decay.reshape(1, n_all))

    tn = 1024
    nb = n_dir // tn
    comb = jax.ShapeDtypeStruct((length, n_dir), F32)
    a, b = pl.pallas_call(
        _filter_combine_kernel,
        grid=(length // tm, nb),
        in_specs=[pl.BlockSpec((tm, tn), lambda i, j: (i, j)),
                  pl.BlockSpec((tm, tn), lambda i, j: (i, nb + j)),
                  pl.BlockSpec((1, tn), lambda i, j: (0, j)),
                  pl.BlockSpec((1, tn), lambda i, j: (0, nb + j))],
        out_specs=[pl.BlockSpec((tm, tn), lambda i, j: (i, j)),
                   pl.BlockSpec((tm, tn), lambda i, j: (i, j))],
        out_shape=[comb, comb],
        compiler_params=_params(("parallel", "parallel"), VMEM_FLOOR_BYTES),
        name="filter_combine",
    )(h, h, ss, ss)
    return a, b


def _cis_product(row_hi, row_lo, period):
    def cis(phase):
        ang = (phase % period).astype(F32) * (2.0 * math.pi / period)
        return jnp.cos(ang)[:, :, None], jnp.sin(ang)[:, :, None]
    (c1, s1), (c0, s0) = cis(row_hi), cis(row_lo)
    c0, s0 = jnp.swapaxes(c0, 1, 2), jnp.swapaxes(s0, 1, 2)
    rows = row_hi.shape[0]
    return ((c1 * c0 - s1 * s0).reshape(rows, -1), (s1 * c0 + c1 * s0).reshape(rows, -1))


def _odd_dft_tables(length):
    split = 1 << (length.bit_length() // 2)
    r = jnp.arange(length, dtype=jnp.int32)[:, None]
    hi = jnp.arange(length // split, dtype=jnp.int32)[None, :] * split
    lo = jnp.arange(split, dtype=jnp.int32)[None, :]
    c, s = _cis_product((2 * r + 1) * hi, (2 * r + 1) * lo, 4 * length)
    ct, st = _cis_product(r * (2 * hi), r * (2 * lo + 1), 4 * length)
    return {"c": c.astype(BF16), "s": s.astype(BF16), "ct": ct.astype(BF16), "st": st.astype(BF16)}


def _dft_tables(length):
    split = 1 << (length.bit_length() // 2)
    r = jnp.arange(length, dtype=jnp.int32)[:, None]
    hi = jnp.arange(length // split, dtype=jnp.int32)[None, :] * split
    lo = jnp.arange(split, dtype=jnp.int32)[None, :]
    c, s = _cis_product(r * hi, r * lo, length)
    return c.astype(BF16), (-s).astype(BF16)


def _hyena_mix_dense(proj, conv_w, conv_b, filt_p, filt_m, skip, row0, n_seq, length):
    tabs = _odd_dft_tables(length)
    kr, ks = _mm(tabs["c"], filt_p), _mm(tabs["s"], filt_m)
    pc = _shortconv(proj, conv_w, conv_b, row0, n_seq, length, split=False)
    v0, x1, x2 = 0, D_MODEL, 2 * D_MODEL
    yr, ys = _dft_fwd(tabs, pc, v0, kr, ks, 0, n_seq, length)
    z1 = _dft_inv(tabs, yr, ys, pc, v0, pc, x1, skip, 0, n_seq, length, F32)
    yr, ys = _dft_fwd(tabs, z1, 0, kr, ks, 1, n_seq, length)
    return _dft_inv(tabs, yr, ys, z1, 0, pc, x2, skip, 1, n_seq, length, BF16)


HY_N1, HY_N2 = 64, 128


def _hyena_mix_pair(proj, conv_w, conv_b, filt_p, filt_m, skip, row0):
    assert DEC_BATCH == 2 and HY_N1 * HY_N2 == 2 * DEC_SEQ
    n1, n2, rows_in = HY_N1, HY_N2, HY_N1 // 2
    t = _ct_tables(n1, n2)
    c_in, s_in = t["c1"][:, :rows_in], t["s1"][:, :rows_in]
    fa = _kron_sub(jnp.block([[c_in, s_in], [-s_in, c_in]])).astype(BF16)
    fa_real = _kron_sub(jnp.concatenate([c_in, -s_in], axis=0)).astype(BF16)
    fb = jnp.block([[t["c2"], t["s2"]], [-t["s2"], t["c2"]]]).astype(BF16)
    fbi = jnp.block([[t["c2"], -t["s2"]], [t["s2"], t["c2"]]]).astype(BF16)
    c_out, s_out = t["c1"][:rows_in], t["s1"][:rows_in]
    fai = _kron_sub(jnp.block([[c_out, -s_out], [s_out, c_out]])).astype(BF16)
    n_filt = HY_ORDER * D_MODEL
    filt = []
    for f in (filt_p, filt_m):
        re, im = _ct_stage_a([(f.reshape(rows_in, n2, n_filt), lambda s: ())], 1, fa_real,
                             t["cw_a"], t["sw_a"])
        filt += [re[0], im[0]]

    pc = _shortconv(proj, conv_w, conv_b, row0, DEC_BATCH, DEC_SEQ, split=True)
    pc = pc.reshape(3, DEC_BATCH, rows_in, n2, D_MODEL)
    z, out_dtypes = pc, (F32, BF16)
    for order in range(HY_ORDER):
        br, bi = _ct_stage_a([(z, lambda s: (0, 0)), (z, lambda s: (0, 1))], 1, fa,
                             t["cw_a"], t["sw_a"])
        vr, vi = _ct_mid(br[0], bi[0], filt, order, fb, fbi, t["cw_b"], t["sw_b"])
        z = _ct_inv_a(vr, vi, fai, z, 0, pc, 1 + order, skip, order, out_dtypes[order])[None]
    return z.reshape(ROWS_LAT, D_MODEL)


def _fnet_chan_kernel(h_ref, sh_ref, sc_ref, w_ref, p_ref, q_ref):
    u = _modulate(h_ref[...], sh_ref[...], sc_ref[...]).astype(BF16)
    r = _dot(u, w_ref[...])
    p_ref[...] = r[:, :FNET_CG]
    q_ref[...] = r[:, FNET_CG:]


def _fnet_chan(h, mods, w_cs, tm=512):
    mod = lambda which: pl.BlockSpec(
        (None, 1, FNET_CG), lambda i, g: (which * COND_PAD + _group_of_tile(i, tm), 0, g))
    out = jax.ShapeDtypeStruct((ROWS, D_MODEL), F32)
    return pl.pallas_call(
        _fnet_chan_kernel,
        grid=(ROWS // tm, FNET_GROUPS),
        in_specs=[pl.BlockSpec((tm, FNET_CG), lambda i, g: (i, g)), mod(0), mod(1),
                  pl.BlockSpec((FNET_CG, 2 * FNET_CG), lambda i, g: (0, 0))],
        out_specs=[pl.BlockSpec((tm, FNET_CG), lambda i, g: (i, g)),
                   pl.BlockSpec((tm, FNET_CG), lambda i, g: (i, g))],
        out_shape=[out, out],
        compiler_params=_params(("parallel", "parallel"), VMEM_FLOOR_BYTES),
        name="fnet_chan",
    )(h, mods, mods, w_cs)


def _fnet_pos_kernel(scale, c_ref, ns_ref, p_ref, q_ref, o_ref, acc_ref):
    k = pl.program_id(3)

    @pl.when(k == 0)
    def _():
        acc_ref[...] = jnp.zeros_like(acc_ref)

    acc_ref[...] += (_dot(c_ref[...], p_ref[...].astype(BF16))
                     + _dot(ns_ref[...], q_ref[...].astype(BF16)))

    @pl.when(k == pl.num_programs(3) - 1)
    def _():
        o_ref[...] = (acc_ref[...] * scale).astype(o_ref.dtype)


def _fnet_pos(c_tab, ns_tab, p, q, n_seq, length):
    tt, tn, tk = _seq_tiles(length)
    nb, nt, nk = D_MODEL // tn, length // tt, length // tk
    scale = (length * FNET_CG) ** -0.5
    return pl.pallas_call(
        functools.partial(_fnet_pos_kernel, scale),
        grid=(n_seq, nt, nb, nk),
        in_specs=[pl.BlockSpec((tt, tk), lambda s, t, n, k: (t, k)),
                  pl.BlockSpec((tt, tk), lambda s, t, n, k: (t, k)),
                  pl.BlockSpec((tk, tn), lambda s, t, n, k: (s * nk + k, n)),
                  pl.BlockSpec((tk, tn), lambda s, t, n, k: (s * nk + k, n))],
        out_specs=pl.BlockSpec((tt, tn), lambda s, t, n, k: (s * nt + t, n)),
        out_shape=jax.ShapeDtypeStruct((n_seq * length, D_MODEL), BF16),
        scratch_shapes=[pltpu.VMEM((tt, tn), F32)],
        compiler_params=_params(("parallel", "parallel", "parallel", "arbitrary"),
                                _vmem_limit(2 * _nbytes((tt, tk), BF16) + 2 * _nbytes((tk, tn), BF16)
                                            + _nbytes((tt, tn), BF16),
                                            resident=_nbytes((tt, tn), F32),
                                            temps=2 * _nbytes((tt, tn), F32))),
        name="fnet_pos",
    )(c_tab, ns_tab, p, q)


FN_N1, FN_N2 = 32, 128


def _fnet_pos_factored(p, q):
    assert FN_N1 * FN_N2 == DEC_SEQ and ROWS % DEC_SEQ == 0
    n1, n2 = FN_N1, FN_N2
    t = _ct_tables(n1, n2)
    fa = _kron_sub(jnp.block([[t["c1"], -t["s1"]], [-t["s1"], -t["c1"]]])).astype(BF16)
    fb_re = jnp.concatenate([t["c2"], t["s2"]], axis=1).astype(BF16)
    lat0 = ROWS_CTX // DEC_SEQ
    view = lambda x: x.reshape(ROWS // DEC_SEQ, n1, n2, D_MODEL)
    seq = lambda s: (lat0 + s,)
    br, bi = _ct_stage_a([(view(p), seq), (view(q), seq)], DEC_BATCH, fa, t["cw_a"], t["sw_a"])
    f = _ct_real_b(br, bi, fb_re, (DEC_SEQ * FNET_CG) ** -0.5)
    return f.reshape(ROWS_LAT, D_MODEL)


def _rope_table():
    rows = DEC_SEQ // GRID_W
    row = jnp.repeat(jnp.arange(rows), GRID_W).astype(F32)
    col = jnp.tile(jnp.arange(GRID_W), rows).astype(F32)
    inv = ROPE_THETA ** (-jnp.arange(0, AXIS_ROPE, 2, dtype=F32) / AXIS_ROPE)
    ang = jnp.concatenate([row[:, None] * inv, col[:, None] * inv], axis=-1)
    cos = jnp.repeat(jnp.cos(ang), 2, axis=-1)
    sin = jnp.repeat(jnp.sin(ang), 2, axis=-1)
    lat = jnp.tile(jnp.concatenate([cos, sin], axis=-1), (DEC_BATCH, 1))
    ctx = jnp.concatenate([jnp.ones((ROWS_CTX, QK_ROPE), F32), jnp.zeros((ROWS_CTX, QK_ROPE), F32)],
                          axis=-1)
    return jnp.concatenate([ctx, lat], axis=0)


def _pair_rotated(w):
    pairs = w.reshape(w.shape[:-1] + (QK_ROPE // 2, 2))
    return jnp.stack([-pairs[..., 1], pairs[..., 0]], axis=-1).reshape(w.shape)


def kernel(x_prompt, x_sample, c, cache_ckv, cache_krope, c_ctx, ada_w, ada_b, ln_g, ln_b, ffn_w_gate, ffn_w_up, ffn_w_down, mla_w_dq, mla_q_norm, mla_w_uq, mla_w_dkv, mla_kv_norm, mla_w_kr, mla_w_ukv, mla_w_o, hy_w_in, hy_b_in, hy_conv_w, hy_conv_b, hy_f_w1, hy_f_b1, hy_f_freq1, hy_f_w2, hy_f_b2, hy_f_freq2, hy_f_w3, hy_log_decay, hy_skip, hy_w_out, hy_b_out, fn_w_out, fn_b_out):
    assert x_prompt.shape == (BATCH, SEQ, D_MODEL) and x_sample.shape == (DEC_BATCH, DEC_SEQ, D_MODEL)
    assert ROWS_CTX % DEC_SEQ == 0 and SEQ == FNET_CG

    h = jnp.concatenate([x_prompt.reshape(ROWS_CTX, D_MODEL), x_sample.reshape(ROWS_LAT, D_MODEL)])
    cond = jnp.concatenate([c_ctx[None, :], c, jnp.zeros((COND_PAD - N_COND, D_MODEL), F32)])
    mods_all = _modulation_vectors(cond, ada_w, ada_b)
    zero_bias = jnp.zeros((D_MODEL,), F32)
    rope_tab = None
    ckv_states, krope_states = [], []

    for i in range(DEPTH):
        kind, j = i % N_MIXERS, i // N_MIXERS
        mods = mods_all[i]
        if kind == 0:
            if rope_tab is None:
                rope_tab = _rope_table()
            w_kr2 = jnp.concatenate([mla_w_kr[j], _pair_rotated(mla_w_kr[j])], axis=-1).astype(BF16)
            wq = mla_w_uq[j].reshape(Q_RANK, MLA_HEADS, QK_NOPE + QK_ROPE)
            w_q = jnp.concatenate([wq, _pair_rotated(wq[..., QK_NOPE:])], axis=-1)
            w_q = w_q.reshape(Q_RANK, MLA_HEADS * HEAD_W).astype(BF16)
            w_ukv = mla_w_ukv[j].reshape(KV_RANK, MLA_HEADS, QK_NOPE + V_DIM)
            w_k = w_ukv[..., :QK_NOPE].reshape(KV_RANK, MLA_HEADS * QK_NOPE).astype(BF16)
            w_vt = w_ukv[..., QK_NOPE:].reshape(KV_RANK, MLA_HEADS * V_DIM).T.astype(BF16)
            cq, ckv, kr, kr2 = _mla_down(h, mods, mla_w_dq[j].astype(BF16), mla_w_dkv[j].astype(BF16),
                                         w_kr2, mla_q_norm[j], mla_kv_norm[j], rope_tab)
            ckv_states.append(ckv[:ROWS_CTX].reshape(BATCH, SEQ, KV_RANK))
            krope_states.append(kr[:ROWS_CTX].reshape(BATCH, SEQ, QK_ROPE))
            q = _q_up(cq, w_q, rope_tab)
            k_tok, vt_tok = _kv_expand(ckv, kr2, w_k, w_vt)
            kc = cache_krope[:, j].reshape(DEC_BATCH * PAST_LEN, QK_ROPE).astype(BF16)
            k_cache, vt_cache = _kv_expand(cache_ckv[:, j].reshape(DEC_BATCH * PAST_LEN, KV_RANK),
                                           jnp.concatenate([kc, kc], axis=-1), w_k, w_vt)
            o = _attention(q, k_tok, vt_tok, k_cache, vt_cache)
            h = _mm_postnorm(o, mla_w_o[j].astype(BF16), zero_bias, h, mods, 2, ln_g[i, 0], ln_b[i, 0])
        elif kind == 1:
            proj = _mod_mm(h, mods, hy_w_in[j].astype(BF16), hy_b_in[j], 0)
            fp = (hy_f_w1[j], hy_f_b1[j], hy_f_freq1[j], hy_f_w2[j], hy_f_b2[j], hy_f_freq2[j],
                  hy_f_w3[j], hy_log_decay[j])
            skip = hy_skip[j].reshape(HY_ORDER, 1, D_MODEL)
            conv = (hy_conv_w[j], hy_conv_b[j])
            z_ctx = _hyena_mix_dense(proj, *conv, *_hyena_filters(SEQ, *fp), skip, 0, BATCH, SEQ)
            z_lat = _hyena_mix_pair(proj, *conv, *_hyena_filters(DEC_SEQ, *fp), skip, ROWS_CTX)
            h = _mm_postnorm(jnp.concatenate([z_ctx, z_lat]), hy_w_out[j].astype(BF16), hy_b_out[j],
                             h, mods, 2, ln_g[i, 0], ln_b[i, 0])
        else:
            c_ch, ns_ch = _dft_tables(FNET_CG)
            p, q = _fnet_chan(h, mods, jnp.concatenate([c_ch, -ns_ch], axis=-1))
            f_ctx = _fnet_pos(c_ch, ns_ch, p, q, BATCH, SEQ)
            f_lat = _fnet_pos_factored(p, q)
            h = _mm_postnorm(jnp.concatenate([f_ctx, f_lat]), fn_w_out[j].astype(BF16), fn_b_out[j],
                             h, mods, 2, ln_g[i, 0], ln_b[i, 0])
        h = _ffn(h, mods, ffn_w_gate[i].astype(BF16), ffn_w_up[i].astype(BF16),
                 ffn_w_down[i].astype(BF16), ln_g[i, 1], ln_b[i, 1])

    y_prompt = h[:ROWS_CTX].reshape(BATCH, SEQ, D_MODEL)
    y_sample = h[ROWS_CTX:].reshape(DEC_BATCH, DEC_SEQ, D_MODEL)
    return (y_prompt, y_sample, jnp.stack(ckv_states, axis=1), jnp.stack(krope_states, axis=1))
```

```python
import functools
import math

import jax
import jax.numpy as jnp
from jax import lax
from jax.experimental import pallas as pl
from jax.experimental.pallas import tpu as pltpu

F32 = jnp.float32
BF16 = jnp.bfloat16

D_MODEL = 2048
BATCH = 16
SEQ = 256
DEPTH = 4
DEC_BATCH = 2
DEC_SEQ = 4096
PAST_LEN = 512
GRID_W = 64
N_MIXERS = 3
MLA_HEADS = 16
QK_NOPE = 128
QK_ROPE = 64
V_DIM = 128
Q_RANK = 512
KV_RANK = 512
ROPE_THETA = 10000.0
AXIS_ROPE = QK_ROPE // 2
HY_ORDER = 2
HY_DIRS = 2
HY_CONV = 3
HY_BANDS = 16
HY_EMB = 1 + 2 * HY_BANDS
HY_FW = 64
HY_SHIFT = 0.05
FNET_GROUPS = 8
FNET_CG = D_MODEL // FNET_GROUPS
D_FF = -(-8 * D_MODEL // (3 * 256)) * 256
DN_ALPHA = (2 * DEPTH) ** 0.25
LN_EPS = 1e-5
RMS_EPS = 1e-6
N_MOD = 6

ROWS_CTX = BATCH * SEQ
ROWS_LAT = DEC_BATCH * DEC_SEQ
ROWS = ROWS_CTX + ROWS_LAT
N_COND = 1 + DEC_BATCH
COND_PAD = 8
HEAD_W = QK_NOPE + 2 * QK_ROPE
ATT_SCALE = (QK_NOPE + QK_ROPE) ** -0.5

V7X_VMEM_BYTES = 64 * 2 ** 20
VMEM_CAP_BYTES = V7X_VMEM_BYTES * 7 // 8
VMEM_FLOOR_BYTES = 32 * 2 ** 20


def _vmem_limit(pipelined, resident=0, temps=0):
    est = 2 * pipelined + resident + temps
    return int(min(max(est, VMEM_FLOOR_BYTES), VMEM_CAP_BYTES))


def _params(semantics, vmem):
    return pltpu.CompilerParams(dimension_semantics=semantics, vmem_limit_bytes=vmem)


def _nbytes(shape, dtype):
    return math.prod(shape) * jnp.dtype(dtype).itemsize


def _group_of_tile(i, tm):
    n_ctx = ROWS_CTX // tm
    return jnp.where(i < n_ctx, 0, 1 + (i - n_ctx) // (DEC_SEQ // tm))


def _mod_spec(which, tm):
    return pl.BlockSpec((None, 1, D_MODEL),
                        lambda i, *_: (which * COND_PAD + _group_of_tile(i, tm), 0, 0))


def _row_spec(width=D_MODEL):
    return pl.BlockSpec((1, width), lambda *_: (0, 0))


def _modulate(h, shift, scale):
    return h * (1.0 + scale) + shift


def _post_norm(h, delta, g, b):
    z = DN_ALPHA * h + delta
    mu = jnp.mean(z, axis=-1, keepdims=True)
    zc = z - mu
    var = jnp.mean(zc * zc, axis=-1, keepdims=True)
    return zc * lax.rsqrt(var + LN_EPS) * g + b


def _rms_norm(x, g):
    ms = jnp.mean(x * x, axis=-1, keepdims=True)
    return x * lax.rsqrt(ms + RMS_EPS) * g


def _dot(a, b):
    return jnp.dot(a, b, preferred_element_type=F32)


def _modvec_kernel(c_ref, w_ref, b_ref, o_ref):
    a = jax.nn.silu(c_ref[...]).astype(BF16)
    o_ref[...] = _dot(a, w_ref[...].astype(BF16)) + b_ref[...]


def _modulation_vectors(cond, ada_w, ada_b):
    tn = 1024
    n = N_MOD * D_MODEL
    out = pl.pallas_call(
        _modvec_kernel,
        grid=(DEPTH, n // tn),
        in_specs=[pl.BlockSpec((COND_PAD, D_MODEL), lambda l, j: (0, 0)),
                  pl.BlockSpec((None, D_MODEL, tn), lambda l, j: (l, 0, j)),
                  pl.BlockSpec((None, 1, tn), lambda l, j: (l, 0, j))],
        out_specs=pl.BlockSpec((None, COND_PAD, tn), lambda l, j: (l, 0, j)),
        out_shape=jax.ShapeDtypeStruct((DEPTH, COND_PAD, n), F32),
        compiler_params=_params(("parallel", "parallel"),
                                _vmem_limit(_nbytes((D_MODEL, tn), F32),
                                            temps=_nbytes((D_MODEL, tn), BF16))),
        name="modvec",
    )(cond, ada_w, ada_b.reshape(DEPTH, 1, n))
    out = out.reshape(DEPTH, COND_PAD, N_MOD, D_MODEL).transpose(0, 2, 1, 3)
    return out.reshape(DEPTH, N_MOD * COND_PAD, 1, D_MODEL)


def _mod_mm_kernel(h_ref, sh_ref, sc_ref, w_ref, b_ref, o_ref, u_ref):
    @pl.when(pl.program_id(1) == 0)
    def _():
        u_ref[...] = _modulate(h_ref[...], sh_ref[...], sc_ref[...]).astype(BF16)

    o_ref[...] = (_dot(u_ref[...], w_ref[...]) + b_ref[...]).astype(o_ref.dtype)


def _mod_mm(h, mods, w, b, which_shift, out_dtype=F32, tm=512, tn=1024):
    k, n = w.shape
    return pl.pallas_call(
        _mod_mm_kernel,
        grid=(ROWS // tm, n // tn),
        in_specs=[pl.BlockSpec((tm, k), lambda i, j: (i, 0)),
                  _mod_spec(which_shift, tm), _mod_spec(which_shift + 1, tm),
                  pl.BlockSpec((k, tn), lambda i, j: (0, j)),
                  pl.BlockSpec((1, tn), lambda i, j: (0, j))],
        out_specs=pl.BlockSpec((tm, tn), lambda i, j: (i, j)),
        out_shape=jax.ShapeDtypeStruct((ROWS, n), out_dtype),
        scratch_shapes=[pltpu.VMEM((tm, k), BF16)],
        compiler_params=_params(("parallel", "arbitrary"),
                                _vmem_limit(_nbytes((tm, k), F32) + _nbytes((k, tn), BF16)
                                            + _nbytes((tm, tn), F32),
                                            resident=_nbytes((tm, k), BF16),
                                            temps=_nbytes((tm, k), F32))),
        name="mod_mm",
    )(h, mods, mods, w, b.reshape(1, n))


def _mm_postnorm_kernel(n_ctx, *refs):
    n_a = 1 if n_ctx is None else 2
    a_refs, (w_ref, bias_ref, h_ref, gate_ref, g_ref, b_ref, o_ref) = refs[:n_a], refs[n_a:]
    half = o_ref.shape[0] // 2
    halves = (slice(0, half), slice(half, 2 * half))

    def rows(r):
        if n_ctx is None:
            return a_refs[0][r, :]
        return jnp.where(pl.program_id(0) < n_ctx, a_refs[0][r, :], a_refs[1][r, :])

    ys = [_dot(rows(r), w_ref[...]) + bias_ref[...] for r in halves]
    for r, y in zip(halves, ys):
        o_ref[r, :] = _post_norm(h_ref[r, :], gate_ref[...] * y, g_ref[...], b_ref[...])


def _mm_postnorm(a, w, bias, h, mods, which_gate, ln_g, ln_b, tm=512):
    if isinstance(a, tuple):
        n_ctx = ROWS_CTX // tm
        k = a[0].shape[1]
        a_specs = [pl.BlockSpec((tm, k), lambda i: (jnp.minimum(i, n_ctx - 1), 0)),
                   pl.BlockSpec((tm, k), lambda i: (jnp.maximum(i - n_ctx, 0), 0))]
    else:
        n_ctx, k, a = None, a.shape[1], (a,)
        a_specs = [pl.BlockSpec((tm, k), lambda i: (i, 0))]
    return pl.pallas_call(
        functools.partial(_mm_postnorm_kernel, n_ctx),
        grid=(ROWS // tm,),
        in_specs=a_specs + [pl.BlockSpec((k, D_MODEL), lambda i: (0, 0)),
                            _row_spec(),
                            pl.BlockSpec((tm, D_MODEL), lambda i: (i, 0)),
                            _mod_spec(which_gate, tm), _row_spec(), _row_spec()],
        out_specs=pl.BlockSpec((tm, D_MODEL), lambda i: (i, 0)),
        out_shape=jax.ShapeDtypeStruct((ROWS, D_MODEL), F32),
        compiler_params=_params(("parallel",),
                                _vmem_limit(len(a) * _nbytes((tm, k), BF16)
                                            + _nbytes((k, D_MODEL), BF16)
                                            + 2 * _nbytes((tm, D_MODEL), F32),
                                            temps=3 * _nbytes((tm, D_MODEL), F32))),
        name="mm_postnorm",
    )(*a, w, bias.reshape(1, D_MODEL), h, mods, ln_g.reshape(1, D_MODEL), ln_b.reshape(1, D_MODEL))


def _ffn_kernel(n_ctx, h_ref, sh_ref, sc_ref, gate_ref, g_ref, b_ref, wg_ref, wu_ref, wd_ref,
                *refs):
    i, f = pl.program_id(0), pl.program_id(1)
    if n_ctx is None:
        o_ref, u_ref = refs
        acc_ref, outs = o_ref, ((o_ref, None),)
    else:
        octx_ref, olat_ref, u_ref, acc_ref = refs
        outs = ((octx_ref, i < n_ctx), (olat_ref, i >= n_ctx))

    @pl.when(f == 0)
    def _():
        u_ref[...] = _modulate(h_ref[...], sh_ref[...], sc_ref[...]).astype(BF16)
        acc_ref[...] = jnp.zeros_like(acc_ref)

    half = u_ref.shape[0] // 2
    halves = (slice(0, half), slice(half, 2 * half))
    proj = [(_dot(u_ref[r, :], wg_ref[...]), _dot(u_ref[r, :], wu_ref[...])) for r in halves]
    for r, (gate, up) in zip(halves, proj):
        act = (jax.nn.silu(gate) * up).astype(BF16)
        acc_ref[r, :] += _dot(act, wd_ref[...])

    last = f == pl.num_programs(1) - 1
    for o_ref, mine in outs:
        @pl.when(last if mine is None else jnp.logical_and(last, mine))
        def _(o_ref=o_ref):
            for r in halves:
                o_ref[r, :] = _post_norm(h_ref[r, :], gate_ref[...] * acc_ref[r, :],
                                         g_ref[...], b_ref[...])


def _ffn(h, mods, w_gate, w_up, w_down, ln_g, ln_b, split_out, tm=512, tf=512):
    tile = _nbytes((tm, D_MODEL), F32)
    if split_out:
        n_ctx = ROWS_CTX // tm
        out_specs = [pl.BlockSpec((tm, D_MODEL), lambda i, f: (jnp.minimum(i, n_ctx - 1), 0)),
                     pl.BlockSpec((tm, D_MODEL), lambda i, f: (jnp.maximum(i - n_ctx, 0), 0))]
        out_shape = [jax.ShapeDtypeStruct((ROWS_CTX, D_MODEL), F32),
                     jax.ShapeDtypeStruct((ROWS_LAT, D_MODEL), F32)]
        scratch = [pltpu.VMEM((tm, D_MODEL), BF16), pltpu.VMEM((tm, D_MODEL), F32)]
        pipelined, resident = 3 * tile, tile + tile // 2
    else:
        n_ctx = None
        out_specs = pl.BlockSpec((tm, D_MODEL), lambda i, f: (i, 0))
        out_shape = jax.ShapeDtypeStruct((ROWS, D_MODEL), F32)
        scratch = [pltpu.VMEM((tm, D_MODEL), BF16)]
        pipelined, resident = 2 * tile, tile // 2
    return pl.pallas_call(
        functools.partial(_ffn_kernel, n_ctx),
        grid=(ROWS // tm, D_FF // tf),
        in_specs=[pl.BlockSpec((tm, D_MODEL), lambda i, f: (i, 0)),
                  _mod_spec(3, tm), _mod_spec(4, tm), _mod_spec(5, tm),
                  _row_spec(), _row_spec(),
                  pl.BlockSpec((D_MODEL, tf), lambda i, f: (0, f)),
                  pl.BlockSpec((D_MODEL, tf), lambda i, f: (0, f)),
                  pl.BlockSpec((tf, D_MODEL), lambda i, f: (f, 0))],
        out_specs=out_specs,
        out_shape=out_shape,
        scratch_shapes=scratch,
        compiler_params=_params(("parallel", "arbitrary"),
                                _vmem_limit(pipelined + 3 * _nbytes((D_MODEL, tf), BF16),
                                            resident=resident, temps=2 * tile)),
        name="ffn",
    )(h, mods, mods, mods, ln_g.reshape(1, D_MODEL), ln_b.reshape(1, D_MODEL),
      w_gate, w_up, w_down)


def _mla_down_kernel(h_ref, sh_ref, sc_ref, wdq_ref, wdkv_ref, wkr_ref, qn_ref, kvn_ref,
                     rope_ref, cq_ref, ckv_ref, kr_ref, kr2_ref):
    u = _modulate(h_ref[...], sh_ref[...], sc_ref[...]).astype(BF16)
    cq_ref[...] = _rms_norm(_dot(u, wdq_ref[...]), qn_ref[...]).astype(BF16)
    ckv_ref[...] = _rms_norm(_dot(u, wdkv_ref[...]), kvn_ref[...])
    t = _dot(u, wkr_ref[...])
    kr_ref[...] = t[:, :QK_ROPE]
    v = t * rope_ref[...]
    kr2_ref[...] = (v + pltpu.roll(v, QK_ROPE, 1)).astype(BF16)


def _mla_down(h, mods, w_dq, w_dkv, w_kr2, q_norm, kv_norm, rope_tab, tm=512):
    row = lambda width: pl.BlockSpec((tm, width), lambda i: (i, 0))
    full = lambda shape: pl.BlockSpec(shape, lambda i: (0, 0))
    return pl.pallas_call(
        _mla_down_kernel,
        grid=(ROWS // tm,),
        in_specs=[row(D_MODEL), _mod_spec(0, tm), _mod_spec(1, tm),
                  full((D_MODEL, Q_RANK)), full((D_MODEL, KV_RANK)), full((D_MODEL, 2 * QK_ROPE)),
                  _row_spec(Q_RANK), _row_spec(KV_RANK), row(2 * QK_ROPE)],
        out_specs=[row(Q_RANK), row(KV_RANK), row(QK_ROPE), row(2 * QK_ROPE)],
        out_shape=[jax.ShapeDtypeStruct((ROWS, Q_RANK), BF16),
                   jax.ShapeDtypeStruct((ROWS, KV_RANK), F32),
                   jax.ShapeDtypeStruct((ROWS, QK_ROPE), F32),
                   jax.ShapeDtypeStruct((ROWS, 2 * QK_ROPE), BF16)],
        compiler_params=_params(("parallel",),
                                _vmem_limit(_nbytes((tm, D_MODEL), F32)
                                            + _nbytes((D_MODEL, Q_RANK + KV_RANK + 2 * QK_ROPE), BF16)
                                            + 3 * _nbytes((tm, KV_RANK), F32),
                                            temps=2 * _nbytes((tm, D_MODEL), F32))),
        name="mla_down",
    )(h, mods, mods, w_dq, w_dkv, w_kr2, q_norm.reshape(1, Q_RANK), kv_norm.reshape(1, KV_RANK),
      rope_tab)


NT_DIMS = (((1,), (1,)), ((), ()))


def _q_up_kernel(cq_ref, w_ref, rope_ref, q_ref):
    cq, tab = cq_ref[...], rope_ref[...] * LOG2E_SCALE
    for h in range(MLA_HEADS):
        r = _dot(cq, w_ref[:, h * HEAD_W:(h + 1) * HEAD_W])
        q_ref[h, :, :QK_NOPE] = (r[:, :QK_NOPE] * LOG2E_SCALE).astype(BF16)
        q_ref[h, :, QK_NOPE:] = (r[:, QK_NOPE:] * tab).astype(BF16)


def _q_up(cq, w_q, rope_tab, tm=512):
    return pl.pallas_call(
        _q_up_kernel,
        grid=(ROWS // tm,),
        in_specs=[pl.BlockSpec((tm, Q_RANK), lambda i: (i, 0)),
                  pl.BlockSpec((Q_RANK, MLA_HEADS * HEAD_W), lambda i: (0, 0)),
                  pl.BlockSpec((tm, 2 * QK_ROPE), lambda i: (i, 0))],
        out_specs=pl.BlockSpec((MLA_HEADS, tm, HEAD_W), lambda i: (0, i, 0)),
        out_shape=jax.ShapeDtypeStruct((MLA_HEADS, ROWS, HEAD_W), BF16),
        compiler_params=_params(("parallel",),
                                _vmem_limit(_nbytes((Q_RANK + tm, MLA_HEADS * HEAD_W), BF16))),
        name="q_up",
    )(cq, w_q, rope_tab)


def _kv_expand_kernel(ckv_ref, kr2_ref, wk_ref, wvt_ref, k_ref, vt_ref):
    c = ckv_ref[...].astype(BF16)
    vt_ref[...] = lax.dot_general(wvt_ref[...], c, NT_DIMS,
                                  preferred_element_type=F32).astype(BF16)
    kr2 = kr2_ref[...]
    pair_w = 2 * QK_NOPE
    for g in range(MLA_HEADS // 2):
        r = _dot(c, wk_ref[:, g * pair_w:(g + 1) * pair_w]).astype(BF16)
        for e in range(2):
            k_ref[2 * g + e, :, :QK_NOPE] = r[:, e * QK_NOPE:(e + 1) * QK_NOPE]
            k_ref[2 * g + e, :, QK_NOPE:] = kr2


def _kv_expand(ckv, kr2, w_k, w_vt, tm=512):
    rows = ckv.shape[0]
    return pl.pallas_call(
        _kv_expand_kernel,
        grid=(rows // tm,),
        in_specs=[pl.BlockSpec((tm, KV_RANK), lambda i: (i, 0)),
                  pl.BlockSpec((tm, 2 * QK_ROPE), lambda i: (i, 0)),
                  pl.BlockSpec((KV_RANK, MLA_HEADS * QK_NOPE), lambda i: (0, 0)),
                  pl.BlockSpec((MLA_HEADS * V_DIM, KV_RANK), lambda i: (0, 0))],
        out_specs=[pl.BlockSpec((MLA_HEADS, tm, HEAD_W), lambda i: (0, i, 0)),
                   pl.BlockSpec((MLA_HEADS * V_DIM, tm), lambda i: (0, i))],
        out_shape=[jax.ShapeDtypeStruct((MLA_HEADS, rows, HEAD_W), BF16),
                   jax.ShapeDtypeStruct((MLA_HEADS * V_DIM, rows), BF16)],
        compiler_params=_params(("parallel",),
                                _vmem_limit(_nbytes((tm, MLA_HEADS * (HEAD_W + V_DIM)), BF16)
                                            + 2 * _nbytes((KV_RANK, MLA_HEADS * V_DIM), BF16),
                                            temps=_nbytes((MLA_HEADS * V_DIM, tm), F32))),
        name="kv_expand",
    )(ckv, kr2, w_k, w_vt)


ATT_CHUNK = 512
ATT_SKEW = 3
LOG2E_SCALE = ATT_SCALE * math.log2(math.e)


def _attn_scores(q, k):
    return lax.dot_general(k, q, NT_DIMS, preferred_element_type=F32)


def _attn_values(s, vt, carry):
    m = jnp.max(s, axis=0, keepdims=True)
    if carry is not None:
        m_old, l_old, acc_old = carry
        m = jnp.maximum(m_old, m)
    p = jnp.exp2(s - m)
    l = jnp.sum(p, axis=0, keepdims=True)
    acc = _dot(vt, p.astype(BF16))
    if carry is not None:
        alpha = jnp.exp2(m_old - m)
        l = alpha * l_old + l
        acc = alpha * acc_old + acc
    return m, l, acc


def _attn_ctx_kernel(q_ref, k_ref, vt_ref, o_ref):
    for h in range(MLA_HEADS):
        _, l, acc = _attn_values(_attn_scores(q_ref[h], k_ref[h]),
                                 vt_ref[h * V_DIM:(h + 1) * V_DIM, :], None)
        o_ref[:, h * V_DIM:(h + 1) * V_DIM] = (acc / l).T.astype(BF16)


def _attn_lat_kernel(q_ref, k_ref, vt_ref, kc_ref, vtc_ref, prev_ref, o_ref):
    del prev_ref
    q = q_ref[...]
    n_tok = DEC_SEQ // ATT_CHUNK
    rows = lambda c: slice(c * ATT_CHUNK, (c + 1) * ATT_CHUNK)
    keys = [k_ref.at[rows(c), :] for c in range(n_tok)] + [kc_ref]
    vals = [vt_ref.at[:, rows(c)] for c in range(n_tok)] + [vtc_ref]
    carry = None
    scores = [_attn_scores(q, keys[c][...]) for c in range(ATT_SKEW)]
    for c in range(n_tok + 1):
        if c + ATT_SKEW <= n_tok:
            scores.append(_attn_scores(q, keys[c + ATT_SKEW][...]))
        carry = _attn_values(scores[c], vals[c][...], carry)
    _, l, acc = carry
    o_ref[...] = (acc / l).T.astype(BF16)


def _attention(q, k_tok, vt_tok, k_cache, vt_cache, tq=512):
    assert PAST_LEN == ATT_CHUNK
    out_shape = jax.ShapeDtypeStruct((ROWS, MLA_HEADS * V_DIM), BF16)
    o = pl.pallas_call(
        _attn_ctx_kernel,
        grid=(BATCH,),
        in_specs=[pl.BlockSpec((MLA_HEADS, SEQ, HEAD_W), lambda s: (0, s, 0)),
                  pl.BlockSpec((MLA_HEADS, SEQ, HEAD_W), lambda s: (0, s, 0)),
                  pl.BlockSpec((MLA_HEADS * V_DIM, SEQ), lambda s: (0, s))],
        out_specs=pl.BlockSpec((SEQ, MLA_HEADS * V_DIM), lambda s: (s, 0)),
        out_shape=out_shape,
        compiler_params=_params(("parallel",), VMEM_FLOOR_BYTES),
        name="attn_ctx",
    )(q, k_tok, vt_tok)

    lat0 = ROWS_CTX // DEC_SEQ
    q0 = ROWS_CTX // tq
    nq = DEC_SEQ // tq
    return pl.pallas_call(
        _attn_lat_kernel,
        grid=(DEC_BATCH, MLA_HEADS, nq),
        in_specs=[pl.BlockSpec((None, tq, HEAD_W), lambda b, h, i: (h, q0 + b * nq + i, 0)),
                  pl.BlockSpec((None, DEC_SEQ, HEAD_W), lambda b, h, i: (h, lat0 + b, 0)),
                  pl.BlockSpec((V_DIM, DEC_SEQ), lambda b, h, i: (h, lat0 + b)),
                  pl.BlockSpec((None, PAST_LEN, HEAD_W), lambda b, h, i: (h, b, 0)),
                  pl.BlockSpec((V_DIM, PAST_LEN), lambda b, h, i: (h, b)),
                  pl.BlockSpec(memory_space=pl.ANY)],
        out_specs=pl.BlockSpec((tq, V_DIM), lambda b, h, i: (q0 + b * nq + i, h)),
        out_shape=out_shape,
        input_output_aliases={5: 0},
        compiler_params=_params(("parallel", "parallel", "arbitrary"),
                                _vmem_limit(_nbytes((DEC_SEQ + PAST_LEN, HEAD_W + V_DIM), BF16),
                                            temps=8 * _nbytes((ATT_CHUNK, tq), F32))),
        name="attn_lat",
    )(q, k_tok, vt_tok, k_cache, vt_cache, o)


def _mm_kernel(a_ref, b_ref, o_ref, acc_ref):
    k = pl.program_id(2)

    @pl.when(k == 0)
    def _():
        acc_ref[...] = jnp.zeros_like(acc_ref)

    acc_ref[...] += _dot(a_ref[...], b_ref[...].astype(BF16))

    @pl.when(k == pl.num_programs(2) - 1)
    def _():
        o_ref[...] = acc_ref[...].astype(o_ref.dtype)


def _mm(a, b, out_dtype=F32, tm=1024, tn=1024, tk=512):
    m, kk = a.shape
    n = b.shape[1]
    tm, tn, tk = min(tm, m), min(tn, n), min(tk, kk)
    return pl.pallas_call(
        _mm_kernel,
        grid=(m // tm, n // tn, kk // tk),
        in_specs=[pl.BlockSpec((tm, tk), lambda i, j, k: (i, k)),
                  pl.BlockSpec((tk, tn), lambda i, j, k: (k, j))],
        out_specs=pl.BlockSpec((tm, tn), lambda i, j, k: (i, j)),
        out_shape=jax.ShapeDtypeStruct((m, n), out_dtype),
        scratch_shapes=[pltpu.VMEM((tm, tn), F32)],
        compiler_params=_params(("parallel", "parallel", "arbitrary"), VMEM_FLOOR_BYTES),
        name="mm",
    )(a, b)


def _seq_tiles(length):
    if length >= 1024:
        return 1024, 512, 1024
    return length, D_MODEL, length


def _dft_fwd_kernel(c_ref, s_ref, z_ref, kr_ref, ks_ref, yr_ref, ys_ref, accr_ref, accs_ref):
    k = pl.program_id(3)

    @pl.when(k == 0)
    def _():
        accr_ref[...] = jnp.zeros_like(accr_ref)
        accs_ref[...] = jnp.zeros_like(accs_ref)

    z = z_ref[...].astype(BF16)
    accr_ref[...] += _dot(c_ref[...], z)
    accs_ref[...] += _dot(s_ref[...], z)

    @pl.when(k == pl.num_programs(3) - 1)
    def _():
        zr, zs, kr, ks = accr_ref[...], accs_ref[...], kr_ref[...], ks_ref[...]
        yr_ref[...] = (zr * kr - zs * ks).astype(BF16)
        ys_ref[...] = (zr * ks + zs * kr).astype(BF16)


def _dft_fwd(tabs, z, z_col0, kr, ks, order, n_seq, length):
    tf, tn, tk = _seq_tiles(length)
    nb, nf, nk = D_MODEL // tn, length // tf, length // tk
    zc0, kc0 = z_col0 // tn, order * nb
    out = jax.ShapeDtypeStruct((n_seq * length, D_MODEL), BF16)
    return pl.pallas_call(
        _dft_fwd_kernel,
        grid=(n_seq, nf, nb, nk),
        in_specs=[pl.BlockSpec((tf, tk), lambda s, f, n, k: (f, k)),
                  pl.BlockSpec((tf, tk), lambda s, f, n, k: (f, k)),
                  pl.BlockSpec((tk, tn), lambda s, f, n, k: (s * nk + k, zc0 + n)),
                  pl.BlockSpec((tf, tn), lambda s, f, n, k: (f, kc0 + n)),
                  pl.BlockSpec((tf, tn), lambda s, f, n, k: (f, kc0 + n))],
        out_specs=[pl.BlockSpec((tf, tn), lambda s, f, n, k: (s * nf + f, n)),
                   pl.BlockSpec((tf, tn), lambda s, f, n, k: (s * nf + f, n))],
        out_shape=[out, out],
        scratch_shapes=[pltpu.VMEM((tf, tn), F32), pltpu.VMEM((tf, tn), F32)],
        compiler_params=_params(("parallel", "parallel", "parallel", "arbitrary"),
                                _vmem_limit(2 * _nbytes((tf, tk), BF16) + _nbytes((tk, tn), F32)
                                            + 2 * _nbytes((tf, tn), F32) + 2 * _nbytes((tf, tn), BF16),
                                            resident=2 * _nbytes((tf, tn), F32),
                                            temps=4 * _nbytes((tf, tn), F32))),
        name="dft_fwd",
    )(tabs["c"], tabs["s"], z, kr, ks)


def _dft_inv_kernel(inv_len, ct_ref, st_ref, yr_ref, ys_ref, z_ref, gate_ref, skip_ref,
                    o_ref, acc_ref):
    k = pl.program_id(3)

    @pl.when(k == 0)
    def _():
        acc_ref[...] = jnp.zeros_like(acc_ref)

    acc_ref[...] += _dot(ct_ref[...], yr_ref[...]) + _dot(st_ref[...], ys_ref[...])

    @pl.when(k == pl.num_programs(3) - 1)
    def _():
        y = acc_ref[...] * inv_len + skip_ref[...] * z_ref[...]
        o_ref[...] = (gate_ref[...] * y).astype(o_ref.dtype)


def _dft_inv(tabs, yr, ys, z, z_col0, gate, gate_col0, skip, order, n_seq, length, out_dtype):
    tt, tn, tk = _seq_tiles(length)
    nb, nt, nk = D_MODEL // tn, length // tt, length // tk
    zc0, gc0 = z_col0 // tn, gate_col0 // tn
    return pl.pallas_call(
        functools.partial(_dft_inv_kernel, 1.0 / length),
        grid=(n_seq, nt, nb, nk),
        in_specs=[pl.BlockSpec((tt, tk), lambda s, t, n, k: (t, k)),
                  pl.BlockSpec((tt, tk), lambda s, t, n, k: (t, k)),
                  pl.BlockSpec((tk, tn), lambda s, t, n, k: (s * nk + k, n)),
                  pl.BlockSpec((tk, tn), lambda s, t, n, k: (s * nk + k, n)),
                  pl.BlockSpec((tt, tn), lambda s, t, n, k: (s * nt + t, zc0 + n)),
                  pl.BlockSpec((tt, tn), lambda s, t, n, k: (s * nt + t, gc0 + n)),
                  pl.BlockSpec((None, 1, tn), lambda s, t, n, k: (order, 0, n))],
        out_specs=pl.BlockSpec((tt, tn), lambda s, t, n, k: (s * nt + t, n)),
        out_shape=jax.ShapeDtypeStruct((n_seq * length, D_MODEL), out_dtype),
        scratch_shapes=[pltpu.VMEM((tt, tn), F32)],
        compiler_params=_params(("parallel", "parallel", "parallel", "arbitrary"),
                                _vmem_limit(2 * _nbytes((tt, tk), BF16) + 2 * _nbytes((tk, tn), BF16)
                                            + 3 * _nbytes((tt, tn), F32),
                                            resident=_nbytes((tt, tn), F32),
                                            temps=3 * _nbytes((tt, tn), F32))),
        name="dft_inv",
    )(tabs["ct"], tabs["st"], yr, ys, z, gate, skip)


TW_LANES = 128
SUB = 8
CT_ROWS = 2 * SUB
CT_COLS = 1024


def _lane_tile(x, width):
    return jnp.tile(x, (1, width // x.shape[-1]))


def _kron_sub(f):
    return jnp.kron(f, jnp.eye(SUB, dtype=f.dtype))


def _sub_rows(x, h):
    part = x[:, h * SUB:(h + 1) * SUB, :]
    return part.reshape(part.shape[0] * SUB, part.shape[2])


def _from_sub_rows(parts):
    split = [p.reshape(p.shape[0] // SUB, SUB, p.shape[1]) for p in parts]
    return jnp.concatenate(split, axis=1)


def _ct_stage_a_kernel(n_in, *refs):
    x_refs, (fa_ref, cw_ref, sw_ref, br_ref, bi_ref) = refs[:n_in], refs[n_in:]
    fa = fa_ref[...]
    half = fa.shape[0] // 2
    width = br_ref.shape[-1]
    xs = [r[...] for r in x_refs]
    b_re, b_im = [], []
    for h in range(CT_ROWS // SUB):
        x = jnp.concatenate([_sub_rows(x, h) for x in xs], axis=0).astype(BF16)
        a = _dot(fa, x)
        ar, ai = a[:half], a[half:]
        cw, sw = _lane_tile(cw_ref[h], width), _lane_tile(sw_ref[h], width)
        b_re.append(ar * cw + ai * sw)
        b_im.append(ai * cw - ar * sw)
    br_ref[...] = _from_sub_rows(b_re).astype(BF16)
    bi_ref[...] = _from_sub_rows(b_im).astype(BF16)


def _ct_stage_a(xs, n_seq, fa, cw, sw):
    n1 = fa.shape[0] // (2 * SUB)
    n2 = cw.shape[0] * CT_ROWS
    width = xs[0][0].shape[-1]
    tn2, tw = CT_ROWS, CT_COLS
    in_specs, blocks = [], 0
    for arr, prefix in xs:
        rows_in = arr.shape[-3]
        lead = (None,) * (arr.ndim - 3)
        in_specs.append(pl.BlockSpec(lead + (rows_in, tn2, tw),
                                     lambda s, i, c, prefix=prefix: prefix(s) + (0, i, c)))
        blocks += _nbytes((rows_in, tn2, tw), arr.dtype)
    twid = pl.BlockSpec((None,) + cw.shape[1:], lambda s, i, c: (i, 0, 0, 0))
    in_specs += [pl.BlockSpec(fa.shape, lambda s, i, c: (0, 0)), twid, twid]
    out = jax.ShapeDtypeStruct((n_seq, n1, n2, width), BF16)
    out_spec = pl.BlockSpec((None, n1, tn2, tw), lambda s, i, c: (s, 0, i, c))
    return pl.pallas_call(
        functools.partial(_ct_stage_a_kernel, len(xs)),
        grid=(n_seq, n2 // tn2, width // tw),
        in_specs=in_specs,
        out_specs=[out_spec, out_spec],
        out_shape=[out, out],
        compiler_params=_params(("parallel", "parallel", "parallel"),
                                _vmem_limit(blocks + 2 * _nbytes((n1, tn2, tw), BF16)
                                            + _nbytes(fa.shape, BF16),
                                            temps=8 * _nbytes((n1, tn2, tw), F32))),
        name="ct_stage_a",
    )(*[arr for arr, _ in xs], fa, cw, sw)


def _ct_mid_kernel(br_ref, bi_ref, pr_ref, pi_ref, mr_ref, mi_ref, fb_ref, fbi_ref, cw_ref, sw_ref,
                   vr_ref, vi_ref):
    fb, fbi = fb_ref[...], fbi_ref[...]
    half = fb.shape[0] // 2
    width = br_ref.shape[-1]
    for j in range(br_ref.shape[0]):
        stack = lambda re_ref, im_ref: jnp.concatenate([re_ref[j], im_ref[j]], axis=0)
        kr = _dot(fb[:half], stack(pr_ref, pi_ref))
        ki = _dot(fb[half:], stack(mr_ref, mi_ref))
        x = _dot(fb, stack(br_ref, bi_ref))
        xr, xi = x[:half], x[half:]
        y = jnp.concatenate([xr * kr - xi * ki, xr * ki + xi * kr], axis=0).astype(BF16)
        v = _dot(fbi, y)
        vr, vi = v[:half], v[half:]
        cw, sw = _lane_tile(cw_ref[j], width), _lane_tile(sw_ref[j], width)
        vr_ref[j] = (vr * cw - vi * sw).astype(BF16)
        vi_ref[j] = (vi * cw + vr * sw).astype(BF16)


def _ct_mid(br, bi, filt, order, fb, fbi, cw, sw, tk1=4, td=1024):
    n1, n2, d = br.shape
    nd = d // td
    data = pl.BlockSpec((tk1, n2, td), lambda i, j: (i, 0, j))
    coef = pl.BlockSpec((tk1, n2, td), lambda i, j: (i, 0, order * nd + j))
    mat = pl.BlockSpec(fb.shape, lambda i, j: (0, 0))
    tw = pl.BlockSpec((tk1, n2, TW_LANES), lambda i, j: (i, 0, 0))
    out = jax.ShapeDtypeStruct((n1, n2, d), BF16)
    return pl.pallas_call(
        _ct_mid_kernel,
        grid=(n1 // tk1, nd),
        in_specs=[data, data, coef, coef, coef, coef, mat, mat, tw, tw],
        out_specs=[data, data],
        out_shape=[out, out],
        compiler_params=_params(("parallel", "parallel"),
                                _vmem_limit(8 * _nbytes((tk1, n2, td), BF16),
                                            temps=10 * _nbytes((2 * n2, td), F32))),
        name="ct_mid",
    )(br, bi, *filt, fb, fbi, cw, sw)


def _ct_inv_a_kernel(scale, vr_ref, vi_ref, fai_ref, z0_ref, z1_ref, g0_ref, g1_ref, skip_ref,
                     o_ref):
    fai = fai_ref[...]
    half = fai.shape[0] // 2
    skip = skip_ref[...]
    vr, vi = vr_ref[...].astype(F32), vi_ref[...].astype(F32)
    zs, gs = (z0_ref[...], z1_ref[...]), (g0_ref[...], g1_ref[...])
    outs = ([], [])
    for h in range(CT_ROWS // SUB):
        v = jnp.concatenate([_sub_rows(vr, h), _sub_rows(vi, h)], axis=0).astype(BF16)
        y = _dot(fai, v) * scale
        for b, yb in enumerate((y[:half], y[half:])):
            outs[b].append(_sub_rows(gs[b], h) * (yb + skip * _sub_rows(zs[b], h)))
    for b in range(2):
        o_ref[b] = _from_sub_rows(outs[b]).astype(o_ref.dtype)


def _ct_inv_a(vr, vi, fai, z, z_which, gate, gate_which, skip, order, out_dtype):
    n1, n2, d = vr.shape
    rows = fai.shape[0] // (2 * SUB)
    tn2, tw = CT_ROWS, CT_COLS // 2
    spec = pl.BlockSpec((n1, tn2, tw), lambda i, c: (0, i, c))
    pair = lambda which, b: pl.BlockSpec((None, None, rows, tn2, tw), lambda i, c: (which, b, 0, i, c))
    return pl.pallas_call(
        functools.partial(_ct_inv_a_kernel, 1.0 / (n1 * n2)),
        grid=(n2 // tn2, d // tw),
        in_specs=[spec, spec, pl.BlockSpec(fai.shape, lambda i, c: (0, 0)),
                  pair(z_which, 0), pair(z_which, 1), pair(gate_which, 0), pair(gate_which, 1),
                  pl.BlockSpec((None, 1, tw), lambda i, c: (order, 0, c))],
        out_specs=pl.BlockSpec((2, rows, tn2, tw), lambda i, c: (0, 0, i, c)),
        out_shape=jax.ShapeDtypeStruct((2, rows, n2, d), out_dtype),
        compiler_params=_params(("parallel", "parallel"),
                                _vmem_limit(2 * _nbytes((n1, tn2, tw), BF16)
                                            + 6 * _nbytes((rows, tn2, tw), F32)
                                            + _nbytes(fai.shape, BF16),
                                            temps=8 * _nbytes((n1, tn2, tw), F32))),
        name="ct_inv_a",
    )(vr, vi, fai, z, z, gate, gate, skip)


def _ct_real_b_kernel(scale, br_ref, bi_ref, fb_ref, o_ref, so_ref):
    fb = fb_ref[...]
    for j in range(br_ref.shape[0]):
        so_ref[:, j, :] = _dot(fb, jnp.concatenate([br_ref[j], bi_ref[j]], axis=0)) * scale
    o_ref[...] = so_ref[...].astype(o_ref.dtype)


def _ct_real_b(br, bi, fb_re, scale):
    n_seq, n1, n2, d = br.shape
    tk1, tw = CT_ROWS, CT_COLS
    blk = pl.BlockSpec((None, tk1, n2, tw), lambda s, i, c: (s, i, 0, c))
    return pl.pallas_call(
        functools.partial(_ct_real_b_kernel, scale),
        grid=(n_seq, n1 // tk1, d // tw),
        in_specs=[blk, blk, pl.BlockSpec(fb_re.shape, lambda s, i, c: (0, 0))],
        out_specs=pl.BlockSpec((None, n2, tk1, tw), lambda s, i, c: (s, 0, i, c)),
        out_shape=jax.ShapeDtypeStruct((n_seq, n2, n1, d), BF16),
        scratch_shapes=[pltpu.VMEM((n2, tk1, tw), F32)],
        compiler_params=_params(("parallel", "parallel", "parallel"),
                                _vmem_limit(3 * _nbytes((tk1, n2, tw), BF16),
                                            resident=_nbytes((n2, tk1, tw), F32),
                                            temps=2 * _nbytes((n2, tk1, tw), F32))),
        name="ct_real_b",
    )(br, bi, fb_re)


def _cos_sin(num, den):
    ang = (num % den).astype(F32) * (2.0 * math.pi / den)
    return jnp.cos(ang), jnp.sin(ang)


def _ct_tables(n1, n2):
    i1 = jnp.arange(n1, dtype=jnp.int32)
    i2 = jnp.arange(n2, dtype=jnp.int32)
    c1, s1 = _cos_sin(i1[:, None] * i1[None, :], n1)
    c2, s2 = _cos_sin(i2[:, None] * i2[None, :], n2)
    cw, sw = _cos_sin(i2[:, None] * i1[None, :], n1 * n2)
    lanes = lambda t: jnp.broadcast_to(t[..., None], t.shape + (TW_LANES,))

    def stage_a_rows(t):
        t = t.reshape(n2 // CT_ROWS, CT_ROWS // SUB, SUB, n1)
        return lanes(jnp.swapaxes(t, 2, 3).reshape(n2 // CT_ROWS, CT_ROWS // SUB, n1 * SUB))

    return {"c1": c1, "s1": s1, "c2": c2, "s2": s2,
            "cw_a": stage_a_rows(cw), "sw_a": stage_a_rows(sw),
            "cw_b": lanes(cw.T), "sw_b": lanes(sw.T)}


def _shortconv_kernel(x_ref, w_ref, b_ref, o_ref):
    x = x_ref[...]
    n = x.shape[0]
    row = lax.broadcasted_iota(jnp.int32, x.shape, 0)
    before = jnp.where(row == 0, 0.0, pltpu.roll(x, 1, 0))
    after = jnp.where(row == n - 1, 0.0, pltpu.roll(x, n - 1, 0))
    o_ref[...] = w_ref[0:1, :] * before + w_ref[1:2, :] * x + w_ref[2:3, :] * after + b_ref[...]


def _shortconv(x, w, b, row0, n_seq, length, split):
    width = x.shape[1]
    tc = 256 if length >= 1024 else 2048
    s0 = row0 // length
    rows = n_seq * length
    if split:
        per = D_MODEL // tc
        out_spec = pl.BlockSpec((None, length, tc), lambda s, j: (j // per, s, j % per))
        out_shape = jax.ShapeDtypeStruct((width // D_MODEL, rows, D_MODEL), F32)
    else:
        out_spec = pl.BlockSpec((length, tc), lambda s, j: (s, j))
        out_shape = jax.ShapeDtypeStruct((rows, width), F32)
    return pl.pallas_call(
        _shortconv_kernel,
        grid=(n_seq, width // tc),
        in_specs=[pl.BlockSpec((length, tc), lambda s, j: (s0 + s, j)),
                  pl.BlockSpec((HY_CONV, tc), lambda s, j: (0, j)),
                  pl.BlockSpec((1, tc), lambda s, j: (0, j))],
        out_specs=out_spec,
        out_shape=out_shape,
        compiler_params=_params(("parallel", "parallel"),
                                _vmem_limit(2 * _nbytes((length, tc), F32),
                                            temps=4 * _nbytes((length, tc), F32))),
        name="shortconv",
    )(x, w, b.reshape(1, width))


def _filter_mlp_kernel(feat_ref, t_ref, w1_ref, b1_ref, fr1_ref, w2_ref, b2_ref, fr2_ref,
                       w3_ref, decay_ref, h_ref, ss_ref):
    x = jnp.sin(fr1_ref[...] * (_dot(feat_ref[...].astype(BF16), w1_ref[...].astype(BF16))
                                + b1_ref[...]))
    x = jnp.sin(fr2_ref[...] * (_dot(x.astype(BF16), w2_ref[...].astype(BF16)) + b2_ref[...]))
    h = _dot(x.astype(BF16), w3_ref[...].astype(BF16))
    h = h * (jnp.exp(-t_ref[...] * jnp.exp(decay_ref[...])) + HY_SHIFT)
    h_ref[...] = h

    @pl.when(pl.program_id(0) == 0)
    def _():
        ss_ref[...] = jnp.zeros_like(ss_ref)

    ss_ref[...] += jnp.sum(h * h, axis=0, keepdims=True)


def _filter_combine_kernel(hf_ref, hb_ref, ssf_ref, ssb_ref, a_ref, b_ref):
    norm = lax.rsqrt(ssf_ref[...] + ssb_ref[...] + 1e-12)
    fwd = hf_ref[...] * norm
    bwd = hb_ref[...] * norm
    row = lax.broadcasted_iota(jnp.int32, bwd.shape, 0) + pl.program_id(0) * bwd.shape[0]
    bwd = jnp.where(row == 0, 0.0, bwd)
    a_ref[...] = fwd + bwd
    b_ref[...] = fwd - bwd


def _hyena_filters(length, f_w1, f_b1, f_freq1, f_w2, f_b2, f_freq2, f_w3, log_decay):
    t = jnp.linspace(0.0, 1.0, length, dtype=F32)[:, None]
    t_idx = jnp.arange(length, dtype=F32)[:, None]
    bands = jnp.linspace(1e-4, HY_BANDS - 1, HY_BANDS, dtype=F32)
    w = 2.0 * math.pi * t_idx * bands / length
    feat = jnp.concatenate([t, jnp.cos(w), -jnp.sin(w)], axis=-1)
    emb_pad = 128
    feat = jnp.pad(feat, ((0, 0), (0, emb_pad - HY_EMB)))
    w1 = jnp.pad(f_w1, ((0, emb_pad - HY_EMB), (0, 0)))
    n_all = HY_DIRS * HY_ORDER * D_MODEL
    n_dir = HY_ORDER * D_MODEL
    tm = 256
    full = lambda shape: pl.BlockSpec(shape, lambda i: (0, 0))
    h, ss = pl.pallas_call(
        _filter_mlp_kernel,
        grid=(length // tm,),
        in_specs=[pl.BlockSpec((tm, emb_pad), lambda i: (i, 0)),
                  pl.BlockSpec((tm, 1), lambda i: (i, 0)),
                  full((emb_pad, HY_FW)), full((1, HY_FW)), full((1, HY_FW)),
                  full((HY_FW, HY_FW)), full((1, HY_FW)), full((1, HY_FW)),
                  full((HY_FW, n_all)), full((1, n_all))],
        out_specs=[pl.BlockSpec((tm, n_all), lambda i: (i, 0)), full((1, n_all))],
        out_shape=[jax.ShapeDtypeStruct((length, n_all), F32),
                   jax.ShapeDtypeStruct((1, n_all), F32)],
        compiler_params=_params(("arbitrary",),
                                _vmem_limit(_nbytes((tm, n_all), F32) + _nbytes((HY_FW, n_all), F32),
                                            temps=3 * _nbytes((tm, n_all), F32))),
        name="filter_mlp",
    )(feat, t, w1, f_b1.reshape(1, HY_FW), f_freq1.reshape(1, HY_FW), f_w2,
      f_b2.reshape(1, HY_FW), f_freq2.reshape(1, HY_FW), f_w3,
      log_decay.reshape(1, n_all))

    tn = 1024
    nb = n_dir // tn
    comb = jax.ShapeDtypeStruct((length, n_dir), F32)
    a, b = pl.pallas_call(
        _filter_combine_kernel,
        grid=(length // tm, nb),
        in_specs=[pl.BlockSpec((tm, tn), lambda i, j: (i, j)),
                  pl.BlockSpec((tm, tn), lambda i, j: (i, nb + j)),
                  pl.BlockSpec((1, tn), lambda i, j: (0, j)),
                  pl.BlockSpec((1, tn), lambda i, j: (0, nb + j))],
        out_specs=[pl.BlockSpec((tm, tn), lambda i, j: (i, j)),
                   pl.BlockSpec((tm, tn), lambda i, j: (i, j))],
        out_shape=[comb, comb],
        compiler_params=_params(("parallel", "parallel"), VMEM_FLOOR_BYTES),
        name="filter_combine",
    )(h, h, ss, ss)
    return a, b


def _cis_product(row_hi, row_lo, period):
    def cis(phase):
        ang = (phase % period).astype(F32) * (2.0 * math.pi / period)
        return jnp.cos(ang)[:, :, None], jnp.sin(ang)[:, :, None]
    (c1, s1), (c0, s0) = cis(row_hi), cis(row_lo)
    c0, s0 = jnp.swapaxes(c0, 1, 2), jnp.swapaxes(s0, 1, 2)
    rows = row_hi.shape[0]
    return ((c1 * c0 - s1 * s0).reshape(rows, -1), (s1 * c0 + c1 * s0).reshape(rows, -1))


def _odd_dft_tables(length):
    split = 1 << (length.bit_length() // 2)
    r = jnp.arange(length, dtype=jnp.int32)[:, None]
    hi = jnp.arange(length // split, dtype=jnp.int32)[None, :] * split
    lo = jnp.arange(split, dtype=jnp.int32)[None, :]
    c, s = _cis_product((2 * r + 1) * hi, (2 * r + 1) * lo, 4 * length)
    ct, st = _cis_product(r * (2 * hi), r * (2 * lo + 1), 4 * length)
    return {"c": c.astype(BF16), "s": s.astype(BF16), "ct": ct.astype(BF16), "st": st.astype(BF16)}


def _dft_tables(length):
    split = 1 << (length.bit_length() // 2)
    r = jnp.arange(length, dtype=jnp.int32)[:, None]
    hi = jnp.arange(length // split, dtype=jnp.int32)[None, :] * split
    lo = jnp.arange(split, dtype=jnp.int32)[None, :]
    c, s = _cis_product(r * hi, r * lo, length)
    return c.astype(BF16), (-s).astype(BF16)


def _hyena_mix_dense(proj, conv_w, conv_b, filt_p, filt_m, skip, row0, n_seq, length):
    tabs = _odd_dft_tables(length)
    kr, ks = _mm(tabs["c"], filt_p), _mm(tabs["s"], filt_m)
    pc = _shortconv(proj, conv_w, conv_b, row0, n_seq, length, split=False)
    v0, x1, x2 = 0, D_MODEL, 2 * D_MODEL
    yr, ys = _dft_fwd(tabs, pc, v0, kr, ks, 0, n_seq, length)
    z1 = _dft_inv(tabs, yr, ys, pc, v0, pc, x1, skip, 0, n_seq, length, F32)
    yr, ys = _dft_fwd(tabs, z1, 0, kr, ks, 1, n_seq, length)
    return _dft_inv(tabs, yr, ys, z1, 0, pc, x2, skip, 1, n_seq, length, BF16)


HY_N1, HY_N2 = 64, 128


def _hyena_mix_pair(proj, conv_w, conv_b, filt_p, filt_m, skip, row0):
    assert DEC_BATCH == 2 and HY_N1 * HY_N2 == 2 * DEC_SEQ
    n1, n2, rows_in = HY_N1, HY_N2, HY_N1 // 2
    t = _ct_tables(n1, n2)
    c_in, s_in = t["c1"][:, :rows_in], t["s1"][:, :rows_in]
    fa = _kron_sub(jnp.block([[c_in, s_in], [-s_in, c_in]])).astype(BF16)
    fa_real = _kron_sub(jnp.concatenate([c_in, -s_in], axis=0)).astype(BF16)
    fb = jnp.block([[t["c2"], t["s2"]], [-t["s2"], t["c2"]]]).astype(BF16)
    fbi = jnp.block([[t["c2"], -t["s2"]], [t["s2"], t["c2"]]]).astype(BF16)
    c_out, s_out = t["c1"][:rows_in], t["s1"][:rows_in]
    fai = _kron_sub(jnp.block([[c_out, -s_out], [s_out, c_out]])).astype(BF16)
    n_filt = HY_ORDER * D_MODEL
    filt = []
    for f in (filt_p, filt_m):
        re, im = _ct_stage_a([(f.reshape(rows_in, n2, n_filt), lambda s: ())], 1, fa_real,
                             t["cw_a"], t["sw_a"])
        filt += [re[0], im[0]]

    pc = _shortconv(proj, conv_w, conv_b, row0, DEC_BATCH, DEC_SEQ, split=True)
    pc = pc.reshape(3, DEC_BATCH, rows_in, n2, D_MODEL)
    z, out_dtypes = pc, (F32, BF16)
    for order in range(HY_ORDER):
        br, bi = _ct_stage_a([(z, lambda s: (0, 0)), (z, lambda s: (0, 1))], 1, fa,
                             t["cw_a"], t["sw_a"])
        vr, vi = _ct_mid(br[0], bi[0], filt, order, fb, fbi, t["cw_b"], t["sw_b"])
        z = _ct_inv_a(vr, vi, fai, z, 0, pc, 1 + order, skip, order, out_dtypes[order])[None]
    return z.reshape(ROWS_LAT, D_MODEL)


def _fnet_chan_kernel(h_ref, sh_ref, sc_ref, w_ref, p_ref, q_ref):
    u = _modulate(h_ref[...], sh_ref[...], sc_ref[...]).astype(BF16)
    w = w_ref[...]
    for g in range(FNET_GROUPS):
        cols = slice(g * FNET_CG, (g + 1) * FNET_CG)
        r = _dot(u[:, cols], w)
        p_ref[:, cols] = r[:, :FNET_CG]
        q_ref[:, cols] = r[:, FNET_CG:]


def _fnet_chan(h, mods, w_cs, tm=512):
    row = pl.BlockSpec((tm, D_MODEL), lambda i: (i, 0))
    out = jax.ShapeDtypeStruct((ROWS, D_MODEL), F32)
    return pl.pallas_call(
        _fnet_chan_kernel,
        grid=(ROWS // tm,),
        in_specs=[row, _mod_spec(0, tm), _mod_spec(1, tm),
                  pl.BlockSpec((FNET_CG, 2 * FNET_CG), lambda i: (0, 0))],
        out_specs=[row, row],
        out_shape=[out, out],
        compiler_params=_params(("parallel",),
                                _vmem_limit(3 * _nbytes((tm, D_MODEL), F32),
                                            temps=2 * _nbytes((tm, D_MODEL), F32))),
        name="fnet_chan",
    )(h, mods, mods, w_cs)


def _fnet_pos_kernel(scale, c_ref, ns_ref, p_ref, q_ref, o_ref, acc_ref):
    k = pl.program_id(3)

    @pl.when(k == 0)
    def _():
        acc_ref[...] = jnp.zeros_like(acc_ref)

    acc_ref[...] += (_dot(c_ref[...], p_ref[...].astype(BF16))
                     + _dot(ns_ref[...], q_ref[...].astype(BF16)))

    @pl.when(k == pl.num_programs(3) - 1)
    def _():
        o_ref[...] = (acc_ref[...] * scale).astype(o_ref.dtype)


def _fnet_pos(c_tab, ns_tab, p, q, n_seq, length):
    tt, tn, tk = _seq_tiles(length)
    nb, nt, nk = D_MODEL // tn, length // tt, length // tk
    scale = (length * FNET_CG) ** -0.5
    return pl.pallas_call(
        functools.partial(_fnet_pos_kernel, scale),
        grid=(n_seq, nt, nb, nk),
        in_specs=[pl.BlockSpec((tt, tk), lambda s, t, n, k: (t, k)),
                  pl.BlockSpec((tt, tk), lambda s, t, n, k: (t, k)),
                  pl.BlockSpec((tk, tn), lambda s, t, n, k: (s * nk + k, n)),
                  pl.BlockSpec((tk, tn), lambda s, t, n, k: (s * nk + k, n))],
        out_specs=pl.BlockSpec((tt, tn), lambda s, t, n, k: (s * nt + t, n)),
        out_shape=jax.ShapeDtypeStruct((n_seq * length, D_MODEL), BF16),
        scratch_shapes=[pltpu.VMEM((tt, tn), F32)],
        compiler_params=_params(("parallel", "parallel", "parallel", "arbitrary"),
                                _vmem_limit(2 * _nbytes((tt, tk), BF16) + 2 * _nbytes((tk, tn), BF16)
                                            + _nbytes((tt, tn), BF16),
                                            resident=_nbytes((tt, tn), F32),
                                            temps=2 * _nbytes((tt, tn), F32))),
        name="fnet_pos",
    )(c_tab, ns_tab, p, q)


FN_N1, FN_N2 = 32, 128


def _fnet_pos_factored(p, q):
    assert FN_N1 * FN_N2 == DEC_SEQ and ROWS % DEC_SEQ == 0
    n1, n2 = FN_N1, FN_N2
    t = _ct_tables(n1, n2)
    fa = _kron_sub(jnp.block([[t["c1"], -t["s1"]], [-t["s1"], -t["c1"]]])).astype(BF16)
    fb_re = jnp.concatenate([t["c2"], t["s2"]], axis=1).astype(BF16)
    lat0 = ROWS_CTX // DEC_SEQ
    view = lambda x: x.reshape(ROWS // DEC_SEQ, n1, n2, D_MODEL)
    seq = lambda s: (lat0 + s,)
    br, bi = _ct_stage_a([(view(p), seq), (view(q), seq)], DEC_BATCH, fa, t["cw_a"], t["sw_a"])
    f = _ct_real_b(br, bi, fb_re, (DEC_SEQ * FNET_CG) ** -0.5)
    return f.reshape(ROWS_LAT, D_MODEL)


def _rope_table():
    rows = DEC_SEQ // GRID_W
    row = jnp.repeat(jnp.arange(rows), GRID_W).astype(F32)
    col = jnp.tile(jnp.arange(GRID_W), rows).astype(F32)
    inv = ROPE_THETA ** (-jnp.arange(0, AXIS_ROPE, 2, dtype=F32) / AXIS_ROPE)
    ang = jnp.concatenate([row[:, None] * inv, col[:, None] * inv], axis=-1)
    cos = jnp.repeat(jnp.cos(ang), 2, axis=-1)
    sin = jnp.repeat(jnp.sin(ang), 2, axis=-1)
    lat = jnp.tile(jnp.concatenate([cos, sin], axis=-1), (DEC_BATCH, 1))
    ctx = jnp.concatenate([jnp.ones((ROWS_CTX, QK_ROPE), F32), jnp.zeros((ROWS_CTX, QK_ROPE), F32)],
                          axis=-1)
    return jnp.concatenate([ctx, lat], axis=0)


def _pair_rotated(w):
    pairs = w.reshape(w.shape[:-1] + (QK_ROPE // 2, 2))
    return jnp.stack([-pairs[..., 1], pairs[..., 0]], axis=-1).reshape(w.shape)


def kernel(x_prompt, x_sample, c, cache_ckv, cache_krope, c_ctx, ada_w, ada_b, ln_g, ln_b, ffn_w_gate, ffn_w_up, ffn_w_down, mla_w_dq, mla_q_norm, mla_w_uq, mla_w_dkv, mla_kv_norm, mla_w_kr, mla_w_ukv, mla_w_o, hy_w_in, hy_b_in, hy_conv_w, hy_conv_b, hy_f_w1, hy_f_b1, hy_f_freq1, hy_f_w2, hy_f_b2, hy_f_freq2, hy_f_w3, hy_log_decay, hy_skip, hy_w_out, hy_b_out, fn_w_out, fn_b_out):
    assert x_prompt.shape == (BATCH, SEQ, D_MODEL) and x_sample.shape == (DEC_BATCH, DEC_SEQ, D_MODEL)
    assert ROWS_CTX % DEC_SEQ == 0 and SEQ == FNET_CG

    h = jnp.concatenate([x_prompt.reshape(ROWS_CTX, D_MODEL), x_sample.reshape(ROWS_LAT, D_MODEL)])
    cond = jnp.concatenate([c_ctx[None, :], c, jnp.zeros((COND_PAD - N_COND, D_MODEL), F32)])
    mods_all = _modulation_vectors(cond, ada_w, ada_b)
    zero_bias = jnp.zeros((D_MODEL,), F32)
    rope_tab = None
    ckv_states, krope_states = [], []

    for i in range(DEPTH):
        kind, j = i % N_MIXERS, i // N_MIXERS
        mods = mods_all[i]
        if kind == 0:
            if rope_tab is None:
                rope_tab = _rope_table()
            w_kr2 = jnp.concatenate([mla_w_kr[j], _pair_rotated(mla_w_kr[j])], axis=-1).astype(BF16)
            wq = mla_w_uq[j].reshape(Q_RANK, MLA_HEADS, QK_NOPE + QK_ROPE)
            w_q = jnp.concatenate([wq, _pair_rotated(wq[..., QK_NOPE:])], axis=-1)
            w_q = w_q.reshape(Q_RANK, MLA_HEADS * HEAD_W).astype(BF16)
            w_ukv = mla_w_ukv[j].reshape(KV_RANK, MLA_HEADS, QK_NOPE + V_DIM)
            w_k = w_ukv[..., :QK_NOPE].reshape(KV_RANK, MLA_HEADS * QK_NOPE).astype(BF16)
            w_vt = w_ukv[..., QK_NOPE:].reshape(KV_RANK, MLA_HEADS * V_DIM).T.astype(BF16)
            cq, ckv, kr, kr2 = _mla_down(h, mods, mla_w_dq[j].astype(BF16), mla_w_dkv[j].astype(BF16),
                                         w_kr2, mla_q_norm[j], mla_kv_norm[j], rope_tab)
            ckv_states.append(ckv[:ROWS_CTX].reshape(BATCH, SEQ, KV_RANK))
            krope_states.append(kr[:ROWS_CTX].reshape(BATCH, SEQ, QK_ROPE))
            q = _q_up(cq, w_q, rope_tab)
            k_tok, vt_tok = _kv_expand(ckv, kr2, w_k, w_vt)
            kc = cache_krope[:, j].reshape(DEC_BATCH * PAST_LEN, QK_ROPE).astype(BF16)
            k_cache, vt_cache = _kv_expand(cache_ckv[:, j].reshape(DEC_BATCH * PAST_LEN, KV_RANK),
                                           jnp.concatenate([kc, kc], axis=-1), w_k, w_vt)
            o = _attention(q, k_tok, vt_tok, k_cache, vt_cache)
            h = _mm_postnorm(o, mla_w_o[j].astype(BF16), zero_bias, h, mods, 2, ln_g[i, 0], ln_b[i, 0])
        elif kind == 1:
            proj = _mod_mm(h, mods, hy_w_in[j].astype(BF16), hy_b_in[j], 0)
            fp = (hy_f_w1[j], hy_f_b1[j], hy_f_freq1[j], hy_f_w2[j], hy_f_b2[j], hy_f_freq2[j],
                  hy_f_w3[j], hy_log_decay[j])
            skip = hy_skip[j].reshape(HY_ORDER, 1, D_MODEL)
            conv = (hy_conv_w[j], hy_conv_b[j])
            z_ctx = _hyena_mix_dense(proj, *conv, *_hyena_filters(SEQ, *fp), skip, 0, BATCH, SEQ)
            z_lat = _hyena_mix_pair(proj, *conv, *_hyena_filters(DEC_SEQ, *fp), skip, ROWS_CTX)
            h = _mm_postnorm((z_ctx, z_lat), hy_w_out[j].astype(BF16), hy_b_out[j],
                             h, mods, 2, ln_g[i, 0], ln_b[i, 0])
        else:
            c_ch, ns_ch = _dft_tables(FNET_CG)
            p, q = _fnet_chan(h, mods, jnp.concatenate([c_ch, -ns_ch], axis=-1))
            f_ctx = _fnet_pos(c_ch, ns_ch, p, q, BATCH, SEQ)
            f_lat = _fnet_pos_factored(p, q)
            h = _mm_postnorm((f_ctx, f_lat), fn_w_out[j].astype(BF16), fn_b_out[j],
                             h, mods, 2, ln_g[i, 0], ln_b[i, 0])
        h = _ffn(h, mods, ffn_w_gate[i].astype(BF16), ffn_w_up[i].astype(BF16),
                 ffn_w_down[i].astype(BF16), ln_g[i, 1], ln_b[i, 1], split_out=i == DEPTH - 1)

    y_prompt = h[0].reshape(BATCH, SEQ, D_MODEL)
    y_sample = h[1].reshape(DEC_BATCH, DEC_SEQ, D_MODEL)
    return (y_prompt, y_sample, jnp.stack(ckv_states, axis=1), jnp.stack(krope_states, axis=1))
```

```python
import functools
import math

import jax
import jax.numpy as jnp
from jax import lax
from jax.experimental import pallas as pl
from jax.experimental.pallas import tpu as pltpu

F32 = jnp.float32
BF16 = jnp.bfloat16

D_MODEL = 2048
BATCH = 16
SEQ = 256
DEPTH = 4
DEC_BATCH = 2
DEC_SEQ = 4096
PAST_LEN = 512
GRID_W = 64
N_MIXERS = 3
MLA_HEADS = 16
QK_NOPE = 128
QK_ROPE = 64
V_DIM = 128
Q_RANK = 512
KV_RANK = 512
ROPE_THETA = 10000.0
AXIS_ROPE = QK_ROPE // 2
HY_ORDER = 2
HY_DIRS = 2
HY_CONV = 3
HY_BANDS = 16
HY_EMB = 1 + 2 * HY_BANDS
HY_FW = 64
HY_SHIFT = 0.05
FNET_GROUPS = 8
FNET_CG = D_MODEL // FNET_GROUPS
D_FF = -(-8 * D_MODEL // (3 * 256)) * 256
DN_ALPHA = (2 * DEPTH) ** 0.25
LN_EPS = 1e-5
RMS_EPS = 1e-6
N_MOD = 6

ROWS_CTX = BATCH * SEQ
ROWS_LAT = DEC_BATCH * DEC_SEQ
ROWS = ROWS_CTX + ROWS_LAT
N_COND = 1 + DEC_BATCH
COND_PAD = 8
HEAD_W = QK_NOPE + 2 * QK_ROPE
ATT_SCALE = (QK_NOPE + QK_ROPE) ** -0.5

V7X_VMEM_BYTES = 64 * 2 ** 20
VMEM_CAP_BYTES = V7X_VMEM_BYTES * 7 // 8
VMEM_FLOOR_BYTES = 32 * 2 ** 20


def _vmem_limit(pipelined, resident=0, temps=0):
    est = 2 * pipelined + resident + temps
    return int(min(max(est, VMEM_FLOOR_BYTES), VMEM_CAP_BYTES))


def _params(semantics, vmem):
    return pltpu.CompilerParams(dimension_semantics=semantics, vmem_limit_bytes=vmem)


def _nbytes(shape, dtype):
    return math.prod(shape) * jnp.dtype(dtype).itemsize


def _group_of_tile(i, tm):
    n_ctx = ROWS_CTX // tm
    return jnp.where(i < n_ctx, 0, 1 + (i - n_ctx) // (DEC_SEQ // tm))


def _mod_spec(which, tm):
    return pl.BlockSpec((None, 1, D_MODEL),
                        lambda i, *_: (which * COND_PAD + _group_of_tile(i, tm), 0, 0))


def _row_spec(width=D_MODEL):
    return pl.BlockSpec((1, width), lambda *_: (0, 0))


def _modulate(h, shift, scale):
    return h * (1.0 + scale) + shift


def _post_norm(h, delta, g, b):
    z = DN_ALPHA * h + delta
    mu = jnp.mean(z, axis=-1, keepdims=True)
    zc = z - mu
    var = jnp.mean(zc * zc, axis=-1, keepdims=True)
    return zc * lax.rsqrt(var + LN_EPS) * g + b


def _rms_norm(x, g):
    ms = jnp.mean(x * x, axis=-1, keepdims=True)
    return x * lax.rsqrt(ms + RMS_EPS) * g


def _dot(a, b):
    return jnp.dot(a, b, preferred_element_type=F32)


def _modvec_kernel(c_ref, w_ref, b_ref, o_ref):
    a = jax.nn.silu(c_ref[...]).astype(BF16)
    o_ref[...] = _dot(a, w_ref[...].astype(BF16)) + b_ref[...]


def _modulation_vectors(cond, ada_w, ada_b):
    tn = 1024
    n = N_MOD * D_MODEL
    out = pl.pallas_call(
        _modvec_kernel,
        grid=(DEPTH, n // tn),
        in_specs=[pl.BlockSpec((COND_PAD, D_MODEL), lambda l, j: (0, 0)),
                  pl.BlockSpec((None, D_MODEL, tn), lambda l, j: (l, 0, j)),
                  pl.BlockSpec((None, 1, tn), lambda l, j: (l, 0, j))],
        out_specs=pl.BlockSpec((None, COND_PAD, tn), lambda l, j: (l, 0, j)),
        out_shape=jax.ShapeDtypeStruct((DEPTH, COND_PAD, n), F32),
        compiler_params=_params(("parallel", "parallel"),
                                _vmem_limit(_nbytes((D_MODEL, tn), F32),
                                            temps=_nbytes((D_MODEL, tn), BF16))),
        name="modvec",
    )(cond, ada_w, ada_b.reshape(DEPTH, 1, n))
    out = out.reshape(DEPTH, COND_PAD, N_MOD, D_MODEL).transpose(0, 2, 1, 3)
    return out.reshape(DEPTH, N_MOD * COND_PAD, 1, D_MODEL)


def _mod_mm_kernel(h_ref, sh_ref, sc_ref, w_ref, b_ref, o_ref, u_ref):
    @pl.when(pl.program_id(1) == 0)
    def _():
        u_ref[...] = _modulate(h_ref[...], sh_ref[...], sc_ref[...]).astype(BF16)

    o_ref[...] = (_dot(u_ref[...], w_ref[...]) + b_ref[...]).astype(o_ref.dtype)


def _mod_mm(h, mods, w, b, which_shift, out_dtype=F32, tm=1024, tn=1024):
    k, n = w.shape
    return pl.pallas_call(
        _mod_mm_kernel,
        grid=(ROWS // tm, n // tn),
        in_specs=[pl.BlockSpec((tm, k), lambda i, j: (i, 0)),
                  _mod_spec(which_shift, tm), _mod_spec(which_shift + 1, tm),
                  pl.BlockSpec((k, tn), lambda i, j: (0, j)),
                  pl.BlockSpec((1, tn), lambda i, j: (0, j))],
        out_specs=pl.BlockSpec((tm, tn), lambda i, j: (i, j)),
        out_shape=jax.ShapeDtypeStruct((ROWS, n), out_dtype),
        scratch_shapes=[pltpu.VMEM((tm, k), BF16)],
        compiler_params=_params(("parallel", "arbitrary"),
                                _vmem_limit(_nbytes((tm, k), F32) + _nbytes((k, tn), BF16)
                                            + _nbytes((tm, tn), F32),
                                            resident=_nbytes((tm, k), BF16),
                                            temps=_nbytes((tm, k), F32))),
        name="mod_mm",
    )(h, mods, mods, w, b.reshape(1, n))


def _mm_postnorm_kernel(n_ctx, *refs):
    n_a = 1 if n_ctx is None else 2
    a_refs, (w_ref, bias_ref, h_ref, gate_ref, g_ref, b_ref, o_ref) = refs[:n_a], refs[n_a:]
    half = o_ref.shape[0] // 2
    halves = (slice(0, half), slice(half, 2 * half))

    def rows(r):
        if n_ctx is None:
            return a_refs[0][r, :]
        return jnp.where(pl.program_id(0) < n_ctx, a_refs[0][r, :], a_refs[1][r, :])

    ys = [_dot(rows(r), w_ref[...]) + bias_ref[...] for r in halves]
    for r, y in zip(halves, ys):
        o_ref[r, :] = _post_norm(h_ref[r, :], gate_ref[...] * y, g_ref[...], b_ref[...])


def _mm_postnorm(a, w, bias, h, mods, which_gate, ln_g, ln_b, tm=512):
    if isinstance(a, tuple):
        n_ctx = ROWS_CTX // tm
        k = a[0].shape[1]
        a_specs = [pl.BlockSpec((tm, k), lambda i: (jnp.minimum(i, n_ctx - 1), 0)),
                   pl.BlockSpec((tm, k), lambda i: (jnp.maximum(i - n_ctx, 0), 0))]
    else:
        n_ctx, k, a = None, a.shape[1], (a,)
        a_specs = [pl.BlockSpec((tm, k), lambda i: (i, 0))]
    return pl.pallas_call(
        functools.partial(_mm_postnorm_kernel, n_ctx),
        grid=(ROWS // tm,),
        in_specs=a_specs + [pl.BlockSpec((k, D_MODEL), lambda i: (0, 0)),
                            _row_spec(),
                            pl.BlockSpec((tm, D_MODEL), lambda i: (i, 0)),
                            _mod_spec(which_gate, tm), _row_spec(), _row_spec()],
        out_specs=pl.BlockSpec((tm, D_MODEL), lambda i: (i, 0)),
        out_shape=jax.ShapeDtypeStruct((ROWS, D_MODEL), F32),
        compiler_params=_params(("parallel",),
                                _vmem_limit(len(a) * _nbytes((tm, k), BF16)
                                            + _nbytes((k, D_MODEL), BF16)
                                            + 2 * _nbytes((tm, D_MODEL), F32),
                                            temps=3 * _nbytes((tm, D_MODEL), F32))),
        name="mm_postnorm",
    )(*a, w, bias.reshape(1, D_MODEL), h, mods, ln_g.reshape(1, D_MODEL), ln_b.reshape(1, D_MODEL))


def _ffn_kernel(n_ctx, h_ref, sh_ref, sc_ref, gate_ref, g_ref, b_ref, wg_ref, wu_ref, wd_ref,
                *refs):
    i, f = pl.program_id(0), pl.program_id(1)
    if n_ctx is None:
        o_ref, u_ref = refs
        acc_ref, outs = o_ref, ((o_ref, None),)
    else:
        octx_ref, olat_ref, u_ref, acc_ref = refs
        outs = ((octx_ref, i < n_ctx), (olat_ref, i >= n_ctx))

    @pl.when(f == 0)
    def _():
        u_ref[...] = _modulate(h_ref[...], sh_ref[...], sc_ref[...]).astype(BF16)
        acc_ref[...] = jnp.zeros_like(acc_ref)

    half = u_ref.shape[0] // 2
    halves = (slice(0, half), slice(half, 2 * half))
    proj = [(_dot(u_ref[r, :], wg_ref[...]), _dot(u_ref[r, :], wu_ref[...])) for r in halves]
    for r, (gate, up) in zip(halves, proj):
        act = (jax.nn.silu(gate) * up).astype(BF16)
        acc_ref[r, :] += _dot(act, wd_ref[...])

    last = f == pl.num_programs(1) - 1
    for o_ref, mine in outs:
        @pl.when(last if mine is None else jnp.logical_and(last, mine))
        def _(o_ref=o_ref):
            for r in halves:
                o_ref[r, :] = _post_norm(h_ref[r, :], gate_ref[...] * acc_ref[r, :],
                                         g_ref[...], b_ref[...])


def _ffn(h, mods, w_gate, w_up, w_down, layer, ln_g, ln_b, split_out, tm=512, tf=512):
    tile = _nbytes((tm, D_MODEL), F32)
    if split_out:
        n_ctx = ROWS_CTX // tm
        out_specs = [pl.BlockSpec((tm, D_MODEL), lambda i, f: (jnp.minimum(i, n_ctx - 1), 0)),
                     pl.BlockSpec((tm, D_MODEL), lambda i, f: (jnp.maximum(i - n_ctx, 0), 0))]
        out_shape = [jax.ShapeDtypeStruct((ROWS_CTX, D_MODEL), F32),
                     jax.ShapeDtypeStruct((ROWS_LAT, D_MODEL), F32)]
        scratch = [pltpu.VMEM((tm, D_MODEL), BF16), pltpu.VMEM((tm, D_MODEL), F32)]
        pipelined, resident = 3 * tile, tile + tile // 2
    else:
        n_ctx = None
        out_specs = pl.BlockSpec((tm, D_MODEL), lambda i, f: (i, 0))
        out_shape = jax.ShapeDtypeStruct((ROWS, D_MODEL), F32)
        scratch = [pltpu.VMEM((tm, D_MODEL), BF16)]
        pipelined, resident = 2 * tile, tile // 2
    return pl.pallas_call(
        functools.partial(_ffn_kernel, n_ctx),
        grid=(ROWS // tm, D_FF // tf),
        in_specs=[pl.BlockSpec((tm, D_MODEL), lambda i, f: (i, 0)),
                  _mod_spec(3, tm), _mod_spec(4, tm), _mod_spec(5, tm),
                  _row_spec(), _row_spec(),
                  pl.BlockSpec((None, D_MODEL, tf), lambda i, f: (layer, 0, f)),
                  pl.BlockSpec((None, D_MODEL, tf), lambda i, f: (layer, 0, f)),
                  pl.BlockSpec((None, tf, D_MODEL), lambda i, f: (layer, f, 0))],
        out_specs=out_specs,
        out_shape=out_shape,
        scratch_shapes=scratch,
        compiler_params=_params(("parallel", "arbitrary"),
                                _vmem_limit(pipelined + 3 * _nbytes((D_MODEL, tf), BF16),
                                            resident=resident, temps=2 * tile)),
        name="ffn",
    )(h, mods, mods, mods, ln_g.reshape(1, D_MODEL), ln_b.reshape(1, D_MODEL),
      w_gate, w_up, w_down)


def _mla_down_kernel(h_ref, sh_ref, sc_ref, wdq_ref, wdkv_ref, wkr_ref, qn_ref, kvn_ref,
                     rope_ref, cq_ref, ckv_ref, kr_ref, kr2_ref):
    u = _modulate(h_ref[...], sh_ref[...], sc_ref[...]).astype(BF16)
    cq_ref[...] = _rms_norm(_dot(u, wdq_ref[...]), qn_ref[...]).astype(BF16)
    ckv_ref[...] = _rms_norm(_dot(u, wdkv_ref[...]), kvn_ref[...])
    t = _dot(u, wkr_ref[...])
    kr_ref[...] = t[:, :QK_ROPE]
    v = t * rope_ref[...]
    kr2_ref[...] = (v + pltpu.roll(v, QK_ROPE, 1)).astype(BF16)


def _mla_down(h, mods, w_dq, w_dkv, w_kr2, q_norm, kv_norm, rope_tab, tm=512):
    row = lambda width: pl.BlockSpec((tm, width), lambda i: (i, 0))
    full = lambda shape: pl.BlockSpec(shape, lambda i: (0, 0))
    return pl.pallas_call(
        _mla_down_kernel,
        grid=(ROWS // tm,),
        in_specs=[row(D_MODEL), _mod_spec(0, tm), _mod_spec(1, tm),
                  full((D_MODEL, Q_RANK)), full((D_MODEL, KV_RANK)), full((D_MODEL, 2 * QK_ROPE)),
                  _row_spec(Q_RANK), _row_spec(KV_RANK), row(2 * QK_ROPE)],
        out_specs=[row(Q_RANK), row(KV_RANK), row(QK_ROPE), row(2 * QK_ROPE)],
        out_shape=[jax.ShapeDtypeStruct((ROWS, Q_RANK), BF16),
                   jax.ShapeDtypeStruct((ROWS, KV_RANK), F32),
                   jax.ShapeDtypeStruct((ROWS, QK_ROPE), F32),
                   jax.ShapeDtypeStruct((ROWS, 2 * QK_ROPE), BF16)],
        compiler_params=_params(("parallel",),
                                _vmem_limit(_nbytes((tm, D_MODEL), F32)
                                            + _nbytes((D_MODEL, Q_RANK + KV_RANK + 2 * QK_ROPE), BF16)
                                            + 3 * _nbytes((tm, KV_RANK), F32),
                                            temps=2 * _nbytes((tm, D_MODEL), F32))),
        name="mla_down",
    )(h, mods, mods, w_dq, w_dkv, w_kr2, q_norm.reshape(1, Q_RANK), kv_norm.reshape(1, KV_RANK),
      rope_tab)


NT_DIMS = (((1,), (1,)), ((), ()))


def _q_up_kernel(cq_ref, w_ref, rope_ref, q_ref):
    cq, tab = cq_ref[...], rope_ref[...] * LOG2E_SCALE
    for h in range(MLA_HEADS):
        r = _dot(cq, w_ref[:, h * HEAD_W:(h + 1) * HEAD_W])
        q_ref[h, :, :QK_NOPE] = (r[:, :QK_NOPE] * LOG2E_SCALE).astype(BF16)
        q_ref[h, :, QK_NOPE:] = (r[:, QK_NOPE:] * tab).astype(BF16)


def _q_up(cq, w_q, rope_tab, tm=512):
    return pl.pallas_call(
        _q_up_kernel,
        grid=(ROWS // tm,),
        in_specs=[pl.BlockSpec((tm, Q_RANK), lambda i: (i, 0)),
                  pl.BlockSpec((Q_RANK, MLA_HEADS * HEAD_W), lambda i: (0, 0)),
                  pl.BlockSpec((tm, 2 * QK_ROPE), lambda i: (i, 0))],
        out_specs=pl.BlockSpec((MLA_HEADS, tm, HEAD_W), lambda i: (0, i, 0)),
        out_shape=jax.ShapeDtypeStruct((MLA_HEADS, ROWS, HEAD_W), BF16),
        compiler_params=_params(("parallel",),
                                _vmem_limit(_nbytes((Q_RANK + tm, MLA_HEADS * HEAD_W), BF16))),
        name="q_up",
    )(cq, w_q, rope_tab)


def _kv_expand_kernel(ckv_ref, kr2_ref, wk_ref, wvt_ref, k_ref, vt_ref):
    c = ckv_ref[...].astype(BF16)
    vt_ref[...] = lax.dot_general(wvt_ref[...], c, NT_DIMS,
                                  preferred_element_type=F32).astype(BF16)
    kr2 = kr2_ref[...]
    pair_w = 2 * QK_NOPE
    for g in range(MLA_HEADS // 2):
        r = _dot(c, wk_ref[:, g * pair_w:(g + 1) * pair_w]).astype(BF16)
        for e in range(2):
            k_ref[2 * g + e, :, :QK_NOPE] = r[:, e * QK_NOPE:(e + 1) * QK_NOPE]
            k_ref[2 * g + e, :, QK_NOPE:] = kr2


def _kv_expand(ckv, kr2, w_k, w_vt, tm=512):
    rows = ckv.shape[0]
    return pl.pallas_call(
        _kv_expand_kernel,
        grid=(rows // tm,),
        in_specs=[pl.BlockSpec((tm, KV_RANK), lambda i: (i, 0)),
                  pl.BlockSpec((tm, 2 * QK_ROPE), lambda i: (i, 0)),
                  pl.BlockSpec((KV_RANK, MLA_HEADS * QK_NOPE), lambda i: (0, 0)),
                  pl.BlockSpec((MLA_HEADS * V_DIM, KV_RANK), lambda i: (0, 0))],
        out_specs=[pl.BlockSpec((MLA_HEADS, tm, HEAD_W), lambda i: (0, i, 0)),
                   pl.BlockSpec((MLA_HEADS * V_DIM, tm), lambda i: (0, i))],
        out_shape=[jax.ShapeDtypeStruct((MLA_HEADS, rows, HEAD_W), BF16),
                   jax.ShapeDtypeStruct((MLA_HEADS * V_DIM, rows), BF16)],
        compiler_params=_params(("parallel",),
                                _vmem_limit(_nbytes((tm, MLA_HEADS * (HEAD_W + V_DIM)), BF16)
                                            + 2 * _nbytes((KV_RANK, MLA_HEADS * V_DIM), BF16),
                                            temps=_nbytes((MLA_HEADS * V_DIM, tm), F32))),
        name="kv_expand",
    )(ckv, kr2, w_k, w_vt)


ATT_CHUNK = 512
ATT_SKEW = 3
LOG2E_SCALE = ATT_SCALE * math.log2(math.e)


def _attn_scores(q, k):
    return lax.dot_general(k, q, NT_DIMS, preferred_element_type=F32)


def _attn_values(s, vt, carry):
    m = jnp.max(s, axis=0, keepdims=True)
    if carry is not None:
        m_old, l_old, acc_old = carry
        m = jnp.maximum(m_old, m)
    p = jnp.exp2(s - m)
    l = jnp.sum(p, axis=0, keepdims=True)
    acc = _dot(vt, p.astype(BF16))
    if carry is not None:
        alpha = jnp.exp2(m_old - m)
        l = alpha * l_old + l
        acc = alpha * acc_old + acc
    return m, l, acc


def _attn_ctx_kernel(q_ref, k_ref, vt_ref, o_ref):
    for h in range(MLA_HEADS):
        _, l, acc = _attn_values(_attn_scores(q_ref[h], k_ref[h]),
                                 vt_ref[h * V_DIM:(h + 1) * V_DIM, :], None)
        o_ref[:, h * V_DIM:(h + 1) * V_DIM] = (acc / l).T.astype(BF16)


def _attn_lat_kernel(q_ref, k_ref, vt_ref, kc_ref, vtc_ref, prev_ref, o_ref):
    del prev_ref
    q = q_ref[...]
    n_tok = DEC_SEQ // ATT_CHUNK
    rows = lambda c: slice(c * ATT_CHUNK, (c + 1) * ATT_CHUNK)
    keys = [k_ref.at[rows(c), :] for c in range(n_tok)] + [kc_ref]
    vals = [vt_ref.at[:, rows(c)] for c in range(n_tok)] + [vtc_ref]
    carry = None
    scores = [_attn_scores(q, keys[c][...]) for c in range(ATT_SKEW)]
    for c in range(n_tok + 1):
        if c + ATT_SKEW <= n_tok:
            scores.append(_attn_scores(q, keys[c + ATT_SKEW][...]))
        carry = _attn_values(scores[c], vals[c][...], carry)
    _, l, acc = carry
    o_ref[...] = (acc / l).T.astype(BF16)


def _attention(q, k_tok, vt_tok, k_cache, vt_cache, tq=1024):
    assert PAST_LEN == ATT_CHUNK
    out_shape = jax.ShapeDtypeStruct((ROWS, MLA_HEADS * V_DIM), BF16)
    o = pl.pallas_call(
        _attn_ctx_kernel,
        grid=(BATCH,),
        in_specs=[pl.BlockSpec((MLA_HEADS, SEQ, HEAD_W), lambda s: (0, s, 0)),
                  pl.BlockSpec((MLA_HEADS, SEQ, HEAD_W), lambda s: (0, s, 0)),
                  pl.BlockSpec((MLA_HEADS * V_DIM, SEQ), lambda s: (0, s))],
        out_specs=pl.BlockSpec((SEQ, MLA_HEADS * V_DIM), lambda s: (s, 0)),
        out_shape=out_shape,
        compiler_params=_params(("parallel",), VMEM_FLOOR_BYTES),
        name="attn_ctx",
    )(q, k_tok, vt_tok)

    lat0 = ROWS_CTX // DEC_SEQ
    q0 = ROWS_CTX // tq
    nq = DEC_SEQ // tq
    return pl.pallas_call(
        _attn_lat_kernel,
        grid=(DEC_BATCH, MLA_HEADS, nq),
        in_specs=[pl.BlockSpec((None, tq, HEAD_W), lambda b, h, i: (h, q0 + b * nq + i, 0)),
                  pl.BlockSpec((None, DEC_SEQ, HEAD_W), lambda b, h, i: (h, lat0 + b, 0)),
                  pl.BlockSpec((V_DIM, DEC_SEQ), lambda b, h, i: (h, lat0 + b)),
                  pl.BlockSpec((None, PAST_LEN, HEAD_W), lambda b, h, i: (h, b, 0)),
                  pl.BlockSpec((V_DIM, PAST_LEN), lambda b, h, i: (h, b)),
                  pl.BlockSpec(memory_space=pl.ANY)],
        out_specs=pl.BlockSpec((tq, V_DIM), lambda b, h, i: (q0 + b * nq + i, h)),
        out_shape=out_shape,
        input_output_aliases={5: 0},
        compiler_params=_params(("parallel", "parallel", "arbitrary"),
                                _vmem_limit(_nbytes((DEC_SEQ + PAST_LEN, HEAD_W + V_DIM), BF16),
                                            temps=8 * _nbytes((ATT_CHUNK, tq), F32))),
        name="attn_lat",
    )(q, k_tok, vt_tok, k_cache, vt_cache, o)


def _mm_kernel(a_ref, b_ref, o_ref, acc_ref):
    k = pl.program_id(2)

    @pl.when(k == 0)
    def _():
        acc_ref[...] = jnp.zeros_like(acc_ref)

    acc_ref[...] += _dot(a_ref[...], b_ref[...].astype(BF16))

    @pl.when(k == pl.num_programs(2) - 1)
    def _():
        o_ref[...] = acc_ref[...].astype(o_ref.dtype)


def _mm(a, b, out_dtype=F32, tm=1024, tn=1024, tk=512):
    m, kk = a.shape
    n = b.shape[1]
    tm, tn, tk = min(tm, m), min(tn, n), min(tk, kk)
    return pl.pallas_call(
        _mm_kernel,
        grid=(m // tm, n // tn, kk // tk),
        in_specs=[pl.BlockSpec((tm, tk), lambda i, j, k: (i, k)),
                  pl.BlockSpec((tk, tn), lambda i, j, k: (k, j))],
        out_specs=pl.BlockSpec((tm, tn), lambda i, j, k: (i, j)),
        out_shape=jax.ShapeDtypeStruct((m, n), out_dtype),
        scratch_shapes=[pltpu.VMEM((tm, tn), F32)],
        compiler_params=_params(("parallel", "parallel", "arbitrary"), VMEM_FLOOR_BYTES),
        name="mm",
    )(a, b)


def _seq_tiles(length):
    if length >= 1024:
        return 1024, 512, 1024
    return length, D_MODEL, length


def _dft_fwd_kernel(c_ref, s_ref, z_ref, kr_ref, ks_ref, yr_ref, ys_ref, accr_ref, accs_ref):
    k = pl.program_id(3)

    @pl.when(k == 0)
    def _():
        accr_ref[...] = jnp.zeros_like(accr_ref)
        accs_ref[...] = jnp.zeros_like(accs_ref)

    z = z_ref[...].astype(BF16)
    accr_ref[...] += _dot(c_ref[...], z)
    accs_ref[...] += _dot(s_ref[...], z)

    @pl.when(k == pl.num_programs(3) - 1)
    def _():
        zr, zs, kr, ks = accr_ref[...], accs_ref[...], kr_ref[...], ks_ref[...]
        yr_ref[...] = (zr * kr - zs * ks).astype(BF16)
        ys_ref[...] = (zr * ks + zs * kr).astype(BF16)


def _dft_fwd(tabs, z, z_col0, kr, ks, order, n_seq, length):
    tf, tn, tk = _seq_tiles(length)
    nb, nf, nk = D_MODEL // tn, length // tf, length // tk
    zc0, kc0 = z_col0 // tn, order * nb
    out = jax.ShapeDtypeStruct((n_seq * length, D_MODEL), BF16)
    return pl.pallas_call(
        _dft_fwd_kernel,
        grid=(n_seq, nf, nb, nk),
        in_specs=[pl.BlockSpec((tf, tk), lambda s, f, n, k: (f, k)),
                  pl.BlockSpec((tf, tk), lambda s, f, n, k: (f, k)),
                  pl.BlockSpec((tk, tn), lambda s, f, n, k: (s * nk + k, zc0 + n)),
                  pl.BlockSpec((tf, tn), lambda s, f, n, k: (f, kc0 + n)),
                  pl.BlockSpec((tf, tn), lambda s, f, n, k: (f, kc0 + n))],
        out_specs=[pl.BlockSpec((tf, tn), lambda s, f, n, k: (s * nf + f, n)),
                   pl.BlockSpec((tf, tn), lambda s, f, n, k: (s * nf + f, n))],
        out_shape=[out, out],
        scratch_shapes=[pltpu.VMEM((tf, tn), F32), pltpu.VMEM((tf, tn), F32)],
        compiler_params=_params(("parallel", "parallel", "parallel", "arbitrary"),
                                _vmem_limit(2 * _nbytes((tf, tk), BF16) + _nbytes((tk, tn), F32)
                                            + 2 * _nbytes((tf, tn), F32) + 2 * _nbytes((tf, tn), BF16),
                                            resident=2 * _nbytes((tf, tn), F32),
                                            temps=4 * _nbytes((tf, tn), F32))),
        name="dft_fwd",
    )(tabs["c"], tabs["s"], z, kr, ks)


def _dft_inv_kernel(inv_len, ct_ref, st_ref, yr_ref, ys_ref, z_ref, gate_ref, skip_ref,
                    o_ref, acc_ref):
    k = pl.program_id(3)

    @pl.when(k == 0)
    def _():
        acc_ref[...] = jnp.zeros_like(acc_ref)

    acc_ref[...] += _dot(ct_ref[...], yr_ref[...]) + _dot(st_ref[...], ys_ref[...])

    @pl.when(k == pl.num_programs(3) - 1)
    def _():
        y = acc_ref[...] * inv_len + skip_ref[...] * z_ref[...]
        o_ref[...] = (gate_ref[...] * y).astype(o_ref.dtype)


def _dft_inv(tabs, yr, ys, z, z_col0, gate, gate_col0, skip, order, n_seq, length, out_dtype):
    tt, tn, tk = _seq_tiles(length)
    nb, nt, nk = D_MODEL // tn, length // tt, length // tk
    zc0, gc0 = z_col0 // tn, gate_col0 // tn
    return pl.pallas_call(
        functools.partial(_dft_inv_kernel, 1.0 / length),
        grid=(n_seq, nt, nb, nk),
        in_specs=[pl.BlockSpec((tt, tk), lambda s, t, n, k: (t, k)),
                  pl.BlockSpec((tt, tk), lambda s, t, n, k: (t, k)),
                  pl.BlockSpec((tk, tn), lambda s, t, n, k: (s * nk + k, n)),
                  pl.BlockSpec((tk, tn), lambda s, t, n, k: (s * nk + k, n)),
                  pl.BlockSpec((tt, tn), lambda s, t, n, k: (s * nt + t, zc0 + n)),
                  pl.BlockSpec((tt, tn), lambda s, t, n, k: (s * nt + t, gc0 + n)),
                  pl.BlockSpec((None, 1, tn), lambda s, t, n, k: (order, 0, n))],
        out_specs=pl.BlockSpec((tt, tn), lambda s, t, n, k: (s * nt + t, n)),
        out_shape=jax.ShapeDtypeStruct((n_seq * length, D_MODEL), out_dtype),
        scratch_shapes=[pltpu.VMEM((tt, tn), F32)],
        compiler_params=_params(("parallel", "parallel", "parallel", "arbitrary"),
                                _vmem_limit(2 * _nbytes((tt, tk), BF16) + 2 * _nbytes((tk, tn), BF16)
                                            + 3 * _nbytes((tt, tn), F32),
                                            resident=_nbytes((tt, tn), F32),
                                            temps=3 * _nbytes((tt, tn), F32))),
        name="dft_inv",
    )(tabs["ct"], tabs["st"], yr, ys, z, gate, skip)


TW_LANES = 128
SUB = 8
CT_ROWS = 2 * SUB
CT_COLS = 1024


def _lane_tile(x, width):
    return jnp.tile(x, (1, width // x.shape[-1]))


def _kron_sub(f):
    return jnp.kron(f, jnp.eye(SUB, dtype=f.dtype))


def _sub_rows(x, h):
    part = x[:, h * SUB:(h + 1) * SUB, :]
    return part.reshape(part.shape[0] * SUB, part.shape[2])


def _from_sub_rows(parts):
    split = [p.reshape(p.shape[0] // SUB, SUB, p.shape[1]) for p in parts]
    return jnp.concatenate(split, axis=1)


def _ct_stage_a_kernel(n_in, *refs):
    x_refs, (fa_ref, cw_ref, sw_ref, br_ref, bi_ref) = refs[:n_in], refs[n_in:]
    fa = fa_ref[...]
    half = fa.shape[0] // 2
    width = br_ref.shape[-1]
    xs = [r[...] for r in x_refs]
    b_re, b_im = [], []
    for h in range(CT_ROWS // SUB):
        x = jnp.concatenate([_sub_rows(x, h) for x in xs], axis=0).astype(BF16)
        a = _dot(fa, x)
        ar, ai = a[:half], a[half:]
        cw, sw = _lane_tile(cw_ref[h], width), _lane_tile(sw_ref[h], width)
        b_re.append(ar * cw + ai * sw)
        b_im.append(ai * cw - ar * sw)
    br_ref[...] = _from_sub_rows(b_re).astype(BF16)
    bi_ref[...] = _from_sub_rows(b_im).astype(BF16)


def _ct_stage_a(xs, n_seq, fa, cw, sw):
    n1 = fa.shape[0] // (2 * SUB)
    n2 = cw.shape[0] * CT_ROWS
    width = xs[0][0].shape[-1]
    tn2, tw = CT_ROWS, CT_COLS
    in_specs, blocks = [], 0
    for arr, prefix in xs:
        rows_in = arr.shape[-3]
        lead = (None,) * (arr.ndim - 3)
        in_specs.append(pl.BlockSpec(lead + (rows_in, tn2, tw),
                                     lambda s, i, c, prefix=prefix: prefix(s) + (0, i, c)))
        blocks += _nbytes((rows_in, tn2, tw), arr.dtype)
    twid = pl.BlockSpec((None,) + cw.shape[1:], lambda s, i, c: (i, 0, 0, 0))
    in_specs += [pl.BlockSpec(fa.shape, lambda s, i, c: (0, 0)), twid, twid]
    out = jax.ShapeDtypeStruct((n_seq, n1, n2, width), BF16)
    out_spec = pl.BlockSpec((None, n1, tn2, tw), lambda s, i, c: (s, 0, i, c))
    return pl.pallas_call(
        functools.partial(_ct_stage_a_kernel, len(xs)),
        grid=(n_seq, n2 // tn2, width // tw),
        in_specs=in_specs,
        out_specs=[out_spec, out_spec],
        out_shape=[out, out],
        compiler_params=_params(("parallel", "parallel", "parallel"),
                                _vmem_limit(blocks + 2 * _nbytes((n1, tn2, tw), BF16)
                                            + _nbytes(fa.shape, BF16),
                                            temps=8 * _nbytes((n1, tn2, tw), F32))),
        name="ct_stage_a",
    )(*[arr for arr, _ in xs], fa, cw, sw)


def _ct_mid_kernel(br_ref, bi_ref, pr_ref, pi_ref, mr_ref, mi_ref, fb_ref, fbi_ref, cw_ref, sw_ref,
                   vr_ref, vi_ref):
    fb, fbi = fb_ref[...], fbi_ref[...]
    half = fb.shape[0] // 2
    width = br_ref.shape[-1]
    for j in range(br_ref.shape[0]):
        stack = lambda re_ref, im_ref: jnp.concatenate([re_ref[j], im_ref[j]], axis=0)
        kr = _dot(fb[:half], stack(pr_ref, pi_ref))
        ki = _dot(fb[half:], stack(mr_ref, mi_ref))
        x = _dot(fb, stack(br_ref, bi_ref))
        xr, xi = x[:half], x[half:]
        y = jnp.concatenate([xr * kr - xi * ki, xr * ki + xi * kr], axis=0).astype(BF16)
        v = _dot(fbi, y)
        vr, vi = v[:half], v[half:]
        cw, sw = _lane_tile(cw_ref[j], width), _lane_tile(sw_ref[j], width)
        vr_ref[j] = (vr * cw - vi * sw).astype(BF16)
        vi_ref[j] = (vi * cw + vr * sw).astype(BF16)


def _ct_mid(br, bi, filt, order, fb, fbi, cw, sw, tk1=4, td=1024):
    n1, n2, d = br.shape
    nd = d // td
    data = pl.BlockSpec((tk1, n2, td), lambda i, j: (i, 0, j))
    coef = pl.BlockSpec((tk1, n2, td), lambda i, j: (i, 0, order * nd + j))
    mat = pl.BlockSpec(fb.shape, lambda i, j: (0, 0))
    tw = pl.BlockSpec((tk1, n2, TW_LANES), lambda i, j: (i, 0, 0))
    out = jax.ShapeDtypeStruct((n1, n2, d), BF16)
    return pl.pallas_call(
        _ct_mid_kernel,
        grid=(n1 // tk1, nd),
        in_specs=[data, data, coef, coef, coef, coef, mat, mat, tw, tw],
        out_specs=[data, data],
        out_shape=[out, out],
        compiler_params=_params(("parallel", "parallel"),
                                _vmem_limit(8 * _nbytes((tk1, n2, td), BF16),
                                            temps=10 * _nbytes((2 * n2, td), F32))),
        name="ct_mid",
    )(br, bi, *filt, fb, fbi, cw, sw)


def _ct_inv_a_kernel(scale, vr_ref, vi_ref, fai_ref, z0_ref, z1_ref, g0_ref, g1_ref, skip_ref,
                     o_ref):
    fai = fai_ref[...]
    half = fai.shape[0] // 2
    skip = skip_ref[...]
    vr, vi = vr_ref[...].astype(F32), vi_ref[...].astype(F32)
    zs, gs = (z0_ref[...], z1_ref[...]), (g0_ref[...], g1_ref[...])
    outs = ([], [])
    for h in range(CT_ROWS // SUB):
        v = jnp.concatenate([_sub_rows(vr, h), _sub_rows(vi, h)], axis=0).astype(BF16)
        y = _dot(fai, v) * scale
        for b, yb in enumerate((y[:half], y[half:])):
            outs[b].append(_sub_rows(gs[b], h) * (yb + skip * _sub_rows(zs[b], h)))
    for b in range(2):
        o_ref[b] = _from_sub_rows(outs[b]).astype(o_ref.dtype)


def _ct_inv_a(vr, vi, fai, z, z_which, gate, gate_which, skip, order, out_dtype):
    n1, n2, d = vr.shape
    rows = fai.shape[0] // (2 * SUB)
    tn2, tw = CT_ROWS, CT_COLS // 2
    spec = pl.BlockSpec((n1, tn2, tw), lambda i, c: (0, i, c))
    pair = lambda which, b: pl.BlockSpec((None, None, rows, tn2, tw), lambda i, c: (which, b, 0, i, c))
    return pl.pallas_call(
        functools.partial(_ct_inv_a_kernel, 1.0 / (n1 * n2)),
        grid=(n2 // tn2, d // tw),
        in_specs=[spec, spec, pl.BlockSpec(fai.shape, lambda i, c: (0, 0)),
                  pair(z_which, 0), pair(z_which, 1), pair(gate_which, 0), pair(gate_which, 1),
                  pl.BlockSpec((None, 1, tw), lambda i, c: (order, 0, c))],
        out_specs=pl.BlockSpec((2, rows, tn2, tw), lambda i, c: (0, 0, i, c)),
        out_shape=jax.ShapeDtypeStruct((2, rows, n2, d), out_dtype),
        compiler_params=_params(("parallel", "parallel"),
                                _vmem_limit(2 * _nbytes((n1, tn2, tw), BF16)
                                            + 6 * _nbytes((rows, tn2, tw), F32)
                                            + _nbytes(fai.shape, BF16),
                                            temps=8 * _nbytes((n1, tn2, tw), F32))),
        name="ct_inv_a",
    )(vr, vi, fai, z, z, gate, gate, skip)


def _ct_real_b_kernel(scale, br_ref, bi_ref, fb_ref, o_ref, so_ref):
    fb = fb_ref[...]
    for j in range(br_ref.shape[0]):
        so_ref[:, j, :] = _dot(fb, jnp.concatenate([br_ref[j], bi_ref[j]], axis=0)) * scale
    o_ref[...] = so_ref[...].astype(o_ref.dtype)


def _ct_real_b(br, bi, fb_re, scale):
    n_seq, n1, n2, d = br.shape
    tk1, tw = CT_ROWS, CT_COLS
    blk = pl.BlockSpec((None, tk1, n2, tw), lambda s, i, c: (s, i, 0, c))
    return pl.pallas_call(
        functools.partial(_ct_real_b_kernel, scale),
        grid=(n_seq, n1 // tk1, d // tw),
        in_specs=[blk, blk, pl.BlockSpec(fb_re.shape, lambda s, i, c: (0, 0))],
        out_specs=pl.BlockSpec((None, n2, tk1, tw), lambda s, i, c: (s, 0, i, c)),
        out_shape=jax.ShapeDtypeStruct((n_seq, n2, n1, d), BF16),
        scratch_shapes=[pltpu.VMEM((n2, tk1, tw), F32)],
        compiler_params=_params(("parallel", "parallel", "parallel"),
                                _vmem_limit(3 * _nbytes((tk1, n2, tw), BF16),
                                            resident=_nbytes((n2, tk1, tw), F32),
                                            temps=2 * _nbytes((n2, tk1, tw), F32))),
        name="ct_real_b",
    )(br, bi, fb_re)


def _cos_sin(num, den):
    ang = (num % den).astype(F32) * (2.0 * math.pi / den)
    return jnp.cos(ang), jnp.sin(ang)


def _ct_tables(n1, n2):
    i1 = jnp.arange(n1, dtype=jnp.int32)
    i2 = jnp.arange(n2, dtype=jnp.int32)
    c1, s1 = _cos_sin(i1[:, None] * i1[None, :], n1)
    c2, s2 = _cos_sin(i2[:, None] * i2[None, :], n2)
    cw, sw = _cos_sin(i2[:, None] * i1[None, :], n1 * n2)
    lanes = lambda t: jnp.broadcast_to(t[..., None], t.shape + (TW_LANES,))

    def stage_a_rows(t):
        t = t.reshape(n2 // CT_ROWS, CT_ROWS // SUB, SUB, n1)
        return lanes(jnp.swapaxes(t, 2, 3).reshape(n2 // CT_ROWS, CT_ROWS // SUB, n1 * SUB))

    return {"c1": c1, "s1": s1, "c2": c2, "s2": s2,
            "cw_a": stage_a_rows(cw), "sw_a": stage_a_rows(sw),
            "cw_b": lanes(cw.T), "sw_b": lanes(sw.T)}


def _shortconv_kernel(x_ref, w_ref, b_ref, o_ref):
    x = x_ref[...]
    n = x.shape[0]
    row = lax.broadcasted_iota(jnp.int32, x.shape, 0)
    before = jnp.where(row == 0, 0.0, pltpu.roll(x, 1, 0))
    after = jnp.where(row == n - 1, 0.0, pltpu.roll(x, n - 1, 0))
    o_ref[...] = w_ref[0:1, :] * before + w_ref[1:2, :] * x + w_ref[2:3, :] * after + b_ref[...]


def _shortconv(x, w, b, row0, n_seq, length, split):
    width = x.shape[1]
    tc = 256 if length >= 1024 else 2048
    s0 = row0 // length
    rows = n_seq * length
    if split:
        per = D_MODEL // tc
        out_spec = pl.BlockSpec((None, length, tc), lambda s, j: (j // per, s, j % per))
        out_shape = jax.ShapeDtypeStruct((width // D_MODEL, rows, D_MODEL), F32)
    else:
        out_spec = pl.BlockSpec((length, tc), lambda s, j: (s, j))
        out_shape = jax.ShapeDtypeStruct((rows, width), F32)
    return pl.pallas_call(
        _shortconv_kernel,
        grid=(n_seq, width // tc),
        in_specs=[pl.BlockSpec((length, tc), lambda s, j: (s0 + s, j)),
                  pl.BlockSpec((HY_CONV, tc), lambda s, j: (0, j)),
                  pl.BlockSpec((1, tc), lambda s, j: (0, j))],
        out_specs=out_spec,
        out_shape=out_shape,
        compiler_params=_params(("parallel", "parallel"),
                                _vmem_limit(2 * _nbytes((length, tc), F32),
                                            temps=4 * _nbytes((length, tc), F32))),
        name="shortconv",
    )(x, w, b.reshape(1, width))


def _filter_mlp_kernel(feat_ref, t_ref, w1_ref, b1_ref, fr1_ref, w2_ref, b2_ref, fr2_ref,
                       w3_ref, decay_ref, h_ref, ss_ref):
    x = jnp.sin(fr1_ref[...] * (_dot(feat_ref[...].astype(BF16), w1_ref[...].astype(BF16))
                                + b1_ref[...]))
    x = jnp.sin(fr2_ref[...] * (_dot(x.astype(BF16), w2_ref[...].astype(BF16)) + b2_ref[...]))
    h = _dot(x.astype(BF16), w3_ref[...].astype(BF16))
    h = h * (jnp.exp(-t_ref[...] * jnp.exp(decay_ref[...])) + HY_SHIFT)
    h_ref[...] = h

    @pl.when(pl.program_id(0) == 0)
    def _():
        ss_ref[...] = jnp.zeros_like(ss_ref)

    ss_ref[...] += jnp.sum(h * h, axis=0, keepdims=True)


def _filter_combine_kernel(hf_ref, hb_ref, ssf_ref, ssb_ref, a_ref, b_ref):
    norm = lax.rsqrt(ssf_ref[...] + ssb_ref[...] + 1e-12)
    fwd = hf_ref[...] * norm
    bwd = hb_ref[...] * norm
    row = lax.broadcasted_iota(jnp.int32, bwd.shape, 0) + pl.program_id(0) * bwd.shape[0]
    bwd = jnp.where(row == 0, 0.0, bwd)
    a_ref[...] = fwd + bwd
    b_ref[...] = fwd - bwd


def _hyena_filters(length, f_w1, f_b1, f_freq1, f_w2, f_b2, f_freq2, f_w3, log_decay):
    t = jnp.linspace(0.0, 1.0, length, dtype=F32)[:, None]
    t_idx = jnp.arange(length, dtype=F32)[:, None]
    bands = jnp.linspace(1e-4, HY_BANDS - 1, HY_BANDS, dtype=F32)
    w = 2.0 * math.pi * t_idx * bands / length
    feat = jnp.concatenate([t, jnp.cos(w), -jnp.sin(w)], axis=-1)
    emb_pad = 128
    feat = jnp.pad(feat, ((0, 0), (0, emb_pad - HY_EMB)))
    w1 = jnp.pad(f_w1, ((0, emb_pad - HY_EMB), (0, 0)))
    n_all = HY_DIRS * HY_ORDER * D_MODEL
    n_dir = HY_ORDER * D_MODEL
    tm = 256
    full = lambda shape: pl.BlockSpec(shape, lambda i: (0, 0))
    h, ss = pl.pallas_call(
        _filter_mlp_kernel,
        grid=(length // tm,),
        in_specs=[pl.BlockSpec((tm, emb_pad), lambda i: (i, 0)),
                  pl.BlockSpec((tm, 1), lambda i: (i, 0)),
                  full((emb_pad, HY_FW)), full((1, HY_FW)), full((1, HY_FW)),
                  full((HY_FW, HY_FW)), full((1, HY_FW)), full((1, HY_FW)),
                  full((HY_FW, n_all)), full((1, n_all))],
        out_specs=[pl.BlockSpec((tm, n_all), lambda i: (i, 0)), full((1, n_all))],
        out_shape=[jax.ShapeDtypeStruct((length, n_all), F32),
                   jax.ShapeDtypeStruct((1, n_all), F32)],
        compiler_params=_params(("arbitrary",),
                                _vmem_limit(_nbytes((tm, n_all), F32) + _nbytes((HY_FW, n_all), F32),
                                            temps=3 * _nbytes((tm, n_all), F32))),
        name="filter_mlp",
    )(feat, t, w1, f_b1.reshape(1, HY_FW), f_freq1.reshape(1, HY_FW), f_w2,
      f_b2.reshape(1, HY_FW), f_freq2.reshape(1, HY_FW), f_w3,
      log_decay.reshape(1, n_all))

    tn = 1024
    nb = n_dir // tn
    comb = jax.ShapeDtypeStruct((length, n_dir), F32)
    a, b = pl.pallas_call(
        _filter_combine_kernel,
        grid=(length // tm, nb),
        in_specs=[pl.BlockSpec((tm, tn), lambda i, j: (i, j)),
                  pl.BlockSpec((tm, tn), lambda i, j: (i, nb + j)),
                  pl.BlockSpec((1, tn), lambda i, j: (0, j)),
                  pl.BlockSpec((1, tn), lambda i, j: (0, nb + j))],
        out_specs=[pl.BlockSpec((tm, tn), lambda i, j: (i, j)),
                   pl.BlockSpec((tm, tn), lambda i, j: (i, j))],
        out_shape=[comb, comb],
        compiler_params=_params(("parallel", "parallel"), VMEM_FLOOR_BYTES),
        name="filter_combine",
    )(h, h, ss, ss)
    return a, b


def _cis_product(row_hi, row_lo, period):
    def cis(phase):
        ang = (phase % period).astype(F32) * (2.0 * math.pi / period)
        return jnp.cos(ang)[:, :, None], jnp.sin(ang)[:, :, None]
    (c1, s1), (c0, s0) = cis(row_hi), cis(row_lo)
    c0, s0 = jnp.swapaxes(c0, 1, 2), jnp.swapaxes(s0, 1, 2)
    rows = row_hi.shape[0]
    return ((c1 * c0 - s1 * s0).reshape(rows, -1), (s1 * c0 + c1 * s0).reshape(rows, -1))


def _odd_dft_tables(length):
    split = 1 << (length.bit_length() // 2)
    r = jnp.arange(length, dtype=jnp.int32)[:, None]
    hi = jnp.arange(length // split, dtype=jnp.int32)[None, :] * split
    lo = jnp.arange(split, dtype=jnp.int32)[None, :]
    c, s = _cis_product((2 * r + 1) * hi, (2 * r + 1) * lo, 4 * length)
    ct, st = _cis_product(r * (2 * hi), r * (2 * lo + 1), 4 * length)
    return {"c": c.astype(BF16), "s": s.astype(BF16), "ct": ct.astype(BF16), "st": st.astype(BF16)}


def _dft_tables(length):
    split = 1 << (length.bit_length() // 2)
    r = jnp.arange(length, dtype=jnp.int32)[:, None]
    hi = jnp.arange(length // split, dtype=jnp.int32)[None, :] * split
    lo = jnp.arange(split, dtype=jnp.int32)[None, :]
    c, s = _cis_product(r * hi, r * lo, length)
    return c.astype(BF16), (-s).astype(BF16)


def _hyena_mix_dense(proj, conv_w, conv_b, filt_p, filt_m, skip, row0, n_seq, length):
    tabs = _odd_dft_tables(length)
    kr, ks = _mm(tabs["c"], filt_p), _mm(tabs["s"], filt_m)
    pc = _shortconv(proj, conv_w, conv_b, row0, n_seq, length, split=False)
    v0, x1, x2 = 0, D_MODEL, 2 * D_MODEL
    yr, ys = _dft_fwd(tabs, pc, v0, kr, ks, 0, n_seq, length)
    z1 = _dft_inv(tabs, yr, ys, pc, v0, pc, x1, skip, 0, n_seq, length, F32)
    yr, ys = _dft_fwd(tabs, z1, 0, kr, ks, 1, n_seq, length)
    return _dft_inv(tabs, yr, ys, z1, 0, pc, x2, skip, 1, n_seq, length, BF16)


HY_N1, HY_N2 = 64, 128


def _hyena_mix_pair(proj, conv_w, conv_b, filt_p, filt_m, skip, row0):
    assert DEC_BATCH == 2 and HY_N1 * HY_N2 == 2 * DEC_SEQ
    n1, n2, rows_in = HY_N1, HY_N2, HY_N1 // 2
    t = _ct_tables(n1, n2)
    c_in, s_in = t["c1"][:, :rows_in], t["s1"][:, :rows_in]
    fa = _kron_sub(jnp.block([[c_in, s_in], [-s_in, c_in]])).astype(BF16)
    fa_real = _kron_sub(jnp.concatenate([c_in, -s_in], axis=0)).astype(BF16)
    fb = jnp.block([[t["c2"], t["s2"]], [-t["s2"], t["c2"]]]).astype(BF16)
    fbi = jnp.block([[t["c2"], -t["s2"]], [t["s2"], t["c2"]]]).astype(BF16)
    c_out, s_out = t["c1"][:rows_in], t["s1"][:rows_in]
    fai = _kron_sub(jnp.block([[c_out, -s_out], [s_out, c_out]])).astype(BF16)
    n_filt = HY_ORDER * D_MODEL
    filt = []
    for f in (filt_p, filt_m):
        re, im = _ct_stage_a([(f.reshape(rows_in, n2, n_filt), lambda s: ())], 1, fa_real,
                             t["cw_a"], t["sw_a"])
        filt += [re[0], im[0]]

    pc = _shortconv(proj, conv_w, conv_b, row0, DEC_BATCH, DEC_SEQ, split=True)
    pc = pc.reshape(3, DEC_BATCH, rows_in, n2, D_MODEL)
    z, out_dtypes = pc, (F32, BF16)
    for order in range(HY_ORDER):
        br, bi = _ct_stage_a([(z, lambda s: (0, 0)), (z, lambda s: (0, 1))], 1, fa,
                             t["cw_a"], t["sw_a"])
        vr, vi = _ct_mid(br[0], bi[0], filt, order, fb, fbi, t["cw_b"], t["sw_b"])
        z = _ct_inv_a(vr, vi, fai, z, 0, pc, 1 + order, skip, order, out_dtypes[order])[None]
    return z.reshape(ROWS_LAT, D_MODEL)


def _fnet_chan_kernel(h_ref, sh_ref, sc_ref, w_ref, p_ref, q_ref):
    u = _modulate(h_ref[...], sh_ref[...], sc_ref[...]).astype(BF16)
    w = w_ref[...]
    for g in range(FNET_GROUPS):
        cols = slice(g * FNET_CG, (g + 1) * FNET_CG)
        r = _dot(u[:, cols], w)
        p_ref[:, cols] = r[:, :FNET_CG]
        q_ref[:, cols] = r[:, FNET_CG:]


def _fnet_chan(h, mods, w_cs, tm=512):
    row = pl.BlockSpec((tm, D_MODEL), lambda i: (i, 0))
    out = jax.ShapeDtypeStruct((ROWS, D_MODEL), F32)
    return pl.pallas_call(
        _fnet_chan_kernel,
        grid=(ROWS // tm,),
        in_specs=[row, _mod_spec(0, tm), _mod_spec(1, tm),
                  pl.BlockSpec((FNET_CG, 2 * FNET_CG), lambda i: (0, 0))],
        out_specs=[row, row],
        out_shape=[out, out],
        compiler_params=_params(("parallel",),
                                _vmem_limit(3 * _nbytes((tm, D_MODEL), F32),
                                            temps=2 * _nbytes((tm, D_MODEL), F32))),
        name="fnet_chan",
    )(h, mods, mods, w_cs)


def _fnet_pos_kernel(scale, c_ref, ns_ref, p_ref, q_ref, o_ref, acc_ref):
    k = pl.program_id(3)

    @pl.when(k == 0)
    def _():
        acc_ref[...] = jnp.zeros_like(acc_ref)

    acc_ref[...] += (_dot(c_ref[...], p_ref[...].astype(BF16))
                     + _dot(ns_ref[...], q_ref[...].astype(BF16)))

    @pl.when(k == pl.num_programs(3) - 1)
    def _():
        o_ref[...] = (acc_ref[...] * scale).astype(o_ref.dtype)


def _fnet_pos(c_tab, ns_tab, p, q, n_seq, length):
    tt, tn, tk = _seq_tiles(length)
    nb, nt, nk = D_MODEL // tn, length // tt, length // tk
    scale = (length * FNET_CG) ** -0.5
    return pl.pallas_call(
        functools.partial(_fnet_pos_kernel, scale),
        grid=(n_seq, nt, nb, nk),
        in_specs=[pl.BlockSpec((tt, tk), lambda s, t, n, k: (t, k)),
                  pl.BlockSpec((tt, tk), lambda s, t, n, k: (t, k)),
                  pl.BlockSpec((tk, tn), lambda s, t, n, k: (s * nk + k, n)),
                  pl.BlockSpec((tk, tn), lambda s, t, n, k: (s * nk + k, n))],
        out_specs=pl.BlockSpec((tt, tn), lambda s, t, n, k: (s * nt + t, n)),
        out_shape=jax.ShapeDtypeStruct((n_seq * length, D_MODEL), BF16),
        scratch_shapes=[pltpu.VMEM((tt, tn), F32)],
        compiler_params=_params(("parallel", "parallel", "parallel", "arbitrary"),
                                _vmem_limit(2 * _nbytes((tt, tk), BF16) + 2 * _nbytes((tk, tn), BF16)
                                            + _nbytes((tt, tn), BF16),
                                            resident=_nbytes((tt, tn), F32),
                                            temps=2 * _nbytes((tt, tn), F32))),
        name="fnet_pos",
    )(c_tab, ns_tab, p, q)


FN_N1, FN_N2 = 32, 128


def _fnet_pos_factored(p, q):
    assert FN_N1 * FN_N2 == DEC_SEQ and ROWS % DEC_SEQ == 0
    n1, n2 = FN_N1, FN_N2
    t = _ct_tables(n1, n2)
    fa = _kron_sub(jnp.block([[t["c1"], -t["s1"]], [-t["s1"], -t["c1"]]])).astype(BF16)
    fb_re = jnp.concatenate([t["c2"], t["s2"]], axis=1).astype(BF16)
    lat0 = ROWS_CTX // DEC_SEQ
    view = lambda x: x.reshape(ROWS // DEC_SEQ, n1, n2, D_MODEL)
    seq = lambda s: (lat0 + s,)
    br, bi = _ct_stage_a([(view(p), seq), (view(q), seq)], DEC_BATCH, fa, t["cw_a"], t["sw_a"])
    f = _ct_real_b(br, bi, fb_re, (DEC_SEQ * FNET_CG) ** -0.5)
    return f.reshape(ROWS_LAT, D_MODEL)


def _rope_table():
    rows = DEC_SEQ // GRID_W
    row = jnp.repeat(jnp.arange(rows), GRID_W).astype(F32)
    col = jnp.tile(jnp.arange(GRID_W), rows).astype(F32)
    inv = ROPE_THETA ** (-jnp.arange(0, AXIS_ROPE, 2, dtype=F32) / AXIS_ROPE)
    ang = jnp.concatenate([row[:, None] * inv, col[:, None] * inv], axis=-1)
    cos = jnp.repeat(jnp.cos(ang), 2, axis=-1)
    sin = jnp.repeat(jnp.sin(ang), 2, axis=-1)
    lat = jnp.tile(jnp.concatenate([cos, sin], axis=-1), (DEC_BATCH, 1))
    ctx = jnp.concatenate([jnp.ones((ROWS_CTX, QK_ROPE), F32), jnp.zeros((ROWS_CTX, QK_ROPE), F32)],
                          axis=-1)
    return jnp.concatenate([ctx, lat], axis=0)


def _pair_rotated(w):
    pairs = w.reshape(w.shape[:-1] + (QK_ROPE // 2, 2))
    return jnp.stack([-pairs[..., 1], pairs[..., 0]], axis=-1).reshape(w.shape)


def kernel(x_prompt, x_sample, c, cache_ckv, cache_krope, c_ctx, ada_w, ada_b, ln_g, ln_b, ffn_w_gate, ffn_w_up, ffn_w_down, mla_w_dq, mla_q_norm, mla_w_uq, mla_w_dkv, mla_kv_norm, mla_w_kr, mla_w_ukv, mla_w_o, hy_w_in, hy_b_in, hy_conv_w, hy_conv_b, hy_f_w1, hy_f_b1, hy_f_freq1, hy_f_w2, hy_f_b2, hy_f_freq2, hy_f_w3, hy_log_decay, hy_skip, hy_w_out, hy_b_out, fn_w_out, fn_b_out):
    assert x_prompt.shape == (BATCH, SEQ, D_MODEL) and x_sample.shape == (DEC_BATCH, DEC_SEQ, D_MODEL)
    assert ROWS_CTX % DEC_SEQ == 0 and SEQ == FNET_CG

    h = jnp.concatenate([x_prompt.reshape(ROWS_CTX, D_MODEL), x_sample.reshape(ROWS_LAT, D_MODEL)])
    cond = jnp.concatenate([c_ctx[None, :], c, jnp.zeros((COND_PAD - N_COND, D_MODEL), F32)])
    mods_all = _modulation_vectors(cond, ada_w, ada_b)
    zero_bias = jnp.zeros((D_MODEL,), F32)
    ffn_w = (ffn_w_gate.astype(BF16), ffn_w_up.astype(BF16), ffn_w_down.astype(BF16))
    rope_tab = None
    ckv_states, krope_states = [], []

    for i in range(DEPTH):
        kind, j = i % N_MIXERS, i // N_MIXERS
        mods = mods_all[i]
        if kind == 0:
            if rope_tab is None:
                rope_tab = _rope_table()
            w_kr2 = jnp.concatenate([mla_w_kr[j], _pair_rotated(mla_w_kr[j])], axis=-1).astype(BF16)
            wq = mla_w_uq[j].reshape(Q_RANK, MLA_HEADS, QK_NOPE + QK_ROPE)
            w_q = jnp.concatenate([wq, _pair_rotated(wq[..., QK_NOPE:])], axis=-1)
            w_q = w_q.reshape(Q_RANK, MLA_HEADS * HEAD_W).astype(BF16)
            w_ukv = mla_w_ukv[j].reshape(KV_RANK, MLA_HEADS, QK_NOPE + V_DIM)
            w_k = w_ukv[..., :QK_NOPE].reshape(KV_RANK, MLA_HEADS * QK_NOPE).astype(BF16)
            w_vt = w_ukv[..., QK_NOPE:].reshape(KV_RANK, MLA_HEADS * V_DIM).T.astype(BF16)
            cq, ckv, kr, kr2 = _mla_down(h, mods, mla_w_dq[j].astype(BF16), mla_w_dkv[j].astype(BF16),
                                         w_kr2, mla_q_norm[j], mla_kv_norm[j], rope_tab)
            ckv_states.append(ckv[:ROWS_CTX].reshape(BATCH, SEQ, KV_RANK))
            krope_states.append(kr[:ROWS_CTX].reshape(BATCH, SEQ, QK_ROPE))
            q = _q_up(cq, w_q, rope_tab)
            k_tok, vt_tok = _kv_expand(ckv, kr2, w_k, w_vt)
            kc = cache_krope[:, j].reshape(DEC_BATCH * PAST_LEN, QK_ROPE).astype(BF16)
            k_cache, vt_cache = _kv_expand(cache_ckv[:, j].reshape(DEC_BATCH * PAST_LEN, KV_RANK),
                                           jnp.concatenate([kc, kc], axis=-1), w_k, w_vt)
            o = _attention(q, k_tok, vt_tok, k_cache, vt_cache)
            h = _mm_postnorm(o, mla_w_o[j].astype(BF16), zero_bias, h, mods, 2, ln_g[i, 0], ln_b[i, 0])
        elif kind == 1:
            proj = _mod_mm(h, mods, hy_w_in[j].astype(BF16), hy_b_in[j], 0)
            fp = (hy_f_w1[j], hy_f_b1[j], hy_f_freq1[j], hy_f_w2[j], hy_f_b2[j], hy_f_freq2[j],
                  hy_f_w3[j], hy_log_decay[j])
            skip = hy_skip[j].reshape(HY_ORDER, 1, D_MODEL)
            conv = (hy_conv_w[j], hy_conv_b[j])
            z_ctx = _hyena_mix_dense(proj, *conv, *_hyena_filters(SEQ, *fp), skip, 0, BATCH, SEQ)
            z_lat = _hyena_mix_pair(proj, *conv, *_hyena_filters(DEC_SEQ, *fp), skip, ROWS_CTX)
            h = _mm_postnorm((z_ctx, z_lat), hy_w_out[j].astype(BF16), hy_b_out[j],
                             h, mods, 2, ln_g[i, 0], ln_b[i, 0])
        else:
            c_ch, ns_ch = _dft_tables(FNET_CG)
            p, q = _fnet_chan(h, mods, jnp.concatenate([c_ch, -ns_ch], axis=-1))
            f_ctx = _fnet_pos(c_ch, ns_ch, p, q, BATCH, SEQ)
            f_lat = _fnet_pos_factored(p, q)
            h = _mm_postnorm((f_ctx, f_lat), fn_w_out[j].astype(BF16), fn_b_out[j],
                             h, mods, 2, ln_g[i, 0], ln_b[i, 0])
        h = _ffn(h, mods, *ffn_w, i, ln_g[i, 1], ln_b[i, 1], split_out=i == DEPTH - 1)

    y_prompt = h[0].reshape(BATCH, SEQ, D_MODEL)
    y_sample = h[1].reshape(DEC_BATCH, DEC_SEQ, D_MODEL)
    return (y_prompt, y_sample, jnp.stack(ckv_states, axis=1), jnp.stack(krope_states, axis=1))
```

```python
import functools
import math

import jax
import jax.numpy as jnp
from jax import lax
from jax.experimental import pallas as pl
from jax.experimental.pallas import tpu as pltpu

F32 = jnp.float32
BF16 = jnp.bfloat16

D_MODEL = 2048
BATCH = 16
SEQ = 256
DEPTH = 4
DEC_BATCH = 2
DEC_SEQ = 4096
PAST_LEN = 512
GRID_W = 64
N_MIXERS = 3
MLA_HEADS = 16
QK_NOPE = 128
QK_ROPE = 64
V_DIM = 128
Q_RANK = 512
KV_RANK = 512
ROPE_THETA = 10000.0
AXIS_ROPE = QK_ROPE // 2
HY_ORDER = 2
HY_DIRS = 2
HY_CONV = 3
HY_BANDS = 16
HY_EMB = 1 + 2 * HY_BANDS
HY_FW = 64
HY_SHIFT = 0.05
FNET_GROUPS = 8
FNET_CG = D_MODEL // FNET_GROUPS
D_FF = -(-8 * D_MODEL // (3 * 256)) * 256
DN_ALPHA = (2 * DEPTH) ** 0.25
LN_EPS = 1e-5
RMS_EPS = 1e-6
N_MOD = 6

ROWS_CTX = BATCH * SEQ
ROWS_LAT = DEC_BATCH * DEC_SEQ
ROWS = ROWS_CTX + ROWS_LAT
N_COND = 1 + DEC_BATCH
COND_PAD = 8
HEAD_W = QK_NOPE + 2 * QK_ROPE
ATT_SCALE = (QK_NOPE + QK_ROPE) ** -0.5

V7X_VMEM_BYTES = 64 * 2 ** 20
VMEM_CAP_BYTES = V7X_VMEM_BYTES * 7 // 8
VMEM_FLOOR_BYTES = 32 * 2 ** 20


def _vmem_limit(pipelined, resident=0, temps=0):
    est = 2 * pipelined + resident + temps
    return int(min(max(est, VMEM_FLOOR_BYTES), VMEM_CAP_BYTES))


def _params(semantics, vmem):
    return pltpu.CompilerParams(dimension_semantics=semantics, vmem_limit_bytes=vmem)


def _nbytes(shape, dtype):
    return math.prod(shape) * jnp.dtype(dtype).itemsize


def _group_of_tile(i, tm):
    n_ctx = ROWS_CTX // tm
    return jnp.where(i < n_ctx, 0, 1 + (i - n_ctx) // (DEC_SEQ // tm))


def _mod_spec(which, tm):
    return pl.BlockSpec((None, 1, D_MODEL),
                        lambda i, *_: (which * COND_PAD + _group_of_tile(i, tm), 0, 0))


def _row_spec(width=D_MODEL):
    return pl.BlockSpec((1, width), lambda *_: (0, 0))


def _modulate(h, shift, scale):
    return h * (1.0 + scale) + shift


def _post_norm(h, delta, g, b):
    z = DN_ALPHA * h + delta
    mu = jnp.mean(z, axis=-1, keepdims=True)
    zc = z - mu
    var = jnp.mean(zc * zc, axis=-1, keepdims=True)
    return zc * lax.rsqrt(var + LN_EPS) * g + b


def _rms_norm(x, g):
    ms = jnp.mean(x * x, axis=-1, keepdims=True)
    return x * lax.rsqrt(ms + RMS_EPS) * g


def _dot(a, b):
    return jnp.dot(a, b, preferred_element_type=F32)


def _modvec_kernel(c_ref, w_ref, b_ref, o_ref):
    a = jax.nn.silu(c_ref[...]).astype(BF16)
    o_ref[...] = _dot(a, w_ref[...].astype(BF16)) + b_ref[...]


def _modulation_vectors(cond, ada_w, ada_b):
    tn = 1024
    n = N_MOD * D_MODEL
    out = pl.pallas_call(
        _modvec_kernel,
        grid=(DEPTH, n // tn),
        in_specs=[pl.BlockSpec((COND_PAD, D_MODEL), lambda l, j: (0, 0)),
                  pl.BlockSpec((None, D_MODEL, tn), lambda l, j: (l, 0, j)),
                  pl.BlockSpec((None, 1, tn), lambda l, j: (l, 0, j))],
        out_specs=pl.BlockSpec((None, COND_PAD, tn), lambda l, j: (l, 0, j)),
        out_shape=jax.ShapeDtypeStruct((DEPTH, COND_PAD, n), F32),
        compiler_params=_params(("parallel", "parallel"),
                                _vmem_limit(_nbytes((D_MODEL, tn), F32),
                                            temps=_nbytes((D_MODEL, tn), BF16))),
        name="modvec",
    )(cond, ada_w, ada_b.reshape(DEPTH, 1, n))
    out = out.reshape(DEPTH, COND_PAD, N_MOD, D_MODEL).transpose(0, 2, 1, 3)
    return out.reshape(DEPTH, N_MOD * COND_PAD, 1, D_MODEL)


def _hyena_in_kernel(n_ctx, h_ref, hp_ref, hn_ref, sh_ref, sc_ref, w_ref, b_ref, cw_ref, cb_ref,
                     o_ref, u_ref):
    i = pl.program_id(0)
    tm = h_ref.shape[0]

    @pl.when(pl.program_id(1) == 0)
    def _():
        mod = lambda x_ref: _modulate(x_ref[...], sh_ref[...], sc_ref[...]).astype(BF16)
        u_ref[:SUB, :] = mod(hp_ref)
        u_ref[SUB:SUB + tm, :] = mod(h_ref)
        u_ref[SUB + tm:, :] = mod(hn_ref)

    y = _dot(u_ref[...], w_ref[...]) + b_ref[...]
    inner = slice(SUB, SUB + tm)
    before = pltpu.roll(y, 1, 0)[inner]
    after = pltpu.roll(y, y.shape[0] - 1, 0)[inner]
    length = jnp.where(i < n_ctx, SEQ, DEC_SEQ)
    pos = (lax.broadcasted_iota(jnp.int32, before.shape, 0) + i * tm) & (length - 1)
    before = jnp.where(pos == 0, 0.0, before)
    after = jnp.where(pos == length - 1, 0.0, after)
    o_ref[...] = (cw_ref[0:1, :] * before + cw_ref[1:2, :] * y[inner] + cw_ref[2:3, :] * after
                  + cb_ref[...])


def _hyena_in(h, mods, w, b, conv_w, conv_b, tm=1024, tn=1024):
    assert SEQ & (SEQ - 1) == 0 and DEC_SEQ & (DEC_SEQ - 1) == 0 and tm % SEQ == 0 and DEC_SEQ % tm == 0
    k, n = w.shape
    per, halo, last = D_MODEL // tn, tm // SUB, ROWS // SUB - 1
    return pl.pallas_call(
        functools.partial(_hyena_in_kernel, ROWS_CTX // tm),
        grid=(ROWS // tm, n // tn),
        in_specs=[pl.BlockSpec((tm, k), lambda i, j: (i, 0)),
                  pl.BlockSpec((SUB, k), lambda i, j: (jnp.maximum(i * halo - 1, 0), 0)),
                  pl.BlockSpec((SUB, k), lambda i, j: (jnp.minimum((i + 1) * halo, last), 0)),
                  _mod_spec(0, tm), _mod_spec(1, tm),
                  pl.BlockSpec((k, tn), lambda i, j: (0, j)),
                  pl.BlockSpec((1, tn), lambda i, j: (0, j)),
                  pl.BlockSpec((HY_CONV, tn), lambda i, j: (0, j)),
                  pl.BlockSpec((1, tn), lambda i, j: (0, j))],
        out_specs=pl.BlockSpec((None, tm, tn), lambda i, j: (j // per, i, j % per)),
        out_shape=jax.ShapeDtypeStruct((n // D_MODEL, ROWS, D_MODEL), F32),
        scratch_shapes=[pltpu.VMEM((tm + 2 * SUB, k), BF16)],
        compiler_params=_params(("parallel", "arbitrary"),
                                _vmem_limit(_nbytes((tm, k), F32) + _nbytes((k, tn), BF16)
                                            + _nbytes((tm, tn), F32),
                                            resident=_nbytes((tm, k), BF16),
                                            temps=5 * _nbytes((tm, tn), F32))),
        name="hyena_in",
    )(h, h, h, mods, mods, w, b.reshape(1, n), conv_w, conv_b.reshape(1, n))


def _mm_postnorm_kernel(n_ctx, *refs):
    n_a = 1 if n_ctx is None else 2
    a_refs, (w_ref, bias_ref, h_ref, gate_ref, g_ref, b_ref, o_ref) = refs[:n_a], refs[n_a:]
    half = o_ref.shape[0] // 2
    halves = (slice(0, half), slice(half, 2 * half))

    def rows(r):
        if n_ctx is None:
            return a_refs[0][r, :]
        return jnp.where(pl.program_id(0) < n_ctx, a_refs[0][r, :], a_refs[1][r, :])

    ys = [_dot(rows(r), w_ref[...]) + bias_ref[...] for r in halves]
    for r, y in zip(halves, ys):
        o_ref[r, :] = _post_norm(h_ref[r, :], gate_ref[...] * y, g_ref[...], b_ref[...])


def _mm_postnorm(a, w, bias, h, mods, which_gate, ln_g, ln_b, tm=512):
    if isinstance(a, tuple):
        n_ctx = ROWS_CTX // tm
        k = a[0].shape[1]
        a_specs = [pl.BlockSpec((tm, k), lambda i: (jnp.minimum(i, n_ctx - 1), 0)),
                   pl.BlockSpec((tm, k), lambda i: (jnp.maximum(i - n_ctx, 0), 0))]
    else:
        n_ctx, k, a = None, a.shape[1], (a,)
        a_specs = [pl.BlockSpec((tm, k), lambda i: (i, 0))]
    return pl.pallas_call(
        functools.partial(_mm_postnorm_kernel, n_ctx),
        grid=(ROWS // tm,),
        in_specs=a_specs + [pl.BlockSpec((k, D_MODEL), lambda i: (0, 0)),
                            _row_spec(),
                            pl.BlockSpec((tm, D_MODEL), lambda i: (i, 0)),
                            _mod_spec(which_gate, tm), _row_spec(), _row_spec()],
        out_specs=pl.BlockSpec((tm, D_MODEL), lambda i: (i, 0)),
        out_shape=jax.ShapeDtypeStruct((ROWS, D_MODEL), F32),
        compiler_params=_params(("parallel",),
                                _vmem_limit(len(a) * _nbytes((tm, k), BF16)
                                            + _nbytes((k, D_MODEL), BF16)
                                            + 2 * _nbytes((tm, D_MODEL), F32),
                                            temps=3 * _nbytes((tm, D_MODEL), F32))),
        name="mm_postnorm",
    )(*a, w, bias.reshape(1, D_MODEL), h, mods, ln_g.reshape(1, D_MODEL), ln_b.reshape(1, D_MODEL))


def _ffn_kernel(n_ctx, h_ref, sh_ref, sc_ref, gate_ref, g_ref, b_ref, wg_ref, wu_ref, wd_ref,
                *refs):
    i, f = pl.program_id(0), pl.program_id(1)
    if n_ctx is None:
        o_ref, u_ref = refs
        acc_ref, outs = o_ref, ((o_ref, None),)
    else:
        octx_ref, olat_ref, u_ref, acc_ref = refs
        outs = ((octx_ref, i < n_ctx), (olat_ref, i >= n_ctx))

    @pl.when(f == 0)
    def _():
        u_ref[...] = _modulate(h_ref[...], sh_ref[...], sc_ref[...]).astype(BF16)
        acc_ref[...] = jnp.zeros_like(acc_ref)

    half = u_ref.shape[0] // 2
    halves = (slice(0, half), slice(half, 2 * half))
    proj = [(_dot(u_ref[r, :], wg_ref[...]), _dot(u_ref[r, :], wu_ref[...])) for r in halves]
    for r, (gate, up) in zip(halves, proj):
        act = (jax.nn.silu(gate) * up).astype(BF16)
        acc_ref[r, :] += _dot(act, wd_ref[...])

    last = f == pl.num_programs(1) - 1
    for o_ref, mine in outs:
        @pl.when(last if mine is None else jnp.logical_and(last, mine))
        def _(o_ref=o_ref):
            for r in halves:
                o_ref[r, :] = _post_norm(h_ref[r, :], gate_ref[...] * acc_ref[r, :],
                                         g_ref[...], b_ref[...])


def _ffn(h, mods, w_gate, w_up, w_down, layer, ln_g, ln_b, split_out, tm=512, tf=512):
    tile = _nbytes((tm, D_MODEL), F32)
    if split_out:
        n_ctx = ROWS_CTX // tm
        out_specs = [pl.BlockSpec((tm, D_MODEL), lambda i, f: (jnp.minimum(i, n_ctx - 1), 0)),
                     pl.BlockSpec((tm, D_MODEL), lambda i, f: (jnp.maximum(i - n_ctx, 0), 0))]
        out_shape = [jax.ShapeDtypeStruct((ROWS_CTX, D_MODEL), F32),
                     jax.ShapeDtypeStruct((ROWS_LAT, D_MODEL), F32)]
        scratch = [pltpu.VMEM((tm, D_MODEL), BF16), pltpu.VMEM((tm, D_MODEL), F32)]
        pipelined, resident = 3 * tile, tile + tile // 2
    else:
        n_ctx = None
        out_specs = pl.BlockSpec((tm, D_MODEL), lambda i, f: (i, 0))
        out_shape = jax.ShapeDtypeStruct((ROWS, D_MODEL), F32)
        scratch = [pltpu.VMEM((tm, D_MODEL), BF16)]
        pipelined, resident = 2 * tile, tile // 2
    return pl.pallas_call(
        functools.partial(_ffn_kernel, n_ctx),
        grid=(ROWS // tm, D_FF // tf),
        in_specs=[pl.BlockSpec((tm, D_MODEL), lambda i, f: (i, 0)),
                  _mod_spec(3, tm), _mod_spec(4, tm), _mod_spec(5, tm),
                  _row_spec(), _row_spec(),
                  pl.BlockSpec((None, D_MODEL, tf), lambda i, f: (layer, 0, f)),
                  pl.BlockSpec((None, D_MODEL, tf), lambda i, f: (layer, 0, f)),
                  pl.BlockSpec((None, tf, D_MODEL), lambda i, f: (layer, f, 0))],
        out_specs=out_specs,
        out_shape=out_shape,
        scratch_shapes=scratch,
        compiler_params=_params(("parallel", "arbitrary"),
                                _vmem_limit(pipelined + 3 * _nbytes((D_MODEL, tf), BF16),
                                            resident=resident, temps=2 * tile)),
        name="ffn",
    )(h, mods, mods, mods, ln_g.reshape(1, D_MODEL), ln_b.reshape(1, D_MODEL),
      w_gate, w_up, w_down)


def _mla_down_kernel(h_ref, sh_ref, sc_ref, wdq_ref, wdkv_ref, wkr_ref, qn_ref, kvn_ref,
                     rope_ref, cq_ref, ckv_ref, kr_ref, kr2_ref):
    u = _modulate(h_ref[...], sh_ref[...], sc_ref[...]).astype(BF16)
    cq_ref[...] = _rms_norm(_dot(u, wdq_ref[...]), qn_ref[...]).astype(BF16)
    ckv_ref[...] = _rms_norm(_dot(u, wdkv_ref[...]), kvn_ref[...])
    t = _dot(u, wkr_ref[...])
    kr_ref[...] = t[:, :QK_ROPE]
    v = t * rope_ref[...]
    kr2_ref[...] = (v + pltpu.roll(v, QK_ROPE, 1)).astype(BF16)


def _mla_down(h, mods, w_dq, w_dkv, w_kr2, q_norm, kv_norm, rope_tab, tm=512):
    row = lambda width: pl.BlockSpec((tm, width), lambda i: (i, 0))
    full = lambda shape: pl.BlockSpec(shape, lambda i: (0, 0))
    return pl.pallas_call(
        _mla_down_kernel,
        grid=(ROWS // tm,),
        in_specs=[row(D_MODEL), _mod_spec(0, tm), _mod_spec(1, tm),
                  full((D_MODEL, Q_RANK)), full((D_MODEL, KV_RANK)), full((D_MODEL, 2 * QK_ROPE)),
                  _row_spec(Q_RANK), _row_spec(KV_RANK), row(2 * QK_ROPE)],
        out_specs=[row(Q_RANK), row(KV_RANK), row(QK_ROPE), row(2 * QK_ROPE)],
        out_shape=[jax.ShapeDtypeStruct((ROWS, Q_RANK), BF16),
                   jax.ShapeDtypeStruct((ROWS, KV_RANK), F32),
                   jax.ShapeDtypeStruct((ROWS, QK_ROPE), F32),
                   jax.ShapeDtypeStruct((ROWS, 2 * QK_ROPE), BF16)],
        compiler_params=_params(("parallel",),
                                _vmem_limit(_nbytes((tm, D_MODEL), F32)
                                            + _nbytes((D_MODEL, Q_RANK + KV_RANK + 2 * QK_ROPE), BF16)
                                            + 3 * _nbytes((tm, KV_RANK), F32),
                                            temps=2 * _nbytes((tm, D_MODEL), F32))),
        name="mla_down",
    )(h, mods, mods, w_dq, w_dkv, w_kr2, q_norm.reshape(1, Q_RANK), kv_norm.reshape(1, KV_RANK),
      rope_tab)


NT_DIMS = (((1,), (1,)), ((), ()))


def _q_up_kernel(cq_ref, w_ref, rope_ref, q_ref):
    cq, tab = cq_ref[...], rope_ref[...] * LOG2E_SCALE
    for h in range(MLA_HEADS):
        r = _dot(cq, w_ref[:, h * HEAD_W:(h + 1) * HEAD_W])
        q_ref[h, :, :QK_NOPE] = (r[:, :QK_NOPE] * LOG2E_SCALE).astype(BF16)
        q_ref[h, :, QK_NOPE:] = (r[:, QK_NOPE:] * tab).astype(BF16)


def _q_up(cq, w_q, rope_tab, tm=512):
    return pl.pallas_call(
        _q_up_kernel,
        grid=(ROWS // tm,),
        in_specs=[pl.BlockSpec((tm, Q_RANK), lambda i: (i, 0)),
                  pl.BlockSpec((Q_RANK, MLA_HEADS * HEAD_W), lambda i: (0, 0)),
                  pl.BlockSpec((tm, 2 * QK_ROPE), lambda i: (i, 0))],
        out_specs=pl.BlockSpec((MLA_HEADS, tm, HEAD_W), lambda i: (0, i, 0)),
        out_shape=jax.ShapeDtypeStruct((MLA_HEADS, ROWS, HEAD_W), BF16),
        compiler_params=_params(("parallel",),
                                _vmem_limit(_nbytes((Q_RANK + tm, MLA_HEADS * HEAD_W), BF16))),
        name="q_up",
    )(cq, w_q, rope_tab)


def _kv_expand_kernel(ckv_ref, kr2_ref, wk_ref, wvt_ref, k_ref, vt_ref):
    c = ckv_ref[...].astype(BF16)
    vt_ref[...] = lax.dot_general(wvt_ref[...], c, NT_DIMS,
                                  preferred_element_type=F32).astype(BF16)
    kr2 = kr2_ref[...]
    pair_w = 2 * QK_NOPE
    for g in range(MLA_HEADS // 2):
        r = _dot(c, wk_ref[:, g * pair_w:(g + 1) * pair_w]).astype(BF16)
        for e in range(2):
            k_ref[2 * g + e, :, :QK_NOPE] = r[:, e * QK_NOPE:(e + 1) * QK_NOPE]
            k_ref[2 * g + e, :, QK_NOPE:] = kr2


def _kv_expand(ckv, kr2, w_k, w_vt, tm=512):
    rows = ckv.shape[0]
    return pl.pallas_call(
        _kv_expand_kernel,
        grid=(rows // tm,),
        in_specs=[pl.BlockSpec((tm, KV_RANK), lambda i: (i, 0)),
                  pl.BlockSpec((tm, 2 * QK_ROPE), lambda i: (i, 0)),
                  pl.BlockSpec((KV_RANK, MLA_HEADS * QK_NOPE), lambda i: (0, 0)),
                  pl.BlockSpec((MLA_HEADS * V_DIM, KV_RANK), lambda i: (0, 0))],
        out_specs=[pl.BlockSpec((MLA_HEADS, tm, HEAD_W), lambda i: (0, i, 0)),
                   pl.BlockSpec((MLA_HEADS * V_DIM, tm), lambda i: (0, i))],
        out_shape=[jax.ShapeDtypeStruct((MLA_HEADS, rows, HEAD_W), BF16),
                   jax.ShapeDtypeStruct((MLA_HEADS * V_DIM, rows), BF16)],
        compiler_params=_params(("parallel",),
                                _vmem_limit(_nbytes((tm, MLA_HEADS * (HEAD_W + V_DIM)), BF16)
                                            + 2 * _nbytes((KV_RANK, MLA_HEADS * V_DIM), BF16),
                                            temps=_nbytes((MLA_HEADS * V_DIM, tm), F32))),
        name="kv_expand",
    )(ckv, kr2, w_k, w_vt)


ATT_CHUNK = 512
ATT_SKEW = 3
LOG2E_SCALE = ATT_SCALE * math.log2(math.e)


def _attn_scores(q, k):
    return lax.dot_general(k, q, NT_DIMS, preferred_element_type=F32)


def _attn_values(s, vt, carry):
    m = jnp.max(s, axis=0, keepdims=True)
    if carry is not None:
        m_old, l_old, acc_old = carry
        m = jnp.maximum(m_old, m)
    p = jnp.exp2(s - m)
    l = jnp.sum(p, axis=0, keepdims=True)
    acc = _dot(vt, p.astype(BF16))
    if carry is not None:
        alpha = jnp.exp2(m_old - m)
        l = alpha * l_old + l
        acc = alpha * acc_old + acc
    return m, l, acc


def _attn_ctx_kernel(q_ref, k_ref, vt_ref, o_ref):
    for h in range(MLA_HEADS):
        _, l, acc = _attn_values(_attn_scores(q_ref[h], k_ref[h]),
                                 vt_ref[h * V_DIM:(h + 1) * V_DIM, :], None)
        o_ref[:, h * V_DIM:(h + 1) * V_DIM] = (acc / l).T.astype(BF16)


def _attn_lat_kernel(q_ref, k_ref, vt_ref, kc_ref, vtc_ref, prev_ref, o_ref):
    del prev_ref
    q = q_ref[...]
    n_tok = DEC_SEQ // ATT_CHUNK
    rows = lambda c: slice(c * ATT_CHUNK, (c + 1) * ATT_CHUNK)
    keys = [k_ref.at[rows(c), :] for c in range(n_tok)] + [kc_ref]
    vals = [vt_ref.at[:, rows(c)] for c in range(n_tok)] + [vtc_ref]
    carry = None
    scores = [_attn_scores(q, keys[c][...]) for c in range(ATT_SKEW)]
    for c in range(n_tok + 1):
        if c + ATT_SKEW <= n_tok:
            scores.append(_attn_scores(q, keys[c + ATT_SKEW][...]))
        carry = _attn_values(scores[c], vals[c][...], carry)
    _, l, acc = carry
    o_ref[...] = (acc / l).T.astype(BF16)


def _attention(q, k_tok, vt_tok, k_cache, vt_cache, tq=1024):
    assert PAST_LEN == ATT_CHUNK
    out_shape = jax.ShapeDtypeStruct((ROWS, MLA_HEADS * V_DIM), BF16)
    o = pl.pallas_call(
        _attn_ctx_kernel,
        grid=(BATCH,),
        in_specs=[pl.BlockSpec((MLA_HEADS, SEQ, HEAD_W), lambda s: (0, s, 0)),
                  pl.BlockSpec((MLA_HEADS, SEQ, HEAD_W), lambda s: (0, s, 0)),
                  pl.BlockSpec((MLA_HEADS * V_DIM, SEQ), lambda s: (0, s))],
        out_specs=pl.BlockSpec((SEQ, MLA_HEADS * V_DIM), lambda s: (s, 0)),
        out_shape=out_shape,
        compiler_params=_params(("parallel",), VMEM_FLOOR_BYTES),
        name="attn_ctx",
    )(q, k_tok, vt_tok)

    lat0 = ROWS_CTX // DEC_SEQ
    q0 = ROWS_CTX // tq
    nq = DEC_SEQ // tq
    return pl.pallas_call(
        _attn_lat_kernel,
        grid=(DEC_BATCH, MLA_HEADS, nq),
        in_specs=[pl.BlockSpec((None, tq, HEAD_W), lambda b, h, i: (h, q0 + b * nq + i, 0)),
                  pl.BlockSpec((None, DEC_SEQ, HEAD_W), lambda b, h, i: (h, lat0 + b, 0)),
                  pl.BlockSpec((V_DIM, DEC_SEQ), lambda b, h, i: (h, lat0 + b)),
                  pl.BlockSpec((None, PAST_LEN, HEAD_W), lambda b, h, i: (h, b, 0)),
                  pl.BlockSpec((V_DIM, PAST_LEN), lambda b, h, i: (h, b)),
                  pl.BlockSpec(memory_space=pl.ANY)],
        out_specs=pl.BlockSpec((tq, V_DIM), lambda b, h, i: (q0 + b * nq + i, h)),
        out_shape=out_shape,
        input_output_aliases={5: 0},
        compiler_params=_params(("parallel", "parallel", "arbitrary"),
                                _vmem_limit(_nbytes((DEC_SEQ + PAST_LEN, HEAD_W + V_DIM), BF16),
                                            temps=8 * _nbytes((ATT_CHUNK, tq), F32))),
        name="attn_lat",
    )(q, k_tok, vt_tok, k_cache, vt_cache, o)


def _mm_kernel(a_ref, b_ref, o_ref, acc_ref):
    k = pl.program_id(2)

    @pl.when(k == 0)
    def _():
        acc_ref[...] = jnp.zeros_like(acc_ref)

    acc_ref[...] += _dot(a_ref[...], b_ref[...].astype(BF16))

    @pl.when(k == pl.num_programs(2) - 1)
    def _():
        o_ref[...] = acc_ref[...].astype(o_ref.dtype)


def _mm(a, b, out_dtype=F32, tm=1024, tn=1024, tk=512):
    m, kk = a.shape
    n = b.shape[1]
    tm, tn, tk = min(tm, m), min(tn, n), min(tk, kk)
    return pl.pallas_call(
        _mm_kernel,
        grid=(m // tm, n // tn, kk // tk),
        in_specs=[pl.BlockSpec((tm, tk), lambda i, j, k: (i, k)),
                  pl.BlockSpec((tk, tn), lambda i, j, k: (k, j))],
        out_specs=pl.BlockSpec((tm, tn), lambda i, j, k: (i, j)),
        out_shape=jax.ShapeDtypeStruct((m, n), out_dtype),
        scratch_shapes=[pltpu.VMEM((tm, tn), F32)],
        compiler_params=_params(("parallel", "parallel", "arbitrary"), VMEM_FLOOR_BYTES),
        name="mm",
    )(a, b)


def _seq_tiles(length):
    if length >= 1024:
        return 1024, 512, 1024
    return length, D_MODEL, length


def _dft_fwd_kernel(c_ref, s_ref, z_ref, kr_ref, ks_ref, yr_ref, ys_ref, accr_ref, accs_ref):
    k = pl.program_id(3)

    @pl.when(k == 0)
    def _():
        accr_ref[...] = jnp.zeros_like(accr_ref)
        accs_ref[...] = jnp.zeros_like(accs_ref)

    z = z_ref[...].astype(BF16)
    accr_ref[...] += _dot(c_ref[...], z)
    accs_ref[...] += _dot(s_ref[...], z)

    @pl.when(k == pl.num_programs(3) - 1)
    def _():
        zr, zs, kr, ks = accr_ref[...], accs_ref[...], kr_ref[...], ks_ref[...]
        yr_ref[...] = (zr * kr - zs * ks).astype(BF16)
        ys_ref[...] = (zr * ks + zs * kr).astype(BF16)


def _dft_fwd(tabs, z, z_which, kr, ks, order, n_seq, length):
    tf, tn, tk = _seq_tiles(length)
    nb, nf, nk = D_MODEL // tn, length // tf, length // tk
    kc0 = order * nb
    out = jax.ShapeDtypeStruct((n_seq * length, D_MODEL), BF16)
    return pl.pallas_call(
        _dft_fwd_kernel,
        grid=(n_seq, nf, nb, nk),
        in_specs=[pl.BlockSpec((tf, tk), lambda s, f, n, k: (f, k)),
                  pl.BlockSpec((tf, tk), lambda s, f, n, k: (f, k)),
                  pl.BlockSpec((None, tk, tn), lambda s, f, n, k: (z_which, s * nk + k, n)),
                  pl.BlockSpec((tf, tn), lambda s, f, n, k: (f, kc0 + n)),
                  pl.BlockSpec((tf, tn), lambda s, f, n, k: (f, kc0 + n))],
        out_specs=[pl.BlockSpec((tf, tn), lambda s, f, n, k: (s * nf + f, n)),
                   pl.BlockSpec((tf, tn), lambda s, f, n, k: (s * nf + f, n))],
        out_shape=[out, out],
        scratch_shapes=[pltpu.VMEM((tf, tn), F32), pltpu.VMEM((tf, tn), F32)],
        compiler_params=_params(("parallel", "parallel", "parallel", "arbitrary"),
                                _vmem_limit(2 * _nbytes((tf, tk), BF16) + _nbytes((tk, tn), F32)
                                            + 2 * _nbytes((tf, tn), F32) + 2 * _nbytes((tf, tn), BF16),
                                            resident=2 * _nbytes((tf, tn), F32),
                                            temps=4 * _nbytes((tf, tn), F32))),
        name="dft_fwd",
    )(tabs["c"], tabs["s"], z, kr, ks)


def _dft_inv_kernel(inv_len, ct_ref, st_ref, yr_ref, ys_ref, z_ref, gate_ref, skip_ref,
                    o_ref, acc_ref):
    k = pl.program_id(3)

    @pl.when(k == 0)
    def _():
        acc_ref[...] = jnp.zeros_like(acc_ref)

    acc_ref[...] += _dot(ct_ref[...], yr_ref[...]) + _dot(st_ref[...], ys_ref[...])

    @pl.when(k == pl.num_programs(3) - 1)
    def _():
        y = acc_ref[...] * inv_len + skip_ref[...] * z_ref[...]
        o_ref[...] = (gate_ref[...] * y).astype(o_ref.dtype)


def _dft_inv(tabs, yr, ys, z, z_which, gate, gate_which, skip, order, n_seq, length, out_dtype):
    tt, tn, tk = _seq_tiles(length)
    nb, nt, nk = D_MODEL // tn, length // tt, length // tk
    return pl.pallas_call(
        functools.partial(_dft_inv_kernel, 1.0 / length),
        grid=(n_seq, nt, nb, nk),
        in_specs=[pl.BlockSpec((tt, tk), lambda s, t, n, k: (t, k)),
                  pl.BlockSpec((tt, tk), lambda s, t, n, k: (t, k)),
                  pl.BlockSpec((tk, tn), lambda s, t, n, k: (s * nk + k, n)),
                  pl.BlockSpec((tk, tn), lambda s, t, n, k: (s * nk + k, n)),
                  pl.BlockSpec((None, tt, tn), lambda s, t, n, k: (z_which, s * nt + t, n)),
                  pl.BlockSpec((None, tt, tn), lambda s, t, n, k: (gate_which, s * nt + t, n)),
                  pl.BlockSpec((None, 1, tn), lambda s, t, n, k: (order, 0, n))],
        out_specs=pl.BlockSpec((tt, tn), lambda s, t, n, k: (s * nt + t, n)),
        out_shape=jax.ShapeDtypeStruct((n_seq * length, D_MODEL), out_dtype),
        scratch_shapes=[pltpu.VMEM((tt, tn), F32)],
        compiler_params=_params(("parallel", "parallel", "parallel", "arbitrary"),
                                _vmem_limit(2 * _nbytes((tt, tk), BF16) + 2 * _nbytes((tk, tn), BF16)
                                            + 3 * _nbytes((tt, tn), F32),
                                            resident=_nbytes((tt, tn), F32),
                                            temps=3 * _nbytes((tt, tn), F32))),
        name="dft_inv",
    )(tabs["ct"], tabs["st"], yr, ys, z, gate, skip)


TW_LANES = 128
SUB = 8
CT_ROWS = 2 * SUB
CT_COLS = 1024


def _lane_tile(x, width):
    return jnp.tile(x, (1, width // x.shape[-1]))


def _kron_sub(f):
    return jnp.kron(f, jnp.eye(SUB, dtype=f.dtype))


def _sub_rows(x, h):
    part = x[:, h * SUB:(h + 1) * SUB, :]
    return part.reshape(part.shape[0] * SUB, part.shape[2])


def _from_sub_rows(parts):
    split = [p.reshape(p.shape[0] // SUB, SUB, p.shape[1]) for p in parts]
    return jnp.concatenate(split, axis=1)


def _ct_stage_a_kernel(n_in, *refs):
    x_refs, (fa_ref, cw_ref, sw_ref, br_ref, bi_ref) = refs[:n_in], refs[n_in:]
    fa = fa_ref[...]
    half = fa.shape[0] // 2
    width = br_ref.shape[-1]
    xs = [r[...] for r in x_refs]
    b_re, b_im = [], []
    for h in range(CT_ROWS // SUB):
        x = jnp.concatenate([_sub_rows(x, h) for x in xs], axis=0).astype(BF16)
        a = _dot(fa, x)
        ar, ai = a[:half], a[half:]
        cw, sw = _lane_tile(cw_ref[h], width), _lane_tile(sw_ref[h], width)
        b_re.append(ar * cw + ai * sw)
        b_im.append(ai * cw - ar * sw)
    br_ref[...] = _from_sub_rows(b_re).astype(BF16)
    bi_ref[...] = _from_sub_rows(b_im).astype(BF16)


def _ct_stage_a(xs, n_seq, fa, cw, sw):
    n1 = fa.shape[0] // (2 * SUB)
    n2 = cw.shape[0] * CT_ROWS
    width = xs[0][0].shape[-1]
    tn2, tw = CT_ROWS, CT_COLS
    in_specs, blocks = [], 0
    for arr, prefix in xs:
        rows_in = arr.shape[-3]
        lead = (None,) * (arr.ndim - 3)
        in_specs.append(pl.BlockSpec(lead + (rows_in, tn2, tw),
                                     lambda s, i, c, prefix=prefix: prefix(s) + (0, i, c)))
        blocks += _nbytes((rows_in, tn2, tw), arr.dtype)
    twid = pl.BlockSpec((None,) + cw.shape[1:], lambda s, i, c: (i, 0, 0, 0))
    in_specs += [pl.BlockSpec(fa.shape, lambda s, i, c: (0, 0)), twid, twid]
    out = jax.ShapeDtypeStruct((n_seq, n1, n2, width), BF16)
    out_spec = pl.BlockSpec((None, n1, tn2, tw), lambda s, i, c: (s, 0, i, c))
    return pl.pallas_call(
        functools.partial(_ct_stage_a_kernel, len(xs)),
        grid=(n_seq, n2 // tn2, width // tw),
        in_specs=in_specs,
        out_specs=[out_spec, out_spec],
        out_shape=[out, out],
        compiler_params=_params(("parallel", "parallel", "parallel"),
                                _vmem_limit(blocks + 2 * _nbytes((n1, tn2, tw), BF16)
                                            + _nbytes(fa.shape, BF16),
                                            temps=8 * _nbytes((n1, tn2, tw), F32))),
        name="ct_stage_a",
    )(*[arr for arr, _ in xs], fa, cw, sw)


def _ct_mid_kernel(br_ref, bi_ref, pr_ref, pi_ref, mr_ref, mi_ref, fb_ref, fbi_ref, cw_ref, sw_ref,
                   vr_ref, vi_ref):
    fb, fbi = fb_ref[...], fbi_ref[...]
    half = fb.shape[0] // 2
    width = br_ref.shape[-1]
    for j in range(br_ref.shape[0]):
        stack = lambda re_ref, im_ref: jnp.concatenate([re_ref[j], im_ref[j]], axis=0)
        kr = _dot(fb[:half], stack(pr_ref, pi_ref))
        ki = _dot(fb[half:], stack(mr_ref, mi_ref))
        x = _dot(fb, stack(br_ref, bi_ref))
        xr, xi = x[:half], x[half:]
        y = jnp.concatenate([xr * kr - xi * ki, xr * ki + xi * kr], axis=0).astype(BF16)
        v = _dot(fbi, y)
        vr, vi = v[:half], v[half:]
        cw, sw = _lane_tile(cw_ref[j], width), _lane_tile(sw_ref[j], width)
        vr_ref[j] = (vr * cw - vi * sw).astype(BF16)
        vi_ref[j] = (vi * cw + vr * sw).astype(BF16)


def _ct_mid(br, bi, filt, order, fb, fbi, cw, sw, tk1=4, td=1024):
    n1, n2, d = br.shape
    nd = d // td
    data = pl.BlockSpec((tk1, n2, td), lambda i, j: (i, 0, j))
    coef = pl.BlockSpec((tk1, n2, td), lambda i, j: (i, 0, order * nd + j))
    mat = pl.BlockSpec(fb.shape, lambda i, j: (0, 0))
    tw = pl.BlockSpec((tk1, n2, TW_LANES), lambda i, j: (i, 0, 0))
    out = jax.ShapeDtypeStruct((n1, n2, d), BF16)
    return pl.pallas_call(
        _ct_mid_kernel,
        grid=(n1 // tk1, nd),
        in_specs=[data, data, coef, coef, coef, coef, mat, mat, tw, tw],
        out_specs=[data, data],
        out_shape=[out, out],
        compiler_params=_params(("parallel", "parallel"),
                                _vmem_limit(8 * _nbytes((tk1, n2, td), BF16),
                                            temps=10 * _nbytes((2 * n2, td), F32))),
        name="ct_mid",
    )(br, bi, *filt, fb, fbi, cw, sw)


def _ct_inv_a_kernel(scale, vr_ref, vi_ref, fai_ref, z0_ref, z1_ref, g0_ref, g1_ref, skip_ref,
                     o_ref):
    fai = fai_ref[...]
    half = fai.shape[0] // 2
    skip = skip_ref[...]
    vr, vi = vr_ref[...].astype(F32), vi_ref[...].astype(F32)
    zs, gs = (z0_ref[...], z1_ref[...]), (g0_ref[...], g1_ref[...])
    outs = ([], [])
    for h in range(CT_ROWS // SUB):
        v = jnp.concatenate([_sub_rows(vr, h), _sub_rows(vi, h)], axis=0).astype(BF16)
        y = _dot(fai, v) * scale
        for b, yb in enumerate((y[:half], y[half:])):
            outs[b].append(_sub_rows(gs[b], h) * (yb + skip * _sub_rows(zs[b], h)))
    for b in range(2):
        o_ref[b] = _from_sub_rows(outs[b]).astype(o_ref.dtype)


def _ct_inv_a(vr, vi, fai, z, z_which, gate, gate_which, skip, order, out_dtype):
    n1, n2, d = vr.shape
    rows = fai.shape[0] // (2 * SUB)
    tn2, tw = CT_ROWS, CT_COLS // 2
    spec = pl.BlockSpec((n1, tn2, tw), lambda i, c: (0, i, c))
    pair = lambda which, b: pl.BlockSpec((None, None, rows, tn2, tw),
                                         lambda i, c: (which[0], which[1] + b, 0, i, c))
    return pl.pallas_call(
        functools.partial(_ct_inv_a_kernel, 1.0 / (n1 * n2)),
        grid=(n2 // tn2, d // tw),
        in_specs=[spec, spec, pl.BlockSpec(fai.shape, lambda i, c: (0, 0)),
                  pair(z_which, 0), pair(z_which, 1), pair(gate_which, 0), pair(gate_which, 1),
                  pl.BlockSpec((None, 1, tw), lambda i, c: (order, 0, c))],
        out_specs=pl.BlockSpec((2, rows, tn2, tw), lambda i, c: (0, 0, i, c)),
        out_shape=jax.ShapeDtypeStruct((2, rows, n2, d), out_dtype),
        compiler_params=_params(("parallel", "parallel"),
                                _vmem_limit(2 * _nbytes((n1, tn2, tw), BF16)
                                            + 6 * _nbytes((rows, tn2, tw), F32)
                                            + _nbytes(fai.shape, BF16),
                                            temps=8 * _nbytes((n1, tn2, tw), F32))),
        name="ct_inv_a",
    )(vr, vi, fai, z, z, gate, gate, skip)


def _ct_real_b_kernel(scale, br_ref, bi_ref, fb_ref, o_ref, so_ref):
    fb = fb_ref[...]
    for j in range(br_ref.shape[0]):
        so_ref[:, j, :] = _dot(fb, jnp.concatenate([br_ref[j], bi_ref[j]], axis=0)) * scale
    o_ref[...] = so_ref[...].astype(o_ref.dtype)


def _ct_real_b(br, bi, fb_re, scale):
    n_seq, n1, n2, d = br.shape
    tk1, tw = CT_ROWS, CT_COLS
    blk = pl.BlockSpec((None, tk1, n2, tw), lambda s, i, c: (s, i, 0, c))
    return pl.pallas_call(
        functools.partial(_ct_real_b_kernel, scale),
        grid=(n_seq, n1 // tk1, d // tw),
        in_specs=[blk, blk, pl.BlockSpec(fb_re.shape, lambda s, i, c: (0, 0))],
        out_specs=pl.BlockSpec((None, n2, tk1, tw), lambda s, i, c: (s, 0, i, c)),
        out_shape=jax.ShapeDtypeStruct((n_seq, n2, n1, d), BF16),
        scratch_shapes=[pltpu.VMEM((n2, tk1, tw), F32)],
        compiler_params=_params(("parallel", "parallel", "parallel"),
                                _vmem_limit(3 * _nbytes((tk1, n2, tw), BF16),
                                            resident=_nbytes((n2, tk1, tw), F32),
                                            temps=2 * _nbytes((n2, tk1, tw), F32))),
        name="ct_real_b",
    )(br, bi, fb_re)


def _cos_sin(num, den):
    ang = (num % den).astype(F32) * (2.0 * math.pi / den)
    return jnp.cos(ang), jnp.sin(ang)


def _ct_tables(n1, n2):
    i1 = jnp.arange(n1, dtype=jnp.int32)
    i2 = jnp.arange(n2, dtype=jnp.int32)
    c1, s1 = _cos_sin(i1[:, None] * i1[None, :], n1)
    c2, s2 = _cos_sin(i2[:, None] * i2[None, :], n2)
    cw, sw = _cos_sin(i2[:, None] * i1[None, :], n1 * n2)
    lanes = lambda t: jnp.broadcast_to(t[..., None], t.shape + (TW_LANES,))

    def stage_a_rows(t):
        t = t.reshape(n2 // CT_ROWS, CT_ROWS // SUB, SUB, n1)
        return lanes(jnp.swapaxes(t, 2, 3).reshape(n2 // CT_ROWS, CT_ROWS // SUB, n1 * SUB))

    return {"c1": c1, "s1": s1, "c2": c2, "s2": s2,
            "cw_a": stage_a_rows(cw), "sw_a": stage_a_rows(sw),
            "cw_b": lanes(cw.T), "sw_b": lanes(sw.T)}


def _filter_mlp_kernel(feat_ref, t_ref, w1_ref, b1_ref, fr1_ref, w2_ref, b2_ref, fr2_ref,
                       w3_ref, decay_ref, h_ref, ss_ref):
    x = jnp.sin(fr1_ref[...] * (_dot(feat_ref[...].astype(BF16), w1_ref[...].astype(BF16))
                                + b1_ref[...]))
    x = jnp.sin(fr2_ref[...] * (_dot(x.astype(BF16), w2_ref[...].astype(BF16)) + b2_ref[...]))
    h = _dot(x.astype(BF16), w3_ref[...].astype(BF16))
    h = h * (jnp.exp(-t_ref[...] * jnp.exp(decay_ref[...])) + HY_SHIFT)
    h_ref[...] = h

    @pl.when(pl.program_id(0) == 0)
    def _():
        ss_ref[...] = jnp.zeros_like(ss_ref)

    ss_ref[...] += jnp.sum(h * h, axis=0, keepdims=True)


def _filter_combine_kernel(hf_ref, hb_ref, ssf_ref, ssb_ref, a_ref, b_ref):
    norm = lax.rsqrt(ssf_ref[...] + ssb_ref[...] + 1e-12)
    fwd = hf_ref[...] * norm
    bwd = hb_ref[...] * norm
    row = lax.broadcasted_iota(jnp.int32, bwd.shape, 0) + pl.program_id(0) * bwd.shape[0]
    bwd = jnp.where(row == 0, 0.0, bwd)
    a_ref[...] = fwd + bwd
    b_ref[...] = fwd - bwd


def _hyena_filters(length, f_w1, f_b1, f_freq1, f_w2, f_b2, f_freq2, f_w3, log_decay):
    t = jnp.linspace(0.0, 1.0, length, dtype=F32)[:, None]
    t_idx = jnp.arange(length, dtype=F32)[:, None]
    bands = jnp.linspace(1e-4, HY_BANDS - 1, HY_BANDS, dtype=F32)
    w = 2.0 * math.pi * t_idx * bands / length
    feat = jnp.concatenate([t, jnp.cos(w), -jnp.sin(w)], axis=-1)
    emb_pad = 128
    feat = jnp.pad(feat, ((0, 0), (0, emb_pad - HY_EMB)))
    w1 = jnp.pad(f_w1, ((0, emb_pad - HY_EMB), (0, 0)))
    n_all = HY_DIRS * HY_ORDER * D_MODEL
    n_dir = HY_ORDER * D_MODEL
    tm = 256
    full = lambda shape: pl.BlockSpec(shape, lambda i: (0, 0))
    h, ss = pl.pallas_call(
        _filter_mlp_kernel,
        grid=(length // tm,),
        in_specs=[pl.BlockSpec((tm, emb_pad), lambda i: (i, 0)),
                  pl.BlockSpec((tm, 1), lambda i: (i, 0)),
                  full((emb_pad, HY_FW)), full((1, HY_FW)), full((1, HY_FW)),
                  full((HY_FW, HY_FW)), full((1, HY_FW)), full((1, HY_FW)),
                  full((HY_FW, n_all)), full((1, n_all))],
        out_specs=[pl.BlockSpec((tm, n_all), lambda i: (i, 0)), full((1, n_all))],
        out_shape=[jax.ShapeDtypeStruct((length, n_all), F32),
                   jax.ShapeDtypeStruct((1, n_all), F32)],
        compiler_params=_params(("arbitrary",),
                                _vmem_limit(_nbytes((tm, n_all), F32) + _nbytes((HY_FW, n_all), F32),
                                            temps=3 * _nbytes((tm, n_all), F32))),
        name="filter_mlp",
    )(feat, t, w1, f_b1.reshape(1, HY_FW), f_freq1.reshape(1, HY_FW), f_w2,
      f_b2.reshape(1, HY_FW), f_freq2.reshape(1, HY_FW), f_w3,
      log_decay.reshape(1, n_all))

    tn = 1024
    nb = n_dir // tn
    comb = jax.ShapeDtypeStruct((length, n_dir), F32)
    a, b = pl.pallas_call(
        _filter_combine_kernel,
        grid=(length // tm, nb),
        in_specs=[pl.BlockSpec((tm, tn), lambda i, j: (i, j)),
                  pl.BlockSpec((tm, tn), lambda i, j: (i, nb + j)),
                  pl.BlockSpec((1, tn), lambda i, j: (0, j)),
                  pl.BlockSpec((1, tn), lambda i, j: (0, nb + j))],
        out_specs=[pl.BlockSpec((tm, tn), lambda i, j: (i, j)),
                   pl.BlockSpec((tm, tn), lambda i, j: (i, j))],
        out_shape=[comb, comb],
        compiler_params=_params(("parallel", "parallel"), VMEM_FLOOR_BYTES),
        name="filter_combine",
    )(h, h, ss, ss)
    return a, b


def _cis_product(row_hi, row_lo, period):
    def cis(phase):
        ang = (phase % period).astype(F32) * (2.0 * math.pi / period)
        return jnp.cos(ang)[:, :, None], jnp.sin(ang)[:, :, None]
    (c1, s1), (c0, s0) = cis(row_hi), cis(row_lo)
    c0, s0 = jnp.swapaxes(c0, 1, 2), jnp.swapaxes(s0, 1, 2)
    rows = row_hi.shape[0]
    return ((c1 * c0 - s1 * s0).reshape(rows, -1), (s1 * c0 + c1 * s0).reshape(rows, -1))


def _odd_dft_tables(length):
    split = 1 << (length.bit_length() // 2)
    r = jnp.arange(length, dtype=jnp.int32)[:, None]
    hi = jnp.arange(length // split, dtype=jnp.int32)[None, :] * split
    lo = jnp.arange(split, dtype=jnp.int32)[None, :]
    c, s = _cis_product((2 * r + 1) * hi, (2 * r + 1) * lo, 4 * length)
    ct, st = _cis_product(r * (2 * hi), r * (2 * lo + 1), 4 * length)
    return {"c": c.astype(BF16), "s": s.astype(BF16), "ct": ct.astype(BF16), "st": st.astype(BF16)}


def _dft_tables(length):
    split = 1 << (length.bit_length() // 2)
    r = jnp.arange(length, dtype=jnp.int32)[:, None]
    hi = jnp.arange(length // split, dtype=jnp.int32)[None, :] * split
    lo = jnp.arange(split, dtype=jnp.int32)[None, :]
    c, s = _cis_product(r * hi, r * lo, length)
    return c.astype(BF16), (-s).astype(BF16)


def _hyena_mix_dense(pc, filt_p, filt_m, skip, n_seq, length):
    tabs = _odd_dft_tables(length)
    kr, ks = _mm(tabs["c"], filt_p), _mm(tabs["s"], filt_m)
    yr, ys = _dft_fwd(tabs, pc, 0, kr, ks, 0, n_seq, length)
    z1 = _dft_inv(tabs, yr, ys, pc, 0, pc, 1, skip, 0, n_seq, length, F32)[None]
    yr, ys = _dft_fwd(tabs, z1, 0, kr, ks, 1, n_seq, length)
    return _dft_inv(tabs, yr, ys, z1, 0, pc, 2, skip, 1, n_seq, length, BF16)


HY_N1, HY_N2 = 64, 128


def _hyena_mix_pair(pc, filt_p, filt_m, skip):
    assert DEC_BATCH == 2 and HY_N1 * HY_N2 == 2 * DEC_SEQ and ROWS % DEC_SEQ == 0
    n1, n2, rows_in = HY_N1, HY_N2, HY_N1 // 2
    t = _ct_tables(n1, n2)
    c_in, s_in = t["c1"][:, :rows_in], t["s1"][:, :rows_in]
    fa = _kron_sub(jnp.block([[c_in, s_in], [-s_in, c_in]])).astype(BF16)
    fa_real = _kron_sub(jnp.concatenate([c_in, -s_in], axis=0)).astype(BF16)
    fb = jnp.block([[t["c2"], t["s2"]], [-t["s2"], t["c2"]]]).astype(BF16)
    fbi = jnp.block([[t["c2"], -t["s2"]], [t["s2"], t["c2"]]]).astype(BF16)
    c_out, s_out = t["c1"][:rows_in], t["s1"][:rows_in]
    fai = _kron_sub(jnp.block([[c_out, -s_out], [s_out, c_out]])).astype(BF16)
    n_filt = HY_ORDER * D_MODEL
    filt = []
    for f in (filt_p, filt_m):
        re, im = _ct_stage_a([(f.reshape(rows_in, n2, n_filt), lambda s: ())], 1, fa_real,
                             t["cw_a"], t["sw_a"])
        filt += [re[0], im[0]]

    pc = pc.reshape(3, ROWS // DEC_SEQ, rows_in, n2, D_MODEL)
    z, z0, out_dtypes = pc, ROWS_CTX // DEC_SEQ, (F32, BF16)
    for order in range(HY_ORDER):
        br, bi = _ct_stage_a([(z, lambda s, z0=z0: (0, z0)), (z, lambda s, z0=z0: (0, z0 + 1))], 1, fa,
                             t["cw_a"], t["sw_a"])
        vr, vi = _ct_mid(br[0], bi[0], filt, order, fb, fbi, t["cw_b"], t["sw_b"])
        z = _ct_inv_a(vr, vi, fai, z, (0, z0), pc, (1 + order, ROWS_CTX // DEC_SEQ), skip, order,
                      out_dtypes[order])[None]
        z0 = 0
    return z.reshape(ROWS_LAT, D_MODEL)


def _fnet_chan_kernel(h_ref, sh_ref, sc_ref, w_ref, p_ref, q_ref):
    u = _modulate(h_ref[...], sh_ref[...], sc_ref[...]).astype(BF16)
    w = w_ref[...]
    for g in range(FNET_GROUPS):
        cols = slice(g * FNET_CG, (g + 1) * FNET_CG)
        r = _dot(u[:, cols], w)
        p_ref[:, cols] = r[:, :FNET_CG]
        q_ref[:, cols] = r[:, FNET_CG:]


def _fnet_chan(h, mods, w_cs, tm=512):
    row = pl.BlockSpec((tm, D_MODEL), lambda i: (i, 0))
    out = jax.ShapeDtypeStruct((ROWS, D_MODEL), F32)
    return pl.pallas_call(
        _fnet_chan_kernel,
        grid=(ROWS // tm,),
        in_specs=[row, _mod_spec(0, tm), _mod_spec(1, tm),
                  pl.BlockSpec((FNET_CG, 2 * FNET_CG), lambda i: (0, 0))],
        out_specs=[row, row],
        out_shape=[out, out],
        compiler_params=_params(("parallel",),
                                _vmem_limit(3 * _nbytes((tm, D_MODEL), F32),
                                            temps=2 * _nbytes((tm, D_MODEL), F32))),
        name="fnet_chan",
    )(h, mods, mods, w_cs)


def _fnet_pos_kernel(scale, c_ref, ns_ref, p_ref, q_ref, o_ref, acc_ref):
    k = pl.program_id(3)

    @pl.when(k == 0)
    def _():
        acc_ref[...] = jnp.zeros_like(acc_ref)

    acc_ref[...] += (_dot(c_ref[...], p_ref[...].astype(BF16))
                     + _dot(ns_ref[...], q_ref[...].astype(BF16)))

    @pl.when(k == pl.num_programs(3) - 1)
    def _():
        o_ref[...] = (acc_ref[...] * scale).astype(o_ref.dtype)


def _fnet_pos(c_tab, ns_tab, p, q, n_seq, length):
    tt, tn, tk = _seq_tiles(length)
    nb, nt, nk = D_MODEL // tn, length // tt, length // tk
    scale = (length * FNET_CG) ** -0.5
    return pl.pallas_call(
        functools.partial(_fnet_pos_kernel, scale),
        grid=(n_seq, nt, nb, nk),
        in_specs=[pl.BlockSpec((tt, tk), lambda s, t, n, k: (t, k)),
                  pl.BlockSpec((tt, tk), lambda s, t, n, k: (t, k)),
                  pl.BlockSpec((tk, tn), lambda s, t, n, k: (s * nk + k, n)),
                  pl.BlockSpec((tk, tn), lambda s, t, n, k: (s * nk + k, n))],
        out_specs=pl.BlockSpec((tt, tn), lambda s, t, n, k: (s * nt + t, n)),
        out_shape=jax.ShapeDtypeStruct((n_seq * length, D_MODEL), BF16),
        scratch_shapes=[pltpu.VMEM((tt, tn), F32)],
        compiler_params=_params(("parallel", "parallel", "parallel", "arbitrary"),
                                _vmem_limit(2 * _nbytes((tt, tk), BF16) + 2 * _nbytes((tk, tn), BF16)
                                            + _nbytes((tt, tn), BF16),
                                            resident=_nbytes((tt, tn), F32),
                                            temps=2 * _nbytes((tt, tn), F32))),
        name="fnet_pos",
    )(c_tab, ns_tab, p, q)


FN_N1, FN_N2 = 32, 128


def _fnet_pos_factored(p, q):
    assert FN_N1 * FN_N2 == DEC_SEQ and ROWS % DEC_SEQ == 0
    n1, n2 = FN_N1, FN_N2
    t = _ct_tables(n1, n2)
    fa = _kron_sub(jnp.block([[t["c1"], -t["s1"]], [-t["s1"], -t["c1"]]])).astype(BF16)
    fb_re = jnp.concatenate([t["c2"], t["s2"]], axis=1).astype(BF16)
    lat0 = ROWS_CTX // DEC_SEQ
    view = lambda x: x.reshape(ROWS // DEC_SEQ, n1, n2, D_MODEL)
    seq = lambda s: (lat0 + s,)
    br, bi = _ct_stage_a([(view(p), seq), (view(q), seq)], DEC_BATCH, fa, t["cw_a"], t["sw_a"])
    f = _ct_real_b(br, bi, fb_re, (DEC_SEQ * FNET_CG) ** -0.5)
    return f.reshape(ROWS_LAT, D_MODEL)


def _rope_table():
    rows = DEC_SEQ // GRID_W
    row = jnp.repeat(jnp.arange(rows), GRID_W).astype(F32)
    col = jnp.tile(jnp.arange(GRID_W), rows).astype(F32)
    inv = ROPE_THETA ** (-jnp.arange(0, AXIS_ROPE, 2, dtype=F32) / AXIS_ROPE)
    ang = jnp.concatenate([row[:, None] * inv, col[:, None] * inv], axis=-1)
    cos = jnp.repeat(jnp.cos(ang), 2, axis=-1)
    sin = jnp.repeat(jnp.sin(ang), 2, axis=-1)
    lat = jnp.tile(jnp.concatenate([cos, sin], axis=-1), (DEC_BATCH, 1))
    ctx = jnp.concatenate([jnp.ones((ROWS_CTX, QK_ROPE), F32), jnp.zeros((ROWS_CTX, QK_ROPE), F32)],
                          axis=-1)
    return jnp.concatenate([ctx, lat], axis=0)


def _pair_rotated(w):
    pairs = w.reshape(w.shape[:-1] + (QK_ROPE // 2, 2))
    return jnp.stack([-pairs[..., 1], pairs[..., 0]], axis=-1).reshape(w.shape)


def kernel(x_prompt, x_sample, c, cache_ckv, cache_krope, c_ctx, ada_w, ada_b, ln_g, ln_b, ffn_w_gate, ffn_w_up, ffn_w_down, mla_w_dq, mla_q_norm, mla_w_uq, mla_w_dkv, mla_kv_norm, mla_w_kr, mla_w_ukv, mla_w_o, hy_w_in, hy_b_in, hy_conv_w, hy_conv_b, hy_f_w1, hy_f_b1, hy_f_freq1, hy_f_w2, hy_f_b2, hy_f_freq2, hy_f_w3, hy_log_decay, hy_skip, hy_w_out, hy_b_out, fn_w_out, fn_b_out):
    assert x_prompt.shape == (BATCH, SEQ, D_MODEL) and x_sample.shape == (DEC_BATCH, DEC_SEQ, D_MODEL)
    assert ROWS_CTX % DEC_SEQ == 0 and SEQ == FNET_CG

    h = jnp.concatenate([x_prompt.reshape(ROWS_CTX, D_MODEL), x_sample.reshape(ROWS_LAT, D_MODEL)])
    cond = jnp.concatenate([c_ctx[None, :], c, jnp.zeros((COND_PAD - N_COND, D_MODEL), F32)])
    mods_all = _modulation_vectors(cond, ada_w, ada_b)
    zero_bias = jnp.zeros((D_MODEL,), F32)
    ffn_w = (ffn_w_gate.astype(BF16), ffn_w_up.astype(BF16), ffn_w_down.astype(BF16))
    rope_tab = None
    ckv_states, krope_states = [], []

    for i in range(DEPTH):
        kind, j = i % N_MIXERS, i // N_MIXERS
        mods = mods_all[i]
        if kind == 0:
            if rope_tab is None:
                rope_tab = _rope_table()
            w_kr2 = jnp.concatenate([mla_w_kr[j], _pair_rotated(mla_w_kr[j])], axis=-1).astype(BF16)
            wq = mla_w_uq[j].reshape(Q_RANK, MLA_HEADS, QK_NOPE + QK_ROPE)
            w_q = jnp.concatenate([wq, _pair_rotated(wq[..., QK_NOPE:])], axis=-1)
            w_q = w_q.reshape(Q_RANK, MLA_HEADS * HEAD_W).astype(BF16)
            w_ukv = mla_w_ukv[j].reshape(KV_RANK, MLA_HEADS, QK_NOPE + V_DIM)
            w_k = w_ukv[..., :QK_NOPE].reshape(KV_RANK, MLA_HEADS * QK_NOPE).astype(BF16)
            w_vt = w_ukv[..., QK_NOPE:].reshape(KV_RANK, MLA_HEADS * V_DIM).T.astype(BF16)
            cq, ckv, kr, kr2 = _mla_down(h, mods, mla_w_dq[j].astype(BF16), mla_w_dkv[j].astype(BF16),
                                         w_kr2, mla_q_norm[j], mla_kv_norm[j], rope_tab)
            ckv_states.append(ckv[:ROWS_CTX].reshape(BATCH, SEQ, KV_RANK))
            krope_states.append(kr[:ROWS_CTX].reshape(BATCH, SEQ, QK_ROPE))
            q = _q_up(cq, w_q, rope_tab)
            k_tok, vt_tok = _kv_expand(ckv, kr2, w_k, w_vt)
            kc = cache_krope[:, j].reshape(DEC_BATCH * PAST_LEN, QK_ROPE).astype(BF16)
            k_cache, vt_cache = _kv_expand(cache_ckv[:, j].reshape(DEC_BATCH * PAST_LEN, KV_RANK),
                                           jnp.concatenate([kc, kc], axis=-1), w_k, w_vt)
            o = _attention(q, k_tok, vt_tok, k_cache, vt_cache)
            h = _mm_postnorm(o, mla_w_o[j].astype(BF16), zero_bias, h, mods, 2, ln_g[i, 0], ln_b[i, 0])
        elif kind == 1:
            pc = _hyena_in(h, mods, hy_w_in[j].astype(BF16), hy_b_in[j], hy_conv_w[j], hy_conv_b[j])
            fp = (hy_f_w1[j], hy_f_b1[j], hy_f_freq1[j], hy_f_w2[j], hy_f_b2[j], hy_f_freq2[j],
                  hy_f_w3[j], hy_log_decay[j])
            skip = hy_skip[j].reshape(HY_ORDER, 1, D_MODEL)
            z_ctx = _hyena_mix_dense(pc, *_hyena_filters(SEQ, *fp), skip, BATCH, SEQ)
            z_lat = _hyena_mix_pair(pc, *_hyena_filters(DEC_SEQ, *fp), skip)
            h = _mm_postnorm((z_ctx, z_lat), hy_w_out[j].astype(BF16), hy_b_out[j],
                             h, mods, 2, ln_g[i, 0], ln_b[i, 0])
        else:
            c_ch, ns_ch = _dft_tables(FNET_CG)
            p, q = _fnet_chan(h, mods, jnp.concatenate([c_ch, -ns_ch], axis=-1))
            f_ctx = _fnet_pos(c_ch, ns_ch, p, q, BATCH, SEQ)
            f_lat = _fnet_pos_factored(p, q)
            h = _mm_postnorm((f_ctx, f_lat), fn_w_out[j].astype(BF16), fn_b_out[j],
                             h, mods, 2, ln_g[i, 0], ln_b[i, 0])
        h = _ffn(h, mods, *ffn_w, i, ln_g[i, 1], ln_b[i, 1], split_out=i == DEPTH - 1)

    y_prompt = h[0].reshape(BATCH, SEQ, D_MODEL)
    y_sample = h[1].reshape(DEC_BATCH, DEC_SEQ, D_MODEL)
    return (y_prompt, y_sample, jnp.stack(ckv_states, axis=1), jnp.stack(krope_states, axis=1))
```

```python
import functools
import math

import jax
import jax.numpy as jnp
from jax import lax
from jax.experimental import pallas as pl
from jax.experimental.pallas import tpu as pltpu

F32 = jnp.float32
BF16 = jnp.bfloat16

D_MODEL = 2048
BATCH = 16
SEQ = 256
DEPTH = 4
DEC_BATCH = 2
DEC_SEQ = 4096
PAST_LEN = 512
GRID_W = 64
N_MIXERS = 3
MLA_HEADS = 16
QK_NOPE = 128
QK_ROPE = 64
V_DIM = 128
Q_RANK = 512
KV_RANK = 512
ROPE_THETA = 10000.0
AXIS_ROPE = QK_ROPE // 2
HY_ORDER = 2
HY_DIRS = 2
HY_CONV = 3
HY_BANDS = 16
HY_EMB = 1 + 2 * HY_BANDS
HY_FW = 64
HY_SHIFT = 0.05
FNET_GROUPS = 8
FNET_CG = D_MODEL // FNET_GROUPS
D_FF = -(-8 * D_MODEL // (3 * 256)) * 256
DN_ALPHA = (2 * DEPTH) ** 0.25
LN_EPS = 1e-5
RMS_EPS = 1e-6
N_MOD = 6

ROWS_CTX = BATCH * SEQ
ROWS_LAT = DEC_BATCH * DEC_SEQ
ROWS = ROWS_CTX + ROWS_LAT
N_COND = 1 + DEC_BATCH
COND_PAD = 8
HEAD_W = QK_NOPE + 2 * QK_ROPE
ATT_SCALE = (QK_NOPE + QK_ROPE) ** -0.5

V7X_VMEM_BYTES = 64 * 2 ** 20
VMEM_CAP_BYTES = V7X_VMEM_BYTES * 7 // 8
VMEM_FLOOR_BYTES = 32 * 2 ** 20


def _vmem_limit(pipelined, resident=0, temps=0):
    est = 2 * pipelined + resident + temps
    return int(min(max(est, VMEM_FLOOR_BYTES), VMEM_CAP_BYTES))


def _params(semantics, vmem):
    return pltpu.CompilerParams(dimension_semantics=semantics, vmem_limit_bytes=vmem)


def _nbytes(shape, dtype):
    return math.prod(shape) * jnp.dtype(dtype).itemsize


def _group_of_tile(i, tm):
    n_ctx = ROWS_CTX // tm
    return jnp.where(i < n_ctx, 0, 1 + (i - n_ctx) // (DEC_SEQ // tm))


def _mod_spec(which, tm):
    return pl.BlockSpec((None, 1, D_MODEL),
                        lambda i, *_: (which * COND_PAD + _group_of_tile(i, tm), 0, 0))


def _row_spec(width=D_MODEL):
    return pl.BlockSpec((1, width), lambda *_: (0, 0))


def _as_tuple(x):
    return x if isinstance(x, tuple) else (x,)


def _token_specs(xs, tm):
    width = xs[0].shape[1]
    if len(xs) == 1:
        return [pl.BlockSpec((tm, width), lambda i, *_: (i, 0))]
    n_ctx = ROWS_CTX // tm
    return [pl.BlockSpec((tm, width), lambda i, *_: (jnp.minimum(i, n_ctx - 1), 0)),
            pl.BlockSpec((tm, width), lambda i, *_: (jnp.maximum(i - n_ctx, 0), 0))]


def _token_rows(refs, r):
    if len(refs) == 1:
        return refs[0][r, :]
    n_ctx = ROWS_CTX // refs[0].shape[0]
    return jnp.where(pl.program_id(0) < n_ctx, refs[0][r, :], refs[1][r, :])


def _modulate(h, shift, scale):
    return h * (1.0 + scale) + shift


def _post_norm(h, delta, g, b):
    z = DN_ALPHA * h + delta
    mu = jnp.mean(z, axis=-1, keepdims=True)
    zc = z - mu
    var = jnp.mean(zc * zc, axis=-1, keepdims=True)
    return zc * lax.rsqrt(var + LN_EPS) * g + b


def _rms_norm(x, g):
    ms = jnp.mean(x * x, axis=-1, keepdims=True)
    return x * lax.rsqrt(ms + RMS_EPS) * g


def _dot(a, b):
    return jnp.dot(a, b, preferred_element_type=F32)


def _modvec_kernel(c_ref, w_ref, b_ref, o_ref):
    a = jax.nn.silu(c_ref[...]).astype(BF16)
    o_ref[...] = _dot(a, w_ref[...].astype(BF16)) + b_ref[...]


def _modulation_vectors(cond, ada_w, ada_b):
    tn = 1024
    n = N_MOD * D_MODEL
    out = pl.pallas_call(
        _modvec_kernel,
        grid=(DEPTH, n // tn),
        in_specs=[pl.BlockSpec((COND_PAD, D_MODEL), lambda l, j: (0, 0)),
                  pl.BlockSpec((None, D_MODEL, tn), lambda l, j: (l, 0, j)),
                  pl.BlockSpec((None, 1, tn), lambda l, j: (l, 0, j))],
        out_specs=pl.BlockSpec((None, COND_PAD, tn), lambda l, j: (l, 0, j)),
        out_shape=jax.ShapeDtypeStruct((DEPTH, COND_PAD, n), F32),
        compiler_params=_params(("parallel", "parallel"),
                                _vmem_limit(_nbytes((D_MODEL, tn), F32),
                                            temps=_nbytes((D_MODEL, tn), BF16))),
        name="modvec",
    )(cond, ada_w, ada_b.reshape(DEPTH, 1, n))
    out = out.reshape(DEPTH, COND_PAD, N_MOD, D_MODEL).transpose(0, 2, 1, 3)
    return out.reshape(DEPTH, N_MOD * COND_PAD, 1, D_MODEL)


def _hyena_in_kernel(n_ctx, h_ref, hp_ref, hn_ref, sh_ref, sc_ref, w_ref, b_ref, cw_ref, cb_ref,
                     o_ref, u_ref):
    i = pl.program_id(0)
    tm = h_ref.shape[0]

    @pl.when(pl.program_id(1) == 0)
    def _():
        mod = lambda x_ref: _modulate(x_ref[...], sh_ref[...], sc_ref[...]).astype(BF16)
        u_ref[:SUB, :] = mod(hp_ref)
        u_ref[SUB:SUB + tm, :] = mod(h_ref)
        u_ref[SUB + tm:, :] = mod(hn_ref)

    length = jnp.where(i < n_ctx, SEQ, DEC_SEQ)
    half = tm // 2
    starts = (0, half)
    ys = [_dot(u_ref[r0:r0 + half + 2 * SUB, :], w_ref[...]) + b_ref[...] for r0 in starts]
    for r0, y in zip(starts, ys):
        inner = slice(SUB, SUB + half)
        before = pltpu.roll(y, 1, 0)[inner]
        after = pltpu.roll(y, y.shape[0] - 1, 0)[inner]
        pos = (lax.broadcasted_iota(jnp.int32, before.shape, 0) + (i * tm + r0)) & (length - 1)
        before = jnp.where(pos == 0, 0.0, before)
        after = jnp.where(pos == length - 1, 0.0, after)
        o_ref[r0:r0 + half, :] = (cw_ref[0:1, :] * before + cw_ref[1:2, :] * y[inner]
                                  + cw_ref[2:3, :] * after + cb_ref[...])


def _hyena_in(h, mods, w, b, conv_w, conv_b, tm=1024, tn=1024):
    assert SEQ & (SEQ - 1) == 0 and DEC_SEQ & (DEC_SEQ - 1) == 0 and tm % SEQ == 0 and DEC_SEQ % tm == 0
    k, n = w.shape
    per, halo, last = D_MODEL // tn, tm // SUB, ROWS // SUB - 1
    return pl.pallas_call(
        functools.partial(_hyena_in_kernel, ROWS_CTX // tm),
        grid=(ROWS // tm, n // tn),
        in_specs=[pl.BlockSpec((tm, k), lambda i, j: (i, 0)),
                  pl.BlockSpec((SUB, k), lambda i, j: (jnp.maximum(i * halo - 1, 0), 0)),
                  pl.BlockSpec((SUB, k), lambda i, j: (jnp.minimum((i + 1) * halo, last), 0)),
                  _mod_spec(0, tm), _mod_spec(1, tm),
                  pl.BlockSpec((k, tn), lambda i, j: (0, j)),
                  pl.BlockSpec((1, tn), lambda i, j: (0, j)),
                  pl.BlockSpec((HY_CONV, tn), lambda i, j: (0, j)),
                  pl.BlockSpec((1, tn), lambda i, j: (0, j))],
        out_specs=pl.BlockSpec((None, tm, tn), lambda i, j: (j // per, i, j % per)),
        out_shape=jax.ShapeDtypeStruct((n // D_MODEL, ROWS, D_MODEL), F32),
        scratch_shapes=[pltpu.VMEM((tm + 2 * SUB, k), BF16)],
        compiler_params=_params(("parallel", "arbitrary"),
                                _vmem_limit(_nbytes((tm, k), F32) + _nbytes((k, tn), BF16)
                                            + _nbytes((tm, tn), F32),
                                            resident=_nbytes((tm, k), BF16),
                                            temps=5 * _nbytes((tm, tn), F32))),
        name="hyena_in",
    )(h, h, h, mods, mods, w, b.reshape(1, n), conv_w, conv_b.reshape(1, n))


def _mm_postnorm_kernel(n_a, n_h, *refs):
    a_refs, refs = refs[:n_a], refs[n_a:]
    (w_ref, bias_ref), refs = refs[:2], refs[2:]
    h_refs, (gate_ref, g_ref, b_ref, o_ref) = refs[:n_h], refs[n_h:]
    half = o_ref.shape[0] // 2
    halves = (slice(0, half), slice(half, 2 * half))
    ys = [_dot(_token_rows(a_refs, r), w_ref[...]) + bias_ref[...] for r in halves]
    for r, y in zip(halves, ys):
        o_ref[r, :] = _post_norm(_token_rows(h_refs, r), gate_ref[...] * y, g_ref[...], b_ref[...])


def _mm_postnorm(a, w, bias, h, mods, which_gate, ln_g, ln_b, tm=512):
    a, h = _as_tuple(a), _as_tuple(h)
    k = a[0].shape[1]
    return pl.pallas_call(
        functools.partial(_mm_postnorm_kernel, len(a), len(h)),
        grid=(ROWS // tm,),
        in_specs=(_token_specs(a, tm) + [pl.BlockSpec((k, D_MODEL), lambda i: (0, 0)), _row_spec()]
                  + _token_specs(h, tm) + [_mod_spec(which_gate, tm), _row_spec(), _row_spec()]),
        out_specs=pl.BlockSpec((tm, D_MODEL), lambda i: (i, 0)),
        out_shape=jax.ShapeDtypeStruct((ROWS, D_MODEL), F32),
        compiler_params=_params(("parallel",),
                                _vmem_limit(len(a) * _nbytes((tm, k), BF16)
                                            + _nbytes((k, D_MODEL), BF16)
                                            + (1 + len(h)) * _nbytes((tm, D_MODEL), F32),
                                            temps=3 * _nbytes((tm, D_MODEL), F32))),
        name="mm_postnorm",
    )(*a, w, bias.reshape(1, D_MODEL), *h, mods, ln_g.reshape(1, D_MODEL), ln_b.reshape(1, D_MODEL))


def _ffn_kernel(n_ctx, h_ref, sh_ref, sc_ref, gate_ref, g_ref, b_ref, wg_ref, wu_ref, wd_ref,
                *refs):
    i, f = pl.program_id(0), pl.program_id(1)
    if n_ctx is None:
        o_ref, u_ref = refs
        acc_ref, outs = o_ref, ((o_ref, None),)
    else:
        octx_ref, olat_ref, u_ref, acc_ref = refs
        outs = ((octx_ref, i < n_ctx), (olat_ref, i >= n_ctx))

    @pl.when(f == 0)
    def _():
        u_ref[...] = _modulate(h_ref[...], sh_ref[...], sc_ref[...]).astype(BF16)
        acc_ref[...] = jnp.zeros_like(acc_ref)

    half = u_ref.shape[0] // 2
    halves = (slice(0, half), slice(half, 2 * half))
    proj = [(_dot(u_ref[r, :], wg_ref[...]), _dot(u_ref[r, :], wu_ref[...])) for r in halves]
    for r, (gate, up) in zip(halves, proj):
        act = (jax.nn.silu(gate) * up).astype(BF16)
        acc_ref[r, :] += _dot(act, wd_ref[...])

    last = f == pl.num_programs(1) - 1
    for o_ref, mine in outs:
        @pl.when(last if mine is None else jnp.logical_and(last, mine))
        def _(o_ref=o_ref):
            for r in halves:
                o_ref[r, :] = _post_norm(h_ref[r, :], gate_ref[...] * acc_ref[r, :],
                                         g_ref[...], b_ref[...])


def _ffn(h, mods, w_gate, w_up, w_down, layer, ln_g, ln_b, split_out, tm=512, tf=512):
    tile = _nbytes((tm, D_MODEL), F32)
    if split_out:
        n_ctx = ROWS_CTX // tm
        out_specs = [pl.BlockSpec((tm, D_MODEL), lambda i, f: (jnp.minimum(i, n_ctx - 1), 0)),
                     pl.BlockSpec((tm, D_MODEL), lambda i, f: (jnp.maximum(i - n_ctx, 0), 0))]
        out_shape = [jax.ShapeDtypeStruct((ROWS_CTX, D_MODEL), F32),
                     jax.ShapeDtypeStruct((ROWS_LAT, D_MODEL), F32)]
        scratch = [pltpu.VMEM((tm, D_MODEL), BF16), pltpu.VMEM((tm, D_MODEL), F32)]
        pipelined, resident = 3 * tile, tile + tile // 2
    else:
        n_ctx = None
        out_specs = pl.BlockSpec((tm, D_MODEL), lambda i, f: (i, 0))
        out_shape = jax.ShapeDtypeStruct((ROWS, D_MODEL), F32)
        scratch = [pltpu.VMEM((tm, D_MODEL), BF16)]
        pipelined, resident = 2 * tile, tile // 2
    return pl.pallas_call(
        functools.partial(_ffn_kernel, n_ctx),
        grid=(ROWS // tm, D_FF // tf),
        in_specs=[pl.BlockSpec((tm, D_MODEL), lambda i, f: (i, 0)),
                  _mod_spec(3, tm), _mod_spec(4, tm), _mod_spec(5, tm),
                  _row_spec(), _row_spec(),
                  pl.BlockSpec((None, D_MODEL, tf), lambda i, f: (layer, 0, f)),
                  pl.BlockSpec((None, D_MODEL, tf), lambda i, f: (layer, 0, f)),
                  pl.BlockSpec((None, tf, D_MODEL), lambda i, f: (layer, f, 0))],
        out_specs=out_specs,
        out_shape=out_shape,
        scratch_shapes=scratch,
        compiler_params=_params(("parallel", "arbitrary"),
                                _vmem_limit(pipelined + 3 * _nbytes((D_MODEL, tf), BF16),
                                            resident=resident, temps=2 * tile)),
        name="ffn",
    )(h, mods, mods, mods, ln_g.reshape(1, D_MODEL), ln_b.reshape(1, D_MODEL),
      w_gate, w_up, w_down)


def _mla_down_kernel(n_h, *refs):
    h_refs, (sh_ref, sc_ref, wdq_ref, wdkv_ref, wkr_ref, qn_ref, kvn_ref, rope_ref,
             cq_ref, ckv_ref, kr_ref, kr2_ref) = refs[:n_h], refs[n_h:]
    u = _modulate(_token_rows(h_refs, slice(None)), sh_ref[...], sc_ref[...]).astype(BF16)
    cq_ref[...] = _rms_norm(_dot(u, wdq_ref[...]), qn_ref[...]).astype(BF16)
    ckv_ref[...] = _rms_norm(_dot(u, wdkv_ref[...]), kvn_ref[...])
    t = _dot(u, wkr_ref[...])
    kr_ref[...] = t[:, :QK_ROPE]
    v = t * rope_ref[...]
    kr2_ref[...] = (v + pltpu.roll(v, QK_ROPE, 1)).astype(BF16)


def _mla_down(h, mods, w_dq, w_dkv, w_kr2, q_norm, kv_norm, rope_tab, tm=512):
    h = _as_tuple(h)
    row = lambda width: pl.BlockSpec((tm, width), lambda i: (i, 0))
    full = lambda shape: pl.BlockSpec(shape, lambda i: (0, 0))
    return pl.pallas_call(
        functools.partial(_mla_down_kernel, len(h)),
        grid=(ROWS // tm,),
        in_specs=_token_specs(h, tm) + [
            _mod_spec(0, tm), _mod_spec(1, tm),
            full((D_MODEL, Q_RANK)), full((D_MODEL, KV_RANK)), full((D_MODEL, 2 * QK_ROPE)),
            _row_spec(Q_RANK), _row_spec(KV_RANK), row(2 * QK_ROPE)],
        out_specs=[row(Q_RANK), row(KV_RANK), row(QK_ROPE), row(2 * QK_ROPE)],
        out_shape=[jax.ShapeDtypeStruct((ROWS, Q_RANK), BF16),
                   jax.ShapeDtypeStruct((ROWS, KV_RANK), F32),
                   jax.ShapeDtypeStruct((ROWS, QK_ROPE), F32),
                   jax.ShapeDtypeStruct((ROWS, 2 * QK_ROPE), BF16)],
        compiler_params=_params(("parallel",),
                                _vmem_limit(len(h) * _nbytes((tm, D_MODEL), F32)
                                            + _nbytes((D_MODEL, Q_RANK + KV_RANK + 2 * QK_ROPE), BF16)
                                            + 3 * _nbytes((tm, KV_RANK), F32),
                                            temps=2 * _nbytes((tm, D_MODEL), F32))),
        name="mla_down",
    )(*h, mods, mods, w_dq, w_dkv, w_kr2, q_norm.reshape(1, Q_RANK), kv_norm.reshape(1, KV_RANK),
      rope_tab)


NT_DIMS = (((1,), (1,)), ((), ()))


def _q_up_kernel(cq_ref, w_ref, rope_ref, q_ref):
    cq, tab = cq_ref[...], rope_ref[...] * LOG2E_SCALE
    for h in range(MLA_HEADS):
        r = _dot(cq, w_ref[:, h * HEAD_W:(h + 1) * HEAD_W])
        q_ref[h, :, :QK_NOPE] = (r[:, :QK_NOPE] * LOG2E_SCALE).astype(BF16)
        q_ref[h, :, QK_NOPE:] = (r[:, QK_NOPE:] * tab).astype(BF16)


def _q_up(cq, w_q, rope_tab, tm=512):
    return pl.pallas_call(
        _q_up_kernel,
        grid=(ROWS // tm,),
        in_specs=[pl.BlockSpec((tm, Q_RANK), lambda i: (i, 0)),
                  pl.BlockSpec((Q_RANK, MLA_HEADS * HEAD_W), lambda i: (0, 0)),
                  pl.BlockSpec((tm, 2 * QK_ROPE), lambda i: (i, 0))],
        out_specs=pl.BlockSpec((MLA_HEADS, tm, HEAD_W), lambda i: (0, i, 0)),
        out_shape=jax.ShapeDtypeStruct((MLA_HEADS, ROWS, HEAD_W), BF16),
        compiler_params=_params(("parallel",),
                                _vmem_limit(_nbytes((Q_RANK + tm, MLA_HEADS * HEAD_W), BF16))),
        name="q_up",
    )(cq, w_q, rope_tab)


VT_PAD = 16
VT_ROWS = V_DIM + VT_PAD


def _kv_expand_kernel(ckv_ref, kr2_ref, wk_ref, wvt_ref, k_ref, vt_ref):
    c = ckv_ref[...].astype(BF16)
    vt = lax.dot_general(wvt_ref[...], c, NT_DIMS, preferred_element_type=F32).astype(BF16)
    ones = jnp.ones((VT_PAD, vt.shape[1]), BF16)
    for h in range(MLA_HEADS):
        vt_ref[h * VT_ROWS:h * VT_ROWS + V_DIM, :] = vt[h * V_DIM:(h + 1) * V_DIM]
        vt_ref[h * VT_ROWS + V_DIM:(h + 1) * VT_ROWS, :] = ones
    kr2 = kr2_ref[...]
    pair_w = 2 * QK_NOPE
    for g in range(MLA_HEADS // 2):
        r = _dot(c, wk_ref[:, g * pair_w:(g + 1) * pair_w]).astype(BF16)
        for e in range(2):
            k_ref[2 * g + e, :, :QK_NOPE] = r[:, e * QK_NOPE:(e + 1) * QK_NOPE]
            k_ref[2 * g + e, :, QK_NOPE:] = kr2


def _kv_expand(ckv, kr2, w_k, w_vt, tm=512):
    rows = ckv.shape[0]
    return pl.pallas_call(
        _kv_expand_kernel,
        grid=(rows // tm,),
        in_specs=[pl.BlockSpec((tm, KV_RANK), lambda i: (i, 0)),
                  pl.BlockSpec((tm, 2 * QK_ROPE), lambda i: (i, 0)),
                  pl.BlockSpec((KV_RANK, MLA_HEADS * QK_NOPE), lambda i: (0, 0)),
                  pl.BlockSpec((MLA_HEADS * V_DIM, KV_RANK), lambda i: (0, 0))],
        out_specs=[pl.BlockSpec((MLA_HEADS, tm, HEAD_W), lambda i: (0, i, 0)),
                   pl.BlockSpec((MLA_HEADS * VT_ROWS, tm), lambda i: (0, i))],
        out_shape=[jax.ShapeDtypeStruct((MLA_HEADS, rows, HEAD_W), BF16),
                   jax.ShapeDtypeStruct((MLA_HEADS * VT_ROWS, rows), BF16)],
        compiler_params=_params(("parallel",),
                                _vmem_limit(_nbytes((tm, MLA_HEADS * (HEAD_W + V_DIM)), BF16)
                                            + 2 * _nbytes((KV_RANK, MLA_HEADS * V_DIM), BF16),
                                            temps=_nbytes((MLA_HEADS * V_DIM, tm), F32))),
        name="kv_expand",
    )(ckv, kr2, w_k, w_vt)


ATT_CHUNK = 512
ATT_SKEW = 3
LOG2E_SCALE = ATT_SCALE * math.log2(math.e)


def _attn_scores(q, k):
    return lax.dot_general(k, q, NT_DIMS, preferred_element_type=F32)


def _attn_values(s, vt, carry):
    m = jnp.max(s, axis=0, keepdims=True)
    if carry is not None:
        m_old, acc_old = carry
        m = jnp.maximum(m_old, m)
    acc = _dot(vt, jnp.exp2(s - m).astype(BF16))
    if carry is not None:
        acc = jnp.exp2(m_old - m) * acc_old + acc
    return m, acc


def _attn_output(acc):
    return (acc[:V_DIM] / acc[V_DIM:V_DIM + 1]).T.astype(BF16)


def _attn_ctx_kernel(q_ref, k_ref, vt_ref, o_ref):
    for h in range(MLA_HEADS):
        _, acc = _attn_values(_attn_scores(q_ref[h], k_ref[h]),
                              vt_ref[h * VT_ROWS:(h + 1) * VT_ROWS, :], None)
        o_ref[:, h * V_DIM:(h + 1) * V_DIM] = _attn_output(acc)


def _attn_lat_kernel(q_ref, k_ref, vt_ref, kc_ref, vtc_ref, prev_ref, o_ref):
    del prev_ref
    q = q_ref[...]
    n_tok = DEC_SEQ // ATT_CHUNK
    rows = lambda c: slice(c * ATT_CHUNK, (c + 1) * ATT_CHUNK)
    keys = [k_ref.at[rows(c), :] for c in range(n_tok)] + [kc_ref]
    vals = [vt_ref.at[:, rows(c)] for c in range(n_tok)] + [vtc_ref]
    carry = None
    scores = [_attn_scores(q, keys[c][...]) for c in range(ATT_SKEW)]
    for c in range(n_tok + 1):
        if c + ATT_SKEW <= n_tok:
            scores.append(_attn_scores(q, keys[c + ATT_SKEW][...]))
        carry = _attn_values(scores[c], vals[c][...], carry)
    o_ref[...] = _attn_output(carry[1])


def _attention(q, k_tok, vt_tok, k_cache, vt_cache, tq=1024):
    assert PAST_LEN == ATT_CHUNK
    out_shape = jax.ShapeDtypeStruct((ROWS, MLA_HEADS * V_DIM), BF16)
    o = pl.pallas_call(
        _attn_ctx_kernel,
        grid=(BATCH,),
        in_specs=[pl.BlockSpec((MLA_HEADS, SEQ, HEAD_W), lambda s: (0, s, 0)),
                  pl.BlockSpec((MLA_HEADS, SEQ, HEAD_W), lambda s: (0, s, 0)),
                  pl.BlockSpec((MLA_HEADS * VT_ROWS, SEQ), lambda s: (0, s))],
        out_specs=pl.BlockSpec((SEQ, MLA_HEADS * V_DIM), lambda s: (s, 0)),
        out_shape=out_shape,
        compiler_params=_params(("parallel",), VMEM_FLOOR_BYTES),
        name="attn_ctx",
    )(q, k_tok, vt_tok)

    lat0 = ROWS_CTX // DEC_SEQ
    q0 = ROWS_CTX // tq
    nq = DEC_SEQ // tq
    return pl.pallas_call(
        _attn_lat_kernel,
        grid=(DEC_BATCH, MLA_HEADS, nq),
        in_specs=[pl.BlockSpec((None, tq, HEAD_W), lambda b, h, i: (h, q0 + b * nq + i, 0)),
                  pl.BlockSpec((None, DEC_SEQ, HEAD_W), lambda b, h, i: (h, lat0 + b, 0)),
                  pl.BlockSpec((VT_ROWS, DEC_SEQ), lambda b, h, i: (h, lat0 + b)),
                  pl.BlockSpec((None, PAST_LEN, HEAD_W), lambda b, h, i: (h, b, 0)),
                  pl.BlockSpec((VT_ROWS, PAST_LEN), lambda b, h, i: (h, b)),
                  pl.BlockSpec(memory_space=pl.ANY)],
        out_specs=pl.BlockSpec((tq, V_DIM), lambda b, h, i: (q0 + b * nq + i, h)),
        out_shape=out_shape,
        input_output_aliases={5: 0},
        compiler_params=_params(("parallel", "parallel", "arbitrary"),
                                _vmem_limit(_nbytes((DEC_SEQ + PAST_LEN, HEAD_W + V_DIM), BF16),
                                            temps=8 * _nbytes((ATT_CHUNK, tq), F32))),
        name="attn_lat",
    )(q, k_tok, vt_tok, k_cache, vt_cache, o)


def _mm_kernel(a_ref, b_ref, o_ref, acc_ref):
    k = pl.program_id(2)

    @pl.when(k == 0)
    def _():
        acc_ref[...] = jnp.zeros_like(acc_ref)

    acc_ref[...] += _dot(a_ref[...], b_ref[...].astype(BF16))

    @pl.when(k == pl.num_programs(2) - 1)
    def _():
        o_ref[...] = acc_ref[...].astype(o_ref.dtype)


def _mm(a, b, out_dtype=F32, tm=1024, tn=1024, tk=512):
    m, kk = a.shape
    n = b.shape[1]
    tm, tn, tk = min(tm, m), min(tn, n), min(tk, kk)
    return pl.pallas_call(
        _mm_kernel,
        grid=(m // tm, n // tn, kk // tk),
        in_specs=[pl.BlockSpec((tm, tk), lambda i, j, k: (i, k)),
                  pl.BlockSpec((tk, tn), lambda i, j, k: (k, j))],
        out_specs=pl.BlockSpec((tm, tn), lambda i, j, k: (i, j)),
        out_shape=jax.ShapeDtypeStruct((m, n), out_dtype),
        scratch_shapes=[pltpu.VMEM((tm, tn), F32)],
        compiler_params=_params(("parallel", "parallel", "arbitrary"), VMEM_FLOOR_BYTES),
        name="mm",
    )(a, b)


def _seq_tiles(length):
    if length >= 1024:
        return 1024, 512, 1024
    return length, D_MODEL, length


def _dft_fwd_kernel(c_ref, s_ref, z_ref, kr_ref, ks_ref, yr_ref, ys_ref, accr_ref, accs_ref):
    k = pl.program_id(3)

    @pl.when(k == 0)
    def _():
        accr_ref[...] = jnp.zeros_like(accr_ref)
        accs_ref[...] = jnp.zeros_like(accs_ref)

    z = z_ref[...].astype(BF16)
    accr_ref[...] += _dot(c_ref[...], z)
    accs_ref[...] += _dot(s_ref[...], z)

    @pl.when(k == pl.num_programs(3) - 1)
    def _():
        zr, zs, kr, ks = accr_ref[...], accs_ref[...], kr_ref[...], ks_ref[...]
        yr_ref[...] = (zr * kr - zs * ks).astype(BF16)
        ys_ref[...] = (zr * ks + zs * kr).astype(BF16)


def _dft_fwd(tabs, z, z_which, kr, ks, order, n_seq, length):
    tf, tn, tk = _seq_tiles(length)
    nb, nf, nk = D_MODEL // tn, length // tf, length // tk
    kc0 = order * nb
    out = jax.ShapeDtypeStruct((n_seq * length, D_MODEL), BF16)
    return pl.pallas_call(
        _dft_fwd_kernel,
        grid=(n_seq, nf, nb, nk),
        in_specs=[pl.BlockSpec((tf, tk), lambda s, f, n, k: (f, k)),
                  pl.BlockSpec((tf, tk), lambda s, f, n, k: (f, k)),
                  pl.BlockSpec((None, tk, tn), lambda s, f, n, k: (z_which, s * nk + k, n)),
                  pl.BlockSpec((tf, tn), lambda s, f, n, k: (f, kc0 + n)),
                  pl.BlockSpec((tf, tn), lambda s, f, n, k: (f, kc0 + n))],
        out_specs=[pl.BlockSpec((tf, tn), lambda s, f, n, k: (s * nf + f, n)),
                   pl.BlockSpec((tf, tn), lambda s, f, n, k: (s * nf + f, n))],
        out_shape=[out, out],
        scratch_shapes=[pltpu.VMEM((tf, tn), F32), pltpu.VMEM((tf, tn), F32)],
        compiler_params=_params(("parallel", "parallel", "parallel", "arbitrary"),
                                _vmem_limit(2 * _nbytes((tf, tk), BF16) + _nbytes((tk, tn), F32)
                                            + 2 * _nbytes((tf, tn), F32) + 2 * _nbytes((tf, tn), BF16),
                                            resident=2 * _nbytes((tf, tn), F32),
                                            temps=4 * _nbytes((tf, tn), F32))),
        name="dft_fwd",
    )(tabs["c"], tabs["s"], z, kr, ks)


def _dft_inv_kernel(inv_len, ct_ref, st_ref, yr_ref, ys_ref, z_ref, gate_ref, skip_ref,
                    o_ref, acc_ref):
    k = pl.program_id(3)

    @pl.when(k == 0)
    def _():
        acc_ref[...] = jnp.zeros_like(acc_ref)

    acc_ref[...] += _dot(ct_ref[...], yr_ref[...]) + _dot(st_ref[...], ys_ref[...])

    @pl.when(k == pl.num_programs(3) - 1)
    def _():
        y = acc_ref[...] * inv_len + skip_ref[...] * z_ref[...]
        o_ref[...] = (gate_ref[...] * y).astype(o_ref.dtype)


def _dft_inv(tabs, yr, ys, z, z_which, gate, gate_which, skip, order, n_seq, length, out_dtype):
    tt, tn, tk = _seq_tiles(length)
    nb, nt, nk = D_MODEL // tn, length // tt, length // tk
    return pl.pallas_call(
        functools.partial(_dft_inv_kernel, 1.0 / length),
        grid=(n_seq, nt, nb, nk),
        in_specs=[pl.BlockSpec((tt, tk), lambda s, t, n, k: (t, k)),
                  pl.BlockSpec((tt, tk), lambda s, t, n, k: (t, k)),
                  pl.BlockSpec((tk, tn), lambda s, t, n, k: (s * nk + k, n)),
                  pl.BlockSpec((tk, tn), lambda s, t, n, k: (s * nk + k, n)),
                  pl.BlockSpec((None, tt, tn), lambda s, t, n, k: (z_which, s * nt + t, n)),
                  pl.BlockSpec((None, tt, tn), lambda s, t, n, k: (gate_which, s * nt + t, n)),
                  pl.BlockSpec((None, 1, tn), lambda s, t, n, k: (order, 0, n))],
        out_specs=pl.BlockSpec((tt, tn), lambda s, t, n, k: (s * nt + t, n)),
        out_shape=jax.ShapeDtypeStruct((n_seq * length, D_MODEL), out_dtype),
        scratch_shapes=[pltpu.VMEM((tt, tn), F32)],
        compiler_params=_params(("parallel", "parallel", "parallel", "arbitrary"),
                                _vmem_limit(2 * _nbytes((tt, tk), BF16) + 2 * _nbytes((tk, tn), BF16)
                                            + 3 * _nbytes((tt, tn), F32),
                                            resident=_nbytes((tt, tn), F32),
                                            temps=3 * _nbytes((tt, tn), F32))),
        name="dft_inv",
    )(tabs["ct"], tabs["st"], yr, ys, z, gate, skip)


TW_LANES = 128
SUB = 8
CT_ROWS = 2 * SUB
CT_COLS = 1024


def _lane_tile(x, width):
    return jnp.tile(x, (1, width // x.shape[-1]))


def _kron_sub(f):
    return jnp.kron(f, jnp.eye(SUB, dtype=f.dtype))


def _sub_rows(x, h):
    part = x[:, h * SUB:(h + 1) * SUB, :]
    return part.reshape(part.shape[0] * SUB, part.shape[2])


def _from_sub_rows(parts):
    split = [p.reshape(p.shape[0] // SUB, SUB, p.shape[1]) for p in parts]
    return jnp.concatenate(split, axis=1)


def _ct_stage_a_kernel(n_in, *refs):
    x_refs, (fa_ref, cw_ref, sw_ref, br_ref, bi_ref) = refs[:n_in], refs[n_in:]
    fa = fa_ref[...]
    half = fa.shape[0] // 2
    width = br_ref.shape[-1]
    xs = [r[...] for r in x_refs]
    b_re, b_im = [], []
    for h in range(CT_ROWS // SUB):
        x = jnp.concatenate([_sub_rows(x, h) for x in xs], axis=0).astype(BF16)
        a = _dot(fa, x)
        ar, ai = a[:half], a[half:]
        cw, sw = _lane_tile(cw_ref[h], width), _lane_tile(sw_ref[h], width)
        b_re.append(ar * cw + ai * sw)
        b_im.append(ai * cw - ar * sw)
    br_ref[...] = _from_sub_rows(b_re).astype(BF16)
    bi_ref[...] = _from_sub_rows(b_im).astype(BF16)


def _ct_stage_a(xs, n_seq, fa, cw, sw):
    n1 = fa.shape[0] // (2 * SUB)
    n2 = cw.shape[0] * CT_ROWS
    width = xs[0][0].shape[-1]
    tn2, tw = CT_ROWS, CT_COLS
    in_specs, blocks = [], 0
    for arr, prefix in xs:
        rows_in = arr.shape[-3]
        lead = (None,) * (arr.ndim - 3)
        in_specs.append(pl.BlockSpec(lead + (rows_in, tn2, tw),
                                     lambda s, i, c, prefix=prefix: prefix(s) + (0, i, c)))
        blocks += _nbytes((rows_in, tn2, tw), arr.dtype)
    twid = pl.BlockSpec((None,) + cw.shape[1:], lambda s, i, c: (i, 0, 0, 0))
    in_specs += [pl.BlockSpec(fa.shape, lambda s, i, c: (0, 0)), twid, twid]
    out = jax.ShapeDtypeStruct((n_seq, n1, n2, width), BF16)
    out_spec = pl.BlockSpec((None, n1, tn2, tw), lambda s, i, c: (s, 0, i, c))
    return pl.pallas_call(
        functools.partial(_ct_stage_a_kernel, len(xs)),
        grid=(n_seq, n2 // tn2, width // tw),
        in_specs=in_specs,
        out_specs=[out_spec, out_spec],
        out_shape=[out, out],
        compiler_params=_params(("parallel", "parallel", "parallel"),
                                _vmem_limit(blocks + 2 * _nbytes((n1, tn2, tw), BF16)
                                            + _nbytes(fa.shape, BF16),
                                            temps=8 * _nbytes((n1, tn2, tw), F32))),
        name="ct_stage_a",
    )(*[arr for arr, _ in xs], fa, cw, sw)


def _ct_mid_kernel(br_ref, bi_ref, pr_ref, pi_ref, mr_ref, mi_ref, fb_ref, fbi_ref, cw_ref, sw_ref,
                   vr_ref, vi_ref):
    fb, fbi = fb_ref[...], fbi_ref[...]
    half = fb.shape[0] // 2
    width = br_ref.shape[-1]
    for j in range(br_ref.shape[0]):
        stack = lambda re_ref, im_ref: jnp.concatenate([re_ref[j], im_ref[j]], axis=0)
        kr = _dot(fb[:half], stack(pr_ref, pi_ref))
        ki = _dot(fb[half:], stack(mr_ref, mi_ref))
        x = _dot(fb, stack(br_ref, bi_ref))
        xr, xi = x[:half], x[half:]
        y = jnp.concatenate([xr * kr - xi * ki, xr * ki + xi * kr], axis=0).astype(BF16)
        v = _dot(fbi, y)
        vr, vi = v[:half], v[half:]
        cw, sw = _lane_tile(cw_ref[j], width), _lane_tile(sw_ref[j], width)
        vr_ref[j] = (vr * cw - vi * sw).astype(BF16)
        vi_ref[j] = (vi * cw + vr * sw).astype(BF16)


def _ct_mid(br, bi, filt, order, fb, fbi, cw, sw, tk1=4, td=1024):
    n1, n2, d = br.shape
    nd = d // td
    data = pl.BlockSpec((tk1, n2, td), lambda i, j: (i, 0, j))
    coef = pl.BlockSpec((tk1, n2, td), lambda i, j: (i, 0, order * nd + j))
    mat = pl.BlockSpec(fb.shape, lambda i, j: (0, 0))
    tw = pl.BlockSpec((tk1, n2, TW_LANES), lambda i, j: (i, 0, 0))
    out = jax.ShapeDtypeStruct((n1, n2, d), BF16)
    return pl.pallas_call(
        _ct_mid_kernel,
        grid=(n1 // tk1, nd),
        in_specs=[data, data, coef, coef, coef, coef, mat, mat, tw, tw],
        out_specs=[data, data],
        out_shape=[out, out],
        compiler_params=_params(("parallel", "parallel"),
                                _vmem_limit(8 * _nbytes((tk1, n2, td), BF16),
                                            temps=10 * _nbytes((2 * n2, td), F32))),
        name="ct_mid",
    )(br, bi, *filt, fb, fbi, cw, sw)


def _ct_inv_a_kernel(scale, vr_ref, vi_ref, fai_ref, z0_ref, z1_ref, g0_ref, g1_ref, skip_ref,
                     o_ref):
    fai = fai_ref[...]
    half = fai.shape[0] // 2
    skip = skip_ref[...]
    vr, vi = vr_ref[...].astype(F32), vi_ref[...].astype(F32)
    zs, gs = (z0_ref[...], z1_ref[...]), (g0_ref[...], g1_ref[...])
    outs = ([], [])
    for h in range(CT_ROWS // SUB):
        v = jnp.concatenate([_sub_rows(vr, h), _sub_rows(vi, h)], axis=0).astype(BF16)
        y = _dot(fai, v) * scale
        for b, yb in enumerate((y[:half], y[half:])):
            outs[b].append(_sub_rows(gs[b], h) * (yb + skip * _sub_rows(zs[b], h)))
    for b in range(2):
        o_ref[b] = _from_sub_rows(outs[b]).astype(o_ref.dtype)


def _ct_inv_a(vr, vi, fai, z, z_which, gate, gate_which, skip, order, out_dtype):
    n1, n2, d = vr.shape
    rows = fai.shape[0] // (2 * SUB)
    tn2, tw = CT_ROWS, CT_COLS // 2
    spec = pl.BlockSpec((n1, tn2, tw), lambda i, c: (0, i, c))
    pair = lambda which, b: pl.BlockSpec((None, None, rows, tn2, tw),
                                         lambda i, c: (which[0], which[1] + b, 0, i, c))
    return pl.pallas_call(
        functools.partial(_ct_inv_a_kernel, 1.0 / (n1 * n2)),
        grid=(n2 // tn2, d // tw),
        in_specs=[spec, spec, pl.BlockSpec(fai.shape, lambda i, c: (0, 0)),
                  pair(z_which, 0), pair(z_which, 1), pair(gate_which, 0), pair(gate_which, 1),
                  pl.BlockSpec((None, 1, tw), lambda i, c: (order, 0, c))],
        out_specs=pl.BlockSpec((2, rows, tn2, tw), lambda i, c: (0, 0, i, c)),
        out_shape=jax.ShapeDtypeStruct((2, rows, n2, d), out_dtype),
        compiler_params=_params(("parallel", "parallel"),
                                _vmem_limit(2 * _nbytes((n1, tn2, tw), BF16)
                                            + 6 * _nbytes((rows, tn2, tw), F32)
                                            + _nbytes(fai.shape, BF16),
                                            temps=8 * _nbytes((n1, tn2, tw), F32))),
        name="ct_inv_a",
    )(vr, vi, fai, z, z, gate, gate, skip)


def _ct_real_b_kernel(scale, br_ref, bi_ref, fb_ref, o_ref, so_ref):
    fb = fb_ref[...]
    for j in range(br_ref.shape[0]):
        so_ref[:, j, :] = _dot(fb, jnp.concatenate([br_ref[j], bi_ref[j]], axis=0)) * scale
    o_ref[...] = so_ref[...].astype(o_ref.dtype)


def _ct_real_b(br, bi, fb_re, scale):
    n_seq, n1, n2, d = br.shape
    tk1, tw = CT_ROWS, CT_COLS
    blk = pl.BlockSpec((None, tk1, n2, tw), lambda s, i, c: (s, i, 0, c))
    return pl.pallas_call(
        functools.partial(_ct_real_b_kernel, scale),
        grid=(n_seq, n1 // tk1, d // tw),
        in_specs=[blk, blk, pl.BlockSpec(fb_re.shape, lambda s, i, c: (0, 0))],
        out_specs=pl.BlockSpec((None, n2, tk1, tw), lambda s, i, c: (s, 0, i, c)),
        out_shape=jax.ShapeDtypeStruct((n_seq, n2, n1, d), BF16),
        scratch_shapes=[pltpu.VMEM((n2, tk1, tw), F32)],
        compiler_params=_params(("parallel", "parallel", "parallel"),
                                _vmem_limit(3 * _nbytes((tk1, n2, tw), BF16),
                                            resident=_nbytes((n2, tk1, tw), F32),
                                            temps=2 * _nbytes((n2, tk1, tw), F32))),
        name="ct_real_b",
    )(br, bi, fb_re)


def _cos_sin(num, den):
    ang = (num % den).astype(F32) * (2.0 * math.pi / den)
    return jnp.cos(ang), jnp.sin(ang)


def _ct_tables(n1, n2):
    i1 = jnp.arange(n1, dtype=jnp.int32)
    i2 = jnp.arange(n2, dtype=jnp.int32)
    c1, s1 = _cos_sin(i1[:, None] * i1[None, :], n1)
    c2, s2 = _cos_sin(i2[:, None] * i2[None, :], n2)
    cw, sw = _cos_sin(i2[:, None] * i1[None, :], n1 * n2)
    lanes = lambda t: jnp.broadcast_to(t[..., None], t.shape + (TW_LANES,))

    def stage_a_rows(t):
        t = t.reshape(n2 // CT_ROWS, CT_ROWS // SUB, SUB, n1)
        return lanes(jnp.swapaxes(t, 2, 3).reshape(n2 // CT_ROWS, CT_ROWS // SUB, n1 * SUB))

    return {"c1": c1, "s1": s1, "c2": c2, "s2": s2,
            "cw_a": stage_a_rows(cw), "sw_a": stage_a_rows(sw),
            "cw_b": lanes(cw.T), "sw_b": lanes(sw.T)}


def _filter_mlp_kernel(feat_ref, t_ref, w1_ref, b1_ref, fr1_ref, w2_ref, b2_ref, fr2_ref,
                       w3_ref, decay_ref, h_ref, ss_ref):
    x = jnp.sin(fr1_ref[...] * (_dot(feat_ref[...].astype(BF16), w1_ref[...].astype(BF16))
                                + b1_ref[...]))
    x = jnp.sin(fr2_ref[...] * (_dot(x.astype(BF16), w2_ref[...].astype(BF16)) + b2_ref[...]))
    h = _dot(x.astype(BF16), w3_ref[...].astype(BF16))
    h = h * (jnp.exp(-t_ref[...] * jnp.exp(decay_ref[...])) + HY_SHIFT)
    h_ref[...] = h

    @pl.when(pl.program_id(0) == 0)
    def _():
        ss_ref[...] = jnp.zeros_like(ss_ref)

    ss_ref[...] += jnp.sum(h * h, axis=0, keepdims=True)


def _filter_combine_kernel(hf_ref, hb_ref, ssf_ref, ssb_ref, a_ref, b_ref):
    norm = lax.rsqrt(ssf_ref[...] + ssb_ref[...] + 1e-12)
    fwd = hf_ref[...] * norm
    bwd = hb_ref[...] * norm
    row = lax.broadcasted_iota(jnp.int32, bwd.shape, 0) + pl.program_id(0) * bwd.shape[0]
    bwd = jnp.where(row == 0, 0.0, bwd)
    a_ref[...] = fwd + bwd
    b_ref[...] = fwd - bwd


def _hyena_filters(length, f_w1, f_b1, f_freq1, f_w2, f_b2, f_freq2, f_w3, log_decay):
    t = jnp.linspace(0.0, 1.0, length, dtype=F32)[:, None]
    t_idx = jnp.arange(length, dtype=F32)[:, None]
    bands = jnp.linspace(1e-4, HY_BANDS - 1, HY_BANDS, dtype=F32)
    w = 2.0 * math.pi * t_idx * bands / length
    feat = jnp.concatenate([t, jnp.cos(w), -jnp.sin(w)], axis=-1)
    emb_pad = 128
    feat = jnp.pad(feat, ((0, 0), (0, emb_pad - HY_EMB)))
    w1 = jnp.pad(f_w1, ((0, emb_pad - HY_EMB), (0, 0)))
    n_all = HY_DIRS * HY_ORDER * D_MODEL
    n_dir = HY_ORDER * D_MODEL
    tm = 256
    full = lambda shape: pl.BlockSpec(shape, lambda i: (0, 0))
    h, ss = pl.pallas_call(
        _filter_mlp_kernel,
        grid=(length // tm,),
        in_specs=[pl.BlockSpec((tm, emb_pad), lambda i: (i, 0)),
                  pl.BlockSpec((tm, 1), lambda i: (i, 0)),
                  full((emb_pad, HY_FW)), full((1, HY_FW)), full((1, HY_FW)),
                  full((HY_FW, HY_FW)), full((1, HY_FW)), full((1, HY_FW)),
                  full((HY_FW, n_all)), full((1, n_all))],
        out_specs=[pl.BlockSpec((tm, n_all), lambda i: (i, 0)), full((1, n_all))],
        out_shape=[jax.ShapeDtypeStruct((length, n_all), F32),
                   jax.ShapeDtypeStruct((1, n_all), F32)],
        compiler_params=_params(("arbitrary",),
                                _vmem_limit(_nbytes((tm, n_all), F32) + _nbytes((HY_FW, n_all), F32),
                                            temps=3 * _nbytes((tm, n_all), F32))),
        name="filter_mlp",
    )(feat, t, w1, f_b1.reshape(1, HY_FW), f_freq1.reshape(1, HY_FW), f_w2,
      f_b2.reshape(1, HY_FW), f_freq2.reshape(1, HY_FW), f_w3,
      log_decay.reshape(1, n_all))

    tn = 1024
    nb = n_dir // tn
    comb = jax.ShapeDtypeStruct((length, n_dir), F32)
    a, b = pl.pallas_call(
        _filter_combine_kernel,
        grid=(length // tm, nb),
        in_specs=[pl.BlockSpec((tm, tn), lambda i, j: (i, j)),
                  pl.BlockSpec((tm, tn), lambda i, j: (i, nb + j)),
                  pl.BlockSpec((1, tn), lambda i, j: (0, j)),
                  pl.BlockSpec((1, tn), lambda i, j: (0, nb + j))],
        out_specs=[pl.BlockSpec((tm, tn), lambda i, j: (i, j)),
                   pl.BlockSpec((tm, tn), lambda i, j: (i, j))],
        out_shape=[comb, comb],
        compiler_params=_params(("parallel", "parallel"), VMEM_FLOOR_BYTES),
        name="filter_combine",
    )(h, h, ss, ss)
    return a, b


def _cis_product(row_hi, row_lo, period):
    def cis(phase):
        ang = (phase % period).astype(F32) * (2.0 * math.pi / period)
        return jnp.cos(ang)[:, :, None], jnp.sin(ang)[:, :, None]
    (c1, s1), (c0, s0) = cis(row_hi), cis(row_lo)
    c0, s0 = jnp.swapaxes(c0, 1, 2), jnp.swapaxes(s0, 1, 2)
    rows = row_hi.shape[0]
    return ((c1 * c0 - s1 * s0).reshape(rows, -1), (s1 * c0 + c1 * s0).reshape(rows, -1))


def _odd_dft_tables(length):
    split = 1 << (length.bit_length() // 2)
    r = jnp.arange(length, dtype=jnp.int32)[:, None]
    hi = jnp.arange(length // split, dtype=jnp.int32)[None, :] * split
    lo = jnp.arange(split, dtype=jnp.int32)[None, :]
    c, s = _cis_product((2 * r + 1) * hi, (2 * r + 1) * lo, 4 * length)
    ct, st = _cis_product(r * (2 * hi), r * (2 * lo + 1), 4 * length)
    return {"c": c.astype(BF16), "s": s.astype(BF16), "ct": ct.astype(BF16), "st": st.astype(BF16)}


def _dft_tables(length):
    split = 1 << (length.bit_length() // 2)
    r = jnp.arange(length, dtype=jnp.int32)[:, None]
    hi = jnp.arange(length // split, dtype=jnp.int32)[None, :] * split
    lo = jnp.arange(split, dtype=jnp.int32)[None, :]
    c, s = _cis_product(r * hi, r * lo, length)
    return c.astype(BF16), (-s).astype(BF16)


def _hyena_mix_dense(pc, filt_p, filt_m, skip, n_seq, length):
    tabs = _odd_dft_tables(length)
    kr, ks = _mm(tabs["c"], filt_p), _mm(tabs["s"], filt_m)
    yr, ys = _dft_fwd(tabs, pc, 0, kr, ks, 0, n_seq, length)
    z1 = _dft_inv(tabs, yr, ys, pc, 0, pc, 1, skip, 0, n_seq, length, F32)[None]
    yr, ys = _dft_fwd(tabs, z1, 0, kr, ks, 1, n_seq, length)
    return _dft_inv(tabs, yr, ys, z1, 0, pc, 2, skip, 1, n_seq, length, BF16)


HY_N1, HY_N2 = 64, 128


def _hyena_mix_pair(pc, filt_p, filt_m, skip):
    assert DEC_BATCH == 2 and HY_N1 * HY_N2 == 2 * DEC_SEQ and ROWS % DEC_SEQ == 0
    n1, n2, rows_in = HY_N1, HY_N2, HY_N1 // 2
    t = _ct_tables(n1, n2)
    c_in, s_in = t["c1"][:, :rows_in], t["s1"][:, :rows_in]
    fa = _kron_sub(jnp.block([[c_in, s_in], [-s_in, c_in]])).astype(BF16)
    fa_real = _kron_sub(jnp.concatenate([c_in, -s_in], axis=0)).astype(BF16)
    fb = jnp.block([[t["c2"], t["s2"]], [-t["s2"], t["c2"]]]).astype(BF16)
    fbi = jnp.block([[t["c2"], -t["s2"]], [t["s2"], t["c2"]]]).astype(BF16)
    c_out, s_out = t["c1"][:rows_in], t["s1"][:rows_in]
    fai = _kron_sub(jnp.block([[c_out, -s_out], [s_out, c_out]])).astype(BF16)
    n_filt = HY_ORDER * D_MODEL
    filt = []
    for f in (filt_p, filt_m):
        re, im = _ct_stage_a([(f.reshape(rows_in, n2, n_filt), lambda s: ())], 1, fa_real,
                             t["cw_a"], t["sw_a"])
        filt += [re[0], im[0]]

    pc = pc.reshape(3, ROWS // DEC_SEQ, rows_in, n2, D_MODEL)
    z, z0, out_dtypes = pc, ROWS_CTX // DEC_SEQ, (F32, BF16)
    for order in range(HY_ORDER):
        br, bi = _ct_stage_a([(z, lambda s, z0=z0: (0, z0)), (z, lambda s, z0=z0: (0, z0 + 1))], 1, fa,
                             t["cw_a"], t["sw_a"])
        vr, vi = _ct_mid(br[0], bi[0], filt, order, fb, fbi, t["cw_b"], t["sw_b"])
        z = _ct_inv_a(vr, vi, fai, z, (0, z0), pc, (1 + order, ROWS_CTX // DEC_SEQ), skip, order,
                      out_dtypes[order])[None]
        z0 = 0
    return z.reshape(ROWS_LAT, D_MODEL)


def _fnet_chan_kernel(h_ref, sh_ref, sc_ref, w_ref, p_ref, q_ref):
    u = _modulate(h_ref[...], sh_ref[...], sc_ref[...]).astype(BF16)
    w = w_ref[...]
    for g in range(FNET_GROUPS):
        cols = slice(g * FNET_CG, (g + 1) * FNET_CG)
        r = _dot(u[:, cols], w)
        p_ref[:, cols] = r[:, :FNET_CG]
        q_ref[:, cols] = r[:, FNET_CG:]


def _fnet_chan(h, mods, w_cs, tm=512):
    row = pl.BlockSpec((tm, D_MODEL), lambda i: (i, 0))
    out = jax.ShapeDtypeStruct((ROWS, D_MODEL), F32)
    return pl.pallas_call(
        _fnet_chan_kernel,
        grid=(ROWS // tm,),
        in_specs=[row, _mod_spec(0, tm), _mod_spec(1, tm),
                  pl.BlockSpec((FNET_CG, 2 * FNET_CG), lambda i: (0, 0))],
        out_specs=[row, row],
        out_shape=[out, out],
        compiler_params=_params(("parallel",),
                                _vmem_limit(3 * _nbytes((tm, D_MODEL), F32),
                                            temps=2 * _nbytes((tm, D_MODEL), F32))),
        name="fnet_chan",
    )(h, mods, mods, w_cs)


def _fnet_pos_kernel(scale, c_ref, ns_ref, p_ref, q_ref, o_ref, acc_ref):
    k = pl.program_id(3)

    @pl.when(k == 0)
    def _():
        acc_ref[...] = jnp.zeros_like(acc_ref)

    acc_ref[...] += (_dot(c_ref[...], p_ref[...].astype(BF16))
                     + _dot(ns_ref[...], q_ref[...].astype(BF16)))

    @pl.when(k == pl.num_programs(3) - 1)
    def _():
        o_ref[...] = (acc_ref[...] * scale).astype(o_ref.dtype)


def _fnet_pos(c_tab, ns_tab, p, q, n_seq, length):
    tt, tn, tk = _seq_tiles(length)
    nb, nt, nk = D_MODEL // tn, length // tt, length // tk
    scale = (length * FNET_CG) ** -0.5
    return pl.pallas_call(
        functools.partial(_fnet_pos_kernel, scale),
        grid=(n_seq, nt, nb, nk),
        in_specs=[pl.BlockSpec((tt, tk), lambda s, t, n, k: (t, k)),
                  pl.BlockSpec((tt, tk), lambda s, t, n, k: (t, k)),
                  pl.BlockSpec((tk, tn), lambda s, t, n, k: (s * nk + k, n)),
                  pl.BlockSpec((tk, tn), lambda s, t, n, k: (s * nk + k, n))],
        out_specs=pl.BlockSpec((tt, tn), lambda s, t, n, k: (s * nt + t, n)),
        out_shape=jax.ShapeDtypeStruct((n_seq * length, D_MODEL), BF16),
        scratch_shapes=[pltpu.VMEM((tt, tn), F32)],
        compiler_params=_params(("parallel", "parallel", "parallel", "arbitrary"),
                                _vmem_limit(2 * _nbytes((tt, tk), BF16) + 2 * _nbytes((tk, tn), BF16)
                                            + _nbytes((tt, tn), BF16),
                                            resident=_nbytes((tt, tn), F32),
                                            temps=2 * _nbytes((tt, tn), F32))),
        name="fnet_pos",
    )(c_tab, ns_tab, p, q)


FN_N1, FN_N2 = 32, 128


def _fnet_pos_factored(p, q):
    assert FN_N1 * FN_N2 == DEC_SEQ and ROWS % DEC_SEQ == 0
    n1, n2 = FN_N1, FN_N2
    t = _ct_tables(n1, n2)
    fa = _kron_sub(jnp.block([[t["c1"], -t["s1"]], [-t["s1"], -t["c1"]]])).astype(BF16)
    fb_re = jnp.concatenate([t["c2"], t["s2"]], axis=1).astype(BF16)
    lat0 = ROWS_CTX // DEC_SEQ
    view = lambda x: x.reshape(ROWS // DEC_SEQ, n1, n2, D_MODEL)
    seq = lambda s: (lat0 + s,)
    br, bi = _ct_stage_a([(view(p), seq), (view(q), seq)], DEC_BATCH, fa, t["cw_a"], t["sw_a"])
    f = _ct_real_b(br, bi, fb_re, (DEC_SEQ * FNET_CG) ** -0.5)
    return f.reshape(ROWS_LAT, D_MODEL)


def _rope_table():
    rows = DEC_SEQ // GRID_W
    row = jnp.repeat(jnp.arange(rows), GRID_W).astype(F32)
    col = jnp.tile(jnp.arange(GRID_W), rows).astype(F32)
    inv = ROPE_THETA ** (-jnp.arange(0, AXIS_ROPE, 2, dtype=F32) / AXIS_ROPE)
    ang = jnp.concatenate([row[:, None] * inv, col[:, None] * inv], axis=-1)
    cos = jnp.repeat(jnp.cos(ang), 2, axis=-1)
    sin = jnp.repeat(jnp.sin(ang), 2, axis=-1)
    lat = jnp.tile(jnp.concatenate([cos, sin], axis=-1), (DEC_BATCH, 1))
    ctx = jnp.concatenate([jnp.ones((ROWS_CTX, QK_ROPE), F32), jnp.zeros((ROWS_CTX, QK_ROPE), F32)],
                          axis=-1)
    return jnp.concatenate([ctx, lat], axis=0)


def _pair_rotated(w):
    pairs = w.reshape(w.shape[:-1] + (QK_ROPE // 2, 2))
    return jnp.stack([-pairs[..., 1], pairs[..., 0]], axis=-1).reshape(w.shape)


def kernel(x_prompt, x_sample, c, cache_ckv, cache_krope, c_ctx, ada_w, ada_b, ln_g, ln_b, ffn_w_gate, ffn_w_up, ffn_w_down, mla_w_dq, mla_q_norm, mla_w_uq, mla_w_dkv, mla_kv_norm, mla_w_kr, mla_w_ukv, mla_w_o, hy_w_in, hy_b_in, hy_conv_w, hy_conv_b, hy_f_w1, hy_f_b1, hy_f_freq1, hy_f_w2, hy_f_b2, hy_f_freq2, hy_f_w3, hy_log_decay, hy_skip, hy_w_out, hy_b_out, fn_w_out, fn_b_out):
    assert x_prompt.shape == (BATCH, SEQ, D_MODEL) and x_sample.shape == (DEC_BATCH, DEC_SEQ, D_MODEL)
    assert ROWS_CTX % DEC_SEQ == 0 and SEQ == FNET_CG

    assert N_MIXERS > 0 and DEPTH > 1
    h = (x_prompt.reshape(ROWS_CTX, D_MODEL), x_sample.reshape(ROWS_LAT, D_MODEL))
    cond = jnp.concatenate([c_ctx[None, :], c, jnp.zeros((COND_PAD - N_COND, D_MODEL), F32)])
    mods_all = _modulation_vectors(cond, ada_w, ada_b)
    zero_bias = jnp.zeros((D_MODEL,), F32)
    ffn_w = (ffn_w_gate.astype(BF16), ffn_w_up.astype(BF16), ffn_w_down.astype(BF16))
    rope_tab = None
    ckv_states, krope_states = [], []

    for i in range(DEPTH):
        kind, j = i % N_MIXERS, i // N_MIXERS
        mods = mods_all[i]
        if kind == 0:
            if rope_tab is None:
                rope_tab = _rope_table()
            w_kr2 = jnp.concatenate([mla_w_kr[j], _pair_rotated(mla_w_kr[j])], axis=-1).astype(BF16)
            wq = mla_w_uq[j].reshape(Q_RANK, MLA_HEADS, QK_NOPE + QK_ROPE)
            w_q = jnp.concatenate([wq, _pair_rotated(wq[..., QK_NOPE:])], axis=-1)
            w_q = w_q.reshape(Q_RANK, MLA_HEADS * HEAD_W).astype(BF16)
            w_ukv = mla_w_ukv[j].reshape(KV_RANK, MLA_HEADS, QK_NOPE + V_DIM)
            w_k = w_ukv[..., :QK_NOPE].reshape(KV_RANK, MLA_HEADS * QK_NOPE).astype(BF16)
            w_vt = w_ukv[..., QK_NOPE:].reshape(KV_RANK, MLA_HEADS * V_DIM).T.astype(BF16)
            cq, ckv, kr, kr2 = _mla_down(h, mods, mla_w_dq[j].astype(BF16), mla_w_dkv[j].astype(BF16),
                                         w_kr2, mla_q_norm[j], mla_kv_norm[j], rope_tab)
            ckv_states.append(ckv[:ROWS_CTX].reshape(BATCH, SEQ, KV_RANK))
            krope_states.append(kr[:ROWS_CTX].reshape(BATCH, SEQ, QK_ROPE))
            q = _q_up(cq, w_q, rope_tab)
            k_tok, vt_tok = _kv_expand(ckv, kr2, w_k, w_vt)
            kc = cache_krope[:, j].reshape(DEC_BATCH * PAST_LEN, QK_ROPE).astype(BF16)
            k_cache, vt_cache = _kv_expand(cache_ckv[:, j].reshape(DEC_BATCH * PAST_LEN, KV_RANK),
                                           jnp.concatenate([kc, kc], axis=-1), w_k, w_vt)
            o = _attention(q, k_tok, vt_tok, k_cache, vt_cache)
            h = _mm_postnorm(o, mla_w_o[j].astype(BF16), zero_bias, h, mods, 2, ln_g[i, 0], ln_b[i, 0])
        elif kind == 1:
            pc = _hyena_in(h, mods, hy_w_in[j].astype(BF16), hy_b_in[j], hy_conv_w[j], hy_conv_b[j])
            fp = (hy_f_w1[j], hy_f_b1[j], hy_f_freq1[j], hy_f_w2[j], hy_f_b2[j], hy_f_freq2[j],
                  hy_f_w3[j], hy_log_decay[j])
            skip = hy_skip[j].reshape(HY_ORDER, 1, D_MODEL)
            z_ctx = _hyena_mix_dense(pc, *_hyena_filters(SEQ, *fp), skip, BATCH, SEQ)
            z_lat = _hyena_mix_pair(pc, *_hyena_filters(DEC_SEQ, *fp), skip)
            h = _mm_postnorm((z_ctx, z_lat), hy_w_out[j].astype(BF16), hy_b_out[j],
                             h, mods, 2, ln_g[i, 0], ln_b[i, 0])
        else:
            c_ch, ns_ch = _dft_tables(FNET_CG)
            p, q = _fnet_chan(h, mods, jnp.concatenate([c_ch, -ns_ch], axis=-1))
            f_ctx = _fnet_pos(c_ch, ns_ch, p, q, BATCH, SEQ)
            f_lat = _fnet_pos_factored(p, q)
            h = _mm_postnorm((f_ctx, f_lat), fn_w_out[j].astype(BF16), fn_b_out[j],
                             h, mods, 2, ln_g[i, 0], ln_b[i, 0])
        h = _ffn(h, mods, *ffn_w, i, ln_g[i, 1], ln_b[i, 1], split_out=i == DEPTH - 1)

    y_prompt = h[0].reshape(BATCH, SEQ, D_MODEL)
    y_sample = h[1].reshape(DEC_BATCH, DEC_SEQ, D_MODEL)
    return (y_prompt, y_sample, jnp.stack(ckv_states, axis=1), jnp.stack(krope_states, axis=1))
```

```python
import functools
import math

import jax
import jax.numpy as jnp
from jax import lax
from jax.experimental import pallas as pl
from jax.experimental.pallas import tpu as pltpu

F32 = jnp.float32
BF16 = jnp.bfloat16

D_MODEL = 2048
BATCH = 16
SEQ = 256
DEPTH = 4
DEC_BATCH = 2
DEC_SEQ = 4096
PAST_LEN = 512
GRID_W = 64
N_MIXERS = 3
MLA_HEADS = 16
QK_NOPE = 128
QK_ROPE = 64
V_DIM = 128
Q_RANK = 512
KV_RANK = 512
ROPE_THETA = 10000.0
AXIS_ROPE = QK_ROPE // 2
HY_ORDER = 2
HY_DIRS = 2
HY_CONV = 3
HY_BANDS = 16
HY_EMB = 1 + 2 * HY_BANDS
HY_FW = 64
HY_SHIFT = 0.05
FNET_GROUPS = 8
FNET_CG = D_MODEL // FNET_GROUPS
D_FF = -(-8 * D_MODEL // (3 * 256)) * 256
DN_ALPHA = (2 * DEPTH) ** 0.25
LN_EPS = 1e-5
RMS_EPS = 1e-6
N_MOD = 6

ROWS_CTX = BATCH * SEQ
ROWS_LAT = DEC_BATCH * DEC_SEQ
ROWS = ROWS_CTX + ROWS_LAT
N_COND = 1 + DEC_BATCH
COND_PAD = 8
HEAD_W = QK_NOPE + 2 * QK_ROPE
ATT_SCALE = (QK_NOPE + QK_ROPE) ** -0.5

V7X_VMEM_BYTES = 64 * 2 ** 20
VMEM_CAP_BYTES = V7X_VMEM_BYTES * 7 // 8
VMEM_FLOOR_BYTES = 32 * 2 ** 20


def _vmem_limit(pipelined, resident=0, temps=0):
    est = 2 * pipelined + resident + temps
    return int(min(max(est, VMEM_FLOOR_BYTES), VMEM_CAP_BYTES))


def _params(semantics, vmem):
    return pltpu.CompilerParams(dimension_semantics=semantics, vmem_limit_bytes=vmem)


def _nbytes(shape, dtype):
    return math.prod(shape) * jnp.dtype(dtype).itemsize


def _group_of_tile(i, tm):
    n_ctx = ROWS_CTX // tm
    return jnp.where(i < n_ctx, 0, 1 + (i - n_ctx) // (DEC_SEQ // tm))


def _mod_spec(which, tm):
    return pl.BlockSpec((None, 1, D_MODEL),
                        lambda i, *_: (which * COND_PAD + _group_of_tile(i, tm), 0, 0))


def _row_spec(width=D_MODEL):
    return pl.BlockSpec((1, width), lambda *_: (0, 0))


def _as_tuple(x):
    return x if isinstance(x, tuple) else (x,)


def _token_specs(xs, tm):
    width = xs[0].shape[1]
    if len(xs) == 1:
        return [pl.BlockSpec((tm, width), lambda i, *_: (i, 0))]
    n_ctx = ROWS_CTX // tm
    return [pl.BlockSpec((tm, width), lambda i, *_: (jnp.minimum(i, n_ctx - 1), 0)),
            pl.BlockSpec((tm, width), lambda i, *_: (jnp.maximum(i - n_ctx, 0), 0))]


def _token_rows(refs, r):
    if len(refs) == 1:
        return refs[0][r, :]
    n_ctx = ROWS_CTX // refs[0].shape[0]
    return jnp.where(pl.program_id(0) < n_ctx, refs[0][r, :], refs[1][r, :])


def _modulate(h, shift, scale):
    return h * (1.0 + scale) + shift


def _post_norm(h, delta, g, b):
    z = DN_ALPHA * h + delta
    mu = jnp.mean(z, axis=-1, keepdims=True)
    zc = z - mu
    var = jnp.mean(zc * zc, axis=-1, keepdims=True)
    return zc * lax.rsqrt(var + LN_EPS) * g + b


def _rms_norm(x, g):
    ms = jnp.mean(x * x, axis=-1, keepdims=True)
    return x * lax.rsqrt(ms + RMS_EPS) * g


def _dot(a, b):
    return jnp.dot(a, b, preferred_element_type=F32)


def _modvec_kernel(c_ref, w_ref, b_ref, o_ref):
    a = jax.nn.silu(c_ref[...]).astype(BF16)
    o_ref[...] = _dot(a, w_ref[...].astype(BF16)) + b_ref[...]


def _modulation_vectors(cond, ada_w, ada_b):
    tn = 1024
    n = N_MOD * D_MODEL
    out = pl.pallas_call(
        _modvec_kernel,
        grid=(DEPTH, n // tn),
        in_specs=[pl.BlockSpec((COND_PAD, D_MODEL), lambda l, j: (0, 0)),
                  pl.BlockSpec((None, D_MODEL, tn), lambda l, j: (l, 0, j)),
                  pl.BlockSpec((None, 1, tn), lambda l, j: (l, 0, j))],
        out_specs=pl.BlockSpec((None, COND_PAD, tn), lambda l, j: (l, 0, j)),
        out_shape=jax.ShapeDtypeStruct((DEPTH, COND_PAD, n), F32),
        compiler_params=_params(("parallel", "parallel"),
                                _vmem_limit(_nbytes((D_MODEL, tn), F32),
                                            temps=_nbytes((D_MODEL, tn), BF16))),
        name="modvec",
    )(cond, ada_w, ada_b.reshape(DEPTH, 1, n))
    out = out.reshape(DEPTH, COND_PAD, N_MOD, D_MODEL).transpose(0, 2, 1, 3)
    return out.reshape(DEPTH, N_MOD * COND_PAD, 1, D_MODEL)


def _hyena_in_kernel(n_ctx, h_ref, hp_ref, hn_ref, sh_ref, sc_ref, w_ref, b_ref, cw_ref, cb_ref,
                     o_ref, u_ref):
    i = pl.program_id(0)
    tm = h_ref.shape[0]

    @pl.when(pl.program_id(1) == 0)
    def _():
        mod = lambda x_ref: _modulate(x_ref[...], sh_ref[...], sc_ref[...]).astype(BF16)
        u_ref[:SUB, :] = mod(hp_ref)
        u_ref[SUB:SUB + tm, :] = mod(h_ref)
        u_ref[SUB + tm:, :] = mod(hn_ref)

    length = jnp.where(i < n_ctx, SEQ, DEC_SEQ)
    half = tm // 2
    starts = (0, half)
    ys = [_dot(u_ref[r0:r0 + half + 2 * SUB, :], w_ref[...]) + b_ref[...] for r0 in starts]
    for r0, y in zip(starts, ys):
        inner = slice(SUB, SUB + half)
        before = pltpu.roll(y, 1, 0)[inner]
        after = pltpu.roll(y, y.shape[0] - 1, 0)[inner]
        pos = (lax.broadcasted_iota(jnp.int32, before.shape, 0) + (i * tm + r0)) & (length - 1)
        before = jnp.where(pos == 0, 0.0, before)
        after = jnp.where(pos == length - 1, 0.0, after)
        o_ref[r0:r0 + half, :] = (cw_ref[0:1, :] * before + cw_ref[1:2, :] * y[inner]
                                  + cw_ref[2:3, :] * after + cb_ref[...])


def _hyena_in(h, mods, w, b, conv_w, conv_b, tm=1024, tn=1024):
    assert SEQ & (SEQ - 1) == 0 and DEC_SEQ & (DEC_SEQ - 1) == 0 and tm % SEQ == 0 and DEC_SEQ % tm == 0
    k, n = w.shape
    per, halo, last = D_MODEL // tn, tm // SUB, ROWS // SUB - 1
    return pl.pallas_call(
        functools.partial(_hyena_in_kernel, ROWS_CTX // tm),
        grid=(ROWS // tm, n // tn),
        in_specs=[pl.BlockSpec((tm, k), lambda i, j: (i, 0)),
                  pl.BlockSpec((SUB, k), lambda i, j: (jnp.maximum(i * halo - 1, 0), 0)),
                  pl.BlockSpec((SUB, k), lambda i, j: (jnp.minimum((i + 1) * halo, last), 0)),
                  _mod_spec(0, tm), _mod_spec(1, tm),
                  pl.BlockSpec((k, tn), lambda i, j: (0, j)),
                  pl.BlockSpec((1, tn), lambda i, j: (0, j)),
                  pl.BlockSpec((HY_CONV, tn), lambda i, j: (0, j)),
                  pl.BlockSpec((1, tn), lambda i, j: (0, j))],
        out_specs=pl.BlockSpec((None, tm, tn), lambda i, j: (j // per, i, j % per)),
        out_shape=jax.ShapeDtypeStruct((n // D_MODEL, ROWS, D_MODEL), F32),
        scratch_shapes=[pltpu.VMEM((tm + 2 * SUB, k), BF16)],
        compiler_params=_params(("parallel", "arbitrary"),
                                _vmem_limit(_nbytes((tm, k), F32) + _nbytes((k, tn), BF16)
                                            + _nbytes((tm, tn), F32),
                                            resident=_nbytes((tm, k), BF16),
                                            temps=5 * _nbytes((tm, tn), F32))),
        name="hyena_in",
    )(h, h, h, mods, mods, w, b.reshape(1, n), conv_w, conv_b.reshape(1, n))


def _mm_postnorm_kernel(n_a, n_h, *refs):
    a_refs, refs = refs[:n_a], refs[n_a:]
    (w_ref, bias_ref), refs = refs[:2], refs[2:]
    h_refs, (gate_ref, g_ref, b_ref, o_ref) = refs[:n_h], refs[n_h:]
    half = o_ref.shape[0] // 2
    halves = (slice(0, half), slice(half, 2 * half))
    ys = [_dot(_token_rows(a_refs, r), w_ref[...]) + bias_ref[...] for r in halves]
    for r, y in zip(halves, ys):
        o_ref[r, :] = _post_norm(_token_rows(h_refs, r), gate_ref[...] * y, g_ref[...], b_ref[...])


def _mm_postnorm(a, w, bias, h, mods, which_gate, ln_g, ln_b, tm=512):
    a, h = _as_tuple(a), _as_tuple(h)
    k = a[0].shape[1]
    return pl.pallas_call(
        functools.partial(_mm_postnorm_kernel, len(a), len(h)),
        grid=(ROWS // tm,),
        in_specs=(_token_specs(a, tm) + [pl.BlockSpec((k, D_MODEL), lambda i: (0, 0)), _row_spec()]
                  + _token_specs(h, tm) + [_mod_spec(which_gate, tm), _row_spec(), _row_spec()]),
        out_specs=pl.BlockSpec((tm, D_MODEL), lambda i: (i, 0)),
        out_shape=jax.ShapeDtypeStruct((ROWS, D_MODEL), F32),
        compiler_params=_params(("parallel",),
                                _vmem_limit(len(a) * _nbytes((tm, k), BF16)
                                            + _nbytes((k, D_MODEL), BF16)
                                            + (1 + len(h)) * _nbytes((tm, D_MODEL), F32),
                                            temps=3 * _nbytes((tm, D_MODEL), F32))),
        name="mm_postnorm",
    )(*a, w, bias.reshape(1, D_MODEL), *h, mods, ln_g.reshape(1, D_MODEL), ln_b.reshape(1, D_MODEL))


def _ffn_kernel(n_ctx, h_ref, sh_ref, sc_ref, gate_ref, g_ref, b_ref, wg_ref, wu_ref, wd_ref,
                *refs):
    i, f = pl.program_id(0), pl.program_id(1)
    if n_ctx is None:
        o_ref, u_ref = refs
        acc_ref, outs = o_ref, ((o_ref, None),)
    else:
        octx_ref, olat_ref, u_ref, acc_ref = refs
        outs = ((octx_ref, i < n_ctx), (olat_ref, i >= n_ctx))

    @pl.when(f == 0)
    def _():
        u_ref[...] = _modulate(h_ref[...], sh_ref[...], sc_ref[...]).astype(BF16)
        acc_ref[...] = jnp.zeros_like(acc_ref)

    half = u_ref.shape[0] // 2
    halves = (slice(0, half), slice(half, 2 * half))
    proj = [(_dot(u_ref[r, :], wg_ref[...]), _dot(u_ref[r, :], wu_ref[...])) for r in halves]
    for r, (gate, up) in zip(halves, proj):
        act = (jax.nn.silu(gate) * up).astype(BF16)
        acc_ref[r, :] += _dot(act, wd_ref[...])

    last = f == pl.num_programs(1) - 1
    for o_ref, mine in outs:
        @pl.when(last if mine is None else jnp.logical_and(last, mine))
        def _(o_ref=o_ref):
            for r in halves:
                o_ref[r, :] = _post_norm(h_ref[r, :], gate_ref[...] * acc_ref[r, :],
                                         g_ref[...], b_ref[...])


def _ffn(h, mods, w_gate, w_up, w_down, layer, ln_g, ln_b, split_out, tm=512, tf=512):
    tile = _nbytes((tm, D_MODEL), F32)
    if split_out:
        n_ctx = ROWS_CTX // tm
        out_specs = [pl.BlockSpec((tm, D_MODEL), lambda i, f: (jnp.minimum(i, n_ctx - 1), 0)),
                     pl.BlockSpec((tm, D_MODEL), lambda i, f: (jnp.maximum(i - n_ctx, 0), 0))]
        out_shape = [jax.ShapeDtypeStruct((ROWS_CTX, D_MODEL), F32),
                     jax.ShapeDtypeStruct((ROWS_LAT, D_MODEL), F32)]
        scratch = [pltpu.VMEM((tm, D_MODEL), BF16), pltpu.VMEM((tm, D_MODEL), F32)]
        pipelined, resident = 3 * tile, tile + tile // 2
    else:
        n_ctx = None
        out_specs = pl.BlockSpec((tm, D_MODEL), lambda i, f: (i, 0))
        out_shape = jax.ShapeDtypeStruct((ROWS, D_MODEL), F32)
        scratch = [pltpu.VMEM((tm, D_MODEL), BF16)]
        pipelined, resident = 2 * tile, tile // 2
    return pl.pallas_call(
        functools.partial(_ffn_kernel, n_ctx),
        grid=(ROWS // tm, D_FF // tf),
        in_specs=[pl.BlockSpec((tm, D_MODEL), lambda i, f: (i, 0)),
                  _mod_spec(3, tm), _mod_spec(4, tm), _mod_spec(5, tm),
                  _row_spec(), _row_spec(),
                  pl.BlockSpec((None, D_MODEL, tf), lambda i, f: (layer, 0, f)),
                  pl.BlockSpec((None, D_MODEL, tf), lambda i, f: (layer, 0, f)),
                  pl.BlockSpec((None, tf, D_MODEL), lambda i, f: (layer, f, 0))],
        out_specs=out_specs,
        out_shape=out_shape,
        scratch_shapes=scratch,
        compiler_params=_params(("parallel", "arbitrary"),
                                _vmem_limit(pipelined + 3 * _nbytes((D_MODEL, tf), BF16),
                                            resident=resident, temps=2 * tile)),
        name="ffn",
    )(h, mods, mods, mods, ln_g.reshape(1, D_MODEL), ln_b.reshape(1, D_MODEL),
      w_gate, w_up, w_down)


def _mla_down_kernel(n_h, *refs):
    h_refs, (sh_ref, sc_ref, wdq_ref, wdkv_ref, wkr_ref, qn_ref, kvn_ref, rope_ref,
             cq_ref, ckv_ref, kr_ref, kr2_ref) = refs[:n_h], refs[n_h:]
    u = _modulate(_token_rows(h_refs, slice(None)), sh_ref[...], sc_ref[...]).astype(BF16)
    cq_ref[...] = _rms_norm(_dot(u, wdq_ref[...]), qn_ref[...]).astype(BF16)
    ckv_ref[...] = _rms_norm(_dot(u, wdkv_ref[...]), kvn_ref[...])
    t = _dot(u, wkr_ref[...])
    kr_ref[...] = t[:, :QK_ROPE]
    v = t * rope_ref[...]
    kr2_ref[...] = (v + pltpu.roll(v, QK_ROPE, 1)).astype(BF16)


def _mla_down(h, mods, w_dq, w_dkv, w_kr2, q_norm, kv_norm, rope_tab, tm=512):
    h = _as_tuple(h)
    row = lambda width: pl.BlockSpec((tm, width), lambda i: (i, 0))
    full = lambda shape: pl.BlockSpec(shape, lambda i: (0, 0))
    return pl.pallas_call(
        functools.partial(_mla_down_kernel, len(h)),
        grid=(ROWS // tm,),
        in_specs=_token_specs(h, tm) + [
            _mod_spec(0, tm), _mod_spec(1, tm),
            full((D_MODEL, Q_RANK)), full((D_MODEL, KV_RANK)), full((D_MODEL, 2 * QK_ROPE)),
            _row_spec(Q_RANK), _row_spec(KV_RANK), row(2 * QK_ROPE)],
        out_specs=[row(Q_RANK), row(KV_RANK), row(QK_ROPE), row(2 * QK_ROPE)],
        out_shape=[jax.ShapeDtypeStruct((ROWS, Q_RANK), BF16),
                   jax.ShapeDtypeStruct((ROWS, KV_RANK), F32),
                   jax.ShapeDtypeStruct((ROWS, QK_ROPE), F32),
                   jax.ShapeDtypeStruct((ROWS, 2 * QK_ROPE), BF16)],
        compiler_params=_params(("parallel",),
                                _vmem_limit(len(h) * _nbytes((tm, D_MODEL), F32)
                                            + _nbytes((D_MODEL, Q_RANK + KV_RANK + 2 * QK_ROPE), BF16)
                                            + 3 * _nbytes((tm, KV_RANK), F32),
                                            temps=2 * _nbytes((tm, D_MODEL), F32))),
        name="mla_down",
    )(*h, mods, mods, w_dq, w_dkv, w_kr2, q_norm.reshape(1, Q_RANK), kv_norm.reshape(1, KV_RANK),
      rope_tab)


NT_DIMS = (((1,), (1,)), ((), ()))


def _q_up_kernel(cq_ref, wt_ref, rope_t_ref, qt_ref):
    cq, tab = cq_ref[...], rope_t_ref[...] * LOG2E_SCALE
    for h in range(MLA_HEADS):
        r = lax.dot_general(wt_ref[h * HEAD_W:(h + 1) * HEAD_W, :], cq, NT_DIMS,
                            preferred_element_type=F32)
        qt_ref[h, :QK_NOPE, :] = (r[:QK_NOPE] * LOG2E_SCALE).astype(BF16)
        qt_ref[h, QK_NOPE:, :] = (r[QK_NOPE:] * tab).astype(BF16)


def _q_up(cq, w_qt, rope_tab_t, tm=512):
    return pl.pallas_call(
        _q_up_kernel,
        grid=(ROWS // tm,),
        in_specs=[pl.BlockSpec((tm, Q_RANK), lambda i: (i, 0)),
                  pl.BlockSpec((MLA_HEADS * HEAD_W, Q_RANK), lambda i: (0, 0)),
                  pl.BlockSpec((2 * QK_ROPE, tm), lambda i: (0, i))],
        out_specs=pl.BlockSpec((MLA_HEADS, HEAD_W, tm), lambda i: (0, 0, i)),
        out_shape=jax.ShapeDtypeStruct((MLA_HEADS, HEAD_W, ROWS), BF16),
        compiler_params=_params(("parallel",),
                                _vmem_limit(_nbytes((Q_RANK + tm, MLA_HEADS * HEAD_W), BF16))),
        name="q_up",
    )(cq, w_qt, rope_tab_t)


VT_PAD = 16
VT_ROWS = V_DIM + VT_PAD


def _kv_expand_kernel(ckv_ref, kr2_ref, wk_ref, wvt_ref, k_ref, vt_ref):
    c = ckv_ref[...].astype(BF16)
    vt = lax.dot_general(wvt_ref[...], c, NT_DIMS, preferred_element_type=F32).astype(BF16)
    ones = jnp.ones((VT_PAD, vt.shape[1]), BF16)
    for h in range(MLA_HEADS):
        vt_ref[h * VT_ROWS:h * VT_ROWS + V_DIM, :] = vt[h * V_DIM:(h + 1) * V_DIM]
        vt_ref[h * VT_ROWS + V_DIM:(h + 1) * VT_ROWS, :] = ones
    kr2 = kr2_ref[...]
    pair_w = 2 * QK_NOPE
    for g in range(MLA_HEADS // 2):
        r = _dot(c, wk_ref[:, g * pair_w:(g + 1) * pair_w]).astype(BF16)
        for e in range(2):
            k_ref[2 * g + e, :, :QK_NOPE] = r[:, e * QK_NOPE:(e + 1) * QK_NOPE]
            k_ref[2 * g + e, :, QK_NOPE:] = kr2


def _kv_expand(ckv, kr2, w_k, w_vt, tm=512):
    rows = ckv.shape[0]
    return pl.pallas_call(
        _kv_expand_kernel,
        grid=(rows // tm,),
        in_specs=[pl.BlockSpec((tm, KV_RANK), lambda i: (i, 0)),
                  pl.BlockSpec((tm, 2 * QK_ROPE), lambda i: (i, 0)),
                  pl.BlockSpec((KV_RANK, MLA_HEADS * QK_NOPE), lambda i: (0, 0)),
                  pl.BlockSpec((MLA_HEADS * V_DIM, KV_RANK), lambda i: (0, 0))],
        out_specs=[pl.BlockSpec((MLA_HEADS, tm, HEAD_W), lambda i: (0, i, 0)),
                   pl.BlockSpec((MLA_HEADS * VT_ROWS, tm), lambda i: (0, i))],
        out_shape=[jax.ShapeDtypeStruct((MLA_HEADS, rows, HEAD_W), BF16),
                   jax.ShapeDtypeStruct((MLA_HEADS * VT_ROWS, rows), BF16)],
        compiler_params=_params(("parallel",),
                                _vmem_limit(_nbytes((tm, MLA_HEADS * (HEAD_W + V_DIM)), BF16)
                                            + 2 * _nbytes((KV_RANK, MLA_HEADS * V_DIM), BF16),
                                            temps=_nbytes((MLA_HEADS * V_DIM, tm), F32))),
        name="kv_expand",
    )(ckv, kr2, w_k, w_vt)


ATT_CHUNK = 512
ATT_SKEW = 3
LOG2E_SCALE = ATT_SCALE * math.log2(math.e)


def _attn_scores(qt, k):
    return _dot(k, qt)


def _attn_values(s, vt, carry):
    m = jnp.max(s, axis=0, keepdims=True)
    if carry is not None:
        m_old, acc_old = carry
        m = jnp.maximum(m_old, m)
    acc = _dot(vt, jnp.exp2(s - m).astype(BF16))
    if carry is not None:
        acc = jnp.exp2(m_old - m) * acc_old + acc
    return m, acc


def _attn_output(acc):
    return (acc[:V_DIM] / acc[V_DIM:V_DIM + 1]).T.astype(BF16)


def _attn_ctx_kernel(q_ref, k_ref, vt_ref, o_ref):
    for h in range(MLA_HEADS):
        _, acc = _attn_values(_attn_scores(q_ref[h], k_ref[h]),
                              vt_ref[h * VT_ROWS:(h + 1) * VT_ROWS, :], None)
        o_ref[:, h * V_DIM:(h + 1) * V_DIM] = _attn_output(acc)


def _attn_lat_kernel(q_ref, k_ref, vt_ref, kc_ref, vtc_ref, prev_ref, o_ref):
    del prev_ref
    q = q_ref[...]
    n_tok = DEC_SEQ // ATT_CHUNK
    rows = lambda c: slice(c * ATT_CHUNK, (c + 1) * ATT_CHUNK)
    keys = [k_ref.at[rows(c), :] for c in range(n_tok)] + [kc_ref]
    vals = [vt_ref.at[:, rows(c)] for c in range(n_tok)] + [vtc_ref]
    carry = None
    scores = [_attn_scores(q, keys[c][...]) for c in range(ATT_SKEW)]
    for c in range(n_tok + 1):
        if c + ATT_SKEW <= n_tok:
            scores.append(_attn_scores(q, keys[c + ATT_SKEW][...]))
        carry = _attn_values(scores[c], vals[c][...], carry)
    o_ref[...] = _attn_output(carry[1])


def _attention(q, k_tok, vt_tok, k_cache, vt_cache, tq=1024):
    assert DEC_SEQ % ATT_CHUNK == 0
    out_shape = jax.ShapeDtypeStruct((ROWS, MLA_HEADS * V_DIM), BF16)
    o = pl.pallas_call(
        _attn_ctx_kernel,
        grid=(BATCH,),
        in_specs=[pl.BlockSpec((MLA_HEADS, HEAD_W, SEQ), lambda s: (0, 0, s)),
                  pl.BlockSpec((MLA_HEADS, SEQ, HEAD_W), lambda s: (0, s, 0)),
                  pl.BlockSpec((MLA_HEADS * VT_ROWS, SEQ), lambda s: (0, s))],
        out_specs=pl.BlockSpec((SEQ, MLA_HEADS * V_DIM), lambda s: (s, 0)),
        out_shape=out_shape,
        compiler_params=_params(("parallel",), VMEM_FLOOR_BYTES),
        name="attn_ctx",
    )(q, k_tok, vt_tok)

    lat0 = ROWS_CTX // DEC_SEQ
    q0 = ROWS_CTX // tq
    nq = DEC_SEQ // tq
    return pl.pallas_call(
        _attn_lat_kernel,
        grid=(DEC_BATCH, MLA_HEADS, nq),
        in_specs=[pl.BlockSpec((None, HEAD_W, tq), lambda b, h, i: (h, 0, q0 + b * nq + i)),
                  pl.BlockSpec((None, DEC_SEQ, HEAD_W), lambda b, h, i: (h, lat0 + b, 0)),
                  pl.BlockSpec((VT_ROWS, DEC_SEQ), lambda b, h, i: (h, lat0 + b)),
                  pl.BlockSpec((None, PAST_LEN, HEAD_W), lambda b, h, i: (h, b, 0)),
                  pl.BlockSpec((VT_ROWS, PAST_LEN), lambda b, h, i: (h, b)),
                  pl.BlockSpec(memory_space=pl.ANY)],
        out_specs=pl.BlockSpec((tq, V_DIM), lambda b, h, i: (q0 + b * nq + i, h)),
        out_shape=out_shape,
        input_output_aliases={5: 0},
        compiler_params=_params(("parallel", "parallel", "arbitrary"),
                                _vmem_limit(_nbytes((DEC_SEQ + PAST_LEN, HEAD_W + V_DIM), BF16),
                                            temps=8 * _nbytes((ATT_CHUNK, tq), F32))),
        name="attn_lat",
    )(q, k_tok, vt_tok, k_cache, vt_cache, o)


def _mm_kernel(a_ref, b_ref, o_ref, acc_ref):
    k = pl.program_id(2)

    @pl.when(k == 0)
    def _():
        acc_ref[...] = jnp.zeros_like(acc_ref)

    acc_ref[...] += _dot(a_ref[...], b_ref[...].astype(BF16))

    @pl.when(k == pl.num_programs(2) - 1)
    def _():
        o_ref[...] = acc_ref[...].astype(o_ref.dtype)


def _mm(a, b, out_dtype=F32, tm=1024, tn=1024, tk=512):
    m, kk = a.shape
    n = b.shape[1]
    tm, tn, tk = min(tm, m), min(tn, n), min(tk, kk)
    return pl.pallas_call(
        _mm_kernel,
        grid=(m // tm, n // tn, kk // tk),
        in_specs=[pl.BlockSpec((tm, tk), lambda i, j, k: (i, k)),
                  pl.BlockSpec((tk, tn), lambda i, j, k: (k, j))],
        out_specs=pl.BlockSpec((tm, tn), lambda i, j, k: (i, j)),
        out_shape=jax.ShapeDtypeStruct((m, n), out_dtype),
        scratch_shapes=[pltpu.VMEM((tm, tn), F32)],
        compiler_params=_params(("parallel", "parallel", "arbitrary"), VMEM_FLOOR_BYTES),
        name="mm",
    )(a, b)


def _seq_tiles(length):
    if length >= 1024:
        return 1024, 512, 1024
    return length, D_MODEL, length


def _dft_fwd_kernel(c_ref, s_ref, z_ref, kr_ref, ks_ref, yr_ref, ys_ref, accr_ref, accs_ref):
    k = pl.program_id(3)

    @pl.when(k == 0)
    def _():
        accr_ref[...] = jnp.zeros_like(accr_ref)
        accs_ref[...] = jnp.zeros_like(accs_ref)

    z = z_ref[...].astype(BF16)
    accr_ref[...] += _dot(c_ref[...], z)
    accs_ref[...] += _dot(s_ref[...], z)

    @pl.when(k == pl.num_programs(3) - 1)
    def _():
        zr, zs, kr, ks = accr_ref[...], accs_ref[...], kr_ref[...], ks_ref[...]
        yr_ref[...] = (zr * kr - zs * ks).astype(BF16)
        ys_ref[...] = (zr * ks + zs * kr).astype(BF16)


def _dft_fwd(tabs, z, z_which, kr, ks, order, n_seq, length):
    tf, tn, tk = _seq_tiles(length)
    nb, nf, nk = D_MODEL // tn, length // tf, length // tk
    kc0 = order * nb
    out = jax.ShapeDtypeStruct((n_seq * length, D_MODEL), BF16)
    return pl.pallas_call(
        _dft_fwd_kernel,
        grid=(n_seq, nf, nb, nk),
        in_specs=[pl.BlockSpec((tf, tk), lambda s, f, n, k: (f, k)),
                  pl.BlockSpec((tf, tk), lambda s, f, n, k: (f, k)),
                  pl.BlockSpec((None, tk, tn), lambda s, f, n, k: (z_which, s * nk + k, n)),
                  pl.BlockSpec((tf, tn), lambda s, f, n, k: (f, kc0 + n)),
                  pl.BlockSpec((tf, tn), lambda s, f, n, k: (f, kc0 + n))],
        out_specs=[pl.BlockSpec((tf, tn), lambda s, f, n, k: (s * nf + f, n)),
                   pl.BlockSpec((tf, tn), lambda s, f, n, k: (s * nf + f, n))],
        out_shape=[out, out],
        scratch_shapes=[pltpu.VMEM((tf, tn), F32), pltpu.VMEM((tf, tn), F32)],
        compiler_params=_params(("parallel", "parallel", "parallel", "arbitrary"),
                                _vmem_limit(2 * _nbytes((tf, tk), BF16) + _nbytes((tk, tn), F32)
                                            + 2 * _nbytes((tf, tn), F32) + 2 * _nbytes((tf, tn), BF16),
                                            resident=2 * _nbytes((tf, tn), F32),
                                            temps=4 * _nbytes((tf, tn), F32))),
        name="dft_fwd",
    )(tabs["c"], tabs["s"], z, kr, ks)


def _dft_inv_kernel(inv_len, ct_ref, st_ref, yr_ref, ys_ref, z_ref, gate_ref, skip_ref,
                    o_ref, acc_ref):
    k = pl.program_id(3)

    @pl.when(k == 0)
    def _():
        acc_ref[...] = jnp.zeros_like(acc_ref)

    acc_ref[...] += _dot(ct_ref[...], yr_ref[...]) + _dot(st_ref[...], ys_ref[...])

    @pl.when(k == pl.num_programs(3) - 1)
    def _():
        y = acc_ref[...] * inv_len + skip_ref[...] * z_ref[...]
        o_ref[...] = (gate_ref[...] * y).astype(o_ref.dtype)


def _dft_inv(tabs, yr, ys, z, z_which, gate, gate_which, skip, order, n_seq, length, out_dtype):
    tt, tn, tk = _seq_tiles(length)
    nb, nt, nk = D_MODEL // tn, length // tt, length // tk
    return pl.pallas_call(
        functools.partial(_dft_inv_kernel, 1.0 / length),
        grid=(n_seq, nt, nb, nk),
        in_specs=[pl.BlockSpec((tt, tk), lambda s, t, n, k: (t, k)),
                  pl.BlockSpec((tt, tk), lambda s, t, n, k: (t, k)),
                  pl.BlockSpec((tk, tn), lambda s, t, n, k: (s * nk + k, n)),
                  pl.BlockSpec((tk, tn), lambda s, t, n, k: (s * nk + k, n)),
                  pl.BlockSpec((None, tt, tn), lambda s, t, n, k: (z_which, s * nt + t, n)),
                  pl.BlockSpec((None, tt, tn), lambda s, t, n, k: (gate_which, s * nt + t, n)),
                  pl.BlockSpec((None, 1, tn), lambda s, t, n, k: (order, 0, n))],
        out_specs=pl.BlockSpec((tt, tn), lambda s, t, n, k: (s * nt + t, n)),
        out_shape=jax.ShapeDtypeStruct((n_seq * length, D_MODEL), out_dtype),
        scratch_shapes=[pltpu.VMEM((tt, tn), F32)],
        compiler_params=_params(("parallel", "parallel", "parallel", "arbitrary"),
                                _vmem_limit(2 * _nbytes((tt, tk), BF16) + 2 * _nbytes((tk, tn), BF16)
                                            + 3 * _nbytes((tt, tn), F32),
                                            resident=_nbytes((tt, tn), F32),
                                            temps=3 * _nbytes((tt, tn), F32))),
        name="dft_inv",
    )(tabs["ct"], tabs["st"], yr, ys, z, gate, skip)


TW_LANES = 128
SUB = 8
CT_ROWS = 2 * SUB
CT_COLS = 1024


def _lane_tile(x, width):
    return jnp.tile(x, (1, width // x.shape[-1]))


def _kron_sub(f):
    return jnp.kron(f, jnp.eye(SUB, dtype=f.dtype))


def _sub_rows(x, h):
    part = x[:, h * SUB:(h + 1) * SUB, :]
    return part.reshape(part.shape[0] * SUB, part.shape[2])


def _from_sub_rows(parts):
    split = [p.reshape(p.shape[0] // SUB, SUB, p.shape[1]) for p in parts]
    return jnp.concatenate(split, axis=1)


def _ct_stage_a_kernel(n_in, *refs):
    x_refs, (fa_ref, cw_ref, sw_ref, br_ref, bi_ref) = refs[:n_in], refs[n_in:]
    fa = fa_ref[...]
    half = fa.shape[0] // 2
    width = br_ref.shape[-1]
    xs = [r[...] for r in x_refs]
    b_re, b_im = [], []
    for h in range(CT_ROWS // SUB):
        x = jnp.concatenate([_sub_rows(x, h) for x in xs], axis=0).astype(BF16)
        a = _dot(fa, x)
        ar, ai = a[:half], a[half:]
        cw, sw = _lane_tile(cw_ref[h], width), _lane_tile(sw_ref[h], width)
        b_re.append(ar * cw + ai * sw)
        b_im.append(ai * cw - ar * sw)
    br_ref[...] = _from_sub_rows(b_re).astype(BF16)
    bi_ref[...] = _from_sub_rows(b_im).astype(BF16)


def _ct_stage_a(xs, n_seq, fa, cw, sw):
    n1 = fa.shape[0] // (2 * SUB)
    n2 = cw.shape[0] * CT_ROWS
    width = xs[0][0].shape[-1]
    tn2, tw = CT_ROWS, CT_COLS
    in_specs, blocks = [], 0
    for arr, prefix in xs:
        rows_in = arr.shape[-3]
        lead = (None,) * (arr.ndim - 3)
        in_specs.append(pl.BlockSpec(lead + (rows_in, tn2, tw),
                                     lambda s, i, c, prefix=prefix: prefix(s) + (0, i, c)))
        blocks += _nbytes((rows_in, tn2, tw), arr.dtype)
    twid = pl.BlockSpec((None,) + cw.shape[1:], lambda s, i, c: (i, 0, 0, 0))
    in_specs += [pl.BlockSpec(fa.shape, lambda s, i, c: (0, 0)), twid, twid]
    out = jax.ShapeDtypeStruct((n_seq, n1, n2, width), BF16)
    out_spec = pl.BlockSpec((None, n1, tn2, tw), lambda s, i, c: (s, 0, i, c))
    return pl.pallas_call(
        functools.partial(_ct_stage_a_kernel, len(xs)),
        grid=(n_seq, n2 // tn2, width // tw),
        in_specs=in_specs,
        out_specs=[out_spec, out_spec],
        out_shape=[out, out],
        compiler_params=_params(("parallel", "parallel", "parallel"),
                                _vmem_limit(blocks + 2 * _nbytes((n1, tn2, tw), BF16)
                                            + _nbytes(fa.shape, BF16),
                                            temps=8 * _nbytes((n1, tn2, tw), F32))),
        name="ct_stage_a",
    )(*[arr for arr, _ in xs], fa, cw, sw)


def _ct_mid_kernel(br_ref, bi_ref, pr_ref, pi_ref, mr_ref, mi_ref, fb_ref, fbi_ref, cw_ref, sw_ref,
                   vr_ref, vi_ref):
    fb, fbi = fb_ref[...], fbi_ref[...]
    half = fb.shape[0] // 2
    width = br_ref.shape[-1]
    for j in range(br_ref.shape[0]):
        stack = lambda re_ref, im_ref: jnp.concatenate([re_ref[j], im_ref[j]], axis=0)
        kr = _dot(fb[:half], stack(pr_ref, pi_ref))
        ki = _dot(fb[half:], stack(mr_ref, mi_ref))
        x = _dot(fb, stack(br_ref, bi_ref))
        xr, xi = x[:half], x[half:]
        y = jnp.concatenate([xr * kr - xi * ki, xr * ki + xi * kr], axis=0).astype(BF16)
        v = _dot(fbi, y)
        vr, vi = v[:half], v[half:]
        cw, sw = _lane_tile(cw_ref[j], width), _lane_tile(sw_ref[j], width)
        vr_ref[j] = (vr * cw - vi * sw).astype(BF16)
        vi_ref[j] = (vi * cw + vr * sw).astype(BF16)


def _ct_mid(br, bi, filt, order, fb, fbi, cw, sw, tk1=4, td=1024):
    n1, n2, d = br.shape
    nd = d // td
    data = pl.BlockSpec((tk1, n2, td), lambda i, j: (i, 0, j))
    coef = pl.BlockSpec((tk1, n2, td), lambda i, j: (i, 0, order * nd + j))
    mat = pl.BlockSpec(fb.shape, lambda i, j: (0, 0))
    tw = pl.BlockSpec((tk1, n2, TW_LANES), lambda i, j: (i, 0, 0))
    out = jax.ShapeDtypeStruct((n1, n2, d), BF16)
    return pl.pallas_call(
        _ct_mid_kernel,
        grid=(n1 // tk1, nd),
        in_specs=[data, data, coef, coef, coef, coef, mat, mat, tw, tw],
        out_specs=[data, data],
        out_shape=[out, out],
        compiler_params=_params(("parallel", "parallel"),
                                _vmem_limit(8 * _nbytes((tk1, n2, td), BF16),
                                            temps=10 * _nbytes((2 * n2, td), F32))),
        name="ct_mid",
    )(br, bi, *filt, fb, fbi, cw, sw)


def _ct_inv_a_kernel(scale, vr_ref, vi_ref, fai_ref, z0_ref, z1_ref, g0_ref, g1_ref, skip_ref,
                     o_ref):
    fai = fai_ref[...]
    half = fai.shape[0] // 2
    skip = skip_ref[...]
    vr, vi = vr_ref[...].astype(F32), vi_ref[...].astype(F32)
    zs, gs = (z0_ref[...], z1_ref[...]), (g0_ref[...], g1_ref[...])
    outs = ([], [])
    for h in range(CT_ROWS // SUB):
        v = jnp.concatenate([_sub_rows(vr, h), _sub_rows(vi, h)], axis=0).astype(BF16)
        y = _dot(fai, v) * scale
        for b, yb in enumerate((y[:half], y[half:])):
            outs[b].append(_sub_rows(gs[b], h) * (yb + skip * _sub_rows(zs[b], h)))
    for b in range(2):
        o_ref[b] = _from_sub_rows(outs[b]).astype(o_ref.dtype)


def _ct_inv_a(vr, vi, fai, z, z_which, gate, gate_which, skip, order, out_dtype):
    n1, n2, d = vr.shape
    rows = fai.shape[0] // (2 * SUB)
    tn2, tw = CT_ROWS, CT_COLS // 2
    spec = pl.BlockSpec((n1, tn2, tw), lambda i, c: (0, i, c))
    pair = lambda which, b: pl.BlockSpec((None, None, rows, tn2, tw),
                                         lambda i, c: (which[0], which[1] + b, 0, i, c))
    return pl.pallas_call(
        functools.partial(_ct_inv_a_kernel, 1.0 / (n1 * n2)),
        grid=(n2 // tn2, d // tw),
        in_specs=[spec, spec, pl.BlockSpec(fai.shape, lambda i, c: (0, 0)),
                  pair(z_which, 0), pair(z_which, 1), pair(gate_which, 0), pair(gate_which, 1),
                  pl.BlockSpec((None, 1, tw), lambda i, c: (order, 0, c))],
        out_specs=pl.BlockSpec((2, rows, tn2, tw), lambda i, c: (0, 0, i, c)),
        out_shape=jax.ShapeDtypeStruct((2, rows, n2, d), out_dtype),
        compiler_params=_params(("parallel", "parallel"),
                                _vmem_limit(2 * _nbytes((n1, tn2, tw), BF16)
                                            + 6 * _nbytes((rows, tn2, tw), F32)
                                            + _nbytes(fai.shape, BF16),
                                            temps=8 * _nbytes((n1, tn2, tw), F32))),
        name="ct_inv_a",
    )(vr, vi, fai, z, z, gate, gate, skip)


def _ct_real_b_kernel(scale, br_ref, bi_ref, fb_ref, o_ref, so_ref):
    fb = fb_ref[...]
    for j in range(br_ref.shape[0]):
        so_ref[:, j, :] = _dot(fb, jnp.concatenate([br_ref[j], bi_ref[j]], axis=0)) * scale
    o_ref[...] = so_ref[...].astype(o_ref.dtype)


def _ct_real_b(br, bi, fb_re, scale):
    n_seq, n1, n2, d = br.shape
    tk1, tw = CT_ROWS, CT_COLS
    blk = pl.BlockSpec((None, tk1, n2, tw), lambda s, i, c: (s, i, 0, c))
    return pl.pallas_call(
        functools.partial(_ct_real_b_kernel, scale),
        grid=(n_seq, n1 // tk1, d // tw),
        in_specs=[blk, blk, pl.BlockSpec(fb_re.shape, lambda s, i, c: (0, 0))],
        out_specs=pl.BlockSpec((None, n2, tk1, tw), lambda s, i, c: (s, 0, i, c)),
        out_shape=jax.ShapeDtypeStruct((n_seq, n2, n1, d), BF16),
        scratch_shapes=[pltpu.VMEM((n2, tk1, tw), F32)],
        compiler_params=_params(("parallel", "parallel", "parallel"),
                                _vmem_limit(3 * _nbytes((tk1, n2, tw), BF16),
                                            resident=_nbytes((n2, tk1, tw), F32),
                                            temps=2 * _nbytes((n2, tk1, tw), F32))),
        name="ct_real_b",
    )(br, bi, fb_re)


def _cos_sin(num, den):
    ang = (num % den).astype(F32) * (2.0 * math.pi / den)
    return jnp.cos(ang), jnp.sin(ang)


def _ct_tables(n1, n2):
    i1 = jnp.arange(n1, dtype=jnp.int32)
    i2 = jnp.arange(n2, dtype=jnp.int32)
    c1, s1 = _cos_sin(i1[:, None] * i1[None, :], n1)
    c2, s2 = _cos_sin(i2[:, None] * i2[None, :], n2)
    cw, sw = _cos_sin(i2[:, None] * i1[None, :], n1 * n2)
    lanes = lambda t: jnp.broadcast_to(t[..., None], t.shape + (TW_LANES,))

    def stage_a_rows(t):
        t = t.reshape(n2 // CT_ROWS, CT_ROWS // SUB, SUB, n1)
        return lanes(jnp.swapaxes(t, 2, 3).reshape(n2 // CT_ROWS, CT_ROWS // SUB, n1 * SUB))

    return {"c1": c1, "s1": s1, "c2": c2, "s2": s2,
            "cw_a": stage_a_rows(cw), "sw_a": stage_a_rows(sw),
            "cw_b": lanes(cw.T), "sw_b": lanes(sw.T)}


def _filter_mlp_kernel(feat_ref, t_ref, w1_ref, b1_ref, fr1_ref, w2_ref, b2_ref, fr2_ref,
                       w3_ref, decay_ref, h_ref, ss_ref):
    x = jnp.sin(fr1_ref[...] * (_dot(feat_ref[...].astype(BF16), w1_ref[...].astype(BF16))
                                + b1_ref[...]))
    x = jnp.sin(fr2_ref[...] * (_dot(x.astype(BF16), w2_ref[...].astype(BF16)) + b2_ref[...]))
    h = _dot(x.astype(BF16), w3_ref[...].astype(BF16))
    h = h * (jnp.exp(-t_ref[...] * jnp.exp(decay_ref[...])) + HY_SHIFT)
    h_ref[...] = h

    @pl.when(pl.program_id(0) == 0)
    def _():
        ss_ref[...] = jnp.zeros_like(ss_ref)

    ss_ref[...] += jnp.sum(h * h, axis=0, keepdims=True)


def _filter_combine_kernel(hf_ref, hb_ref, ssf_ref, ssb_ref, a_ref, b_ref):
    norm = lax.rsqrt(ssf_ref[...] + ssb_ref[...] + 1e-12)
    fwd = hf_ref[...] * norm
    bwd = hb_ref[...] * norm
    row = lax.broadcasted_iota(jnp.int32, bwd.shape, 0) + pl.program_id(0) * bwd.shape[0]
    bwd = jnp.where(row == 0, 0.0, bwd)
    a_ref[...] = fwd + bwd
    b_ref[...] = fwd - bwd


def _hyena_filters(length, f_w1, f_b1, f_freq1, f_w2, f_b2, f_freq2, f_w3, log_decay):
    t = jnp.linspace(0.0, 1.0, length, dtype=F32)[:, None]
    t_idx = jnp.arange(length, dtype=F32)[:, None]
    bands = jnp.linspace(1e-4, HY_BANDS - 1, HY_BANDS, dtype=F32)
    w = 2.0 * math.pi * t_idx * bands / length
    feat = jnp.concatenate([t, jnp.cos(w), -jnp.sin(w)], axis=-1)
    emb_pad = 128
    feat = jnp.pad(feat, ((0, 0), (0, emb_pad - HY_EMB)))
    w1 = jnp.pad(f_w1, ((0, emb_pad - HY_EMB), (0, 0)))
    n_all = HY_DIRS * HY_ORDER * D_MODEL
    n_dir = HY_ORDER * D_MODEL
    tm = 256
    full = lambda shape: pl.BlockSpec(shape, lambda i: (0, 0))
    h, ss = pl.pallas_call(
        _filter_mlp_kernel,
        grid=(length // tm,),
        in_specs=[pl.BlockSpec((tm, emb_pad), lambda i: (i, 0)),
                  pl.BlockSpec((tm, 1), lambda i: (i, 0)),
                  full((emb_pad, HY_FW)), full((1, HY_FW)), full((1, HY_FW)),
                  full((HY_FW, HY_FW)), full((1, HY_FW)), full((1, HY_FW)),
                  full((HY_FW, n_all)), full((1, n_all))],
        out_specs=[pl.BlockSpec((tm, n_all), lambda i: (i, 0)), full((1, n_all))],
        out_shape=[jax.ShapeDtypeStruct((length, n_all), F32),
                   jax.ShapeDtypeStruct((1, n_all), F32)],
        compiler_params=_params(("arbitrary",),
                                _vmem_limit(_nbytes((tm, n_all), F32) + _nbytes((HY_FW, n_all), F32),
                                            temps=3 * _nbytes((tm, n_all), F32))),
        name="filter_mlp",
    )(feat, t, w1, f_b1.reshape(1, HY_FW), f_freq1.reshape(1, HY_FW), f_w2,
      f_b2.reshape(1, HY_FW), f_freq2.reshape(1, HY_FW), f_w3,
      log_decay.reshape(1, n_all))

    tn = 1024
    nb = n_dir // tn
    comb = jax.ShapeDtypeStruct((length, n_dir), F32)
    a, b = pl.pallas_call(
        _filter_combine_kernel,
        grid=(length // tm, nb),
        in_specs=[pl.BlockSpec((tm, tn), lambda i, j: (i, j)),
                  pl.BlockSpec((tm, tn), lambda i, j: (i, nb + j)),
                  pl.BlockSpec((1, tn), lambda i, j: (0, j)),
                  pl.BlockSpec((1, tn), lambda i, j: (0, nb + j))],
        out_specs=[pl.BlockSpec((tm, tn), lambda i, j: (i, j)),
                   pl.BlockSpec((tm, tn), lambda i, j: (i, j))],
        out_shape=[comb, comb],
        compiler_params=_params(("parallel", "parallel"), VMEM_FLOOR_BYTES),
        name="filter_combine",
    )(h, h, ss, ss)
    return a, b


def _cis_product(row_hi, row_lo, period):
    def cis(phase):
        ang = (phase % period).astype(F32) * (2.0 * math.pi / period)
        return jnp.cos(ang)[:, :, None], jnp.sin(ang)[:, :, None]
    (c1, s1), (c0, s0) = cis(row_hi), cis(row_lo)
    c0, s0 = jnp.swapaxes(c0, 1, 2), jnp.swapaxes(s0, 1, 2)
    rows = row_hi.shape[0]
    return ((c1 * c0 - s1 * s0).reshape(rows, -1), (s1 * c0 + c1 * s0).reshape(rows, -1))


def _odd_dft_tables(length):
    split = 1 << (length.bit_length() // 2)
    r = jnp.arange(length, dtype=jnp.int32)[:, None]
    hi = jnp.arange(length // split, dtype=jnp.int32)[None, :] * split
    lo = jnp.arange(split, dtype=jnp.int32)[None, :]
    c, s = _cis_product((2 * r + 1) * hi, (2 * r + 1) * lo, 4 * length)
    ct, st = _cis_product(r * (2 * hi), r * (2 * lo + 1), 4 * length)
    return {"c": c.astype(BF16), "s": s.astype(BF16), "ct": ct.astype(BF16), "st": st.astype(BF16)}


def _dft_tables(length):
    split = 1 << (length.bit_length() // 2)
    r = jnp.arange(length, dtype=jnp.int32)[:, None]
    hi = jnp.arange(length // split, dtype=jnp.int32)[None, :] * split
    lo = jnp.arange(split, dtype=jnp.int32)[None, :]
    c, s = _cis_product(r * hi, r * lo, length)
    return c.astype(BF16), (-s).astype(BF16)


def _hyena_mix_dense(pc, filt_p, filt_m, skip, n_seq, length):
    tabs = _odd_dft_tables(length)
    kr, ks = _mm(tabs["c"], filt_p), _mm(tabs["s"], filt_m)
    yr, ys = _dft_fwd(tabs, pc, 0, kr, ks, 0, n_seq, length)
    z1 = _dft_inv(tabs, yr, ys, pc, 0, pc, 1, skip, 0, n_seq, length, F32)[None]
    yr, ys = _dft_fwd(tabs, z1, 0, kr, ks, 1, n_seq, length)
    return _dft_inv(tabs, yr, ys, z1, 0, pc, 2, skip, 1, n_seq, length, BF16)


HY_N1, HY_N2 = 64, 128


def _hyena_mix_pair(pc, filt_p, filt_m, skip):
    assert DEC_BATCH == 2 and HY_N1 * HY_N2 == 2 * DEC_SEQ and ROWS % DEC_SEQ == 0
    n1, n2, rows_in = HY_N1, HY_N2, HY_N1 // 2
    t = _ct_tables(n1, n2)
    c_in, s_in = t["c1"][:, :rows_in], t["s1"][:, :rows_in]
    fa = _kron_sub(jnp.block([[c_in, s_in], [-s_in, c_in]])).astype(BF16)
    fa_real = _kron_sub(jnp.concatenate([c_in, -s_in], axis=0)).astype(BF16)
    fb = jnp.block([[t["c2"], t["s2"]], [-t["s2"], t["c2"]]]).astype(BF16)
    fbi = jnp.block([[t["c2"], -t["s2"]], [t["s2"], t["c2"]]]).astype(BF16)
    c_out, s_out = t["c1"][:rows_in], t["s1"][:rows_in]
    fai = _kron_sub(jnp.block([[c_out, -s_out], [s_out, c_out]])).astype(BF16)
    n_filt = HY_ORDER * D_MODEL
    filt = []
    for f in (filt_p, filt_m):
        re, im = _ct_stage_a([(f.reshape(rows_in, n2, n_filt), lambda s: ())], 1, fa_real,
                             t["cw_a"], t["sw_a"])
        filt += [re[0], im[0]]

    pc = pc.reshape(3, ROWS // DEC_SEQ, rows_in, n2, D_MODEL)
    z, z0, out_dtypes = pc, ROWS_CTX // DEC_SEQ, (F32, BF16)
    for order in range(HY_ORDER):
        br, bi = _ct_stage_a([(z, lambda s, z0=z0: (0, z0)), (z, lambda s, z0=z0: (0, z0 + 1))], 1, fa,
                             t["cw_a"], t["sw_a"])
        vr, vi = _ct_mid(br[0], bi[0], filt, order, fb, fbi, t["cw_b"], t["sw_b"])
        z = _ct_inv_a(vr, vi, fai, z, (0, z0), pc, (1 + order, ROWS_CTX // DEC_SEQ), skip, order,
                      out_dtypes[order])[None]
        z0 = 0
    return z.reshape(ROWS_LAT, D_MODEL)


def _fnet_chan_kernel(h_ref, sh_ref, sc_ref, w_ref, p_ref, q_ref):
    u = _modulate(h_ref[...], sh_ref[...], sc_ref[...]).astype(BF16)
    w = w_ref[...]
    for g in range(FNET_GROUPS):
        cols = slice(g * FNET_CG, (g + 1) * FNET_CG)
        r = _dot(u[:, cols], w)
        p_ref[:, cols] = r[:, :FNET_CG]
        q_ref[:, cols] = r[:, FNET_CG:]


def _fnet_chan(h, mods, w_cs, tm=512):
    row = pl.BlockSpec((tm, D_MODEL), lambda i: (i, 0))
    out = jax.ShapeDtypeStruct((ROWS, D_MODEL), F32)
    return pl.pallas_call(
        _fnet_chan_kernel,
        grid=(ROWS // tm,),
        in_specs=[row, _mod_spec(0, tm), _mod_spec(1, tm),
                  pl.BlockSpec((FNET_CG, 2 * FNET_CG), lambda i: (0, 0))],
        out_specs=[row, row],
        out_shape=[out, out],
        compiler_params=_params(("parallel",),
                                _vmem_limit(3 * _nbytes((tm, D_MODEL), F32),
                                            temps=2 * _nbytes((tm, D_MODEL), F32))),
        name="fnet_chan",
    )(h, mods, mods, w_cs)


def _fnet_pos_kernel(scale, c_ref, ns_ref, p_ref, q_ref, o_ref, acc_ref):
    k = pl.program_id(3)

    @pl.when(k == 0)
    def _():
        acc_ref[...] = jnp.zeros_like(acc_ref)

    acc_ref[...] += (_dot(c_ref[...], p_ref[...].astype(BF16))
                     + _dot(ns_ref[...], q_ref[...].astype(BF16)))

    @pl.when(k == pl.num_programs(3) - 1)
    def _():
        o_ref[...] = (acc_ref[...] * scale).astype(o_ref.dtype)


def _fnet_pos(c_tab, ns_tab, p, q, n_seq, length):
    tt, tn, tk = _seq_tiles(length)
    nb, nt, nk = D_MODEL // tn, length // tt, length // tk
    scale = (length * FNET_CG) ** -0.5
    return pl.pallas_call(
        functools.partial(_fnet_pos_kernel, scale),
        grid=(n_seq, nt, nb, nk),
        in_specs=[pl.BlockSpec((tt, tk), lambda s, t, n, k: (t, k)),
                  pl.BlockSpec((tt, tk), lambda s, t, n, k: (t, k)),
                  pl.BlockSpec((tk, tn), lambda s, t, n, k: (s * nk + k, n)),
                  pl.BlockSpec((tk, tn), lambda s, t, n, k: (s * nk + k, n))],
        out_specs=pl.BlockSpec((tt, tn), lambda s, t, n, k: (s * nt + t, n)),
        out_shape=jax.ShapeDtypeStruct((n_seq * length, D_MODEL), BF16),
        scratch_shapes=[pltpu.VMEM((tt, tn), F32)],
        compiler_params=_params(("parallel", "parallel", "parallel", "arbitrary"),
                                _vmem_limit(2 * _nbytes((tt, tk), BF16) + 2 * _nbytes((tk, tn), BF16)
                                            + _nbytes((tt, tn), BF16),
                                            resident=_nbytes((tt, tn), F32),
                                            temps=2 * _nbytes((tt, tn), F32))),
        name="fnet_pos",
    )(c_tab, ns_tab, p, q)


FN_N1, FN_N2 = 32, 128


def _fnet_pos_factored(p, q):
    assert FN_N1 * FN_N2 == DEC_SEQ and ROWS % DEC_SEQ == 0
    n1, n2 = FN_N1, FN_N2
    t = _ct_tables(n1, n2)
    fa = _kron_sub(jnp.block([[t["c1"], -t["s1"]], [-t["s1"], -t["c1"]]])).astype(BF16)
    fb_re = jnp.concatenate([t["c2"], t["s2"]], axis=1).astype(BF16)
    lat0 = ROWS_CTX // DEC_SEQ
    view = lambda x: x.reshape(ROWS // DEC_SEQ, n1, n2, D_MODEL)
    seq = lambda s: (lat0 + s,)
    br, bi = _ct_stage_a([(view(p), seq), (view(q), seq)], DEC_BATCH, fa, t["cw_a"], t["sw_a"])
    f = _ct_real_b(br, bi, fb_re, (DEC_SEQ * FNET_CG) ** -0.5)
    return f.reshape(ROWS_LAT, D_MODEL)


def _rope_table():
    rows = DEC_SEQ // GRID_W
    row = jnp.repeat(jnp.arange(rows), GRID_W).astype(F32)
    col = jnp.tile(jnp.arange(GRID_W), rows).astype(F32)
    inv = ROPE_THETA ** (-jnp.arange(0, AXIS_ROPE, 2, dtype=F32) / AXIS_ROPE)
    ang = jnp.concatenate([row[:, None] * inv, col[:, None] * inv], axis=-1)
    cos = jnp.repeat(jnp.cos(ang), 2, axis=-1)
    sin = jnp.repeat(jnp.sin(ang), 2, axis=-1)
    lat = jnp.tile(jnp.concatenate([cos, sin], axis=-1), (DEC_BATCH, 1))
    ctx = jnp.concatenate([jnp.ones((ROWS_CTX, QK_ROPE), F32), jnp.zeros((ROWS_CTX, QK_ROPE), F32)],
                          axis=-1)
    return jnp.concatenate([ctx, lat], axis=0)


def _pair_rotated(w):
    pairs = w.reshape(w.shape[:-1] + (QK_ROPE // 2, 2))
    return jnp.stack([-pairs[..., 1], pairs[..., 0]], axis=-1).reshape(w.shape)


def kernel(x_prompt, x_sample, c, cache_ckv, cache_krope, c_ctx, ada_w, ada_b, ln_g, ln_b, ffn_w_gate, ffn_w_up, ffn_w_down, mla_w_dq, mla_q_norm, mla_w_uq, mla_w_dkv, mla_kv_norm, mla_w_kr, mla_w_ukv, mla_w_o, hy_w_in, hy_b_in, hy_conv_w, hy_conv_b, hy_f_w1, hy_f_b1, hy_f_freq1, hy_f_w2, hy_f_b2, hy_f_freq2, hy_f_w3, hy_log_decay, hy_skip, hy_w_out, hy_b_out, fn_w_out, fn_b_out):
    assert x_prompt.shape == (BATCH, SEQ, D_MODEL) and x_sample.shape == (DEC_BATCH, DEC_SEQ, D_MODEL)
    assert ROWS_CTX % DEC_SEQ == 0 and SEQ == FNET_CG

    assert N_MIXERS > 0 and DEPTH > 1
    h = (x_prompt.reshape(ROWS_CTX, D_MODEL), x_sample.reshape(ROWS_LAT, D_MODEL))
    cond = jnp.concatenate([c_ctx[None, :], c, jnp.zeros((COND_PAD - N_COND, D_MODEL), F32)])
    mods_all = _modulation_vectors(cond, ada_w, ada_b)
    zero_bias = jnp.zeros((D_MODEL,), F32)
    ffn_w = (ffn_w_gate.astype(BF16), ffn_w_up.astype(BF16), ffn_w_down.astype(BF16))
    rope_tab = None
    ckv_states, krope_states = [], []

    for i in range(DEPTH):
        kind, j = i % N_MIXERS, i // N_MIXERS
        mods = mods_all[i]
        if kind == 0:
            if rope_tab is None:
                rope_tab = _rope_table()
            w_kr2 = jnp.concatenate([mla_w_kr[j], _pair_rotated(mla_w_kr[j])], axis=-1).astype(BF16)
            wq = mla_w_uq[j].reshape(Q_RANK, MLA_HEADS, QK_NOPE + QK_ROPE)
            w_q = jnp.concatenate([wq, _pair_rotated(wq[..., QK_NOPE:])], axis=-1)
            w_qt = w_q.reshape(Q_RANK, MLA_HEADS * HEAD_W).T.astype(BF16)
            w_ukv = mla_w_ukv[j].reshape(KV_RANK, MLA_HEADS, QK_NOPE + V_DIM)
            w_k = w_ukv[..., :QK_NOPE].reshape(KV_RANK, MLA_HEADS * QK_NOPE).astype(BF16)
            w_vt = w_ukv[..., QK_NOPE:].reshape(KV_RANK, MLA_HEADS * V_DIM).T.astype(BF16)
            cq, ckv, kr, kr2 = _mla_down(h, mods, mla_w_dq[j].astype(BF16), mla_w_dkv[j].astype(BF16),
                                         w_kr2, mla_q_norm[j], mla_kv_norm[j], rope_tab)
            ckv_states.append(ckv[:ROWS_CTX].reshape(BATCH, SEQ, KV_RANK))
            krope_states.append(kr[:ROWS_CTX].reshape(BATCH, SEQ, QK_ROPE))
            q = _q_up(cq, w_qt, rope_tab.T)
            k_tok, vt_tok = _kv_expand(ckv, kr2, w_k, w_vt)
            kc = cache_krope[:, j].reshape(DEC_BATCH * PAST_LEN, QK_ROPE).astype(BF16)
            k_cache, vt_cache = _kv_expand(cache_ckv[:, j].reshape(DEC_BATCH * PAST_LEN, KV_RANK),
                                           jnp.concatenate([kc, kc], axis=-1), w_k, w_vt)
            o = _attention(q, k_tok, vt_tok, k_cache, vt_cache)
            h = _mm_postnorm(o, mla_w_o[j].astype(BF16), zero_bias, h, mods, 2, ln_g[i, 0], ln_b[i, 0])
        elif kind == 1:
            pc = _hyena_in(h, mods, hy_w_in[j].astype(BF16), hy_b_in[j], hy_conv_w[j], hy_conv_b[j])
            fp = (hy_f_w1[j], hy_f_b1[j], hy_f_freq1[j], hy_f_w2[j], hy_f_b2[j], hy_f_freq2[j],
                  hy_f_w3[j], hy_log_decay[j])
            skip = hy_skip[j].reshape(HY_ORDER, 1, D_MODEL)
            z_ctx = _hyena_mix_dense(pc, *_hyena_filters(SEQ, *fp), skip, BATCH, SEQ)
            z_lat = _hyena_mix_pair(pc, *_hyena_filters(DEC_SEQ, *fp), skip)
            h = _mm_postnorm((z_ctx, z_lat), hy_w_out[j].astype(BF16), hy_b_out[j],
                             h, mods, 2, ln_g[i, 0], ln_b[i, 0])
        else:
            c_ch, ns_ch = _dft_tables(FNET_CG)
            p, q = _fnet_chan(h, mods, jnp.concatenate([c_ch, -ns_ch], axis=-1))
            f_ctx = _fnet_pos(c_ch, ns_ch, p, q, BATCH, SEQ)
            f_lat = _fnet_pos_factored(p, q)
            h = _mm_postnorm((f_ctx, f_lat), fn_w_out[j].astype(BF16), fn_b_out[j],
                             h, mods, 2, ln_g[i, 0], ln_b[i, 0])
        h = _ffn(h, mods, *ffn_w, i, ln_g[i, 1], ln_b[i, 1], split_out=i == DEPTH - 1)

    y_prompt = h[0].reshape(BATCH, SEQ, D_MODEL)
    y_sample = h[1].reshape(DEC_BATCH, DEC_SEQ, D_MODEL)
    return (y_prompt, y_sample, jnp.stack(ckv_states, axis=1), jnp.stack(krope_states, axis=1))
```

```python
import functools
import math

import jax
import jax.numpy as jnp
from jax import lax
from jax.experimental import pallas as pl
from jax.experimental.pallas import tpu as pltpu

F32 = jnp.float32
BF16 = jnp.bfloat16

D_MODEL = 2048
BATCH = 16
SEQ = 256
DEPTH = 4
DEC_BATCH = 2
DEC_SEQ = 4096
PAST_LEN = 512
GRID_W = 64
N_MIXERS = 3
MLA_HEADS = 16
QK_NOPE = 128
QK_ROPE = 64
V_DIM = 128
Q_RANK = 512
KV_RANK = 512
ROPE_THETA = 10000.0
AXIS_ROPE = QK_ROPE // 2
HY_ORDER = 2
HY_DIRS = 2
HY_CONV = 3
HY_BANDS = 16
HY_EMB = 1 + 2 * HY_BANDS
HY_FW = 64
HY_SHIFT = 0.05
FNET_GROUPS = 8
FNET_CG = D_MODEL // FNET_GROUPS
D_FF = -(-8 * D_MODEL // (3 * 256)) * 256
DN_ALPHA = (2 * DEPTH) ** 0.25
LN_EPS = 1e-5
RMS_EPS = 1e-6
N_MOD = 6

ROWS_CTX = BATCH * SEQ
ROWS_LAT = DEC_BATCH * DEC_SEQ
ROWS = ROWS_CTX + ROWS_LAT
N_COND = 1 + DEC_BATCH
COND_PAD = 8
HEAD_W = QK_NOPE + 2 * QK_ROPE
ATT_SCALE = (QK_NOPE + QK_ROPE) ** -0.5

V7X_VMEM_BYTES = 64 * 2 ** 20
VMEM_CAP_BYTES = V7X_VMEM_BYTES * 7 // 8
VMEM_FLOOR_BYTES = 32 * 2 ** 20


def _vmem_limit(pipelined, resident=0, temps=0):
    est = 2 * pipelined + resident + temps
    return int(min(max(est, VMEM_FLOOR_BYTES), VMEM_CAP_BYTES))


def _params(semantics, vmem):
    return pltpu.CompilerParams(dimension_semantics=semantics, vmem_limit_bytes=vmem)


def _nbytes(shape, dtype):
    return math.prod(shape) * jnp.dtype(dtype).itemsize


def _group_of_tile(i, tm):
    n_ctx = ROWS_CTX // tm
    return jnp.where(i < n_ctx, 0, 1 + (i - n_ctx) // (DEC_SEQ // tm))


def _mod_spec(which, tm):
    return pl.BlockSpec((None, 1, D_MODEL),
                        lambda i, *_: (which * COND_PAD + _group_of_tile(i, tm), 0, 0))


def _row_spec(width=D_MODEL):
    return pl.BlockSpec((1, width), lambda *_: (0, 0))


def _as_tuple(x):
    return x if isinstance(x, tuple) else (x,)


def _token_specs(xs, tm):
    width = xs[0].shape[1]
    if len(xs) == 1:
        return [pl.BlockSpec((tm, width), lambda i, *_: (i, 0))]
    n_ctx = ROWS_CTX // tm
    return [pl.BlockSpec((tm, width), lambda i, *_: (jnp.minimum(i, n_ctx - 1), 0)),
            pl.BlockSpec((tm, width), lambda i, *_: (jnp.maximum(i - n_ctx, 0), 0))]


def _token_rows(refs, r):
    if len(refs) == 1:
        return refs[0][r, :]
    n_ctx = ROWS_CTX // refs[0].shape[0]
    return jnp.where(pl.program_id(0) < n_ctx, refs[0][r, :], refs[1][r, :])


def _modulate(h, shift, scale):
    return h * (1.0 + scale) + shift


def _post_norm(h, delta, g, b):
    z = DN_ALPHA * h + delta
    mu = jnp.mean(z, axis=-1, keepdims=True)
    zc = z - mu
    var = jnp.mean(zc * zc, axis=-1, keepdims=True)
    return zc * lax.rsqrt(var + LN_EPS) * g + b


def _rms_norm(x, g):
    ms = jnp.mean(x * x, axis=-1, keepdims=True)
    return x * lax.rsqrt(ms + RMS_EPS) * g


def _dot(a, b):
    return jnp.dot(a, b, preferred_element_type=F32)


def _modvec_kernel(c_ref, w_ref, b_ref, o_ref):
    a = jax.nn.silu(c_ref[...]).astype(BF16)
    o_ref[...] = _dot(a, w_ref[...].astype(BF16)) + b_ref[...]


def _modulation_vectors(cond, ada_w, ada_b):
    tn = 1024
    n = N_MOD * D_MODEL
    out = pl.pallas_call(
        _modvec_kernel,
        grid=(DEPTH, n // tn),
        in_specs=[pl.BlockSpec((COND_PAD, D_MODEL), lambda l, j: (0, 0)),
                  pl.BlockSpec((None, D_MODEL, tn), lambda l, j: (l, 0, j)),
                  pl.BlockSpec((None, 1, tn), lambda l, j: (l, 0, j))],
        out_specs=pl.BlockSpec((None, COND_PAD, tn), lambda l, j: (l, 0, j)),
        out_shape=jax.ShapeDtypeStruct((DEPTH, COND_PAD, n), F32),
        compiler_params=_params(("parallel", "parallel"),
                                _vmem_limit(_nbytes((D_MODEL, tn), F32),
                                            temps=_nbytes((D_MODEL, tn), BF16))),
        name="modvec",
    )(cond, ada_w, ada_b.reshape(DEPTH, 1, n))
    out = out.reshape(DEPTH, COND_PAD, N_MOD, D_MODEL).transpose(0, 2, 1, 3)
    return out.reshape(DEPTH, N_MOD * COND_PAD, 1, D_MODEL)


def _hyena_in_kernel(n_ctx, h_ref, hp_ref, hn_ref, sh_ref, sc_ref, w_ref, b_ref, cw_ref, cb_ref,
                     o_ref, u_ref):
    i = pl.program_id(0)
    tm = h_ref.shape[0]

    @pl.when(pl.program_id(1) == 0)
    def _():
        mod = lambda x_ref: _modulate(x_ref[...], sh_ref[...], sc_ref[...]).astype(BF16)
        u_ref[:SUB, :] = mod(hp_ref)
        u_ref[SUB:SUB + tm, :] = mod(h_ref)
        u_ref[SUB + tm:, :] = mod(hn_ref)

    length = jnp.where(i < n_ctx, SEQ, DEC_SEQ)
    half = tm // 2
    starts = (0, half)
    ys = [_dot(u_ref[r0:r0 + half + 2 * SUB, :], w_ref[...]) + b_ref[...] for r0 in starts]
    for r0, y in zip(starts, ys):
        inner = slice(SUB, SUB + half)
        before = pltpu.roll(y, 1, 0)[inner]
        after = pltpu.roll(y, y.shape[0] - 1, 0)[inner]
        pos = (lax.broadcasted_iota(jnp.int32, before.shape, 0) + (i * tm + r0)) & (length - 1)
        before = jnp.where(pos == 0, 0.0, before)
        after = jnp.where(pos == length - 1, 0.0, after)
        o_ref[r0:r0 + half, :] = (cw_ref[0:1, :] * before + cw_ref[1:2, :] * y[inner]
                                  + cw_ref[2:3, :] * after + cb_ref[...])


def _hyena_in(h, mods, w, b, conv_w, conv_b, tm=1024, tn=1024):
    assert SEQ & (SEQ - 1) == 0 and DEC_SEQ & (DEC_SEQ - 1) == 0 and tm % SEQ == 0 and DEC_SEQ % tm == 0
    k, n = w.shape
    per, halo, last = D_MODEL // tn, tm // SUB, ROWS // SUB - 1
    return pl.pallas_call(
        functools.partial(_hyena_in_kernel, ROWS_CTX // tm),
        grid=(ROWS // tm, n // tn),
        in_specs=[pl.BlockSpec((tm, k), lambda i, j: (i, 0)),
                  pl.BlockSpec((SUB, k), lambda i, j: (jnp.maximum(i * halo - 1, 0), 0)),
                  pl.BlockSpec((SUB, k), lambda i, j: (jnp.minimum((i + 1) * halo, last), 0)),
                  _mod_spec(0, tm), _mod_spec(1, tm),
                  pl.BlockSpec((k, tn), lambda i, j: (0, j)),
                  pl.BlockSpec((1, tn), lambda i, j: (0, j)),
                  pl.BlockSpec((HY_CONV, tn), lambda i, j: (0, j)),
                  pl.BlockSpec((1, tn), lambda i, j: (0, j))],
        out_specs=pl.BlockSpec((None, tm, tn), lambda i, j: (j // per, i, j % per)),
        out_shape=jax.ShapeDtypeStruct((n // D_MODEL, ROWS, D_MODEL), F32),
        scratch_shapes=[pltpu.VMEM((tm + 2 * SUB, k), BF16)],
        compiler_params=_params(("parallel", "arbitrary"),
                                _vmem_limit(_nbytes((tm, k), F32) + _nbytes((k, tn), BF16)
                                            + _nbytes((tm, tn), F32),
                                            resident=_nbytes((tm, k), BF16),
                                            temps=5 * _nbytes((tm, tn), F32))),
        name="hyena_in",
    )(h, h, h, mods, mods, w, b.reshape(1, n), conv_w, conv_b.reshape(1, n))


def _mm_postnorm_kernel(n_a, n_h, *refs):
    a_refs, refs = refs[:n_a], refs[n_a:]
    (w_ref, bias_ref), refs = refs[:2], refs[2:]
    h_refs, (gate_ref, g_ref, b_ref, o_ref) = refs[:n_h], refs[n_h:]
    half = o_ref.shape[0] // 2
    halves = (slice(0, half), slice(half, 2 * half))
    ys = [_dot(_token_rows(a_refs, r), w_ref[...]) + bias_ref[...] for r in halves]
    for r, y in zip(halves, ys):
        o_ref[r, :] = _post_norm(_token_rows(h_refs, r), gate_ref[...] * y, g_ref[...], b_ref[...])


def _mm_postnorm(a, w, bias, h, mods, which_gate, ln_g, ln_b, tm=512):
    a, h = _as_tuple(a), _as_tuple(h)
    k = a[0].shape[1]
    return pl.pallas_call(
        functools.partial(_mm_postnorm_kernel, len(a), len(h)),
        grid=(ROWS // tm,),
        in_specs=(_token_specs(a, tm) + [pl.BlockSpec((k, D_MODEL), lambda i: (0, 0)), _row_spec()]
                  + _token_specs(h, tm) + [_mod_spec(which_gate, tm), _row_spec(), _row_spec()]),
        out_specs=pl.BlockSpec((tm, D_MODEL), lambda i: (i, 0)),
        out_shape=jax.ShapeDtypeStruct((ROWS, D_MODEL), F32),
        compiler_params=_params(("parallel",),
                                _vmem_limit(len(a) * _nbytes((tm, k), BF16)
                                            + _nbytes((k, D_MODEL), BF16)
                                            + (1 + len(h)) * _nbytes((tm, D_MODEL), F32),
                                            temps=3 * _nbytes((tm, D_MODEL), F32))),
        name="mm_postnorm",
    )(*a, w, bias.reshape(1, D_MODEL), *h, mods, ln_g.reshape(1, D_MODEL), ln_b.reshape(1, D_MODEL))


def _ffn_kernel(n_ctx, h_ref, sh_ref, sc_ref, gate_ref, g_ref, b_ref, wg_ref, wu_ref, wd_ref,
                *refs):
    i, f = pl.program_id(0), pl.program_id(1)
    if n_ctx is None:
        o_ref, u_ref = refs
        acc_ref, outs = o_ref, ((o_ref, None),)
    else:
        octx_ref, olat_ref, u_ref, acc_ref = refs
        outs = ((octx_ref, i < n_ctx), (olat_ref, i >= n_ctx))

    @pl.when(f == 0)
    def _():
        u_ref[...] = _modulate(h_ref[...], sh_ref[...], sc_ref[...]).astype(BF16)
        acc_ref[...] = jnp.zeros_like(acc_ref)

    half = u_ref.shape[0] // 2
    halves = (slice(0, half), slice(half, 2 * half))
    proj = [(_dot(u_ref[r, :], wg_ref[...]), _dot(u_ref[r, :], wu_ref[...])) for r in halves]
    for r, (gate, up) in zip(halves, proj):
        act = (jax.nn.silu(gate) * up).astype(BF16)
        acc_ref[r, :] += _dot(act, wd_ref[...])

    last = f == pl.num_programs(1) - 1
    for o_ref, mine in outs:
        @pl.when(last if mine is None else jnp.logical_and(last, mine))
        def _(o_ref=o_ref):
            for r in halves:
                o_ref[r, :] = _post_norm(h_ref[r, :], gate_ref[...] * acc_ref[r, :],
                                         g_ref[...], b_ref[...])


def _ffn(h, mods, w_gate, w_up, w_down, layer, ln_g, ln_b, split_out, tm=512, tf=512):
    tile = _nbytes((tm, D_MODEL), F32)
    if split_out:
        n_ctx = ROWS_CTX // tm
        out_specs = [pl.BlockSpec((tm, D_MODEL), lambda i, f: (jnp.minimum(i, n_ctx - 1), 0)),
                     pl.BlockSpec((tm, D_MODEL), lambda i, f: (jnp.maximum(i - n_ctx, 0), 0))]
        out_shape = [jax.ShapeDtypeStruct((ROWS_CTX, D_MODEL), F32),
                     jax.ShapeDtypeStruct((ROWS_LAT, D_MODEL), F32)]
        scratch = [pltpu.VMEM((tm, D_MODEL), BF16), pltpu.VMEM((tm, D_MODEL), F32)]
        pipelined, resident = 3 * tile, tile + tile // 2
    else:
        n_ctx = None
        out_specs = pl.BlockSpec((tm, D_MODEL), lambda i, f: (i, 0))
        out_shape = jax.ShapeDtypeStruct((ROWS, D_MODEL), F32)
        scratch = [pltpu.VMEM((tm, D_MODEL), BF16)]
        pipelined, resident = 2 * tile, tile // 2
    return pl.pallas_call(
        functools.partial(_ffn_kernel, n_ctx),
        grid=(ROWS // tm, D_FF // tf),
        in_specs=[pl.BlockSpec((tm, D_MODEL), lambda i, f: (i, 0)),
                  _mod_spec(3, tm), _mod_spec(4, tm), _mod_spec(5, tm),
                  _row_spec(), _row_spec(),
                  pl.BlockSpec((None, D_MODEL, tf), lambda i, f: (layer, 0, f)),
                  pl.BlockSpec((None, D_MODEL, tf), lambda i, f: (layer, 0, f)),
                  pl.BlockSpec((None, tf, D_MODEL), lambda i, f: (layer, f, 0))],
        out_specs=out_specs,
        out_shape=out_shape,
        scratch_shapes=scratch,
        compiler_params=_params(("arbitrary" if split_out else "parallel", "arbitrary"),
                                _vmem_limit(pipelined + 3 * _nbytes((D_MODEL, tf), BF16),
                                            resident=resident, temps=2 * tile)),
        name="ffn",
    )(h, mods, mods, mods, ln_g.reshape(1, D_MODEL), ln_b.reshape(1, D_MODEL),
      w_gate, w_up, w_down)


def _mla_down_kernel(n_h, *refs):
    h_refs, (sh_ref, sc_ref, wdq_ref, wdkv_ref, wkr_ref, qn_ref, kvn_ref, rope_ref,
             cq_ref, ckv_ref, kr_ref, kr2_ref) = refs[:n_h], refs[n_h:]
    u = _modulate(_token_rows(h_refs, slice(None)), sh_ref[...], sc_ref[...]).astype(BF16)
    cq_ref[...] = _rms_norm(_dot(u, wdq_ref[...]), qn_ref[...]).astype(BF16)
    ckv_ref[...] = _rms_norm(_dot(u, wdkv_ref[...]), kvn_ref[...])
    t = _dot(u, wkr_ref[...])
    kr_ref[...] = t[:, :QK_ROPE]
    v = t * rope_ref[...]
    kr2_ref[...] = (v + pltpu.roll(v, QK_ROPE, 1)).astype(BF16)


def _mla_down(h, mods, w_dq, w_dkv, w_kr2, q_norm, kv_norm, rope_tab, tm=512):
    h = _as_tuple(h)
    row = lambda width: pl.BlockSpec((tm, width), lambda i: (i, 0))
    full = lambda shape: pl.BlockSpec(shape, lambda i: (0, 0))
    return pl.pallas_call(
        functools.partial(_mla_down_kernel, len(h)),
        grid=(ROWS // tm,),
        in_specs=_token_specs(h, tm) + [
            _mod_spec(0, tm), _mod_spec(1, tm),
            full((D_MODEL, Q_RANK)), full((D_MODEL, KV_RANK)), full((D_MODEL, 2 * QK_ROPE)),
            _row_spec(Q_RANK), _row_spec(KV_RANK), row(2 * QK_ROPE)],
        out_specs=[row(Q_RANK), row(KV_RANK), row(QK_ROPE), row(2 * QK_ROPE)],
        out_shape=[jax.ShapeDtypeStruct((ROWS, Q_RANK), BF16),
                   jax.ShapeDtypeStruct((ROWS, KV_RANK), F32),
                   jax.ShapeDtypeStruct((ROWS, QK_ROPE), F32),
                   jax.ShapeDtypeStruct((ROWS, 2 * QK_ROPE), BF16)],
        compiler_params=_params(("parallel",),
                                _vmem_limit(len(h) * _nbytes((tm, D_MODEL), F32)
                                            + _nbytes((D_MODEL, Q_RANK + KV_RANK + 2 * QK_ROPE), BF16)
                                            + 3 * _nbytes((tm, KV_RANK), F32),
                                            temps=2 * _nbytes((tm, D_MODEL), F32))),
        name="mla_down",
    )(*h, mods, mods, w_dq, w_dkv, w_kr2, q_norm.reshape(1, Q_RANK), kv_norm.reshape(1, KV_RANK),
      rope_tab)


NT_DIMS = (((1,), (1,)), ((), ()))


def _q_up_kernel(cq_ref, w_ref, rope_ref, q_ref):
    cq, tab = cq_ref[...], rope_ref[...] * LOG2E_SCALE
    for h in range(MLA_HEADS):
        r = _dot(cq, w_ref[:, h * HEAD_W:(h + 1) * HEAD_W])
        q_ref[h, :, :QK_NOPE] = (r[:, :QK_NOPE] * LOG2E_SCALE).astype(BF16)
        q_ref[h, :, QK_NOPE:] = (r[:, QK_NOPE:] * tab).astype(BF16)


def _q_up(cq, w_q, rope_tab, tm=512):
    return pl.pallas_call(
        _q_up_kernel,
        grid=(ROWS // tm,),
        in_specs=[pl.BlockSpec((tm, Q_RANK), lambda i: (i, 0)),
                  pl.BlockSpec((Q_RANK, MLA_HEADS * HEAD_W), lambda i: (0, 0)),
                  pl.BlockSpec((tm, 2 * QK_ROPE), lambda i: (i, 0))],
        out_specs=pl.BlockSpec((MLA_HEADS, tm, HEAD_W), lambda i: (0, i, 0)),
        out_shape=jax.ShapeDtypeStruct((MLA_HEADS, ROWS, HEAD_W), BF16),
        compiler_params=_params(("parallel",),
                                _vmem_limit(_nbytes((Q_RANK + tm, MLA_HEADS * HEAD_W), BF16))),
        name="q_up",
    )(cq, w_q, rope_tab)


VT_PAD = 16
VT_ROWS = V_DIM + VT_PAD


def _kv_expand_kernel(ckv_ref, kr2_ref, wk_ref, wvt_ref, k_ref, vt_ref):
    c = ckv_ref[...].astype(BF16)
    vt = lax.dot_general(wvt_ref[...], c, NT_DIMS, preferred_element_type=F32).astype(BF16)
    ones = jnp.ones((VT_PAD, vt.shape[1]), BF16)
    for h in range(MLA_HEADS):
        vt_ref[h * VT_ROWS:h * VT_ROWS + V_DIM, :] = vt[h * V_DIM:(h + 1) * V_DIM]
        vt_ref[h * VT_ROWS + V_DIM:(h + 1) * VT_ROWS, :] = ones
    kr2 = kr2_ref[...]
    pair_w = 2 * QK_NOPE
    for g in range(MLA_HEADS // 2):
        r = _dot(c, wk_ref[:, g * pair_w:(g + 1) * pair_w]).astype(BF16)
        for e in range(2):
            k_ref[2 * g + e, :, :QK_NOPE] = r[:, e * QK_NOPE:(e + 1) * QK_NOPE]
            k_ref[2 * g + e, :, QK_NOPE:] = kr2


def _kv_expand(ckv, kr2, w_k, w_vt, tm=512):
    rows = ckv.shape[0]
    return pl.pallas_call(
        _kv_expand_kernel,
        grid=(rows // tm,),
        in_specs=[pl.BlockSpec((tm, KV_RANK), lambda i: (i, 0)),
                  pl.BlockSpec((tm, 2 * QK_ROPE), lambda i: (i, 0)),
                  pl.BlockSpec((KV_RANK, MLA_HEADS * QK_NOPE), lambda i: (0, 0)),
                  pl.BlockSpec((MLA_HEADS * V_DIM, KV_RANK), lambda i: (0, 0))],
        out_specs=[pl.BlockSpec((MLA_HEADS, tm, HEAD_W), lambda i: (0, i, 0)),
                   pl.BlockSpec((MLA_HEADS * VT_ROWS, tm), lambda i: (0, i))],
        out_shape=[jax.ShapeDtypeStruct((MLA_HEADS, rows, HEAD_W), BF16),
                   jax.ShapeDtypeStruct((MLA_HEADS * VT_ROWS, rows), BF16)],
        compiler_params=_params(("parallel",),
                                _vmem_limit(_nbytes((tm, MLA_HEADS * (HEAD_W + V_DIM)), BF16)
                                            + 2 * _nbytes((KV_RANK, MLA_HEADS * V_DIM), BF16),
                                            temps=_nbytes((MLA_HEADS * V_DIM, tm), F32))),
        name="kv_expand",
    )(ckv, kr2, w_k, w_vt)


ATT_CHUNK = 512
ATT_SKEW = 3
LOG2E_SCALE = ATT_SCALE * math.log2(math.e)


def _attn_scores(q, k):
    return lax.dot_general(k, q, NT_DIMS, preferred_element_type=F32)


def _attn_values(s, vt, carry):
    m = jnp.max(s, axis=0, keepdims=True)
    if carry is not None:
        m_old, acc_old = carry
        m = jnp.maximum(m_old, m)
    acc = _dot(vt, jnp.exp2(s - m).astype(BF16))
    if carry is not None:
        acc = jnp.exp2(m_old - m) * acc_old + acc
    return m, acc


def _attn_output(acc):
    return (acc[:V_DIM] / acc[V_DIM:V_DIM + 1]).T.astype(BF16)


def _attn_ctx_kernel(q_ref, k_ref, vt_ref, o_ref):
    for h in range(MLA_HEADS):
        _, acc = _attn_values(_attn_scores(q_ref[h], k_ref[h]),
                              vt_ref[h * VT_ROWS:(h + 1) * VT_ROWS, :], None)
        o_ref[:, h * V_DIM:(h + 1) * V_DIM] = _attn_output(acc)


def _attn_lat_kernel(q_ref, k_ref, vt_ref, kc_ref, vtc_ref, prev_ref, o_ref):
    del prev_ref
    q = q_ref[...]
    n_tok = DEC_SEQ // ATT_CHUNK
    rows = lambda c: slice(c * ATT_CHUNK, (c + 1) * ATT_CHUNK)
    keys = [k_ref.at[rows(c), :] for c in range(n_tok)] + [kc_ref]
    vals = [vt_ref.at[:, rows(c)] for c in range(n_tok)] + [vtc_ref]
    carry = None
    scores = [_attn_scores(q, keys[c][...]) for c in range(ATT_SKEW)]
    for c in range(n_tok + 1):
        if c + ATT_SKEW <= n_tok:
            scores.append(_attn_scores(q, keys[c + ATT_SKEW][...]))
        carry = _attn_values(scores[c], vals[c][...], carry)
    o_ref[...] = _attn_output(carry[1])


def _attention(q, k_tok, vt_tok, k_cache, vt_cache, tq=2048):
    assert DEC_SEQ % ATT_CHUNK == 0
    out_shape = jax.ShapeDtypeStruct((ROWS, MLA_HEADS * V_DIM), BF16)
    o = pl.pallas_call(
        _attn_ctx_kernel,
        grid=(BATCH,),
        in_specs=[pl.BlockSpec((MLA_HEADS, SEQ, HEAD_W), lambda s: (0, s, 0)),
                  pl.BlockSpec((MLA_HEADS, SEQ, HEAD_W), lambda s: (0, s, 0)),
                  pl.BlockSpec((MLA_HEADS * VT_ROWS, SEQ), lambda s: (0, s))],
        out_specs=pl.BlockSpec((SEQ, MLA_HEADS * V_DIM), lambda s: (s, 0)),
        out_shape=out_shape,
        compiler_params=_params(("parallel",), VMEM_FLOOR_BYTES),
        name="attn_ctx",
    )(q, k_tok, vt_tok)

    lat0 = ROWS_CTX // DEC_SEQ
    q0 = ROWS_CTX // tq
    nq = DEC_SEQ // tq
    return pl.pallas_call(
        _attn_lat_kernel,
        grid=(DEC_BATCH, MLA_HEADS, nq),
        in_specs=[pl.BlockSpec((None, tq, HEAD_W), lambda b, h, i: (h, q0 + b * nq + i, 0)),
                  pl.BlockSpec((None, DEC_SEQ, HEAD_W), lambda b, h, i: (h, lat0 + b, 0)),
                  pl.BlockSpec((VT_ROWS, DEC_SEQ), lambda b, h, i: (h, lat0 + b)),
                  pl.BlockSpec((None, PAST_LEN, HEAD_W), lambda b, h, i: (h, b, 0)),
                  pl.BlockSpec((VT_ROWS, PAST_LEN), lambda b, h, i: (h, b)),
                  pl.BlockSpec(memory_space=pl.ANY)],
        out_specs=pl.BlockSpec((tq, V_DIM), lambda b, h, i: (q0 + b * nq + i, h)),
        out_shape=out_shape,
        input_output_aliases={5: 0},
        compiler_params=_params(("parallel", "parallel", "arbitrary"),
                                _vmem_limit(_nbytes((DEC_SEQ + PAST_LEN, HEAD_W + V_DIM), BF16),
                                            temps=8 * _nbytes((ATT_CHUNK, tq), F32))),
        name="attn_lat",
    )(q, k_tok, vt_tok, k_cache, vt_cache, o)


def _mm_kernel(a_ref, b_ref, o_ref, acc_ref):
    k = pl.program_id(2)

    @pl.when(k == 0)
    def _():
        acc_ref[...] = jnp.zeros_like(acc_ref)

    acc_ref[...] += _dot(a_ref[...], b_ref[...].astype(BF16))

    @pl.when(k == pl.num_programs(2) - 1)
    def _():
        o_ref[...] = acc_ref[...].astype(o_ref.dtype)


def _mm(a, b, out_dtype=F32, tm=1024, tn=1024, tk=512):
    m, kk = a.shape
    n = b.shape[1]
    tm, tn, tk = min(tm, m), min(tn, n), min(tk, kk)
    return pl.pallas_call(
        _mm_kernel,
        grid=(m // tm, n // tn, kk // tk),
        in_specs=[pl.BlockSpec((tm, tk), lambda i, j, k: (i, k)),
                  pl.BlockSpec((tk, tn), lambda i, j, k: (k, j))],
        out_specs=pl.BlockSpec((tm, tn), lambda i, j, k: (i, j)),
        out_shape=jax.ShapeDtypeStruct((m, n), out_dtype),
        scratch_shapes=[pltpu.VMEM((tm, tn), F32)],
        compiler_params=_params(("parallel", "parallel", "arbitrary"), VMEM_FLOOR_BYTES),
        name="mm",
    )(a, b)


def _seq_tiles(length):
    if length >= 1024:
        return 1024, 512, 1024
    return length, D_MODEL, length


def _dft_fwd_kernel(c_ref, s_ref, z_ref, kr_ref, ks_ref, yr_ref, ys_ref, accr_ref, accs_ref):
    k = pl.program_id(3)

    @pl.when(k == 0)
    def _():
        accr_ref[...] = jnp.zeros_like(accr_ref)
        accs_ref[...] = jnp.zeros_like(accs_ref)

    z = z_ref[...].astype(BF16)
    accr_ref[...] += _dot(c_ref[...], z)
    accs_ref[...] += _dot(s_ref[...], z)

    @pl.when(k == pl.num_programs(3) - 1)
    def _():
        zr, zs, kr, ks = accr_ref[...], accs_ref[...], kr_ref[...], ks_ref[...]
        yr_ref[...] = (zr * kr - zs * ks).astype(BF16)
        ys_ref[...] = (zr * ks + zs * kr).astype(BF16)


def _dft_fwd(tabs, z, z_which, kr, ks, order, n_seq, length):
    tf, tn, tk = _seq_tiles(length)
    nb, nf, nk = D_MODEL // tn, length // tf, length // tk
    kc0 = order * nb
    out = jax.ShapeDtypeStruct((n_seq * length, D_MODEL), BF16)
    return pl.pallas_call(
        _dft_fwd_kernel,
        grid=(n_seq, nf, nb, nk),
        in_specs=[pl.BlockSpec((tf, tk), lambda s, f, n, k: (f, k)),
                  pl.BlockSpec((tf, tk), lambda s, f, n, k: (f, k)),
                  pl.BlockSpec((None, tk, tn), lambda s, f, n, k: (z_which, s * nk + k, n)),
                  pl.BlockSpec((tf, tn), lambda s, f, n, k: (f, kc0 + n)),
                  pl.BlockSpec((tf, tn), lambda s, f, n, k: (f, kc0 + n))],
        out_specs=[pl.BlockSpec((tf, tn), lambda s, f, n, k: (s * nf + f, n)),
                   pl.BlockSpec((tf, tn), lambda s, f, n, k: (s * nf + f, n))],
        out_shape=[out, out],
        scratch_shapes=[pltpu.VMEM((tf, tn), F32), pltpu.VMEM((tf, tn), F32)],
        compiler_params=_params(("parallel", "parallel", "parallel", "arbitrary"),
                                _vmem_limit(2 * _nbytes((tf, tk), BF16) + _nbytes((tk, tn), F32)
                                            + 2 * _nbytes((tf, tn), F32) + 2 * _nbytes((tf, tn), BF16),
                                            resident=2 * _nbytes((tf, tn), F32),
                                            temps=4 * _nbytes((tf, tn), F32))),
        name="dft_fwd",
    )(tabs["c"], tabs["s"], z, kr, ks)


def _dft_inv_kernel(inv_len, ct_ref, st_ref, yr_ref, ys_ref, z_ref, gate_ref, skip_ref,
                    o_ref, acc_ref):
    k = pl.program_id(3)

    @pl.when(k == 0)
    def _():
        acc_ref[...] = jnp.zeros_like(acc_ref)

    acc_ref[...] += _dot(ct_ref[...], yr_ref[...]) + _dot(st_ref[...], ys_ref[...])

    @pl.when(k == pl.num_programs(3) - 1)
    def _():
        y = acc_ref[...] * inv_len + skip_ref[...] * z_ref[...]
        o_ref[...] = (gate_ref[...] * y).astype(o_ref.dtype)


def _dft_inv(tabs, yr, ys, z, z_which, gate, gate_which, skip, order, n_seq, length, out_dtype):
    tt, tn, tk = _seq_tiles(length)
    nb, nt, nk = D_MODEL // tn, length // tt, length // tk
    return pl.pallas_call(
        functools.partial(_dft_inv_kernel, 1.0 / length),
        grid=(n_seq, nt, nb, nk),
        in_specs=[pl.BlockSpec((tt, tk), lambda s, t, n, k: (t, k)),
                  pl.BlockSpec((tt, tk), lambda s, t, n, k: (t, k)),
                  pl.BlockSpec((tk, tn), lambda s, t, n, k: (s * nk + k, n)),
                  pl.BlockSpec((tk, tn), lambda s, t, n, k: (s * nk + k, n)),
                  pl.BlockSpec((None, tt, tn), lambda s, t, n, k: (z_which, s * nt + t, n)),
                  pl.BlockSpec((None, tt, tn), lambda s, t, n, k: (gate_which, s * nt + t, n)),
                  pl.BlockSpec((None, 1, tn), lambda s, t, n, k: (order, 0, n))],
        out_specs=pl.BlockSpec((tt, tn), lambda s, t, n, k: (s * nt + t, n)),
        out_shape=jax.ShapeDtypeStruct((n_seq * length, D_MODEL), out_dtype),
        scratch_shapes=[pltpu.VMEM((tt, tn), F32)],
        compiler_params=_params(("parallel", "parallel", "parallel", "arbitrary"),
                                _vmem_limit(2 * _nbytes((tt, tk), BF16) + 2 * _nbytes((tk, tn), BF16)
                                            + 3 * _nbytes((tt, tn), F32),
                                            resident=_nbytes((tt, tn), F32),
                                            temps=3 * _nbytes((tt, tn), F32))),
        name="dft_inv",
    )(tabs["ct"], tabs["st"], yr, ys, z, gate, skip)


TW_LANES = 128
SUB = 8
CT_ROWS = 2 * SUB
CT_COLS = 1024


def _lane_tile(x, width):
    return jnp.tile(x, (1, width // x.shape[-1]))


def _kron_sub(f):
    return jnp.kron(f, jnp.eye(SUB, dtype=f.dtype))


def _sub_rows(x, h):
    part = x[:, h * SUB:(h + 1) * SUB, :]
    return part.reshape(part.shape[0] * SUB, part.shape[2])


def _from_sub_rows(parts):
    split = [p.reshape(p.shape[0] // SUB, SUB, p.shape[1]) for p in parts]
    return jnp.concatenate(split, axis=1)


def _ct_stage_a_kernel(n_in, *refs):
    x_refs, (fa_ref, cw_ref, sw_ref, br_ref, bi_ref) = refs[:n_in], refs[n_in:]
    fa = fa_ref[...]
    half = fa.shape[0] // 2
    width = br_ref.shape[-1]
    xs = [r[...] for r in x_refs]
    b_re, b_im = [], []
    for h in range(CT_ROWS // SUB):
        x = jnp.concatenate([_sub_rows(x, h) for x in xs], axis=0).astype(BF16)
        a = _dot(fa, x)
        ar, ai = a[:half], a[half:]
        cw, sw = _lane_tile(cw_ref[h], width), _lane_tile(sw_ref[h], width)
        b_re.append(ar * cw + ai * sw)
        b_im.append(ai * cw - ar * sw)
    br_ref[...] = _from_sub_rows(b_re).astype(BF16)
    bi_ref[...] = _from_sub_rows(b_im).astype(BF16)


def _ct_stage_a(xs, n_seq, fa, cw, sw):
    n1 = fa.shape[0] // (2 * SUB)
    n2 = cw.shape[0] * CT_ROWS
    width = xs[0][0].shape[-1]
    tn2, tw = CT_ROWS, CT_COLS
    in_specs, blocks = [], 0
    for arr, prefix in xs:
        rows_in = arr.shape[-3]
        lead = (None,) * (arr.ndim - 3)
        in_specs.append(pl.BlockSpec(lead + (rows_in, tn2, tw),
                                     lambda s, i, c, prefix=prefix: prefix(s) + (0, i, c)))
        blocks += _nbytes((rows_in, tn2, tw), arr.dtype)
    twid = pl.BlockSpec((None,) + cw.shape[1:], lambda s, i, c: (i, 0, 0, 0))
    in_specs += [pl.BlockSpec(fa.shape, lambda s, i, c: (0, 0)), twid, twid]
    out = jax.ShapeDtypeStruct((n_seq, n1, n2, width), BF16)
    out_spec = pl.BlockSpec((None, n1, tn2, tw), lambda s, i, c: (s, 0, i, c))
    return pl.pallas_call(
        functools.partial(_ct_stage_a_kernel, len(xs)),
        grid=(n_seq, n2 // tn2, width // tw),
        in_specs=in_specs,
        out_specs=[out_spec, out_spec],
        out_shape=[out, out],
        compiler_params=_params(("parallel", "parallel", "parallel"),
                                _vmem_limit(blocks + 2 * _nbytes((n1, tn2, tw), BF16)
                                            + _nbytes(fa.shape, BF16),
                                            temps=8 * _nbytes((n1, tn2, tw), F32))),
        name="ct_stage_a",
    )(*[arr for arr, _ in xs], fa, cw, sw)


def _ct_mid_kernel(br_ref, bi_ref, pr_ref, pi_ref, mr_ref, mi_ref, fb_ref, fbi_ref, cw_ref, sw_ref,
                   vr_ref, vi_ref):
    fb, fbi = fb_ref[...], fbi_ref[...]
    half = fb.shape[0] // 2
    width = br_ref.shape[-1]
    for j in range(br_ref.shape[0]):
        stack = lambda re_ref, im_ref: jnp.concatenate([re_ref[j], im_ref[j]], axis=0)
        kr = _dot(fb[:half], stack(pr_ref, pi_ref))
        ki = _dot(fb[half:], stack(mr_ref, mi_ref))
        x = _dot(fb, stack(br_ref, bi_ref))
        xr, xi = x[:half], x[half:]
        y = jnp.concatenate([xr * kr - xi * ki, xr * ki + xi * kr], axis=0).astype(BF16)
        v = _dot(fbi, y)
        vr, vi = v[:half], v[half:]
        cw, sw = _lane_tile(cw_ref[j], width), _lane_tile(sw_ref[j], width)
        vr_ref[j] = (vr * cw - vi * sw).astype(BF16)
        vi_ref[j] = (vi * cw + vr * sw).astype(BF16)


def _ct_mid(br, bi, filt, order, fb, fbi, cw, sw, tk1=4, td=1024):
    n1, n2, d = br.shape
    nd = d // td
    data = pl.BlockSpec((tk1, n2, td), lambda i, j: (i, 0, j))
    coef = pl.BlockSpec((tk1, n2, td), lambda i, j: (i, 0, order * nd + j))
    mat = pl.BlockSpec(fb.shape, lambda i, j: (0, 0))
    tw = pl.BlockSpec((tk1, n2, TW_LANES), lambda i, j: (i, 0, 0))
    out = jax.ShapeDtypeStruct((n1, n2, d), BF16)
    return pl.pallas_call(
        _ct_mid_kernel,
        grid=(n1 // tk1, nd),
        in_specs=[data, data, coef, coef, coef, coef, mat, mat, tw, tw],
        out_specs=[data, data],
        out_shape=[out, out],
        compiler_params=_params(("parallel", "parallel"),
                                _vmem_limit(8 * _nbytes((tk1, n2, td), BF16),
                                            temps=10 * _nbytes((2 * n2, td), F32))),
        name="ct_mid",
    )(br, bi, *filt, fb, fbi, cw, sw)


def _ct_inv_a_kernel(scale, vr_ref, vi_ref, fai_ref, z0_ref, z1_ref, g0_ref, g1_ref, skip_ref,
                     o_ref):
    fai = fai_ref[...]
    half = fai.shape[0] // 2
    skip = skip_ref[...]
    vr, vi = vr_ref[...].astype(F32), vi_ref[...].astype(F32)
    zs, gs = (z0_ref[...], z1_ref[...]), (g0_ref[...], g1_ref[...])
    outs = ([], [])
    for h in range(CT_ROWS // SUB):
        v = jnp.concatenate([_sub_rows(vr, h), _sub_rows(vi, h)], axis=0).astype(BF16)
        y = _dot(fai, v) * scale
        for b, yb in enumerate((y[:half], y[half:])):
            outs[b].append(_sub_rows(gs[b], h) * (yb + skip * _sub_rows(zs[b], h)))
    for b in range(2):
        o_ref[b] = _from_sub_rows(outs[b]).astype(o_ref.dtype)


def _ct_inv_a(vr, vi, fai, z, z_which, gate, gate_which, skip, order, out_dtype):
    n1, n2, d = vr.shape
    rows = fai.shape[0] // (2 * SUB)
    tn2, tw = CT_ROWS, CT_COLS // 2
    spec = pl.BlockSpec((n1, tn2, tw), lambda i, c: (0, i, c))
    pair = lambda which, b: pl.BlockSpec((None, None, rows, tn2, tw),
                                         lambda i, c: (which[0], which[1] + b, 0, i, c))
    return pl.pallas_call(
        functools.partial(_ct_inv_a_kernel, 1.0 / (n1 * n2)),
        grid=(n2 // tn2, d // tw),
        in_specs=[spec, spec, pl.BlockSpec(fai.shape, lambda i, c: (0, 0)),
                  pair(z_which, 0), pair(z_which, 1), pair(gate_which, 0), pair(gate_which, 1),
                  pl.BlockSpec((None, 1, tw), lambda i, c: (order, 0, c))],
        out_specs=pl.BlockSpec((2, rows, tn2, tw), lambda i, c: (0, 0, i, c)),
        out_shape=jax.ShapeDtypeStruct((2, rows, n2, d), out_dtype),
        compiler_params=_params(("parallel", "parallel"),
                                _vmem_limit(2 * _nbytes((n1, tn2, tw), BF16)
                                            + 6 * _nbytes((rows, tn2, tw), F32)
                                            + _nbytes(fai.shape, BF16),
                                            temps=8 * _nbytes((n1, tn2, tw), F32))),
        name="ct_inv_a",
    )(vr, vi, fai, z, z, gate, gate, skip)


def _ct_real_b_kernel(scale, br_ref, bi_ref, fb_ref, o_ref, so_ref):
    fb = fb_ref[...]
    for j in range(br_ref.shape[0]):
        so_ref[:, j, :] = _dot(fb, jnp.concatenate([br_ref[j], bi_ref[j]], axis=0)) * scale
    o_ref[...] = so_ref[...].astype(o_ref.dtype)


def _ct_real_b(br, bi, fb_re, scale):
    n_seq, n1, n2, d = br.shape
    tk1, tw = CT_ROWS, CT_COLS
    blk = pl.BlockSpec((None, tk1, n2, tw), lambda s, i, c: (s, i, 0, c))
    return pl.pallas_call(
        functools.partial(_ct_real_b_kernel, scale),
        grid=(n_seq, n1 // tk1, d // tw),
        in_specs=[blk, blk, pl.BlockSpec(fb_re.shape, lambda s, i, c: (0, 0))],
        out_specs=pl.BlockSpec((None, n2, tk1, tw), lambda s, i, c: (s, 0, i, c)),
        out_shape=jax.ShapeDtypeStruct((n_seq, n2, n1, d), BF16),
        scratch_shapes=[pltpu.VMEM((n2, tk1, tw), F32)],
        compiler_params=_params(("parallel", "parallel", "parallel"),
                                _vmem_limit(3 * _nbytes((tk1, n2, tw), BF16),
                                            resident=_nbytes((n2, tk1, tw), F32),
                                            temps=2 * _nbytes((n2, tk1, tw), F32))),
        name="ct_real_b",
    )(br, bi, fb_re)


def _cos_sin(num, den):
    ang = (num % den).astype(F32) * (2.0 * math.pi / den)
    return jnp.cos(ang), jnp.sin(ang)


def _ct_tables(n1, n2):
    i1 = jnp.arange(n1, dtype=jnp.int32)
    i2 = jnp.arange(n2, dtype=jnp.int32)
    c1, s1 = _cos_sin(i1[:, None] * i1[None, :], n1)
    c2, s2 = _cos_sin(i2[:, None] * i2[None, :], n2)
    cw, sw = _cos_sin(i2[:, None] * i1[None, :], n1 * n2)
    lanes = lambda t: jnp.broadcast_to(t[..., None], t.shape + (TW_LANES,))

    def stage_a_rows(t):
        t = t.reshape(n2 // CT_ROWS, CT_ROWS // SUB, SUB, n1)
        return lanes(jnp.swapaxes(t, 2, 3).reshape(n2 // CT_ROWS, CT_ROWS // SUB, n1 * SUB))

    return {"c1": c1, "s1": s1, "c2": c2, "s2": s2,
            "cw_a": stage_a_rows(cw), "sw_a": stage_a_rows(sw),
            "cw_b": lanes(cw.T), "sw_b": lanes(sw.T)}


def _filter_rows(feat_ref, t_ref, w1_ref, b1_ref, fr1_ref, w2_ref, b2_ref, fr2_ref, w3_ref,
                 decay_ref):
    x = jnp.sin(fr1_ref[...] * (_dot(feat_ref[...].astype(BF16), w1_ref[...].astype(BF16))
                                + b1_ref[...]))
    x = jnp.sin(fr2_ref[...] * (_dot(x.astype(BF16), w2_ref[...].astype(BF16)) + b2_ref[...]))
    h = _dot(x.astype(BF16), w3_ref[...].astype(BF16))
    return h * (jnp.exp(-t_ref[...] * jnp.exp(decay_ref[...])) + HY_SHIFT)


def _filter_stats_kernel(*refs):
    ss_ref = refs[-1]
    h = _filter_rows(*refs[:-1])

    @pl.when(pl.program_id(0) == 0)
    def _():
        ss_ref[...] = jnp.zeros_like(ss_ref)

    ss_ref[...] += jnp.sum(h * h, axis=0, keepdims=True)


def _filter_emit_kernel(*refs):
    ss_ref, a_ref, b_ref = refs[-3:]
    h = _filter_rows(*refs[:-3])
    n_dir = a_ref.shape[1]
    ss = ss_ref[...]
    norm = lax.rsqrt(ss[:, :n_dir] + ss[:, n_dir:] + 1e-12)
    fwd = h[:, :n_dir] * norm
    bwd = h[:, n_dir:] * norm
    row = lax.broadcasted_iota(jnp.int32, bwd.shape, 0) + pl.program_id(0) * bwd.shape[0]
    bwd = jnp.where(row == 0, 0.0, bwd)
    a_ref[...] = fwd + bwd
    b_ref[...] = fwd - bwd


def _hyena_filters(length, f_w1, f_b1, f_freq1, f_w2, f_b2, f_freq2, f_w3, log_decay):
    t = jnp.linspace(0.0, 1.0, length, dtype=F32)[:, None]
    t_idx = jnp.arange(length, dtype=F32)[:, None]
    bands = jnp.linspace(1e-4, HY_BANDS - 1, HY_BANDS, dtype=F32)
    w = 2.0 * math.pi * t_idx * bands / length
    feat = jnp.concatenate([t, jnp.cos(w), -jnp.sin(w)], axis=-1)
    emb_pad = 128
    feat = jnp.pad(feat, ((0, 0), (0, emb_pad - HY_EMB)))
    w1 = jnp.pad(f_w1, ((0, emb_pad - HY_EMB), (0, 0)))
    n_all = HY_DIRS * HY_ORDER * D_MODEL
    n_dir = HY_ORDER * D_MODEL
    tm = 256
    full = lambda shape: pl.BlockSpec(shape, lambda i: (0, 0))
    mlp_specs = [pl.BlockSpec((tm, emb_pad), lambda i: (i, 0)),
                 pl.BlockSpec((tm, 1), lambda i: (i, 0)),
                 full((emb_pad, HY_FW)), full((1, HY_FW)), full((1, HY_FW)),
                 full((HY_FW, HY_FW)), full((1, HY_FW)), full((1, HY_FW)),
                 full((HY_FW, n_all)), full((1, n_all))]
    mlp_args = (feat, t, w1, f_b1.reshape(1, HY_FW), f_freq1.reshape(1, HY_FW), f_w2,
                f_b2.reshape(1, HY_FW), f_freq2.reshape(1, HY_FW), f_w3, log_decay.reshape(1, n_all))
    vmem = _vmem_limit(_nbytes((tm, n_all), F32) + _nbytes((HY_FW, n_all), F32),
                       temps=4 * _nbytes((tm, n_all), F32))
    ss = pl.pallas_call(
        _filter_stats_kernel,
        grid=(length // tm,),
        in_specs=mlp_specs,
        out_specs=full((1, n_all)),
        out_shape=jax.ShapeDtypeStruct((1, n_all), F32),
        compiler_params=_params(("arbitrary",), vmem),
        name="filter_stats",
    )(*mlp_args)
    comb = jax.ShapeDtypeStruct((length, n_dir), F32)
    return pl.pallas_call(
        _filter_emit_kernel,
        grid=(length // tm,),
        in_specs=mlp_specs + [full((1, n_all))],
        out_specs=[pl.BlockSpec((tm, n_dir), lambda i: (i, 0)), pl.BlockSpec((tm, n_dir), lambda i: (i, 0))],
        out_shape=[comb, comb],
        compiler_params=_params(("parallel",), vmem),
        name="filter_emit",
    )(*mlp_args, ss)


def _cis_product(row_hi, row_lo, period):
    def cis(phase):
        ang = (phase % period).astype(F32) * (2.0 * math.pi / period)
        return jnp.cos(ang)[:, :, None], jnp.sin(ang)[:, :, None]
    (c1, s1), (c0, s0) = cis(row_hi), cis(row_lo)
    c0, s0 = jnp.swapaxes(c0, 1, 2), jnp.swapaxes(s0, 1, 2)
    rows = row_hi.shape[0]
    return ((c1 * c0 - s1 * s0).reshape(rows, -1), (s1 * c0 + c1 * s0).reshape(rows, -1))


def _odd_dft_tables(length):
    split = 1 << (length.bit_length() // 2)
    r = jnp.arange(length, dtype=jnp.int32)[:, None]
    hi = jnp.arange(length // split, dtype=jnp.int32)[None, :] * split
    lo = jnp.arange(split, dtype=jnp.int32)[None, :]
    c, s = _cis_product((2 * r + 1) * hi, (2 * r + 1) * lo, 4 * length)
    ct, st = _cis_product(r * (2 * hi), r * (2 * lo + 1), 4 * length)
    return {"c": c.astype(BF16), "s": s.astype(BF16), "ct": ct.astype(BF16), "st": st.astype(BF16)}


def _dft_tables(length):
    split = 1 << (length.bit_length() // 2)
    r = jnp.arange(length, dtype=jnp.int32)[:, None]
    hi = jnp.arange(length // split, dtype=jnp.int32)[None, :] * split
    lo = jnp.arange(split, dtype=jnp.int32)[None, :]
    c, s = _cis_product(r * hi, r * lo, length)
    return c.astype(BF16), (-s).astype(BF16)


def _hyena_mix_dense(pc, filt_p, filt_m, skip, n_seq, length):
    tabs = _odd_dft_tables(length)
    kr, ks = _mm(tabs["c"], filt_p), _mm(tabs["s"], filt_m)
    yr, ys = _dft_fwd(tabs, pc, 0, kr, ks, 0, n_seq, length)
    z1 = _dft_inv(tabs, yr, ys, pc, 0, pc, 1, skip, 0, n_seq, length, F32)[None]
    yr, ys = _dft_fwd(tabs, z1, 0, kr, ks, 1, n_seq, length)
    return _dft_inv(tabs, yr, ys, z1, 0, pc, 2, skip, 1, n_seq, length, BF16)


HY_N1, HY_N2 = 64, 128


def _hyena_mix_pair(pc, filt_p, filt_m, skip):
    assert DEC_BATCH == 2 and HY_N1 * HY_N2 == 2 * DEC_SEQ and ROWS % DEC_SEQ == 0
    n1, n2, rows_in = HY_N1, HY_N2, HY_N1 // 2
    t = _ct_tables(n1, n2)
    c_in, s_in = t["c1"][:, :rows_in], t["s1"][:, :rows_in]
    fa = _kron_sub(jnp.block([[c_in, s_in], [-s_in, c_in]])).astype(BF16)
    fa_real = _kron_sub(jnp.concatenate([c_in, -s_in], axis=0)).astype(BF16)
    fb = jnp.block([[t["c2"], t["s2"]], [-t["s2"], t["c2"]]]).astype(BF16)
    fbi = jnp.block([[t["c2"], -t["s2"]], [t["s2"], t["c2"]]]).astype(BF16)
    c_out, s_out = t["c1"][:rows_in], t["s1"][:rows_in]
    fai = _kron_sub(jnp.block([[c_out, -s_out], [s_out, c_out]])).astype(BF16)
    n_filt = HY_ORDER * D_MODEL
    filt = []
    for f in (filt_p, filt_m):
        re, im = _ct_stage_a([(f.reshape(rows_in, n2, n_filt), lambda s: ())], 1, fa_real,
                             t["cw_a"], t["sw_a"])
        filt += [re[0], im[0]]

    pc = pc.reshape(3, ROWS // DEC_SEQ, rows_in, n2, D_MODEL)
    z, z0, out_dtypes = pc, ROWS_CTX // DEC_SEQ, (F32, BF16)
    for order in range(HY_ORDER):
        br, bi = _ct_stage_a([(z, lambda s, z0=z0: (0, z0)), (z, lambda s, z0=z0: (0, z0 + 1))], 1, fa,
                             t["cw_a"], t["sw_a"])
        vr, vi = _ct_mid(br[0], bi[0], filt, order, fb, fbi, t["cw_b"], t["sw_b"])
        z = _ct_inv_a(vr, vi, fai, z, (0, z0), pc, (1 + order, ROWS_CTX // DEC_SEQ), skip, order,
                      out_dtypes[order])[None]
        z0 = 0
    return z.reshape(ROWS_LAT, D_MODEL)


def _fnet_chan_kernel(h_ref, sh_ref, sc_ref, w_ref, p_ref, q_ref):
    u = _modulate(h_ref[...], sh_ref[...], sc_ref[...]).astype(BF16)
    w = w_ref[...]
    for g in range(FNET_GROUPS):
        cols = slice(g * FNET_CG, (g + 1) * FNET_CG)
        r = _dot(u[:, cols], w)
        p_ref[:, cols] = r[:, :FNET_CG]
        q_ref[:, cols] = r[:, FNET_CG:]


def _fnet_chan(h, mods, w_cs, tm=512):
    row = pl.BlockSpec((tm, D_MODEL), lambda i: (i, 0))
    out = jax.ShapeDtypeStruct((ROWS, D_MODEL), F32)
    return pl.pallas_call(
        _fnet_chan_kernel,
        grid=(ROWS // tm,),
        in_specs=[row, _mod_spec(0, tm), _mod_spec(1, tm),
                  pl.BlockSpec((FNET_CG, 2 * FNET_CG), lambda i: (0, 0))],
        out_specs=[row, row],
        out_shape=[out, out],
        compiler_params=_params(("parallel",),
                                _vmem_limit(3 * _nbytes((tm, D_MODEL), F32),
                                            temps=2 * _nbytes((tm, D_MODEL), F32))),
        name="fnet_chan",
    )(h, mods, mods, w_cs)


def _fnet_pos_kernel(scale, c_ref, ns_ref, p_ref, q_ref, o_ref, acc_ref):
    k = pl.program_id(3)

    @pl.when(k == 0)
    def _():
        acc_ref[...] = jnp.zeros_like(acc_ref)

    acc_ref[...] += (_dot(c_ref[...], p_ref[...].astype(BF16))
                     + _dot(ns_ref[...], q_ref[...].astype(BF16)))

    @pl.when(k == pl.num_programs(3) - 1)
    def _():
        o_ref[...] = (acc_ref[...] * scale).astype(o_ref.dtype)


def _fnet_pos(c_tab, ns_tab, p, q, n_seq, length):
    tt, tn, tk = _seq_tiles(length)
    nb, nt, nk = D_MODEL // tn, length // tt, length // tk
    scale = (length * FNET_CG) ** -0.5
    return pl.pallas_call(
        functools.partial(_fnet_pos_kernel, scale),
        grid=(n_seq, nt, nb, nk),
        in_specs=[pl.BlockSpec((tt, tk), lambda s, t, n, k: (t, k)),
                  pl.BlockSpec((tt, tk), lambda s, t, n, k: (t, k)),
                  pl.BlockSpec((tk, tn), lambda s, t, n, k: (s * nk + k, n)),
                  pl.BlockSpec((tk, tn), lambda s, t, n, k: (s * nk + k, n))],
        out_specs=pl.BlockSpec((tt, tn), lambda s, t, n, k: (s * nt + t, n)),
        out_shape=jax.ShapeDtypeStruct((n_seq * length, D_MODEL), BF16),
        scratch_shapes=[pltpu.VMEM((tt, tn), F32)],
        compiler_params=_params(("parallel", "parallel", "parallel", "arbitrary"),
                                _vmem_limit(2 * _nbytes((tt, tk), BF16) + 2 * _nbytes((tk, tn), BF16)
                                            + _nbytes((tt, tn), BF16),
                                            resident=_nbytes((tt, tn), F32),
                                            temps=2 * _nbytes((tt, tn), F32))),
        name="fnet_pos",
    )(c_tab, ns_tab, p, q)


FN_N1, FN_N2 = 32, 128


def _fnet_pos_factored(p, q):
    assert FN_N1 * FN_N2 == DEC_SEQ and ROWS % DEC_SEQ == 0
    n1, n2 = FN_N1, FN_N2
    t = _ct_tables(n1, n2)
    fa = _kron_sub(jnp.block([[t["c1"], -t["s1"]], [-t["s1"], -t["c1"]]])).astype(BF16)
    fb_re = jnp.concatenate([t["c2"], t["s2"]], axis=1).astype(BF16)
    lat0 = ROWS_CTX // DEC_SEQ
    view = lambda x: x.reshape(ROWS // DEC_SEQ, n1, n2, D_MODEL)
    seq = lambda s: (lat0 + s,)
    br, bi = _ct_stage_a([(view(p), seq), (view(q), seq)], DEC_BATCH, fa, t["cw_a"], t["sw_a"])
    f = _ct_real_b(br, bi, fb_re, (DEC_SEQ * FNET_CG) ** -0.5)
    return f.reshape(ROWS_LAT, D_MODEL)


def _rope_table():
    rows = DEC_SEQ // GRID_W
    row = jnp.repeat(jnp.arange(rows), GRID_W).astype(F32)
    col = jnp.tile(jnp.arange(GRID_W), rows).astype(F32)
    inv = ROPE_THETA ** (-jnp.arange(0, AXIS_ROPE, 2, dtype=F32) / AXIS_ROPE)
    ang = jnp.concatenate([row[:, None] * inv, col[:, None] * inv], axis=-1)
    cos = jnp.repeat(jnp.cos(ang), 2, axis=-1)
    sin = jnp.repeat(jnp.sin(ang), 2, axis=-1)
    lat = jnp.tile(jnp.concatenate([cos, sin], axis=-1), (DEC_BATCH, 1))
    ctx = jnp.concatenate([jnp.ones((ROWS_CTX, QK_ROPE), F32), jnp.zeros((ROWS_CTX, QK_ROPE), F32)],
                          axis=-1)
    return jnp.concatenate([ctx, lat], axis=0)


def _pair_rotated(w):
    pairs = w.reshape(w.shape[:-1] + (QK_ROPE // 2, 2))
    return jnp.stack([-pairs[..., 1], pairs[..., 0]], axis=-1).reshape(w.shape)


def kernel(x_prompt, x_sample, c, cache_ckv, cache_krope, c_ctx, ada_w, ada_b, ln_g, ln_b, ffn_w_gate, ffn_w_up, ffn_w_down, mla_w_dq, mla_q_norm, mla_w_uq, mla_w_dkv, mla_kv_norm, mla_w_kr, mla_w_ukv, mla_w_o, hy_w_in, hy_b_in, hy_conv_w, hy_conv_b, hy_f_w1, hy_f_b1, hy_f_freq1, hy_f_w2, hy_f_b2, hy_f_freq2, hy_f_w3, hy_log_decay, hy_skip, hy_w_out, hy_b_out, fn_w_out, fn_b_out):
    assert x_prompt.shape == (BATCH, SEQ, D_MODEL) and x_sample.shape == (DEC_BATCH, DEC_SEQ, D_MODEL)
    assert ROWS_CTX % DEC_SEQ == 0 and SEQ == FNET_CG

    assert N_MIXERS > 0 and DEPTH > 1
    h = (x_prompt.reshape(ROWS_CTX, D_MODEL), x_sample.reshape(ROWS_LAT, D_MODEL))
    cond = jnp.concatenate([c_ctx[None, :], c, jnp.zeros((COND_PAD - N_COND, D_MODEL), F32)])
    mods_all = _modulation_vectors(cond, ada_w, ada_b)
    zero_bias = jnp.zeros((D_MODEL,), F32)
    ffn_w = (ffn_w_gate.astype(BF16), ffn_w_up.astype(BF16), ffn_w_down.astype(BF16))
    rope_tab = None
    ckv_states, krope_states = [], []

    for i in range(DEPTH):
        kind, j = i % N_MIXERS, i // N_MIXERS
        mods = mods_all[i]
        if kind == 0:
            if rope_tab is None:
                rope_tab = _rope_table()
            w_kr2 = jnp.concatenate([mla_w_kr[j], _pair_rotated(mla_w_kr[j])], axis=-1).astype(BF16)
            wq = mla_w_uq[j].reshape(Q_RANK, MLA_HEADS, QK_NOPE + QK_ROPE)
            w_q = jnp.concatenate([wq, _pair_rotated(wq[..., QK_NOPE:])], axis=-1)
            w_q = w_q.reshape(Q_RANK, MLA_HEADS * HEAD_W).astype(BF16)
            w_ukv = mla_w_ukv[j].reshape(KV_RANK, MLA_HEADS, QK_NOPE + V_DIM)
            w_k = w_ukv[..., :QK_NOPE].reshape(KV_RANK, MLA_HEADS * QK_NOPE).astype(BF16)
            w_vt = w_ukv[..., QK_NOPE:].reshape(KV_RANK, MLA_HEADS * V_DIM).T.astype(BF16)
            cq, ckv, kr, kr2 = _mla_down(h, mods, mla_w_dq[j].astype(BF16), mla_w_dkv[j].astype(BF16),
                                         w_kr2, mla_q_norm[j], mla_kv_norm[j], rope_tab)
            ckv_states.append(ckv[:ROWS_CTX].reshape(BATCH, SEQ, KV_RANK))
            krope_states.append(kr[:ROWS_CTX].reshape(BATCH, SEQ, QK_ROPE))
            q = _q_up(cq, w_q, rope_tab)
            k_tok, vt_tok = _kv_expand(ckv, kr2, w_k, w_vt)
            kc = cache_krope[:, j].reshape(DEC_BATCH * PAST_LEN, QK_ROPE).astype(BF16)
            k_cache, vt_cache = _kv_expand(cache_ckv[:, j].reshape(DEC_BATCH * PAST_LEN, KV_RANK),
                                           jnp.concatenate([kc, kc], axis=-1), w_k, w_vt)
            o = _attention(q, k_tok, vt_tok, k_cache, vt_cache)
            h = _mm_postnorm(o, mla_w_o[j].astype(BF16), zero_bias, h, mods, 2, ln_g[i, 0], ln_b[i, 0])
        elif kind == 1:
            pc = _hyena_in(h, mods, hy_w_in[j].astype(BF16), hy_b_in[j], hy_conv_w[j], hy_conv_b[j])
            fp = (hy_f_w1[j], hy_f_b1[j], hy_f_freq1[j], hy_f_w2[j], hy_f_b2[j], hy_f_freq2[j],
                  hy_f_w3[j], hy_log_decay[j])
            skip = hy_skip[j].reshape(HY_ORDER, 1, D_MODEL)
            z_ctx = _hyena_mix_dense(pc, *_hyena_filters(SEQ, *fp), skip, BATCH, SEQ)
            z_lat = _hyena_mix_pair(pc, *_hyena_filters(DEC_SEQ, *fp), skip)
            h = _mm_postnorm((z_ctx, z_lat), hy_w_out[j].astype(BF16), hy_b_out[j],
                             h, mods, 2, ln_g[i, 0], ln_b[i, 0])
        else:
            c_ch, ns_ch = _dft_tables(FNET_CG)
            p, q = _fnet_chan(h, mods, jnp.concatenate([c_ch, -ns_ch], axis=-1))
            f_ctx = _fnet_pos(c_ch, ns_ch, p, q, BATCH, SEQ)
            f_lat = _fnet_pos_factored(p, q)
            h = _mm_postnorm((f_ctx, f_lat), fn_w_out[j].astype(BF16), fn_b_out[j],
                             h, mods, 2, ln_g[i, 0], ln_b[i, 0])
        h = _ffn(h, mods, *ffn_w, i, ln_g[i, 1], ln_b[i, 1], split_out=i == DEPTH - 1)

    y_prompt = h[0].reshape(BATCH, SEQ, D_MODEL)
    y_sample = h[1].reshape(DEC_BATCH, DEC_SEQ, D_MODEL)
    return (y_prompt, y_sample, jnp.stack(ckv_states, axis=1), jnp.stack(krope_states, axis=1))
```

```python
import functools
import math

import jax
import jax.numpy as jnp
from jax import lax
from jax.experimental import pallas as pl
from jax.experimental.pallas import tpu as pltpu

F32 = jnp.float32
BF16 = jnp.bfloat16

D_MODEL = 2048
BATCH = 16
SEQ = 256
DEPTH = 4
DEC_BATCH = 2
DEC_SEQ = 4096
PAST_LEN = 512
GRID_W = 64
N_MIXERS = 3
MLA_HEADS = 16
QK_NOPE = 128
QK_ROPE = 64
V_DIM = 128
Q_RANK = 512
KV_RANK = 512
ROPE_THETA = 10000.0
AXIS_ROPE = QK_ROPE // 2
HY_ORDER = 2
HY_DIRS = 2
HY_CONV = 3
HY_BANDS = 16
HY_EMB = 1 + 2 * HY_BANDS
HY_FW = 64
HY_SHIFT = 0.05
FNET_GROUPS = 8
FNET_CG = D_MODEL // FNET_GROUPS
D_FF = -(-8 * D_MODEL // (3 * 256)) * 256
DN_ALPHA = (2 * DEPTH) ** 0.25
LN_EPS = 1e-5
RMS_EPS = 1e-6
N_MOD = 6

ROWS_CTX = BATCH * SEQ
ROWS_LAT = DEC_BATCH * DEC_SEQ
ROWS = ROWS_CTX + ROWS_LAT
N_COND = 1 + DEC_BATCH
COND_PAD = 8
HEAD_W = QK_NOPE + 2 * QK_ROPE
ATT_SCALE = (QK_NOPE + QK_ROPE) ** -0.5

V7X_VMEM_BYTES = 64 * 2 ** 20
VMEM_CAP_BYTES = V7X_VMEM_BYTES * 7 // 8
VMEM_FLOOR_BYTES = 32 * 2 ** 20


def _vmem_limit(pipelined, resident=0, temps=0):
    est = 2 * pipelined + resident + temps
    return int(min(max(est, VMEM_FLOOR_BYTES), VMEM_CAP_BYTES))


def _params(semantics, vmem):
    return pltpu.CompilerParams(dimension_semantics=semantics, vmem_limit_bytes=vmem)


def _nbytes(shape, dtype):
    return math.prod(shape) * jnp.dtype(dtype).itemsize


def _group_of_tile(i, tm):
    n_ctx = ROWS_CTX // tm
    return jnp.where(i < n_ctx, 0, 1 + (i - n_ctx) // (DEC_SEQ // tm))


def _mod_spec(which, tm):
    return pl.BlockSpec((None, 1, D_MODEL),
                        lambda i, *_: (which * COND_PAD + _group_of_tile(i, tm), 0, 0))


def _row_spec(width=D_MODEL):
    return pl.BlockSpec((1, width), lambda *_: (0, 0))


def _as_tuple(x):
    return x if isinstance(x, tuple) else (x,)


def _token_specs(xs, tm):
    width = xs[0].shape[1]
    if len(xs) == 1:
        return [pl.BlockSpec((tm, width), lambda i, *_: (i, 0))]
    n_ctx = ROWS_CTX // tm
    return [pl.BlockSpec((tm, width), lambda i, *_: (jnp.minimum(i, n_ctx - 1), 0)),
            pl.BlockSpec((tm, width), lambda i, *_: (jnp.maximum(i - n_ctx, 0), 0))]


def _token_rows(refs, r):
    if len(refs) == 1:
        return refs[0][r, :]
    n_ctx = ROWS_CTX // refs[0].shape[0]
    return jnp.where(pl.program_id(0) < n_ctx, refs[0][r, :], refs[1][r, :])


def _modulate(h, shift, scale):
    return h * (1.0 + scale) + shift


def _post_norm(h, delta, g, b):
    z = DN_ALPHA * h + delta
    mu = jnp.mean(z, axis=-1, keepdims=True)
    zc = z - mu
    var = jnp.mean(zc * zc, axis=-1, keepdims=True)
    return zc * lax.rsqrt(var + LN_EPS) * g + b


def _rms_norm(x, g):
    ms = jnp.mean(x * x, axis=-1, keepdims=True)
    return x * lax.rsqrt(ms + RMS_EPS) * g


def _dot(a, b):
    return jnp.dot(a, b, preferred_element_type=F32)


def _modvec_kernel(c_ref, w_ref, b_ref, o_ref):
    a = jax.nn.silu(c_ref[...]).astype(BF16)
    o_ref[...] = _dot(a, w_ref[...].astype(BF16)) + b_ref[...]


def _modulation_vectors(cond, ada_w, ada_b):
    tn = 1024
    n = N_MOD * D_MODEL
    out = pl.pallas_call(
        _modvec_kernel,
        grid=(DEPTH, n // tn),
        in_specs=[pl.BlockSpec((COND_PAD, D_MODEL), lambda l, j: (0, 0)),
                  pl.BlockSpec((None, D_MODEL, tn), lambda l, j: (l, 0, j)),
                  pl.BlockSpec((None, 1, tn), lambda l, j: (l, 0, j))],
        out_specs=pl.BlockSpec((None, COND_PAD, tn), lambda l, j: (l, 0, j)),
        out_shape=jax.ShapeDtypeStruct((DEPTH, COND_PAD, n), F32),
        compiler_params=_params(("parallel", "parallel"),
                                _vmem_limit(_nbytes((D_MODEL, tn), F32),
                                            temps=_nbytes((D_MODEL, tn), BF16))),
        name="modvec",
    )(cond, ada_w, ada_b.reshape(DEPTH, 1, n))
    out = out.reshape(DEPTH, COND_PAD, N_MOD, D_MODEL).transpose(0, 2, 1, 3)
    return out.reshape(DEPTH, N_MOD * COND_PAD, 1, D_MODEL)


def _hyena_in_kernel(n_ctx, h_ref, hp_ref, hn_ref, sh_ref, sc_ref, w_ref, b_ref, cw_ref, cb_ref,
                     o_ref, u_ref):
    i = pl.program_id(0)
    tm = h_ref.shape[0]

    @pl.when(pl.program_id(1) == 0)
    def _():
        mod = lambda x_ref: _modulate(x_ref[...], sh_ref[...], sc_ref[...]).astype(BF16)
        u_ref[:SUB, :] = mod(hp_ref)
        u_ref[SUB:SUB + tm, :] = mod(h_ref)
        u_ref[SUB + tm:, :] = mod(hn_ref)

    length = jnp.where(i < n_ctx, SEQ, DEC_SEQ)
    half = tm // 2
    starts = (0, half)
    ys = [_dot(u_ref[r0:r0 + half + 2 * SUB, :], w_ref[...]) + b_ref[...] for r0 in starts]
    for r0, y in zip(starts, ys):
        inner = slice(SUB, SUB + half)
        before = pltpu.roll(y, 1, 0)[inner]
        after = pltpu.roll(y, y.shape[0] - 1, 0)[inner]
        pos = (lax.broadcasted_iota(jnp.int32, before.shape, 0) + (i * tm + r0)) & (length - 1)
        before = jnp.where(pos == 0, 0.0, before)
        after = jnp.where(pos == length - 1, 0.0, after)
        o_ref[r0:r0 + half, :] = (cw_ref[0:1, :] * before + cw_ref[1:2, :] * y[inner]
                                  + cw_ref[2:3, :] * after + cb_ref[...])


def _hyena_in(h, mods, w, b, conv_w, conv_b, tm=1024, tn=1024):
    assert SEQ & (SEQ - 1) == 0 and DEC_SEQ & (DEC_SEQ - 1) == 0 and tm % SEQ == 0 and DEC_SEQ % tm == 0
    k, n = w.shape
    per, halo, last = D_MODEL // tn, tm // SUB, ROWS // SUB - 1
    return pl.pallas_call(
        functools.partial(_hyena_in_kernel, ROWS_CTX // tm),
        grid=(ROWS // tm, n // tn),
        in_specs=[pl.BlockSpec((tm, k), lambda i, j: (i, 0)),
                  pl.BlockSpec((SUB, k), lambda i, j: (jnp.maximum(i * halo - 1, 0), 0)),
                  pl.BlockSpec((SUB, k), lambda i, j: (jnp.minimum((i + 1) * halo, last), 0)),
                  _mod_spec(0, tm), _mod_spec(1, tm),
                  pl.BlockSpec((k, tn), lambda i, j: (0, j)),
                  pl.BlockSpec((1, tn), lambda i, j: (0, j)),
                  pl.BlockSpec((HY_CONV, tn), lambda i, j: (0, j)),
                  pl.BlockSpec((1, tn), lambda i, j: (0, j))],
        out_specs=pl.BlockSpec((None, tm, tn), lambda i, j: (j // per, i, j % per)),
        out_shape=jax.ShapeDtypeStruct((n // D_MODEL, ROWS, D_MODEL), F32),
        scratch_shapes=[pltpu.VMEM((tm + 2 * SUB, k), BF16)],
        compiler_params=_params(("parallel", "arbitrary"),
                                _vmem_limit(_nbytes((tm, k), F32) + _nbytes((k, tn), BF16)
                                            + _nbytes((tm, tn), F32),
                                            resident=_nbytes((tm, k), BF16),
                                            temps=5 * _nbytes((tm, tn), F32))),
        name="hyena_in",
    )(h, h, h, mods, mods, w, b.reshape(1, n), conv_w, conv_b.reshape(1, n))


def _mm_postnorm_kernel(n_a, n_h, *refs):
    a_refs, refs = refs[:n_a], refs[n_a:]
    (w_ref, bias_ref), refs = refs[:2], refs[2:]
    h_refs, (gate_ref, g_ref, b_ref, o_ref) = refs[:n_h], refs[n_h:]
    half = o_ref.shape[0] // 2
    halves = (slice(0, half), slice(half, 2 * half))
    ys = [_dot(_token_rows(a_refs, r), w_ref[...]) + bias_ref[...] for r in halves]
    for r, y in zip(halves, ys):
        o_ref[r, :] = _post_norm(_token_rows(h_refs, r), gate_ref[...] * y, g_ref[...], b_ref[...])


def _mm_postnorm(a, w, bias, h, mods, which_gate, ln_g, ln_b, tm=512):
    a, h = _as_tuple(a), _as_tuple(h)
    k = a[0].shape[1]
    return pl.pallas_call(
        functools.partial(_mm_postnorm_kernel, len(a), len(h)),
        grid=(ROWS // tm,),
        in_specs=(_token_specs(a, tm) + [pl.BlockSpec((k, D_MODEL), lambda i: (0, 0)), _row_spec()]
                  + _token_specs(h, tm) + [_mod_spec(which_gate, tm), _row_spec(), _row_spec()]),
        out_specs=pl.BlockSpec((tm, D_MODEL), lambda i: (i, 0)),
        out_shape=jax.ShapeDtypeStruct((ROWS, D_MODEL), F32),
        compiler_params=_params(("parallel",),
                                _vmem_limit(len(a) * _nbytes((tm, k), BF16)
                                            + _nbytes((k, D_MODEL), BF16)
                                            + (1 + len(h)) * _nbytes((tm, D_MODEL), F32),
                                            temps=3 * _nbytes((tm, D_MODEL), F32))),
        name="mm_postnorm",
    )(*a, w, bias.reshape(1, D_MODEL), *h, mods, ln_g.reshape(1, D_MODEL), ln_b.reshape(1, D_MODEL))


def _ffn_kernel(n_ctx, h_ref, sh_ref, sc_ref, gate_ref, g_ref, b_ref, wg_ref, wu_ref, wd_ref,
                *refs):
    i, f = pl.program_id(0), pl.program_id(1)
    if n_ctx is None:
        o_ref, u_ref = refs
        acc_ref, outs = o_ref, ((o_ref, None),)
    else:
        octx_ref, olat_ref, u_ref, acc_ref = refs
        outs = ((octx_ref, i < n_ctx), (olat_ref, i >= n_ctx))

    @pl.when(f == 0)
    def _():
        u_ref[...] = _modulate(h_ref[...], sh_ref[...], sc_ref[...]).astype(BF16)
        acc_ref[...] = jnp.zeros_like(acc_ref)

    half = u_ref.shape[0] // 2
    halves = (slice(0, half), slice(half, 2 * half))
    proj = [(_dot(u_ref[r, :], wg_ref[...]), _dot(u_ref[r, :], wu_ref[...])) for r in halves]
    for r, (gate, up) in zip(halves, proj):
        act = (jax.nn.silu(gate) * up).astype(BF16)
        acc_ref[r, :] += _dot(act, wd_ref[...])

    last = f == pl.num_programs(1) - 1
    for o_ref, mine in outs:
        @pl.when(last if mine is None else jnp.logical_and(last, mine))
        def _(o_ref=o_ref):
            for r in halves:
                o_ref[r, :] = _post_norm(h_ref[r, :], gate_ref[...] * acc_ref[r, :],
                                         g_ref[...], b_ref[...])


def _ffn(h, mods, w_gate, w_up, w_down, layer, ln_g, ln_b, split_out, tm=512, tf=512):
    tile = _nbytes((tm, D_MODEL), F32)
    if split_out:
        n_ctx = ROWS_CTX // tm
        out_specs = [pl.BlockSpec((tm, D_MODEL), lambda i, f: (jnp.minimum(i, n_ctx - 1), 0)),
                     pl.BlockSpec((tm, D_MODEL), lambda i, f: (jnp.maximum(i - n_ctx, 0), 0))]
        out_shape = [jax.ShapeDtypeStruct((ROWS_CTX, D_MODEL), F32),
                     jax.ShapeDtypeStruct((ROWS_LAT, D_MODEL), F32)]
        scratch = [pltpu.VMEM((tm, D_MODEL), BF16), pltpu.VMEM((tm, D_MODEL), F32)]
        pipelined, resident = 3 * tile, tile + tile // 2
    else:
        n_ctx = None
        out_specs = pl.BlockSpec((tm, D_MODEL), lambda i, f: (i, 0))
        out_shape = jax.ShapeDtypeStruct((ROWS, D_MODEL), F32)
        scratch = [pltpu.VMEM((tm, D_MODEL), BF16)]
        pipelined, resident = 2 * tile, tile // 2
    return pl.pallas_call(
        functools.partial(_ffn_kernel, n_ctx),
        grid=(ROWS // tm, D_FF // tf),
        in_specs=[pl.BlockSpec((tm, D_MODEL), lambda i, f: (i, 0)),
                  _mod_spec(3, tm), _mod_spec(4, tm), _mod_spec(5, tm),
                  _row_spec(), _row_spec(),
                  pl.BlockSpec((None, D_MODEL, tf), lambda i, f: (layer, 0, f)),
                  pl.BlockSpec((None, D_MODEL, tf), lambda i, f: (layer, 0, f)),
                  pl.BlockSpec((None, tf, D_MODEL), lambda i, f: (layer, f, 0))],
        out_specs=out_specs,
        out_shape=out_shape,
        scratch_shapes=scratch,
        compiler_params=_params(("arbitrary" if split_out else "parallel", "arbitrary"),
                                _vmem_limit(pipelined + 3 * _nbytes((D_MODEL, tf), BF16),
                                            resident=resident, temps=2 * tile)),
        name="ffn",
    )(h, mods, mods, mods, ln_g.reshape(1, D_MODEL), ln_b.reshape(1, D_MODEL),
      w_gate, w_up, w_down)


def _mla_down_kernel(n_h, *refs):
    h_refs, (sh_ref, sc_ref, wdq_ref, wdkv_ref, wkr_ref, qn_ref, kvn_ref, rope_ref,
             cq_ref, ckv_ref, kr_ref, kr2_ref) = refs[:n_h], refs[n_h:]
    half = cq_ref.shape[0] // 2
    halves = (slice(0, half), slice(half, 2 * half))
    us = [_modulate(_token_rows(h_refs, r), sh_ref[...], sc_ref[...]).astype(BF16) for r in halves]
    projs = [(_dot(u, wdq_ref[...]), _dot(u, wdkv_ref[...]), _dot(u, wkr_ref[...])) for u in us]
    for r, (q_lat, kv_lat, t) in zip(halves, projs):
        cq_ref[r, :] = _rms_norm(q_lat, qn_ref[...]).astype(BF16)
        ckv_ref[r, :] = _rms_norm(kv_lat, kvn_ref[...])
        kr_ref[r, :] = t[:, :QK_ROPE]
        v = t * rope_ref[r, :]
        kr2_ref[r, :] = (v + pltpu.roll(v, QK_ROPE, 1)).astype(BF16)


def _mla_down(h, mods, w_dq, w_dkv, w_kr2, q_norm, kv_norm, rope_tab, tm=512):
    h = _as_tuple(h)
    row = lambda width: pl.BlockSpec((tm, width), lambda i: (i, 0))
    full = lambda shape: pl.BlockSpec(shape, lambda i: (0, 0))
    return pl.pallas_call(
        functools.partial(_mla_down_kernel, len(h)),
        grid=(ROWS // tm,),
        in_specs=_token_specs(h, tm) + [
            _mod_spec(0, tm), _mod_spec(1, tm),
            full((D_MODEL, Q_RANK)), full((D_MODEL, KV_RANK)), full((D_MODEL, 2 * QK_ROPE)),
            _row_spec(Q_RANK), _row_spec(KV_RANK), row(2 * QK_ROPE)],
        out_specs=[row(Q_RANK), row(KV_RANK), row(QK_ROPE), row(2 * QK_ROPE)],
        out_shape=[jax.ShapeDtypeStruct((ROWS, Q_RANK), BF16),
                   jax.ShapeDtypeStruct((ROWS, KV_RANK), F32),
                   jax.ShapeDtypeStruct((ROWS, QK_ROPE), F32),
                   jax.ShapeDtypeStruct((ROWS, 2 * QK_ROPE), BF16)],
        compiler_params=_params(("parallel",),
                                _vmem_limit(len(h) * _nbytes((tm, D_MODEL), F32)
                                            + _nbytes((D_MODEL, Q_RANK + KV_RANK + 2 * QK_ROPE), BF16)
                                            + 3 * _nbytes((tm, KV_RANK), F32),
                                            temps=2 * _nbytes((tm, D_MODEL), F32))),
        name="mla_down",
    )(*h, mods, mods, w_dq, w_dkv, w_kr2, q_norm.reshape(1, Q_RANK), kv_norm.reshape(1, KV_RANK),
      rope_tab)


NT_DIMS = (((1,), (1,)), ((), ()))


def _q_up_kernel(cq_ref, w_ref, rope_ref, q_ref):
    cq, tab = cq_ref[...], rope_ref[...] * LOG2E_SCALE
    for h in range(MLA_HEADS):
        r = _dot(cq, w_ref[:, h * HEAD_W:(h + 1) * HEAD_W])
        q_ref[h, :, :QK_NOPE] = (r[:, :QK_NOPE] * LOG2E_SCALE).astype(BF16)
        q_ref[h, :, QK_NOPE:] = (r[:, QK_NOPE:] * tab).astype(BF16)


def _q_up(cq, w_q, rope_tab, tm=512):
    return pl.pallas_call(
        _q_up_kernel,
        grid=(ROWS // tm,),
        in_specs=[pl.BlockSpec((tm, Q_RANK), lambda i: (i, 0)),
                  pl.BlockSpec((Q_RANK, MLA_HEADS * HEAD_W), lambda i: (0, 0)),
                  pl.BlockSpec((tm, 2 * QK_ROPE), lambda i: (i, 0))],
        out_specs=pl.BlockSpec((MLA_HEADS, tm, HEAD_W), lambda i: (0, i, 0)),
        out_shape=jax.ShapeDtypeStruct((MLA_HEADS, ROWS, HEAD_W), BF16),
        compiler_params=_params(("parallel",),
                                _vmem_limit(_nbytes((Q_RANK + tm, MLA_HEADS * HEAD_W), BF16))),
        name="q_up",
    )(cq, w_q, rope_tab)


VT_PAD = 16
VT_ROWS = V_DIM + VT_PAD


def _kv_expand_kernel(ckv_ref, kr2_ref, wk_ref, wvt_ref, k_ref, vt_ref):
    c = ckv_ref[...].astype(BF16)
    vt = lax.dot_general(wvt_ref[...], c, NT_DIMS, preferred_element_type=F32).astype(BF16)
    ones = jnp.ones((VT_PAD, vt.shape[1]), BF16)
    for h in range(MLA_HEADS):
        vt_ref[h * VT_ROWS:h * VT_ROWS + V_DIM, :] = vt[h * V_DIM:(h + 1) * V_DIM]
        vt_ref[h * VT_ROWS + V_DIM:(h + 1) * VT_ROWS, :] = ones
    kr2 = kr2_ref[...]
    pair_w = 2 * QK_NOPE
    for g in range(MLA_HEADS // 2):
        r = _dot(c, wk_ref[:, g * pair_w:(g + 1) * pair_w]).astype(BF16)
        for e in range(2):
            k_ref[2 * g + e, :, :QK_NOPE] = r[:, e * QK_NOPE:(e + 1) * QK_NOPE]
            k_ref[2 * g + e, :, QK_NOPE:] = kr2


def _kv_expand(ckv, kr2, w_k, w_vt, tm=512):
    rows = ckv.shape[0]
    return pl.pallas_call(
        _kv_expand_kernel,
        grid=(rows // tm,),
        in_specs=[pl.BlockSpec((tm, KV_RANK), lambda i: (i, 0)),
                  pl.BlockSpec((tm, 2 * QK_ROPE), lambda i: (i, 0)),
                  pl.BlockSpec((KV_RANK, MLA_HEADS * QK_NOPE), lambda i: (0, 0)),
                  pl.BlockSpec((MLA_HEADS * V_DIM, KV_RANK), lambda i: (0, 0))],
        out_specs=[pl.BlockSpec((MLA_HEADS, tm, HEAD_W), lambda i: (0, i, 0)),
                   pl.BlockSpec((MLA_HEADS * VT_ROWS, tm), lambda i: (0, i))],
        out_shape=[jax.ShapeDtypeStruct((MLA_HEADS, rows, HEAD_W), BF16),
                   jax.ShapeDtypeStruct((MLA_HEADS * VT_ROWS, rows), BF16)],
        compiler_params=_params(("parallel",),
                                _vmem_limit(_nbytes((tm, MLA_HEADS * (HEAD_W + V_DIM)), BF16)
                                            + 2 * _nbytes((KV_RANK, MLA_HEADS * V_DIM), BF16),
                                            temps=_nbytes((MLA_HEADS * V_DIM, tm), F32))),
        name="kv_expand",
    )(ckv, kr2, w_k, w_vt)


ATT_CHUNK = 512
ATT_SKEW = 3
LOG2E_SCALE = ATT_SCALE * math.log2(math.e)


def _attn_scores(q, k):
    return lax.dot_general(k, q, NT_DIMS, preferred_element_type=F32)


def _attn_values(s, vt, carry):
    m = jnp.max(s, axis=0, keepdims=True)
    if carry is not None:
        m_old, acc_old = carry
        m = jnp.maximum(m_old, m)
    acc = _dot(vt, jnp.exp2(s - m).astype(BF16))
    if carry is not None:
        acc = jnp.exp2(m_old - m) * acc_old + acc
    return m, acc


def _attn_output(acc):
    return (acc[:V_DIM] / acc[V_DIM:V_DIM + 1]).T.astype(BF16)


def _attn_ctx_kernel(q_ref, k_ref, vt_ref, o_ref):
    for h in range(MLA_HEADS):
        _, acc = _attn_values(_attn_scores(q_ref[h], k_ref[h]),
                              vt_ref[h * VT_ROWS:(h + 1) * VT_ROWS, :], None)
        o_ref[:, h * V_DIM:(h + 1) * V_DIM] = _attn_output(acc)


def _attn_lat_kernel(q_ref, k_ref, vt_ref, kc_ref, vtc_ref, prev_ref, o_ref):
    del prev_ref
    q = q_ref[...]
    n_tok = DEC_SEQ // ATT_CHUNK
    rows = lambda c: slice(c * ATT_CHUNK, (c + 1) * ATT_CHUNK)
    keys = [k_ref.at[rows(c), :] for c in range(n_tok)] + [kc_ref]
    vals = [vt_ref.at[:, rows(c)] for c in range(n_tok)] + [vtc_ref]
    carry = None
    scores = [_attn_scores(q, keys[c][...]) for c in range(ATT_SKEW)]
    for c in range(n_tok + 1):
        if c + ATT_SKEW <= n_tok:
            scores.append(_attn_scores(q, keys[c + ATT_SKEW][...]))
        carry = _attn_values(scores[c], vals[c][...], carry)
    o_ref[...] = _attn_output(carry[1])


def _attention(q, k_tok, vt_tok, k_cache, vt_cache, tq=2048):
    assert DEC_SEQ % ATT_CHUNK == 0
    out_shape = jax.ShapeDtypeStruct((ROWS, MLA_HEADS * V_DIM), BF16)
    o = pl.pallas_call(
        _attn_ctx_kernel,
        grid=(BATCH,),
        in_specs=[pl.BlockSpec((MLA_HEADS, SEQ, HEAD_W), lambda s: (0, s, 0)),
                  pl.BlockSpec((MLA_HEADS, SEQ, HEAD_W), lambda s: (0, s, 0)),
                  pl.BlockSpec((MLA_HEADS * VT_ROWS, SEQ), lambda s: (0, s))],
        out_specs=pl.BlockSpec((SEQ, MLA_HEADS * V_DIM), lambda s: (s, 0)),
        out_shape=out_shape,
        compiler_params=_params(("parallel",), VMEM_FLOOR_BYTES),
        name="attn_ctx",
    )(q, k_tok, vt_tok)

    lat0 = ROWS_CTX // DEC_SEQ
    q0 = ROWS_CTX // tq
    nq = DEC_SEQ // tq
    return pl.pallas_call(
        _attn_lat_kernel,
        grid=(DEC_BATCH, MLA_HEADS, nq),
        in_specs=[pl.BlockSpec((None, tq, HEAD_W), lambda b, h, i: (h, q0 + b * nq + i, 0)),
                  pl.BlockSpec((None, DEC_SEQ, HEAD_W), lambda b, h, i: (h, lat0 + b, 0)),
                  pl.BlockSpec((VT_ROWS, DEC_SEQ), lambda b, h, i: (h, lat0 + b)),
                  pl.BlockSpec((None, PAST_LEN, HEAD_W), lambda b, h, i: (h, b, 0)),
                  pl.BlockSpec((VT_ROWS, PAST_LEN), lambda b, h, i: (h, b)),
                  pl.BlockSpec(memory_space=pl.ANY)],
        out_specs=pl.BlockSpec((tq, V_DIM), lambda b, h, i: (q0 + b * nq + i, h)),
        out_shape=out_shape,
        input_output_aliases={5: 0},
        compiler_params=_params(("parallel", "parallel", "arbitrary"),
                                _vmem_limit(_nbytes((DEC_SEQ + PAST_LEN, HEAD_W + V_DIM), BF16),
                                            temps=8 * _nbytes((ATT_CHUNK, tq), F32))),
        name="attn_lat",
    )(q, k_tok, vt_tok, k_cache, vt_cache, o)


def _mm_kernel(a_ref, b_ref, o_ref, acc_ref):
    k = pl.program_id(2)

    @pl.when(k == 0)
    def _():
        acc_ref[...] = jnp.zeros_like(acc_ref)

    acc_ref[...] += _dot(a_ref[...], b_ref[...].astype(BF16))

    @pl.when(k == pl.num_programs(2) - 1)
    def _():
        o_ref[...] = acc_ref[...].astype(o_ref.dtype)


def _mm(a, b, out_dtype=F32, tm=1024, tn=1024, tk=512):
    m, kk = a.shape
    n = b.shape[1]
    tm, tn, tk = min(tm, m), min(tn, n), min(tk, kk)
    return pl.pallas_call(
        _mm_kernel,
        grid=(m // tm, n // tn, kk // tk),
        in_specs=[pl.BlockSpec((tm, tk), lambda i, j, k: (i, k)),
                  pl.BlockSpec((tk, tn), lambda i, j, k: (k, j))],
        out_specs=pl.BlockSpec((tm, tn), lambda i, j, k: (i, j)),
        out_shape=jax.ShapeDtypeStruct((m, n), out_dtype),
        scratch_shapes=[pltpu.VMEM((tm, tn), F32)],
        compiler_params=_params(("parallel", "parallel", "arbitrary"), VMEM_FLOOR_BYTES),
        name="mm",
    )(a, b)


def _seq_tiles(length):
    if length >= 1024:
        return 1024, 512, 1024
    return length, D_MODEL, length


def _dft_fwd_kernel(c_ref, s_ref, z_ref, kr_ref, ks_ref, yr_ref, ys_ref, accr_ref, accs_ref):
    k = pl.program_id(3)

    @pl.when(k == 0)
    def _():
        accr_ref[...] = jnp.zeros_like(accr_ref)
        accs_ref[...] = jnp.zeros_like(accs_ref)

    z = z_ref[...].astype(BF16)
    accr_ref[...] += _dot(c_ref[...], z)
    accs_ref[...] += _dot(s_ref[...], z)

    @pl.when(k == pl.num_programs(3) - 1)
    def _():
        zr, zs, kr, ks = accr_ref[...], accs_ref[...], kr_ref[...], ks_ref[...]
        yr_ref[...] = (zr * kr - zs * ks).astype(BF16)
        ys_ref[...] = (zr * ks + zs * kr).astype(BF16)


def _dft_fwd(tabs, z, z_which, kr, ks, order, n_seq, length):
    tf, tn, tk = _seq_tiles(length)
    nb, nf, nk = D_MODEL // tn, length // tf, length // tk
    kc0 = order * nb
    out = jax.ShapeDtypeStruct((n_seq * length, D_MODEL), BF16)
    return pl.pallas_call(
        _dft_fwd_kernel,
        grid=(n_seq, nf, nb, nk),
        in_specs=[pl.BlockSpec((tf, tk), lambda s, f, n, k: (f, k)),
                  pl.BlockSpec((tf, tk), lambda s, f, n, k: (f, k)),
                  pl.BlockSpec((None, tk, tn), lambda s, f, n, k: (z_which, s * nk + k, n)),
                  pl.BlockSpec((tf, tn), lambda s, f, n, k: (f, kc0 + n)),
                  pl.BlockSpec((tf, tn), lambda s, f, n, k: (f, kc0 + n))],
        out_specs=[pl.BlockSpec((tf, tn), lambda s, f, n, k: (s * nf + f, n)),
                   pl.BlockSpec((tf, tn), lambda s, f, n, k: (s * nf + f, n))],
        out_shape=[out, out],
        scratch_shapes=[pltpu.VMEM((tf, tn), F32), pltpu.VMEM((tf, tn), F32)],
        compiler_params=_params(("parallel", "parallel", "parallel", "arbitrary"),
                                _vmem_limit(2 * _nbytes((tf, tk), BF16) + _nbytes((tk, tn), F32)
                                            + 2 * _nbytes((tf, tn), F32) + 2 * _nbytes((tf, tn), BF16),
                                            resident=2 * _nbytes((tf, tn), F32),
                                            temps=4 * _nbytes((tf, tn), F32))),
        name="dft_fwd",
    )(tabs["c"], tabs["s"], z, kr, ks)


def _dft_inv_kernel(inv_len, ct_ref, st_ref, yr_ref, ys_ref, z_ref, gate_ref, skip_ref,
                    o_ref, acc_ref):
    k = pl.program_id(3)

    @pl.when(k == 0)
    def _():
        acc_ref[...] = jnp.zeros_like(acc_ref)

    acc_ref[...] += _dot(ct_ref[...], yr_ref[...]) + _dot(st_ref[...], ys_ref[...])

    @pl.when(k == pl.num_programs(3) - 1)
    def _():
        y = acc_ref[...] * inv_len + skip_ref[...] * z_ref[...]
        o_ref[...] = (gate_ref[...] * y).astype(o_ref.dtype)


def _dft_inv(tabs, yr, ys, z, z_which, gate, gate_which, skip, order, n_seq, length, out_dtype):
    tt, tn, tk = _seq_tiles(length)
    nb, nt, nk = D_MODEL // tn, length // tt, length // tk
    return pl.pallas_call(
        functools.partial(_dft_inv_kernel, 1.0 / length),
        grid=(n_seq, nt, nb, nk),
        in_specs=[pl.BlockSpec((tt, tk), lambda s, t, n, k: (t, k)),
                  pl.BlockSpec((tt, tk), lambda s, t, n, k: (t, k)),
                  pl.BlockSpec((tk, tn), lambda s, t, n, k: (s * nk + k, n)),
                  pl.BlockSpec((tk, tn), lambda s, t, n, k: (s * nk + k, n)),
                  pl.BlockSpec((None, tt, tn), lambda s, t, n, k: (z_which, s * nt + t, n)),
                  pl.BlockSpec((None, tt, tn), lambda s, t, n, k: (gate_which, s * nt + t, n)),
                  pl.BlockSpec((None, 1, tn), lambda s, t, n, k: (order, 0, n))],
        out_specs=pl.BlockSpec((tt, tn), lambda s, t, n, k: (s * nt + t, n)),
        out_shape=jax.ShapeDtypeStruct((n_seq * length, D_MODEL), out_dtype),
        scratch_shapes=[pltpu.VMEM((tt, tn), F32)],
        compiler_params=_params(("parallel", "parallel", "parallel", "arbitrary"),
                                _vmem_limit(2 * _nbytes((tt, tk), BF16) + 2 * _nbytes((tk, tn), BF16)
                                            + 3 * _nbytes((tt, tn), F32),
                                            resident=_nbytes((tt, tn), F32),
                                            temps=3 * _nbytes((tt, tn), F32))),
        name="dft_inv",
    )(tabs["ct"], tabs["st"], yr, ys, z, gate, skip)


TW_LANES = 128
SUB = 8
CT_ROWS = 2 * SUB
CT_COLS = 1024


def _lane_tile(x, width):
    return jnp.tile(x, (1, width // x.shape[-1]))


def _kron_sub(f):
    return jnp.kron(f, jnp.eye(SUB, dtype=f.dtype))


def _sub_rows(x, h):
    part = x[:, h * SUB:(h + 1) * SUB, :]
    return part.reshape(part.shape[0] * SUB, part.shape[2])


def _from_sub_rows(parts):
    split = [p.reshape(p.shape[0] // SUB, SUB, p.shape[1]) for p in parts]
    return jnp.concatenate(split, axis=1)


def _ct_stage_a_kernel(n_in, *refs):
    x_refs, (fa_ref, cw_ref, sw_ref, br_ref, bi_ref) = refs[:n_in], refs[n_in:]
    fa = fa_ref[...]
    half = fa.shape[0] // 2
    width = br_ref.shape[-1]
    xs = [r[...].astype(F32) for r in x_refs]
    b_re, b_im = [], []
    for h in range(CT_ROWS // SUB):
        x = jnp.concatenate([_sub_rows(x, h) for x in xs], axis=0).astype(BF16)
        a = _dot(fa, x)
        ar, ai = a[:half], a[half:]
        cw, sw = _lane_tile(cw_ref[h], width), _lane_tile(sw_ref[h], width)
        b_re.append(ar * cw + ai * sw)
        b_im.append(ai * cw - ar * sw)
    br_ref[...] = _from_sub_rows(b_re).astype(BF16)
    bi_ref[...] = _from_sub_rows(b_im).astype(BF16)


def _ct_stage_a(xs, n_seq, fa, cw, sw):
    n1 = fa.shape[0] // (2 * SUB)
    n2 = cw.shape[0] * CT_ROWS
    width = xs[0][0].shape[-1]
    tn2, tw = CT_ROWS, CT_COLS
    in_specs, blocks = [], 0
    for arr, prefix in xs:
        rows_in = arr.shape[-3]
        lead = (None,) * (arr.ndim - 3)
        in_specs.append(pl.BlockSpec(lead + (rows_in, tn2, tw),
                                     lambda s, i, c, prefix=prefix: prefix(s) + (0, i, c)))
        blocks += _nbytes((rows_in, tn2, tw), arr.dtype)
    twid = pl.BlockSpec((None,) + cw.shape[1:], lambda s, i, c: (i, 0, 0, 0))
    in_specs += [pl.BlockSpec(fa.shape, lambda s, i, c: (0, 0)), twid, twid]
    out = jax.ShapeDtypeStruct((n_seq, n1, n2, width), BF16)
    out_spec = pl.BlockSpec((None, n1, tn2, tw), lambda s, i, c: (s, 0, i, c))
    return pl.pallas_call(
        functools.partial(_ct_stage_a_kernel, len(xs)),
        grid=(n_seq, n2 // tn2, width // tw),
        in_specs=in_specs,
        out_specs=[out_spec, out_spec],
        out_shape=[out, out],
        compiler_params=_params(("parallel", "parallel", "parallel"),
                                _vmem_limit(blocks + 2 * _nbytes((n1, tn2, tw), BF16)
                                            + _nbytes(fa.shape, BF16),
                                            temps=8 * _nbytes((n1, tn2, tw), F32))),
        name="ct_stage_a",
    )(*[arr for arr, _ in xs], fa, cw, sw)


def _ct_mid_kernel(br_ref, bi_ref, pr_ref, pi_ref, mr_ref, mi_ref, fb_ref, fbi_ref, cw_ref, sw_ref,
                   vr_ref, vi_ref):
    fb, fbi = fb_ref[...], fbi_ref[...]
    half = fb.shape[0] // 2
    width = br_ref.shape[-1]
    for j in range(br_ref.shape[0]):
        stack = lambda re_ref, im_ref: jnp.concatenate([re_ref[j], im_ref[j]], axis=0)
        kr = _dot(fb[:half], stack(pr_ref, pi_ref))
        ki = _dot(fb[half:], stack(mr_ref, mi_ref))
        x = _dot(fb, stack(br_ref, bi_ref))
        xr, xi = x[:half], x[half:]
        y = jnp.concatenate([xr * kr - xi * ki, xr * ki + xi * kr], axis=0).astype(BF16)
        v = _dot(fbi, y)
        vr, vi = v[:half], v[half:]
        cw, sw = _lane_tile(cw_ref[j], width), _lane_tile(sw_ref[j], width)
        vr_ref[j] = (vr * cw - vi * sw).astype(BF16)
        vi_ref[j] = (vi * cw + vr * sw).astype(BF16)


def _ct_mid(br, bi, filt, order, fb, fbi, cw, sw, tk1=4, td=1024):
    n1, n2, d = br.shape
    nd = d // td
    data = pl.BlockSpec((tk1, n2, td), lambda i, j: (i, 0, j))
    coef = pl.BlockSpec((tk1, n2, td), lambda i, j: (i, 0, order * nd + j))
    mat = pl.BlockSpec(fb.shape, lambda i, j: (0, 0))
    tw = pl.BlockSpec((tk1, n2, TW_LANES), lambda i, j: (i, 0, 0))
    out = jax.ShapeDtypeStruct((n1, n2, d), BF16)
    return pl.pallas_call(
        _ct_mid_kernel,
        grid=(n1 // tk1, nd),
        in_specs=[data, data, coef, coef, coef, coef, mat, mat, tw, tw],
        out_specs=[data, data],
        out_shape=[out, out],
        compiler_params=_params(("parallel", "parallel"),
                                _vmem_limit(8 * _nbytes((tk1, n2, td), BF16),
                                            temps=10 * _nbytes((2 * n2, td), F32))),
        name="ct_mid",
    )(br, bi, *filt, fb, fbi, cw, sw)


def _ct_inv_a_kernel(scale, vr_ref, vi_ref, fai_ref, z0_ref, z1_ref, g0_ref, g1_ref, skip_ref,
                     o_ref):
    fai = fai_ref[...]
    half = fai.shape[0] // 2
    skip = skip_ref[...]
    vr, vi = vr_ref[...].astype(F32), vi_ref[...].astype(F32)
    zs, gs = (z0_ref[...], z1_ref[...]), (g0_ref[...], g1_ref[...])
    outs = ([], [])
    for h in range(CT_ROWS // SUB):
        v = jnp.concatenate([_sub_rows(vr, h), _sub_rows(vi, h)], axis=0).astype(BF16)
        y = _dot(fai, v) * scale
        for b, yb in enumerate((y[:half], y[half:])):
            outs[b].append(_sub_rows(gs[b], h) * (yb + skip * _sub_rows(zs[b], h)))
    for b in range(2):
        o_ref[b] = _from_sub_rows(outs[b]).astype(o_ref.dtype)


def _ct_inv_a(vr, vi, fai, z, z_which, gate, gate_which, skip, order, out_dtype):
    n1, n2, d = vr.shape
    rows = fai.shape[0] // (2 * SUB)
    tn2, tw = CT_ROWS, CT_COLS // 2
    spec = pl.BlockSpec((n1, tn2, tw), lambda i, c: (0, i, c))
    pair = lambda which, b: pl.BlockSpec((None, None, rows, tn2, tw),
                                         lambda i, c: (which[0], which[1] + b, 0, i, c))
    return pl.pallas_call(
        functools.partial(_ct_inv_a_kernel, 1.0 / (n1 * n2)),
        grid=(n2 // tn2, d // tw),
        in_specs=[spec, spec, pl.BlockSpec(fai.shape, lambda i, c: (0, 0)),
                  pair(z_which, 0), pair(z_which, 1), pair(gate_which, 0), pair(gate_which, 1),
                  pl.BlockSpec((None, 1, tw), lambda i, c: (order, 0, c))],
        out_specs=pl.BlockSpec((2, rows, tn2, tw), lambda i, c: (0, 0, i, c)),
        out_shape=jax.ShapeDtypeStruct((2, rows, n2, d), out_dtype),
        compiler_params=_params(("parallel", "parallel"),
                                _vmem_limit(2 * _nbytes((n1, tn2, tw), BF16)
                                            + 6 * _nbytes((rows, tn2, tw), F32)
                                            + _nbytes(fai.shape, BF16),
                                            temps=8 * _nbytes((n1, tn2, tw), F32))),
        name="ct_inv_a",
    )(vr, vi, fai, z, z, gate, gate, skip)


def _ct_real_b_kernel(scale, br_ref, bi_ref, fb_ref, o_ref, so_ref):
    fb = fb_ref[...]
    for j in range(br_ref.shape[0]):
        so_ref[:, j, :] = _dot(fb, jnp.concatenate([br_ref[j], bi_ref[j]], axis=0)) * scale
    o_ref[...] = so_ref[...].astype(o_ref.dtype)


def _ct_real_b(br, bi, fb_re, scale):
    n_seq, n1, n2, d = br.shape
    tk1, tw = CT_ROWS, CT_COLS
    blk = pl.BlockSpec((None, tk1, n2, tw), lambda s, i, c: (s, i, 0, c))
    return pl.pallas_call(
        functools.partial(_ct_real_b_kernel, scale),
        grid=(n_seq, n1 // tk1, d // tw),
        in_specs=[blk, blk, pl.BlockSpec(fb_re.shape, lambda s, i, c: (0, 0))],
        out_specs=pl.BlockSpec((None, n2, tk1, tw), lambda s, i, c: (s, 0, i, c)),
        out_shape=jax.ShapeDtypeStruct((n_seq, n2, n1, d), BF16),
        scratch_shapes=[pltpu.VMEM((n2, tk1, tw), F32)],
        compiler_params=_params(("parallel", "parallel", "parallel"),
                                _vmem_limit(3 * _nbytes((tk1, n2, tw), BF16),
                                            resident=_nbytes((n2, tk1, tw), F32),
                                            temps=2 * _nbytes((n2, tk1, tw), F32))),
        name="ct_real_b",
    )(br, bi, fb_re)


def _cos_sin(num, den):
    ang = (num % den).astype(F32) * (2.0 * math.pi / den)
    return jnp.cos(ang), jnp.sin(ang)


def _ct_tables(n1, n2):
    i1 = jnp.arange(n1, dtype=jnp.int32)
    i2 = jnp.arange(n2, dtype=jnp.int32)
    c1, s1 = _cos_sin(i1[:, None] * i1[None, :], n1)
    c2, s2 = _cos_sin(i2[:, None] * i2[None, :], n2)
    cw, sw = _cos_sin(i2[:, None] * i1[None, :], n1 * n2)
    lanes = lambda t: jnp.broadcast_to(t[..., None], t.shape + (TW_LANES,))

    def stage_a_rows(t):
        t = t.reshape(n2 // CT_ROWS, CT_ROWS // SUB, SUB, n1)
        return lanes(jnp.swapaxes(t, 2, 3).reshape(n2 // CT_ROWS, CT_ROWS // SUB, n1 * SUB))

    return {"c1": c1, "s1": s1, "c2": c2, "s2": s2,
            "cw_a": stage_a_rows(cw), "sw_a": stage_a_rows(sw),
            "cw_b": lanes(cw.T), "sw_b": lanes(sw.T)}


def _filter_rows(feat_ref, t_ref, w1_ref, b1_ref, fr1_ref, w2_ref, b2_ref, fr2_ref, w3_ref,
                 decay_ref):
    x = jnp.sin(fr1_ref[...] * (_dot(feat_ref[...].astype(BF16), w1_ref[...].astype(BF16))
                                + b1_ref[...]))
    x = jnp.sin(fr2_ref[...] * (_dot(x.astype(BF16), w2_ref[...].astype(BF16)) + b2_ref[...]))
    h = _dot(x.astype(BF16), w3_ref[...].astype(BF16))
    return h * (jnp.exp(-t_ref[...] * jnp.exp(decay_ref[...])) + HY_SHIFT)


def _filter_stats_kernel(*refs):
    ss_ref = refs[-1]
    h = _filter_rows(*refs[:-1])

    @pl.when(pl.program_id(0) == 0)
    def _():
        ss_ref[...] = jnp.zeros_like(ss_ref)

    ss_ref[...] += jnp.sum(h * h, axis=0, keepdims=True)


def _filter_emit_kernel(*refs):
    ss_ref, a_ref, b_ref = refs[-3:]
    h = _filter_rows(*refs[:-3])
    n_dir = a_ref.shape[1]
    ss = ss_ref[...]
    norm = lax.rsqrt(ss[:, :n_dir] + ss[:, n_dir:] + 1e-12)
    fwd = h[:, :n_dir] * norm
    bwd = h[:, n_dir:] * norm
    row = lax.broadcasted_iota(jnp.int32, bwd.shape, 0) + pl.program_id(0) * bwd.shape[0]
    bwd = jnp.where(row == 0, 0.0, bwd)
    a_ref[...] = (fwd + bwd).astype(BF16)
    b_ref[...] = (fwd - bwd).astype(BF16)


def _hyena_filters(length, f_w1, f_b1, f_freq1, f_w2, f_b2, f_freq2, f_w3, log_decay):
    t = jnp.linspace(0.0, 1.0, length, dtype=F32)[:, None]
    t_idx = jnp.arange(length, dtype=F32)[:, None]
    bands = jnp.linspace(1e-4, HY_BANDS - 1, HY_BANDS, dtype=F32)
    w = 2.0 * math.pi * t_idx * bands / length
    feat = jnp.concatenate([t, jnp.cos(w), -jnp.sin(w)], axis=-1)
    emb_pad = 128
    feat = jnp.pad(feat, ((0, 0), (0, emb_pad - HY_EMB)))
    w1 = jnp.pad(f_w1, ((0, emb_pad - HY_EMB), (0, 0)))
    n_all = HY_DIRS * HY_ORDER * D_MODEL
    n_dir = HY_ORDER * D_MODEL
    tm = 256
    full = lambda shape: pl.BlockSpec(shape, lambda i: (0, 0))
    mlp_specs = [pl.BlockSpec((tm, emb_pad), lambda i: (i, 0)),
                 pl.BlockSpec((tm, 1), lambda i: (i, 0)),
                 full((emb_pad, HY_FW)), full((1, HY_FW)), full((1, HY_FW)),
                 full((HY_FW, HY_FW)), full((1, HY_FW)), full((1, HY_FW)),
                 full((HY_FW, n_all)), full((1, n_all))]
    mlp_args = (feat, t, w1, f_b1.reshape(1, HY_FW), f_freq1.reshape(1, HY_FW), f_w2,
                f_b2.reshape(1, HY_FW), f_freq2.reshape(1, HY_FW), f_w3, log_decay.reshape(1, n_all))
    vmem = _vmem_limit(_nbytes((tm, n_all), F32) + _nbytes((HY_FW, n_all), F32),
                       temps=4 * _nbytes((tm, n_all), F32))
    ss = pl.pallas_call(
        _filter_stats_kernel,
        grid=(length // tm,),
        in_specs=mlp_specs,
        out_specs=full((1, n_all)),
        out_shape=jax.ShapeDtypeStruct((1, n_all), F32),
        compiler_params=_params(("arbitrary",), vmem),
        name="filter_stats",
    )(*mlp_args)
    comb = jax.ShapeDtypeStruct((length, n_dir), BF16)
    return pl.pallas_call(
        _filter_emit_kernel,
        grid=(length // tm,),
        in_specs=mlp_specs + [full((1, n_all))],
        out_specs=[pl.BlockSpec((tm, n_dir), lambda i: (i, 0)), pl.BlockSpec((tm, n_dir), lambda i: (i, 0))],
        out_shape=[comb, comb],
        compiler_params=_params(("parallel",), vmem),
        name="filter_emit",
    )(*mlp_args, ss)


def _cis_product(row_hi, row_lo, period):
    def cis(phase):
        ang = (phase % period).astype(F32) * (2.0 * math.pi / period)
        return jnp.cos(ang)[:, :, None], jnp.sin(ang)[:, :, None]
    (c1, s1), (c0, s0) = cis(row_hi), cis(row_lo)
    c0, s0 = jnp.swapaxes(c0, 1, 2), jnp.swapaxes(s0, 1, 2)
    rows = row_hi.shape[0]
    return ((c1 * c0 - s1 * s0).reshape(rows, -1), (s1 * c0 + c1 * s0).reshape(rows, -1))


def _odd_dft_tables(length):
    split = 1 << (length.bit_length() // 2)
    r = jnp.arange(length, dtype=jnp.int32)[:, None]
    hi = jnp.arange(length // split, dtype=jnp.int32)[None, :] * split
    lo = jnp.arange(split, dtype=jnp.int32)[None, :]
    c, s = _cis_product((2 * r + 1) * hi, (2 * r + 1) * lo, 4 * length)
    ct, st = _cis_product(r * (2 * hi), r * (2 * lo + 1), 4 * length)
    return {"c": c.astype(BF16), "s": s.astype(BF16), "ct": ct.astype(BF16), "st": st.astype(BF16)}


def _dft_tables(length):
    split = 1 << (length.bit_length() // 2)
    r = jnp.arange(length, dtype=jnp.int32)[:, None]
    hi = jnp.arange(length // split, dtype=jnp.int32)[None, :] * split
    lo = jnp.arange(split, dtype=jnp.int32)[None, :]
    c, s = _cis_product(r * hi, r * lo, length)
    return c.astype(BF16), (-s).astype(BF16)


def _hyena_mix_dense(pc, filt_p, filt_m, skip, n_seq, length):
    tabs = _odd_dft_tables(length)
    kr, ks = _mm(tabs["c"], filt_p), _mm(tabs["s"], filt_m)
    yr, ys = _dft_fwd(tabs, pc, 0, kr, ks, 0, n_seq, length)
    z1 = _dft_inv(tabs, yr, ys, pc, 0, pc, 1, skip, 0, n_seq, length, F32)[None]
    yr, ys = _dft_fwd(tabs, z1, 0, kr, ks, 1, n_seq, length)
    return _dft_inv(tabs, yr, ys, z1, 0, pc, 2, skip, 1, n_seq, length, BF16)


HY_N1, HY_N2 = 64, 128


def _hyena_mix_pair(pc, filt_p, filt_m, skip):
    assert DEC_BATCH == 2 and HY_N1 * HY_N2 == 2 * DEC_SEQ and ROWS % DEC_SEQ == 0
    n1, n2, rows_in = HY_N1, HY_N2, HY_N1 // 2
    t = _ct_tables(n1, n2)
    c_in, s_in = t["c1"][:, :rows_in], t["s1"][:, :rows_in]
    fa = _kron_sub(jnp.block([[c_in, s_in], [-s_in, c_in]])).astype(BF16)
    fa_real = _kron_sub(jnp.concatenate([c_in, -s_in], axis=0)).astype(BF16)
    fb = jnp.block([[t["c2"], t["s2"]], [-t["s2"], t["c2"]]]).astype(BF16)
    fbi = jnp.block([[t["c2"], -t["s2"]], [t["s2"], t["c2"]]]).astype(BF16)
    c_out, s_out = t["c1"][:rows_in], t["s1"][:rows_in]
    fai = _kron_sub(jnp.block([[c_out, -s_out], [s_out, c_out]])).astype(BF16)
    n_filt = HY_ORDER * D_MODEL
    filt = []
    for f in (filt_p, filt_m):
        re, im = _ct_stage_a([(f.reshape(rows_in, n2, n_filt), lambda s: ())], 1, fa_real,
                             t["cw_a"], t["sw_a"])
        filt += [re[0], im[0]]

    pc = pc.reshape(3, ROWS // DEC_SEQ, rows_in, n2, D_MODEL)
    z, z0, out_dtypes = pc, ROWS_CTX // DEC_SEQ, (F32, BF16)
    for order in range(HY_ORDER):
        br, bi = _ct_stage_a([(z, lambda s, z0=z0: (0, z0)), (z, lambda s, z0=z0: (0, z0 + 1))], 1, fa,
                             t["cw_a"], t["sw_a"])
        vr, vi = _ct_mid(br[0], bi[0], filt, order, fb, fbi, t["cw_b"], t["sw_b"])
        z = _ct_inv_a(vr, vi, fai, z, (0, z0), pc, (1 + order, ROWS_CTX // DEC_SEQ), skip, order,
                      out_dtypes[order])[None]
        z0 = 0
    return z.reshape(ROWS_LAT, D_MODEL)


def _fnet_chan_kernel(h_ref, sh_ref, sc_ref, w_ref, p_ref, q_ref):
    u = _modulate(h_ref[...], sh_ref[...], sc_ref[...]).astype(BF16)
    w = w_ref[...]
    for g in range(FNET_GROUPS):
        cols = slice(g * FNET_CG, (g + 1) * FNET_CG)
        r = _dot(u[:, cols], w)
        p_ref[:, cols] = r[:, :FNET_CG].astype(BF16)
        q_ref[:, cols] = r[:, FNET_CG:].astype(BF16)


def _fnet_chan(h, mods, w_cs, tm=512):
    row = pl.BlockSpec((tm, D_MODEL), lambda i: (i, 0))
    out = jax.ShapeDtypeStruct((ROWS, D_MODEL), BF16)
    return pl.pallas_call(
        _fnet_chan_kernel,
        grid=(ROWS // tm,),
        in_specs=[row, _mod_spec(0, tm), _mod_spec(1, tm),
                  pl.BlockSpec((FNET_CG, 2 * FNET_CG), lambda i: (0, 0))],
        out_specs=[row, row],
        out_shape=[out, out],
        compiler_params=_params(("parallel",),
                                _vmem_limit(3 * _nbytes((tm, D_MODEL), F32),
                                            temps=2 * _nbytes((tm, D_MODEL), F32))),
        name="fnet_chan",
    )(h, mods, mods, w_cs)


def _fnet_pos_kernel(scale, c_ref, ns_ref, p_ref, q_ref, o_ref, acc_ref):
    k = pl.program_id(3)

    @pl.when(k == 0)
    def _():
        acc_ref[...] = jnp.zeros_like(acc_ref)

    acc_ref[...] += (_dot(c_ref[...], p_ref[...].astype(BF16))
                     + _dot(ns_ref[...], q_ref[...].astype(BF16)))

    @pl.when(k == pl.num_programs(3) - 1)
    def _():
        o_ref[...] = (acc_ref[...] * scale).astype(o_ref.dtype)


def _fnet_pos(c_tab, ns_tab, p, q, n_seq, length):
    tt, tn, tk = _seq_tiles(length)
    nb, nt, nk = D_MODEL // tn, length // tt, length // tk
    scale = (length * FNET_CG) ** -0.5
    return pl.pallas_call(
        functools.partial(_fnet_pos_kernel, scale),
        grid=(n_seq, nt, nb, nk),
        in_specs=[pl.BlockSpec((tt, tk), lambda s, t, n, k: (t, k)),
                  pl.BlockSpec((tt, tk), lambda s, t, n, k: (t, k)),
                  pl.BlockSpec((tk, tn), lambda s, t, n, k: (s * nk + k, n)),
                  pl.BlockSpec((tk, tn), lambda s, t, n, k: (s * nk + k, n))],
        out_specs=pl.BlockSpec((tt, tn), lambda s, t, n, k: (s * nt + t, n)),
        out_shape=jax.ShapeDtypeStruct((n_seq * length, D_MODEL), BF16),
        scratch_shapes=[pltpu.VMEM((tt, tn), F32)],
        compiler_params=_params(("parallel", "parallel", "parallel", "arbitrary"),
                                _vmem_limit(2 * _nbytes((tt, tk), BF16) + 2 * _nbytes((tk, tn), BF16)
                                            + _nbytes((tt, tn), BF16),
                                            resident=_nbytes((tt, tn), F32),
                                            temps=2 * _nbytes((tt, tn), F32))),
        name="fnet_pos",
    )(c_tab, ns_tab, p, q)


FN_N1, FN_N2 = 32, 128


def _fnet_pos_factored(p, q):
    assert FN_N1 * FN_N2 == DEC_SEQ and ROWS % DEC_SEQ == 0
    n1, n2 = FN_N1, FN_N2
    t = _ct_tables(n1, n2)
    fa = _kron_sub(jnp.block([[t["c1"], -t["s1"]], [-t["s1"], -t["c1"]]])).astype(BF16)
    fb_re = jnp.concatenate([t["c2"], t["s2"]], axis=1).astype(BF16)
    lat0 = ROWS_CTX // DEC_SEQ
    view = lambda x: x.reshape(ROWS // DEC_SEQ, n1, n2, D_MODEL)
    seq = lambda s: (lat0 + s,)
    br, bi = _ct_stage_a([(view(p), seq), (view(q), seq)], DEC_BATCH, fa, t["cw_a"], t["sw_a"])
    f = _ct_real_b(br, bi, fb_re, (DEC_SEQ * FNET_CG) ** -0.5)
    return f.reshape(ROWS_LAT, D_MODEL)


def _rope_table():
    rows = DEC_SEQ // GRID_W
    row = jnp.repeat(jnp.arange(rows), GRID_W).astype(F32)
    col = jnp.tile(jnp.arange(GRID_W), rows).astype(F32)
    inv = ROPE_THETA ** (-jnp.arange(0, AXIS_ROPE, 2, dtype=F32) / AXIS_ROPE)
    ang = jnp.concatenate([row[:, None] * inv, col[:, None] * inv], axis=-1)
    cos = jnp.repeat(jnp.cos(ang), 2, axis=-1)
    sin = jnp.repeat(jnp.sin(ang), 2, axis=-1)
    lat = jnp.tile(jnp.concatenate([cos, sin], axis=-1), (DEC_BATCH, 1))
    ctx = jnp.concatenate([jnp.ones((ROWS_CTX, QK_ROPE), F32), jnp.zeros((ROWS_CTX, QK_ROPE), F32)],
                          axis=-1)
    return jnp.concatenate([ctx, lat], axis=0)


def _pair_rotated(w):
    pairs = w.reshape(w.shape[:-1] + (QK_ROPE // 2, 2))
    return jnp.stack([-pairs[..., 1], pairs[..., 0]], axis=-1).reshape(w.shape)


def kernel(x_prompt, x_sample, c, cache_ckv, cache_krope, c_ctx, ada_w, ada_b, ln_g, ln_b, ffn_w_gate, ffn_w_up, ffn_w_down, mla_w_dq, mla_q_norm, mla_w_uq, mla_w_dkv, mla_kv_norm, mla_w_kr, mla_w_ukv, mla_w_o, hy_w_in, hy_b_in, hy_conv_w, hy_conv_b, hy_f_w1, hy_f_b1, hy_f_freq1, hy_f_w2, hy_f_b2, hy_f_freq2, hy_f_w3, hy_log_decay, hy_skip, hy_w_out, hy_b_out, fn_w_out, fn_b_out):
    assert x_prompt.shape == (BATCH, SEQ, D_MODEL) and x_sample.shape == (DEC_BATCH, DEC_SEQ, D_MODEL)
    assert ROWS_CTX % DEC_SEQ == 0 and SEQ == FNET_CG

    assert N_MIXERS > 0 and DEPTH > 1
    h = (x_prompt.reshape(ROWS_CTX, D_MODEL), x_sample.reshape(ROWS_LAT, D_MODEL))
    cond = jnp.concatenate([c_ctx[None, :], c, jnp.zeros((COND_PAD - N_COND, D_MODEL), F32)])
    mods_all = _modulation_vectors(cond, ada_w, ada_b)
    zero_bias = jnp.zeros((D_MODEL,), F32)
    ffn_w = (ffn_w_gate.astype(BF16), ffn_w_up.astype(BF16), ffn_w_down.astype(BF16))
    rope_tab = None
    ckv_states, krope_states = [], []

    for i in range(DEPTH):
        kind, j = i % N_MIXERS, i // N_MIXERS
        mods = mods_all[i]
        if kind == 0:
            if rope_tab is None:
                rope_tab = _rope_table()
            w_kr2 = jnp.concatenate([mla_w_kr[j], _pair_rotated(mla_w_kr[j])], axis=-1).astype(BF16)
            wq = mla_w_uq[j].reshape(Q_RANK, MLA_HEADS, QK_NOPE + QK_ROPE)
            w_q = jnp.concatenate([wq, _pair_rotated(wq[..., QK_NOPE:])], axis=-1)
            w_q = w_q.reshape(Q_RANK, MLA_HEADS * HEAD_W).astype(BF16)
            w_ukv = mla_w_ukv[j].reshape(KV_RANK, MLA_HEADS, QK_NOPE + V_DIM)
            w_k = w_ukv[..., :QK_NOPE].reshape(KV_RANK, MLA_HEADS * QK_NOPE).astype(BF16)
            w_vt = w_ukv[..., QK_NOPE:].reshape(KV_RANK, MLA_HEADS * V_DIM).T.astype(BF16)
            cq, ckv, kr, kr2 = _mla_down(h, mods, mla_w_dq[j].astype(BF16), mla_w_dkv[j].astype(BF16),
                                         w_kr2, mla_q_norm[j], mla_kv_norm[j], rope_tab)
            ckv_states.append(ckv[:ROWS_CTX].reshape(BATCH, SEQ, KV_RANK))
            krope_states.append(kr[:ROWS_CTX].reshape(BATCH, SEQ, QK_ROPE))
            q = _q_up(cq, w_q, rope_tab)
            k_tok, vt_tok = _kv_expand(ckv, kr2, w_k, w_vt)
            kc = cache_krope[:, j].reshape(DEC_BATCH * PAST_LEN, QK_ROPE).astype(BF16)
            k_cache, vt_cache = _kv_expand(cache_ckv[:, j].reshape(DEC_BATCH * PAST_LEN, KV_RANK),
                                           jnp.concatenate([kc, kc], axis=-1), w_k, w_vt)
            o = _attention(q, k_tok, vt_tok, k_cache, vt_cache)
            h = _mm_postnorm(o, mla_w_o[j].astype(BF16), zero_bias, h, mods, 2, ln_g[i, 0], ln_b[i, 0])
        elif kind == 1:
            pc = _hyena_in(h, mods, hy_w_in[j].astype(BF16), hy_b_in[j], hy_conv_w[j], hy_conv_b[j])
            fp = (hy_f_w1[j], hy_f_b1[j], hy_f_freq1[j], hy_f_w2[j], hy_f_b2[j], hy_f_freq2[j],
                  hy_f_w3[j], hy_log_decay[j])
            skip = hy_skip[j].reshape(HY_ORDER, 1, D_MODEL)
            z_ctx = _hyena_mix_dense(pc, *_hyena_filters(SEQ, *fp), skip, BATCH, SEQ)
            z_lat = _hyena_mix_pair(pc, *_hyena_filters(DEC_SEQ, *fp), skip)
            h = _mm_postnorm((z_ctx, z_lat), hy_w_out[j].astype(BF16), hy_b_out[j],
                             h, mods, 2, ln_g[i, 0], ln_b[i, 0])
        else:
            c_ch, ns_ch = _dft_tables(FNET_CG)
            p, q = _fnet_chan(h, mods, jnp.concatenate([c_ch, -ns_ch], axis=-1))
            f_ctx = _fnet_pos(c_ch, ns_ch, p, q, BATCH, SEQ)
            f_lat = _fnet_pos_factored(p, q)
            h = _mm_postnorm((f_ctx, f_lat), fn_w_out[j].astype(BF16), fn_b_out[j],
                             h, mods, 2, ln_g[i, 0], ln_b[i, 0])
        h = _ffn(h, mods, *ffn_w, i, ln_g[i, 1], ln_b[i, 1], split_out=i == DEPTH - 1)

    y_prompt = h[0].reshape(BATCH, SEQ, D_MODEL)
    y_sample = h[1].reshape(DEC_BATCH, DEC_SEQ, D_MODEL)
    return (y_prompt, y_sample, jnp.stack(ckv_states, axis=1), jnp.stack(krope_states, axis=1))
```

```python
import functools
import math

import jax
import jax.numpy as jnp
from jax import lax
from jax.experimental import pallas as pl
from jax.experimental.pallas import tpu as pltpu

F32 = jnp.float32
BF16 = jnp.bfloat16

D_MODEL = 2048
BATCH = 16
SEQ = 256
DEPTH = 4
DEC_BATCH = 2
DEC_SEQ = 4096
PAST_LEN = 512
GRID_W = 64
N_MIXERS = 3
MLA_HEADS = 16
QK_NOPE = 128
QK_ROPE = 64
V_DIM = 128
Q_RANK = 512
KV_RANK = 512
ROPE_THETA = 10000.0
AXIS_ROPE = QK_ROPE // 2
HY_ORDER = 2
HY_DIRS = 2
HY_CONV = 3
HY_BANDS = 16
HY_EMB = 1 + 2 * HY_BANDS
HY_FW = 64
HY_SHIFT = 0.05
FNET_GROUPS = 8
FNET_CG = D_MODEL // FNET_GROUPS
D_FF = -(-8 * D_MODEL // (3 * 256)) * 256
DN_ALPHA = (2 * DEPTH) ** 0.25
LN_EPS = 1e-5
RMS_EPS = 1e-6
N_MOD = 6

ROWS_CTX = BATCH * SEQ
ROWS_LAT = DEC_BATCH * DEC_SEQ
ROWS = ROWS_CTX + ROWS_LAT
N_COND = 1 + DEC_BATCH
COND_PAD = 8
HEAD_W = QK_NOPE + 2 * QK_ROPE
ATT_SCALE = (QK_NOPE + QK_ROPE) ** -0.5

V7X_VMEM_BYTES = 64 * 2 ** 20
VMEM_CAP_BYTES = V7X_VMEM_BYTES * 7 // 8
VMEM_FLOOR_BYTES = 32 * 2 ** 20


def _vmem_limit(pipelined, resident=0, temps=0):
    est = 2 * pipelined + resident + temps
    return int(min(max(est, VMEM_FLOOR_BYTES), VMEM_CAP_BYTES))


def _params(semantics, vmem):
    return pltpu.CompilerParams(dimension_semantics=semantics, vmem_limit_bytes=vmem)


def _nbytes(shape, dtype):
    return math.prod(shape) * jnp.dtype(dtype).itemsize


def _group_of_tile(i, tm):
    n_ctx = ROWS_CTX // tm
    return jnp.where(i < n_ctx, 0, 1 + (i - n_ctx) // (DEC_SEQ // tm))


def _mod_spec(which, tm):
    return pl.BlockSpec((None, 1, D_MODEL),
                        lambda i, *_: (which * COND_PAD + _group_of_tile(i, tm), 0, 0))


def _row_spec(width=D_MODEL):
    return pl.BlockSpec((1, width), lambda *_: (0, 0))


def _as_tuple(x):
    return x if isinstance(x, tuple) else (x,)


def _token_specs(xs, tm):
    width = xs[0].shape[1]
    if len(xs) == 1:
        return [pl.BlockSpec((tm, width), lambda i, *_: (i, 0))]
    n_ctx = ROWS_CTX // tm
    return [pl.BlockSpec((tm, width), lambda i, *_: (jnp.minimum(i, n_ctx - 1), 0)),
            pl.BlockSpec((tm, width), lambda i, *_: (jnp.maximum(i - n_ctx, 0), 0))]


def _token_rows(refs, r):
    if len(refs) == 1:
        return refs[0][r, :]
    n_ctx = ROWS_CTX // refs[0].shape[0]
    return jnp.where(pl.program_id(0) < n_ctx, refs[0][r, :], refs[1][r, :])


def _modulate(h, shift, scale):
    return h * (1.0 + scale) + shift


def _post_norm(h, delta, g, b):
    z = DN_ALPHA * h + delta
    mu = jnp.mean(z, axis=-1, keepdims=True)
    zc = z - mu
    var = jnp.mean(zc * zc, axis=-1, keepdims=True)
    return zc * lax.rsqrt(var + LN_EPS) * g + b


def _rms_norm(x, g):
    ms = jnp.mean(x * x, axis=-1, keepdims=True)
    return x * lax.rsqrt(ms + RMS_EPS) * g


def _dot(a, b):
    return jnp.dot(a, b, preferred_element_type=F32)


def _modvec_kernel(c_ref, w_ref, b_ref, o_ref):
    a = jax.nn.silu(c_ref[...]).astype(BF16)
    o_ref[...] = _dot(a, w_ref[...].astype(BF16)) + b_ref[...]


def _modulation_vectors(cond, ada_w, ada_b):
    tn = 1024
    n = N_MOD * D_MODEL
    out = pl.pallas_call(
        _modvec_kernel,
        grid=(DEPTH, n // tn),
        in_specs=[pl.BlockSpec((COND_PAD, D_MODEL), lambda l, j: (0, 0)),
                  pl.BlockSpec((None, D_MODEL, tn), lambda l, j: (l, 0, j)),
                  pl.BlockSpec((None, 1, tn), lambda l, j: (l, 0, j))],
        out_specs=pl.BlockSpec((None, COND_PAD, tn), lambda l, j: (l, 0, j)),
        out_shape=jax.ShapeDtypeStruct((DEPTH, COND_PAD, n), F32),
        compiler_params=_params(("parallel", "parallel"),
                                _vmem_limit(_nbytes((D_MODEL, tn), F32),
                                            temps=_nbytes((D_MODEL, tn), BF16))),
        name="modvec",
    )(cond, ada_w, ada_b.reshape(DEPTH, 1, n))
    out = out.reshape(DEPTH, COND_PAD, N_MOD, D_MODEL).transpose(0, 2, 1, 3)
    return out.reshape(DEPTH, N_MOD * COND_PAD, 1, D_MODEL)


def _hyena_in_kernel(n_ctx, h_ref, hp_ref, hn_ref, sh_ref, sc_ref, w_ref, b_ref, cw_ref, cb_ref,
                     o_ref, u_ref):
    i = pl.program_id(0)
    tm = h_ref.shape[0]

    @pl.when(pl.program_id(1) == 0)
    def _():
        mod = lambda x_ref: _modulate(x_ref[...], sh_ref[...], sc_ref[...]).astype(BF16)
        u_ref[:SUB, :] = mod(hp_ref)
        u_ref[SUB:SUB + tm, :] = mod(h_ref)
        u_ref[SUB + tm:, :] = mod(hn_ref)

    length = jnp.where(i < n_ctx, SEQ, DEC_SEQ)
    half = tm // 2
    starts = (0, half)
    ys = [_dot(u_ref[r0:r0 + half + 2 * SUB, :], w_ref[...]) + b_ref[...] for r0 in starts]
    for r0, y in zip(starts, ys):
        inner = slice(SUB, SUB + half)
        before = pltpu.roll(y, 1, 0)[inner]
        after = pltpu.roll(y, y.shape[0] - 1, 0)[inner]
        pos = (lax.broadcasted_iota(jnp.int32, before.shape, 0) + (i * tm + r0)) & (length - 1)
        before = jnp.where(pos == 0, 0.0, before)
        after = jnp.where(pos == length - 1, 0.0, after)
        o_ref[r0:r0 + half, :] = (cw_ref[0:1, :] * before + cw_ref[1:2, :] * y[inner]
                                  + cw_ref[2:3, :] * after + cb_ref[...])


def _hyena_in(h, mods, w, b, conv_w, conv_b, tm=1024, tn=1024):
    assert SEQ & (SEQ - 1) == 0 and DEC_SEQ & (DEC_SEQ - 1) == 0 and tm % SEQ == 0 and DEC_SEQ % tm == 0
    k, n = w.shape
    per, halo, last = D_MODEL // tn, tm // SUB, ROWS // SUB - 1
    return pl.pallas_call(
        functools.partial(_hyena_in_kernel, ROWS_CTX // tm),
        grid=(ROWS // tm, n // tn),
        in_specs=[pl.BlockSpec((tm, k), lambda i, j: (i, 0)),
                  pl.BlockSpec((SUB, k), lambda i, j: (jnp.maximum(i * halo - 1, 0), 0)),
                  pl.BlockSpec((SUB, k), lambda i, j: (jnp.minimum((i + 1) * halo, last), 0)),
                  _mod_spec(0, tm), _mod_spec(1, tm),
                  pl.BlockSpec((k, tn), lambda i, j: (0, j)),
                  pl.BlockSpec((1, tn), lambda i, j: (0, j)),
                  pl.BlockSpec((HY_CONV, tn), lambda i, j: (0, j)),
                  pl.BlockSpec((1, tn), lambda i, j: (0, j))],
        out_specs=pl.BlockSpec((None, tm, tn), lambda i, j: (j // per, i, j % per)),
        out_shape=jax.ShapeDtypeStruct((n // D_MODEL, ROWS, D_MODEL), F32),
        scratch_shapes=[pltpu.VMEM((tm + 2 * SUB, k), BF16)],
        compiler_params=_params(("parallel", "arbitrary"),
                                _vmem_limit(_nbytes((tm, k), F32) + _nbytes((k, tn), BF16)
                                            + _nbytes((tm, tn), F32),
                                            resident=_nbytes((tm, k), BF16),
                                            temps=5 * _nbytes((tm, tn), F32))),
        name="hyena_in",
    )(h, h, h, mods, mods, w, b.reshape(1, n), conv_w, conv_b.reshape(1, n))


def _mm_postnorm_kernel(n_a, n_h, *refs):
    a_refs, refs = refs[:n_a], refs[n_a:]
    (w_ref, bias_ref), refs = refs[:2], refs[2:]
    h_refs, (gate_ref, g_ref, b_ref, o_ref) = refs[:n_h], refs[n_h:]
    half = o_ref.shape[0] // 2
    halves = (slice(0, half), slice(half, 2 * half))
    ys = [_dot(_token_rows(a_refs, r), w_ref[...]) + bias_ref[...] for r in halves]
    for r, y in zip(halves, ys):
        o_ref[r, :] = _post_norm(_token_rows(h_refs, r), gate_ref[...] * y, g_ref[...], b_ref[...])


def _mm_postnorm(a, w, bias, h, mods, which_gate, ln_g, ln_b, tm=512):
    a, h = _as_tuple(a), _as_tuple(h)
    k = a[0].shape[1]
    return pl.pallas_call(
        functools.partial(_mm_postnorm_kernel, len(a), len(h)),
        grid=(ROWS // tm,),
        in_specs=(_token_specs(a, tm) + [pl.BlockSpec((k, D_MODEL), lambda i: (0, 0)), _row_spec()]
                  + _token_specs(h, tm) + [_mod_spec(which_gate, tm), _row_spec(), _row_spec()]),
        out_specs=pl.BlockSpec((tm, D_MODEL), lambda i: (i, 0)),
        out_shape=jax.ShapeDtypeStruct((ROWS, D_MODEL), F32),
        compiler_params=_params(("parallel",),
                                _vmem_limit(len(a) * _nbytes((tm, k), BF16)
                                            + _nbytes((k, D_MODEL), BF16)
                                            + (1 + len(h)) * _nbytes((tm, D_MODEL), F32),
                                            temps=3 * _nbytes((tm, D_MODEL), F32))),
        name="mm_postnorm",
    )(*a, w, bias.reshape(1, D_MODEL), *h, mods, ln_g.reshape(1, D_MODEL), ln_b.reshape(1, D_MODEL))


def _ffn_kernel(n_ctx, h_ref, sh_ref, sc_ref, gate_ref, g_ref, b_ref, wg_ref, wu_ref, wd_ref,
                *refs):
    i, f = pl.program_id(0), pl.program_id(1)
    if n_ctx is None:
        o_ref, u_ref = refs
        acc_ref, outs = o_ref, ((o_ref, None),)
    else:
        octx_ref, olat_ref, u_ref, acc_ref = refs
        outs = ((octx_ref, i < n_ctx), (olat_ref, i >= n_ctx))

    @pl.when(f == 0)
    def _():
        u_ref[...] = _modulate(h_ref[...], sh_ref[...], sc_ref[...]).astype(BF16)
        acc_ref[...] = jnp.zeros_like(acc_ref)

    half = u_ref.shape[0] // 2
    halves = (slice(0, half), slice(half, 2 * half))
    proj = [(_dot(u_ref[r, :], wg_ref[...]), _dot(u_ref[r, :], wu_ref[...])) for r in halves]
    for r, (gate, up) in zip(halves, proj):
        act = (jax.nn.silu(gate) * up).astype(BF16)
        acc_ref[r, :] += _dot(act, wd_ref[...])

    last = f == pl.num_programs(1) - 1
    for o_ref, mine in outs:
        @pl.when(last if mine is None else jnp.logical_and(last, mine))
        def _(o_ref=o_ref):
            for r in halves:
                o_ref[r, :] = _post_norm(h_ref[r, :], gate_ref[...] * acc_ref[r, :],
                                         g_ref[...], b_ref[...])


def _ffn(h, mods, w_gate, w_up, w_down, layer, ln_g, ln_b, split_out, tm=512, tf=512):
    tile = _nbytes((tm, D_MODEL), F32)
    if split_out:
        n_ctx = ROWS_CTX // tm
        out_specs = [pl.BlockSpec((tm, D_MODEL), lambda i, f: (jnp.minimum(i, n_ctx - 1), 0)),
                     pl.BlockSpec((tm, D_MODEL), lambda i, f: (jnp.maximum(i - n_ctx, 0), 0))]
        out_shape = [jax.ShapeDtypeStruct((ROWS_CTX, D_MODEL), F32),
                     jax.ShapeDtypeStruct((ROWS_LAT, D_MODEL), F32)]
        scratch = [pltpu.VMEM((tm, D_MODEL), BF16), pltpu.VMEM((tm, D_MODEL), F32)]
        pipelined, resident = 3 * tile, tile + tile // 2
    else:
        n_ctx = None
        out_specs = pl.BlockSpec((tm, D_MODEL), lambda i, f: (i, 0))
        out_shape = jax.ShapeDtypeStruct((ROWS, D_MODEL), F32)
        scratch = [pltpu.VMEM((tm, D_MODEL), BF16)]
        pipelined, resident = 2 * tile, tile // 2
    return pl.pallas_call(
        functools.partial(_ffn_kernel, n_ctx),
        grid=(ROWS // tm, D_FF // tf),
        in_specs=[pl.BlockSpec((tm, D_MODEL), lambda i, f: (i, 0)),
                  _mod_spec(3, tm), _mod_spec(4, tm), _mod_spec(5, tm),
                  _row_spec(), _row_spec(),
                  pl.BlockSpec((None, D_MODEL, tf), lambda i, f: (layer, 0, f)),
                  pl.BlockSpec((None, D_MODEL, tf), lambda i, f: (layer, 0, f)),
                  pl.BlockSpec((None, tf, D_MODEL), lambda i, f: (layer, f, 0))],
        out_specs=out_specs,
        out_shape=out_shape,
        scratch_shapes=scratch,
        compiler_params=_params(("arbitrary" if split_out else "parallel", "arbitrary"),
                                _vmem_limit(pipelined + 3 * _nbytes((D_MODEL, tf), BF16),
                                            resident=resident, temps=2 * tile)),
        name="ffn",
    )(h, mods, mods, mods, ln_g.reshape(1, D_MODEL), ln_b.reshape(1, D_MODEL),
      w_gate, w_up, w_down)


def _mla_down_kernel(n_h, *refs):
    h_refs, (sh_ref, sc_ref, wdq_ref, wdkv_ref, wkr_ref, qn_ref, kvn_ref, rope_ref,
             cq_ref, ckv_ref, kr_ref, kr2_ref) = refs[:n_h], refs[n_h:]
    half = cq_ref.shape[0] // 2
    halves = (slice(0, half), slice(half, 2 * half))
    us = [_modulate(_token_rows(h_refs, r), sh_ref[...], sc_ref[...]).astype(BF16) for r in halves]
    projs = [(_dot(u, wdq_ref[...]), _dot(u, wdkv_ref[...]), _dot(u, wkr_ref[...])) for u in us]
    for r, (q_lat, kv_lat, t) in zip(halves, projs):
        cq_ref[r, :] = _rms_norm(q_lat, qn_ref[...]).astype(BF16)
        ckv_ref[r, :] = _rms_norm(kv_lat, kvn_ref[...])
        kr_ref[r, :] = t[:, :QK_ROPE]
        v = t * rope_ref[r, :]
        kr2_ref[r, :] = (v + pltpu.roll(v, QK_ROPE, 1)).astype(BF16)


def _mla_down(h, mods, w_dq, w_dkv, w_kr2, q_norm, kv_norm, rope_tab, tm=512):
    h = _as_tuple(h)
    row = lambda width: pl.BlockSpec((tm, width), lambda i: (i, 0))
    full = lambda shape: pl.BlockSpec(shape, lambda i: (0, 0))
    return pl.pallas_call(
        functools.partial(_mla_down_kernel, len(h)),
        grid=(ROWS // tm,),
        in_specs=_token_specs(h, tm) + [
            _mod_spec(0, tm), _mod_spec(1, tm),
            full((D_MODEL, Q_RANK)), full((D_MODEL, KV_RANK)), full((D_MODEL, 2 * QK_ROPE)),
            _row_spec(Q_RANK), _row_spec(KV_RANK), row(2 * QK_ROPE)],
        out_specs=[row(Q_RANK), row(KV_RANK), row(QK_ROPE), row(2 * QK_ROPE)],
        out_shape=[jax.ShapeDtypeStruct((ROWS, Q_RANK), BF16),
                   jax.ShapeDtypeStruct((ROWS, KV_RANK), F32),
                   jax.ShapeDtypeStruct((ROWS, QK_ROPE), F32),
                   jax.ShapeDtypeStruct((ROWS, 2 * QK_ROPE), BF16)],
        compiler_params=_params(("parallel",),
                                _vmem_limit(len(h) * _nbytes((tm, D_MODEL), F32)
                                            + _nbytes((D_MODEL, Q_RANK + KV_RANK + 2 * QK_ROPE), BF16)
                                            + 3 * _nbytes((tm, KV_RANK), F32),
                                            temps=2 * _nbytes((tm, D_MODEL), F32))),
        name="mla_down",
    )(*h, mods, mods, w_dq, w_dkv, w_kr2, q_norm.reshape(1, Q_RANK), kv_norm.reshape(1, KV_RANK),
      rope_tab)


NT_DIMS = (((1,), (1,)), ((), ()))


def _q_up_kernel(cq_ref, w_ref, rope_ref, q_ref):
    cq, tab = cq_ref[...], rope_ref[...] * LOG2E_SCALE
    for h in range(MLA_HEADS):
        r = _dot(cq, w_ref[:, h * HEAD_W:(h + 1) * HEAD_W])
        q_ref[h, :, :QK_NOPE] = (r[:, :QK_NOPE] * LOG2E_SCALE).astype(BF16)
        q_ref[h, :, QK_NOPE:] = (r[:, QK_NOPE:] * tab).astype(BF16)


def _q_up(cq, w_q, rope_tab, tm=512):
    return pl.pallas_call(
        _q_up_kernel,
        grid=(ROWS // tm,),
        in_specs=[pl.BlockSpec((tm, Q_RANK), lambda i: (i, 0)),
                  pl.BlockSpec((Q_RANK, MLA_HEADS * HEAD_W), lambda i: (0, 0)),
                  pl.BlockSpec((tm, 2 * QK_ROPE), lambda i: (i, 0))],
        out_specs=pl.BlockSpec((MLA_HEADS, tm, HEAD_W), lambda i: (0, i, 0)),
        out_shape=jax.ShapeDtypeStruct((MLA_HEADS, ROWS, HEAD_W), BF16),
        compiler_params=_params(("parallel",),
                                _vmem_limit(_nbytes((Q_RANK + tm, MLA_HEADS * HEAD_W), BF16))),
        name="q_up",
    )(cq, w_q, rope_tab)


VT_PAD = 16
VT_ROWS = V_DIM + VT_PAD


def _kv_expand_kernel(ckv_ref, kr2_ref, wk_ref, wvt_ref, k_ref, vt_ref):
    c = ckv_ref[...].astype(BF16)
    vt = lax.dot_general(wvt_ref[...], c, NT_DIMS, preferred_element_type=F32).astype(BF16)
    ones = jnp.ones((VT_PAD, vt.shape[1]), BF16)
    for h in range(MLA_HEADS):
        vt_ref[h * VT_ROWS:h * VT_ROWS + V_DIM, :] = vt[h * V_DIM:(h + 1) * V_DIM]
        vt_ref[h * VT_ROWS + V_DIM:(h + 1) * VT_ROWS, :] = ones
    kr2 = kr2_ref[...]
    pair_w = 2 * QK_NOPE
    for g in range(MLA_HEADS // 2):
        r = _dot(c, wk_ref[:, g * pair_w:(g + 1) * pair_w]).astype(BF16)
        for e in range(2):
            k_ref[2 * g + e, :, :QK_NOPE] = r[:, e * QK_NOPE:(e + 1) * QK_NOPE]
            k_ref[2 * g + e, :, QK_NOPE:] = kr2


def _kv_expand(ckv, kr2, w_k, w_vt, tm=512):
    rows = ckv.shape[0]
    return pl.pallas_call(
        _kv_expand_kernel,
        grid=(rows // tm,),
        in_specs=[pl.BlockSpec((tm, KV_RANK), lambda i: (i, 0)),
                  pl.BlockSpec((tm, 2 * QK_ROPE), lambda i: (i, 0)),
                  pl.BlockSpec((KV_RANK, MLA_HEADS * QK_NOPE), lambda i: (0, 0)),
                  pl.BlockSpec((MLA_HEADS * V_DIM, KV_RANK), lambda i: (0, 0))],
        out_specs=[pl.BlockSpec((MLA_HEADS, tm, HEAD_W), lambda i: (0, i, 0)),
                   pl.BlockSpec((MLA_HEADS * VT_ROWS, tm), lambda i: (0, i))],
        out_shape=[jax.ShapeDtypeStruct((MLA_HEADS, rows, HEAD_W), BF16),
                   jax.ShapeDtypeStruct((MLA_HEADS * VT_ROWS, rows), BF16)],
        compiler_params=_params(("parallel",),
                                _vmem_limit(_nbytes((tm, MLA_HEADS * (HEAD_W + V_DIM)), BF16)
                                            + 2 * _nbytes((KV_RANK, MLA_HEADS * V_DIM), BF16),
                                            temps=_nbytes((MLA_HEADS * V_DIM, tm), F32))),
        name="kv_expand",
    )(ckv, kr2, w_k, w_vt)


ATT_CHUNK = 512
ATT_SKEW = 3
LOG2E_SCALE = ATT_SCALE * math.log2(math.e)


def _attn_scores(q, k):
    return lax.dot_general(k, q, NT_DIMS, preferred_element_type=F32)


def _attn_values(s, vt, carry):
    m = jnp.max(s, axis=0, keepdims=True)
    if carry is not None:
        m_old, acc_old = carry
        m = jnp.maximum(m_old, m)
    acc = _dot(vt, jnp.exp2(s - m).astype(BF16))
    if carry is not None:
        acc = jnp.exp2(m_old - m) * acc_old + acc
    return m, acc


def _attn_output(acc):
    return (acc[:V_DIM] / acc[V_DIM:V_DIM + 1]).T.astype(BF16)


def _attn_ctx_kernel(q_ref, k_ref, vt_ref, o_ref):
    for h in range(MLA_HEADS):
        _, acc = _attn_values(_attn_scores(q_ref[h], k_ref[h]),
                              vt_ref[h * VT_ROWS:(h + 1) * VT_ROWS, :], None)
        o_ref[:, h * V_DIM:(h + 1) * V_DIM] = _attn_output(acc)


def _attn_lat_kernel(q_ref, k_ref, vt_ref, kc_ref, vtc_ref, prev_ref, o_ref):
    del prev_ref
    q = q_ref[...]
    n_tok = DEC_SEQ // ATT_CHUNK
    rows = lambda c: slice(c * ATT_CHUNK, (c + 1) * ATT_CHUNK)
    keys = [k_ref.at[rows(c), :] for c in range(n_tok)] + [kc_ref]
    vals = [vt_ref.at[:, rows(c)] for c in range(n_tok)] + [vtc_ref]
    carry = None
    scores = [_attn_scores(q, keys[c][...]) for c in range(ATT_SKEW)]
    for c in range(n_tok + 1):
        if c + ATT_SKEW <= n_tok:
            scores.append(_attn_scores(q, keys[c + ATT_SKEW][...]))
        carry = _attn_values(scores[c], vals[c][...], carry)
    o_ref[...] = _attn_output(carry[1])


def _attention(q, k_tok, vt_tok, k_cache, vt_cache, tq=2048):
    assert DEC_SEQ % ATT_CHUNK == 0
    out_shape = jax.ShapeDtypeStruct((ROWS, MLA_HEADS * V_DIM), BF16)
    o = pl.pallas_call(
        _attn_ctx_kernel,
        grid=(BATCH,),
        in_specs=[pl.BlockSpec((MLA_HEADS, SEQ, HEAD_W), lambda s: (0, s, 0)),
                  pl.BlockSpec((MLA_HEADS, SEQ, HEAD_W), lambda s: (0, s, 0)),
                  pl.BlockSpec((MLA_HEADS * VT_ROWS, SEQ), lambda s: (0, s))],
        out_specs=pl.BlockSpec((SEQ, MLA_HEADS * V_DIM), lambda s: (s, 0)),
        out_shape=out_shape,
        compiler_params=_params(("parallel",), VMEM_FLOOR_BYTES),
        name="attn_ctx",
    )(q, k_tok, vt_tok)

    lat0 = ROWS_CTX // DEC_SEQ
    q0 = ROWS_CTX // tq
    nq = DEC_SEQ // tq
    return pl.pallas_call(
        _attn_lat_kernel,
        grid=(DEC_BATCH, MLA_HEADS, nq),
        in_specs=[pl.BlockSpec((None, tq, HEAD_W), lambda b, h, i: (h, q0 + b * nq + i, 0)),
                  pl.BlockSpec((None, DEC_SEQ, HEAD_W), lambda b, h, i: (h, lat0 + b, 0)),
                  pl.BlockSpec((VT_ROWS, DEC_SEQ), lambda b, h, i: (h, lat0 + b)),
                  pl.BlockSpec((None, PAST_LEN, HEAD_W), lambda b, h, i: (h, b, 0)),
                  pl.BlockSpec((VT_ROWS, PAST_LEN), lambda b, h, i: (h, b)),
                  pl.BlockSpec(memory_space=pl.ANY)],
        out_specs=pl.BlockSpec((tq, V_DIM), lambda b, h, i: (q0 + b * nq + i, h)),
        out_shape=out_shape,
        input_output_aliases={5: 0},
        compiler_params=_params(("parallel", "parallel", "arbitrary"),
                                _vmem_limit(_nbytes((DEC_SEQ + PAST_LEN, HEAD_W + V_DIM), BF16),
                                            temps=8 * _nbytes((ATT_CHUNK, tq), F32))),
        name="attn_lat",
    )(q, k_tok, vt_tok, k_cache, vt_cache, o)


def _mm_kernel(a_ref, b_ref, o_ref, acc_ref):
    k = pl.program_id(2)

    @pl.when(k == 0)
    def _():
        acc_ref[...] = jnp.zeros_like(acc_ref)

    acc_ref[...] += _dot(a_ref[...], b_ref[...].astype(BF16))

    @pl.when(k == pl.num_programs(2) - 1)
    def _():
        o_ref[...] = acc_ref[...].astype(o_ref.dtype)


def _mm(a, b, out_dtype=F32, tm=1024, tn=1024, tk=512):
    m, kk = a.shape
    n = b.shape[1]
    tm, tn, tk = min(tm, m), min(tn, n), min(tk, kk)
    return pl.pallas_call(
        _mm_kernel,
        grid=(m // tm, n // tn, kk // tk),
        in_specs=[pl.BlockSpec((tm, tk), lambda i, j, k: (i, k)),
                  pl.BlockSpec((tk, tn), lambda i, j, k: (k, j))],
        out_specs=pl.BlockSpec((tm, tn), lambda i, j, k: (i, j)),
        out_shape=jax.ShapeDtypeStruct((m, n), out_dtype),
        scratch_shapes=[pltpu.VMEM((tm, tn), F32)],
        compiler_params=_params(("parallel", "parallel", "arbitrary"), VMEM_FLOOR_BYTES),
        name="mm",
    )(a, b)


def _seq_tiles(length):
    if length >= 1024:
        return 1024, 512, 1024
    return length, D_MODEL, length


def _dft_fwd_kernel(c_ref, s_ref, z_ref, kr_ref, ks_ref, yr_ref, ys_ref, accr_ref, accs_ref):
    k = pl.program_id(3)

    @pl.when(k == 0)
    def _():
        accr_ref[...] = jnp.zeros_like(accr_ref)
        accs_ref[...] = jnp.zeros_like(accs_ref)

    z = z_ref[...].astype(BF16)
    accr_ref[...] += _dot(c_ref[...], z)
    accs_ref[...] += _dot(s_ref[...], z)

    @pl.when(k == pl.num_programs(3) - 1)
    def _():
        zr, zs, kr, ks = accr_ref[...], accs_ref[...], kr_ref[...], ks_ref[...]
        yr_ref[...] = (zr * kr - zs * ks).astype(BF16)
        ys_ref[...] = (zr * ks + zs * kr).astype(BF16)


def _dft_fwd(tabs, z, z_which, kr, ks, order, n_seq, length):
    tf, tn, tk = _seq_tiles(length)
    nb, nf, nk = D_MODEL // tn, length // tf, length // tk
    kc0 = order * nb
    out = jax.ShapeDtypeStruct((n_seq * length, D_MODEL), BF16)
    return pl.pallas_call(
        _dft_fwd_kernel,
        grid=(n_seq, nf, nb, nk),
        in_specs=[pl.BlockSpec((tf, tk), lambda s, f, n, k: (f, k)),
                  pl.BlockSpec((tf, tk), lambda s, f, n, k: (f, k)),
                  pl.BlockSpec((None, tk, tn), lambda s, f, n, k: (z_which, s * nk + k, n)),
                  pl.BlockSpec((tf, tn), lambda s, f, n, k: (f, kc0 + n)),
                  pl.BlockSpec((tf, tn), lambda s, f, n, k: (f, kc0 + n))],
        out_specs=[pl.BlockSpec((tf, tn), lambda s, f, n, k: (s * nf + f, n)),
                   pl.BlockSpec((tf, tn), lambda s, f, n, k: (s * nf + f, n))],
        out_shape=[out, out],
        scratch_shapes=[pltpu.VMEM((tf, tn), F32), pltpu.VMEM((tf, tn), F32)],
        compiler_params=_params(("parallel", "parallel", "parallel", "arbitrary"),
                                _vmem_limit(2 * _nbytes((tf, tk), BF16) + _nbytes((tk, tn), F32)
                                            + 2 * _nbytes((tf, tn), F32) + 2 * _nbytes((tf, tn), BF16),
                                            resident=2 * _nbytes((tf, tn), F32),
                                            temps=4 * _nbytes((tf, tn), F32))),
        name="dft_fwd",
    )(tabs["c"], tabs["s"], z, kr, ks)


def _dft_inv_kernel(inv_len, ct_ref, st_ref, yr_ref, ys_ref, z_ref, gate_ref, skip_ref,
                    o_ref, acc_ref):
    k = pl.program_id(3)

    @pl.when(k == 0)
    def _():
        acc_ref[...] = jnp.zeros_like(acc_ref)

    acc_ref[...] += _dot(ct_ref[...], yr_ref[...]) + _dot(st_ref[...], ys_ref[...])

    @pl.when(k == pl.num_programs(3) - 1)
    def _():
        y = acc_ref[...] * inv_len + skip_ref[...] * z_ref[...]
        o_ref[...] = (gate_ref[...] * y).astype(o_ref.dtype)


def _dft_inv(tabs, yr, ys, z, z_which, gate, gate_which, skip, order, n_seq, length, out_dtype):
    tt, tn, tk = _seq_tiles(length)
    nb, nt, nk = D_MODEL // tn, length // tt, length // tk
    return pl.pallas_call(
        functools.partial(_dft_inv_kernel, 1.0 / length),
        grid=(n_seq, nt, nb, nk),
        in_specs=[pl.BlockSpec((tt, tk), lambda s, t, n, k: (t, k)),
                  pl.BlockSpec((tt, tk), lambda s, t, n, k: (t, k)),
                  pl.BlockSpec((tk, tn), lambda s, t, n, k: (s * nk + k, n)),
                  pl.BlockSpec((tk, tn), lambda s, t, n, k: (s * nk + k, n)),
                  pl.BlockSpec((None, tt, tn), lambda s, t, n, k: (z_which, s * nt + t, n)),
                  pl.BlockSpec((None, tt, tn), lambda s, t, n, k: (gate_which, s * nt + t, n)),
                  pl.BlockSpec((None, 1, tn), lambda s, t, n, k: (order, 0, n))],
        out_specs=pl.BlockSpec((tt, tn), lambda s, t, n, k: (s * nt + t, n)),
        out_shape=jax.ShapeDtypeStruct((n_seq * length, D_MODEL), out_dtype),
        scratch_shapes=[pltpu.VMEM((tt, tn), F32)],
        compiler_params=_params(("parallel", "parallel", "parallel", "arbitrary"),
                                _vmem_limit(2 * _nbytes((tt, tk), BF16) + 2 * _nbytes((tk, tn), BF16)
                                            + 3 * _nbytes((tt, tn), F32),
                                            resident=_nbytes((tt, tn), F32),
                                            temps=3 * _nbytes((tt, tn), F32))),
        name="dft_inv",
    )(tabs["ct"], tabs["st"], yr, ys, z, gate, skip)


TW_LANES = 128
SUB = 8
CT_ROWS = 2 * SUB
CT_COLS = 1024


def _lane_tile(x, width):
    return jnp.tile(x, (1, width // x.shape[-1]))


def _kron_sub(f):
    return jnp.kron(f, jnp.eye(SUB, dtype=f.dtype))


def _sub_rows(x, h):
    part = x[:, h * SUB:(h + 1) * SUB, :]
    return part.reshape(part.shape[0] * SUB, part.shape[2])


def _from_sub_rows(parts):
    split = [p.reshape(p.shape[0] // SUB, SUB, p.shape[1]) for p in parts]
    return jnp.concatenate(split, axis=1)


def _ct_stage_a_kernel(n_in, *refs):
    x_refs, (fa_ref, cw_ref, sw_ref, br_ref, bi_ref) = refs[:n_in], refs[n_in:]
    fa = fa_ref[...]
    half = fa.shape[0] // 2
    width = br_ref.shape[-1]
    xs = [r[...].astype(F32) for r in x_refs]
    b_re, b_im = [], []
    for h in range(CT_ROWS // SUB):
        x = jnp.concatenate([_sub_rows(x, h) for x in xs], axis=0).astype(BF16)
        a = _dot(fa, x)
        ar, ai = a[:half], a[half:]
        cw, sw = _lane_tile(cw_ref[h], width), _lane_tile(sw_ref[h], width)
        b_re.append(ar * cw + ai * sw)
        b_im.append(ai * cw - ar * sw)
    br_ref[...] = _from_sub_rows(b_re).astype(BF16)
    bi_ref[...] = _from_sub_rows(b_im).astype(BF16)


def _ct_stage_a(xs, n_seq, fa, cw, sw):
    n1 = fa.shape[0] // (2 * SUB)
    n2 = cw.shape[0] * CT_ROWS
    width = xs[0][0].shape[-1]
    tn2, tw = CT_ROWS, CT_COLS
    in_specs, blocks = [], 0
    for arr, prefix in xs:
        rows_in = arr.shape[-3]
        lead = (None,) * (arr.ndim - 3)
        in_specs.append(pl.BlockSpec(lead + (rows_in, tn2, tw),
                                     lambda s, i, c, prefix=prefix: prefix(s) + (0, i, c)))
        blocks += _nbytes((rows_in, tn2, tw), arr.dtype)
    twid = pl.BlockSpec((None,) + cw.shape[1:], lambda s, i, c: (i, 0, 0, 0))
    in_specs += [pl.BlockSpec(fa.shape, lambda s, i, c: (0, 0)), twid, twid]
    out = jax.ShapeDtypeStruct((n_seq, n1, n2, width), BF16)
    out_spec = pl.BlockSpec((None, n1, tn2, tw), lambda s, i, c: (s, 0, i, c))
    return pl.pallas_call(
        functools.partial(_ct_stage_a_kernel, len(xs)),
        grid=(n_seq, n2 // tn2, width // tw),
        in_specs=in_specs,
        out_specs=[out_spec, out_spec],
        out_shape=[out, out],
        compiler_params=_params(("parallel", "parallel", "parallel"),
                                _vmem_limit(blocks + 2 * _nbytes((n1, tn2, tw), BF16)
                                            + _nbytes(fa.shape, BF16),
                                            temps=8 * _nbytes((n1, tn2, tw), F32))),
        name="ct_stage_a",
    )(*[arr for arr, _ in xs], fa, cw, sw)


def _ct_mid_kernel(tk1, br_ref, bi_ref, ur_ref, ui_ref, *refs):
    partners, refs = refs[:2 * tk1], refs[2 * tk1:]
    fb_ref, fbi_ref, fbp0_ref, fbp1_ref, cw_ref, sw_ref, vr_ref, vi_ref = refs
    fb, fbi = fb_ref[...], fbi_ref[...]
    half = fb.shape[0] // 2
    width = br_ref.shape[-1]
    first_tile = pl.program_id(0) == 0
    for j in range(tk1):
        stack = lambda re_ref, im_ref, p=j: jnp.concatenate([re_ref[p], im_ref[p]], axis=0)
        fbp = fbp1_ref[...] if j else jnp.where(first_tile, fbp0_ref[...], fbp1_ref[...])
        u_own = _dot(fb[:half], stack(ur_ref, ui_ref))
        u_neg = _dot(fbp, stack(partners[2 * j], partners[2 * j + 1], 0))
        kr = 0.5 * (u_own + u_neg)
        ki = 0.5 * (u_neg - u_own)
        x = _dot(fb, stack(br_ref, bi_ref))
        xr, xi = x[:half], x[half:]
        y = jnp.concatenate([xr * kr - xi * ki, xr * ki + xi * kr], axis=0).astype(BF16)
        v = _dot(fbi, y)
        vr, vi = v[:half], v[half:]
        cw, sw = _lane_tile(cw_ref[j], width), _lane_tile(sw_ref[j], width)
        vr_ref[j] = (vr * cw - vi * sw).astype(BF16)
        vi_ref[j] = (vi * cw + vr * sw).astype(BF16)


def _ct_mid(br, bi, ur, ui, order, fb, fbi, fbp0, fbp1, cw, sw, tk1=4, td=1024):
    n1, n2, d = br.shape
    nd = d // td
    data = pl.BlockSpec((tk1, n2, td), lambda i, j: (i, 0, j))
    coef = pl.BlockSpec((tk1, n2, td), lambda i, j: (i, 0, order * nd + j))
    page = lambda p: pl.BlockSpec((1, n2, td),
                                  lambda i, j: ((n1 - (i * tk1 + p)) % n1, 0, order * nd + j))
    partners = [page(p) for p in range(tk1) for _ in range(2)]
    mat = pl.BlockSpec(fb.shape, lambda i, j: (0, 0))
    mat_p = pl.BlockSpec(fbp0.shape, lambda i, j: (0, 0))
    tw = pl.BlockSpec((tk1, n2, TW_LANES), lambda i, j: (i, 0, 0))
    out = jax.ShapeDtypeStruct((n1, n2, d), BF16)
    return pl.pallas_call(
        functools.partial(_ct_mid_kernel, tk1),
        grid=(n1 // tk1, nd),
        in_specs=[data, data, coef, coef] + partners + [mat, mat, mat_p, mat_p, tw, tw],
        out_specs=[data, data],
        out_shape=[out, out],
        compiler_params=_params(("parallel", "parallel"),
                                _vmem_limit(8 * _nbytes((tk1, n2, td), BF16),
                                            temps=10 * _nbytes((2 * n2, td), F32))),
        name="ct_mid",
    )(br, bi, ur, ui, *([ur, ui] * tk1), fb, fbi, fbp0, fbp1, cw, sw)


def _ct_inv_a_kernel(scale, vr_ref, vi_ref, fai_ref, z0_ref, z1_ref, g0_ref, g1_ref, skip_ref,
                     o_ref):
    fai = fai_ref[...]
    half = fai.shape[0] // 2
    skip = skip_ref[...]
    vr, vi = vr_ref[...].astype(F32), vi_ref[...].astype(F32)
    zs, gs = (z0_ref[...], z1_ref[...]), (g0_ref[...], g1_ref[...])
    outs = ([], [])
    for h in range(CT_ROWS // SUB):
        v = jnp.concatenate([_sub_rows(vr, h), _sub_rows(vi, h)], axis=0).astype(BF16)
        y = _dot(fai, v) * scale
        for b, yb in enumerate((y[:half], y[half:])):
            outs[b].append(_sub_rows(gs[b], h) * (yb + skip * _sub_rows(zs[b], h)))
    for b in range(2):
        o_ref[b] = _from_sub_rows(outs[b]).astype(o_ref.dtype)


def _ct_inv_a(vr, vi, fai, z, z_which, gate, gate_which, skip, order, out_dtype):
    n1, n2, d = vr.shape
    rows = fai.shape[0] // (2 * SUB)
    tn2, tw = CT_ROWS, CT_COLS // 2
    spec = pl.BlockSpec((n1, tn2, tw), lambda i, c: (0, i, c))
    pair = lambda which, b: pl.BlockSpec((None, None, rows, tn2, tw),
                                         lambda i, c: (which[0], which[1] + b, 0, i, c))
    return pl.pallas_call(
        functools.partial(_ct_inv_a_kernel, 1.0 / (n1 * n2)),
        grid=(n2 // tn2, d // tw),
        in_specs=[spec, spec, pl.BlockSpec(fai.shape, lambda i, c: (0, 0)),
                  pair(z_which, 0), pair(z_which, 1), pair(gate_which, 0), pair(gate_which, 1),
                  pl.BlockSpec((None, 1, tw), lambda i, c: (order, 0, c))],
        out_specs=pl.BlockSpec((2, rows, tn2, tw), lambda i, c: (0, 0, i, c)),
        out_shape=jax.ShapeDtypeStruct((2, rows, n2, d), out_dtype),
        compiler_params=_params(("parallel", "parallel"),
                                _vmem_limit(2 * _nbytes((n1, tn2, tw), BF16)
                                            + 6 * _nbytes((rows, tn2, tw), F32)
                                            + _nbytes(fai.shape, BF16),
                                            temps=8 * _nbytes((n1, tn2, tw), F32))),
        name="ct_inv_a",
    )(vr, vi, fai, z, z, gate, gate, skip)


def _ct_real_b_kernel(scale, br_ref, bi_ref, fb_ref, o_ref, so_ref):
    fb = fb_ref[...]
    for j in range(br_ref.shape[0]):
        so_ref[:, j, :] = _dot(fb, jnp.concatenate([br_ref[j], bi_ref[j]], axis=0)) * scale
    o_ref[...] = so_ref[...].astype(o_ref.dtype)


def _ct_real_b(br, bi, fb_re, scale):
    n_seq, n1, n2, d = br.shape
    tk1, tw = CT_ROWS, CT_COLS
    blk = pl.BlockSpec((None, tk1, n2, tw), lambda s, i, c: (s, i, 0, c))
    return pl.pallas_call(
        functools.partial(_ct_real_b_kernel, scale),
        grid=(n_seq, n1 // tk1, d // tw),
        in_specs=[blk, blk, pl.BlockSpec(fb_re.shape, lambda s, i, c: (0, 0))],
        out_specs=pl.BlockSpec((None, n2, tk1, tw), lambda s, i, c: (s, 0, i, c)),
        out_shape=jax.ShapeDtypeStruct((n_seq, n2, n1, d), BF16),
        scratch_shapes=[pltpu.VMEM((n2, tk1, tw), F32)],
        compiler_params=_params(("parallel", "parallel", "parallel"),
                                _vmem_limit(3 * _nbytes((tk1, n2, tw), BF16),
                                            resident=_nbytes((n2, tk1, tw), F32),
                                            temps=2 * _nbytes((n2, tk1, tw), F32))),
        name="ct_real_b",
    )(br, bi, fb_re)


def _cos_sin(num, den):
    ang = (num % den).astype(F32) * (2.0 * math.pi / den)
    return jnp.cos(ang), jnp.sin(ang)


def _ct_tables(n1, n2):
    i1 = jnp.arange(n1, dtype=jnp.int32)
    i2 = jnp.arange(n2, dtype=jnp.int32)
    c1, s1 = _cos_sin(i1[:, None] * i1[None, :], n1)
    c2, s2 = _cos_sin(i2[:, None] * i2[None, :], n2)
    cw, sw = _cos_sin(i2[:, None] * i1[None, :], n1 * n2)
    lanes = lambda t: jnp.broadcast_to(t[..., None], t.shape + (TW_LANES,))

    def stage_a_rows(t):
        t = t.reshape(n2 // CT_ROWS, CT_ROWS // SUB, SUB, n1)
        return lanes(jnp.swapaxes(t, 2, 3).reshape(n2 // CT_ROWS, CT_ROWS // SUB, n1 * SUB))

    return {"c1": c1, "s1": s1, "c2": c2, "s2": s2,
            "cw_a": stage_a_rows(cw), "sw_a": stage_a_rows(sw),
            "cw_b": lanes(cw.T), "sw_b": lanes(sw.T)}


def _filter_rows(feat_ref, t_ref, w1_ref, b1_ref, fr1_ref, w2_ref, b2_ref, fr2_ref, w3_ref,
                 decay_ref):
    x = jnp.sin(fr1_ref[...] * (_dot(feat_ref[...].astype(BF16), w1_ref[...].astype(BF16))
                                + b1_ref[...]))
    x = jnp.sin(fr2_ref[...] * (_dot(x.astype(BF16), w2_ref[...].astype(BF16)) + b2_ref[...]))
    h = _dot(x.astype(BF16), w3_ref[...].astype(BF16))
    return h * (jnp.exp(-t_ref[...] * jnp.exp(decay_ref[...])) + HY_SHIFT)


def _filter_stats_kernel(*refs):
    ss_ref = refs[-1]
    h = _filter_rows(*refs[:-1])

    @pl.when(pl.program_id(0) == 0)
    def _():
        ss_ref[...] = jnp.zeros_like(ss_ref)

    ss_ref[...] += jnp.sum(h * h, axis=0, keepdims=True)


def _filter_emit_kernel(*refs):
    ss_ref, a_ref, b_ref = refs[-3:]
    h = _filter_rows(*refs[:-3])
    n_dir = a_ref.shape[1]
    ss = ss_ref[...]
    norm = lax.rsqrt(ss[:, :n_dir] + ss[:, n_dir:] + 1e-12)
    fwd = h[:, :n_dir] * norm
    bwd = h[:, n_dir:] * norm
    row = lax.broadcasted_iota(jnp.int32, bwd.shape, 0) + pl.program_id(0) * bwd.shape[0]
    bwd = jnp.where(row == 0, 0.0, bwd)
    a_ref[...] = (fwd + bwd).astype(BF16)
    b_ref[...] = (fwd - bwd).astype(BF16)


def _hyena_filters(length, f_w1, f_b1, f_freq1, f_w2, f_b2, f_freq2, f_w3, log_decay):
    t = jnp.linspace(0.0, 1.0, length, dtype=F32)[:, None]
    t_idx = jnp.arange(length, dtype=F32)[:, None]
    bands = jnp.linspace(1e-4, HY_BANDS - 1, HY_BANDS, dtype=F32)
    w = 2.0 * math.pi * t_idx * bands / length
    feat = jnp.concatenate([t, jnp.cos(w), -jnp.sin(w)], axis=-1)
    emb_pad = 128
    feat = jnp.pad(feat, ((0, 0), (0, emb_pad - HY_EMB)))
    w1 = jnp.pad(f_w1, ((0, emb_pad - HY_EMB), (0, 0)))
    n_all = HY_DIRS * HY_ORDER * D_MODEL
    n_dir = HY_ORDER * D_MODEL
    tm = 256
    full = lambda shape: pl.BlockSpec(shape, lambda i: (0, 0))
    mlp_specs = [pl.BlockSpec((tm, emb_pad), lambda i: (i, 0)),
                 pl.BlockSpec((tm, 1), lambda i: (i, 0)),
                 full((emb_pad, HY_FW)), full((1, HY_FW)), full((1, HY_FW)),
                 full((HY_FW, HY_FW)), full((1, HY_FW)), full((1, HY_FW)),
                 full((HY_FW, n_all)), full((1, n_all))]
    mlp_args = (feat, t, w1, f_b1.reshape(1, HY_FW), f_freq1.reshape(1, HY_FW), f_w2,
                f_b2.reshape(1, HY_FW), f_freq2.reshape(1, HY_FW), f_w3, log_decay.reshape(1, n_all))
    vmem = _vmem_limit(_nbytes((tm, n_all), F32) + _nbytes((HY_FW, n_all), F32),
                       temps=4 * _nbytes((tm, n_all), F32))
    ss = pl.pallas_call(
        _filter_stats_kernel,
        grid=(length // tm,),
        in_specs=mlp_specs,
        out_specs=full((1, n_all)),
        out_shape=jax.ShapeDtypeStruct((1, n_all), F32),
        compiler_params=_params(("arbitrary",), vmem),
        name="filter_stats",
    )(*mlp_args)
    comb = jax.ShapeDtypeStruct((length, n_dir), BF16)
    return pl.pallas_call(
        _filter_emit_kernel,
        grid=(length // tm,),
        in_specs=mlp_specs + [full((1, n_all))],
        out_specs=[pl.BlockSpec((tm, n_dir), lambda i: (i, 0)), pl.BlockSpec((tm, n_dir), lambda i: (i, 0))],
        out_shape=[comb, comb],
        compiler_params=_params(("parallel",), vmem),
        name="filter_emit",
    )(*mlp_args, ss)


def _cis_product(row_hi, row_lo, period):
    def cis(phase):
        ang = (phase % period).astype(F32) * (2.0 * math.pi / period)
        return jnp.cos(ang)[:, :, None], jnp.sin(ang)[:, :, None]
    (c1, s1), (c0, s0) = cis(row_hi), cis(row_lo)
    c0, s0 = jnp.swapaxes(c0, 1, 2), jnp.swapaxes(s0, 1, 2)
    rows = row_hi.shape[0]
    return ((c1 * c0 - s1 * s0).reshape(rows, -1), (s1 * c0 + c1 * s0).reshape(rows, -1))


def _odd_dft_tables(length):
    split = 1 << (length.bit_length() // 2)
    r = jnp.arange(length, dtype=jnp.int32)[:, None]
    hi = jnp.arange(length // split, dtype=jnp.int32)[None, :] * split
    lo = jnp.arange(split, dtype=jnp.int32)[None, :]
    c, s = _cis_product((2 * r + 1) * hi, (2 * r + 1) * lo, 4 * length)
    ct, st = _cis_product(r * (2 * hi), r * (2 * lo + 1), 4 * length)
    return {"c": c.astype(BF16), "s": s.astype(BF16), "ct": ct.astype(BF16), "st": st.astype(BF16)}


def _dft_tables(length):
    split = 1 << (length.bit_length() // 2)
    r = jnp.arange(length, dtype=jnp.int32)[:, None]
    hi = jnp.arange(length // split, dtype=jnp.int32)[None, :] * split
    lo = jnp.arange(split, dtype=jnp.int32)[None, :]
    c, s = _cis_product(r * hi, r * lo, length)
    return c.astype(BF16), (-s).astype(BF16)


def _hyena_mix_dense(pc, filt_p, filt_m, skip, n_seq, length):
    tabs = _odd_dft_tables(length)
    kr, ks = _mm(tabs["c"], filt_p), _mm(tabs["s"], filt_m)
    yr, ys = _dft_fwd(tabs, pc, 0, kr, ks, 0, n_seq, length)
    z1 = _dft_inv(tabs, yr, ys, pc, 0, pc, 1, skip, 0, n_seq, length, F32)[None]
    yr, ys = _dft_fwd(tabs, z1, 0, kr, ks, 1, n_seq, length)
    return _dft_inv(tabs, yr, ys, z1, 0, pc, 2, skip, 1, n_seq, length, BF16)


HY_N1, HY_N2 = 64, 128


def _hyena_mix_pair(pc, filt_p, filt_m, skip):
    assert DEC_BATCH == 2 and HY_N1 * HY_N2 == 2 * DEC_SEQ and ROWS % DEC_SEQ == 0
    n1, n2, rows_in = HY_N1, HY_N2, HY_N1 // 2
    t = _ct_tables(n1, n2)
    c_in, s_in = t["c1"][:, :rows_in], t["s1"][:, :rows_in]
    fa = _kron_sub(jnp.block([[c_in, s_in], [-s_in, c_in]])).astype(BF16)
    fb = jnp.block([[t["c2"], t["s2"]], [-t["s2"], t["c2"]]]).astype(BF16)
    fbi = jnp.block([[t["c2"], -t["s2"]], [t["s2"], t["c2"]]]).astype(BF16)
    mirror = lambda shift: (-jnp.arange(n2) - shift) % n2
    fbp0, fbp1 = [jnp.concatenate([t["c2"][mirror(sh)], t["s2"][mirror(sh)]], axis=1).astype(BF16)
                  for sh in (0, 1)]
    c_out, s_out = t["c1"][:rows_in], t["s1"][:rows_in]
    fai = _kron_sub(jnp.block([[c_out, -s_out], [s_out, c_out]])).astype(BF16)
    n_filt = HY_ORDER * D_MODEL
    filt_view = lambda f: (f.reshape(rows_in, n2, n_filt), lambda s: ())
    ur, ui = _ct_stage_a([filt_view(filt_p), filt_view(filt_m)], 1, fa, t["cw_a"], t["sw_a"])

    pc = pc.reshape(3, ROWS // DEC_SEQ, rows_in, n2, D_MODEL)
    z, z0, out_dtypes = pc, ROWS_CTX // DEC_SEQ, (F32, BF16)
    for order in range(HY_ORDER):
        br, bi = _ct_stage_a([(z, lambda s, z0=z0: (0, z0)), (z, lambda s, z0=z0: (0, z0 + 1))], 1, fa,
                             t["cw_a"], t["sw_a"])
        vr, vi = _ct_mid(br[0], bi[0], ur[0], ui[0], order, fb, fbi, fbp0, fbp1, t["cw_b"], t["sw_b"])
        z = _ct_inv_a(vr, vi, fai, z, (0, z0), pc, (1 + order, ROWS_CTX // DEC_SEQ), skip, order,
                      out_dtypes[order])[None]
        z0 = 0
    return z.reshape(ROWS_LAT, D_MODEL)


def _fnet_chan_kernel(h_ref, sh_ref, sc_ref, w_ref, p_ref, q_ref):
    u = _modulate(h_ref[...], sh_ref[...], sc_ref[...]).astype(BF16)
    w = w_ref[...]
    for g in range(FNET_GROUPS):
        cols = slice(g * FNET_CG, (g + 1) * FNET_CG)
        r = _dot(u[:, cols], w)
        p_ref[:, cols] = r[:, :FNET_CG].astype(BF16)
        q_ref[:, cols] = r[:, FNET_CG:].astype(BF16)


def _fnet_chan(h, mods, w_cs, tm=512):
    row = pl.BlockSpec((tm, D_MODEL), lambda i: (i, 0))
    out = jax.ShapeDtypeStruct((ROWS, D_MODEL), BF16)
    return pl.pallas_call(
        _fnet_chan_kernel,
        grid=(ROWS // tm,),
        in_specs=[row, _mod_spec(0, tm), _mod_spec(1, tm),
                  pl.BlockSpec((FNET_CG, 2 * FNET_CG), lambda i: (0, 0))],
        out_specs=[row, row],
        out_shape=[out, out],
        compiler_params=_params(("parallel",),
                                _vmem_limit(3 * _nbytes((tm, D_MODEL), F32),
                                            temps=2 * _nbytes((tm, D_MODEL), F32))),
        name="fnet_chan",
    )(h, mods, mods, w_cs)


def _fnet_pos_kernel(scale, c_ref, ns_ref, p_ref, q_ref, o_ref, acc_ref):
    k = pl.program_id(3)

    @pl.when(k == 0)
    def _():
        acc_ref[...] = jnp.zeros_like(acc_ref)

    acc_ref[...] += (_dot(c_ref[...], p_ref[...].astype(BF16))
                     + _dot(ns_ref[...], q_ref[...].astype(BF16)))

    @pl.when(k == pl.num_programs(3) - 1)
    def _():
        o_ref[...] = (acc_ref[...] * scale).astype(o_ref.dtype)


def _fnet_pos(c_tab, ns_tab, p, q, n_seq, length):
    tt, tn, tk = _seq_tiles(length)
    nb, nt, nk = D_MODEL // tn, length // tt, length // tk
    scale = (length * FNET_CG) ** -0.5
    return pl.pallas_call(
        functools.partial(_fnet_pos_kernel, scale),
        grid=(n_seq, nt, nb, nk),
        in_specs=[pl.BlockSpec((tt, tk), lambda s, t, n, k: (t, k)),
                  pl.BlockSpec((tt, tk), lambda s, t, n, k: (t, k)),
                  pl.BlockSpec((tk, tn), lambda s, t, n, k: (s * nk + k, n)),
                  pl.BlockSpec((tk, tn), lambda s, t, n, k: (s * nk + k, n))],
        out_specs=pl.BlockSpec((tt, tn), lambda s, t, n, k: (s * nt + t, n)),
        out_shape=jax.ShapeDtypeStruct((n_seq * length, D_MODEL), BF16),
        scratch_shapes=[pltpu.VMEM((tt, tn), F32)],
        compiler_params=_params(("parallel", "parallel", "parallel", "arbitrary"),
                                _vmem_limit(2 * _nbytes((tt, tk), BF16) + 2 * _nbytes((tk, tn), BF16)
                                            + _nbytes((tt, tn), BF16),
                                            resident=_nbytes((tt, tn), F32),
                                            temps=2 * _nbytes((tt, tn), F32))),
        name="fnet_pos",
    )(c_tab, ns_tab, p, q)


FN_N1, FN_N2 = 32, 128


def _fnet_pos_factored(p, q):
    assert FN_N1 * FN_N2 == DEC_SEQ and ROWS % DEC_SEQ == 0
    n1, n2 = FN_N1, FN_N2
    t = _ct_tables(n1, n2)
    fa = _kron_sub(jnp.block([[t["c1"], -t["s1"]], [-t["s1"], -t["c1"]]])).astype(BF16)
    fb_re = jnp.concatenate([t["c2"], t["s2"]], axis=1).astype(BF16)
    lat0 = ROWS_CTX // DEC_SEQ
    view = lambda x: x.reshape(ROWS // DEC_SEQ, n1, n2, D_MODEL)
    seq = lambda s: (lat0 + s,)
    br, bi = _ct_stage_a([(view(p), seq), (view(q), seq)], DEC_BATCH, fa, t["cw_a"], t["sw_a"])
    f = _ct_real_b(br, bi, fb_re, (DEC_SEQ * FNET_CG) ** -0.5)
    return f.reshape(ROWS_LAT, D_MODEL)


def _rope_table():
    rows = DEC_SEQ // GRID_W
    row = jnp.repeat(jnp.arange(rows), GRID_W).astype(F32)
    col = jnp.tile(jnp.arange(GRID_W), rows).astype(F32)
    inv = ROPE_THETA ** (-jnp.arange(0, AXIS_ROPE, 2, dtype=F32) / AXIS_ROPE)
    ang = jnp.concatenate([row[:, None] * inv, col[:, None] * inv], axis=-1)
    cos = jnp.repeat(jnp.cos(ang), 2, axis=-1)
    sin = jnp.repeat(jnp.sin(ang), 2, axis=-1)
    lat = jnp.tile(jnp.concatenate([cos, sin], axis=-1), (DEC_BATCH, 1))
    ctx = jnp.concatenate([jnp.ones((ROWS_CTX, QK_ROPE), F32), jnp.zeros((ROWS_CTX, QK_ROPE), F32)],
                          axis=-1)
    return jnp.concatenate([ctx, lat], axis=0)


def _pair_rotated(w):
    pairs = w.reshape(w.shape[:-1] + (QK_ROPE // 2, 2))
    return jnp.stack([-pairs[..., 1], pairs[..., 0]], axis=-1).reshape(w.shape)


def kernel(x_prompt, x_sample, c, cache_ckv, cache_krope, c_ctx, ada_w, ada_b, ln_g, ln_b, ffn_w_gate, ffn_w_up, ffn_w_down, mla_w_dq, mla_q_norm, mla_w_uq, mla_w_dkv, mla_kv_norm, mla_w_kr, mla_w_ukv, mla_w_o, hy_w_in, hy_b_in, hy_conv_w, hy_conv_b, hy_f_w1, hy_f_b1, hy_f_freq1, hy_f_w2, hy_f_b2, hy_f_freq2, hy_f_w3, hy_log_decay, hy_skip, hy_w_out, hy_b_out, fn_w_out, fn_b_out):
    assert x_prompt.shape == (BATCH, SEQ, D_MODEL) and x_sample.shape == (DEC_BATCH, DEC_SEQ, D_MODEL)
    assert ROWS_CTX % DEC_SEQ == 0 and SEQ == FNET_CG

    assert N_MIXERS > 0 and DEPTH > 1
    h = (x_prompt.reshape(ROWS_CTX, D_MODEL), x_sample.reshape(ROWS_LAT, D_MODEL))
    cond = jnp.concatenate([c_ctx[None, :], c, jnp.zeros((COND_PAD - N_COND, D_MODEL), F32)])
    mods_all = _modulation_vectors(cond, ada_w, ada_b)
    zero_bias = jnp.zeros((D_MODEL,), F32)
    ffn_w = (ffn_w_gate.astype(BF16), ffn_w_up.astype(BF16), ffn_w_down.astype(BF16))
    rope_tab = None
    ckv_states, krope_states = [], []

    for i in range(DEPTH):
        kind, j = i % N_MIXERS, i // N_MIXERS
        mods = mods_all[i]
        if kind == 0:
            if rope_tab is None:
                rope_tab = _rope_table()
            w_kr2 = jnp.concatenate([mla_w_kr[j], _pair_rotated(mla_w_kr[j])], axis=-1).astype(BF16)
            wq = mla_w_uq[j].reshape(Q_RANK, MLA_HEADS, QK_NOPE + QK_ROPE)
            w_q = jnp.concatenate([wq, _pair_rotated(wq[..., QK_NOPE:])], axis=-1)
            w_q = w_q.reshape(Q_RANK, MLA_HEADS * HEAD_W).astype(BF16)
            w_ukv = mla_w_ukv[j].reshape(KV_RANK, MLA_HEADS, QK_NOPE + V_DIM)
            w_k = w_ukv[..., :QK_NOPE].reshape(KV_RANK, MLA_HEADS * QK_NOPE).astype(BF16)
            w_vt = w_ukv[..., QK_NOPE:].reshape(KV_RANK, MLA_HEADS * V_DIM).T.astype(BF16)
            cq, ckv, kr, kr2 = _mla_down(h, mods, mla_w_dq[j].astype(BF16), mla_w_dkv[j].astype(BF16),
                                         w_kr2, mla_q_norm[j], mla_kv_norm[j], rope_tab)
            ckv_states.append(ckv[:ROWS_CTX].reshape(BATCH, SEQ, KV_RANK))
            krope_states.append(kr[:ROWS_CTX].reshape(BATCH, SEQ, QK_ROPE))
            q = _q_up(cq, w_q, rope_tab)
            k_tok, vt_tok = _kv_expand(ckv, kr2, w_k, w_vt)
            kc = cache_krope[:, j].reshape(DEC_BATCH * PAST_LEN, QK_ROPE).astype(BF16)
            k_cache, vt_cache = _kv_expand(cache_ckv[:, j].reshape(DEC_BATCH * PAST_LEN, KV_RANK),
                                           jnp.concatenate([kc, kc], axis=-1), w_k, w_vt)
            o = _attention(q, k_tok, vt_tok, k_cache, vt_cache)
            h = _mm_postnorm(o, mla_w_o[j].astype(BF16), zero_bias, h, mods, 2, ln_g[i, 0], ln_b[i, 0])
        elif kind == 1:
            pc = _hyena_in(h, mods, hy_w_in[j].astype(BF16), hy_b_in[j], hy_conv_w[j], hy_conv_b[j])
            fp = (hy_f_w1[j], hy_f_b1[j], hy_f_freq1[j], hy_f_w2[j], hy_f_b2[j], hy_f_freq2[j],
                  hy_f_w3[j], hy_log_decay[j])
            skip = hy_skip[j].reshape(HY_ORDER, 1, D_MODEL)
            z_ctx = _hyena_mix_dense(pc, *_hyena_filters(SEQ, *fp), skip, BATCH, SEQ)
            z_lat = _hyena_mix_pair(pc, *_hyena_filters(DEC_SEQ, *fp), skip)
            h = _mm_postnorm((z_ctx, z_lat), hy_w_out[j].astype(BF16), hy_b_out[j],
                             h, mods, 2, ln_g[i, 0], ln_b[i, 0])
        else:
            c_ch, ns_ch = _dft_tables(FNET_CG)
            p, q = _fnet_chan(h, mods, jnp.concatenate([c_ch, -ns_ch], axis=-1))
            f_ctx = _fnet_pos(c_ch, ns_ch, p, q, BATCH, SEQ)
            f_lat = _fnet_pos_factored(p, q)
            h = _mm_postnorm((f_ctx, f_lat), fn_w_out[j].astype(BF16), fn_b_out[j],
                             h, mods, 2, ln_g[i, 0], ln_b[i, 0])
        h = _ffn(h, mods, *ffn_w, i, ln_g[i, 1], ln_b[i, 1], split_out=i == DEPTH - 1)

    y_prompt = h[0].reshape(BATCH, SEQ, D_MODEL)
    y_sample = h[1].reshape(DEC_BATCH, DEC_SEQ, D_MODEL)
    return (y_prompt, y_sample, jnp.stack(ckv_states, axis=1), jnp.stack(krope_states, axis=1))
```

```python
import functools
import math

import jax
import jax.numpy as jnp
from jax import lax
from jax.experimental import pallas as pl
from jax.experimental.pallas import tpu as pltpu

F32 = jnp.float32
BF16 = jnp.bfloat16

D_MODEL = 2048
BATCH = 16
SEQ = 256
DEPTH = 4
DEC_BATCH = 2
DEC_SEQ = 4096
PAST_LEN = 512
GRID_W = 64
N_MIXERS = 3
MLA_HEADS = 16
QK_NOPE = 128
QK_ROPE = 64
V_DIM = 128
Q_RANK = 512
KV_RANK = 512
ROPE_THETA = 10000.0
AXIS_ROPE = QK_ROPE // 2
HY_ORDER = 2
HY_DIRS = 2
HY_CONV = 3
HY_BANDS = 16
HY_EMB = 1 + 2 * HY_BANDS
HY_FW = 64
HY_SHIFT = 0.05
FNET_GROUPS = 8
FNET_CG = D_MODEL // FNET_GROUPS
D_FF = -(-8 * D_MODEL // (3 * 256)) * 256
DN_ALPHA = (2 * DEPTH) ** 0.25
LN_EPS = 1e-5
RMS_EPS = 1e-6
N_MOD = 6

ROWS_CTX = BATCH * SEQ
ROWS_LAT = DEC_BATCH * DEC_SEQ
ROWS = ROWS_CTX + ROWS_LAT
N_COND = 1 + DEC_BATCH
COND_PAD = 8
HEAD_W = QK_NOPE + 2 * QK_ROPE
ATT_SCALE = (QK_NOPE + QK_ROPE) ** -0.5

V7X_VMEM_BYTES = 64 * 2 ** 20
VMEM_CAP_BYTES = V7X_VMEM_BYTES * 7 // 8
VMEM_FLOOR_BYTES = 32 * 2 ** 20


def _vmem_limit(pipelined, resident=0, temps=0):
    est = 2 * pipelined + resident + temps
    return int(min(max(est, VMEM_FLOOR_BYTES), VMEM_CAP_BYTES))


def _params(semantics, vmem):
    return pltpu.CompilerParams(dimension_semantics=semantics, vmem_limit_bytes=vmem)


def _nbytes(shape, dtype):
    return math.prod(shape) * jnp.dtype(dtype).itemsize


def _group_of_tile(i, tm):
    n_ctx = ROWS_CTX // tm
    return jnp.where(i < n_ctx, 0, 1 + (i - n_ctx) // (DEC_SEQ // tm))


def _mod_spec(which, tm):
    return pl.BlockSpec((None, 1, D_MODEL),
                        lambda i, *_: (which * COND_PAD + _group_of_tile(i, tm), 0, 0))


def _row_spec(width=D_MODEL):
    return pl.BlockSpec((1, width), lambda *_: (0, 0))


def _as_tuple(x):
    return x if isinstance(x, tuple) else (x,)


def _token_specs(xs, tm):
    width = xs[0].shape[1]
    if len(xs) == 1:
        return [pl.BlockSpec((tm, width), lambda i, *_: (i, 0))]
    n_ctx = ROWS_CTX // tm
    return [pl.BlockSpec((tm, width), lambda i, *_: (jnp.minimum(i, n_ctx - 1), 0)),
            pl.BlockSpec((tm, width), lambda i, *_: (jnp.maximum(i - n_ctx, 0), 0))]


def _token_rows(refs, r):
    if len(refs) == 1:
        return refs[0][r, :]
    n_ctx = ROWS_CTX // refs[0].shape[0]
    return jnp.where(pl.program_id(0) < n_ctx, refs[0][r, :], refs[1][r, :])


def _modulate(h, shift, scale):
    return h * (1.0 + scale) + shift


def _post_norm(h, delta, g, b):
    z = DN_ALPHA * h + delta
    mu = jnp.mean(z, axis=-1, keepdims=True)
    zc = z - mu
    var = jnp.mean(zc * zc, axis=-1, keepdims=True)
    return zc * lax.rsqrt(var + LN_EPS) * g + b


def _rms_norm(x, g):
    ms = jnp.mean(x * x, axis=-1, keepdims=True)
    return x * lax.rsqrt(ms + RMS_EPS) * g


def _dot(a, b):
    return jnp.dot(a, b, preferred_element_type=F32)


def _modvec_kernel(c_ref, w_ref, b_ref, o_ref):
    a = jax.nn.silu(c_ref[...]).astype(BF16)
    o_ref[...] = _dot(a, w_ref[...].astype(BF16)) + b_ref[...]


def _modulation_vectors(cond, ada_w, ada_b):
    tn = 1024
    n = N_MOD * D_MODEL
    out = pl.pallas_call(
        _modvec_kernel,
        grid=(DEPTH, n // tn),
        in_specs=[pl.BlockSpec((COND_PAD, D_MODEL), lambda l, j: (0, 0)),
                  pl.BlockSpec((None, D_MODEL, tn), lambda l, j: (l, 0, j)),
                  pl.BlockSpec((None, 1, tn), lambda l, j: (l, 0, j))],
        out_specs=pl.BlockSpec((None, COND_PAD, tn), lambda l, j: (l, 0, j)),
        out_shape=jax.ShapeDtypeStruct((DEPTH, COND_PAD, n), F32),
        compiler_params=_params(("parallel", "parallel"),
                                _vmem_limit(_nbytes((D_MODEL, tn), F32),
                                            temps=_nbytes((D_MODEL, tn), BF16))),
        name="modvec",
    )(cond, ada_w, ada_b.reshape(DEPTH, 1, n))
    out = out.reshape(DEPTH, COND_PAD, N_MOD, D_MODEL).transpose(0, 2, 1, 3)
    return out.reshape(DEPTH, N_MOD * COND_PAD, 1, D_MODEL)


def _hyena_in_kernel(n_ctx, h_ref, hp_ref, hn_ref, sh_ref, sc_ref, w_ref, b_ref, cw_ref, cb_ref,
                     o_ref, u_ref):
    i = pl.program_id(0)
    tm = h_ref.shape[0]

    @pl.when(pl.program_id(1) == 0)
    def _():
        mod = lambda x_ref: _modulate(x_ref[...], sh_ref[...], sc_ref[...]).astype(BF16)
        u_ref[:SUB, :] = mod(hp_ref)
        u_ref[SUB:SUB + tm, :] = mod(h_ref)
        u_ref[SUB + tm:, :] = mod(hn_ref)

    length = jnp.where(i < n_ctx, SEQ, DEC_SEQ)
    half = tm // 2
    starts = (0, half)
    ys = [_dot(u_ref[r0:r0 + half + 2 * SUB, :], w_ref[...]) + b_ref[...] for r0 in starts]
    for r0, y in zip(starts, ys):
        inner = slice(SUB, SUB + half)
        before = pltpu.roll(y, 1, 0)[inner]
        after = pltpu.roll(y, y.shape[0] - 1, 0)[inner]
        pos = (lax.broadcasted_iota(jnp.int32, before.shape, 0) + (i * tm + r0)) & (length - 1)
        before = jnp.where(pos == 0, 0.0, before)
        after = jnp.where(pos == length - 1, 0.0, after)
        o_ref[r0:r0 + half, :] = (cw_ref[0:1, :] * before + cw_ref[1:2, :] * y[inner]
                                  + cw_ref[2:3, :] * after + cb_ref[...])


def _hyena_in(h, mods, w, b, conv_w, conv_b, tm=1024, tn=1024):
    assert SEQ & (SEQ - 1) == 0 and DEC_SEQ & (DEC_SEQ - 1) == 0 and tm % SEQ == 0 and DEC_SEQ % tm == 0
    k, n = w.shape
    per, halo, last = D_MODEL // tn, tm // SUB, ROWS // SUB - 1
    return pl.pallas_call(
        functools.partial(_hyena_in_kernel, ROWS_CTX // tm),
        grid=(ROWS // tm, n // tn),
        in_specs=[pl.BlockSpec((tm, k), lambda i, j: (i, 0)),
                  pl.BlockSpec((SUB, k), lambda i, j: (jnp.maximum(i * halo - 1, 0), 0)),
                  pl.BlockSpec((SUB, k), lambda i, j: (jnp.minimum((i + 1) * halo, last), 0)),
                  _mod_spec(0, tm), _mod_spec(1, tm),
                  pl.BlockSpec((k, tn), lambda i, j: (0, j)),
                  pl.BlockSpec((1, tn), lambda i, j: (0, j)),
                  pl.BlockSpec((HY_CONV, tn), lambda i, j: (0, j)),
                  pl.BlockSpec((1, tn), lambda i, j: (0, j))],
        out_specs=pl.BlockSpec((None, tm, tn), lambda i, j: (j // per, i, j % per)),
        out_shape=jax.ShapeDtypeStruct((n // D_MODEL, ROWS, D_MODEL), F32),
        scratch_shapes=[pltpu.VMEM((tm + 2 * SUB, k), BF16)],
        compiler_params=_params(("parallel", "arbitrary"),
                                _vmem_limit(_nbytes((tm, k), F32) + _nbytes((k, tn), BF16)
                                            + _nbytes((tm, tn), F32),
                                            resident=_nbytes((tm, k), BF16),
                                            temps=5 * _nbytes((tm, tn), F32))),
        name="hyena_in",
    )(h, h, h, mods, mods, w, b.reshape(1, n), conv_w, conv_b.reshape(1, n))


def _mm_postnorm_kernel(n_a, n_h, *refs):
    a_refs, refs = refs[:n_a], refs[n_a:]
    (w_ref, bias_ref), refs = refs[:2], refs[2:]
    h_refs, (gate_ref, g_ref, b_ref, o_ref) = refs[:n_h], refs[n_h:]
    half = o_ref.shape[0] // 2
    halves = (slice(0, half), slice(half, 2 * half))
    ys = [_dot(_token_rows(a_refs, r), w_ref[...]) + bias_ref[...] for r in halves]
    for r, y in zip(halves, ys):
        o_ref[r, :] = _post_norm(_token_rows(h_refs, r), gate_ref[...] * y, g_ref[...], b_ref[...])


def _mm_postnorm(a, w, bias, h, mods, which_gate, ln_g, ln_b, tm=512):
    a, h = _as_tuple(a), _as_tuple(h)
    k = a[0].shape[1]
    return pl.pallas_call(
        functools.partial(_mm_postnorm_kernel, len(a), len(h)),
        grid=(ROWS // tm,),
        in_specs=(_token_specs(a, tm) + [pl.BlockSpec((k, D_MODEL), lambda i: (0, 0)), _row_spec()]
                  + _token_specs(h, tm) + [_mod_spec(which_gate, tm), _row_spec(), _row_spec()]),
        out_specs=pl.BlockSpec((tm, D_MODEL), lambda i: (i, 0)),
        out_shape=jax.ShapeDtypeStruct((ROWS, D_MODEL), F32),
        compiler_params=_params(("parallel",),
                                _vmem_limit(len(a) * _nbytes((tm, k), BF16)
                                            + _nbytes((k, D_MODEL), BF16)
                                            + (1 + len(h)) * _nbytes((tm, D_MODEL), F32),
                                            temps=3 * _nbytes((tm, D_MODEL), F32))),
        name="mm_postnorm",
    )(*a, w, bias.reshape(1, D_MODEL), *h, mods, ln_g.reshape(1, D_MODEL), ln_b.reshape(1, D_MODEL))


def _ffn_kernel(n_ctx, h_ref, sh_ref, sc_ref, gate_ref, g_ref, b_ref, wg_ref, wu_ref, wd_ref,
                *refs):
    i, f = pl.program_id(0), pl.program_id(1)
    if n_ctx is None:
        o_ref, u_ref = refs
        acc_ref, outs = o_ref, ((o_ref, None),)
    else:
        octx_ref, olat_ref, u_ref, acc_ref = refs
        outs = ((octx_ref, i < n_ctx), (olat_ref, i >= n_ctx))

    @pl.when(f == 0)
    def _():
        u_ref[...] = _modulate(h_ref[...], sh_ref[...], sc_ref[...]).astype(BF16)
        acc_ref[...] = jnp.zeros_like(acc_ref)

    half = u_ref.shape[0] // 2
    halves = (slice(0, half), slice(half, 2 * half))

    def step(o_ref):
        proj = [(_dot(u_ref[r, :], wg_ref[...]), _dot(u_ref[r, :], wu_ref[...])) for r in halves]
        for r, (gate, up) in zip(halves, proj):
            act = (jax.nn.silu(gate) * up).astype(BF16)
            total = acc_ref[r, :] + _dot(act, wd_ref[...])
            if o_ref is None:
                acc_ref[r, :] = total
            else:
                o_ref[r, :] = _post_norm(h_ref[r, :], gate_ref[...] * total, g_ref[...], b_ref[...])

    last = f == pl.num_programs(1) - 1
    pl.when(jnp.logical_not(last))(lambda: step(None))
    for o_ref, mine in outs:
        pl.when(last if mine is None else jnp.logical_and(last, mine))(functools.partial(step, o_ref))


def _ffn(h, mods, w_gate, w_up, w_down, layer, ln_g, ln_b, split_out, tm=512, tf=512):
    tile = _nbytes((tm, D_MODEL), F32)
    if split_out:
        n_ctx = ROWS_CTX // tm
        out_specs = [pl.BlockSpec((tm, D_MODEL), lambda i, f: (jnp.minimum(i, n_ctx - 1), 0)),
                     pl.BlockSpec((tm, D_MODEL), lambda i, f: (jnp.maximum(i - n_ctx, 0), 0))]
        out_shape = [jax.ShapeDtypeStruct((ROWS_CTX, D_MODEL), F32),
                     jax.ShapeDtypeStruct((ROWS_LAT, D_MODEL), F32)]
        scratch = [pltpu.VMEM((tm, D_MODEL), BF16), pltpu.VMEM((tm, D_MODEL), F32)]
        pipelined, resident = 3 * tile, tile + tile // 2
    else:
        n_ctx = None
        out_specs = pl.BlockSpec((tm, D_MODEL), lambda i, f: (i, 0))
        out_shape = jax.ShapeDtypeStruct((ROWS, D_MODEL), F32)
        scratch = [pltpu.VMEM((tm, D_MODEL), BF16)]
        pipelined, resident = 2 * tile, tile // 2
    return pl.pallas_call(
        functools.partial(_ffn_kernel, n_ctx),
        grid=(ROWS // tm, D_FF // tf),
        in_specs=[pl.BlockSpec((tm, D_MODEL), lambda i, f: (i, 0)),
                  _mod_spec(3, tm), _mod_spec(4, tm), _mod_spec(5, tm),
                  _row_spec(), _row_spec(),
                  pl.BlockSpec((None, D_MODEL, tf), lambda i, f: (layer, 0, f)),
                  pl.BlockSpec((None, D_MODEL, tf), lambda i, f: (layer, 0, f)),
                  pl.BlockSpec((None, tf, D_MODEL), lambda i, f: (layer, f, 0))],
        out_specs=out_specs,
        out_shape=out_shape,
        scratch_shapes=scratch,
        compiler_params=_params(("arbitrary" if split_out else "parallel", "arbitrary"),
                                _vmem_limit(pipelined + 3 * _nbytes((D_MODEL, tf), BF16),
                                            resident=resident, temps=2 * tile)),
        name="ffn",
    )(h, mods, mods, mods, ln_g.reshape(1, D_MODEL), ln_b.reshape(1, D_MODEL),
      w_gate, w_up, w_down)


def _mla_down_kernel(n_h, *refs):
    h_refs, (sh_ref, sc_ref, wdq_ref, wdkv_ref, wkr_ref, qn_ref, kvn_ref, rope_ref,
             cq_ref, ckv_ref, kr_ref, kr2_ref) = refs[:n_h], refs[n_h:]
    half = cq_ref.shape[0] // 2
    halves = (slice(0, half), slice(half, 2 * half))
    us = [_modulate(_token_rows(h_refs, r), sh_ref[...], sc_ref[...]).astype(BF16) for r in halves]
    projs = [(_dot(u, wdq_ref[...]), _dot(u, wdkv_ref[...]), _dot(u, wkr_ref[...])) for u in us]
    for r, (q_lat, kv_lat, t) in zip(halves, projs):
        cq_ref[r, :] = _rms_norm(q_lat, qn_ref[...]).astype(BF16)
        ckv_ref[r, :] = _rms_norm(kv_lat, kvn_ref[...])
        kr_ref[r, :] = t[:, :QK_ROPE]
        v = t * rope_ref[r, :]
        kr2_ref[r, :] = (v + pltpu.roll(v, QK_ROPE, 1)).astype(BF16)


def _mla_down(h, mods, w_dq, w_dkv, w_kr2, q_norm, kv_norm, rope_tab, tm=512):
    h = _as_tuple(h)
    row = lambda width: pl.BlockSpec((tm, width), lambda i: (i, 0))
    full = lambda shape: pl.BlockSpec(shape, lambda i: (0, 0))
    return pl.pallas_call(
        functools.partial(_mla_down_kernel, len(h)),
        grid=(ROWS // tm,),
        in_specs=_token_specs(h, tm) + [
            _mod_spec(0, tm), _mod_spec(1, tm),
            full((D_MODEL, Q_RANK)), full((D_MODEL, KV_RANK)), full((D_MODEL, 2 * QK_ROPE)),
            _row_spec(Q_RANK), _row_spec(KV_RANK), row(2 * QK_ROPE)],
        out_specs=[row(Q_RANK), row(KV_RANK), row(QK_ROPE), row(2 * QK_ROPE)],
        out_shape=[jax.ShapeDtypeStruct((ROWS, Q_RANK), BF16),
                   jax.ShapeDtypeStruct((ROWS, KV_RANK), F32),
                   jax.ShapeDtypeStruct((ROWS, QK_ROPE), F32),
                   jax.ShapeDtypeStruct((ROWS, 2 * QK_ROPE), BF16)],
        compiler_params=_params(("parallel",),
                                _vmem_limit(len(h) * _nbytes((tm, D_MODEL), F32)
                                            + _nbytes((D_MODEL, Q_RANK + KV_RANK + 2 * QK_ROPE), BF16)
                                            + 3 * _nbytes((tm, KV_RANK), F32),
                                            temps=2 * _nbytes((tm, D_MODEL), F32))),
        name="mla_down",
    )(*h, mods, mods, w_dq, w_dkv, w_kr2, q_norm.reshape(1, Q_RANK), kv_norm.reshape(1, KV_RANK),
      rope_tab)


NT_DIMS = (((1,), (1,)), ((), ()))


def _q_up_kernel(cq_ref, w_ref, rope_ref, q_ref):
    cq, tab = cq_ref[...], rope_ref[...] * LOG2E_SCALE
    for h in range(MLA_HEADS):
        r = _dot(cq, w_ref[:, h * HEAD_W:(h + 1) * HEAD_W])
        q_ref[h, :, :QK_NOPE] = (r[:, :QK_NOPE] * LOG2E_SCALE).astype(BF16)
        q_ref[h, :, QK_NOPE:] = (r[:, QK_NOPE:] * tab).astype(BF16)


def _q_up(cq, w_q, rope_tab, tm=512):
    return pl.pallas_call(
        _q_up_kernel,
        grid=(ROWS // tm,),
        in_specs=[pl.BlockSpec((tm, Q_RANK), lambda i: (i, 0)),
                  pl.BlockSpec((Q_RANK, MLA_HEADS * HEAD_W), lambda i: (0, 0)),
                  pl.BlockSpec((tm, 2 * QK_ROPE), lambda i: (i, 0))],
        out_specs=pl.BlockSpec((MLA_HEADS, tm, HEAD_W), lambda i: (0, i, 0)),
        out_shape=jax.ShapeDtypeStruct((MLA_HEADS, ROWS, HEAD_W), BF16),
        compiler_params=_params(("parallel",),
                                _vmem_limit(_nbytes((Q_RANK + tm, MLA_HEADS * HEAD_W), BF16))),
        name="q_up",
    )(cq, w_q, rope_tab)


VT_PAD = 16
VT_ROWS = V_DIM + VT_PAD


def _kv_expand_kernel(ckv_ref, kr2_ref, wk_ref, wvt_ref, k_ref, vt_ref):
    c = ckv_ref[...].astype(BF16)
    vt = lax.dot_general(wvt_ref[...], c, NT_DIMS, preferred_element_type=F32).astype(BF16)
    ones = jnp.ones((VT_PAD, vt.shape[1]), BF16)
    for h in range(MLA_HEADS):
        vt_ref[h * VT_ROWS:h * VT_ROWS + V_DIM, :] = vt[h * V_DIM:(h + 1) * V_DIM]
        vt_ref[h * VT_ROWS + V_DIM:(h + 1) * VT_ROWS, :] = ones
    kr2 = kr2_ref[...]
    pair_w = 2 * QK_NOPE
    for g in range(MLA_HEADS // 2):
        r = _dot(c, wk_ref[:, g * pair_w:(g + 1) * pair_w]).astype(BF16)
        for e in range(2):
            k_ref[2 * g + e, :, :QK_NOPE] = r[:, e * QK_NOPE:(e + 1) * QK_NOPE]
            k_ref[2 * g + e, :, QK_NOPE:] = kr2


def _kv_expand(ckv, kr2, w_k, w_vt, tm=512):
    rows = ckv.shape[0]
    return pl.pallas_call(
        _kv_expand_kernel,
        grid=(rows // tm,),
        in_specs=[pl.BlockSpec((tm, KV_RANK), lambda i: (i, 0)),
                  pl.BlockSpec((tm, 2 * QK_ROPE), lambda i: (i, 0)),
                  pl.BlockSpec((KV_RANK, MLA_HEADS * QK_NOPE), lambda i: (0, 0)),
                  pl.BlockSpec((MLA_HEADS * V_DIM, KV_RANK), lambda i: (0, 0))],
        out_specs=[pl.BlockSpec((MLA_HEADS, tm, HEAD_W), lambda i: (0, i, 0)),
                   pl.BlockSpec((MLA_HEADS * VT_ROWS, tm), lambda i: (0, i))],
        out_shape=[jax.ShapeDtypeStruct((MLA_HEADS, rows, HEAD_W), BF16),
                   jax.ShapeDtypeStruct((MLA_HEADS * VT_ROWS, rows), BF16)],
        compiler_params=_params(("parallel",),
                                _vmem_limit(_nbytes((tm, MLA_HEADS * (HEAD_W + V_DIM)), BF16)
                                            + 2 * _nbytes((KV_RANK, MLA_HEADS * V_DIM), BF16),
                                            temps=_nbytes((MLA_HEADS * V_DIM, tm), F32))),
        name="kv_expand",
    )(ckv, kr2, w_k, w_vt)


ATT_CHUNK = 512
ATT_SKEW = 3
LOG2E_SCALE = ATT_SCALE * math.log2(math.e)


def _attn_scores(q, k):
    return lax.dot_general(k, q, NT_DIMS, preferred_element_type=F32)


def _attn_values(s, vt, carry):
    m = jnp.max(s, axis=0, keepdims=True)
    if carry is not None:
        m_old, acc_old = carry
        m = jnp.maximum(m_old, m)
    acc = _dot(vt, jnp.exp2(s - m).astype(BF16))
    if carry is not None:
        acc = jnp.exp2(m_old - m) * acc_old + acc
    return m, acc


def _attn_output(acc):
    return (acc[:V_DIM] / acc[V_DIM:V_DIM + 1]).T.astype(BF16)


def _attn_ctx_kernel(q_ref, k_ref, vt_ref, o_ref):
    for h in range(MLA_HEADS):
        _, acc = _attn_values(_attn_scores(q_ref[h], k_ref[h]),
                              vt_ref[h * VT_ROWS:(h + 1) * VT_ROWS, :], None)
        o_ref[:, h * V_DIM:(h + 1) * V_DIM] = _attn_output(acc)


def _attn_lat_kernel(q_ref, k_ref, vt_ref, kc_ref, vtc_ref, prev_ref, o_ref):
    del prev_ref
    q = q_ref[...]
    n_tok = DEC_SEQ // ATT_CHUNK
    rows = lambda c: slice(c * ATT_CHUNK, (c + 1) * ATT_CHUNK)
    keys = [k_ref.at[rows(c), :] for c in range(n_tok)] + [kc_ref]
    vals = [vt_ref.at[:, rows(c)] for c in range(n_tok)] + [vtc_ref]
    carry = None
    scores = [_attn_scores(q, keys[c][...]) for c in range(ATT_SKEW)]
    for c in range(n_tok + 1):
        if c + ATT_SKEW <= n_tok:
            scores.append(_attn_scores(q, keys[c + ATT_SKEW][...]))
        carry = _attn_values(scores[c], vals[c][...], carry)
    o_ref[...] = _attn_output(carry[1])


def _attention(q, k_tok, vt_tok, k_cache, vt_cache, tq=2048):
    assert DEC_SEQ % ATT_CHUNK == 0
    out_shape = jax.ShapeDtypeStruct((ROWS, MLA_HEADS * V_DIM), BF16)
    o = pl.pallas_call(
        _attn_ctx_kernel,
        grid=(BATCH,),
        in_specs=[pl.BlockSpec((MLA_HEADS, SEQ, HEAD_W), lambda s: (0, s, 0)),
                  pl.BlockSpec((MLA_HEADS, SEQ, HEAD_W), lambda s: (0, s, 0)),
                  pl.BlockSpec((MLA_HEADS * VT_ROWS, SEQ), lambda s: (0, s))],
        out_specs=pl.BlockSpec((SEQ, MLA_HEADS * V_DIM), lambda s: (s, 0)),
        out_shape=out_shape,
        compiler_params=_params(("parallel",), VMEM_FLOOR_BYTES),
        name="attn_ctx",
    )(q, k_tok, vt_tok)

    lat0 = ROWS_CTX // DEC_SEQ
    q0 = ROWS_CTX // tq
    nq = DEC_SEQ // tq
    return pl.pallas_call(
        _attn_lat_kernel,
        grid=(DEC_BATCH, MLA_HEADS, nq),
        in_specs=[pl.BlockSpec((None, tq, HEAD_W), lambda b, h, i: (h, q0 + b * nq + i, 0)),
                  pl.BlockSpec((None, DEC_SEQ, HEAD_W), lambda b, h, i: (h, lat0 + b, 0)),
                  pl.BlockSpec((VT_ROWS, DEC_SEQ), lambda b, h, i: (h, lat0 + b)),
                  pl.BlockSpec((None, PAST_LEN, HEAD_W), lambda b, h, i: (h, b, 0)),
                  pl.BlockSpec((VT_ROWS, PAST_LEN), lambda b, h, i: (h, b)),
                  pl.BlockSpec(memory_space=pl.ANY)],
        out_specs=pl.BlockSpec((tq, V_DIM), lambda b, h, i: (q0 + b * nq + i, h)),
        out_shape=out_shape,
        input_output_aliases={5: 0},
        compiler_params=_params(("parallel", "parallel", "arbitrary"),
                                _vmem_limit(_nbytes((DEC_SEQ + PAST_LEN, HEAD_W + V_DIM), BF16),
                                            temps=8 * _nbytes((ATT_CHUNK, tq), F32))),
        name="attn_lat",
    )(q, k_tok, vt_tok, k_cache, vt_cache, o)


def _mm_kernel(a_ref, b_ref, o_ref, acc_ref):
    k = pl.program_id(2)

    @pl.when(k == 0)
    def _():
        acc_ref[...] = jnp.zeros_like(acc_ref)

    acc_ref[...] += _dot(a_ref[...], b_ref[...].astype(BF16))

    @pl.when(k == pl.num_programs(2) - 1)
    def _():
        o_ref[...] = acc_ref[...].astype(o_ref.dtype)


def _mm(a, b, out_dtype=F32, tm=1024, tn=1024, tk=512):
    m, kk = a.shape
    n = b.shape[1]
    tm, tn, tk = min(tm, m), min(tn, n), min(tk, kk)
    return pl.pallas_call(
        _mm_kernel,
        grid=(m // tm, n // tn, kk // tk),
        in_specs=[pl.BlockSpec((tm, tk), lambda i, j, k: (i, k)),
                  pl.BlockSpec((tk, tn), lambda i, j, k: (k, j))],
        out_specs=pl.BlockSpec((tm, tn), lambda i, j, k: (i, j)),
        out_shape=jax.ShapeDtypeStruct((m, n), out_dtype),
        scratch_shapes=[pltpu.VMEM((tm, tn), F32)],
        compiler_params=_params(("parallel", "parallel", "arbitrary"), VMEM_FLOOR_BYTES),
        name="mm",
    )(a, b)


def _seq_tiles(length):
    if length >= 1024:
        return 1024, 512, 1024
    return length, D_MODEL, length


def _dft_fwd_kernel(c_ref, s_ref, z_ref, kr_ref, ks_ref, yr_ref, ys_ref, accr_ref, accs_ref):
    k = pl.program_id(3)

    @pl.when(k == 0)
    def _():
        accr_ref[...] = jnp.zeros_like(accr_ref)
        accs_ref[...] = jnp.zeros_like(accs_ref)

    z = z_ref[...].astype(BF16)
    accr_ref[...] += _dot(c_ref[...], z)
    accs_ref[...] += _dot(s_ref[...], z)

    @pl.when(k == pl.num_programs(3) - 1)
    def _():
        zr, zs, kr, ks = accr_ref[...], accs_ref[...], kr_ref[...], ks_ref[...]
        yr_ref[...] = (zr * kr - zs * ks).astype(BF16)
        ys_ref[...] = (zr * ks + zs * kr).astype(BF16)


def _dft_fwd(tabs, z, z_which, kr, ks, order, n_seq, length):
    tf, tn, tk = _seq_tiles(length)
    nb, nf, nk = D_MODEL // tn, length // tf, length // tk
    kc0 = order * nb
    out = jax.ShapeDtypeStruct((n_seq * length, D_MODEL), BF16)
    return pl.pallas_call(
        _dft_fwd_kernel,
        grid=(n_seq, nf, nb, nk),
        in_specs=[pl.BlockSpec((tf, tk), lambda s, f, n, k: (f, k)),
                  pl.BlockSpec((tf, tk), lambda s, f, n, k: (f, k)),
                  pl.BlockSpec((None, tk, tn), lambda s, f, n, k: (z_which, s * nk + k, n)),
                  pl.BlockSpec((tf, tn), lambda s, f, n, k: (f, kc0 + n)),
                  pl.BlockSpec((tf, tn), lambda s, f, n, k: (f, kc0 + n))],
        out_specs=[pl.BlockSpec((tf, tn), lambda s, f, n, k: (s * nf + f, n)),
                   pl.BlockSpec((tf, tn), lambda s, f, n, k: (s * nf + f, n))],
        out_shape=[out, out],
        scratch_shapes=[pltpu.VMEM((tf, tn), F32), pltpu.VMEM((tf, tn), F32)],
        compiler_params=_params(("parallel", "parallel", "parallel", "arbitrary"),
                                _vmem_limit(2 * _nbytes((tf, tk), BF16) + _nbytes((tk, tn), F32)
                                            + 2 * _nbytes((tf, tn), F32) + 2 * _nbytes((tf, tn), BF16),
                                            resident=2 * _nbytes((tf, tn), F32),
                                            temps=4 * _nbytes((tf, tn), F32))),
        name="dft_fwd",
    )(tabs["c"], tabs["s"], z, kr, ks)


def _dft_inv_kernel(inv_len, ct_ref, st_ref, yr_ref, ys_ref, z_ref, gate_ref, skip_ref,
                    o_ref, acc_ref):
    k = pl.program_id(3)

    @pl.when(k == 0)
    def _():
        acc_ref[...] = jnp.zeros_like(acc_ref)

    acc_ref[...] += _dot(ct_ref[...], yr_ref[...]) + _dot(st_ref[...], ys_ref[...])

    @pl.when(k == pl.num_programs(3) - 1)
    def _():
        y = acc_ref[...] * inv_len + skip_ref[...] * z_ref[...]
        o_ref[...] = (gate_ref[...] * y).astype(o_ref.dtype)


def _dft_inv(tabs, yr, ys, z, z_which, gate, gate_which, skip, order, n_seq, length, out_dtype):
    tt, tn, tk = _seq_tiles(length)
    nb, nt, nk = D_MODEL // tn, length // tt, length // tk
    return pl.pallas_call(
        functools.partial(_dft_inv_kernel, 1.0 / length),
        grid=(n_seq, nt, nb, nk),
        in_specs=[pl.BlockSpec((tt, tk), lambda s, t, n, k: (t, k)),
                  pl.BlockSpec((tt, tk), lambda s, t, n, k: (t, k)),
                  pl.BlockSpec((tk, tn), lambda s, t, n, k: (s * nk + k, n)),
                  pl.BlockSpec((tk, tn), lambda s, t, n, k: (s * nk + k, n)),
                  pl.BlockSpec((None, tt, tn), lambda s, t, n, k: (z_which, s * nt + t, n)),
                  pl.BlockSpec((None, tt, tn), lambda s, t, n, k: (gate_which, s * nt + t, n)),
                  pl.BlockSpec((None, 1, tn), lambda s, t, n, k: (order, 0, n))],
        out_specs=pl.BlockSpec((tt, tn), lambda s, t, n, k: (s * nt + t, n)),
        out_shape=jax.ShapeDtypeStruct((n_seq * length, D_MODEL), out_dtype),
        scratch_shapes=[pltpu.VMEM((tt, tn), F32)],
        compiler_params=_params(("parallel", "parallel", "parallel", "arbitrary"),
                                _vmem_limit(2 * _nbytes((tt, tk), BF16) + 2 * _nbytes((tk, tn), BF16)
                                            + 3 * _nbytes((tt, tn), F32),
                                            resident=_nbytes((tt, tn), F32),
                                            temps=3 * _nbytes((tt, tn), F32))),
        name="dft_inv",
    )(tabs["ct"], tabs["st"], yr, ys, z, gate, skip)


TW_LANES = 128
SUB = 8
CT_ROWS = 2 * SUB
CT_COLS = 1024


def _lane_tile(x, width):
    return jnp.tile(x, (1, width // x.shape[-1]))


def _kron_sub(f):
    return jnp.kron(f, jnp.eye(SUB, dtype=f.dtype))


def _sub_rows(x, h):
    part = x[:, h * SUB:(h + 1) * SUB, :]
    return part.reshape(part.shape[0] * SUB, part.shape[2])


def _from_sub_rows(parts):
    split = [p.reshape(p.shape[0] // SUB, SUB, p.shape[1]) for p in parts]
    return jnp.concatenate(split, axis=1)


def _ct_stage_a_kernel(n_in, *refs):
    x_refs, (fa_ref, cw_ref, sw_ref, br_ref, bi_ref) = refs[:n_in], refs[n_in:]
    fa = fa_ref[...]
    half = fa.shape[0] // 2
    width = br_ref.shape[-1]
    xs = [r[...].astype(F32) for r in x_refs]
    b_re, b_im = [], []
    for h in range(CT_ROWS // SUB):
        x = jnp.concatenate([_sub_rows(x, h) for x in xs], axis=0).astype(BF16)
        a = _dot(fa, x)
        ar, ai = a[:half], a[half:]
        cw, sw = _lane_tile(cw_ref[h], width), _lane_tile(sw_ref[h], width)
        b_re.append(ar * cw + ai * sw)
        b_im.append(ai * cw - ar * sw)
    br_ref[...] = _from_sub_rows(b_re).astype(BF16)
    bi_ref[...] = _from_sub_rows(b_im).astype(BF16)


def _ct_stage_a(xs, n_seq, fa, cw, sw):
    n1 = fa.shape[0] // (2 * SUB)
    n2 = cw.shape[0] * CT_ROWS
    width = xs[0][0].shape[-1]
    tn2, tw = CT_ROWS, CT_COLS
    in_specs, blocks = [], 0
    for arr, prefix in xs:
        rows_in = arr.shape[-3]
        lead = (None,) * (arr.ndim - 3)
        in_specs.append(pl.BlockSpec(lead + (rows_in, tn2, tw),
                                     lambda s, i, c, prefix=prefix: prefix(s) + (0, i, c)))
        blocks += _nbytes((rows_in, tn2, tw), arr.dtype)
    twid = pl.BlockSpec((None,) + cw.shape[1:], lambda s, i, c: (i, 0, 0, 0))
    in_specs += [pl.BlockSpec(fa.shape, lambda s, i, c: (0, 0)), twid, twid]
    out = jax.ShapeDtypeStruct((n_seq, n1, n2, width), BF16)
    out_spec = pl.BlockSpec((None, n1, tn2, tw), lambda s, i, c: (s, 0, i, c))
    return pl.pallas_call(
        functools.partial(_ct_stage_a_kernel, len(xs)),
        grid=(n_seq, n2 // tn2, width // tw),
        in_specs=in_specs,
        out_specs=[out_spec, out_spec],
        out_shape=[out, out],
        compiler_params=_params(("parallel", "parallel", "parallel"),
                                _vmem_limit(blocks + 2 * _nbytes((n1, tn2, tw), BF16)
                                            + _nbytes(fa.shape, BF16),
                                            temps=8 * _nbytes((n1, tn2, tw), F32))),
        name="ct_stage_a",
    )(*[arr for arr, _ in xs], fa, cw, sw)


def _ct_mid_kernel(tk1, br_ref, bi_ref, ur_ref, ui_ref, *refs):
    partners, refs = refs[:2 * tk1], refs[2 * tk1:]
    fb_ref, fbi_ref, fbp0_ref, fbp1_ref, cw_ref, sw_ref, vr_ref, vi_ref = refs
    fb, fbi = fb_ref[...], fbi_ref[...]
    half = fb.shape[0] // 2
    width = br_ref.shape[-1]
    first_tile = pl.program_id(0) == 0
    for j in range(tk1):
        stack = lambda re_ref, im_ref, p=j: jnp.concatenate([re_ref[p], im_ref[p]], axis=0)
        fbp = fbp1_ref[...] if j else jnp.where(first_tile, fbp0_ref[...], fbp1_ref[...])
        u_own = _dot(fb[:half], stack(ur_ref, ui_ref))
        u_neg = _dot(fbp, stack(partners[2 * j], partners[2 * j + 1], 0))
        kr = 0.5 * (u_own + u_neg)
        ki = 0.5 * (u_neg - u_own)
        x = _dot(fb, stack(br_ref, bi_ref))
        xr, xi = x[:half], x[half:]
        y = jnp.concatenate([xr * kr - xi * ki, xr * ki + xi * kr], axis=0).astype(BF16)
        v = _dot(fbi, y)
        vr, vi = v[:half], v[half:]
        cw, sw = _lane_tile(cw_ref[j], width), _lane_tile(sw_ref[j], width)
        vr_ref[j] = (vr * cw - vi * sw).astype(BF16)
        vi_ref[j] = (vi * cw + vr * sw).astype(BF16)


def _ct_mid(br, bi, ur, ui, order, fb, fbi, fbp0, fbp1, cw, sw, tk1=4, td=1024):
    n1, n2, d = br.shape
    nd = d // td
    data = pl.BlockSpec((tk1, n2, td), lambda i, j: (i, 0, j))
    coef = pl.BlockSpec((tk1, n2, td), lambda i, j: (i, 0, order * nd + j))
    page = lambda p: pl.BlockSpec((1, n2, td),
                                  lambda i, j: ((n1 - (i * tk1 + p)) % n1, 0, order * nd + j))
    partners = [page(p) for p in range(tk1) for _ in range(2)]
    mat = pl.BlockSpec(fb.shape, lambda i, j: (0, 0))
    mat_p = pl.BlockSpec(fbp0.shape, lambda i, j: (0, 0))
    tw = pl.BlockSpec((tk1, n2, TW_LANES), lambda i, j: (i, 0, 0))
    out = jax.ShapeDtypeStruct((n1, n2, d), BF16)
    return pl.pallas_call(
        functools.partial(_ct_mid_kernel, tk1),
        grid=(n1 // tk1, nd),
        in_specs=[data, data, coef, coef] + partners + [mat, mat, mat_p, mat_p, tw, tw],
        out_specs=[data, data],
        out_shape=[out, out],
        compiler_params=_params(("parallel", "parallel"),
                                _vmem_limit(8 * _nbytes((tk1, n2, td), BF16),
                                            temps=10 * _nbytes((2 * n2, td), F32))),
        name="ct_mid",
    )(br, bi, ur, ui, *([ur, ui] * tk1), fb, fbi, fbp0, fbp1, cw, sw)


def _ct_inv_a_kernel(scale, vr_ref, vi_ref, fai_ref, z0_ref, z1_ref, g0_ref, g1_ref, skip_ref,
                     o_ref):
    fai = fai_ref[...]
    half = fai.shape[0] // 2
    skip = skip_ref[...]
    vr, vi = vr_ref[...].astype(F32), vi_ref[...].astype(F32)
    zs, gs = (z0_ref[...], z1_ref[...]), (g0_ref[...], g1_ref[...])
    outs = ([], [])
    for h in range(CT_ROWS // SUB):
        v = jnp.concatenate([_sub_rows(vr, h), _sub_rows(vi, h)], axis=0).astype(BF16)
        y = _dot(fai, v) * scale
        for b, yb in enumerate((y[:half], y[half:])):
            outs[b].append(_sub_rows(gs[b], h) * (yb + skip * _sub_rows(zs[b], h)))
    for b in range(2):
        o_ref[b] = _from_sub_rows(outs[b]).astype(o_ref.dtype)


def _ct_inv_a(vr, vi, fai, z, z_which, gate, gate_which, skip, order, out_dtype):
    n1, n2, d = vr.shape
    rows = fai.shape[0] // (2 * SUB)
    tn2, tw = CT_ROWS, CT_COLS // 2
    spec = pl.BlockSpec((n1, tn2, tw), lambda i, c: (0, i, c))
    pair = lambda which, b: pl.BlockSpec((None, None, rows, tn2, tw),
                                         lambda i, c: (which[0], which[1] + b, 0, i, c))
    return pl.pallas_call(
        functools.partial(_ct_inv_a_kernel, 1.0 / (n1 * n2)),
        grid=(n2 // tn2, d // tw),
        in_specs=[spec, spec, pl.BlockSpec(fai.shape, lambda i, c: (0, 0)),
                  pair(z_which, 0), pair(z_which, 1), pair(gate_which, 0), pair(gate_which, 1),
                  pl.BlockSpec((None, 1, tw), lambda i, c: (order, 0, c))],
        out_specs=pl.BlockSpec((2, rows, tn2, tw), lambda i, c: (0, 0, i, c)),
        out_shape=jax.ShapeDtypeStruct((2, rows, n2, d), out_dtype),
        compiler_params=_params(("parallel", "parallel"),
                                _vmem_limit(2 * _nbytes((n1, tn2, tw), BF16)
                                            + 6 * _nbytes((rows, tn2, tw), F32)
                                            + _nbytes(fai.shape, BF16),
                                            temps=8 * _nbytes((n1, tn2, tw), F32))),
        name="ct_inv_a",
    )(vr, vi, fai, z, z, gate, gate, skip)


def _ct_real_b_kernel(scale, br_ref, bi_ref, fb_ref, o_ref, so_ref):
    fb = fb_ref[...]
    for j in range(br_ref.shape[0]):
        so_ref[:, j, :] = _dot(fb, jnp.concatenate([br_ref[j], bi_ref[j]], axis=0)) * scale
    o_ref[...] = so_ref[...].astype(o_ref.dtype)


def _ct_real_b(br, bi, fb_re, scale):
    n_seq, n1, n2, d = br.shape
    tk1, tw = CT_ROWS, CT_COLS
    blk = pl.BlockSpec((None, tk1, n2, tw), lambda s, i, c: (s, i, 0, c))
    return pl.pallas_call(
        functools.partial(_ct_real_b_kernel, scale),
        grid=(n_seq, n1 // tk1, d // tw),
        in_specs=[blk, blk, pl.BlockSpec(fb_re.shape, lambda s, i, c: (0, 0))],
        out_specs=pl.BlockSpec((None, n2, tk1, tw), lambda s, i, c: (s, 0, i, c)),
        out_shape=jax.ShapeDtypeStruct((n_seq, n2, n1, d), BF16),
        scratch_shapes=[pltpu.VMEM((n2, tk1, tw), F32)],
        compiler_params=_params(("parallel", "parallel", "parallel"),
                                _vmem_limit(3 * _nbytes((tk1, n2, tw), BF16),
                                            resident=_nbytes((n2, tk1, tw), F32),
                                            temps=2 * _nbytes((n2, tk1, tw), F32))),
        name="ct_real_b",
    )(br, bi, fb_re)


def _cos_sin(num, den):
    ang = (num % den).astype(F32) * (2.0 * math.pi / den)
    return jnp.cos(ang), jnp.sin(ang)


def _ct_tables(n1, n2):
    i1 = jnp.arange(n1, dtype=jnp.int32)
    i2 = jnp.arange(n2, dtype=jnp.int32)
    c1, s1 = _cos_sin(i1[:, None] * i1[None, :], n1)
    c2, s2 = _cos_sin(i2[:, None] * i2[None, :], n2)
    cw, sw = _cos_sin(i2[:, None] * i1[None, :], n1 * n2)
    lanes = lambda t: jnp.broadcast_to(t[..., None], t.shape + (TW_LANES,))

    def stage_a_rows(t):
        t = t.reshape(n2 // CT_ROWS, CT_ROWS // SUB, SUB, n1)
        return lanes(jnp.swapaxes(t, 2, 3).reshape(n2 // CT_ROWS, CT_ROWS // SUB, n1 * SUB))

    return {"c1": c1, "s1": s1, "c2": c2, "s2": s2,
            "cw_a": stage_a_rows(cw), "sw_a": stage_a_rows(sw),
            "cw_b": lanes(cw.T), "sw_b": lanes(sw.T)}


def _filter_rows(feat_ref, t_ref, w1_ref, b1_ref, fr1_ref, w2_ref, b2_ref, fr2_ref, w3_ref,
                 decay_ref):
    x = jnp.sin(fr1_ref[...] * (_dot(feat_ref[...].astype(BF16), w1_ref[...].astype(BF16))
                                + b1_ref[...]))
    x = jnp.sin(fr2_ref[...] * (_dot(x.astype(BF16), w2_ref[...].astype(BF16)) + b2_ref[...]))
    h = _dot(x.astype(BF16), w3_ref[...].astype(BF16))
    return h * (jnp.exp(-t_ref[...] * jnp.exp(decay_ref[...])) + HY_SHIFT)


def _filter_stats_kernel(*refs):
    ss_ref = refs[-1]
    h = _filter_rows(*refs[:-1])

    @pl.when(pl.program_id(0) == 0)
    def _():
        ss_ref[...] = jnp.zeros_like(ss_ref)

    ss_ref[...] += jnp.sum(h * h, axis=0, keepdims=True)


def _filter_emit_kernel(*refs):
    ss_ref, a_ref, b_ref = refs[-3:]
    h = _filter_rows(*refs[:-3])
    n_dir = a_ref.shape[1]
    ss = ss_ref[...]
    norm = lax.rsqrt(ss[:, :n_dir] + ss[:, n_dir:] + 1e-12)
    fwd = h[:, :n_dir] * norm
    bwd = h[:, n_dir:] * norm
    row = lax.broadcasted_iota(jnp.int32, bwd.shape, 0) + pl.program_id(0) * bwd.shape[0]
    bwd = jnp.where(row == 0, 0.0, bwd)
    a_ref[...] = (fwd + bwd).astype(BF16)
    b_ref[...] = (fwd - bwd).astype(BF16)


def _hyena_filters(length, f_w1, f_b1, f_freq1, f_w2, f_b2, f_freq2, f_w3, log_decay):
    t = jnp.linspace(0.0, 1.0, length, dtype=F32)[:, None]
    t_idx = jnp.arange(length, dtype=F32)[:, None]
    bands = jnp.linspace(1e-4, HY_BANDS - 1, HY_BANDS, dtype=F32)
    w = 2.0 * math.pi * t_idx * bands / length
    feat = jnp.concatenate([t, jnp.cos(w), -jnp.sin(w)], axis=-1)
    emb_pad = 128
    feat = jnp.pad(feat, ((0, 0), (0, emb_pad - HY_EMB)))
    w1 = jnp.pad(f_w1, ((0, emb_pad - HY_EMB), (0, 0)))
    n_all = HY_DIRS * HY_ORDER * D_MODEL
    n_dir = HY_ORDER * D_MODEL
    tm = 256
    full = lambda shape: pl.BlockSpec(shape, lambda i: (0, 0))
    mlp_specs = [pl.BlockSpec((tm, emb_pad), lambda i: (i, 0)),
                 pl.BlockSpec((tm, 1), lambda i: (i, 0)),
                 full((emb_pad, HY_FW)), full((1, HY_FW)), full((1, HY_FW)),
                 full((HY_FW, HY_FW)), full((1, HY_FW)), full((1, HY_FW)),
                 full((HY_FW, n_all)), full((1, n_all))]
    mlp_args = (feat, t, w1, f_b1.reshape(1, HY_FW), f_freq1.reshape(1, HY_FW), f_w2,
                f_b2.reshape(1, HY_FW), f_freq2.reshape(1, HY_FW), f_w3, log_decay.reshape(1, n_all))
    vmem = _vmem_limit(_nbytes((tm, n_all), F32) + _nbytes((HY_FW, n_all), F32),
                       temps=4 * _nbytes((tm, n_all), F32))
    ss = pl.pallas_call(
        _filter_stats_kernel,
        grid=(length // tm,),
        in_specs=mlp_specs,
        out_specs=full((1, n_all)),
        out_shape=jax.ShapeDtypeStruct((1, n_all), F32),
        compiler_params=_params(("arbitrary",), vmem),
        name="filter_stats",
    )(*mlp_args)
    comb = jax.ShapeDtypeStruct((length, n_dir), BF16)
    return pl.pallas_call(
        _filter_emit_kernel,
        grid=(length // tm,),
        in_specs=mlp_specs + [full((1, n_all))],
        out_specs=[pl.BlockSpec((tm, n_dir), lambda i: (i, 0)), pl.BlockSpec((tm, n_dir), lambda i: (i, 0))],
        out_shape=[comb, comb],
        compiler_params=_params(("parallel",), vmem),
        name="filter_emit",
    )(*mlp_args, ss)


def _cis_product(row_hi, row_lo, period):
    def cis(phase):
        ang = (phase % period).astype(F32) * (2.0 * math.pi / period)
        return jnp.cos(ang)[:, :, None], jnp.sin(ang)[:, :, None]
    (c1, s1), (c0, s0) = cis(row_hi), cis(row_lo)
    c0, s0 = jnp.swapaxes(c0, 1, 2), jnp.swapaxes(s0, 1, 2)
    rows = row_hi.shape[0]
    return ((c1 * c0 - s1 * s0).reshape(rows, -1), (s1 * c0 + c1 * s0).reshape(rows, -1))


def _odd_dft_tables(length):
    split = 1 << (length.bit_length() // 2)
    r = jnp.arange(length, dtype=jnp.int32)[:, None]
    hi = jnp.arange(length // split, dtype=jnp.int32)[None, :] * split
    lo = jnp.arange(split, dtype=jnp.int32)[None, :]
    c, s = _cis_product((2 * r + 1) * hi, (2 * r + 1) * lo, 4 * length)
    ct, st = _cis_product(r * (2 * hi), r * (2 * lo + 1), 4 * length)
    return {"c": c.astype(BF16), "s": s.astype(BF16), "ct": ct.astype(BF16), "st": st.astype(BF16)}


def _dft_tables(length):
    split = 1 << (length.bit_length() // 2)
    r = jnp.arange(length, dtype=jnp.int32)[:, None]
    hi = jnp.arange(length // split, dtype=jnp.int32)[None, :] * split
    lo = jnp.arange(split, dtype=jnp.int32)[None, :]
    c, s = _cis_product(r * hi, r * lo, length)
    return c.astype(BF16), (-s).astype(BF16)


def _hyena_mix_dense(pc, filt_p, filt_m, skip, n_seq, length):
    tabs = _odd_dft_tables(length)
    kr, ks = _mm(tabs["c"], filt_p), _mm(tabs["s"], filt_m)
    yr, ys = _dft_fwd(tabs, pc, 0, kr, ks, 0, n_seq, length)
    z1 = _dft_inv(tabs, yr, ys, pc, 0, pc, 1, skip, 0, n_seq, length, F32)[None]
    yr, ys = _dft_fwd(tabs, z1, 0, kr, ks, 1, n_seq, length)
    return _dft_inv(tabs, yr, ys, z1, 0, pc, 2, skip, 1, n_seq, length, BF16)


HY_N1, HY_N2 = 64, 128


def _hyena_mix_pair(pc, filt_p, filt_m, skip):
    assert DEC_BATCH == 2 and HY_N1 * HY_N2 == 2 * DEC_SEQ and ROWS % DEC_SEQ == 0
    n1, n2, rows_in = HY_N1, HY_N2, HY_N1 // 2
    t = _ct_tables(n1, n2)
    c_in, s_in = t["c1"][:, :rows_in], t["s1"][:, :rows_in]
    fa = _kron_sub(jnp.block([[c_in, s_in], [-s_in, c_in]])).astype(BF16)
    fb = jnp.block([[t["c2"], t["s2"]], [-t["s2"], t["c2"]]]).astype(BF16)
    fbi = jnp.block([[t["c2"], -t["s2"]], [t["s2"], t["c2"]]]).astype(BF16)
    mirror = lambda shift: (-jnp.arange(n2) - shift) % n2
    fbp0, fbp1 = [jnp.concatenate([t["c2"][mirror(sh)], t["s2"][mirror(sh)]], axis=1).astype(BF16)
                  for sh in (0, 1)]
    c_out, s_out = t["c1"][:rows_in], t["s1"][:rows_in]
    fai = _kron_sub(jnp.block([[c_out, -s_out], [s_out, c_out]])).astype(BF16)
    n_filt = HY_ORDER * D_MODEL
    filt_view = lambda f: (f.reshape(rows_in, n2, n_filt), lambda s: ())
    ur, ui = _ct_stage_a([filt_view(filt_p), filt_view(filt_m)], 1, fa, t["cw_a"], t["sw_a"])

    pc = pc.reshape(3, ROWS // DEC_SEQ, rows_in, n2, D_MODEL)
    z, z0, out_dtypes = pc, ROWS_CTX // DEC_SEQ, (F32, BF16)
    for order in range(HY_ORDER):
        br, bi = _ct_stage_a([(z, lambda s, z0=z0: (0, z0)), (z, lambda s, z0=z0: (0, z0 + 1))], 1, fa,
                             t["cw_a"], t["sw_a"])
        vr, vi = _ct_mid(br[0], bi[0], ur[0], ui[0], order, fb, fbi, fbp0, fbp1, t["cw_b"], t["sw_b"])
        z = _ct_inv_a(vr, vi, fai, z, (0, z0), pc, (1 + order, ROWS_CTX // DEC_SEQ), skip, order,
                      out_dtypes[order])[None]
        z0 = 0
    return z.reshape(ROWS_LAT, D_MODEL)


def _fnet_chan_kernel(h_ref, sh_ref, sc_ref, w_ref, p_ref, q_ref):
    u = _modulate(h_ref[...], sh_ref[...], sc_ref[...]).astype(BF16)
    w = w_ref[...]
    for g in range(FNET_GROUPS):
        cols = slice(g * FNET_CG, (g + 1) * FNET_CG)
        r = _dot(u[:, cols], w)
        p_ref[:, cols] = r[:, :FNET_CG].astype(BF16)
        q_ref[:, cols] = r[:, FNET_CG:].astype(BF16)


def _fnet_chan(h, mods, w_cs, tm=512):
    row = pl.BlockSpec((tm, D_MODEL), lambda i: (i, 0))
    out = jax.ShapeDtypeStruct((ROWS, D_MODEL), BF16)
    return pl.pallas_call(
        _fnet_chan_kernel,
        grid=(ROWS // tm,),
        in_specs=[row, _mod_spec(0, tm), _mod_spec(1, tm),
                  pl.BlockSpec((FNET_CG, 2 * FNET_CG), lambda i: (0, 0))],
        out_specs=[row, row],
        out_shape=[out, out],
        compiler_params=_params(("parallel",),
                                _vmem_limit(3 * _nbytes((tm, D_MODEL), F32),
                                            temps=2 * _nbytes((tm, D_MODEL), F32))),
        name="fnet_chan",
    )(h, mods, mods, w_cs)


def _fnet_pos_kernel(scale, c_ref, ns_ref, p_ref, q_ref, o_ref, acc_ref):
    k = pl.program_id(3)

    @pl.when(k == 0)
    def _():
        acc_ref[...] = jnp.zeros_like(acc_ref)

    acc_ref[...] += (_dot(c_ref[...], p_ref[...].astype(BF16))
                     + _dot(ns_ref[...], q_ref[...].astype(BF16)))

    @pl.when(k == pl.num_programs(3) - 1)
    def _():
        o_ref[...] = (acc_ref[...] * scale).astype(o_ref.dtype)


def _fnet_pos(c_tab, ns_tab, p, q, n_seq, length):
    tt, tn, tk = _seq_tiles(length)
    nb, nt, nk = D_MODEL // tn, length // tt, length // tk
    scale = (length * FNET_CG) ** -0.5
    return pl.pallas_call(
        functools.partial(_fnet_pos_kernel, scale),
        grid=(n_seq, nt, nb, nk),
        in_specs=[pl.BlockSpec((tt, tk), lambda s, t, n, k: (t, k)),
                  pl.BlockSpec((tt, tk), lambda s, t, n, k: (t, k)),
                  pl.BlockSpec((tk, tn), lambda s, t, n, k: (s * nk + k, n)),
                  pl.BlockSpec((tk, tn), lambda s, t, n, k: (s * nk + k, n))],
        out_specs=pl.BlockSpec((tt, tn), lambda s, t, n, k: (s * nt + t, n)),
        out_shape=jax.ShapeDtypeStruct((n_seq * length, D_MODEL), BF16),
        scratch_shapes=[pltpu.VMEM((tt, tn), F32)],
        compiler_params=_params(("parallel", "parallel", "parallel", "arbitrary"),
                                _vmem_limit(2 * _nbytes((tt, tk), BF16) + 2 * _nbytes((tk, tn), BF16)
                                            + _nbytes((tt, tn), BF16),
                                            resident=_nbytes((tt, tn), F32),
                                            temps=2 * _nbytes((tt, tn), F32))),
        name="fnet_pos",
    )(c_tab, ns_tab, p, q)


FN_N1, FN_N2 = 32, 128


def _fnet_pos_factored(p, q):
    assert FN_N1 * FN_N2 == DEC_SEQ and ROWS % DEC_SEQ == 0
    n1, n2 = FN_N1, FN_N2
    t = _ct_tables(n1, n2)
    fa = _kron_sub(jnp.block([[t["c1"], -t["s1"]], [-t["s1"], -t["c1"]]])).astype(BF16)
    fb_re = jnp.concatenate([t["c2"], t["s2"]], axis=1).astype(BF16)
    lat0 = ROWS_CTX // DEC_SEQ
    view = lambda x: x.reshape(ROWS // DEC_SEQ, n1, n2, D_MODEL)
    seq = lambda s: (lat0 + s,)
    br, bi = _ct_stage_a([(view(p), seq), (view(q), seq)], DEC_BATCH, fa, t["cw_a"], t["sw_a"])
    f = _ct_real_b(br, bi, fb_re, (DEC_SEQ * FNET_CG) ** -0.5)
    return f.reshape(ROWS_LAT, D_MODEL)


def _rope_table():
    rows = DEC_SEQ // GRID_W
    row = jnp.repeat(jnp.arange(rows), GRID_W).astype(F32)
    col = jnp.tile(jnp.arange(GRID_W), rows).astype(F32)
    inv = ROPE_THETA ** (-jnp.arange(0, AXIS_ROPE, 2, dtype=F32) / AXIS_ROPE)
    ang = jnp.concatenate([row[:, None] * inv, col[:, None] * inv], axis=-1)
    cos = jnp.repeat(jnp.cos(ang), 2, axis=-1)
    sin = jnp.repeat(jnp.sin(ang), 2, axis=-1)
    lat = jnp.tile(jnp.concatenate([cos, sin], axis=-1), (DEC_BATCH, 1))
    ctx = jnp.concatenate([jnp.ones((ROWS_CTX, QK_ROPE), F32), jnp.zeros((ROWS_CTX, QK_ROPE), F32)],
                          axis=-1)
    return jnp.concatenate([ctx, lat], axis=0)


def _pair_rotated(w):
    pairs = w.reshape(w.shape[:-1] + (QK_ROPE // 2, 2))
    return jnp.stack([-pairs[..., 1], pairs[..., 0]], axis=-1).reshape(w.shape)


def kernel(x_prompt, x_sample, c, cache_ckv, cache_krope, c_ctx, ada_w, ada_b, ln_g, ln_b, ffn_w_gate, ffn_w_up, ffn_w_down, mla_w_dq, mla_q_norm, mla_w_uq, mla_w_dkv, mla_kv_norm, mla_w_kr, mla_w_ukv, mla_w_o, hy_w_in, hy_b_in, hy_conv_w, hy_conv_b, hy_f_w1, hy_f_b1, hy_f_freq1, hy_f_w2, hy_f_b2, hy_f_freq2, hy_f_w3, hy_log_decay, hy_skip, hy_w_out, hy_b_out, fn_w_out, fn_b_out):
    assert x_prompt.shape == (BATCH, SEQ, D_MODEL) and x_sample.shape == (DEC_BATCH, DEC_SEQ, D_MODEL)
    assert ROWS_CTX % DEC_SEQ == 0 and SEQ == FNET_CG

    assert N_MIXERS > 0 and DEPTH > 1
    h = (x_prompt.reshape(ROWS_CTX, D_MODEL), x_sample.reshape(ROWS_LAT, D_MODEL))
    cond = jnp.concatenate([c_ctx[None, :], c, jnp.zeros((COND_PAD - N_COND, D_MODEL), F32)])
    mods_all = _modulation_vectors(cond, ada_w, ada_b)
    zero_bias = jnp.zeros((D_MODEL,), F32)
    ffn_w = (ffn_w_gate.astype(BF16), ffn_w_up.astype(BF16), ffn_w_down.astype(BF16))
    rope_tab = None
    ckv_states, krope_states = [], []

    for i in range(DEPTH):
        kind, j = i % N_MIXERS, i // N_MIXERS
        mods = mods_all[i]
        if kind == 0:
            if rope_tab is None:
                rope_tab = _rope_table()
            w_kr2 = jnp.concatenate([mla_w_kr[j], _pair_rotated(mla_w_kr[j])], axis=-1).astype(BF16)
            wq = mla_w_uq[j].reshape(Q_RANK, MLA_HEADS, QK_NOPE + QK_ROPE)
            w_q = jnp.concatenate([wq, _pair_rotated(wq[..., QK_NOPE:])], axis=-1)
            w_q = w_q.reshape(Q_RANK, MLA_HEADS * HEAD_W).astype(BF16)
            w_ukv = mla_w_ukv[j].reshape(KV_RANK, MLA_HEADS, QK_NOPE + V_DIM)
            w_k = w_ukv[..., :QK_NOPE].reshape(KV_RANK, MLA_HEADS * QK_NOPE).astype(BF16)
            w_vt = w_ukv[..., QK_NOPE:].reshape(KV_RANK, MLA_HEADS * V_DIM).T.astype(BF16)
            cq, ckv, kr, kr2 = _mla_down(h, mods, mla_w_dq[j].astype(BF16), mla_w_dkv[j].astype(BF16),
                                         w_kr2, mla_q_norm[j], mla_kv_norm[j], rope_tab)
            ckv_states.append(ckv[:ROWS_CTX].reshape(BATCH, SEQ, KV_RANK))
            krope_states.append(kr[:ROWS_CTX].reshape(BATCH, SEQ, QK_ROPE))
            q = _q_up(cq, w_q, rope_tab)
            k_tok, vt_tok = _kv_expand(ckv, kr2, w_k, w_vt)
            kc = cache_krope[:, j].reshape(DEC_BATCH * PAST_LEN, QK_ROPE).astype(BF16)
            k_cache, vt_cache = _kv_expand(cache_ckv[:, j].reshape(DEC_BATCH * PAST_LEN, KV_RANK),
                                           jnp.concatenate([kc, kc], axis=-1), w_k, w_vt)
            o = _attention(q, k_tok, vt_tok, k_cache, vt_cache)
            h = _mm_postnorm(o, mla_w_o[j].astype(BF16), zero_bias, h, mods, 2, ln_g[i, 0], ln_b[i, 0])
        elif kind == 1:
            pc = _hyena_in(h, mods, hy_w_in[j].astype(BF16), hy_b_in[j], hy_conv_w[j], hy_conv_b[j])
            fp = (hy_f_w1[j], hy_f_b1[j], hy_f_freq1[j], hy_f_w2[j], hy_f_b2[j], hy_f_freq2[j],
                  hy_f_w3[j], hy_log_decay[j])
            skip = hy_skip[j].reshape(HY_ORDER, 1, D_MODEL)
            z_ctx = _hyena_mix_dense(pc, *_hyena_filters(SEQ, *fp), skip, BATCH, SEQ)
            z_lat = _hyena_mix_pair(pc, *_hyena_filters(DEC_SEQ, *fp), skip)
            h = _mm_postnorm((z_ctx, z_lat), hy_w_out[j].astype(BF16), hy_b_out[j],
                             h, mods, 2, ln_g[i, 0], ln_b[i, 0])
        else:
            c_ch, ns_ch = _dft_tables(FNET_CG)
            p, q = _fnet_chan(h, mods, jnp.concatenate([c_ch, -ns_ch], axis=-1))
            f_ctx = _fnet_pos(c_ch, ns_ch, p, q, BATCH, SEQ)
            f_lat = _fnet_pos_factored(p, q)
            h = _mm_postnorm((f_ctx, f_lat), fn_w_out[j].astype(BF16), fn_b_out[j],
                             h, mods, 2, ln_g[i, 0], ln_b[i, 0])
        h = _ffn(h, mods, *ffn_w, i, ln_g[i, 1], ln_b[i, 1], split_out=i == DEPTH - 1)

    y_prompt = h[0].reshape(BATCH, SEQ, D_MODEL)
    y_sample = h[1].reshape(DEC_BATCH, DEC_SEQ, D_MODEL)
    return (y_prompt, y_sample, jnp.stack(ckv_states, axis=1), jnp.stack(krope_states, axis=1))
```

```python
import functools
import math

import jax
import jax.numpy as jnp
from jax import lax
from jax.experimental import pallas as pl
from jax.experimental.pallas import tpu as pltpu

F32 = jnp.float32
BF16 = jnp.bfloat16

D_MODEL = 2048
BATCH = 16
SEQ = 256
DEPTH = 4
DEC_BATCH = 2
DEC_SEQ = 4096
PAST_LEN = 512
GRID_W = 64
N_MIXERS = 3
MLA_HEADS = 16
QK_NOPE = 128
QK_ROPE = 64
V_DIM = 128
Q_RANK = 512
KV_RANK = 512
ROPE_THETA = 10000.0
AXIS_ROPE = QK_ROPE // 2
HY_ORDER = 2
HY_DIRS = 2
HY_CONV = 3
HY_BANDS = 16
HY_EMB = 1 + 2 * HY_BANDS
HY_FW = 64
HY_SHIFT = 0.05
FNET_GROUPS = 8
FNET_CG = D_MODEL // FNET_GROUPS
D_FF = -(-8 * D_MODEL // (3 * 256)) * 256
DN_ALPHA = (2 * DEPTH) ** 0.25
LN_EPS = 1e-5
RMS_EPS = 1e-6
N_MOD = 6

ROWS_CTX = BATCH * SEQ
ROWS_LAT = DEC_BATCH * DEC_SEQ
ROWS = ROWS_CTX + ROWS_LAT
N_COND = 1 + DEC_BATCH
COND_PAD = 8
HEAD_W = QK_NOPE + 2 * QK_ROPE
ATT_SCALE = (QK_NOPE + QK_ROPE) ** -0.5

V7X_VMEM_BYTES = 64 * 2 ** 20
VMEM_CAP_BYTES = V7X_VMEM_BYTES * 7 // 8
VMEM_FLOOR_BYTES = 32 * 2 ** 20


def _vmem_limit(pipelined, resident=0, temps=0):
    est = 2 * pipelined + resident + temps
    return int(min(max(est, VMEM_FLOOR_BYTES), VMEM_CAP_BYTES))


def _params(semantics, vmem):
    return pltpu.CompilerParams(dimension_semantics=semantics, vmem_limit_bytes=vmem)


def _nbytes(shape, dtype):
    return math.prod(shape) * jnp.dtype(dtype).itemsize


def _group_of_tile(i, tm):
    n_ctx = ROWS_CTX // tm
    return jnp.where(i < n_ctx, 0, 1 + (i - n_ctx) // (DEC_SEQ // tm))


def _mod_spec(which, tm):
    return pl.BlockSpec((None, 1, D_MODEL),
                        lambda i, *_: (which * COND_PAD + _group_of_tile(i, tm), 0, 0))


def _row_spec(width=D_MODEL):
    return pl.BlockSpec((1, width), lambda *_: (0, 0))


def _as_tuple(x):
    return x if isinstance(x, tuple) else (x,)


def _token_specs(xs, tm):
    width = xs[0].shape[1]
    if len(xs) == 1:
        return [pl.BlockSpec((tm, width), lambda i, *_: (i, 0))]
    n_ctx = ROWS_CTX // tm
    return [pl.BlockSpec((tm, width), lambda i, *_: (jnp.minimum(i, n_ctx - 1), 0)),
            pl.BlockSpec((tm, width), lambda i, *_: (jnp.maximum(i - n_ctx, 0), 0))]


def _token_rows(refs, r):
    if len(refs) == 1:
        return refs[0][r, :]
    n_ctx = ROWS_CTX // refs[0].shape[0]
    return jnp.where(pl.program_id(0) < n_ctx, refs[0][r, :], refs[1][r, :])


def _modulate(h, shift, scale):
    return h * (1.0 + scale) + shift


def _post_norm(h, delta, g, b):
    z = DN_ALPHA * h + delta
    mu = jnp.mean(z, axis=-1, keepdims=True)
    zc = z - mu
    var = jnp.mean(zc * zc, axis=-1, keepdims=True)
    return zc * lax.rsqrt(var + LN_EPS) * g + b


def _rms_norm(x, g):
    ms = jnp.mean(x * x, axis=-1, keepdims=True)
    return x * lax.rsqrt(ms + RMS_EPS) * g


def _dot(a, b):
    return jnp.dot(a, b, preferred_element_type=F32)


def _modvec_kernel(c_ref, w_ref, b_ref, o_ref):
    a = jax.nn.silu(c_ref[...]).astype(BF16)
    o_ref[...] = _dot(a, w_ref[...].astype(BF16)) + b_ref[...]


def _modulation_vectors(cond, ada_w, ada_b):
    tn = 1024
    n = N_MOD * D_MODEL
    out = pl.pallas_call(
        _modvec_kernel,
        grid=(DEPTH, n // tn),
        in_specs=[pl.BlockSpec((COND_PAD, D_MODEL), lambda l, j: (0, 0)),
                  pl.BlockSpec((None, D_MODEL, tn), lambda l, j: (l, 0, j)),
                  pl.BlockSpec((None, 1, tn), lambda l, j: (l, 0, j))],
        out_specs=pl.BlockSpec((None, COND_PAD, tn), lambda l, j: (l, 0, j)),
        out_shape=jax.ShapeDtypeStruct((DEPTH, COND_PAD, n), F32),
        compiler_params=_params(("parallel", "parallel"),
                                _vmem_limit(_nbytes((D_MODEL, tn), F32),
                                            temps=_nbytes((D_MODEL, tn), BF16))),
        name="modvec",
    )(cond, ada_w, ada_b.reshape(DEPTH, 1, n))
    out = out.reshape(DEPTH, COND_PAD, N_MOD, D_MODEL).transpose(0, 2, 1, 3)
    return out.reshape(DEPTH, N_MOD * COND_PAD, 1, D_MODEL)


def _hyena_in_kernel(n_ctx, h_ref, hp_ref, hn_ref, sh_ref, sc_ref, w_ref, b_ref, cw_ref, cb_ref,
                     o_ref, u_ref):
    i = pl.program_id(0)
    tm = h_ref.shape[0]

    @pl.when(pl.program_id(1) == 0)
    def _():
        mod = lambda x_ref: _modulate(x_ref[...], sh_ref[...], sc_ref[...]).astype(BF16)
        u_ref[:SUB, :] = mod(hp_ref)
        u_ref[SUB:SUB + tm, :] = mod(h_ref)
        u_ref[SUB + tm:, :] = mod(hn_ref)

    length = jnp.where(i < n_ctx, SEQ, DEC_SEQ)
    half = tm // 2
    starts = (0, half)
    ys = [_dot(u_ref[r0:r0 + half + 2 * SUB, :], w_ref[...]) + b_ref[...] for r0 in starts]
    for r0, y in zip(starts, ys):
        inner = slice(SUB, SUB + half)
        before = pltpu.roll(y, 1, 0)[inner]
        after = pltpu.roll(y, y.shape[0] - 1, 0)[inner]
        pos = (lax.broadcasted_iota(jnp.int32, before.shape, 0) + (i * tm + r0)) & (length - 1)
        before = jnp.where(pos == 0, 0.0, before)
        after = jnp.where(pos == length - 1, 0.0, after)
        o_ref[r0:r0 + half, :] = (cw_ref[0:1, :] * before + cw_ref[1:2, :] * y[inner]
                                  + cw_ref[2:3, :] * after + cb_ref[...])


def _hyena_in(h, mods, w, b, conv_w, conv_b, tm=1024, tn=1024):
    assert SEQ & (SEQ - 1) == 0 and DEC_SEQ & (DEC_SEQ - 1) == 0 and tm % SEQ == 0 and DEC_SEQ % tm == 0
    k, n = w.shape
    per, halo, last = D_MODEL // tn, tm // SUB, ROWS // SUB - 1
    return pl.pallas_call(
        functools.partial(_hyena_in_kernel, ROWS_CTX // tm),
        grid=(ROWS // tm, n // tn),
        in_specs=[pl.BlockSpec((tm, k), lambda i, j: (i, 0)),
                  pl.BlockSpec((SUB, k), lambda i, j: (jnp.maximum(i * halo - 1, 0), 0)),
                  pl.BlockSpec((SUB, k), lambda i, j: (jnp.minimum((i + 1) * halo, last), 0)),
                  _mod_spec(0, tm), _mod_spec(1, tm),
                  pl.BlockSpec((k, tn), lambda i, j: (0, j)),
                  pl.BlockSpec((1, tn), lambda i, j: (0, j)),
                  pl.BlockSpec((HY_CONV, tn), lambda i, j: (0, j)),
                  pl.BlockSpec((1, tn), lambda i, j: (0, j))],
        out_specs=pl.BlockSpec((None, tm, tn), lambda i, j: (j // per, i, j % per)),
        out_shape=jax.ShapeDtypeStruct((n // D_MODEL, ROWS, D_MODEL), F32),
        scratch_shapes=[pltpu.VMEM((tm + 2 * SUB, k), BF16)],
        compiler_params=_params(("parallel", "arbitrary"),
                                _vmem_limit(_nbytes((tm, k), F32) + _nbytes((k, tn), BF16)
                                            + _nbytes((tm, tn), F32),
                                            resident=_nbytes((tm, k), BF16),
                                            temps=5 * _nbytes((tm, tn), F32))),
        name="hyena_in",
    )(h, h, h, mods, mods, w, b.reshape(1, n), conv_w, conv_b.reshape(1, n))


def _mm_postnorm_kernel(n_a, n_h, *refs):
    a_refs, refs = refs[:n_a], refs[n_a:]
    (w_ref, bias_ref), refs = refs[:2], refs[2:]
    h_refs, (gate_ref, g_ref, b_ref, o_ref) = refs[:n_h], refs[n_h:]
    half = o_ref.shape[0] // 2
    halves = (slice(0, half), slice(half, 2 * half))
    ys = [_dot(_token_rows(a_refs, r), w_ref[...]) + bias_ref[...] for r in halves]
    for r, y in zip(halves, ys):
        o_ref[r, :] = _post_norm(_token_rows(h_refs, r), gate_ref[...] * y, g_ref[...], b_ref[...])


def _mm_postnorm(a, w, bias, h, mods, which_gate, ln_g, ln_b, tm=512):
    a, h = _as_tuple(a), _as_tuple(h)
    k = a[0].shape[1]
    return pl.pallas_call(
        functools.partial(_mm_postnorm_kernel, len(a), len(h)),
        grid=(ROWS // tm,),
        in_specs=(_token_specs(a, tm) + [pl.BlockSpec((k, D_MODEL), lambda i: (0, 0)), _row_spec()]
                  + _token_specs(h, tm) + [_mod_spec(which_gate, tm), _row_spec(), _row_spec()]),
        out_specs=pl.BlockSpec((tm, D_MODEL), lambda i: (i, 0)),
        out_shape=jax.ShapeDtypeStruct((ROWS, D_MODEL), F32),
        compiler_params=_params(("parallel",),
                                _vmem_limit(len(a) * _nbytes((tm, k), BF16)
                                            + _nbytes((k, D_MODEL), BF16)
                                            + (1 + len(h)) * _nbytes((tm, D_MODEL), F32),
                                            temps=3 * _nbytes((tm, D_MODEL), F32))),
        name="mm_postnorm",
    )(*a, w, bias.reshape(1, D_MODEL), *h, mods, ln_g.reshape(1, D_MODEL), ln_b.reshape(1, D_MODEL))


def _ffn_kernel(n_ctx, h_ref, sh_ref, sc_ref, gate_ref, g_ref, b_ref, wg_ref, wu_ref, wd_ref,
                *refs):
    i, f = pl.program_id(0), pl.program_id(1)
    if n_ctx is None:
        o_ref, u_ref = refs
        acc_ref, outs = o_ref, ((o_ref, None),)
    else:
        octx_ref, olat_ref, u_ref, acc_ref = refs
        outs = ((octx_ref, i < n_ctx), (olat_ref, i >= n_ctx))

    @pl.when(f == 0)
    def _():
        u_ref[...] = _modulate(h_ref[...], sh_ref[...], sc_ref[...]).astype(BF16)
        acc_ref[...] = jnp.zeros_like(acc_ref)

    half = u_ref.shape[0] // 2
    halves = (slice(0, half), slice(half, 2 * half))

    def step(o_ref):
        proj = [(_dot(u_ref[r, :], wg_ref[...]), _dot(u_ref[r, :], wu_ref[...])) for r in halves]
        wd = wd_ref[...].astype(BF16)
        for r, (gate, up) in zip(halves, proj):
            act = (jax.nn.silu(gate) * up).astype(BF16)
            total = acc_ref[r, :] + _dot(act, wd)
            if o_ref is None:
                acc_ref[r, :] = total
            else:
                o_ref[r, :] = _post_norm(h_ref[r, :], gate_ref[...] * total, g_ref[...], b_ref[...])

    last = f == pl.num_programs(1) - 1
    pl.when(jnp.logical_not(last))(lambda: step(None))
    for o_ref, mine in outs:
        pl.when(last if mine is None else jnp.logical_and(last, mine))(functools.partial(step, o_ref))


def _ffn(h, mods, w_gate, w_up, w_down, layer, ln_g, ln_b, split_out, tm=512, tf=512):
    tile = _nbytes((tm, D_MODEL), F32)
    if split_out:
        n_ctx = ROWS_CTX // tm
        out_specs = [pl.BlockSpec((tm, D_MODEL), lambda i, f: (jnp.minimum(i, n_ctx - 1), 0)),
                     pl.BlockSpec((tm, D_MODEL), lambda i, f: (jnp.maximum(i - n_ctx, 0), 0))]
        out_shape = [jax.ShapeDtypeStruct((ROWS_CTX, D_MODEL), F32),
                     jax.ShapeDtypeStruct((ROWS_LAT, D_MODEL), F32)]
        scratch = [pltpu.VMEM((tm, D_MODEL), BF16), pltpu.VMEM((tm, D_MODEL), F32)]
        pipelined, resident = 3 * tile, tile + tile // 2
    else:
        n_ctx = None
        out_specs = pl.BlockSpec((tm, D_MODEL), lambda i, f: (i, 0))
        out_shape = jax.ShapeDtypeStruct((ROWS, D_MODEL), F32)
        scratch = [pltpu.VMEM((tm, D_MODEL), BF16)]
        pipelined, resident = 2 * tile, tile // 2
    return pl.pallas_call(
        functools.partial(_ffn_kernel, n_ctx),
        grid=(ROWS // tm, D_FF // tf),
        in_specs=[pl.BlockSpec((tm, D_MODEL), lambda i, f: (i, 0)),
                  _mod_spec(3, tm), _mod_spec(4, tm), _mod_spec(5, tm),
                  _row_spec(), _row_spec(),
                  pl.BlockSpec((None, D_MODEL, tf), lambda i, f: (layer, 0, f)),
                  pl.BlockSpec((None, D_MODEL, tf), lambda i, f: (layer, 0, f)),
                  pl.BlockSpec((None, tf, D_MODEL), lambda i, f: (layer, f, 0))],
        out_specs=out_specs,
        out_shape=out_shape,
        scratch_shapes=scratch,
        compiler_params=_params(("arbitrary" if split_out else "parallel", "arbitrary"),
                                _vmem_limit(pipelined + 2 * _nbytes((D_MODEL, tf), BF16)
                                            + _nbytes((tf, D_MODEL), w_down.dtype),
                                            resident=resident, temps=2 * tile)),
        name="ffn",
    )(h, mods, mods, mods, ln_g.reshape(1, D_MODEL), ln_b.reshape(1, D_MODEL),
      w_gate, w_up, w_down)


def _mla_down_kernel(n_h, *refs):
    h_refs, (sh_ref, sc_ref, wdq_ref, wdkv_ref, wkr_ref, qn_ref, kvn_ref, rope_ref,
             cq_ref, ckv_ref, kr_ref, kr2_ref) = refs[:n_h], refs[n_h:]
    half = cq_ref.shape[0] // 2
    halves = (slice(0, half), slice(half, 2 * half))
    us = [_modulate(_token_rows(h_refs, r), sh_ref[...], sc_ref[...]).astype(BF16) for r in halves]
    projs = [(_dot(u, wdq_ref[...]), _dot(u, wdkv_ref[...]), _dot(u, wkr_ref[...])) for u in us]
    for r, (q_lat, kv_lat, t) in zip(halves, projs):
        cq_ref[r, :] = _rms_norm(q_lat, qn_ref[...]).astype(BF16)
        ckv_ref[r, :] = _rms_norm(kv_lat, kvn_ref[...])
        kr_ref[r, :] = t[:, :QK_ROPE]
        v = t * rope_ref[r, :]
        kr2_ref[r, :] = (v + pltpu.roll(v, QK_ROPE, 1)).astype(BF16)


def _mla_down(h, mods, w_dq, w_dkv, w_kr2, q_norm, kv_norm, rope_tab, tm=512):
    h = _as_tuple(h)
    row = lambda width: pl.BlockSpec((tm, width), lambda i: (i, 0))
    full = lambda shape: pl.BlockSpec(shape, lambda i: (0, 0))
    return pl.pallas_call(
        functools.partial(_mla_down_kernel, len(h)),
        grid=(ROWS // tm,),
        in_specs=_token_specs(h, tm) + [
            _mod_spec(0, tm), _mod_spec(1, tm),
            full((D_MODEL, Q_RANK)), full((D_MODEL, KV_RANK)), full((D_MODEL, 2 * QK_ROPE)),
            _row_spec(Q_RANK), _row_spec(KV_RANK), row(2 * QK_ROPE)],
        out_specs=[row(Q_RANK), row(KV_RANK), row(QK_ROPE), row(2 * QK_ROPE)],
        out_shape=[jax.ShapeDtypeStruct((ROWS, Q_RANK), BF16),
                   jax.ShapeDtypeStruct((ROWS, KV_RANK), F32),
                   jax.ShapeDtypeStruct((ROWS, QK_ROPE), F32),
                   jax.ShapeDtypeStruct((ROWS, 2 * QK_ROPE), BF16)],
        compiler_params=_params(("parallel",),
                                _vmem_limit(len(h) * _nbytes((tm, D_MODEL), F32)
                                            + _nbytes((D_MODEL, Q_RANK + KV_RANK + 2 * QK_ROPE), BF16)
                                            + 3 * _nbytes((tm, KV_RANK), F32),
                                            temps=2 * _nbytes((tm, D_MODEL), F32))),
        name="mla_down",
    )(*h, mods, mods, w_dq, w_dkv, w_kr2, q_norm.reshape(1, Q_RANK), kv_norm.reshape(1, KV_RANK),
      rope_tab)


NT_DIMS = (((1,), (1,)), ((), ()))


def _q_up_kernel(cq_ref, w_ref, rope_ref, q_ref):
    cq, tab = cq_ref[...], rope_ref[...] * LOG2E_SCALE
    for h in range(MLA_HEADS):
        r = _dot(cq, w_ref[:, h * HEAD_W:(h + 1) * HEAD_W])
        q_ref[h, :, :QK_NOPE] = (r[:, :QK_NOPE] * LOG2E_SCALE).astype(BF16)
        q_ref[h, :, QK_NOPE:] = (r[:, QK_NOPE:] * tab).astype(BF16)


def _q_up(cq, w_q, rope_tab, tm=512):
    return pl.pallas_call(
        _q_up_kernel,
        grid=(ROWS // tm,),
        in_specs=[pl.BlockSpec((tm, Q_RANK), lambda i: (i, 0)),
                  pl.BlockSpec((Q_RANK, MLA_HEADS * HEAD_W), lambda i: (0, 0)),
                  pl.BlockSpec((tm, 2 * QK_ROPE), lambda i: (i, 0))],
        out_specs=pl.BlockSpec((MLA_HEADS, tm, HEAD_W), lambda i: (0, i, 0)),
        out_shape=jax.ShapeDtypeStruct((MLA_HEADS, ROWS, HEAD_W), BF16),
        compiler_params=_params(("parallel",),
                                _vmem_limit(_nbytes((Q_RANK + tm, MLA_HEADS * HEAD_W), BF16))),
        name="q_up",
    )(cq, w_q, rope_tab)


VT_PAD = 16
VT_ROWS = V_DIM + VT_PAD


def _kv_expand_kernel(ckv_ref, kr2_ref, wk_ref, wvt_ref, k_ref, vt_ref):
    c = ckv_ref[...].astype(BF16)
    vt = lax.dot_general(wvt_ref[...], c, NT_DIMS, preferred_element_type=F32).astype(BF16)
    ones = jnp.ones((VT_PAD, vt.shape[1]), BF16)
    for h in range(MLA_HEADS):
        vt_ref[h * VT_ROWS:h * VT_ROWS + V_DIM, :] = vt[h * V_DIM:(h + 1) * V_DIM]
        vt_ref[h * VT_ROWS + V_DIM:(h + 1) * VT_ROWS, :] = ones
    kr2 = kr2_ref[...]
    pair_w = 2 * QK_NOPE
    for g in range(MLA_HEADS // 2):
        r = _dot(c, wk_ref[:, g * pair_w:(g + 1) * pair_w]).astype(BF16)
        for e in range(2):
            k_ref[2 * g + e, :, :QK_NOPE] = r[:, e * QK_NOPE:(e + 1) * QK_NOPE]
            k_ref[2 * g + e, :, QK_NOPE:] = kr2


def _kv_expand(ckv, kr2, w_k, w_vt, tm=512):
    rows = ckv.shape[0]
    return pl.pallas_call(
        _kv_expand_kernel,
        grid=(rows // tm,),
        in_specs=[pl.BlockSpec((tm, KV_RANK), lambda i: (i, 0)),
                  pl.BlockSpec((tm, 2 * QK_ROPE), lambda i: (i, 0)),
                  pl.BlockSpec((KV_RANK, MLA_HEADS * QK_NOPE), lambda i: (0, 0)),
                  pl.BlockSpec((MLA_HEADS * V_DIM, KV_RANK), lambda i: (0, 0))],
        out_specs=[pl.BlockSpec((MLA_HEADS, tm, HEAD_W), lambda i: (0, i, 0)),
                   pl.BlockSpec((MLA_HEADS * VT_ROWS, tm), lambda i: (0, i))],
        out_shape=[jax.ShapeDtypeStruct((MLA_HEADS, rows, HEAD_W), BF16),
                   jax.ShapeDtypeStruct((MLA_HEADS * VT_ROWS, rows), BF16)],
        compiler_params=_params(("parallel",),
                                _vmem_limit(_nbytes((tm, MLA_HEADS * (HEAD_W + V_DIM)), BF16)
                                            + 2 * _nbytes((KV_RANK, MLA_HEADS * V_DIM), BF16),
                                            temps=_nbytes((MLA_HEADS * V_DIM, tm), F32))),
        name="kv_expand",
    )(ckv, kr2, w_k, w_vt)


ATT_CHUNK = 512
ATT_SKEW = 3
LOG2E_SCALE = ATT_SCALE * math.log2(math.e)


def _attn_scores(q, k):
    return lax.dot_general(k, q, NT_DIMS, preferred_element_type=F32)


def _attn_values(s, vt, carry):
    m = jnp.max(s, axis=0, keepdims=True)
    if carry is not None:
        m_old, acc_old = carry
        m = jnp.maximum(m_old, m)
    acc = _dot(vt, jnp.exp2(s - m).astype(BF16))
    if carry is not None:
        acc = jnp.exp2(m_old - m) * acc_old + acc
    return m, acc


def _attn_output(acc):
    return (acc[:V_DIM] / acc[V_DIM:V_DIM + 1]).T.astype(BF16)


def _attn_ctx_kernel(q_ref, k_ref, vt_ref, o_ref):
    for h in range(MLA_HEADS):
        _, acc = _attn_values(_attn_scores(q_ref[h], k_ref[h]),
                              vt_ref[h * VT_ROWS:(h + 1) * VT_ROWS, :], None)
        o_ref[:, h * V_DIM:(h + 1) * V_DIM] = _attn_output(acc)


def _attn_lat_kernel(q_ref, k_ref, vt_ref, kc_ref, vtc_ref, prev_ref, o_ref):
    del prev_ref
    q = q_ref[...]
    n_tok = DEC_SEQ // ATT_CHUNK
    rows = lambda c: slice(c * ATT_CHUNK, (c + 1) * ATT_CHUNK)
    keys = [k_ref.at[rows(c), :] for c in range(n_tok)] + [kc_ref]
    vals = [vt_ref.at[:, rows(c)] for c in range(n_tok)] + [vtc_ref]
    carry = None
    scores = [_attn_scores(q, keys[c][...]) for c in range(ATT_SKEW)]
    for c in range(n_tok + 1):
        if c + ATT_SKEW <= n_tok:
            scores.append(_attn_scores(q, keys[c + ATT_SKEW][...]))
        carry = _attn_values(scores[c], vals[c][...], carry)
    o_ref[...] = _attn_output(carry[1])


def _attention(q, k_tok, vt_tok, k_cache, vt_cache, tq=2048):
    assert DEC_SEQ % ATT_CHUNK == 0
    out_shape = jax.ShapeDtypeStruct((ROWS, MLA_HEADS * V_DIM), BF16)
    o = pl.pallas_call(
        _attn_ctx_kernel,
        grid=(BATCH,),
        in_specs=[pl.BlockSpec((MLA_HEADS, SEQ, HEAD_W), lambda s: (0, s, 0)),
                  pl.BlockSpec((MLA_HEADS, SEQ, HEAD_W), lambda s: (0, s, 0)),
                  pl.BlockSpec((MLA_HEADS * VT_ROWS, SEQ), lambda s: (0, s))],
        out_specs=pl.BlockSpec((SEQ, MLA_HEADS * V_DIM), lambda s: (s, 0)),
        out_shape=out_shape,
        compiler_params=_params(("parallel",), VMEM_FLOOR_BYTES),
        name="attn_ctx",
    )(q, k_tok, vt_tok)

    lat0 = ROWS_CTX // DEC_SEQ
    q0 = ROWS_CTX // tq
    nq = DEC_SEQ // tq
    return pl.pallas_call(
        _attn_lat_kernel,
        grid=(DEC_BATCH, MLA_HEADS, nq),
        in_specs=[pl.BlockSpec((None, tq, HEAD_W), lambda b, h, i: (h, q0 + b * nq + i, 0)),
                  pl.BlockSpec((None, DEC_SEQ, HEAD_W), lambda b, h, i: (h, lat0 + b, 0)),
                  pl.BlockSpec((VT_ROWS, DEC_SEQ), lambda b, h, i: (h, lat0 + b)),
                  pl.BlockSpec((None, PAST_LEN, HEAD_W), lambda b, h, i: (h, b, 0)),
                  pl.BlockSpec((VT_ROWS, PAST_LEN), lambda b, h, i: (h, b)),
                  pl.BlockSpec(memory_space=pl.ANY)],
        out_specs=pl.BlockSpec((tq, V_DIM), lambda b, h, i: (q0 + b * nq + i, h)),
        out_shape=out_shape,
        input_output_aliases={5: 0},
        compiler_params=_params(("parallel", "parallel", "arbitrary"),
                                _vmem_limit(_nbytes((DEC_SEQ + PAST_LEN, HEAD_W + V_DIM), BF16),
                                            temps=8 * _nbytes((ATT_CHUNK, tq), F32))),
        name="attn_lat",
    )(q, k_tok, vt_tok, k_cache, vt_cache, o)


def _mm_kernel(a_ref, b_ref, o_ref, acc_ref):
    k = pl.program_id(2)

    @pl.when(k == 0)
    def _():
        acc_ref[...] = jnp.zeros_like(acc_ref)

    acc_ref[...] += _dot(a_ref[...], b_ref[...].astype(BF16))

    @pl.when(k == pl.num_programs(2) - 1)
    def _():
        o_ref[...] = acc_ref[...].astype(o_ref.dtype)


def _mm(a, b, out_dtype=F32, tm=1024, tn=1024, tk=512):
    m, kk = a.shape
    n = b.shape[1]
    tm, tn, tk = min(tm, m), min(tn, n), min(tk, kk)
    return pl.pallas_call(
        _mm_kernel,
        grid=(m // tm, n // tn, kk // tk),
        in_specs=[pl.BlockSpec((tm, tk), lambda i, j, k: (i, k)),
                  pl.BlockSpec((tk, tn), lambda i, j, k: (k, j))],
        out_specs=pl.BlockSpec((tm, tn), lambda i, j, k: (i, j)),
        out_shape=jax.ShapeDtypeStruct((m, n), out_dtype),
        scratch_shapes=[pltpu.VMEM((tm, tn), F32)],
        compiler_params=_params(("parallel", "parallel", "arbitrary"), VMEM_FLOOR_BYTES),
        name="mm",
    )(a, b)


def _seq_tiles(length):
    if length >= 1024:
        return 1024, 512, 1024
    return length, D_MODEL, length


def _dft_fwd_kernel(c_ref, s_ref, z_ref, kr_ref, ks_ref, yr_ref, ys_ref, accr_ref, accs_ref):
    k = pl.program_id(3)

    @pl.when(k == 0)
    def _():
        accr_ref[...] = jnp.zeros_like(accr_ref)
        accs_ref[...] = jnp.zeros_like(accs_ref)

    z = z_ref[...].astype(BF16)
    accr_ref[...] += _dot(c_ref[...], z)
    accs_ref[...] += _dot(s_ref[...], z)

    @pl.when(k == pl.num_programs(3) - 1)
    def _():
        zr, zs, kr, ks = accr_ref[...], accs_ref[...], kr_ref[...], ks_ref[...]
        yr_ref[...] = (zr * kr - zs * ks).astype(BF16)
        ys_ref[...] = (zr * ks + zs * kr).astype(BF16)


def _dft_fwd(tabs, z, z_which, kr, ks, order, n_seq, length):
    tf, tn, tk = _seq_tiles(length)
    nb, nf, nk = D_MODEL // tn, length // tf, length // tk
    kc0 = order * nb
    out = jax.ShapeDtypeStruct((n_seq * length, D_MODEL), BF16)
    return pl.pallas_call(
        _dft_fwd_kernel,
        grid=(n_seq, nf, nb, nk),
        in_specs=[pl.BlockSpec((tf, tk), lambda s, f, n, k: (f, k)),
                  pl.BlockSpec((tf, tk), lambda s, f, n, k: (f, k)),
                  pl.BlockSpec((None, tk, tn), lambda s, f, n, k: (z_which, s * nk + k, n)),
                  pl.BlockSpec((tf, tn), lambda s, f, n, k: (f, kc0 + n)),
                  pl.BlockSpec((tf, tn), lambda s, f, n, k: (f, kc0 + n))],
        out_specs=[pl.BlockSpec((tf, tn), lambda s, f, n, k: (s * nf + f, n)),
                   pl.BlockSpec((tf, tn), lambda s, f, n, k: (s * nf + f, n))],
        out_shape=[out, out],
        scratch_shapes=[pltpu.VMEM((tf, tn), F32), pltpu.VMEM((tf, tn), F32)],
        compiler_params=_params(("parallel", "parallel", "parallel", "arbitrary"),
                                _vmem_limit(2 * _nbytes((tf, tk), BF16) + _nbytes((tk, tn), F32)
                                            + 2 * _nbytes((tf, tn), F32) + 2 * _nbytes((tf, tn), BF16),
                                            resident=2 * _nbytes((tf, tn), F32),
                                            temps=4 * _nbytes((tf, tn), F32))),
        name="dft_fwd",
    )(tabs["c"], tabs["s"], z, kr, ks)


def _dft_inv_kernel(inv_len, ct_ref, st_ref, yr_ref, ys_ref, z_ref, gate_ref, skip_ref,
                    o_ref, acc_ref):
    k = pl.program_id(3)

    @pl.when(k == 0)
    def _():
        acc_ref[...] = jnp.zeros_like(acc_ref)

    acc_ref[...] += _dot(ct_ref[...], yr_ref[...]) + _dot(st_ref[...], ys_ref[...])

    @pl.when(k == pl.num_programs(3) - 1)
    def _():
        y = acc_ref[...] * inv_len + skip_ref[...] * z_ref[...]
        o_ref[...] = (gate_ref[...] * y).astype(o_ref.dtype)


def _dft_inv(tabs, yr, ys, z, z_which, gate, gate_which, skip, order, n_seq, length, out_dtype):
    tt, tn, tk = _seq_tiles(length)
    nb, nt, nk = D_MODEL // tn, length // tt, length // tk
    return pl.pallas_call(
        functools.partial(_dft_inv_kernel, 1.0 / length),
        grid=(n_seq, nt, nb, nk),
        in_specs=[pl.BlockSpec((tt, tk), lambda s, t, n, k: (t, k)),
                  pl.BlockSpec((tt, tk), lambda s, t, n, k: (t, k)),
                  pl.BlockSpec((tk, tn), lambda s, t, n, k: (s * nk + k, n)),
                  pl.BlockSpec((tk, tn), lambda s, t, n, k: (s * nk + k, n)),
                  pl.BlockSpec((None, tt, tn), lambda s, t, n, k: (z_which, s * nt + t, n)),
                  pl.BlockSpec((None, tt, tn), lambda s, t, n, k: (gate_which, s * nt + t, n)),
                  pl.BlockSpec((None, 1, tn), lambda s, t, n, k: (order, 0, n))],
        out_specs=pl.BlockSpec((tt, tn), lambda s, t, n, k: (s * nt + t, n)),
        out_shape=jax.ShapeDtypeStruct((n_seq * length, D_MODEL), out_dtype),
        scratch_shapes=[pltpu.VMEM((tt, tn), F32)],
        compiler_params=_params(("parallel", "parallel", "parallel", "arbitrary"),
                                _vmem_limit(2 * _nbytes((tt, tk), BF16) + 2 * _nbytes((tk, tn), BF16)
                                            + 3 * _nbytes((tt, tn), F32),
                                            resident=_nbytes((tt, tn), F32),
                                            temps=3 * _nbytes((tt, tn), F32))),
        name="dft_inv",
    )(tabs["ct"], tabs["st"], yr, ys, z, gate, skip)


TW_LANES = 128
SUB = 8
CT_ROWS = 2 * SUB
CT_COLS = 1024


def _lane_tile(x, width):
    return jnp.tile(x, (1, width // x.shape[-1]))


def _kron_sub(f):
    return jnp.kron(f, jnp.eye(SUB, dtype=f.dtype))


def _sub_rows(x, h):
    part = x[:, h * SUB:(h + 1) * SUB, :]
    return part.reshape(part.shape[0] * SUB, part.shape[2])


def _from_sub_rows(parts):
    split = [p.reshape(p.shape[0] // SUB, SUB, p.shape[1]) for p in parts]
    return jnp.concatenate(split, axis=1)


def _ct_stage_a_kernel(n_in, *refs):
    x_refs, (fa_ref, cw_ref, sw_ref, br_ref, bi_ref) = refs[:n_in], refs[n_in:]
    fa = fa_ref[...]
    half = fa.shape[0] // 2
    width = br_ref.shape[-1]
    xs = [r[...].astype(F32) for r in x_refs]
    b_re, b_im = [], []
    for h in range(CT_ROWS // SUB):
        x = jnp.concatenate([_sub_rows(x, h) for x in xs], axis=0).astype(BF16)
        a = _dot(fa, x)
        ar, ai = a[:half], a[half:]
        cw, sw = _lane_tile(cw_ref[h], width), _lane_tile(sw_ref[h], width)
        b_re.append(ar * cw + ai * sw)
        b_im.append(ai * cw - ar * sw)
    br_ref[...] = _from_sub_rows(b_re).astype(BF16)
    bi_ref[...] = _from_sub_rows(b_im).astype(BF16)


def _ct_stage_a(xs, n_seq, fa, cw, sw):
    n1 = fa.shape[0] // (2 * SUB)
    n2 = cw.shape[0] * CT_ROWS
    width = xs[0][0].shape[-1]
    tn2, tw = CT_ROWS, CT_COLS
    in_specs, blocks = [], 0
    for arr, prefix in xs:
        rows_in = arr.shape[-3]
        lead = (None,) * (arr.ndim - 3)
        in_specs.append(pl.BlockSpec(lead + (rows_in, tn2, tw),
                                     lambda s, i, c, prefix=prefix: prefix(s) + (0, i, c)))
        blocks += _nbytes((rows_in, tn2, tw), arr.dtype)
    twid = pl.BlockSpec((None,) + cw.shape[1:], lambda s, i, c: (i, 0, 0, 0))
    in_specs += [pl.BlockSpec(fa.shape, lambda s, i, c: (0, 0)), twid, twid]
    out = jax.ShapeDtypeStruct((n_seq, n1, n2, width), BF16)
    out_spec = pl.BlockSpec((None, n1, tn2, tw), lambda s, i, c: (s, 0, i, c))
    return pl.pallas_call(
        functools.partial(_ct_stage_a_kernel, len(xs)),
        grid=(n_seq, n2 // tn2, width // tw),
        in_specs=in_specs,
        out_specs=[out_spec, out_spec],
        out_shape=[out, out],
        compiler_params=_params(("parallel", "parallel", "parallel"),
                                _vmem_limit(blocks + 2 * _nbytes((n1, tn2, tw), BF16)
                                            + _nbytes(fa.shape, BF16),
                                            temps=8 * _nbytes((n1, tn2, tw), F32))),
        name="ct_stage_a",
    )(*[arr for arr, _ in xs], fa, cw, sw)


def _ct_mid_kernel(tk1, br_ref, bi_ref, ur_ref, ui_ref, *refs):
    partners, refs = refs[:2 * tk1], refs[2 * tk1:]
    fb_ref, fbi_ref, fbp0_ref, fbp1_ref, cw_ref, sw_ref, vr_ref, vi_ref = refs
    fb, fbi = fb_ref[...], fbi_ref[...]
    half = fb.shape[0] // 2
    width = br_ref.shape[-1]
    first_tile = pl.program_id(0) == 0
    for j in range(tk1):
        stack = lambda re_ref, im_ref, p=j: jnp.concatenate([re_ref[p], im_ref[p]], axis=0)
        fbp = fbp1_ref[...] if j else jnp.where(first_tile, fbp0_ref[...], fbp1_ref[...])
        u_own = _dot(fb[:half], stack(ur_ref, ui_ref))
        u_neg = _dot(fbp, stack(partners[2 * j], partners[2 * j + 1], 0))
        kr = 0.5 * (u_own + u_neg)
        ki = 0.5 * (u_neg - u_own)
        x = _dot(fb, stack(br_ref, bi_ref))
        xr, xi = x[:half], x[half:]
        y = jnp.concatenate([xr * kr - xi * ki, xr * ki + xi * kr], axis=0).astype(BF16)
        v = _dot(fbi, y)
        vr, vi = v[:half], v[half:]
        cw, sw = _lane_tile(cw_ref[j], width), _lane_tile(sw_ref[j], width)
        vr_ref[j] = (vr * cw - vi * sw).astype(BF16)
        vi_ref[j] = (vi * cw + vr * sw).astype(BF16)


def _ct_mid(br, bi, ur, ui, order, fb, fbi, fbp0, fbp1, cw, sw, tk1=4, td=1024):
    n1, n2, d = br.shape
    nd = d // td
    data = pl.BlockSpec((tk1, n2, td), lambda i, j: (i, 0, j))
    coef = pl.BlockSpec((tk1, n2, td), lambda i, j: (i, 0, order * nd + j))
    page = lambda p: pl.BlockSpec((1, n2, td),
                                  lambda i, j: ((n1 - (i * tk1 + p)) % n1, 0, order * nd + j))
    partners = [page(p) for p in range(tk1) for _ in range(2)]
    mat = pl.BlockSpec(fb.shape, lambda i, j: (0, 0))
    mat_p = pl.BlockSpec(fbp0.shape, lambda i, j: (0, 0))
    tw = pl.BlockSpec((tk1, n2, TW_LANES), lambda i, j: (i, 0, 0))
    out = jax.ShapeDtypeStruct((n1, n2, d), BF16)
    return pl.pallas_call(
        functools.partial(_ct_mid_kernel, tk1),
        grid=(n1 // tk1, nd),
        in_specs=[data, data, coef, coef] + partners + [mat, mat, mat_p, mat_p, tw, tw],
        out_specs=[data, data],
        out_shape=[out, out],
        compiler_params=_params(("parallel", "parallel"),
                                _vmem_limit(8 * _nbytes((tk1, n2, td), BF16),
                                            temps=10 * _nbytes((2 * n2, td), F32))),
        name="ct_mid",
    )(br, bi, ur, ui, *([ur, ui] * tk1), fb, fbi, fbp0, fbp1, cw, sw)


def _ct_inv_a_kernel(scale, vr_ref, vi_ref, fai_ref, z0_ref, z1_ref, g0_ref, g1_ref, skip_ref,
                     o_ref):
    fai = fai_ref[...]
    half = fai.shape[0] // 2
    skip = skip_ref[...]
    vr, vi = vr_ref[...].astype(F32), vi_ref[...].astype(F32)
    zs, gs = (z0_ref[...], z1_ref[...]), (g0_ref[...], g1_ref[...])
    outs = ([], [])
    for h in range(CT_ROWS // SUB):
        v = jnp.concatenate([_sub_rows(vr, h), _sub_rows(vi, h)], axis=0).astype(BF16)
        y = _dot(fai, v) * scale
        for b, yb in enumerate((y[:half], y[half:])):
            outs[b].append(_sub_rows(gs[b], h) * (yb + skip * _sub_rows(zs[b], h)))
    for b in range(2):
        o_ref[b] = _from_sub_rows(outs[b]).astype(o_ref.dtype)


def _ct_inv_a(vr, vi, fai, z, z_which, gate, gate_which, skip, order, out_dtype):
    n1, n2, d = vr.shape
    rows = fai.shape[0] // (2 * SUB)
    tn2, tw = CT_ROWS, CT_COLS // 2
    spec = pl.BlockSpec((n1, tn2, tw), lambda i, c: (0, i, c))
    pair = lambda which, b: pl.BlockSpec((None, None, rows, tn2, tw),
                                         lambda i, c: (which[0], which[1] + b, 0, i, c))
    return pl.pallas_call(
        functools.partial(_ct_inv_a_kernel, 1.0 / (n1 * n2)),
        grid=(n2 // tn2, d // tw),
        in_specs=[spec, spec, pl.BlockSpec(fai.shape, lambda i, c: (0, 0)),
                  pair(z_which, 0), pair(z_which, 1), pair(gate_which, 0), pair(gate_which, 1),
                  pl.BlockSpec((None, 1, tw), lambda i, c: (order, 0, c))],
        out_specs=pl.BlockSpec((2, rows, tn2, tw), lambda i, c: (0, 0, i, c)),
        out_shape=jax.ShapeDtypeStruct((2, rows, n2, d), out_dtype),
        compiler_params=_params(("parallel", "parallel"),
                                _vmem_limit(2 * _nbytes((n1, tn2, tw), BF16)
                                            + 6 * _nbytes((rows, tn2, tw), F32)
                                            + _nbytes(fai.shape, BF16),
                                            temps=8 * _nbytes((n1, tn2, tw), F32))),
        name="ct_inv_a",
    )(vr, vi, fai, z, z, gate, gate, skip)


def _ct_real_b_kernel(scale, br_ref, bi_ref, fb_ref, o_ref, so_ref):
    fb = fb_ref[...]
    for j in range(br_ref.shape[0]):
        so_ref[:, j, :] = _dot(fb, jnp.concatenate([br_ref[j], bi_ref[j]], axis=0)) * scale
    o_ref[...] = so_ref[...].astype(o_ref.dtype)


def _ct_real_b(br, bi, fb_re, scale):
    n_seq, n1, n2, d = br.shape
    tk1, tw = CT_ROWS, CT_COLS
    blk = pl.BlockSpec((None, tk1, n2, tw), lambda s, i, c: (s, i, 0, c))
    return pl.pallas_call(
        functools.partial(_ct_real_b_kernel, scale),
        grid=(n_seq, n1 // tk1, d // tw),
        in_specs=[blk, blk, pl.BlockSpec(fb_re.shape, lambda s, i, c: (0, 0))],
        out_specs=pl.BlockSpec((None, n2, tk1, tw), lambda s, i, c: (s, 0, i, c)),
        out_shape=jax.ShapeDtypeStruct((n_seq, n2, n1, d), BF16),
        scratch_shapes=[pltpu.VMEM((n2, tk1, tw), F32)],
        compiler_params=_params(("parallel", "parallel", "parallel"),
                                _vmem_limit(3 * _nbytes((tk1, n2, tw), BF16),
                                            resident=_nbytes((n2, tk1, tw), F32),
                                            temps=2 * _nbytes((n2, tk1, tw), F32))),
        name="ct_real_b",
    )(br, bi, fb_re)


def _cos_sin(num, den):
    ang = (num % den).astype(F32) * (2.0 * math.pi / den)
    return jnp.cos(ang), jnp.sin(ang)


def _ct_tables(n1, n2):
    i1 = jnp.arange(n1, dtype=jnp.int32)
    i2 = jnp.arange(n2, dtype=jnp.int32)
    c1, s1 = _cos_sin(i1[:, None] * i1[None, :], n1)
    c2, s2 = _cos_sin(i2[:, None] * i2[None, :], n2)
    cw, sw = _cos_sin(i2[:, None] * i1[None, :], n1 * n2)
    lanes = lambda t: jnp.broadcast_to(t[..., None], t.shape + (TW_LANES,))

    def stage_a_rows(t):
        t = t.reshape(n2 // CT_ROWS, CT_ROWS // SUB, SUB, n1)
        return lanes(jnp.swapaxes(t, 2, 3).reshape(n2 // CT_ROWS, CT_ROWS // SUB, n1 * SUB))

    return {"c1": c1, "s1": s1, "c2": c2, "s2": s2,
            "cw_a": stage_a_rows(cw), "sw_a": stage_a_rows(sw),
            "cw_b": lanes(cw.T), "sw_b": lanes(sw.T)}


def _filter_rows(feat_ref, t_ref, w1_ref, b1_ref, fr1_ref, w2_ref, b2_ref, fr2_ref, w3_ref,
                 decay_ref):
    x = jnp.sin(fr1_ref[...] * (_dot(feat_ref[...].astype(BF16), w1_ref[...].astype(BF16))
                                + b1_ref[...]))
    x = jnp.sin(fr2_ref[...] * (_dot(x.astype(BF16), w2_ref[...].astype(BF16)) + b2_ref[...]))
    h = _dot(x.astype(BF16), w3_ref[...].astype(BF16))
    return h * (jnp.exp(-t_ref[...] * jnp.exp(decay_ref[...])) + HY_SHIFT)


def _filter_stats_kernel(*refs):
    ss_ref = refs[-1]
    h = _filter_rows(*refs[:-1])

    @pl.when(pl.program_id(0) == 0)
    def _():
        ss_ref[...] = jnp.zeros_like(ss_ref)

    ss_ref[...] += jnp.sum(h * h, axis=0, keepdims=True)


def _filter_emit_kernel(*refs):
    ss_ref, a_ref, b_ref = refs[-3:]
    h = _filter_rows(*refs[:-3])
    n_dir = a_ref.shape[1]
    ss = ss_ref[...]
    norm = lax.rsqrt(ss[:, :n_dir] + ss[:, n_dir:] + 1e-12)
    fwd = h[:, :n_dir] * norm
    bwd = h[:, n_dir:] * norm
    row = lax.broadcasted_iota(jnp.int32, bwd.shape, 0) + pl.program_id(0) * bwd.shape[0]
    bwd = jnp.where(row == 0, 0.0, bwd)
    a_ref[...] = (fwd + bwd).astype(BF16)
    b_ref[...] = (fwd - bwd).astype(BF16)


def _hyena_filters(length, f_w1, f_b1, f_freq1, f_w2, f_b2, f_freq2, f_w3, log_decay):
    t = jnp.linspace(0.0, 1.0, length, dtype=F32)[:, None]
    t_idx = jnp.arange(length, dtype=F32)[:, None]
    bands = jnp.linspace(1e-4, HY_BANDS - 1, HY_BANDS, dtype=F32)
    w = 2.0 * math.pi * t_idx * bands / length
    feat = jnp.concatenate([t, jnp.cos(w), -jnp.sin(w)], axis=-1)
    emb_pad = 128
    feat = jnp.pad(feat, ((0, 0), (0, emb_pad - HY_EMB)))
    w1 = jnp.pad(f_w1, ((0, emb_pad - HY_EMB), (0, 0)))
    n_all = HY_DIRS * HY_ORDER * D_MODEL
    n_dir = HY_ORDER * D_MODEL
    tm = 256
    full = lambda shape: pl.BlockSpec(shape, lambda i: (0, 0))
    mlp_specs = [pl.BlockSpec((tm, emb_pad), lambda i: (i, 0)),
                 pl.BlockSpec((tm, 1), lambda i: (i, 0)),
                 full((emb_pad, HY_FW)), full((1, HY_FW)), full((1, HY_FW)),
                 full((HY_FW, HY_FW)), full((1, HY_FW)), full((1, HY_FW)),
                 full((HY_FW, n_all)), full((1, n_all))]
    mlp_args = (feat, t, w1, f_b1.reshape(1, HY_FW), f_freq1.reshape(1, HY_FW), f_w2,
                f_b2.reshape(1, HY_FW), f_freq2.reshape(1, HY_FW), f_w3, log_decay.reshape(1, n_all))
    vmem = _vmem_limit(_nbytes((tm, n_all), F32) + _nbytes((HY_FW, n_all), F32),
                       temps=4 * _nbytes((tm, n_all), F32))
    ss = pl.pallas_call(
        _filter_stats_kernel,
        grid=(length // tm,),
        in_specs=mlp_specs,
        out_specs=full((1, n_all)),
        out_shape=jax.ShapeDtypeStruct((1, n_all), F32),
        compiler_params=_params(("arbitrary",), vmem),
        name="filter_stats",
    )(*mlp_args)
    comb = jax.ShapeDtypeStruct((length, n_dir), BF16)
    return pl.pallas_call(
        _filter_emit_kernel,
        grid=(length // tm,),
        in_specs=mlp_specs + [full((1, n_all))],
        out_specs=[pl.BlockSpec((tm, n_dir), lambda i: (i, 0)), pl.BlockSpec((tm, n_dir), lambda i: (i, 0))],
        out_shape=[comb, comb],
        compiler_params=_params(("parallel",), vmem),
        name="filter_emit",
    )(*mlp_args, ss)


def _cis_product(row_hi, row_lo, period):
    def cis(phase):
        ang = (phase % period).astype(F32) * (2.0 * math.pi / period)
        return jnp.cos(ang)[:, :, None], jnp.sin(ang)[:, :, None]
    (c1, s1), (c0, s0) = cis(row_hi), cis(row_lo)
    c0, s0 = jnp.swapaxes(c0, 1, 2), jnp.swapaxes(s0, 1, 2)
    rows = row_hi.shape[0]
    return ((c1 * c0 - s1 * s0).reshape(rows, -1), (s1 * c0 + c1 * s0).reshape(rows, -1))


def _odd_dft_tables(length):
    split = 1 << (length.bit_length() // 2)
    r = jnp.arange(length, dtype=jnp.int32)[:, None]
    hi = jnp.arange(length // split, dtype=jnp.int32)[None, :] * split
    lo = jnp.arange(split, dtype=jnp.int32)[None, :]
    c, s = _cis_product((2 * r + 1) * hi, (2 * r + 1) * lo, 4 * length)
    ct, st = _cis_product(r * (2 * hi), r * (2 * lo + 1), 4 * length)
    return {"c": c.astype(BF16), "s": s.astype(BF16), "ct": ct.astype(BF16), "st": st.astype(BF16)}


def _dft_tables(length):
    split = 1 << (length.bit_length() // 2)
    r = jnp.arange(length, dtype=jnp.int32)[:, None]
    hi = jnp.arange(length // split, dtype=jnp.int32)[None, :] * split
    lo = jnp.arange(split, dtype=jnp.int32)[None, :]
    c, s = _cis_product(r * hi, r * lo, length)
    return c.astype(BF16), (-s).astype(BF16)


def _hyena_mix_dense(pc, filt_p, filt_m, skip, n_seq, length):
    tabs = _odd_dft_tables(length)
    kr, ks = _mm(tabs["c"], filt_p), _mm(tabs["s"], filt_m)
    yr, ys = _dft_fwd(tabs, pc, 0, kr, ks, 0, n_seq, length)
    z1 = _dft_inv(tabs, yr, ys, pc, 0, pc, 1, skip, 0, n_seq, length, F32)[None]
    yr, ys = _dft_fwd(tabs, z1, 0, kr, ks, 1, n_seq, length)
    return _dft_inv(tabs, yr, ys, z1, 0, pc, 2, skip, 1, n_seq, length, BF16)


HY_N1, HY_N2 = 64, 128


def _hyena_mix_pair(pc, filt_p, filt_m, skip):
    assert DEC_BATCH == 2 and HY_N1 * HY_N2 == 2 * DEC_SEQ and ROWS % DEC_SEQ == 0
    n1, n2, rows_in = HY_N1, HY_N2, HY_N1 // 2
    t = _ct_tables(n1, n2)
    c_in, s_in = t["c1"][:, :rows_in], t["s1"][:, :rows_in]
    fa = _kron_sub(jnp.block([[c_in, s_in], [-s_in, c_in]])).astype(BF16)
    fb = jnp.block([[t["c2"], t["s2"]], [-t["s2"], t["c2"]]]).astype(BF16)
    fbi = jnp.block([[t["c2"], -t["s2"]], [t["s2"], t["c2"]]]).astype(BF16)
    mirror = lambda shift: (-jnp.arange(n2) - shift) % n2
    fbp0, fbp1 = [jnp.concatenate([t["c2"][mirror(sh)], t["s2"][mirror(sh)]], axis=1).astype(BF16)
                  for sh in (0, 1)]
    c_out, s_out = t["c1"][:rows_in], t["s1"][:rows_in]
    fai = _kron_sub(jnp.block([[c_out, -s_out], [s_out, c_out]])).astype(BF16)
    n_filt = HY_ORDER * D_MODEL
    filt_view = lambda f: (f.reshape(rows_in, n2, n_filt), lambda s: ())
    ur, ui = _ct_stage_a([filt_view(filt_p), filt_view(filt_m)], 1, fa, t["cw_a"], t["sw_a"])

    pc = pc.reshape(3, ROWS // DEC_SEQ, rows_in, n2, D_MODEL)
    z, z0, out_dtypes = pc, ROWS_CTX // DEC_SEQ, (F32, BF16)
    for order in range(HY_ORDER):
        br, bi = _ct_stage_a([(z, lambda s, z0=z0: (0, z0)), (z, lambda s, z0=z0: (0, z0 + 1))], 1, fa,
                             t["cw_a"], t["sw_a"])
        vr, vi = _ct_mid(br[0], bi[0], ur[0], ui[0], order, fb, fbi, fbp0, fbp1, t["cw_b"], t["sw_b"])
        z = _ct_inv_a(vr, vi, fai, z, (0, z0), pc, (1 + order, ROWS_CTX // DEC_SEQ), skip, order,
                      out_dtypes[order])[None]
        z0 = 0
    return z.reshape(ROWS_LAT, D_MODEL)


def _fnet_chan_kernel(h_ref, sh_ref, sc_ref, w_ref, p_ref, q_ref):
    u = _modulate(h_ref[...], sh_ref[...], sc_ref[...]).astype(BF16)
    w = w_ref[...]
    for g in range(FNET_GROUPS):
        cols = slice(g * FNET_CG, (g + 1) * FNET_CG)
        r = _dot(u[:, cols], w)
        p_ref[:, cols] = r[:, :FNET_CG].astype(BF16)
        q_ref[:, cols] = r[:, FNET_CG:].astype(BF16)


def _fnet_chan(h, mods, w_cs, tm=512):
    row = pl.BlockSpec((tm, D_MODEL), lambda i: (i, 0))
    out = jax.ShapeDtypeStruct((ROWS, D_MODEL), BF16)
    return pl.pallas_call(
        _fnet_chan_kernel,
        grid=(ROWS // tm,),
        in_specs=[row, _mod_spec(0, tm), _mod_spec(1, tm),
                  pl.BlockSpec((FNET_CG, 2 * FNET_CG), lambda i: (0, 0))],
        out_specs=[row, row],
        out_shape=[out, out],
        compiler_params=_params(("parallel",),
                                _vmem_limit(3 * _nbytes((tm, D_MODEL), F32),
                                            temps=2 * _nbytes((tm, D_MODEL), F32))),
        name="fnet_chan",
    )(h, mods, mods, w_cs)


def _fnet_pos_kernel(scale, c_ref, ns_ref, p_ref, q_ref, o_ref, acc_ref):
    k = pl.program_id(3)

    @pl.when(k == 0)
    def _():
        acc_ref[...] = jnp.zeros_like(acc_ref)

    acc_ref[...] += (_dot(c_ref[...], p_ref[...].astype(BF16))
                     + _dot(ns_ref[...], q_ref[...].astype(BF16)))

    @pl.when(k == pl.num_programs(3) - 1)
    def _():
        o_ref[...] = (acc_ref[...] * scale).astype(o_ref.dtype)


def _fnet_pos(c_tab, ns_tab, p, q, n_seq, length):
    tt, tn, tk = _seq_tiles(length)
    nb, nt, nk = D_MODEL // tn, length // tt, length // tk
    scale = (length * FNET_CG) ** -0.5
    return pl.pallas_call(
        functools.partial(_fnet_pos_kernel, scale),
        grid=(n_seq, nt, nb, nk),
        in_specs=[pl.BlockSpec((tt, tk), lambda s, t, n, k: (t, k)),
                  pl.BlockSpec((tt, tk), lambda s, t, n, k: (t, k)),
                  pl.BlockSpec((tk, tn), lambda s, t, n, k: (s * nk + k, n)),
                  pl.BlockSpec((tk, tn), lambda s, t, n, k: (s * nk + k, n))],
        out_specs=pl.BlockSpec((tt, tn), lambda s, t, n, k: (s * nt + t, n)),
        out_shape=jax.ShapeDtypeStruct((n_seq * length, D_MODEL), BF16),
        scratch_shapes=[pltpu.VMEM((tt, tn), F32)],
        compiler_params=_params(("parallel", "parallel", "parallel", "arbitrary"),
                                _vmem_limit(2 * _nbytes((tt, tk), BF16) + 2 * _nbytes((tk, tn), BF16)
                                            + _nbytes((tt, tn), BF16),
                                            resident=_nbytes((tt, tn), F32),
                                            temps=2 * _nbytes((tt, tn), F32))),
        name="fnet_pos",
    )(c_tab, ns_tab, p, q)


FN_N1, FN_N2 = 32, 128


def _fnet_pos_factored(p, q):
    assert FN_N1 * FN_N2 == DEC_SEQ and ROWS % DEC_SEQ == 0
    n1, n2 = FN_N1, FN_N2
    t = _ct_tables(n1, n2)
    fa = _kron_sub(jnp.block([[t["c1"], -t["s1"]], [-t["s1"], -t["c1"]]])).astype(BF16)
    fb_re = jnp.concatenate([t["c2"], t["s2"]], axis=1).astype(BF16)
    lat0 = ROWS_CTX // DEC_SEQ
    view = lambda x: x.reshape(ROWS // DEC_SEQ, n1, n2, D_MODEL)
    seq = lambda s: (lat0 + s,)
    br, bi = _ct_stage_a([(view(p), seq), (view(q), seq)], DEC_BATCH, fa, t["cw_a"], t["sw_a"])
    f = _ct_real_b(br, bi, fb_re, (DEC_SEQ * FNET_CG) ** -0.5)
    return f.reshape(ROWS_LAT, D_MODEL)


def _rope_table():
    rows = DEC_SEQ // GRID_W
    row = jnp.repeat(jnp.arange(rows), GRID_W).astype(F32)
    col = jnp.tile(jnp.arange(GRID_W), rows).astype(F32)
    inv = ROPE_THETA ** (-jnp.arange(0, AXIS_ROPE, 2, dtype=F32) / AXIS_ROPE)
    ang = jnp.concatenate([row[:, None] * inv, col[:, None] * inv], axis=-1)
    cos = jnp.repeat(jnp.cos(ang), 2, axis=-1)
    sin = jnp.repeat(jnp.sin(ang), 2, axis=-1)
    lat = jnp.tile(jnp.concatenate([cos, sin], axis=-1), (DEC_BATCH, 1))
    ctx = jnp.concatenate([jnp.ones((ROWS_CTX, QK_ROPE), F32), jnp.zeros((ROWS_CTX, QK_ROPE), F32)],
                          axis=-1)
    return jnp.concatenate([ctx, lat], axis=0)


def _pair_rotated(w):
    pairs = w.reshape(w.shape[:-1] + (QK_ROPE // 2, 2))
    return jnp.stack([-pairs[..., 1], pairs[..., 0]], axis=-1).reshape(w.shape)


def kernel(x_prompt, x_sample, c, cache_ckv, cache_krope, c_ctx, ada_w, ada_b, ln_g, ln_b, ffn_w_gate, ffn_w_up, ffn_w_down, mla_w_dq, mla_q_norm, mla_w_uq, mla_w_dkv, mla_kv_norm, mla_w_kr, mla_w_ukv, mla_w_o, hy_w_in, hy_b_in, hy_conv_w, hy_conv_b, hy_f_w1, hy_f_b1, hy_f_freq1, hy_f_w2, hy_f_b2, hy_f_freq2, hy_f_w3, hy_log_decay, hy_skip, hy_w_out, hy_b_out, fn_w_out, fn_b_out):
    assert x_prompt.shape == (BATCH, SEQ, D_MODEL) and x_sample.shape == (DEC_BATCH, DEC_SEQ, D_MODEL)
    assert ROWS_CTX % DEC_SEQ == 0 and SEQ == FNET_CG

    assert N_MIXERS > 0 and DEPTH > 1
    h = (x_prompt.reshape(ROWS_CTX, D_MODEL), x_sample.reshape(ROWS_LAT, D_MODEL))
    cond = jnp.concatenate([c_ctx[None, :], c, jnp.zeros((COND_PAD - N_COND, D_MODEL), F32)])
    mods_all = _modulation_vectors(cond, ada_w, ada_b)
    zero_bias = jnp.zeros((D_MODEL,), F32)
    ffn_w = (ffn_w_gate.astype(BF16), ffn_w_up.astype(BF16), ffn_w_down)
    rope_tab = None
    ckv_states, krope_states = [], []

    for i in range(DEPTH):
        kind, j = i % N_MIXERS, i // N_MIXERS
        mods = mods_all[i]
        if kind == 0:
            if rope_tab is None:
                rope_tab = _rope_table()
            w_kr2 = jnp.concatenate([mla_w_kr[j], _pair_rotated(mla_w_kr[j])], axis=-1).astype(BF16)
            wq = mla_w_uq[j].reshape(Q_RANK, MLA_HEADS, QK_NOPE + QK_ROPE)
            w_q = jnp.concatenate([wq, _pair_rotated(wq[..., QK_NOPE:])], axis=-1)
            w_q = w_q.reshape(Q_RANK, MLA_HEADS * HEAD_W).astype(BF16)
            w_ukv = mla_w_ukv[j].reshape(KV_RANK, MLA_HEADS, QK_NOPE + V_DIM)
            w_k = w_ukv[..., :QK_NOPE].reshape(KV_RANK, MLA_HEADS * QK_NOPE).astype(BF16)
            w_vt = w_ukv[..., QK_NOPE:].reshape(KV_RANK, MLA_HEADS * V_DIM).T.astype(BF16)
            cq, ckv, kr, kr2 = _mla_down(h, mods, mla_w_dq[j].astype(BF16), mla_w_dkv[j].astype(BF16),
                                         w_kr2, mla_q_norm[j], mla_kv_norm[j], rope_tab)
            ckv_states.append(ckv[:ROWS_CTX].reshape(BATCH, SEQ, KV_RANK))
            krope_states.append(kr[:ROWS_CTX].reshape(BATCH, SEQ, QK_ROPE))
            q = _q_up(cq, w_q, rope_tab)
            k_tok, vt_tok = _kv_expand(ckv, kr2, w_k, w_vt)
            kc = cache_krope[:, j].reshape(DEC_BATCH * PAST_LEN, QK_ROPE).astype(BF16)
            k_cache, vt_cache = _kv_expand(cache_ckv[:, j].reshape(DEC_BATCH * PAST_LEN, KV_RANK),
                                           jnp.concatenate([kc, kc], axis=-1), w_k, w_vt)
            o = _attention(q, k_tok, vt_tok, k_cache, vt_cache)
            h = _mm_postnorm(o, mla_w_o[j].astype(BF16), zero_bias, h, mods, 2, ln_g[i, 0], ln_b[i, 0])
        elif kind == 1:
            pc = _hyena_in(h, mods, hy_w_in[j].astype(BF16), hy_b_in[j], hy_conv_w[j], hy_conv_b[j])
            fp = (hy_f_w1[j], hy_f_b1[j], hy_f_freq1[j], hy_f_w2[j], hy_f_b2[j], hy_f_freq2[j],
                  hy_f_w3[j], hy_log_decay[j])
            skip = hy_skip[j].reshape(HY_ORDER, 1, D_MODEL)
            z_ctx = _hyena_mix_dense(pc, *_hyena_filters(SEQ, *fp), skip, BATCH, SEQ)
            z_lat = _hyena_mix_pair(pc, *_hyena_filters(DEC_SEQ, *fp), skip)
            h = _mm_postnorm((z_ctx, z_lat), hy_w_out[j].astype(BF16), hy_b_out[j],
                             h, mods, 2, ln_g[i, 0], ln_b[i, 0])
        else:
            c_ch, ns_ch = _dft_tables(FNET_CG)
            p, q = _fnet_chan(h, mods, jnp.concatenate([c_ch, -ns_ch], axis=-1))
            f_ctx = _fnet_pos(c_ch, ns_ch, p, q, BATCH, SEQ)
            f_lat = _fnet_pos_factored(p, q)
            h = _mm_postnorm((f_ctx, f_lat), fn_w_out[j].astype(BF16), fn_b_out[j],
                             h, mods, 2, ln_g[i, 0], ln_b[i, 0])
        h = _ffn(h, mods, *ffn_w, i, ln_g[i, 1], ln_b[i, 1], split_out=i == DEPTH - 1)

    y_prompt = h[0].reshape(BATCH, SEQ, D_MODEL)
    y_sample = h[1].reshape(DEC_BATCH, DEC_SEQ, D_MODEL)
    return (y_prompt, y_sample, jnp.stack(ckv_states, axis=1), jnp.stack(krope_states, axis=1))
```

```python
import functools
import math

import jax
import jax.numpy as jnp
from jax import lax
from jax.experimental import pallas as pl
from jax.experimental.pallas import tpu as pltpu

F32 = jnp.float32
BF16 = jnp.bfloat16

D_MODEL = 2048
BATCH = 16
SEQ = 256
DEPTH = 4
DEC_BATCH = 2
DEC_SEQ = 4096
PAST_LEN = 512
GRID_W = 64
N_MIXERS = 3
MLA_HEADS = 16
QK_NOPE = 128
QK_ROPE = 64
V_DIM = 128
Q_RANK = 512
KV_RANK = 512
ROPE_THETA = 10000.0
AXIS_ROPE = QK_ROPE // 2
HY_ORDER = 2
HY_DIRS = 2
HY_CONV = 3
HY_BANDS = 16
HY_EMB = 1 + 2 * HY_BANDS
HY_FW = 64
HY_SHIFT = 0.05
FNET_GROUPS = 8
FNET_CG = D_MODEL // FNET_GROUPS
D_FF = -(-8 * D_MODEL // (3 * 256)) * 256
DN_ALPHA = (2 * DEPTH) ** 0.25
LN_EPS = 1e-5
RMS_EPS = 1e-6
N_MOD = 6

ROWS_CTX = BATCH * SEQ
ROWS_LAT = DEC_BATCH * DEC_SEQ
ROWS = ROWS_CTX + ROWS_LAT
N_COND = 1 + DEC_BATCH
COND_PAD = 8
HEAD_W = QK_NOPE + 2 * QK_ROPE
ATT_SCALE = (QK_NOPE + QK_ROPE) ** -0.5

V7X_VMEM_BYTES = 64 * 2 ** 20
VMEM_CAP_BYTES = V7X_VMEM_BYTES * 7 // 8
VMEM_FLOOR_BYTES = 32 * 2 ** 20


def _vmem_limit(pipelined, resident=0, temps=0):
    est = 2 * pipelined + resident + temps
    return int(min(max(est, VMEM_FLOOR_BYTES), VMEM_CAP_BYTES))


def _params(semantics, vmem):
    return pltpu.CompilerParams(dimension_semantics=semantics, vmem_limit_bytes=vmem)


def _nbytes(shape, dtype):
    return math.prod(shape) * jnp.dtype(dtype).itemsize


def _group_of_tile(i, tm):
    n_ctx = ROWS_CTX // tm
    return jnp.where(i < n_ctx, 0, 1 + (i - n_ctx) // (DEC_SEQ // tm))


def _mod_spec(which, tm):
    return pl.BlockSpec((None, 1, D_MODEL),
                        lambda i, *_: (which * COND_PAD + _group_of_tile(i, tm), 0, 0))


def _row_spec(width=D_MODEL):
    return pl.BlockSpec((1, width), lambda *_: (0, 0))


def _as_tuple(x):
    return x if isinstance(x, tuple) else (x,)


def _token_specs(xs, tm):
    width = xs[0].shape[1]
    if len(xs) == 1:
        return [pl.BlockSpec((tm, width), lambda i, *_: (i, 0))]
    n_ctx = ROWS_CTX // tm
    return [pl.BlockSpec((tm, width), lambda i, *_: (jnp.minimum(i, n_ctx - 1), 0)),
            pl.BlockSpec((tm, width), lambda i, *_: (jnp.maximum(i - n_ctx, 0), 0))]


def _token_rows(refs, r):
    if len(refs) == 1:
        return refs[0][r, :]
    n_ctx = ROWS_CTX // refs[0].shape[0]
    return jnp.where(pl.program_id(0) < n_ctx, refs[0][r, :], refs[1][r, :])


def _modulate(h, shift, scale):
    return h * (1.0 + scale) + shift


def _post_norm(h, delta, g, b):
    z = DN_ALPHA * h + delta
    mu = jnp.mean(z, axis=-1, keepdims=True)
    zc = z - mu
    var = jnp.mean(zc * zc, axis=-1, keepdims=True)
    return zc * lax.rsqrt(var + LN_EPS) * g + b


def _rms_norm(x, g):
    ms = jnp.mean(x * x, axis=-1, keepdims=True)
    return x * lax.rsqrt(ms + RMS_EPS) * g


def _dot(a, b):
    return jnp.dot(a, b, preferred_element_type=F32)


def _modvec_kernel(c_ref, w_ref, b_ref, o_ref):
    a = jax.nn.silu(c_ref[...]).astype(BF16)
    o_ref[...] = _dot(a, w_ref[...].astype(BF16)) + b_ref[...]


def _modulation_vectors(cond, ada_w, ada_b):
    tn = 1024
    n = N_MOD * D_MODEL
    out = pl.pallas_call(
        _modvec_kernel,
        grid=(DEPTH, n // tn),
        in_specs=[pl.BlockSpec((COND_PAD, D_MODEL), lambda l, j: (0, 0)),
                  pl.BlockSpec((None, D_MODEL, tn), lambda l, j: (l, 0, j)),
                  pl.BlockSpec((None, 1, tn), lambda l, j: (l, 0, j))],
        out_specs=pl.BlockSpec((None, COND_PAD, tn), lambda l, j: (l, 0, j)),
        out_shape=jax.ShapeDtypeStruct((DEPTH, COND_PAD, n), F32),
        compiler_params=_params(("parallel", "parallel"),
                                _vmem_limit(_nbytes((D_MODEL, tn), F32),
                                            temps=_nbytes((D_MODEL, tn), BF16))),
        name="modvec",
    )(cond, ada_w, ada_b.reshape(DEPTH, 1, n))
    out = out.reshape(DEPTH, COND_PAD, N_MOD, D_MODEL).transpose(0, 2, 1, 3)
    return out.reshape(DEPTH, N_MOD * COND_PAD, 1, D_MODEL)


def _hyena_in_kernel(n_ctx, h_ref, hp_ref, hn_ref, sh_ref, sc_ref, w_ref, b_ref, cw_ref, cb_ref,
                     o_ref, u_ref):
    i = pl.program_id(0)
    tm = h_ref.shape[0]

    @pl.when(pl.program_id(1) == 0)
    def _():
        mod = lambda x_ref: _modulate(x_ref[...], sh_ref[...], sc_ref[...]).astype(BF16)
        u_ref[:SUB, :] = mod(hp_ref)
        u_ref[SUB:SUB + tm, :] = mod(h_ref)
        u_ref[SUB + tm:, :] = mod(hn_ref)

    length = jnp.where(i < n_ctx, SEQ, DEC_SEQ)
    half = tm // 2
    starts = (0, half)
    ys = [_dot(u_ref[r0:r0 + half + 2 * SUB, :], w_ref[...]) + b_ref[...] for r0 in starts]
    for r0, y in zip(starts, ys):
        inner = slice(SUB, SUB + half)
        before = pltpu.roll(y, 1, 0)[inner]
        after = pltpu.roll(y, y.shape[0] - 1, 0)[inner]
        pos = (lax.broadcasted_iota(jnp.int32, before.shape, 0) + (i * tm + r0)) & (length - 1)
        before = jnp.where(pos == 0, 0.0, before)
        after = jnp.where(pos == length - 1, 0.0, after)
        o_ref[r0:r0 + half, :] = (cw_ref[0:1, :] * before + cw_ref[1:2, :] * y[inner]
                                  + cw_ref[2:3, :] * after + cb_ref[...])


def _hyena_in(h, mods, w, b, conv_w, conv_b, tm=1024, tn=1024):
    assert SEQ & (SEQ - 1) == 0 and DEC_SEQ & (DEC_SEQ - 1) == 0 and tm % SEQ == 0 and DEC_SEQ % tm == 0
    k, n = w.shape
    per, halo, last = D_MODEL // tn, tm // SUB, ROWS // SUB - 1
    return pl.pallas_call(
        functools.partial(_hyena_in_kernel, ROWS_CTX // tm),
        grid=(ROWS // tm, n // tn),
        in_specs=[pl.BlockSpec((tm, k), lambda i, j: (i, 0)),
                  pl.BlockSpec((SUB, k), lambda i, j: (jnp.maximum(i * halo - 1, 0), 0)),
                  pl.BlockSpec((SUB, k), lambda i, j: (jnp.minimum((i + 1) * halo, last), 0)),
                  _mod_spec(0, tm), _mod_spec(1, tm),
                  pl.BlockSpec((k, tn), lambda i, j: (0, j)),
                  pl.BlockSpec((1, tn), lambda i, j: (0, j)),
                  pl.BlockSpec((HY_CONV, tn), lambda i, j: (0, j)),
                  pl.BlockSpec((1, tn), lambda i, j: (0, j))],
        out_specs=pl.BlockSpec((None, tm, tn), lambda i, j: (j // per, i, j % per)),
        out_shape=jax.ShapeDtypeStruct((n // D_MODEL, ROWS, D_MODEL), F32),
        scratch_shapes=[pltpu.VMEM((tm + 2 * SUB, k), BF16)],
        compiler_params=_params(("parallel", "arbitrary"),
                                _vmem_limit(_nbytes((tm, k), F32) + _nbytes((k, tn), BF16)
                                            + _nbytes((tm, tn), F32),
                                            resident=_nbytes((tm, k), BF16),
                                            temps=5 * _nbytes((tm, tn), F32))),
        name="hyena_in",
    )(h, h, h, mods, mods, w, b.reshape(1, n), conv_w, conv_b.reshape(1, n))


def _mm_postnorm_kernel(n_a, n_h, *refs):
    a_refs, refs = refs[:n_a], refs[n_a:]
    (w_ref, bias_ref), refs = refs[:2], refs[2:]
    h_refs, (gate_ref, g_ref, b_ref, o_ref) = refs[:n_h], refs[n_h:]
    half = o_ref.shape[0] // 2
    halves = (slice(0, half), slice(half, 2 * half))
    ys = [_dot(_token_rows(a_refs, r), w_ref[...]) + bias_ref[...] for r in halves]
    for r, y in zip(halves, ys):
        o_ref[r, :] = _post_norm(_token_rows(h_refs, r), gate_ref[...] * y, g_ref[...], b_ref[...])


def _mm_postnorm(a, w, bias, h, mods, which_gate, ln_g, ln_b, tm=512):
    a, h = _as_tuple(a), _as_tuple(h)
    k = a[0].shape[1]
    return pl.pallas_call(
        functools.partial(_mm_postnorm_kernel, len(a), len(h)),
        grid=(ROWS // tm,),
        in_specs=(_token_specs(a, tm) + [pl.BlockSpec((k, D_MODEL), lambda i: (0, 0)), _row_spec()]
                  + _token_specs(h, tm) + [_mod_spec(which_gate, tm), _row_spec(), _row_spec()]),
        out_specs=pl.BlockSpec((tm, D_MODEL), lambda i: (i, 0)),
        out_shape=jax.ShapeDtypeStruct((ROWS, D_MODEL), F32),
        compiler_params=_params(("parallel",),
                                _vmem_limit(len(a) * _nbytes((tm, k), BF16)
                                            + _nbytes((k, D_MODEL), BF16)
                                            + (1 + len(h)) * _nbytes((tm, D_MODEL), F32),
                                            temps=3 * _nbytes((tm, D_MODEL), F32))),
        name="mm_postnorm",
    )(*a, w, bias.reshape(1, D_MODEL), *h, mods, ln_g.reshape(1, D_MODEL), ln_b.reshape(1, D_MODEL))


def _ffn_kernel(n_ctx, h_ref, sh_ref, sc_ref, gate_ref, g_ref, b_ref, wg_ref, wu_ref, wd_ref,
                *refs):
    i, f = pl.program_id(0), pl.program_id(1)
    if n_ctx is None:
        o_ref, u_ref = refs
        acc_ref, outs = o_ref, ((o_ref, None),)
    else:
        octx_ref, olat_ref, u_ref, acc_ref = refs
        outs = ((octx_ref, i < n_ctx), (olat_ref, i >= n_ctx))

    @pl.when(f == 0)
    def _():
        u_ref[...] = _modulate(h_ref[...], sh_ref[...], sc_ref[...]).astype(BF16)
        acc_ref[...] = jnp.zeros_like(acc_ref)

    half = u_ref.shape[0] // 2
    halves = (slice(0, half), slice(half, 2 * half))

    def step(o_ref):
        wu = wu_ref[...].astype(BF16)
        proj = [(_dot(u_ref[r, :], wg_ref[...]), _dot(u_ref[r, :], wu)) for r in halves]
        wd = wd_ref[...].astype(BF16)
        for r, (gate, up) in zip(halves, proj):
            act = (jax.nn.silu(gate) * up).astype(BF16)
            total = acc_ref[r, :] + _dot(act, wd)
            if o_ref is None:
                acc_ref[r, :] = total
            else:
                o_ref[r, :] = _post_norm(h_ref[r, :], gate_ref[...] * total, g_ref[...], b_ref[...])

    last = f == pl.num_programs(1) - 1
    pl.when(jnp.logical_not(last))(lambda: step(None))
    for o_ref, mine in outs:
        pl.when(last if mine is None else jnp.logical_and(last, mine))(functools.partial(step, o_ref))


def _ffn(h, mods, w_gate, w_up, w_down, layer, ln_g, ln_b, split_out, tm=512, tf=512):
    tile = _nbytes((tm, D_MODEL), F32)
    if split_out:
        n_ctx = ROWS_CTX // tm
        out_specs = [pl.BlockSpec((tm, D_MODEL), lambda i, f: (jnp.minimum(i, n_ctx - 1), 0)),
                     pl.BlockSpec((tm, D_MODEL), lambda i, f: (jnp.maximum(i - n_ctx, 0), 0))]
        out_shape = [jax.ShapeDtypeStruct((ROWS_CTX, D_MODEL), F32),
                     jax.ShapeDtypeStruct((ROWS_LAT, D_MODEL), F32)]
        scratch = [pltpu.VMEM((tm, D_MODEL), BF16), pltpu.VMEM((tm, D_MODEL), F32)]
        pipelined, resident = 3 * tile, tile + tile // 2
    else:
        n_ctx = None
        out_specs = pl.BlockSpec((tm, D_MODEL), lambda i, f: (i, 0))
        out_shape = jax.ShapeDtypeStruct((ROWS, D_MODEL), F32)
        scratch = [pltpu.VMEM((tm, D_MODEL), BF16)]
        pipelined, resident = 2 * tile, tile // 2
    return pl.pallas_call(
        functools.partial(_ffn_kernel, n_ctx),
        grid=(ROWS // tm, D_FF // tf),
        in_specs=[pl.BlockSpec((tm, D_MODEL), lambda i, f: (i, 0)),
                  _mod_spec(3, tm), _mod_spec(4, tm), _mod_spec(5, tm),
                  _row_spec(), _row_spec(),
                  pl.BlockSpec((None, D_MODEL, tf), lambda i, f: (layer, 0, f)),
                  pl.BlockSpec((None, D_MODEL, tf), lambda i, f: (layer, 0, f)),
                  pl.BlockSpec((None, tf, D_MODEL), lambda i, f: (layer, f, 0))],
        out_specs=out_specs,
        out_shape=out_shape,
        scratch_shapes=scratch,
        compiler_params=_params(("arbitrary" if split_out else "parallel", "arbitrary"),
                                _vmem_limit(pipelined + _nbytes((D_MODEL, tf), BF16)
                                            + _nbytes((D_MODEL, tf), w_up.dtype)
                                            + _nbytes((tf, D_MODEL), w_down.dtype),
                                            resident=resident, temps=2 * tile)),
        name="ffn",
    )(h, mods, mods, mods, ln_g.reshape(1, D_MODEL), ln_b.reshape(1, D_MODEL),
      w_gate, w_up, w_down)


def _mla_down_kernel(n_h, *refs):
    h_refs, (sh_ref, sc_ref, wdq_ref, wdkv_ref, wkr_ref, qn_ref, kvn_ref, rope_ref,
             cq_ref, ckv_ref, kr_ref, kr2_ref) = refs[:n_h], refs[n_h:]
    half = cq_ref.shape[0] // 2
    halves = (slice(0, half), slice(half, 2 * half))
    us = [_modulate(_token_rows(h_refs, r), sh_ref[...], sc_ref[...]).astype(BF16) for r in halves]
    projs = [(_dot(u, wdq_ref[...]), _dot(u, wdkv_ref[...]), _dot(u, wkr_ref[...])) for u in us]
    for r, (q_lat, kv_lat, t) in zip(halves, projs):
        cq_ref[r, :] = _rms_norm(q_lat, qn_ref[...]).astype(BF16)
        ckv_ref[r, :] = _rms_norm(kv_lat, kvn_ref[...])
        kr_ref[r, :] = t[:, :QK_ROPE]
        v = t * rope_ref[r, :]
        kr2_ref[r, :] = (v + pltpu.roll(v, QK_ROPE, 1)).astype(BF16)


def _mla_down(h, mods, w_dq, w_dkv, w_kr2, q_norm, kv_norm, rope_tab, tm=512):
    h = _as_tuple(h)
    row = lambda width: pl.BlockSpec((tm, width), lambda i: (i, 0))
    full = lambda shape: pl.BlockSpec(shape, lambda i: (0, 0))
    return pl.pallas_call(
        functools.partial(_mla_down_kernel, len(h)),
        grid=(ROWS // tm,),
        in_specs=_token_specs(h, tm) + [
            _mod_spec(0, tm), _mod_spec(1, tm),
            full((D_MODEL, Q_RANK)), full((D_MODEL, KV_RANK)), full((D_MODEL, 2 * QK_ROPE)),
            _row_spec(Q_RANK), _row_spec(KV_RANK), row(2 * QK_ROPE)],
        out_specs=[row(Q_RANK), row(KV_RANK), row(QK_ROPE), row(2 * QK_ROPE)],
        out_shape=[jax.ShapeDtypeStruct((ROWS, Q_RANK), BF16),
                   jax.ShapeDtypeStruct((ROWS, KV_RANK), F32),
                   jax.ShapeDtypeStruct((ROWS, QK_ROPE), F32),
                   jax.ShapeDtypeStruct((ROWS, 2 * QK_ROPE), BF16)],
        compiler_params=_params(("parallel",),
                                _vmem_limit(len(h) * _nbytes((tm, D_MODEL), F32)
                                            + _nbytes((D_MODEL, Q_RANK + KV_RANK + 2 * QK_ROPE), BF16)
                                            + 3 * _nbytes((tm, KV_RANK), F32),
                                            temps=2 * _nbytes((tm, D_MODEL), F32))),
        name="mla_down",
    )(*h, mods, mods, w_dq, w_dkv, w_kr2, q_norm.reshape(1, Q_RANK), kv_norm.reshape(1, KV_RANK),
      rope_tab)


NT_DIMS = (((1,), (1,)), ((), ()))


def _q_up_kernel(cq_ref, w_ref, rope_ref, q_ref):
    cq, tab = cq_ref[...], rope_ref[...] * LOG2E_SCALE
    for h in range(MLA_HEADS):
        r = _dot(cq, w_ref[:, h * HEAD_W:(h + 1) * HEAD_W])
        q_ref[h, :, :QK_NOPE] = (r[:, :QK_NOPE] * LOG2E_SCALE).astype(BF16)
        q_ref[h, :, QK_NOPE:] = (r[:, QK_NOPE:] * tab).astype(BF16)


def _q_up(cq, w_q, rope_tab, tm=512):
    return pl.pallas_call(
        _q_up_kernel,
        grid=(ROWS // tm,),
        in_specs=[pl.BlockSpec((tm, Q_RANK), lambda i: (i, 0)),
                  pl.BlockSpec((Q_RANK, MLA_HEADS * HEAD_W), lambda i: (0, 0)),
                  pl.BlockSpec((tm, 2 * QK_ROPE), lambda i: (i, 0))],
        out_specs=pl.BlockSpec((MLA_HEADS, tm, HEAD_W), lambda i: (0, i, 0)),
        out_shape=jax.ShapeDtypeStruct((MLA_HEADS, ROWS, HEAD_W), BF16),
        compiler_params=_params(("parallel",),
                                _vmem_limit(_nbytes((Q_RANK + tm, MLA_HEADS * HEAD_W), BF16))),
        name="q_up",
    )(cq, w_q, rope_tab)


VT_PAD = 16
VT_ROWS = V_DIM + VT_PAD


def _kv_expand_kernel(ckv_ref, kr2_ref, wk_ref, wvt_ref, k_ref, vt_ref):
    c = ckv_ref[...].astype(BF16)
    vt = lax.dot_general(wvt_ref[...], c, NT_DIMS, preferred_element_type=F32).astype(BF16)
    ones = jnp.ones((VT_PAD, vt.shape[1]), BF16)
    for h in range(MLA_HEADS):
        vt_ref[h * VT_ROWS:h * VT_ROWS + V_DIM, :] = vt[h * V_DIM:(h + 1) * V_DIM]
        vt_ref[h * VT_ROWS + V_DIM:(h + 1) * VT_ROWS, :] = ones
    kr2 = kr2_ref[...]
    pair_w = 2 * QK_NOPE
    for g in range(MLA_HEADS // 2):
        r = _dot(c, wk_ref[:, g * pair_w:(g + 1) * pair_w]).astype(BF16)
        for e in range(2):
            k_ref[2 * g + e, :, :QK_NOPE] = r[:, e * QK_NOPE:(e + 1) * QK_NOPE]
            k_ref[2 * g + e, :, QK_NOPE:] = kr2


def _kv_expand(ckv, kr2, w_k, w_vt, tm=512):
    rows = ckv.shape[0]
    return pl.pallas_call(
        _kv_expand_kernel,
        grid=(rows // tm,),
        in_specs=[pl.BlockSpec((tm, KV_RANK), lambda i: (i, 0)),
                  pl.BlockSpec((tm, 2 * QK_ROPE), lambda i: (i, 0)),
                  pl.BlockSpec((KV_RANK, MLA_HEADS * QK_NOPE), lambda i: (0, 0)),
                  pl.BlockSpec((MLA_HEADS * V_DIM, KV_RANK), lambda i: (0, 0))],
        out_specs=[pl.BlockSpec((MLA_HEADS, tm, HEAD_W), lambda i: (0, i, 0)),
                   pl.BlockSpec((MLA_HEADS * VT_ROWS, tm), lambda i: (0, i))],
        out_shape=[jax.ShapeDtypeStruct((MLA_HEADS, rows, HEAD_W), BF16),
                   jax.ShapeDtypeStruct((MLA_HEADS * VT_ROWS, rows), BF16)],
        compiler_params=_params(("parallel",),
                                _vmem_limit(_nbytes((tm, MLA_HEADS * (HEAD_W + V_DIM)), BF16)
                                            + 2 * _nbytes((KV_RANK, MLA_HEADS * V_DIM), BF16),
                                            temps=_nbytes((MLA_HEADS * V_DIM, tm), F32))),
        name="kv_expand",
    )(ckv, kr2, w_k, w_vt)


ATT_CHUNK = 512
ATT_SKEW = 3
LOG2E_SCALE = ATT_SCALE * math.log2(math.e)


def _attn_scores(q, k):
    return lax.dot_general(k, q, NT_DIMS, preferred_element_type=F32)


def _attn_values(s, vt, carry):
    m = jnp.max(s, axis=0, keepdims=True)
    if carry is not None:
        m_old, acc_old = carry
        m = jnp.maximum(m_old, m)
    acc = _dot(vt, jnp.exp2(s - m).astype(BF16))
    if carry is not None:
        acc = jnp.exp2(m_old - m) * acc_old + acc
    return m, acc


def _attn_output(acc):
    return (acc[:V_DIM] / acc[V_DIM:V_DIM + 1]).T.astype(BF16)


def _attn_ctx_kernel(q_ref, k_ref, vt_ref, o_ref):
    for h in range(MLA_HEADS):
        _, acc = _attn_values(_attn_scores(q_ref[h], k_ref[h]),
                              vt_ref[h * VT_ROWS:(h + 1) * VT_ROWS, :], None)
        o_ref[:, h * V_DIM:(h + 1) * V_DIM] = _attn_output(acc)


def _attn_lat_kernel(q_ref, k_ref, vt_ref, kc_ref, vtc_ref, prev_ref, o_ref):
    del prev_ref
    q = q_ref[...]
    n_tok = DEC_SEQ // ATT_CHUNK
    rows = lambda c: slice(c * ATT_CHUNK, (c + 1) * ATT_CHUNK)
    keys = [k_ref.at[rows(c), :] for c in range(n_tok)] + [kc_ref]
    vals = [vt_ref.at[:, rows(c)] for c in range(n_tok)] + [vtc_ref]
    carry = None
    scores = [_attn_scores(q, keys[c][...]) for c in range(ATT_SKEW)]
    for c in range(n_tok + 1):
        if c + ATT_SKEW <= n_tok:
            scores.append(_attn_scores(q, keys[c + ATT_SKEW][...]))
        carry = _attn_values(scores[c], vals[c][...], carry)
    o_ref[...] = _attn_output(carry[1])


def _attention(q, k_tok, vt_tok, k_cache, vt_cache, tq=2048):
    assert DEC_SEQ % ATT_CHUNK == 0
    out_shape = jax.ShapeDtypeStruct((ROWS, MLA_HEADS * V_DIM), BF16)
    o = pl.pallas_call(
        _attn_ctx_kernel,
        grid=(BATCH,),
        in_specs=[pl.BlockSpec((MLA_HEADS, SEQ, HEAD_W), lambda s: (0, s, 0)),
                  pl.BlockSpec((MLA_HEADS, SEQ, HEAD_W), lambda s: (0, s, 0)),
                  pl.BlockSpec((MLA_HEADS * VT_ROWS, SEQ), lambda s: (0, s))],
        out_specs=pl.BlockSpec((SEQ, MLA_HEADS * V_DIM), lambda s: (s, 0)),
        out_shape=out_shape,
        compiler_params=_params(("parallel",), VMEM_FLOOR_BYTES),
        name="attn_ctx",
    )(q, k_tok, vt_tok)

    lat0 = ROWS_CTX // DEC_SEQ
    q0 = ROWS_CTX // tq
    nq = DEC_SEQ // tq
    return pl.pallas_call(
        _attn_lat_kernel,
        grid=(DEC_BATCH, MLA_HEADS, nq),
        in_specs=[pl.BlockSpec((None, tq, HEAD_W), lambda b, h, i: (h, q0 + b * nq + i, 0)),
                  pl.BlockSpec((None, DEC_SEQ, HEAD_W), lambda b, h, i: (h, lat0 + b, 0)),
                  pl.BlockSpec((VT_ROWS, DEC_SEQ), lambda b, h, i: (h, lat0 + b)),
                  pl.BlockSpec((None, PAST_LEN, HEAD_W), lambda b, h, i: (h, b, 0)),
                  pl.BlockSpec((VT_ROWS, PAST_LEN), lambda b, h, i: (h, b)),
                  pl.BlockSpec(memory_space=pl.ANY)],
        out_specs=pl.BlockSpec((tq, V_DIM), lambda b, h, i: (q0 + b * nq + i, h)),
        out_shape=out_shape,
        input_output_aliases={5: 0},
        compiler_params=_params(("parallel", "parallel", "arbitrary"),
                                _vmem_limit(_nbytes((DEC_SEQ + PAST_LEN, HEAD_W + V_DIM), BF16),
                                            temps=8 * _nbytes((ATT_CHUNK, tq), F32))),
        name="attn_lat",
    )(q, k_tok, vt_tok, k_cache, vt_cache, o)


def _mm_kernel(a_ref, b_ref, o_ref, acc_ref):
    k = pl.program_id(2)

    @pl.when(k == 0)
    def _():
        acc_ref[...] = jnp.zeros_like(acc_ref)

    acc_ref[...] += _dot(a_ref[...], b_ref[...].astype(BF16))

    @pl.when(k == pl.num_programs(2) - 1)
    def _():
        o_ref[...] = acc_ref[...].astype(o_ref.dtype)


def _mm(a, b, out_dtype=F32, tm=1024, tn=1024, tk=512):
    m, kk = a.shape
    n = b.shape[1]
    tm, tn, tk = min(tm, m), min(tn, n), min(tk, kk)
    return pl.pallas_call(
        _mm_kernel,
        grid=(m // tm, n // tn, kk // tk),
        in_specs=[pl.BlockSpec((tm, tk), lambda i, j, k: (i, k)),
                  pl.BlockSpec((tk, tn), lambda i, j, k: (k, j))],
        out_specs=pl.BlockSpec((tm, tn), lambda i, j, k: (i, j)),
        out_shape=jax.ShapeDtypeStruct((m, n), out_dtype),
        scratch_shapes=[pltpu.VMEM((tm, tn), F32)],
        compiler_params=_params(("parallel", "parallel", "arbitrary"), VMEM_FLOOR_BYTES),
        name="mm",
    )(a, b)


def _seq_tiles(length):
    if length >= 1024:
        return 1024, 512, 1024
    return length, D_MODEL, length


def _dft_fwd_kernel(c_ref, s_ref, z_ref, kr_ref, ks_ref, yr_ref, ys_ref, accr_ref, accs_ref):
    k = pl.program_id(3)

    @pl.when(k == 0)
    def _():
        accr_ref[...] = jnp.zeros_like(accr_ref)
        accs_ref[...] = jnp.zeros_like(accs_ref)

    z = z_ref[...].astype(BF16)
    accr_ref[...] += _dot(c_ref[...], z)
    accs_ref[...] += _dot(s_ref[...], z)

    @pl.when(k == pl.num_programs(3) - 1)
    def _():
        zr, zs, kr, ks = accr_ref[...], accs_ref[...], kr_ref[...], ks_ref[...]
        yr_ref[...] = (zr * kr - zs * ks).astype(BF16)
        ys_ref[...] = (zr * ks + zs * kr).astype(BF16)


def _dft_fwd(tabs, z, z_which, kr, ks, order, n_seq, length):
    tf, tn, tk = _seq_tiles(length)
    nb, nf, nk = D_MODEL // tn, length // tf, length // tk
    kc0 = order * nb
    out = jax.ShapeDtypeStruct((n_seq * length, D_MODEL), BF16)
    return pl.pallas_call(
        _dft_fwd_kernel,
        grid=(n_seq, nf, nb, nk),
        in_specs=[pl.BlockSpec((tf, tk), lambda s, f, n, k: (f, k)),
                  pl.BlockSpec((tf, tk), lambda s, f, n, k: (f, k)),
                  pl.BlockSpec((None, tk, tn), lambda s, f, n, k: (z_which, s * nk + k, n)),
                  pl.BlockSpec((tf, tn), lambda s, f, n, k: (f, kc0 + n)),
                  pl.BlockSpec((tf, tn), lambda s, f, n, k: (f, kc0 + n))],
        out_specs=[pl.BlockSpec((tf, tn), lambda s, f, n, k: (s * nf + f, n)),
                   pl.BlockSpec((tf, tn), lambda s, f, n, k: (s * nf + f, n))],
        out_shape=[out, out],
        scratch_shapes=[pltpu.VMEM((tf, tn), F32), pltpu.VMEM((tf, tn), F32)],
        compiler_params=_params(("parallel", "parallel", "parallel", "arbitrary"),
                                _vmem_limit(2 * _nbytes((tf, tk), BF16) + _nbytes((tk, tn), F32)
                                            + 2 * _nbytes((tf, tn), F32) + 2 * _nbytes((tf, tn), BF16),
                                            resident=2 * _nbytes((tf, tn), F32),
                                            temps=4 * _nbytes((tf, tn), F32))),
        name="dft_fwd",
    )(tabs["c"], tabs["s"], z, kr, ks)


def _dft_inv_kernel(inv_len, ct_ref, st_ref, yr_ref, ys_ref, z_ref, gate_ref, skip_ref,
                    o_ref, acc_ref):
    k = pl.program_id(3)

    @pl.when(k == 0)
    def _():
        acc_ref[...] = jnp.zeros_like(acc_ref)

    acc_ref[...] += _dot(ct_ref[...], yr_ref[...]) + _dot(st_ref[...], ys_ref[...])

    @pl.when(k == pl.num_programs(3) - 1)
    def _():
        y = acc_ref[...] * inv_len + skip_ref[...] * z_ref[...]
        o_ref[...] = (gate_ref[...] * y).astype(o_ref.dtype)


def _dft_inv(tabs, yr, ys, z, z_which, gate, gate_which, skip, order, n_seq, length, out_dtype):
    tt, tn, tk = _seq_tiles(length)
    nb, nt, nk = D_MODEL // tn, length // tt, length // tk
    return pl.pallas_call(
        functools.partial(_dft_inv_kernel, 1.0 / length),
        grid=(n_seq, nt, nb, nk),
        in_specs=[pl.BlockSpec((tt, tk), lambda s, t, n, k: (t, k)),
                  pl.BlockSpec((tt, tk), lambda s, t, n, k: (t, k)),
                  pl.BlockSpec((tk, tn), lambda s, t, n, k: (s * nk + k, n)),
                  pl.BlockSpec((tk, tn), lambda s, t, n, k: (s * nk + k, n)),
                  pl.BlockSpec((None, tt, tn), lambda s, t, n, k: (z_which, s * nt + t, n)),
                  pl.BlockSpec((None, tt, tn), lambda s, t, n, k: (gate_which, s * nt + t, n)),
                  pl.BlockSpec((None, 1, tn), lambda s, t, n, k: (order, 0, n))],
        out_specs=pl.BlockSpec((tt, tn), lambda s, t, n, k: (s * nt + t, n)),
        out_shape=jax.ShapeDtypeStruct((n_seq * length, D_MODEL), out_dtype),
        scratch_shapes=[pltpu.VMEM((tt, tn), F32)],
        compiler_params=_params(("parallel", "parallel", "parallel", "arbitrary"),
                                _vmem_limit(2 * _nbytes((tt, tk), BF16) + 2 * _nbytes((tk, tn), BF16)
                                            + 3 * _nbytes((tt, tn), F32),
                                            resident=_nbytes((tt, tn), F32),
                                            temps=3 * _nbytes((tt, tn), F32))),
        name="dft_inv",
    )(tabs["ct"], tabs["st"], yr, ys, z, gate, skip)


TW_LANES = 128
SUB = 8
CT_ROWS = 2 * SUB
CT_COLS = 1024


def _lane_tile(x, width):
    return jnp.tile(x, (1, width // x.shape[-1]))


def _kron_sub(f):
    return jnp.kron(f, jnp.eye(SUB, dtype=f.dtype))


def _sub_rows(x, h):
    part = x[:, h * SUB:(h + 1) * SUB, :]
    return part.reshape(part.shape[0] * SUB, part.shape[2])


def _from_sub_rows(parts):
    split = [p.reshape(p.shape[0] // SUB, SUB, p.shape[1]) for p in parts]
    return jnp.concatenate(split, axis=1)


def _ct_stage_a_kernel(n_in, *refs):
    x_refs, (fa_ref, cw_ref, sw_ref, br_ref, bi_ref) = refs[:n_in], refs[n_in:]
    fa = fa_ref[...]
    half = fa.shape[0] // 2
    width = br_ref.shape[-1]
    xs = [r[...].astype(F32) for r in x_refs]
    b_re, b_im = [], []
    for h in range(CT_ROWS // SUB):
        x = jnp.concatenate([_sub_rows(x, h) for x in xs], axis=0).astype(BF16)
        a = _dot(fa, x)
        ar, ai = a[:half], a[half:]
        cw, sw = _lane_tile(cw_ref[h], width), _lane_tile(sw_ref[h], width)
        b_re.append(ar * cw + ai * sw)
        b_im.append(ai * cw - ar * sw)
    br_ref[...] = _from_sub_rows(b_re).astype(BF16)
    bi_ref[...] = _from_sub_rows(b_im).astype(BF16)


def _ct_stage_a(xs, n_seq, fa, cw, sw):
    n1 = fa.shape[0] // (2 * SUB)
    n2 = cw.shape[0] * CT_ROWS
    width = xs[0][0].shape[-1]
    tn2, tw = CT_ROWS, CT_COLS
    in_specs, blocks = [], 0
    for arr, prefix in xs:
        rows_in = arr.shape[-3]
        lead = (None,) * (arr.ndim - 3)
        in_specs.append(pl.BlockSpec(lead + (rows_in, tn2, tw),
                                     lambda s, i, c, prefix=prefix: prefix(s) + (0, i, c)))
        blocks += _nbytes((rows_in, tn2, tw), arr.dtype)
    twid = pl.BlockSpec((None,) + cw.shape[1:], lambda s, i, c: (i, 0, 0, 0))
    in_specs += [pl.BlockSpec(fa.shape, lambda s, i, c: (0, 0)), twid, twid]
    out = jax.ShapeDtypeStruct((n_seq, n1, n2, width), BF16)
    out_spec = pl.BlockSpec((None, n1, tn2, tw), lambda s, i, c: (s, 0, i, c))
    return pl.pallas_call(
        functools.partial(_ct_stage_a_kernel, len(xs)),
        grid=(n_seq, n2 // tn2, width // tw),
        in_specs=in_specs,
        out_specs=[out_spec, out_spec],
        out_shape=[out, out],
        compiler_params=_params(("parallel", "parallel", "parallel"),
                                _vmem_limit(blocks + 2 * _nbytes((n1, tn2, tw), BF16)
                                            + _nbytes(fa.shape, BF16),
                                            temps=8 * _nbytes((n1, tn2, tw), F32))),
        name="ct_stage_a",
    )(*[arr for arr, _ in xs], fa, cw, sw)


def _ct_mid_kernel(tk1, br_ref, bi_ref, ur_ref, ui_ref, *refs):
    partners, refs = refs[:2 * tk1], refs[2 * tk1:]
    fb_ref, fbi_ref, fbp0_ref, fbp1_ref, cw_ref, sw_ref, vr_ref, vi_ref = refs
    fb, fbi = fb_ref[...], fbi_ref[...]
    half = fb.shape[0] // 2
    width = br_ref.shape[-1]
    first_tile = pl.program_id(0) == 0
    for j in range(tk1):
        stack = lambda re_ref, im_ref, p=j: jnp.concatenate([re_ref[p], im_ref[p]], axis=0)
        fbp = fbp1_ref[...] if j else jnp.where(first_tile, fbp0_ref[...], fbp1_ref[...])
        u_own = _dot(fb[:half], stack(ur_ref, ui_ref))
        u_neg = _dot(fbp, stack(partners[2 * j], partners[2 * j + 1], 0))
        kr = 0.5 * (u_own + u_neg)
        ki = 0.5 * (u_neg - u_own)
        x = _dot(fb, stack(br_ref, bi_ref))
        xr, xi = x[:half], x[half:]
        y = jnp.concatenate([xr * kr - xi * ki, xr * ki + xi * kr], axis=0).astype(BF16)
        v = _dot(fbi, y)
        vr, vi = v[:half], v[half:]
        cw, sw = _lane_tile(cw_ref[j], width), _lane_tile(sw_ref[j], width)
        vr_ref[j] = (vr * cw - vi * sw).astype(BF16)
        vi_ref[j] = (vi * cw + vr * sw).astype(BF16)


def _ct_mid(br, bi, ur, ui, order, fb, fbi, fbp0, fbp1, cw, sw, tk1=4, td=1024):
    n1, n2, d = br.shape
    nd = d // td
    data = pl.BlockSpec((tk1, n2, td), lambda i, j: (i, 0, j))
    coef = pl.BlockSpec((tk1, n2, td), lambda i, j: (i, 0, order * nd + j))
    page = lambda p: pl.BlockSpec((1, n2, td),
                                  lambda i, j: ((n1 - (i * tk1 + p)) % n1, 0, order * nd + j))
    partners = [page(p) for p in range(tk1) for _ in range(2)]
    mat = pl.BlockSpec(fb.shape, lambda i, j: (0, 0))
    mat_p = pl.BlockSpec(fbp0.shape, lambda i, j: (0, 0))
    tw = pl.BlockSpec((tk1, n2, TW_LANES), lambda i, j: (i, 0, 0))
    out = jax.ShapeDtypeStruct((n1, n2, d), BF16)
    return pl.pallas_call(
        functools.partial(_ct_mid_kernel, tk1),
        grid=(n1 // tk1, nd),
        in_specs=[data, data, coef, coef] + partners + [mat, mat, mat_p, mat_p, tw, tw],
        out_specs=[data, data],
        out_shape=[out, out],
        compiler_params=_params(("parallel", "parallel"),
                                _vmem_limit(8 * _nbytes((tk1, n2, td), BF16),
                                            temps=10 * _nbytes((2 * n2, td), F32))),
        name="ct_mid",
    )(br, bi, ur, ui, *([ur, ui] * tk1), fb, fbi, fbp0, fbp1, cw, sw)


def _ct_inv_a_kernel(scale, vr_ref, vi_ref, fai_ref, z0_ref, z1_ref, g0_ref, g1_ref, skip_ref,
                     o_ref):
    fai = fai_ref[...]
    half = fai.shape[0] // 2
    skip = skip_ref[...]
    vr, vi = vr_ref[...].astype(F32), vi_ref[...].astype(F32)
    zs, gs = (z0_ref[...], z1_ref[...]), (g0_ref[...], g1_ref[...])
    outs = ([], [])
    for h in range(CT_ROWS // SUB):
        v = jnp.concatenate([_sub_rows(vr, h), _sub_rows(vi, h)], axis=0).astype(BF16)
        y = _dot(fai, v) * scale
        for b, yb in enumerate((y[:half], y[half:])):
            outs[b].append(_sub_rows(gs[b], h) * (yb + skip * _sub_rows(zs[b], h)))
    for b in range(2):
        o_ref[b] = _from_sub_rows(outs[b]).astype(o_ref.dtype)


def _ct_inv_a(vr, vi, fai, z, z_which, gate, gate_which, skip, order, out_dtype):
    n1, n2, d = vr.shape
    rows = fai.shape[0] // (2 * SUB)
    tn2, tw = CT_ROWS, CT_COLS // 2
    spec = pl.BlockSpec((n1, tn2, tw), lambda i, c: (0, i, c))
    pair = lambda which, b: pl.BlockSpec((None, None, rows, tn2, tw),
                                         lambda i, c: (which[0], which[1] + b, 0, i, c))
    return pl.pallas_call(
        functools.partial(_ct_inv_a_kernel, 1.0 / (n1 * n2)),
        grid=(n2 // tn2, d // tw),
        in_specs=[spec, spec, pl.BlockSpec(fai.shape, lambda i, c: (0, 0)),
                  pair(z_which, 0), pair(z_which, 1), pair(gate_which, 0), pair(gate_which, 1),
                  pl.BlockSpec((None, 1, tw), lambda i, c: (order, 0, c))],
        out_specs=pl.BlockSpec((2, rows, tn2, tw), lambda i, c: (0, 0, i, c)),
        out_shape=jax.ShapeDtypeStruct((2, rows, n2, d), out_dtype),
        compiler_params=_params(("parallel", "parallel"),
                                _vmem_limit(2 * _nbytes((n1, tn2, tw), BF16)
                                            + 6 * _nbytes((rows, tn2, tw), F32)
                                            + _nbytes(fai.shape, BF16),
                                            temps=8 * _nbytes((n1, tn2, tw), F32))),
        name="ct_inv_a",
    )(vr, vi, fai, z, z, gate, gate, skip)


def _ct_real_b_kernel(scale, br_ref, bi_ref, fb_ref, o_ref, so_ref):
    fb = fb_ref[...]
    for j in range(br_ref.shape[0]):
        so_ref[:, j, :] = _dot(fb, jnp.concatenate([br_ref[j], bi_ref[j]], axis=0)) * scale
    o_ref[...] = so_ref[...].astype(o_ref.dtype)


def _ct_real_b(br, bi, fb_re, scale):
    n_seq, n1, n2, d = br.shape
    tk1, tw = CT_ROWS, CT_COLS
    blk = pl.BlockSpec((None, tk1, n2, tw), lambda s, i, c: (s, i, 0, c))
    return pl.pallas_call(
        functools.partial(_ct_real_b_kernel, scale),
        grid=(n_seq, n1 // tk1, d // tw),
        in_specs=[blk, blk, pl.BlockSpec(fb_re.shape, lambda s, i, c: (0, 0))],
        out_specs=pl.BlockSpec((None, n2, tk1, tw), lambda s, i, c: (s, 0, i, c)),
        out_shape=jax.ShapeDtypeStruct((n_seq, n2, n1, d), BF16),
        scratch_shapes=[pltpu.VMEM((n2, tk1, tw), F32)],
        compiler_params=_params(("parallel", "parallel", "parallel"),
                                _vmem_limit(3 * _nbytes((tk1, n2, tw), BF16),
                                            resident=_nbytes((n2, tk1, tw), F32),
                                            temps=2 * _nbytes((n2, tk1, tw), F32))),
        name="ct_real_b",
    )(br, bi, fb_re)


def _cos_sin(num, den):
    ang = (num % den).astype(F32) * (2.0 * math.pi / den)
    return jnp.cos(ang), jnp.sin(ang)


def _ct_tables(n1, n2):
    i1 = jnp.arange(n1, dtype=jnp.int32)
    i2 = jnp.arange(n2, dtype=jnp.int32)
    c1, s1 = _cos_sin(i1[:, None] * i1[None, :], n1)
    c2, s2 = _cos_sin(i2[:, None] * i2[None, :], n2)
    cw, sw = _cos_sin(i2[:, None] * i1[None, :], n1 * n2)
    lanes = lambda t: jnp.broadcast_to(t[..., None], t.shape + (TW_LANES,))

    def stage_a_rows(t):
        t = t.reshape(n2 // CT_ROWS, CT_ROWS // SUB, SUB, n1)
        return lanes(jnp.swapaxes(t, 2, 3).reshape(n2 // CT_ROWS, CT_ROWS // SUB, n1 * SUB))

    return {"c1": c1, "s1": s1, "c2": c2, "s2": s2,
            "cw_a": stage_a_rows(cw), "sw_a": stage_a_rows(sw),
            "cw_b": lanes(cw.T), "sw_b": lanes(sw.T)}


def _filter_rows(feat_ref, t_ref, w1_ref, b1_ref, fr1_ref, w2_ref, b2_ref, fr2_ref, w3_ref,
                 decay_ref):
    x = jnp.sin(fr1_ref[...] * (_dot(feat_ref[...].astype(BF16), w1_ref[...].astype(BF16))
                                + b1_ref[...]))
    x = jnp.sin(fr2_ref[...] * (_dot(x.astype(BF16), w2_ref[...].astype(BF16)) + b2_ref[...]))
    h = _dot(x.astype(BF16), w3_ref[...].astype(BF16))
    return h * (jnp.exp(-t_ref[...] * jnp.exp(decay_ref[...])) + HY_SHIFT)


def _filter_stats_kernel(*refs):
    ss_ref = refs[-1]
    h = _filter_rows(*refs[:-1])

    @pl.when(pl.program_id(0) == 0)
    def _():
        ss_ref[...] = jnp.zeros_like(ss_ref)

    ss_ref[...] += jnp.sum(h * h, axis=0, keepdims=True)


def _filter_emit_kernel(*refs):
    ss_ref, a_ref, b_ref = refs[-3:]
    h = _filter_rows(*refs[:-3])
    n_dir = a_ref.shape[1]
    ss = ss_ref[...]
    norm = lax.rsqrt(ss[:, :n_dir] + ss[:, n_dir:] + 1e-12)
    fwd = h[:, :n_dir] * norm
    bwd = h[:, n_dir:] * norm
    row = lax.broadcasted_iota(jnp.int32, bwd.shape, 0) + pl.program_id(0) * bwd.shape[0]
    bwd = jnp.where(row == 0, 0.0, bwd)
    a_ref[...] = (fwd + bwd).astype(BF16)
    b_ref[...] = (fwd - bwd).astype(BF16)


def _hyena_filters(length, f_w1, f_b1, f_freq1, f_w2, f_b2, f_freq2, f_w3, log_decay):
    t = jnp.linspace(0.0, 1.0, length, dtype=F32)[:, None]
    t_idx = jnp.arange(length, dtype=F32)[:, None]
    bands = jnp.linspace(1e-4, HY_BANDS - 1, HY_BANDS, dtype=F32)
    w = 2.0 * math.pi * t_idx * bands / length
    feat = jnp.concatenate([t, jnp.cos(w), -jnp.sin(w)], axis=-1)
    emb_pad = 128
    feat = jnp.pad(feat, ((0, 0), (0, emb_pad - HY_EMB)))
    w1 = jnp.pad(f_w1, ((0, emb_pad - HY_EMB), (0, 0)))
    n_all = HY_DIRS * HY_ORDER * D_MODEL
    n_dir = HY_ORDER * D_MODEL
    tm = 256
    full = lambda shape: pl.BlockSpec(shape, lambda i: (0, 0))
    mlp_specs = [pl.BlockSpec((tm, emb_pad), lambda i: (i, 0)),
                 pl.BlockSpec((tm, 1), lambda i: (i, 0)),
                 full((emb_pad, HY_FW)), full((1, HY_FW)), full((1, HY_FW)),
                 full((HY_FW, HY_FW)), full((1, HY_FW)), full((1, HY_FW)),
                 full((HY_FW, n_all)), full((1, n_all))]
    mlp_args = (feat, t, w1, f_b1.reshape(1, HY_FW), f_freq1.reshape(1, HY_FW), f_w2,
                f_b2.reshape(1, HY_FW), f_freq2.reshape(1, HY_FW), f_w3, log_decay.reshape(1, n_all))
    vmem = _vmem_limit(_nbytes((tm, n_all), F32) + _nbytes((HY_FW, n_all), F32),
                       temps=4 * _nbytes((tm, n_all), F32))
    ss = pl.pallas_call(
        _filter_stats_kernel,
        grid=(length // tm,),
        in_specs=mlp_specs,
        out_specs=full((1, n_all)),
        out_shape=jax.ShapeDtypeStruct((1, n_all), F32),
        compiler_params=_params(("arbitrary",), vmem),
        name="filter_stats",
    )(*mlp_args)
    comb = jax.ShapeDtypeStruct((length, n_dir), BF16)
    return pl.pallas_call(
        _filter_emit_kernel,
        grid=(length // tm,),
        in_specs=mlp_specs + [full((1, n_all))],
        out_specs=[pl.BlockSpec((tm, n_dir), lambda i: (i, 0)), pl.BlockSpec((tm, n_dir), lambda i: (i, 0))],
        out_shape=[comb, comb],
        compiler_params=_params(("parallel",), vmem),
        name="filter_emit",
    )(*mlp_args, ss)


def _cis_product(row_hi, row_lo, period):
    def cis(phase):
        ang = (phase % period).astype(F32) * (2.0 * math.pi / period)
        return jnp.cos(ang)[:, :, None], jnp.sin(ang)[:, :, None]
    (c1, s1), (c0, s0) = cis(row_hi), cis(row_lo)
    c0, s0 = jnp.swapaxes(c0, 1, 2), jnp.swapaxes(s0, 1, 2)
    rows = row_hi.shape[0]
    return ((c1 * c0 - s1 * s0).reshape(rows, -1), (s1 * c0 + c1 * s0).reshape(rows, -1))


def _odd_dft_tables(length):
    split = 1 << (length.bit_length() // 2)
    r = jnp.arange(length, dtype=jnp.int32)[:, None]
    hi = jnp.arange(length // split, dtype=jnp.int32)[None, :] * split
    lo = jnp.arange(split, dtype=jnp.int32)[None, :]
    c, s = _cis_product((2 * r + 1) * hi, (2 * r + 1) * lo, 4 * length)
    ct, st = _cis_product(r * (2 * hi), r * (2 * lo + 1), 4 * length)
    return {"c": c.astype(BF16), "s": s.astype(BF16), "ct": ct.astype(BF16), "st": st.astype(BF16)}


def _dft_tables(length):
    split = 1 << (length.bit_length() // 2)
    r = jnp.arange(length, dtype=jnp.int32)[:, None]
    hi = jnp.arange(length // split, dtype=jnp.int32)[None, :] * split
    lo = jnp.arange(split, dtype=jnp.int32)[None, :]
    c, s = _cis_product(r * hi, r * lo, length)
    return c.astype(BF16), (-s).astype(BF16)


def _hyena_mix_dense(pc, filt_p, filt_m, skip, n_seq, length):
    tabs = _odd_dft_tables(length)
    kr, ks = _mm(tabs["c"], filt_p), _mm(tabs["s"], filt_m)
    yr, ys = _dft_fwd(tabs, pc, 0, kr, ks, 0, n_seq, length)
    z1 = _dft_inv(tabs, yr, ys, pc, 0, pc, 1, skip, 0, n_seq, length, F32)[None]
    yr, ys = _dft_fwd(tabs, z1, 0, kr, ks, 1, n_seq, length)
    return _dft_inv(tabs, yr, ys, z1, 0, pc, 2, skip, 1, n_seq, length, BF16)


HY_N1, HY_N2 = 64, 128


def _hyena_mix_pair(pc, filt_p, filt_m, skip):
    assert DEC_BATCH == 2 and HY_N1 * HY_N2 == 2 * DEC_SEQ and ROWS % DEC_SEQ == 0
    n1, n2, rows_in = HY_N1, HY_N2, HY_N1 // 2
    t = _ct_tables(n1, n2)
    c_in, s_in = t["c1"][:, :rows_in], t["s1"][:, :rows_in]
    fa = _kron_sub(jnp.block([[c_in, s_in], [-s_in, c_in]])).astype(BF16)
    fb = jnp.block([[t["c2"], t["s2"]], [-t["s2"], t["c2"]]]).astype(BF16)
    fbi = jnp.block([[t["c2"], -t["s2"]], [t["s2"], t["c2"]]]).astype(BF16)
    mirror = lambda shift: (-jnp.arange(n2) - shift) % n2
    fbp0, fbp1 = [jnp.concatenate([t["c2"][mirror(sh)], t["s2"][mirror(sh)]], axis=1).astype(BF16)
                  for sh in (0, 1)]
    c_out, s_out = t["c1"][:rows_in], t["s1"][:rows_in]
    fai = _kron_sub(jnp.block([[c_out, -s_out], [s_out, c_out]])).astype(BF16)
    n_filt = HY_ORDER * D_MODEL
    filt_view = lambda f: (f.reshape(rows_in, n2, n_filt), lambda s: ())
    ur, ui = _ct_stage_a([filt_view(filt_p), filt_view(filt_m)], 1, fa, t["cw_a"], t["sw_a"])

    pc = pc.reshape(3, ROWS // DEC_SEQ, rows_in, n2, D_MODEL)
    z, z0, out_dtypes = pc, ROWS_CTX // DEC_SEQ, (F32, BF16)
    for order in range(HY_ORDER):
        br, bi = _ct_stage_a([(z, lambda s, z0=z0: (0, z0)), (z, lambda s, z0=z0: (0, z0 + 1))], 1, fa,
                             t["cw_a"], t["sw_a"])
        vr, vi = _ct_mid(br[0], bi[0], ur[0], ui[0], order, fb, fbi, fbp0, fbp1, t["cw_b"], t["sw_b"])
        z = _ct_inv_a(vr, vi, fai, z, (0, z0), pc, (1 + order, ROWS_CTX // DEC_SEQ), skip, order,
                      out_dtypes[order])[None]
        z0 = 0
    return z.reshape(ROWS_LAT, D_MODEL)


def _fnet_chan_kernel(h_ref, sh_ref, sc_ref, w_ref, p_ref, q_ref):
    u = _modulate(h_ref[...], sh_ref[...], sc_ref[...]).astype(BF16)
    w = w_ref[...]
    for g in range(FNET_GROUPS):
        cols = slice(g * FNET_CG, (g + 1) * FNET_CG)
        r = _dot(u[:, cols], w)
        p_ref[:, cols] = r[:, :FNET_CG].astype(BF16)
        q_ref[:, cols] = r[:, FNET_CG:].astype(BF16)


def _fnet_chan(h, mods, w_cs, tm=512):
    row = pl.BlockSpec((tm, D_MODEL), lambda i: (i, 0))
    out = jax.ShapeDtypeStruct((ROWS, D_MODEL), BF16)
    return pl.pallas_call(
        _fnet_chan_kernel,
        grid=(ROWS // tm,),
        in_specs=[row, _mod_spec(0, tm), _mod_spec(1, tm),
                  pl.BlockSpec((FNET_CG, 2 * FNET_CG), lambda i: (0, 0))],
        out_specs=[row, row],
        out_shape=[out, out],
        compiler_params=_params(("parallel",),
                                _vmem_limit(3 * _nbytes((tm, D_MODEL), F32),
                                            temps=2 * _nbytes((tm, D_MODEL), F32))),
        name="fnet_chan",
    )(h, mods, mods, w_cs)


def _fnet_pos_kernel(scale, c_ref, ns_ref, p_ref, q_ref, o_ref, acc_ref):
    k = pl.program_id(3)

    @pl.when(k == 0)
    def _():
        acc_ref[...] = jnp.zeros_like(acc_ref)

    acc_ref[...] += (_dot(c_ref[...], p_ref[...].astype(BF16))
                     + _dot(ns_ref[...], q_ref[...].astype(BF16)))

    @pl.when(k == pl.num_programs(3) - 1)
    def _():
        o_ref[...] = (acc_ref[...] * scale).astype(o_ref.dtype)


def _fnet_pos(c_tab, ns_tab, p, q, n_seq, length):
    tt, tn, tk = _seq_tiles(length)
    nb, nt, nk = D_MODEL // tn, length // tt, length // tk
    scale = (length * FNET_CG) ** -0.5
    return pl.pallas_call(
        functools.partial(_fnet_pos_kernel, scale),
        grid=(n_seq, nt, nb, nk),
        in_specs=[pl.BlockSpec((tt, tk), lambda s, t, n, k: (t, k)),
                  pl.BlockSpec((tt, tk), lambda s, t, n, k: (t, k)),
                  pl.BlockSpec((tk, tn), lambda s, t, n, k: (s * nk + k, n)),
                  pl.BlockSpec((tk, tn), lambda s, t, n, k: (s * nk + k, n))],
        out_specs=pl.BlockSpec((tt, tn), lambda s, t, n, k: (s * nt + t, n)),
        out_shape=jax.ShapeDtypeStruct((n_seq * length, D_MODEL), BF16),
        scratch_shapes=[pltpu.VMEM((tt, tn), F32)],
        compiler_params=_params(("parallel", "parallel", "parallel", "arbitrary"),
                                _vmem_limit(2 * _nbytes((tt, tk), BF16) + 2 * _nbytes((tk, tn), BF16)
                                            + _nbytes((tt, tn), BF16),
                                            resident=_nbytes((tt, tn), F32),
                                            temps=2 * _nbytes((tt, tn), F32))),
        name="fnet_pos",
    )(c_tab, ns_tab, p, q)


FN_N1, FN_N2 = 32, 128


def _fnet_pos_factored(p, q):
    assert FN_N1 * FN_N2 == DEC_SEQ and ROWS % DEC_SEQ == 0
    n1, n2 = FN_N1, FN_N2
    t = _ct_tables(n1, n2)
    fa = _kron_sub(jnp.block([[t["c1"], -t["s1"]], [-t["s1"], -t["c1"]]])).astype(BF16)
    fb_re = jnp.concatenate([t["c2"], t["s2"]], axis=1).astype(BF16)
    lat0 = ROWS_CTX // DEC_SEQ
    view = lambda x: x.reshape(ROWS // DEC_SEQ, n1, n2, D_MODEL)
    seq = lambda s: (lat0 + s,)
    br, bi = _ct_stage_a([(view(p), seq), (view(q), seq)], DEC_BATCH, fa, t["cw_a"], t["sw_a"])
    f = _ct_real_b(br, bi, fb_re, (DEC_SEQ * FNET_CG) ** -0.5)
    return f.reshape(ROWS_LAT, D_MODEL)


def _rope_table():
    rows = DEC_SEQ // GRID_W
    row = jnp.repeat(jnp.arange(rows), GRID_W).astype(F32)
    col = jnp.tile(jnp.arange(GRID_W), rows).astype(F32)
    inv = ROPE_THETA ** (-jnp.arange(0, AXIS_ROPE, 2, dtype=F32) / AXIS_ROPE)
    ang = jnp.concatenate([row[:, None] * inv, col[:, None] * inv], axis=-1)
    cos = jnp.repeat(jnp.cos(ang), 2, axis=-1)
    sin = jnp.repeat(jnp.sin(ang), 2, axis=-1)
    lat = jnp.tile(jnp.concatenate([cos, sin], axis=-1), (DEC_BATCH, 1))
    ctx = jnp.concatenate([jnp.ones((ROWS_CTX, QK_ROPE), F32), jnp.zeros((ROWS_CTX, QK_ROPE), F32)],
                          axis=-1)
    return jnp.concatenate([ctx, lat], axis=0)


def _pair_rotated(w):
    pairs = w.reshape(w.shape[:-1] + (QK_ROPE // 2, 2))
    return jnp.stack([-pairs[..., 1], pairs[..., 0]], axis=-1).reshape(w.shape)


def kernel(x_prompt, x_sample, c, cache_ckv, cache_krope, c_ctx, ada_w, ada_b, ln_g, ln_b, ffn_w_gate, ffn_w_up, ffn_w_down, mla_w_dq, mla_q_norm, mla_w_uq, mla_w_dkv, mla_kv_norm, mla_w_kr, mla_w_ukv, mla_w_o, hy_w_in, hy_b_in, hy_conv_w, hy_conv_b, hy_f_w1, hy_f_b1, hy_f_freq1, hy_f_w2, hy_f_b2, hy_f_freq2, hy_f_w3, hy_log_decay, hy_skip, hy_w_out, hy_b_out, fn_w_out, fn_b_out):
    assert x_prompt.shape == (BATCH, SEQ, D_MODEL) and x_sample.shape == (DEC_BATCH, DEC_SEQ, D_MODEL)
    assert ROWS_CTX % DEC_SEQ == 0 and SEQ == FNET_CG

    assert N_MIXERS > 0 and DEPTH > 1
    h = (x_prompt.reshape(ROWS_CTX, D_MODEL), x_sample.reshape(ROWS_LAT, D_MODEL))
    cond = jnp.concatenate([c_ctx[None, :], c, jnp.zeros((COND_PAD - N_COND, D_MODEL), F32)])
    mods_all = _modulation_vectors(cond, ada_w, ada_b)
    zero_bias = jnp.zeros((D_MODEL,), F32)
    ffn_w = (ffn_w_gate.astype(BF16), ffn_w_up, ffn_w_down)
    rope_tab = None
    ckv_states, krope_states = [], []

    for i in range(DEPTH):
        kind, j = i % N_MIXERS, i // N_MIXERS
        mods = mods_all[i]
        if kind == 0:
            if rope_tab is None:
                rope_tab = _rope_table()
            w_kr2 = jnp.concatenate([mla_w_kr[j], _pair_rotated(mla_w_kr[j])], axis=-1).astype(BF16)
            wq = mla_w_uq[j].reshape(Q_RANK, MLA_HEADS, QK_NOPE + QK_ROPE)
            w_q = jnp.concatenate([wq, _pair_rotated(wq[..., QK_NOPE:])], axis=-1)
            w_q = w_q.reshape(Q_RANK, MLA_HEADS * HEAD_W).astype(BF16)
            w_ukv = mla_w_ukv[j].reshape(KV_RANK, MLA_HEADS, QK_NOPE + V_DIM)
            w_k = w_ukv[..., :QK_NOPE].reshape(KV_RANK, MLA_HEADS * QK_NOPE).astype(BF16)
            w_vt = w_ukv[..., QK_NOPE:].reshape(KV_RANK, MLA_HEADS * V_DIM).T.astype(BF16)
            cq, ckv, kr, kr2 = _mla_down(h, mods, mla_w_dq[j].astype(BF16), mla_w_dkv[j].astype(BF16),
                                         w_kr2, mla_q_norm[j], mla_kv_norm[j], rope_tab)
            ckv_states.append(ckv[:ROWS_CTX].reshape(BATCH, SEQ, KV_RANK))
            krope_states.append(kr[:ROWS_CTX].reshape(BATCH, SEQ, QK_ROPE))
            q = _q_up(cq, w_q, rope_tab)
            k_tok, vt_tok = _kv_expand(ckv, kr2, w_k, w_vt)
            kc = cache_krope[:, j].reshape(DEC_BATCH * PAST_LEN, QK_ROPE).astype(BF16)
            k_cache, vt_cache = _kv_expand(cache_ckv[:, j].reshape(DEC_BATCH * PAST_LEN, KV_RANK),
                                           jnp.concatenate([kc, kc], axis=-1), w_k, w_vt)
            o = _attention(q, k_tok, vt_tok, k_cache, vt_cache)
            h = _mm_postnorm(o, mla_w_o[j].astype(BF16), zero_bias, h, mods, 2, ln_g[i, 0], ln_b[i, 0])
        elif kind == 1:
            pc = _hyena_in(h, mods, hy_w_in[j].astype(BF16), hy_b_in[j], hy_conv_w[j], hy_conv_b[j])
            fp = (hy_f_w1[j], hy_f_b1[j], hy_f_freq1[j], hy_f_w2[j], hy_f_b2[j], hy_f_freq2[j],
                  hy_f_w3[j], hy_log_decay[j])
            skip = hy_skip[j].reshape(HY_ORDER, 1, D_MODEL)
            z_ctx = _hyena_mix_dense(pc, *_hyena_filters(SEQ, *fp), skip, BATCH, SEQ)
            z_lat = _hyena_mix_pair(pc, *_hyena_filters(DEC_SEQ, *fp), skip)
            h = _mm_postnorm((z_ctx, z_lat), hy_w_out[j].astype(BF16), hy_b_out[j],
                             h, mods, 2, ln_g[i, 0], ln_b[i, 0])
        else:
            c_ch, ns_ch = _dft_tables(FNET_CG)
            p, q = _fnet_chan(h, mods, jnp.concatenate([c_ch, -ns_ch], axis=-1))
            f_ctx = _fnet_pos(c_ch, ns_ch, p, q, BATCH, SEQ)
            f_lat = _fnet_pos_factored(p, q)
            h = _mm_postnorm((f_ctx, f_lat), fn_w_out[j].astype(BF16), fn_b_out[j],
                             h, mods, 2, ln_g[i, 0], ln_b[i, 0])
        h = _ffn(h, mods, *ffn_w, i, ln_g[i, 1], ln_b[i, 1], split_out=i == DEPTH - 1)

    y_prompt = h[0].reshape(BATCH, SEQ, D_MODEL)
    y_sample = h[1].reshape(DEC_BATCH, DEC_SEQ, D_MODEL)
    return (y_prompt, y_sample, jnp.stack(ckv_states, axis=1), jnp.stack(krope_states, axis=1))
```

```python
import functools
import math

import jax
import jax.numpy as jnp
from jax import lax
from jax.experimental import pallas as pl
from jax.experimental.pallas import tpu as pltpu

F32 = jnp.float32
BF16 = jnp.bfloat16

D_MODEL = 2048
BATCH = 16
SEQ = 256
DEPTH = 4
DEC_BATCH = 2
DEC_SEQ = 4096
PAST_LEN = 512
GRID_W = 64
N_MIXERS = 3
MLA_HEADS = 16
QK_NOPE = 128
QK_ROPE = 64
V_DIM = 128
Q_RANK = 512
KV_RANK = 512
ROPE_THETA = 10000.0
AXIS_ROPE = QK_ROPE // 2
HY_ORDER = 2
HY_DIRS = 2
HY_CONV = 3
HY_BANDS = 16
HY_EMB = 1 + 2 * HY_BANDS
HY_FW = 64
HY_SHIFT = 0.05
FNET_GROUPS = 8
FNET_CG = D_MODEL // FNET_GROUPS
D_FF = -(-8 * D_MODEL // (3 * 256)) * 256
DN_ALPHA = (2 * DEPTH) ** 0.25
LN_EPS = 1e-5
RMS_EPS = 1e-6
N_MOD = 6

ROWS_CTX = BATCH * SEQ
ROWS_LAT = DEC_BATCH * DEC_SEQ
ROWS = ROWS_CTX + ROWS_LAT
N_COND = 1 + DEC_BATCH
COND_PAD = 8
HEAD_W = QK_NOPE + 2 * QK_ROPE
ATT_SCALE = (QK_NOPE + QK_ROPE) ** -0.5

V7X_VMEM_BYTES = 64 * 2 ** 20
VMEM_CAP_BYTES = V7X_VMEM_BYTES * 7 // 8
VMEM_FLOOR_BYTES = 32 * 2 ** 20


def _vmem_limit(pipelined, resident=0, temps=0):
    est = 2 * pipelined + resident + temps
    return int(min(max(est, VMEM_FLOOR_BYTES), VMEM_CAP_BYTES))


def _params(semantics, vmem):
    return pltpu.CompilerParams(dimension_semantics=semantics, vmem_limit_bytes=vmem)


def _nbytes(shape, dtype):
    return math.prod(shape) * jnp.dtype(dtype).itemsize


def _group_of_tile(i, tm):
    n_ctx = ROWS_CTX // tm
    return jnp.where(i < n_ctx, 0, 1 + (i - n_ctx) // (DEC_SEQ // tm))


def _mod_spec(which, tm):
    return pl.BlockSpec((None, 1, D_MODEL),
                        lambda i, *_: (which * COND_PAD + _group_of_tile(i, tm), 0, 0))


def _row_spec(width=D_MODEL):
    return pl.BlockSpec((1, width), lambda *_: (0, 0))


def _as_tuple(x):
    return x if isinstance(x, tuple) else (x,)


def _token_specs(xs, tm):
    width = xs[0].shape[1]
    if len(xs) == 1:
        return [pl.BlockSpec((tm, width), lambda i, *_: (i, 0))]
    n_ctx = ROWS_CTX // tm
    return [pl.BlockSpec((tm, width), lambda i, *_: (jnp.minimum(i, n_ctx - 1), 0)),
            pl.BlockSpec((tm, width), lambda i, *_: (jnp.maximum(i - n_ctx, 0), 0))]


def _token_rows(refs, r):
    if len(refs) == 1:
        return refs[0][r, :]
    n_ctx = ROWS_CTX // refs[0].shape[0]
    return jnp.where(pl.program_id(0) < n_ctx, refs[0][r, :], refs[1][r, :])


def _modulate(h, shift, scale):
    return h * (1.0 + scale) + shift


def _post_norm(h, delta, g, b):
    z = DN_ALPHA * h + delta
    mu = jnp.mean(z, axis=-1, keepdims=True)
    zc = z - mu
    var = jnp.mean(zc * zc, axis=-1, keepdims=True)
    return zc * lax.rsqrt(var + LN_EPS) * g + b


def _rms_norm(x, g):
    ms = jnp.mean(x * x, axis=-1, keepdims=True)
    return x * lax.rsqrt(ms + RMS_EPS) * g


def _dot(a, b):
    return jnp.dot(a, b, preferred_element_type=F32)


def _modvec_kernel(c_ref, w_ref, b_ref, o_ref):
    a = jax.nn.silu(c_ref[...]).astype(BF16)
    o_ref[...] = _dot(a, w_ref[...].astype(BF16)) + b_ref[...]


def _modulation_vectors(cond, ada_w, ada_b):
    tn = 1024
    n = N_MOD * D_MODEL
    out = pl.pallas_call(
        _modvec_kernel,
        grid=(DEPTH, n // tn),
        in_specs=[pl.BlockSpec((COND_PAD, D_MODEL), lambda l, j: (0, 0)),
                  pl.BlockSpec((None, D_MODEL, tn), lambda l, j: (l, 0, j)),
                  pl.BlockSpec((None, 1, tn), lambda l, j: (l, 0, j))],
        out_specs=pl.BlockSpec((None, COND_PAD, tn), lambda l, j: (l, 0, j)),
        out_shape=jax.ShapeDtypeStruct((DEPTH, COND_PAD, n), F32),
        compiler_params=_params(("parallel", "parallel"),
                                _vmem_limit(_nbytes((D_MODEL, tn), F32),
                                            temps=_nbytes((D_MODEL, tn), BF16))),
        name="modvec",
    )(cond, ada_w, ada_b.reshape(DEPTH, 1, n))
    out = out.reshape(DEPTH, COND_PAD, N_MOD, D_MODEL).transpose(0, 2, 1, 3)
    return out.reshape(DEPTH, N_MOD * COND_PAD, 1, D_MODEL)


def _hyena_in_kernel(n_ctx, h_ref, hp_ref, hn_ref, sh_ref, sc_ref, w_ref, b_ref, cw_ref, cb_ref,
                     o_ref, u_ref):
    i = pl.program_id(0)
    tm = h_ref.shape[0]

    @pl.when(pl.program_id(1) == 0)
    def _():
        mod = lambda x_ref: _modulate(x_ref[...], sh_ref[...], sc_ref[...]).astype(BF16)
        u_ref[:SUB, :] = mod(hp_ref)
        u_ref[SUB:SUB + tm, :] = mod(h_ref)
        u_ref[SUB + tm:, :] = mod(hn_ref)

    length = jnp.where(i < n_ctx, SEQ, DEC_SEQ)
    half = tm // 2
    starts = (0, half)
    ys = [_dot(u_ref[r0:r0 + half + 2 * SUB, :], w_ref[...]) + b_ref[...] for r0 in starts]
    for r0, y in zip(starts, ys):
        inner = slice(SUB, SUB + half)
        before = pltpu.roll(y, 1, 0)[inner]
        after = pltpu.roll(y, y.shape[0] - 1, 0)[inner]
        pos = (lax.broadcasted_iota(jnp.int32, before.shape, 0) + (i * tm + r0)) & (length - 1)
        before = jnp.where(pos == 0, 0.0, before)
        after = jnp.where(pos == length - 1, 0.0, after)
        o_ref[r0:r0 + half, :] = (cw_ref[0:1, :] * before + cw_ref[1:2, :] * y[inner]
                                  + cw_ref[2:3, :] * after + cb_ref[...])


def _hyena_in(h, mods, w, b, conv_w, conv_b, tm=1024, tn=1024):
    assert SEQ & (SEQ - 1) == 0 and DEC_SEQ & (DEC_SEQ - 1) == 0 and tm % SEQ == 0 and DEC_SEQ % tm == 0
    k, n = w.shape
    per, halo, last = D_MODEL // tn, tm // SUB, ROWS // SUB - 1
    return pl.pallas_call(
        functools.partial(_hyena_in_kernel, ROWS_CTX // tm),
        grid=(ROWS // tm, n // tn),
        in_specs=[pl.BlockSpec((tm, k), lambda i, j: (i, 0)),
                  pl.BlockSpec((SUB, k), lambda i, j: (jnp.maximum(i * halo - 1, 0), 0)),
                  pl.BlockSpec((SUB, k), lambda i, j: (jnp.minimum((i + 1) * halo, last), 0)),
                  _mod_spec(0, tm), _mod_spec(1, tm),
                  pl.BlockSpec((k, tn), lambda i, j: (0, j)),
                  pl.BlockSpec((1, tn), lambda i, j: (0, j)),
                  pl.BlockSpec((HY_CONV, tn), lambda i, j: (0, j)),
                  pl.BlockSpec((1, tn), lambda i, j: (0, j))],
        out_specs=pl.BlockSpec((None, tm, tn), lambda i, j: (j // per, i, j % per)),
        out_shape=jax.ShapeDtypeStruct((n // D_MODEL, ROWS, D_MODEL), F32),
        scratch_shapes=[pltpu.VMEM((tm + 2 * SUB, k), BF16)],
        compiler_params=_params(("parallel", "arbitrary"),
                                _vmem_limit(_nbytes((tm, k), F32) + _nbytes((k, tn), BF16)
                                            + _nbytes((tm, tn), F32),
                                            resident=_nbytes((tm, k), BF16),
                                            temps=5 * _nbytes((tm, tn), F32))),
        name="hyena_in",
    )(h, h, h, mods, mods, w, b.reshape(1, n), conv_w, conv_b.reshape(1, n))


def _mm_postnorm_kernel(n_a, n_h, *refs):
    a_refs, refs = refs[:n_a], refs[n_a:]
    (w_ref, bias_ref), refs = refs[:2], refs[2:]
    h_refs, (gate_ref, g_ref, b_ref, o_ref) = refs[:n_h], refs[n_h:]
    half = o_ref.shape[0] // 2
    halves = (slice(0, half), slice(half, 2 * half))
    ys = [_dot(_token_rows(a_refs, r), w_ref[...]) + bias_ref[...] for r in halves]
    for r, y in zip(halves, ys):
        o_ref[r, :] = _post_norm(_token_rows(h_refs, r), gate_ref[...] * y, g_ref[...], b_ref[...])


def _mm_postnorm(a, w, bias, h, mods, which_gate, ln_g, ln_b, tm=512):
    a, h = _as_tuple(a), _as_tuple(h)
    k = a[0].shape[1]
    return pl.pallas_call(
        functools.partial(_mm_postnorm_kernel, len(a), len(h)),
        grid=(ROWS // tm,),
        in_specs=(_token_specs(a, tm) + [pl.BlockSpec((k, D_MODEL), lambda i: (0, 0)), _row_spec()]
                  + _token_specs(h, tm) + [_mod_spec(which_gate, tm), _row_spec(), _row_spec()]),
        out_specs=pl.BlockSpec((tm, D_MODEL), lambda i: (i, 0)),
        out_shape=jax.ShapeDtypeStruct((ROWS, D_MODEL), F32),
        compiler_params=_params(("parallel",),
                                _vmem_limit(len(a) * _nbytes((tm, k), BF16)
                                            + _nbytes((k, D_MODEL), BF16)
                                            + (1 + len(h)) * _nbytes((tm, D_MODEL), F32),
                                            temps=3 * _nbytes((tm, D_MODEL), F32))),
        name="mm_postnorm",
    )(*a, w, bias.reshape(1, D_MODEL), *h, mods, ln_g.reshape(1, D_MODEL), ln_b.reshape(1, D_MODEL))


def _ffn_kernel(n_ctx, h_ref, sh_ref, sc_ref, gate_ref, g_ref, b_ref, wg_ref, wu_ref, wd_ref,
                *refs):
    i, f = pl.program_id(0), pl.program_id(1)
    if n_ctx is None:
        o_ref, u_ref = refs
        acc_ref, outs = o_ref, ((o_ref, None),)
    else:
        octx_ref, olat_ref, u_ref, acc_ref = refs
        outs = ((octx_ref, i < n_ctx), (olat_ref, i >= n_ctx))

    half = u_ref.shape[0] // 2
    halves = (slice(0, half), slice(half, 2 * half))

    def step(first, o_ref):
        if first:
            us = [_modulate(h_ref[r, :], sh_ref[...], sc_ref[...]).astype(BF16) for r in halves]
            for r, u in zip(halves, us):
                u_ref[r, :] = u
        else:
            us = [u_ref[r, :] for r in halves]
        proj = [(_dot(u, wg_ref[...]), _dot(u, wu_ref[...])) for u in us]
        wd = wd_ref[...].astype(BF16)
        for r, (gate, up) in zip(halves, proj):
            act = (jax.nn.silu(gate) * up).astype(BF16)
            total = _dot(act, wd)
            if not first:
                total = acc_ref[r, :] + total
            if o_ref is None:
                acc_ref[r, :] = total
            else:
                o_ref[r, :] = _post_norm(h_ref[r, :], gate_ref[...] * total, g_ref[...], b_ref[...])

    last = f == pl.num_programs(1) - 1
    pl.when(f == 0)(functools.partial(step, True, None))
    pl.when(jnp.logical_and(f > 0, jnp.logical_not(last)))(functools.partial(step, False, None))
    for o_ref, mine in outs:
        pl.when(last if mine is None else jnp.logical_and(last, mine))(
            functools.partial(step, False, o_ref))


def _ffn(h, mods, w_gate, w_up, w_down, layer, ln_g, ln_b, split_out, tm=512, tf=512):
    assert D_FF // tf > 1
    tile = _nbytes((tm, D_MODEL), F32)
    if split_out:
        n_ctx = ROWS_CTX // tm
        out_specs = [pl.BlockSpec((tm, D_MODEL), lambda i, f: (jnp.minimum(i, n_ctx - 1), 0)),
                     pl.BlockSpec((tm, D_MODEL), lambda i, f: (jnp.maximum(i - n_ctx, 0), 0))]
        out_shape = [jax.ShapeDtypeStruct((ROWS_CTX, D_MODEL), F32),
                     jax.ShapeDtypeStruct((ROWS_LAT, D_MODEL), F32)]
        scratch = [pltpu.VMEM((tm, D_MODEL), BF16), pltpu.VMEM((tm, D_MODEL), F32)]
        pipelined, resident = 3 * tile, tile + tile // 2
    else:
        n_ctx = None
        out_specs = pl.BlockSpec((tm, D_MODEL), lambda i, f: (i, 0))
        out_shape = jax.ShapeDtypeStruct((ROWS, D_MODEL), F32)
        scratch = [pltpu.VMEM((tm, D_MODEL), BF16)]
        pipelined, resident = 2 * tile, tile // 2
    return pl.pallas_call(
        functools.partial(_ffn_kernel, n_ctx),
        grid=(ROWS // tm, D_FF // tf),
        in_specs=[pl.BlockSpec((tm, D_MODEL), lambda i, f: (i, 0)),
                  _mod_spec(3, tm), _mod_spec(4, tm), _mod_spec(5, tm),
                  _row_spec(), _row_spec(),
                  pl.BlockSpec((None, D_MODEL, tf), lambda i, f: (layer, 0, f)),
                  pl.BlockSpec((None, D_MODEL, tf), lambda i, f: (layer, 0, f)),
                  pl.BlockSpec((None, tf, D_MODEL), lambda i, f: (layer, f, 0))],
        out_specs=out_specs,
        out_shape=out_shape,
        scratch_shapes=scratch,
        compiler_params=_params(("arbitrary" if split_out else "parallel", "arbitrary"),
                                _vmem_limit(pipelined + 2 * _nbytes((D_MODEL, tf), BF16)
                                            + _nbytes((tf, D_MODEL), w_down.dtype),
                                            resident=resident, temps=2 * tile)),
        name="ffn",
    )(h, mods, mods, mods, ln_g.reshape(1, D_MODEL), ln_b.reshape(1, D_MODEL),
      w_gate, w_up, w_down)


def _mla_down_kernel(n_h, *refs):
    h_refs, (sh_ref, sc_ref, wdq_ref, wdkv_ref, wkr_ref, qn_ref, kvn_ref, rope_ref,
             cq_ref, ckv_ref, kr_ref, kr2_ref) = refs[:n_h], refs[n_h:]
    half = cq_ref.shape[0] // 2
    halves = (slice(0, half), slice(half, 2 * half))
    us = [_modulate(_token_rows(h_refs, r), sh_ref[...], sc_ref[...]).astype(BF16) for r in halves]
    projs = [(_dot(u, wdq_ref[...]), _dot(u, wdkv_ref[...]), _dot(u, wkr_ref[...])) for u in us]
    for r, (q_lat, kv_lat, t) in zip(halves, projs):
        cq_ref[r, :] = _rms_norm(q_lat, qn_ref[...]).astype(BF16)
        ckv_ref[r, :] = _rms_norm(kv_lat, kvn_ref[...])
        kr_ref[r, :] = t[:, :QK_ROPE]
        v = t * rope_ref[r, :]
        kr2_ref[r, :] = (v + pltpu.roll(v, QK_ROPE, 1)).astype(BF16)


def _mla_down(h, mods, w_dq, w_dkv, w_kr2, q_norm, kv_norm, rope_tab, tm=512):
    h = _as_tuple(h)
    row = lambda width: pl.BlockSpec((tm, width), lambda i: (i, 0))
    full = lambda shape: pl.BlockSpec(shape, lambda i: (0, 0))
    return pl.pallas_call(
        functools.partial(_mla_down_kernel, len(h)),
        grid=(ROWS // tm,),
        in_specs=_token_specs(h, tm) + [
            _mod_spec(0, tm), _mod_spec(1, tm),
            full((D_MODEL, Q_RANK)), full((D_MODEL, KV_RANK)), full((D_MODEL, 2 * QK_ROPE)),
            _row_spec(Q_RANK), _row_spec(KV_RANK), row(2 * QK_ROPE)],
        out_specs=[row(Q_RANK), row(KV_RANK), row(QK_ROPE), row(2 * QK_ROPE)],
        out_shape=[jax.ShapeDtypeStruct((ROWS, Q_RANK), BF16),
                   jax.ShapeDtypeStruct((ROWS, KV_RANK), F32),
                   jax.ShapeDtypeStruct((ROWS, QK_ROPE), F32),
                   jax.ShapeDtypeStruct((ROWS, 2 * QK_ROPE), BF16)],
        compiler_params=_params(("parallel",),
                                _vmem_limit(len(h) * _nbytes((tm, D_MODEL), F32)
                                            + _nbytes((D_MODEL, Q_RANK + KV_RANK + 2 * QK_ROPE), BF16)
                                            + 3 * _nbytes((tm, KV_RANK), F32),
                                            temps=2 * _nbytes((tm, D_MODEL), F32))),
        name="mla_down",
    )(*h, mods, mods, w_dq, w_dkv, w_kr2, q_norm.reshape(1, Q_RANK), kv_norm.reshape(1, KV_RANK),
      rope_tab)


NT_DIMS = (((1,), (1,)), ((), ()))


def _q_up_kernel(cq_ref, w_ref, rope_ref, q_ref):
    cq, tab = cq_ref[...], rope_ref[...] * LOG2E_SCALE
    for h in range(MLA_HEADS):
        r = _dot(cq, w_ref[:, h * HEAD_W:(h + 1) * HEAD_W])
        q_ref[h, :, :QK_NOPE] = (r[:, :QK_NOPE] * LOG2E_SCALE).astype(BF16)
        q_ref[h, :, QK_NOPE:] = (r[:, QK_NOPE:] * tab).astype(BF16)


def _q_up(cq, w_q, rope_tab, tm=512):
    return pl.pallas_call(
        _q_up_kernel,
        grid=(ROWS // tm,),
        in_specs=[pl.BlockSpec((tm, Q_RANK), lambda i: (i, 0)),
                  pl.BlockSpec((Q_RANK, MLA_HEADS * HEAD_W), lambda i: (0, 0)),
                  pl.BlockSpec((tm, 2 * QK_ROPE), lambda i: (i, 0))],
        out_specs=pl.BlockSpec((MLA_HEADS, tm, HEAD_W), lambda i: (0, i, 0)),
        out_shape=jax.ShapeDtypeStruct((MLA_HEADS, ROWS, HEAD_W), BF16),
        compiler_params=_params(("parallel",),
                                _vmem_limit(_nbytes((Q_RANK + tm, MLA_HEADS * HEAD_W), BF16))),
        name="q_up",
    )(cq, w_q, rope_tab)


VT_PAD = 16
VT_ROWS = V_DIM + VT_PAD


def _kv_expand_kernel(ckv_ref, kr2_ref, wk_ref, wvt_ref, k_ref, vt_ref):
    c = ckv_ref[...].astype(BF16)
    vt = lax.dot_general(wvt_ref[...], c, NT_DIMS, preferred_element_type=F32).astype(BF16)
    ones = jnp.ones((VT_PAD, vt.shape[1]), BF16)
    for h in range(MLA_HEADS):
        vt_ref[h * VT_ROWS:h * VT_ROWS + V_DIM, :] = vt[h * V_DIM:(h + 1) * V_DIM]
        vt_ref[h * VT_ROWS + V_DIM:(h + 1) * VT_ROWS, :] = ones
    kr2 = kr2_ref[...]
    pair_w = 2 * QK_NOPE
    for g in range(MLA_HEADS // 2):
        r = _dot(c, wk_ref[:, g * pair_w:(g + 1) * pair_w]).astype(BF16)
        for e in range(2):
            k_ref[2 * g + e, :, :QK_NOPE] = r[:, e * QK_NOPE:(e + 1) * QK_NOPE]
            k_ref[2 * g + e, :, QK_NOPE:] = kr2


def _kv_expand(ckv, kr2, w_k, w_vt, tm=512):
    rows = ckv.shape[0]
    return pl.pallas_call(
        _kv_expand_kernel,
        grid=(rows // tm,),
        in_specs=[pl.BlockSpec((tm, KV_RANK), lambda i: (i, 0)),
                  pl.BlockSpec((tm, 2 * QK_ROPE), lambda i: (i, 0)),
                  pl.BlockSpec((KV_RANK, MLA_HEADS * QK_NOPE), lambda i: (0, 0)),
                  pl.BlockSpec((MLA_HEADS * V_DIM, KV_RANK), lambda i: (0, 0))],
        out_specs=[pl.BlockSpec((MLA_HEADS, tm, HEAD_W), lambda i: (0, i, 0)),
                   pl.BlockSpec((MLA_HEADS * VT_ROWS, tm), lambda i: (0, i))],
        out_shape=[jax.ShapeDtypeStruct((MLA_HEADS, rows, HEAD_W), BF16),
                   jax.ShapeDtypeStruct((MLA_HEADS * VT_ROWS, rows), BF16)],
        compiler_params=_params(("parallel",),
                                _vmem_limit(_nbytes((tm, MLA_HEADS * (HEAD_W + V_DIM)), BF16)
                                            + 2 * _nbytes((KV_RANK, MLA_HEADS * V_DIM), BF16),
                                            temps=_nbytes((MLA_HEADS * V_DIM, tm), F32))),
        name="kv_expand",
    )(ckv, kr2, w_k, w_vt)


ATT_CHUNK = 512
ATT_SKEW = 4
LOG2E_SCALE = ATT_SCALE * math.log2(math.e)


def _attn_scores(q, k):
    return lax.dot_general(k, q, NT_DIMS, preferred_element_type=F32)


def _attn_values(s, vt, carry):
    m = jnp.max(s, axis=0, keepdims=True)
    if carry is not None:
        m_old, acc_old = carry
        m = jnp.maximum(m_old, m)
    acc = _dot(vt, jnp.exp2(s - m).astype(BF16))
    if carry is not None:
        acc = jnp.exp2(m_old - m) * acc_old + acc
    return m, acc


def _attn_output(acc):
    return (acc[:V_DIM] / acc[V_DIM:V_DIM + 1]).T.astype(BF16)


def _attn_ctx_kernel(q_ref, k_ref, vt_ref, o_ref):
    for h in range(MLA_HEADS):
        _, acc = _attn_values(_attn_scores(q_ref[h], k_ref[h]),
                              vt_ref[h * VT_ROWS:(h + 1) * VT_ROWS, :], None)
        o_ref[:, h * V_DIM:(h + 1) * V_DIM] = _attn_output(acc)


def _attn_lat_kernel(q_ref, k_ref, vt_ref, kc_ref, vtc_ref, prev_ref, o_ref):
    del prev_ref
    q = q_ref[...]
    n_tok = DEC_SEQ // ATT_CHUNK
    rows = lambda c: slice(c * ATT_CHUNK, (c + 1) * ATT_CHUNK)
    keys = [k_ref.at[rows(c), :] for c in range(n_tok)] + [kc_ref]
    vals = [vt_ref.at[:, rows(c)] for c in range(n_tok)] + [vtc_ref]
    carry = None
    scores = [_attn_scores(q, keys[c][...]) for c in range(ATT_SKEW)]
    for c in range(n_tok + 1):
        if c + ATT_SKEW <= n_tok:
            scores.append(_attn_scores(q, keys[c + ATT_SKEW][...]))
        carry = _attn_values(scores[c], vals[c][...], carry)
    o_ref[...] = _attn_output(carry[1])


def _attention(q, k_tok, vt_tok, k_cache, vt_cache, tq=2048):
    assert DEC_SEQ % ATT_CHUNK == 0
    out_shape = jax.ShapeDtypeStruct((ROWS, MLA_HEADS * V_DIM), BF16)
    o = pl.pallas_call(
        _attn_ctx_kernel,
        grid=(BATCH,),
        in_specs=[pl.BlockSpec((MLA_HEADS, SEQ, HEAD_W), lambda s: (0, s, 0)),
                  pl.BlockSpec((MLA_HEADS, SEQ, HEAD_W), lambda s: (0, s, 0)),
                  pl.BlockSpec((MLA_HEADS * VT_ROWS, SEQ), lambda s: (0, s))],
        out_specs=pl.BlockSpec((SEQ, MLA_HEADS * V_DIM), lambda s: (s, 0)),
        out_shape=out_shape,
        compiler_params=_params(("parallel",), VMEM_FLOOR_BYTES),
        name="attn_ctx",
    )(q, k_tok, vt_tok)

    lat0 = ROWS_CTX // DEC_SEQ
    q0 = ROWS_CTX // tq
    nq = DEC_SEQ // tq
    return pl.pallas_call(
        _attn_lat_kernel,
        grid=(DEC_BATCH, MLA_HEADS, nq),
        in_specs=[pl.BlockSpec((None, tq, HEAD_W), lambda b, h, i: (h, q0 + b * nq + i, 0)),
                  pl.BlockSpec((None, DEC_SEQ, HEAD_W), lambda b, h, i: (h, lat0 + b, 0)),
                  pl.BlockSpec((VT_ROWS, DEC_SEQ), lambda b, h, i: (h, lat0 + b)),
                  pl.BlockSpec((None, PAST_LEN, HEAD_W), lambda b, h, i: (h, b, 0)),
                  pl.BlockSpec((VT_ROWS, PAST_LEN), lambda b, h, i: (h, b)),
                  pl.BlockSpec(memory_space=pl.ANY)],
        out_specs=pl.BlockSpec((tq, V_DIM), lambda b, h, i: (q0 + b * nq + i, h)),
        out_shape=out_shape,
        input_output_aliases={5: 0},
        compiler_params=_params(("parallel", "parallel", "arbitrary"),
                                _vmem_limit(_nbytes((DEC_SEQ + PAST_LEN, HEAD_W + V_DIM), BF16),
                                            temps=8 * _nbytes((ATT_CHUNK, tq), F32))),
        name="attn_lat",
    )(q, k_tok, vt_tok, k_cache, vt_cache, o)


def _mm_kernel(a_ref, b_ref, o_ref, acc_ref):
    k = pl.program_id(2)

    @pl.when(k == 0)
    def _():
        acc_ref[...] = jnp.zeros_like(acc_ref)

    acc_ref[...] += _dot(a_ref[...], b_ref[...].astype(BF16))

    @pl.when(k == pl.num_programs(2) - 1)
    def _():
        o_ref[...] = acc_ref[...].astype(o_ref.dtype)


def _mm(a, b, out_dtype=F32, tm=1024, tn=1024, tk=512):
    m, kk = a.shape
    n = b.shape[1]
    tm, tn, tk = min(tm, m), min(tn, n), min(tk, kk)
    return pl.pallas_call(
        _mm_kernel,
        grid=(m // tm, n // tn, kk // tk),
        in_specs=[pl.BlockSpec((tm, tk), lambda i, j, k: (i, k)),
                  pl.BlockSpec((tk, tn), lambda i, j, k: (k, j))],
        out_specs=pl.BlockSpec((tm, tn), lambda i, j, k: (i, j)),
        out_shape=jax.ShapeDtypeStruct((m, n), out_dtype),
        scratch_shapes=[pltpu.VMEM((tm, tn), F32)],
        compiler_params=_params(("parallel", "parallel", "arbitrary"), VMEM_FLOOR_BYTES),
        name="mm",
    )(a, b)


def _seq_tiles(length):
    if length >= 1024:
        return 1024, 512, 1024
    return length, D_MODEL, length


def _dft_fwd_kernel(c_ref, s_ref, z_ref, kr_ref, ks_ref, yr_ref, ys_ref, accr_ref, accs_ref):
    k = pl.program_id(3)

    @pl.when(k == 0)
    def _():
        accr_ref[...] = jnp.zeros_like(accr_ref)
        accs_ref[...] = jnp.zeros_like(accs_ref)

    z = z_ref[...].astype(BF16)
    accr_ref[...] += _dot(c_ref[...], z)
    accs_ref[...] += _dot(s_ref[...], z)

    @pl.when(k == pl.num_programs(3) - 1)
    def _():
        zr, zs, kr, ks = accr_ref[...], accs_ref[...], kr_ref[...], ks_ref[...]
        yr_ref[...] = (zr * kr - zs * ks).astype(BF16)
        ys_ref[...] = (zr * ks + zs * kr).astype(BF16)


def _dft_fwd(tabs, z, z_which, kr, ks, order, n_seq, length):
    tf, tn, tk = _seq_tiles(length)
    nb, nf, nk = D_MODEL // tn, length // tf, length // tk
    kc0 = order * nb
    out = jax.ShapeDtypeStruct((n_seq * length, D_MODEL), BF16)
    return pl.pallas_call(
        _dft_fwd_kernel,
        grid=(n_seq, nf, nb, nk),
        in_specs=[pl.BlockSpec((tf, tk), lambda s, f, n, k: (f, k)),
                  pl.BlockSpec((tf, tk), lambda s, f, n, k: (f, k)),
                  pl.BlockSpec((None, tk, tn), lambda s, f, n, k: (z_which, s * nk + k, n)),
                  pl.BlockSpec((tf, tn), lambda s, f, n, k: (f, kc0 + n)),
                  pl.BlockSpec((tf, tn), lambda s, f, n, k: (f, kc0 + n))],
        out_specs=[pl.BlockSpec((tf, tn), lambda s, f, n, k: (s * nf + f, n)),
                   pl.BlockSpec((tf, tn), lambda s, f, n, k: (s * nf + f, n))],
        out_shape=[out, out],
        scratch_shapes=[pltpu.VMEM((tf, tn), F32), pltpu.VMEM((tf, tn), F32)],
        compiler_params=_params(("parallel", "parallel", "parallel", "arbitrary"),
                                _vmem_limit(2 * _nbytes((tf, tk), BF16) + _nbytes((tk, tn), F32)
                                            + 2 * _nbytes((tf, tn), F32) + 2 * _nbytes((tf, tn), BF16),
                                            resident=2 * _nbytes((tf, tn), F32),
                                            temps=4 * _nbytes((tf, tn), F32))),
        name="dft_fwd",
    )(tabs["c"], tabs["s"], z, kr, ks)


def _dft_inv_kernel(inv_len, ct_ref, st_ref, yr_ref, ys_ref, z_ref, gate_ref, skip_ref,
                    o_ref, acc_ref):
    k = pl.program_id(3)

    @pl.when(k == 0)
    def _():
        acc_ref[...] = jnp.zeros_like(acc_ref)

    acc_ref[...] += _dot(ct_ref[...], yr_ref[...]) + _dot(st_ref[...], ys_ref[...])

    @pl.when(k == pl.num_programs(3) - 1)
    def _():
        y = acc_ref[...] * inv_len + skip_ref[...] * z_ref[...]
        o_ref[...] = (gate_ref[...] * y).astype(o_ref.dtype)


def _dft_inv(tabs, yr, ys, z, z_which, gate, gate_which, skip, order, n_seq, length, out_dtype):
    tt, tn, tk = _seq_tiles(length)
    nb, nt, nk = D_MODEL // tn, length // tt, length // tk
    return pl.pallas_call(
        functools.partial(_dft_inv_kernel, 1.0 / length),
        grid=(n_seq, nt, nb, nk),
        in_specs=[pl.BlockSpec((tt, tk), lambda s, t, n, k: (t, k)),
                  pl.BlockSpec((tt, tk), lambda s, t, n, k: (t, k)),
                  pl.BlockSpec((tk, tn), lambda s, t, n, k: (s * nk + k, n)),
                  pl.BlockSpec((tk, tn), lambda s, t, n, k: (s * nk + k, n)),
                  pl.BlockSpec((None, tt, tn), lambda s, t, n, k: (z_which, s * nt + t, n)),
                  pl.BlockSpec((None, tt, tn), lambda s, t, n, k: (gate_which, s * nt + t, n)),
                  pl.BlockSpec((None, 1, tn), lambda s, t, n, k: (order, 0, n))],
        out_specs=pl.BlockSpec((tt, tn), lambda s, t, n, k: (s * nt + t, n)),
        out_shape=jax.ShapeDtypeStruct((n_seq * length, D_MODEL), out_dtype),
        scratch_shapes=[pltpu.VMEM((tt, tn), F32)],
        compiler_params=_params(("parallel", "parallel", "parallel", "arbitrary"),
                                _vmem_limit(2 * _nbytes((tt, tk), BF16) + 2 * _nbytes((tk, tn), BF16)
                                            + 3 * _nbytes((tt, tn), F32),
                                            resident=_nbytes((tt, tn), F32),
                                            temps=3 * _nbytes((tt, tn), F32))),
        name="dft_inv",
    )(tabs["ct"], tabs["st"], yr, ys, z, gate, skip)


TW_LANES = 128
SUB = 8
CT_ROWS = 2 * SUB
CT_COLS = 1024


def _lane_tile(x, width):
    return jnp.tile(x, (1, width // x.shape[-1]))


def _kron_sub(f):
    return jnp.kron(f, jnp.eye(SUB, dtype=f.dtype))


def _sub_rows(x, h):
    part = x[:, h * SUB:(h + 1) * SUB, :]
    return part.reshape(part.shape[0] * SUB, part.shape[2])


def _from_sub_rows(parts):
    split = [p.reshape(p.shape[0] // SUB, SUB, p.shape[1]) for p in parts]
    return jnp.concatenate(split, axis=1)


def _ct_stage_a_kernel(n_in, *refs):
    x_refs, (fa_ref, cw_ref, sw_ref, br_ref, bi_ref) = refs[:n_in], refs[n_in:]
    fa = fa_ref[...]
    half = fa.shape[0] // 2
    width = br_ref.shape[-1]
    xs = [r[...].astype(F32) for r in x_refs]
    b_re, b_im = [], []
    for h in range(CT_ROWS // SUB):
        x = jnp.concatenate([_sub_rows(x, h) for x in xs], axis=0).astype(BF16)
        a = _dot(fa, x)
        ar, ai = a[:half], a[half:]
        cw, sw = _lane_tile(cw_ref[h], width), _lane_tile(sw_ref[h], width)
        b_re.append(ar * cw + ai * sw)
        b_im.append(ai * cw - ar * sw)
    br_ref[...] = _from_sub_rows(b_re).astype(BF16)
    bi_ref[...] = _from_sub_rows(b_im).astype(BF16)


def _ct_stage_a(xs, n_seq, fa, cw, sw):
    n1 = fa.shape[0] // (2 * SUB)
    n2 = cw.shape[0] * CT_ROWS
    width = xs[0][0].shape[-1]
    tn2, tw = CT_ROWS, CT_COLS
    in_specs, blocks = [], 0
    for arr, prefix in xs:
        rows_in = arr.shape[-3]
        lead = (None,) * (arr.ndim - 3)
        in_specs.append(pl.BlockSpec(lead + (rows_in, tn2, tw),
                                     lambda s, i, c, prefix=prefix: prefix(s) + (0, i, c)))
        blocks += _nbytes((rows_in, tn2, tw), arr.dtype)
    twid = pl.BlockSpec((None,) + cw.shape[1:], lambda s, i, c: (i, 0, 0, 0))
    in_specs += [pl.BlockSpec(fa.shape, lambda s, i, c: (0, 0)), twid, twid]
    out = jax.ShapeDtypeStruct((n_seq, n1, n2, width), BF16)
    out_spec = pl.BlockSpec((None, n1, tn2, tw), lambda s, i, c: (s, 0, i, c))
    return pl.pallas_call(
        functools.partial(_ct_stage_a_kernel, len(xs)),
        grid=(n_seq, n2 // tn2, width // tw),
        in_specs=in_specs,
        out_specs=[out_spec, out_spec],
        out_shape=[out, out],
        compiler_params=_params(("parallel", "parallel", "parallel"),
                                _vmem_limit(blocks + 2 * _nbytes((n1, tn2, tw), BF16)
                                            + _nbytes(fa.shape, BF16),
                                            temps=8 * _nbytes((n1, tn2, tw), F32))),
        name="ct_stage_a",
    )(*[arr for arr, _ in xs], fa, cw, sw)


def _ct_mid_kernel(tk1, br_ref, bi_ref, ur_ref, ui_ref, *refs):
    partners, refs = refs[:2 * tk1], refs[2 * tk1:]
    fb_ref, fbi_ref, fbp0_ref, fbp1_ref, cw_ref, sw_ref, vr_ref, vi_ref = refs
    fb, fbi = fb_ref[...], fbi_ref[...]
    half = fb.shape[0] // 2
    width = br_ref.shape[-1]
    first_tile = pl.program_id(0) == 0
    for j in range(tk1):
        stack = lambda re_ref, im_ref, p=j: jnp.concatenate([re_ref[p], im_ref[p]], axis=0)
        fbp = fbp1_ref[...] if j else jnp.where(first_tile, fbp0_ref[...], fbp1_ref[...])
        u_own = _dot(fb[:half], stack(ur_ref, ui_ref))
        u_neg = _dot(fbp, stack(partners[2 * j], partners[2 * j + 1], 0))
        kr = 0.5 * (u_own + u_neg)
        ki = 0.5 * (u_neg - u_own)
        x = _dot(fb, stack(br_ref, bi_ref))
        xr, xi = x[:half], x[half:]
        y = jnp.concatenate([xr * kr - xi * ki, xr * ki + xi * kr], axis=0).astype(BF16)
        v = _dot(fbi, y)
        vr, vi = v[:half], v[half:]
        cw, sw = _lane_tile(cw_ref[j], width), _lane_tile(sw_ref[j], width)
        vr_ref[j] = (vr * cw - vi * sw).astype(BF16)
        vi_ref[j] = (vi * cw + vr * sw).astype(BF16)


def _ct_mid(br, bi, ur, ui, order, fb, fbi, fbp0, fbp1, cw, sw, tk1=4, td=1024):
    n1, n2, d = br.shape
    nd = d // td
    data = pl.BlockSpec((tk1, n2, td), lambda i, j: (i, 0, j))
    coef = pl.BlockSpec((tk1, n2, td), lambda i, j: (i, 0, order * nd + j))
    page = lambda p: pl.BlockSpec((1, n2, td),
                                  lambda i, j: ((n1 - (i * tk1 + p)) % n1, 0, order * nd + j))
    partners = [page(p) for p in range(tk1) for _ in range(2)]
    mat = pl.BlockSpec(fb.shape, lambda i, j: (0, 0))
    mat_p = pl.BlockSpec(fbp0.shape, lambda i, j: (0, 0))
    tw = pl.BlockSpec((tk1, n2, TW_LANES), lambda i, j: (i, 0, 0))
    out = jax.ShapeDtypeStruct((n1, n2, d), BF16)
    return pl.pallas_call(
        functools.partial(_ct_mid_kernel, tk1),
        grid=(n1 // tk1, nd),
        in_specs=[data, data, coef, coef] + partners + [mat, mat, mat_p, mat_p, tw, tw],
        out_specs=[data, data],
        out_shape=[out, out],
        compiler_params=_params(("parallel", "parallel"),
                                _vmem_limit(8 * _nbytes((tk1, n2, td), BF16),
                                            temps=10 * _nbytes((2 * n2, td), F32))),
        name="ct_mid",
    )(br, bi, ur, ui, *([ur, ui] * tk1), fb, fbi, fbp0, fbp1, cw, sw)


def _ct_inv_a_kernel(scale, vr_ref, vi_ref, fai_ref, z0_ref, z1_ref, g0_ref, g1_ref, skip_ref,
                     o_ref):
    fai = fai_ref[...]
    half = fai.shape[0] // 2
    skip = skip_ref[...]
    vr, vi = vr_ref[...].astype(F32), vi_ref[...].astype(F32)
    zs, gs = (z0_ref[...], z1_ref[...]), (g0_ref[...], g1_ref[...])
    outs = ([], [])
    for h in range(CT_ROWS // SUB):
        v = jnp.concatenate([_sub_rows(vr, h), _sub_rows(vi, h)], axis=0).astype(BF16)
        y = _dot(fai, v) * scale
        for b, yb in enumerate((y[:half], y[half:])):
            outs[b].append(_sub_rows(gs[b], h) * (yb + skip * _sub_rows(zs[b], h)))
    for b in range(2):
        o_ref[b] = _from_sub_rows(outs[b]).astype(o_ref.dtype)


def _ct_inv_a(vr, vi, fai, z, z_which, gate, gate_which, skip, order, out_dtype):
    n1, n2, d = vr.shape
    rows = fai.shape[0] // (2 * SUB)
    tn2, tw = CT_ROWS, CT_COLS // 2
    spec = pl.BlockSpec((n1, tn2, tw), lambda i, c: (0, i, c))
    pair = lambda which, b: pl.BlockSpec((None, None, rows, tn2, tw),
                                         lambda i, c: (which[0], which[1] + b, 0, i, c))
    return pl.pallas_call(
        functools.partial(_ct_inv_a_kernel, 1.0 / (n1 * n2)),
        grid=(n2 // tn2, d // tw),
        in_specs=[spec, spec, pl.BlockSpec(fai.shape, lambda i, c: (0, 0)),
                  pair(z_which, 0), pair(z_which, 1), pair(gate_which, 0), pair(gate_which, 1),
                  pl.BlockSpec((None, 1, tw), lambda i, c: (order, 0, c))],
        out_specs=pl.BlockSpec((2, rows, tn2, tw), lambda i, c: (0, 0, i, c)),
        out_shape=jax.ShapeDtypeStruct((2, rows, n2, d), out_dtype),
        compiler_params=_params(("parallel", "parallel"),
                                _vmem_limit(2 * _nbytes((n1, tn2, tw), BF16)
                                            + 6 * _nbytes((rows, tn2, tw), F32)
                                            + _nbytes(fai.shape, BF16),
                                            temps=8 * _nbytes((n1, tn2, tw), F32))),
        name="ct_inv_a",
    )(vr, vi, fai, z, z, gate, gate, skip)


def _ct_real_b_kernel(scale, br_ref, bi_ref, fb_ref, o_ref, so_ref):
    fb = fb_ref[...]
    for j in range(br_ref.shape[0]):
        so_ref[:, j, :] = _dot(fb, jnp.concatenate([br_ref[j], bi_ref[j]], axis=0)) * scale
    o_ref[...] = so_ref[...].astype(o_ref.dtype)


def _ct_real_b(br, bi, fb_re, scale):
    n_seq, n1, n2, d = br.shape
    tk1, tw = CT_ROWS, CT_COLS
    blk = pl.BlockSpec((None, tk1, n2, tw), lambda s, i, c: (s, i, 0, c))
    return pl.pallas_call(
        functools.partial(_ct_real_b_kernel, scale),
        grid=(n_seq, n1 // tk1, d // tw),
        in_specs=[blk, blk, pl.BlockSpec(fb_re.shape, lambda s, i, c: (0, 0))],
        out_specs=pl.BlockSpec((None, n2, tk1, tw), lambda s, i, c: (s, 0, i, c)),
        out_shape=jax.ShapeDtypeStruct((n_seq, n2, n1, d), BF16),
        scratch_shapes=[pltpu.VMEM((n2, tk1, tw), F32)],
        compiler_params=_params(("parallel", "parallel", "parallel"),
                                _vmem_limit(3 * _nbytes((tk1, n2, tw), BF16),
                                            resident=_nbytes((n2, tk1, tw), F32),
                                            temps=2 * _nbytes((n2, tk1, tw), F32))),
        name="ct_real_b",
    )(br, bi, fb_re)


def _cos_sin(num, den):
    ang = (num % den).astype(F32) * (2.0 * math.pi / den)
    return jnp.cos(ang), jnp.sin(ang)


def _ct_tables(n1, n2):
    i1 = jnp.arange(n1, dtype=jnp.int32)
    i2 = jnp.arange(n2, dtype=jnp.int32)
    c1, s1 = _cos_sin(i1[:, None] * i1[None, :], n1)
    c2, s2 = _cos_sin(i2[:, None] * i2[None, :], n2)
    cw, sw = _cos_sin(i2[:, None] * i1[None, :], n1 * n2)
    lanes = lambda t: jnp.broadcast_to(t[..., None], t.shape + (TW_LANES,))

    def stage_a_rows(t):
        t = t.reshape(n2 // CT_ROWS, CT_ROWS // SUB, SUB, n1)
        return lanes(jnp.swapaxes(t, 2, 3).reshape(n2 // CT_ROWS, CT_ROWS // SUB, n1 * SUB))

    return {"c1": c1, "s1": s1, "c2": c2, "s2": s2,
            "cw_a": stage_a_rows(cw), "sw_a": stage_a_rows(sw),
            "cw_b": lanes(cw.T), "sw_b": lanes(sw.T)}


def _filter_rows(feat_ref, t_ref, w1_ref, b1_ref, fr1_ref, w2_ref, b2_ref, fr2_ref, w3_ref,
                 decay_ref):
    x = jnp.sin(fr1_ref[...] * (_dot(feat_ref[...].astype(BF16), w1_ref[...].astype(BF16))
                                + b1_ref[...]))
    x = jnp.sin(fr2_ref[...] * (_dot(x.astype(BF16), w2_ref[...].astype(BF16)) + b2_ref[...]))
    h = _dot(x.astype(BF16), w3_ref[...].astype(BF16))
    return h * (jnp.exp(-t_ref[...] * jnp.exp(decay_ref[...])) + HY_SHIFT)


def _filter_stats_kernel(*refs):
    ss_ref = refs[-1]
    h = _filter_rows(*refs[:-1])

    @pl.when(pl.program_id(0) == 0)
    def _():
        ss_ref[...] = jnp.zeros_like(ss_ref)

    ss_ref[...] += jnp.sum(h * h, axis=0, keepdims=True)


def _filter_emit_kernel(*refs):
    ss_ref, a_ref, b_ref = refs[-3:]
    h = _filter_rows(*refs[:-3])
    n_dir = a_ref.shape[1]
    ss = ss_ref[...]
    norm = lax.rsqrt(ss[:, :n_dir] + ss[:, n_dir:] + 1e-12)
    fwd = h[:, :n_dir] * norm
    bwd = h[:, n_dir:] * norm
    row = lax.broadcasted_iota(jnp.int32, bwd.shape, 0) + pl.program_id(0) * bwd.shape[0]
    bwd = jnp.where(row == 0, 0.0, bwd)
    a_ref[...] = (fwd + bwd).astype(BF16)
    b_ref[...] = (fwd - bwd).astype(BF16)


def _hyena_filters(length, f_w1, f_b1, f_freq1, f_w2, f_b2, f_freq2, f_w3, log_decay):
    t = jnp.linspace(0.0, 1.0, length, dtype=F32)[:, None]
    t_idx = jnp.arange(length, dtype=F32)[:, None]
    bands = jnp.linspace(1e-4, HY_BANDS - 1, HY_BANDS, dtype=F32)
    w = 2.0 * math.pi * t_idx * bands / length
    feat = jnp.concatenate([t, jnp.cos(w), -jnp.sin(w)], axis=-1)
    emb_pad = 128
    feat = jnp.pad(feat, ((0, 0), (0, emb_pad - HY_EMB)))
    w1 = jnp.pad(f_w1, ((0, emb_pad - HY_EMB), (0, 0)))
    n_all = HY_DIRS * HY_ORDER * D_MODEL
    n_dir = HY_ORDER * D_MODEL
    tm = 256
    full = lambda shape: pl.BlockSpec(shape, lambda i: (0, 0))
    mlp_specs = [pl.BlockSpec((tm, emb_pad), lambda i: (i, 0)),
                 pl.BlockSpec((tm, 1), lambda i: (i, 0)),
                 full((emb_pad, HY_FW)), full((1, HY_FW)), full((1, HY_FW)),
                 full((HY_FW, HY_FW)), full((1, HY_FW)), full((1, HY_FW)),
                 full((HY_FW, n_all)), full((1, n_all))]
    mlp_args = (feat, t, w1, f_b1.reshape(1, HY_FW), f_freq1.reshape(1, HY_FW), f_w2,
                f_b2.reshape(1, HY_FW), f_freq2.reshape(1, HY_FW), f_w3, log_decay.reshape(1, n_all))
    vmem = _vmem_limit(_nbytes((tm, n_all), F32) + _nbytes((HY_FW, n_all), F32),
                       temps=4 * _nbytes((tm, n_all), F32))
    ss = pl.pallas_call(
        _filter_stats_kernel,
        grid=(length // tm,),
        in_specs=mlp_specs,
        out_specs=full((1, n_all)),
        out_shape=jax.ShapeDtypeStruct((1, n_all), F32),
        compiler_params=_params(("arbitrary",), vmem),
        name="filter_stats",
    )(*mlp_args)
    comb = jax.ShapeDtypeStruct((length, n_dir), BF16)
    return pl.pallas_call(
        _filter_emit_kernel,
        grid=(length // tm,),
        in_specs=mlp_specs + [full((1, n_all))],
        out_specs=[pl.BlockSpec((tm, n_dir), lambda i: (i, 0)), pl.BlockSpec((tm, n_dir), lambda i: (i, 0))],
        out_shape=[comb, comb],
        compiler_params=_params(("parallel",), vmem),
        name="filter_emit",
    )(*mlp_args, ss)


def _cis_product(row_hi, row_lo, period):
    def cis(phase):
        ang = (phase % period).astype(F32) * (2.0 * math.pi / period)
        return jnp.cos(ang)[:, :, None], jnp.sin(ang)[:, :, None]
    (c1, s1), (c0, s0) = cis(row_hi), cis(row_lo)
    c0, s0 = jnp.swapaxes(c0, 1, 2), jnp.swapaxes(s0, 1, 2)
    rows = row_hi.shape[0]
    return ((c1 * c0 - s1 * s0).reshape(rows, -1), (s1 * c0 + c1 * s0).reshape(rows, -1))


def _odd_dft_tables(length):
    split = 1 << (length.bit_length() // 2)
    r = jnp.arange(length, dtype=jnp.int32)[:, None]
    hi = jnp.arange(length // split, dtype=jnp.int32)[None, :] * split
    lo = jnp.arange(split, dtype=jnp.int32)[None, :]
    c, s = _cis_product((2 * r + 1) * hi, (2 * r + 1) * lo, 4 * length)
    ct, st = _cis_product(r * (2 * hi), r * (2 * lo + 1), 4 * length)
    return {"c": c.astype(BF16), "s": s.astype(BF16), "ct": ct.astype(BF16), "st": st.astype(BF16)}


def _dft_tables(length):
    split = 1 << (length.bit_length() // 2)
    r = jnp.arange(length, dtype=jnp.int32)[:, None]
    hi = jnp.arange(length // split, dtype=jnp.int32)[None, :] * split
    lo = jnp.arange(split, dtype=jnp.int32)[None, :]
    c, s = _cis_product(r * hi, r * lo, length)
    return c.astype(BF16), (-s).astype(BF16)


def _hyena_mix_dense(pc, filt_p, filt_m, skip, n_seq, length):
    tabs = _odd_dft_tables(length)
    kr, ks = _mm(tabs["c"], filt_p), _mm(tabs["s"], filt_m)
    yr, ys = _dft_fwd(tabs, pc, 0, kr, ks, 0, n_seq, length)
    z1 = _dft_inv(tabs, yr, ys, pc, 0, pc, 1, skip, 0, n_seq, length, F32)[None]
    yr, ys = _dft_fwd(tabs, z1, 0, kr, ks, 1, n_seq, length)
    return _dft_inv(tabs, yr, ys, z1, 0, pc, 2, skip, 1, n_seq, length, BF16)


HY_N1, HY_N2 = 64, 128


def _hyena_mix_pair(pc, filt_p, filt_m, skip):
    assert DEC_BATCH == 2 and HY_N1 * HY_N2 == 2 * DEC_SEQ and ROWS % DEC_SEQ == 0
    n1, n2, rows_in = HY_N1, HY_N2, HY_N1 // 2
    t = _ct_tables(n1, n2)
    c_in, s_in = t["c1"][:, :rows_in], t["s1"][:, :rows_in]
    fa = _kron_sub(jnp.block([[c_in, s_in], [-s_in, c_in]])).astype(BF16)
    fb = jnp.block([[t["c2"], t["s2"]], [-t["s2"], t["c2"]]]).astype(BF16)
    fbi = jnp.block([[t["c2"], -t["s2"]], [t["s2"], t["c2"]]]).astype(BF16)
    mirror = lambda shift: (-jnp.arange(n2) - shift) % n2
    fbp0, fbp1 = [jnp.concatenate([t["c2"][mirror(sh)], t["s2"][mirror(sh)]], axis=1).astype(BF16)
                  for sh in (0, 1)]
    c_out, s_out = t["c1"][:rows_in], t["s1"][:rows_in]
    fai = _kron_sub(jnp.block([[c_out, -s_out], [s_out, c_out]])).astype(BF16)
    n_filt = HY_ORDER * D_MODEL
    filt_view = lambda f: (f.reshape(rows_in, n2, n_filt), lambda s: ())
    ur, ui = _ct_stage_a([filt_view(filt_p), filt_view(filt_m)], 1, fa, t["cw_a"], t["sw_a"])

    pc = pc.reshape(3, ROWS // DEC_SEQ, rows_in, n2, D_MODEL)
    z, z0, out_dtypes = pc, ROWS_CTX // DEC_SEQ, (F32, BF16)
    for order in range(HY_ORDER):
        br, bi = _ct_stage_a([(z, lambda s, z0=z0: (0, z0)), (z, lambda s, z0=z0: (0, z0 + 1))], 1, fa,
                             t["cw_a"], t["sw_a"])
        vr, vi = _ct_mid(br[0], bi[0], ur[0], ui[0], order, fb, fbi, fbp0, fbp1, t["cw_b"], t["sw_b"])
        z = _ct_inv_a(vr, vi, fai, z, (0, z0), pc, (1 + order, ROWS_CTX // DEC_SEQ), skip, order,
                      out_dtypes[order])[None]
        z0 = 0
    return z.reshape(ROWS_LAT, D_MODEL)


def _fnet_chan_kernel(h_ref, sh_ref, sc_ref, w_ref, p_ref, q_ref):
    u = _modulate(h_ref[...], sh_ref[...], sc_ref[...]).astype(BF16)
    w = w_ref[...]
    for g in range(FNET_GROUPS):
        cols = slice(g * FNET_CG, (g + 1) * FNET_CG)
        r = _dot(u[:, cols], w)
        p_ref[:, cols] = r[:, :FNET_CG].astype(BF16)
        q_ref[:, cols] = r[:, FNET_CG:].astype(BF16)


def _fnet_chan(h, mods, w_cs, tm=512):
    row = pl.BlockSpec((tm, D_MODEL), lambda i: (i, 0))
    out = jax.ShapeDtypeStruct((ROWS, D_MODEL), BF16)
    return pl.pallas_call(
        _fnet_chan_kernel,
        grid=(ROWS // tm,),
        in_specs=[row, _mod_spec(0, tm), _mod_spec(1, tm),
                  pl.BlockSpec((FNET_CG, 2 * FNET_CG), lambda i: (0, 0))],
        out_specs=[row, row],
        out_shape=[out, out],
        compiler_params=_params(("parallel",),
                                _vmem_limit(3 * _nbytes((tm, D_MODEL), F32),
                                            temps=2 * _nbytes((tm, D_MODEL), F32))),
        name="fnet_chan",
    )(h, mods, mods, w_cs)


def _fnet_pos_kernel(scale, c_ref, ns_ref, p_ref, q_ref, o_ref, acc_ref):
    k = pl.program_id(3)

    @pl.when(k == 0)
    def _():
        acc_ref[...] = jnp.zeros_like(acc_ref)

    acc_ref[...] += (_dot(c_ref[...], p_ref[...].astype(BF16))
                     + _dot(ns_ref[...], q_ref[...].astype(BF16)))

    @pl.when(k == pl.num_programs(3) - 1)
    def _():
        o_ref[...] = (acc_ref[...] * scale).astype(o_ref.dtype)


def _fnet_pos(c_tab, ns_tab, p, q, n_seq, length):
    tt, tn, tk = _seq_tiles(length)
    nb, nt, nk = D_MODEL // tn, length // tt, length // tk
    scale = (length * FNET_CG) ** -0.5
    return pl.pallas_call(
        functools.partial(_fnet_pos_kernel, scale),
        grid=(n_seq, nt, nb, nk),
        in_specs=[pl.BlockSpec((tt, tk), lambda s, t, n, k: (t, k)),
                  pl.BlockSpec((tt, tk), lambda s, t, n, k: (t, k)),
                  pl.BlockSpec((tk, tn), lambda s, t, n, k: (s * nk + k, n)),
                  pl.BlockSpec((tk, tn), lambda s, t, n, k: (s * nk + k, n))],
        out_specs=pl.BlockSpec((tt, tn), lambda s, t, n, k: (s * nt + t, n)),
        out_shape=jax.ShapeDtypeStruct((n_seq * length, D_MODEL), BF16),
        scratch_shapes=[pltpu.VMEM((tt, tn), F32)],
        compiler_params=_params(("parallel", "parallel", "parallel", "arbitrary"),
                                _vmem_limit(2 * _nbytes((tt, tk), BF16) + 2 * _nbytes((tk, tn), BF16)
                                            + _nbytes((tt, tn), BF16),
                                            resident=_nbytes((tt, tn), F32),
                                            temps=2 * _nbytes((tt, tn), F32))),
        name="fnet_pos",
    )(c_tab, ns_tab, p, q)


FN_N1, FN_N2 = 32, 128


def _fnet_pos_factored(p, q):
    assert FN_N1 * FN_N2 == DEC_SEQ and ROWS % DEC_SEQ == 0
    n1, n2 = FN_N1, FN_N2
    t = _ct_tables(n1, n2)
    fa = _kron_sub(jnp.block([[t["c1"], -t["s1"]], [-t["s1"], -t["c1"]]])).astype(BF16)
    fb_re = jnp.concatenate([t["c2"], t["s2"]], axis=1).astype(BF16)
    lat0 = ROWS_CTX // DEC_SEQ
    view = lambda x: x.reshape(ROWS // DEC_SEQ, n1, n2, D_MODEL)
    seq = lambda s: (lat0 + s,)
    br, bi = _ct_stage_a([(view(p), seq), (view(q), seq)], DEC_BATCH, fa, t["cw_a"], t["sw_a"])
    f = _ct_real_b(br, bi, fb_re, (DEC_SEQ * FNET_CG) ** -0.5)
    return f.reshape(ROWS_LAT, D_MODEL)


def _rope_table():
    rows = DEC_SEQ // GRID_W
    row = jnp.repeat(jnp.arange(rows), GRID_W).astype(F32)
    col = jnp.tile(jnp.arange(GRID_W), rows).astype(F32)
    inv = ROPE_THETA ** (-jnp.arange(0, AXIS_ROPE, 2, dtype=F32) / AXIS_ROPE)
    ang = jnp.concatenate([row[:, None] * inv, col[:, None] * inv], axis=-1)
    cos = jnp.repeat(jnp.cos(ang), 2, axis=-1)
    sin = jnp.repeat(jnp.sin(ang), 2, axis=-1)
    lat = jnp.tile(jnp.concatenate([cos, sin], axis=-1), (DEC_BATCH, 1))
    ctx = jnp.concatenate([jnp.ones((ROWS_CTX, QK_ROPE), F32), jnp.zeros((ROWS_CTX, QK_ROPE), F32)],
                          axis=-1)
    return jnp.concatenate([ctx, lat], axis=0)


def _pair_rotated(w):
    pairs = w.reshape(w.shape[:-1] + (QK_ROPE // 2, 2))
    return jnp.stack([-pairs[..., 1], pairs[..., 0]], axis=-1).reshape(w.shape)


def kernel(x_prompt, x_sample, c, cache_ckv, cache_krope, c_ctx, ada_w, ada_b, ln_g, ln_b, ffn_w_gate, ffn_w_up, ffn_w_down, mla_w_dq, mla_q_norm, mla_w_uq, mla_w_dkv, mla_kv_norm, mla_w_kr, mla_w_ukv, mla_w_o, hy_w_in, hy_b_in, hy_conv_w, hy_conv_b, hy_f_w1, hy_f_b1, hy_f_freq1, hy_f_w2, hy_f_b2, hy_f_freq2, hy_f_w3, hy_log_decay, hy_skip, hy_w_out, hy_b_out, fn_w_out, fn_b_out):
    assert x_prompt.shape == (BATCH, SEQ, D_MODEL) and x_sample.shape == (DEC_BATCH, DEC_SEQ, D_MODEL)
    assert ROWS_CTX % DEC_SEQ == 0 and SEQ == FNET_CG

    assert N_MIXERS > 0 and DEPTH > 1
    h = (x_prompt.reshape(ROWS_CTX, D_MODEL), x_sample.reshape(ROWS_LAT, D_MODEL))
    cond = jnp.concatenate([c_ctx[None, :], c, jnp.zeros((COND_PAD - N_COND, D_MODEL), F32)])
    mods_all = _modulation_vectors(cond, ada_w, ada_b)
    zero_bias = jnp.zeros((D_MODEL,), F32)
    ffn_w = (ffn_w_gate.astype(BF16), ffn_w_up.astype(BF16), ffn_w_down)
    rope_tab = None
    ckv_states, krope_states = [], []

    for i in range(DEPTH):
        kind, j = i % N_MIXERS, i // N_MIXERS
        mods = mods_all[i]
        if kind == 0:
            if rope_tab is None:
                rope_tab = _rope_table()
            w_kr2 = jnp.concatenate([mla_w_kr[j], _pair_rotated(mla_w_kr[j])], axis=-1).astype(BF16)
            wq = mla_w_uq[j].reshape(Q_RANK, MLA_HEADS, QK_NOPE + QK_ROPE)
            w_q = jnp.concatenate([wq, _pair_rotated(wq[..., QK_NOPE:])], axis=-1)
            w_q = w_q.reshape(Q_RANK, MLA_HEADS * HEAD_W).astype(BF16)
            w_ukv = mla_w_ukv[j].reshape(KV_RANK, MLA_HEADS, QK_NOPE + V_DIM)
            w_k = w_ukv[..., :QK_NOPE].reshape(KV_RANK, MLA_HEADS * QK_NOPE).astype(BF16)
            w_vt = w_ukv[..., QK_NOPE:].reshape(KV_RANK, MLA_HEADS * V_DIM).T.astype(BF16)
            cq, ckv, kr, kr2 = _mla_down(h, mods, mla_w_dq[j].astype(BF16), mla_w_dkv[j].astype(BF16),
                                         w_kr2, mla_q_norm[j], mla_kv_norm[j], rope_tab)
            ckv_states.append(ckv[:ROWS_CTX].reshape(BATCH, SEQ, KV_RANK))
            krope_states.append(kr[:ROWS_CTX].reshape(BATCH, SEQ, QK_ROPE))
            q = _q_up(cq, w_q, rope_tab)
            k_tok, vt_tok = _kv_expand(ckv, kr2, w_k, w_vt)
            kc = cache_krope[:, j].reshape(DEC_BATCH * PAST_LEN, QK_ROPE).astype(BF16)
            k_cache, vt_cache = _kv_expand(cache_ckv[:, j].reshape(DEC_BATCH * PAST_LEN, KV_RANK),
                                           jnp.concatenate([kc, kc], axis=-1), w_k, w_vt)
            o = _attention(q, k_tok, vt_tok, k_cache, vt_cache)
            h = _mm_postnorm(o, mla_w_o[j].astype(BF16), zero_bias, h, mods, 2, ln_g[i, 0], ln_b[i, 0])
        elif kind == 1:
            pc = _hyena_in(h, mods, hy_w_in[j].astype(BF16), hy_b_in[j], hy_conv_w[j], hy_conv_b[j])
            fp = (hy_f_w1[j], hy_f_b1[j], hy_f_freq1[j], hy_f_w2[j], hy_f_b2[j], hy_f_freq2[j],
                  hy_f_w3[j], hy_log_decay[j])
            skip = hy_skip[j].reshape(HY_ORDER, 1, D_MODEL)
            z_ctx = _hyena_mix_dense(pc, *_hyena_filters(SEQ, *fp), skip, BATCH, SEQ)
            z_lat = _hyena_mix_pair(pc, *_hyena_filters(DEC_SEQ, *fp), skip)
            h = _mm_postnorm((z_ctx, z_lat), hy_w_out[j].astype(BF16), hy_b_out[j],
                             h, mods, 2, ln_g[i, 0], ln_b[i, 0])
        else:
            c_ch, ns_ch = _dft_tables(FNET_CG)
            p, q = _fnet_chan(h, mods, jnp.concatenate([c_ch, -ns_ch], axis=-1))
            f_ctx = _fnet_pos(c_ch, ns_ch, p, q, BATCH, SEQ)
            f_lat = _fnet_pos_factored(p, q)
            h = _mm_postnorm((f_ctx, f_lat), fn_w_out[j].astype(BF16), fn_b_out[j],
                             h, mods, 2, ln_g[i, 0], ln_b[i, 0])
        h = _ffn(h, mods, *ffn_w, i, ln_g[i, 1], ln_b[i, 1], split_out=i == DEPTH - 1)

    y_prompt = h[0].reshape(BATCH, SEQ, D_MODEL)
    y_sample = h[1].reshape(DEC_BATCH, DEC_SEQ, D_MODEL)
    return (y_prompt, y_sample, jnp.stack(ckv_states, axis=1), jnp.stack(krope_states, axis=1))
```

```python
import functools
import math

import jax
import jax.numpy as jnp
from jax import lax
from jax.experimental import pallas as pl
from jax.experimental.pallas import tpu as pltpu

F32 = jnp.float32
BF16 = jnp.bfloat16

D_MODEL = 2048
BATCH = 16
SEQ = 256
DEPTH = 4
DEC_BATCH = 2
DEC_SEQ = 4096
PAST_LEN = 512
GRID_W = 64
N_MIXERS = 3
MLA_HEADS = 16
QK_NOPE = 128
QK_ROPE = 64
V_DIM = 128
Q_RANK = 512
KV_RANK = 512
ROPE_THETA = 10000.0
AXIS_ROPE = QK_ROPE // 2
HY_ORDER = 2
HY_DIRS = 2
HY_CONV = 3
HY_BANDS = 16
HY_EMB = 1 + 2 * HY_BANDS
HY_FW = 64
HY_SHIFT = 0.05
FNET_GROUPS = 8
FNET_CG = D_MODEL // FNET_GROUPS
D_FF = -(-8 * D_MODEL // (3 * 256)) * 256
DN_ALPHA = (2 * DEPTH) ** 0.25
LN_EPS = 1e-5
RMS_EPS = 1e-6
N_MOD = 6

ROWS_CTX = BATCH * SEQ
ROWS_LAT = DEC_BATCH * DEC_SEQ
ROWS = ROWS_CTX + ROWS_LAT
N_COND = 1 + DEC_BATCH
COND_PAD = 8
HEAD_W = QK_NOPE + 2 * QK_ROPE
ATT_SCALE = (QK_NOPE + QK_ROPE) ** -0.5

V7X_VMEM_BYTES = 64 * 2 ** 20
VMEM_CAP_BYTES = V7X_VMEM_BYTES * 7 // 8
VMEM_FLOOR_BYTES = 32 * 2 ** 20


def _vmem_limit(pipelined, resident=0, temps=0):
    est = 2 * pipelined + resident + temps
    return int(min(max(est, VMEM_FLOOR_BYTES), VMEM_CAP_BYTES))


def _params(semantics, vmem):
    return pltpu.CompilerParams(dimension_semantics=semantics, vmem_limit_bytes=vmem)


def _nbytes(shape, dtype):
    return math.prod(shape) * jnp.dtype(dtype).itemsize


def _group_of_tile(i, tm):
    n_ctx = ROWS_CTX // tm
    return jnp.where(i < n_ctx, 0, 1 + (i - n_ctx) // (DEC_SEQ // tm))


def _mod_spec(which, tm):
    return pl.BlockSpec((None, 1, D_MODEL),
                        lambda i, *_: (which * COND_PAD + _group_of_tile(i, tm), 0, 0))


def _row_spec(width=D_MODEL):
    return pl.BlockSpec((1, width), lambda *_: (0, 0))


def _as_tuple(x):
    return x if isinstance(x, tuple) else (x,)


def _token_specs(xs, tm):
    width = xs[0].shape[1]
    if len(xs) == 1:
        return [pl.BlockSpec((tm, width), lambda i, *_: (i, 0))]
    n_ctx = ROWS_CTX // tm
    return [pl.BlockSpec((tm, width), lambda i, *_: (jnp.minimum(i, n_ctx - 1), 0)),
            pl.BlockSpec((tm, width), lambda i, *_: (jnp.maximum(i - n_ctx, 0), 0))]


def _token_rows(refs, r):
    if len(refs) == 1:
        return refs[0][r, :]
    n_ctx = ROWS_CTX // refs[0].shape[0]
    return jnp.where(pl.program_id(0) < n_ctx, refs[0][r, :], refs[1][r, :])


def _modulate(h, shift, scale):
    return h * (1.0 + scale) + shift


def _post_norm(h, delta, g, b):
    z = DN_ALPHA * h + delta
    mu = jnp.mean(z, axis=-1, keepdims=True)
    zc = z - mu
    var = jnp.mean(zc * zc, axis=-1, keepdims=True)
    return zc * lax.rsqrt(var + LN_EPS) * g + b


def _rms_norm(x, g):
    ms = jnp.mean(x * x, axis=-1, keepdims=True)
    return x * lax.rsqrt(ms + RMS_EPS) * g


def _dot(a, b):
    return jnp.dot(a, b, preferred_element_type=F32)


def _modvec_kernel(c_ref, w_ref, b_ref, o_ref):
    a = jax.nn.silu(c_ref[...]).astype(BF16)
    o_ref[...] = _dot(a, w_ref[...].astype(BF16)) + b_ref[...]


def _modulation_vectors(cond, ada_w, ada_b):
    tn = 1024
    n = N_MOD * D_MODEL
    out = pl.pallas_call(
        _modvec_kernel,
        grid=(DEPTH, n // tn),
        in_specs=[pl.BlockSpec((COND_PAD, D_MODEL), lambda l, j: (0, 0)),
                  pl.BlockSpec((None, D_MODEL, tn), lambda l, j: (l, 0, j)),
                  pl.BlockSpec((None, 1, tn), lambda l, j: (l, 0, j))],
        out_specs=pl.BlockSpec((None, COND_PAD, tn), lambda l, j: (l, 0, j)),
        out_shape=jax.ShapeDtypeStruct((DEPTH, COND_PAD, n), F32),
        compiler_params=_params(("parallel", "parallel"),
                                _vmem_limit(_nbytes((D_MODEL, tn), F32),
                                            temps=_nbytes((D_MODEL, tn), BF16))),
        name="modvec",
    )(cond, ada_w, ada_b.reshape(DEPTH, 1, n))
    out = out.reshape(DEPTH, COND_PAD, N_MOD, D_MODEL).transpose(0, 2, 1, 3)
    return out.reshape(DEPTH, N_MOD * COND_PAD, 1, D_MODEL)


def _hyena_in_kernel(n_ctx, h_ref, hp_ref, hn_ref, sh_ref, sc_ref, w_ref, b_ref, cw_ref, cb_ref,
                     o_ref, u_ref):
    i = pl.program_id(0)
    tm = h_ref.shape[0]

    @pl.when(pl.program_id(1) == 0)
    def _():
        mod = lambda x_ref: _modulate(x_ref[...], sh_ref[...], sc_ref[...]).astype(BF16)
        u_ref[:SUB, :] = mod(hp_ref)
        u_ref[SUB:SUB + tm, :] = mod(h_ref)
        u_ref[SUB + tm:, :] = mod(hn_ref)

    length = jnp.where(i < n_ctx, SEQ, DEC_SEQ)
    half = tm // 2
    starts = (0, half)
    ys = [_dot(u_ref[r0:r0 + half + 2 * SUB, :], w_ref[...]) + b_ref[...] for r0 in starts]
    for r0, y in zip(starts, ys):
        inner = slice(SUB, SUB + half)
        before = pltpu.roll(y, 1, 0)[inner]
        after = pltpu.roll(y, y.shape[0] - 1, 0)[inner]
        pos = (lax.broadcasted_iota(jnp.int32, before.shape, 0) + (i * tm + r0)) & (length - 1)
        before = jnp.where(pos == 0, 0.0, before)
        after = jnp.where(pos == length - 1, 0.0, after)
        o_ref[r0:r0 + half, :] = (cw_ref[0:1, :] * before + cw_ref[1:2, :] * y[inner]
                                  + cw_ref[2:3, :] * after + cb_ref[...])


def _hyena_in(h, mods, w, b, conv_w, conv_b, tm=1024, tn=1024):
    assert SEQ & (SEQ - 1) == 0 and DEC_SEQ & (DEC_SEQ - 1) == 0 and tm % SEQ == 0 and DEC_SEQ % tm == 0
    k, n = w.shape
    per, halo, last = D_MODEL // tn, tm // SUB, ROWS // SUB - 1
    return pl.pallas_call(
        functools.partial(_hyena_in_kernel, ROWS_CTX // tm),
        grid=(ROWS // tm, n // tn),
        in_specs=[pl.BlockSpec((tm, k), lambda i, j: (i, 0)),
                  pl.BlockSpec((SUB, k), lambda i, j: (jnp.maximum(i * halo - 1, 0), 0)),
                  pl.BlockSpec((SUB, k), lambda i, j: (jnp.minimum((i + 1) * halo, last), 0)),
                  _mod_spec(0, tm), _mod_spec(1, tm),
                  pl.BlockSpec((k, tn), lambda i, j: (0, j)),
                  pl.BlockSpec((1, tn), lambda i, j: (0, j)),
                  pl.BlockSpec((HY_CONV, tn), lambda i, j: (0, j)),
                  pl.BlockSpec((1, tn), lambda i, j: (0, j))],
        out_specs=pl.BlockSpec((None, tm, tn), lambda i, j: (j // per, i, j % per)),
        out_shape=jax.ShapeDtypeStruct((n // D_MODEL, ROWS, D_MODEL), F32),
        scratch_shapes=[pltpu.VMEM((tm + 2 * SUB, k), BF16)],
        compiler_params=_params(("parallel", "arbitrary"),
                                _vmem_limit(_nbytes((tm, k), F32) + _nbytes((k, tn), BF16)
                                            + _nbytes((tm, tn), F32),
                                            resident=_nbytes((tm, k), BF16),
                                            temps=5 * _nbytes((tm, tn), F32))),
        name="hyena_in",
    )(h, h, h, mods, mods, w, b.reshape(1, n), conv_w, conv_b.reshape(1, n))


def _mm_postnorm_kernel(n_a, n_h, *refs):
    a_refs, refs = refs[:n_a], refs[n_a:]
    (w_ref, bias_ref), refs = refs[:2], refs[2:]
    h_refs, (gate_ref, g_ref, b_ref, o_ref) = refs[:n_h], refs[n_h:]
    half = o_ref.shape[0] // 2
    halves = (slice(0, half), slice(half, 2 * half))
    ys = [_dot(_token_rows(a_refs, r), w_ref[...]) + bias_ref[...] for r in halves]
    for r, y in zip(halves, ys):
        o_ref[r, :] = _post_norm(_token_rows(h_refs, r), gate_ref[...] * y, g_ref[...], b_ref[...])


def _mm_postnorm(a, w, bias, h, mods, which_gate, ln_g, ln_b, tm=512):
    a, h = _as_tuple(a), _as_tuple(h)
    k = a[0].shape[1]
    return pl.pallas_call(
        functools.partial(_mm_postnorm_kernel, len(a), len(h)),
        grid=(ROWS // tm,),
        in_specs=(_token_specs(a, tm) + [pl.BlockSpec((k, D_MODEL), lambda i: (0, 0)), _row_spec()]
                  + _token_specs(h, tm) + [_mod_spec(which_gate, tm), _row_spec(), _row_spec()]),
        out_specs=pl.BlockSpec((tm, D_MODEL), lambda i: (i, 0)),
        out_shape=jax.ShapeDtypeStruct((ROWS, D_MODEL), F32),
        compiler_params=_params(("parallel",),
                                _vmem_limit(len(a) * _nbytes((tm, k), BF16)
                                            + _nbytes((k, D_MODEL), BF16)
                                            + (1 + len(h)) * _nbytes((tm, D_MODEL), F32),
                                            temps=3 * _nbytes((tm, D_MODEL), F32))),
        name="mm_postnorm",
    )(*a, w, bias.reshape(1, D_MODEL), *h, mods, ln_g.reshape(1, D_MODEL), ln_b.reshape(1, D_MODEL))


def _ffn_kernel(n_ctx, h_ref, sh_ref, sc_ref, gate_ref, g_ref, b_ref, wg_ref, wu_ref, wd_ref,
                *refs):
    i, f = pl.program_id(0), pl.program_id(1)
    if n_ctx is None:
        o_ref, u_ref = refs
        acc_ref, outs = o_ref, ((o_ref, None),)
    else:
        octx_ref, olat_ref, u_ref, acc_ref = refs
        outs = ((octx_ref, i < n_ctx), (olat_ref, i >= n_ctx))

    half = u_ref.shape[0] // 2
    halves = (slice(0, half), slice(half, 2 * half))

    def step(first, o_ref):
        if first:
            us = [_modulate(h_ref[r, :], sh_ref[...], sc_ref[...]).astype(BF16) for r in halves]
            for r, u in zip(halves, us):
                u_ref[r, :] = u
        else:
            us = [u_ref[r, :] for r in halves]
        proj = [(_dot(u, wg_ref[...]), _dot(u, wu_ref[...])) for u in us]
        wd = wd_ref[...].astype(BF16)
        for r, (gate, up) in zip(halves, proj):
            act = (jax.nn.silu(gate) * up).astype(BF16)
            total = _dot(act, wd)
            if not first:
                total = acc_ref[r, :] + total
            if o_ref is None:
                acc_ref[r, :] = total
            else:
                o_ref[r, :] = _post_norm(h_ref[r, :], gate_ref[...] * total, g_ref[...], b_ref[...])

    last = f == pl.num_programs(1) - 1
    pl.when(f == 0)(functools.partial(step, True, None))
    pl.when(jnp.logical_and(f > 0, jnp.logical_not(last)))(functools.partial(step, False, None))
    for o_ref, mine in outs:
        pl.when(last if mine is None else jnp.logical_and(last, mine))(
            functools.partial(step, False, o_ref))


def _ffn(h, mods, w_gate, w_up, w_down, layer, ln_g, ln_b, split_out, tm=512, tf=512):
    assert D_FF // tf > 1
    tile = _nbytes((tm, D_MODEL), F32)
    if split_out:
        n_ctx = ROWS_CTX // tm
        out_specs = [pl.BlockSpec((tm, D_MODEL), lambda i, f: (jnp.minimum(i, n_ctx - 1), 0)),
                     pl.BlockSpec((tm, D_MODEL), lambda i, f: (jnp.maximum(i - n_ctx, 0), 0))]
        out_shape = [jax.ShapeDtypeStruct((ROWS_CTX, D_MODEL), F32),
                     jax.ShapeDtypeStruct((ROWS_LAT, D_MODEL), F32)]
        scratch = [pltpu.VMEM((tm, D_MODEL), BF16), pltpu.VMEM((tm, D_MODEL), F32)]
        pipelined, resident = 3 * tile, tile + tile // 2
    else:
        n_ctx = None
        out_specs = pl.BlockSpec((tm, D_MODEL), lambda i, f: (i, 0))
        out_shape = jax.ShapeDtypeStruct((ROWS, D_MODEL), F32)
        scratch = [pltpu.VMEM((tm, D_MODEL), BF16)]
        pipelined, resident = 2 * tile, tile // 2
    return pl.pallas_call(
        functools.partial(_ffn_kernel, n_ctx),
        grid=(ROWS // tm, D_FF // tf),
        in_specs=[pl.BlockSpec((tm, D_MODEL), lambda i, f: (i, 0)),
                  _mod_spec(3, tm), _mod_spec(4, tm), _mod_spec(5, tm),
                  _row_spec(), _row_spec(),
                  pl.BlockSpec((None, D_MODEL, tf), lambda i, f: (layer, 0, f)),
                  pl.BlockSpec((None, D_MODEL, tf), lambda i, f: (layer, 0, f)),
                  pl.BlockSpec((None, tf, D_MODEL), lambda i, f: (layer, f, 0))],
        out_specs=out_specs,
        out_shape=out_shape,
        scratch_shapes=scratch,
        compiler_params=_params(("arbitrary" if split_out else "parallel", "arbitrary"),
                                _vmem_limit(pipelined + 2 * _nbytes((D_MODEL, tf), BF16)
                                            + _nbytes((tf, D_MODEL), w_down.dtype),
                                            resident=resident, temps=2 * tile)),
        name="ffn",
    )(h, mods, mods, mods, ln_g.reshape(1, D_MODEL), ln_b.reshape(1, D_MODEL),
      w_gate, w_up, w_down)


def _mla_down_kernel(n_h, *refs):
    h_refs, (sh_ref, sc_ref, wdq_ref, wdkv_ref, wkr_ref, qn_ref, kvn_ref, rope_ref,
             cq_ref, ckv_ref, kr_ref, kr2_ref) = refs[:n_h], refs[n_h:]
    half = cq_ref.shape[0] // 2
    halves = (slice(0, half), slice(half, 2 * half))
    us = [_modulate(_token_rows(h_refs, r), sh_ref[...], sc_ref[...]).astype(BF16) for r in halves]
    projs = [(_dot(u, wdq_ref[...]), _dot(u, wdkv_ref[...]), _dot(u, wkr_ref[...])) for u in us]
    for r, (q_lat, kv_lat, t) in zip(halves, projs):
        cq_ref[r, :] = _rms_norm(q_lat, qn_ref[...]).astype(BF16)
        ckv_ref[r, :] = _rms_norm(kv_lat, kvn_ref[...])
        kr_ref[r, :] = t[:, :QK_ROPE]
        v = t * rope_ref[r, :]
        kr2_ref[r, :] = (v + pltpu.roll(v, QK_ROPE, 1)).astype(BF16)


def _mla_down(h, mods, w_dq, w_dkv, w_kr2, q_norm, kv_norm, rope_tab, tm=512):
    h = _as_tuple(h)
    row = lambda width: pl.BlockSpec((tm, width), lambda i: (i, 0))
    full = lambda shape: pl.BlockSpec(shape, lambda i: (0, 0))
    return pl.pallas_call(
        functools.partial(_mla_down_kernel, len(h)),
        grid=(ROWS // tm,),
        in_specs=_token_specs(h, tm) + [
            _mod_spec(0, tm), _mod_spec(1, tm),
            full((D_MODEL, Q_RANK)), full((D_MODEL, KV_RANK)), full((D_MODEL, 2 * QK_ROPE)),
            _row_spec(Q_RANK), _row_spec(KV_RANK), row(2 * QK_ROPE)],
        out_specs=[row(Q_RANK), row(KV_RANK), row(QK_ROPE), row(2 * QK_ROPE)],
        out_shape=[jax.ShapeDtypeStruct((ROWS, Q_RANK), BF16),
                   jax.ShapeDtypeStruct((ROWS, KV_RANK), F32),
                   jax.ShapeDtypeStruct((ROWS, QK_ROPE), F32),
                   jax.ShapeDtypeStruct((ROWS, 2 * QK_ROPE), BF16)],
        compiler_params=_params(("parallel",),
                                _vmem_limit(len(h) * _nbytes((tm, D_MODEL), F32)
                                            + _nbytes((D_MODEL, Q_RANK + KV_RANK + 2 * QK_ROPE), BF16)
                                            + 3 * _nbytes((tm, KV_RANK), F32),
                                            temps=2 * _nbytes((tm, D_MODEL), F32))),
        name="mla_down",
    )(*h, mods, mods, w_dq, w_dkv, w_kr2, q_norm.reshape(1, Q_RANK), kv_norm.reshape(1, KV_RANK),
      rope_tab)


NT_DIMS = (((1,), (1,)), ((), ()))


def _q_up_kernel(cq_ref, w_ref, rope_ref, q_ref):
    cq, tab = cq_ref[...], rope_ref[...] * LOG2E_SCALE
    for h in range(MLA_HEADS):
        r = _dot(cq, w_ref[:, h * HEAD_W:(h + 1) * HEAD_W])
        q_ref[h, :, :QK_NOPE] = (r[:, :QK_NOPE] * LOG2E_SCALE).astype(BF16)
        q_ref[h, :, QK_NOPE:] = (r[:, QK_NOPE:] * tab).astype(BF16)


def _q_up(cq, w_q, rope_tab, tm=512):
    return pl.pallas_call(
        _q_up_kernel,
        grid=(ROWS // tm,),
        in_specs=[pl.BlockSpec((tm, Q_RANK), lambda i: (i, 0)),
                  pl.BlockSpec((Q_RANK, MLA_HEADS * HEAD_W), lambda i: (0, 0)),
                  pl.BlockSpec((tm, 2 * QK_ROPE), lambda i: (i, 0))],
        out_specs=pl.BlockSpec((MLA_HEADS, tm, HEAD_W), lambda i: (0, i, 0)),
        out_shape=jax.ShapeDtypeStruct((MLA_HEADS, ROWS, HEAD_W), BF16),
        compiler_params=_params(("parallel",),
                                _vmem_limit(_nbytes((Q_RANK + tm, MLA_HEADS * HEAD_W), BF16))),
        name="q_up",
    )(cq, w_q, rope_tab)


VT_PAD = 16
VT_ROWS = V_DIM + VT_PAD


def _kv_expand_kernel(ckv_ref, kr2_ref, wk_ref, wvt_ref, k_ref, vt_ref):
    c = ckv_ref[...].astype(BF16)
    vt = lax.dot_general(wvt_ref[...], c, NT_DIMS, preferred_element_type=F32).astype(BF16)
    ones = jnp.ones((VT_PAD, vt.shape[1]), BF16)
    for h in range(MLA_HEADS):
        vt_ref[h * VT_ROWS:h * VT_ROWS + V_DIM, :] = vt[h * V_DIM:(h + 1) * V_DIM]
        vt_ref[h * VT_ROWS + V_DIM:(h + 1) * VT_ROWS, :] = ones
    kr2 = kr2_ref[...]
    pair_w = 2 * QK_NOPE
    for g in range(MLA_HEADS // 2):
        r = _dot(c, wk_ref[:, g * pair_w:(g + 1) * pair_w]).astype(BF16)
        for e in range(2):
            k_ref[2 * g + e, :, :QK_NOPE] = r[:, e * QK_NOPE:(e + 1) * QK_NOPE]
            k_ref[2 * g + e, :, QK_NOPE:] = kr2


def _kv_expand(ckv, kr2, w_k, w_vt, tm=512):
    rows = ckv.shape[0]
    return pl.pallas_call(
        _kv_expand_kernel,
        grid=(rows // tm,),
        in_specs=[pl.BlockSpec((tm, KV_RANK), lambda i: (i, 0)),
                  pl.BlockSpec((tm, 2 * QK_ROPE), lambda i: (i, 0)),
                  pl.BlockSpec((KV_RANK, MLA_HEADS * QK_NOPE), lambda i: (0, 0)),
                  pl.BlockSpec((MLA_HEADS * V_DIM, KV_RANK), lambda i: (0, 0))],
        out_specs=[pl.BlockSpec((MLA_HEADS, tm, HEAD_W), lambda i: (0, i, 0)),
                   pl.BlockSpec((MLA_HEADS * VT_ROWS, tm), lambda i: (0, i))],
        out_shape=[jax.ShapeDtypeStruct((MLA_HEADS, rows, HEAD_W), BF16),
                   jax.ShapeDtypeStruct((MLA_HEADS * VT_ROWS, rows), BF16)],
        compiler_params=_params(("parallel",),
                                _vmem_limit(_nbytes((tm, MLA_HEADS * (HEAD_W + V_DIM)), BF16)
                                            + 2 * _nbytes((KV_RANK, MLA_HEADS * V_DIM), BF16),
                                            temps=_nbytes((MLA_HEADS * V_DIM, tm), F32))),
        name="kv_expand",
    )(ckv, kr2, w_k, w_vt)


ATT_CHUNK = 512
ATT_SKEW = 4
LOG2E_SCALE = ATT_SCALE * math.log2(math.e)


def _attn_scores(q, k):
    return lax.dot_general(k, q, NT_DIMS, preferred_element_type=F32)


def _attn_values(s, vt, carry):
    m = jnp.max(s, axis=0, keepdims=True)
    if carry is not None:
        m_old, acc_old = carry
        m = jnp.maximum(m_old, m)
    acc = _dot(vt, jnp.exp2(s - m).astype(BF16))
    if carry is not None:
        acc = jnp.exp2(m_old - m) * acc_old + acc
    return m, acc


def _attn_output(acc):
    return (acc[:V_DIM] / acc[V_DIM:V_DIM + 1]).T.astype(BF16)


def _attn_ctx_kernel(q_ref, k_ref, vt_ref, o_ref):
    for h in range(MLA_HEADS):
        _, acc = _attn_values(_attn_scores(q_ref[h], k_ref[h]),
                              vt_ref[h * VT_ROWS:(h + 1) * VT_ROWS, :], None)
        o_ref[:, h * V_DIM:(h + 1) * V_DIM] = _attn_output(acc)


def _attn_lat_kernel(q_ref, k_ref, vt_ref, kc_ref, vtc_ref, prev_ref, o_ref):
    del prev_ref
    q = q_ref[...]
    n_tok = DEC_SEQ // ATT_CHUNK
    rows = lambda c: slice(c * ATT_CHUNK, (c + 1) * ATT_CHUNK)
    keys = [k_ref.at[rows(c), :] for c in range(n_tok)] + [kc_ref]
    vals = [vt_ref.at[:, rows(c)] for c in range(n_tok)] + [vtc_ref]
    carry = None
    scores = [_attn_scores(q, keys[c][...]) for c in range(ATT_SKEW)]
    for c in range(n_tok + 1):
        if c + ATT_SKEW <= n_tok:
            scores.append(_attn_scores(q, keys[c + ATT_SKEW][...]))
        carry = _attn_values(scores[c], vals[c][...], carry)
    o_ref[...] = _attn_output(carry[1])


def _attention(q, k_tok, vt_tok, k_cache, vt_cache, tq=2048):
    assert DEC_SEQ % ATT_CHUNK == 0
    out_shape = jax.ShapeDtypeStruct((ROWS, MLA_HEADS * V_DIM), BF16)
    o = pl.pallas_call(
        _attn_ctx_kernel,
        grid=(BATCH,),
        in_specs=[pl.BlockSpec((MLA_HEADS, SEQ, HEAD_W), lambda s: (0, s, 0)),
                  pl.BlockSpec((MLA_HEADS, SEQ, HEAD_W), lambda s: (0, s, 0)),
                  pl.BlockSpec((MLA_HEADS * VT_ROWS, SEQ), lambda s: (0, s))],
        out_specs=pl.BlockSpec((SEQ, MLA_HEADS * V_DIM), lambda s: (s, 0)),
        out_shape=out_shape,
        compiler_params=_params(("parallel",), VMEM_FLOOR_BYTES),
        name="attn_ctx",
    )(q, k_tok, vt_tok)

    lat0 = ROWS_CTX // DEC_SEQ
    q0 = ROWS_CTX // tq
    nq = DEC_SEQ // tq
    return pl.pallas_call(
        _attn_lat_kernel,
        grid=(DEC_BATCH, MLA_HEADS, nq),
        in_specs=[pl.BlockSpec((None, tq, HEAD_W), lambda b, h, i: (h, q0 + b * nq + i, 0)),
                  pl.BlockSpec((None, DEC_SEQ, HEAD_W), lambda b, h, i: (h, lat0 + b, 0)),
                  pl.BlockSpec((VT_ROWS, DEC_SEQ), lambda b, h, i: (h, lat0 + b)),
                  pl.BlockSpec((None, PAST_LEN, HEAD_W), lambda b, h, i: (h, b, 0)),
                  pl.BlockSpec((VT_ROWS, PAST_LEN), lambda b, h, i: (h, b)),
                  pl.BlockSpec(memory_space=pl.ANY)],
        out_specs=pl.BlockSpec((tq, V_DIM), lambda b, h, i: (q0 + b * nq + i, h)),
        out_shape=out_shape,
        input_output_aliases={5: 0},
        compiler_params=_params(("parallel", "parallel", "arbitrary"),
                                _vmem_limit(_nbytes((DEC_SEQ + PAST_LEN, HEAD_W + V_DIM), BF16),
                                            temps=8 * _nbytes((ATT_CHUNK, tq), F32))),
        name="attn_lat",
    )(q, k_tok, vt_tok, k_cache, vt_cache, o)


def _mm_kernel(a_ref, b_ref, o_ref, acc_ref):
    k = pl.program_id(2)

    @pl.when(k == 0)
    def _():
        acc_ref[...] = jnp.zeros_like(acc_ref)

    acc_ref[...] += _dot(a_ref[...], b_ref[...].astype(BF16))

    @pl.when(k == pl.num_programs(2) - 1)
    def _():
        o_ref[...] = acc_ref[...].astype(o_ref.dtype)


def _mm(a, b, out_dtype=F32, tm=1024, tn=1024, tk=512):
    m, kk = a.shape
    n = b.shape[1]
    tm, tn, tk = min(tm, m), min(tn, n), min(tk, kk)
    return pl.pallas_call(
        _mm_kernel,
        grid=(m // tm, n // tn, kk // tk),
        in_specs=[pl.BlockSpec((tm, tk), lambda i, j, k: (i, k)),
                  pl.BlockSpec((tk, tn), lambda i, j, k: (k, j))],
        out_specs=pl.BlockSpec((tm, tn), lambda i, j, k: (i, j)),
        out_shape=jax.ShapeDtypeStruct((m, n), out_dtype),
        scratch_shapes=[pltpu.VMEM((tm, tn), F32)],
        compiler_params=_params(("parallel", "parallel", "arbitrary"), VMEM_FLOOR_BYTES),
        name="mm",
    )(a, b)


def _seq_tiles(length):
    if length >= 1024:
        return 1024, 512, 1024
    return length, D_MODEL, length


def _dft_fwd_kernel(c_ref, s_ref, z_ref, kr_ref, ks_ref, yr_ref, ys_ref, accr_ref, accs_ref):
    k = pl.program_id(3)

    @pl.when(k == 0)
    def _():
        accr_ref[...] = jnp.zeros_like(accr_ref)
        accs_ref[...] = jnp.zeros_like(accs_ref)

    z = z_ref[...].astype(BF16)
    accr_ref[...] += _dot(c_ref[...], z)
    accs_ref[...] += _dot(s_ref[...], z)

    @pl.when(k == pl.num_programs(3) - 1)
    def _():
        zr, zs, kr, ks = accr_ref[...], accs_ref[...], kr_ref[...], ks_ref[...]
        yr_ref[...] = (zr * kr - zs * ks).astype(BF16)
        ys_ref[...] = (zr * ks + zs * kr).astype(BF16)


def _dft_fwd(tabs, z, z_which, kr, ks, order, n_seq, length):
    tf, tn, tk = _seq_tiles(length)
    nb, nf, nk = D_MODEL // tn, length // tf, length // tk
    kc0 = order * nb
    out = jax.ShapeDtypeStruct((n_seq * length, D_MODEL), BF16)
    return pl.pallas_call(
        _dft_fwd_kernel,
        grid=(n_seq, nf, nb, nk),
        in_specs=[pl.BlockSpec((tf, tk), lambda s, f, n, k: (f, k)),
                  pl.BlockSpec((tf, tk), lambda s, f, n, k: (f, k)),
                  pl.BlockSpec((None, tk, tn), lambda s, f, n, k: (z_which, s * nk + k, n)),
                  pl.BlockSpec((tf, tn), lambda s, f, n, k: (f, kc0 + n)),
                  pl.BlockSpec((tf, tn), lambda s, f, n, k: (f, kc0 + n))],
        out_specs=[pl.BlockSpec((tf, tn), lambda s, f, n, k: (s * nf + f, n)),
                   pl.BlockSpec((tf, tn), lambda s, f, n, k: (s * nf + f, n))],
        out_shape=[out, out],
        scratch_shapes=[pltpu.VMEM((tf, tn), F32), pltpu.VMEM((tf, tn), F32)],
        compiler_params=_params(("parallel", "parallel", "parallel", "arbitrary"),
                                _vmem_limit(2 * _nbytes((tf, tk), BF16) + _nbytes((tk, tn), F32)
                                            + 2 * _nbytes((tf, tn), F32) + 2 * _nbytes((tf, tn), BF16),
                                            resident=2 * _nbytes((tf, tn), F32),
                                            temps=4 * _nbytes((tf, tn), F32))),
        name="dft_fwd",
    )(tabs["c"], tabs["s"], z, kr, ks)


def _dft_inv_kernel(inv_len, ct_ref, st_ref, yr_ref, ys_ref, z_ref, gate_ref, skip_ref,
                    o_ref, acc_ref):
    k = pl.program_id(3)

    @pl.when(k == 0)
    def _():
        acc_ref[...] = jnp.zeros_like(acc_ref)

    acc_ref[...] += _dot(ct_ref[...], yr_ref[...]) + _dot(st_ref[...], ys_ref[...])

    @pl.when(k == pl.num_programs(3) - 1)
    def _():
        y = acc_ref[...] * inv_len + skip_ref[...] * z_ref[...]
        o_ref[...] = (gate_ref[...] * y).astype(o_ref.dtype)


def _dft_inv(tabs, yr, ys, z, z_which, gate, gate_which, skip, order, n_seq, length, out_dtype):
    tt, tn, tk = _seq_tiles(length)
    nb, nt, nk = D_MODEL // tn, length // tt, length // tk
    return pl.pallas_call(
        functools.partial(_dft_inv_kernel, 1.0 / length),
        grid=(n_seq, nt, nb, nk),
        in_specs=[pl.BlockSpec((tt, tk), lambda s, t, n, k: (t, k)),
                  pl.BlockSpec((tt, tk), lambda s, t, n, k: (t, k)),
                  pl.BlockSpec((tk, tn), lambda s, t, n, k: (s * nk + k, n)),
                  pl.BlockSpec((tk, tn), lambda s, t, n, k: (s * nk + k, n)),
                  pl.BlockSpec((None, tt, tn), lambda s, t, n, k: (z_which, s * nt + t, n)),
                  pl.BlockSpec((None, tt, tn), lambda s, t, n, k: (gate_which, s * nt + t, n)),
                  pl.BlockSpec((None, 1, tn), lambda s, t, n, k: (order, 0, n))],
        out_specs=pl.BlockSpec((tt, tn), lambda s, t, n, k: (s * nt + t, n)),
        out_shape=jax.ShapeDtypeStruct((n_seq * length, D_MODEL), out_dtype),
        scratch_shapes=[pltpu.VMEM((tt, tn), F32)],
        compiler_params=_params(("parallel", "parallel", "parallel", "arbitrary"),
                                _vmem_limit(2 * _nbytes((tt, tk), BF16) + 2 * _nbytes((tk, tn), BF16)
                                            + 3 * _nbytes((tt, tn), F32),
                                            resident=_nbytes((tt, tn), F32),
                                            temps=3 * _nbytes((tt, tn), F32))),
        name="dft_inv",
    )(tabs["ct"], tabs["st"], yr, ys, z, gate, skip)


TW_LANES = 128
SUB = 8
CT_ROWS = 2 * SUB
CT_COLS = 1024


def _lane_tile(x, width):
    return jnp.tile(x, (1, width // x.shape[-1]))


def _kron_sub(f):
    return jnp.kron(f, jnp.eye(SUB, dtype=f.dtype))


def _sub_rows(x, h):
    part = x[:, h * SUB:(h + 1) * SUB, :]
    return part.reshape(part.shape[0] * SUB, part.shape[2])


def _from_sub_rows(parts):
    split = [p.reshape(p.shape[0] // SUB, SUB, p.shape[1]) for p in parts]
    return jnp.concatenate(split, axis=1)


def _ct_stage_a_kernel(n_in, *refs):
    x_refs, (fa_ref, cw_ref, sw_ref, br_ref, bi_ref) = refs[:n_in], refs[n_in:]
    fa = fa_ref[...]
    half = fa.shape[0] // 2
    width = br_ref.shape[-1]
    xs = [r[...].astype(F32) for r in x_refs]
    b_re, b_im = [], []
    for h in range(CT_ROWS // SUB):
        x = jnp.concatenate([_sub_rows(x, h) for x in xs], axis=0).astype(BF16)
        a = _dot(fa, x)
        ar, ai = a[:half], a[half:]
        cw, sw = _lane_tile(cw_ref[h], width), _lane_tile(sw_ref[h], width)
        b_re.append(ar * cw + ai * sw)
        b_im.append(ai * cw - ar * sw)
    br_ref[...] = _from_sub_rows(b_re).astype(BF16)
    bi_ref[...] = _from_sub_rows(b_im).astype(BF16)


def _ct_stage_a(xs, n_seq, fa, cw, sw):
    n1 = fa.shape[0] // (2 * SUB)
    n2 = cw.shape[0] * CT_ROWS
    width = xs[0][0].shape[-1]
    tn2, tw = CT_ROWS, CT_COLS
    in_specs, blocks = [], 0
    for arr, prefix in xs:
        rows_in = arr.shape[-3]
        lead = (None,) * (arr.ndim - 3)
        in_specs.append(pl.BlockSpec(lead + (rows_in, tn2, tw),
                                     lambda s, i, c, prefix=prefix: prefix(s) + (0, i, c)))
        blocks += _nbytes((rows_in, tn2, tw), arr.dtype)
    twid = pl.BlockSpec((None,) + cw.shape[1:], lambda s, i, c: (i, 0, 0, 0))
    in_specs += [pl.BlockSpec(fa.shape, lambda s, i, c: (0, 0)), twid, twid]
    out = jax.ShapeDtypeStruct((n_seq, n1, n2, width), BF16)
    out_spec = pl.BlockSpec((None, n1, tn2, tw), lambda s, i, c: (s, 0, i, c))
    return pl.pallas_call(
        functools.partial(_ct_stage_a_kernel, len(xs)),
        grid=(n_seq, n2 // tn2, width // tw),
        in_specs=in_specs,
        out_specs=[out_spec, out_spec],
        out_shape=[out, out],
        compiler_params=_params(("parallel", "parallel", "parallel"),
                                _vmem_limit(blocks + 2 * _nbytes((n1, tn2, tw), BF16)
                                            + _nbytes(fa.shape, BF16),
                                            temps=8 * _nbytes((n1, tn2, tw), F32))),
        name="ct_stage_a",
    )(*[arr for arr, _ in xs], fa, cw, sw)


def _ct_mid_kernel(tk1, br_ref, bi_ref, ur_ref, ui_ref, *refs):
    partners, refs = refs[:2 * tk1], refs[2 * tk1:]
    fb_ref, fbi_ref, fbp0_ref, fbp1_ref, cw_ref, sw_ref, vr_ref, vi_ref = refs
    fb, fbi = fb_ref[...], fbi_ref[...]
    half = fb.shape[0] // 2
    width = br_ref.shape[-1]
    first_tile = pl.program_id(0) == 0
    for j in range(tk1):
        stack = lambda re_ref, im_ref, p=j: jnp.concatenate([re_ref[p], im_ref[p]], axis=0)
        fbp = fbp1_ref[...] if j else jnp.where(first_tile, fbp0_ref[...], fbp1_ref[...])
        u_own = _dot(fb[:half], stack(ur_ref, ui_ref))
        u_neg = _dot(fbp, stack(partners[2 * j], partners[2 * j + 1], 0))
        kr = 0.5 * (u_own + u_neg)
        ki = 0.5 * (u_neg - u_own)
        x = _dot(fb, stack(br_ref, bi_ref))
        xr, xi = x[:half], x[half:]
        y = jnp.concatenate([xr * kr - xi * ki, xr * ki + xi * kr], axis=0).astype(BF16)
        v = _dot(fbi, y)
        vr, vi = v[:half], v[half:]
        cw, sw = _lane_tile(cw_ref[j], width), _lane_tile(sw_ref[j], width)
        vr_ref[j] = (vr * cw - vi * sw).astype(BF16)
        vi_ref[j] = (vi * cw + vr * sw).astype(BF16)


def _ct_mid(br, bi, ur, ui, order, fb, fbi, fbp0, fbp1, cw, sw, tk1=4, td=1024):
    n1, n2, d = br.shape
    nd = d // td
    data = pl.BlockSpec((tk1, n2, td), lambda i, j: (i, 0, j))
    coef = pl.BlockSpec((tk1, n2, td), lambda i, j: (i, 0, order * nd + j))
    page = lambda p: pl.BlockSpec((1, n2, td),
                                  lambda i, j: ((n1 - (i * tk1 + p)) % n1, 0, order * nd + j))
    partners = [page(p) for p in range(tk1) for _ in range(2)]
    mat = pl.BlockSpec(fb.shape, lambda i, j: (0, 0))
    mat_p = pl.BlockSpec(fbp0.shape, lambda i, j: (0, 0))
    tw = pl.BlockSpec((tk1, n2, TW_LANES), lambda i, j: (i, 0, 0))
    out = jax.ShapeDtypeStruct((n1, n2, d), BF16)
    return pl.pallas_call(
        functools.partial(_ct_mid_kernel, tk1),
        grid=(n1 // tk1, nd),
        in_specs=[data, data, coef, coef] + partners + [mat, mat, mat_p, mat_p, tw, tw],
        out_specs=[data, data],
        out_shape=[out, out],
        compiler_params=_params(("parallel", "parallel"),
                                _vmem_limit(8 * _nbytes((tk1, n2, td), BF16),
                                            temps=10 * _nbytes((2 * n2, td), F32))),
        name="ct_mid",
    )(br, bi, ur, ui, *([ur, ui] * tk1), fb, fbi, fbp0, fbp1, cw, sw)


def _ct_inv_a_kernel(scale, chain, vr_ref, vi_ref, fai_ref, z0_ref, z1_ref, g0_ref, g1_ref, skip_ref,
                     *refs):
    if chain:
        fa_ref, cw_ref, sw_ref, o_ref, br_ref, bi_ref = refs
    else:
        (o_ref,) = refs
    fai = fai_ref[...]
    half = fai.shape[0] // 2
    width = o_ref.shape[-1]
    skip = skip_ref[...]
    vr, vi = vr_ref[...].astype(F32), vi_ref[...].astype(F32)
    zs, gs = (z0_ref[...], z1_ref[...]), (g0_ref[...], g1_ref[...])
    outs, b_re, b_im = ([], []), [], []
    for h in range(CT_ROWS // SUB):
        v = jnp.concatenate([_sub_rows(vr, h), _sub_rows(vi, h)], axis=0).astype(BF16)
        y = _dot(fai, v) * scale
        for b, yb in enumerate((y[:half], y[half:])):
            outs[b].append(_sub_rows(gs[b], h) * (yb + skip * _sub_rows(zs[b], h)))
        if chain:
            fa = fa_ref[...]
            x = jnp.concatenate([outs[0][h], outs[1][h]], axis=0).astype(BF16)
            a = _dot(fa, x)
            ar, ai = a[:fa.shape[0] // 2], a[fa.shape[0] // 2:]
            cw, sw = _lane_tile(cw_ref[h], width), _lane_tile(sw_ref[h], width)
            b_re.append(ar * cw + ai * sw)
            b_im.append(ai * cw - ar * sw)
    for b in range(2):
        o_ref[b] = _from_sub_rows(outs[b]).astype(o_ref.dtype)
    if chain:
        br_ref[...] = _from_sub_rows(b_re).astype(BF16)
        bi_ref[...] = _from_sub_rows(b_im).astype(BF16)


def _ct_inv_a(vr, vi, fai, z, z_which, gate, gate_which, skip, order, out_dtype, chain=None):
    n1, n2, d = vr.shape
    rows = fai.shape[0] // (2 * SUB)
    tn2, tw = CT_ROWS, CT_COLS // 2
    spec = pl.BlockSpec((n1, tn2, tw), lambda i, c: (0, i, c))
    pair = lambda which, b: pl.BlockSpec((None, None, rows, tn2, tw),
                                         lambda i, c: (which[0], which[1] + b, 0, i, c))
    in_specs = [spec, spec, pl.BlockSpec(fai.shape, lambda i, c: (0, 0)),
                pair(z_which, 0), pair(z_which, 1), pair(gate_which, 0), pair(gate_which, 1),
                pl.BlockSpec((None, 1, tw), lambda i, c: (order, 0, c))]
    out_specs = [pl.BlockSpec((2, rows, tn2, tw), lambda i, c: (0, 0, i, c))]
    out_shape = [jax.ShapeDtypeStruct((2, rows, n2, d), out_dtype)]
    args = [vr, vi, fai, z, z, gate, gate, skip]
    if chain is not None:
        fa, cw, sw = chain
        twid = pl.BlockSpec((None,) + cw.shape[1:], lambda i, c: (i, 0, 0, 0))
        in_specs += [pl.BlockSpec(fa.shape, lambda i, c: (0, 0)), twid, twid]
        out_specs += [spec, spec]
        out_shape += [jax.ShapeDtypeStruct((n1, n2, d), BF16)] * 2
        args += [fa, cw, sw]
    return pl.pallas_call(
        functools.partial(_ct_inv_a_kernel, 1.0 / (n1 * n2), chain is not None),
        grid=(n2 // tn2, d // tw),
        in_specs=in_specs,
        out_specs=out_specs,
        out_shape=out_shape,
        compiler_params=_params(("parallel", "parallel"),
                                _vmem_limit(4 * _nbytes((n1, tn2, tw), BF16)
                                            + 6 * _nbytes((rows, tn2, tw), F32)
                                            + 2 * _nbytes(fai.shape, BF16),
                                            temps=10 * _nbytes((n1, tn2, tw), F32))),
        name="ct_inv_a",
    )(*args)


def _ct_real_b_kernel(scale, br_ref, bi_ref, fb_ref, o_ref, so_ref):
    fb = fb_ref[...]
    for j in range(br_ref.shape[0]):
        so_ref[:, j, :] = _dot(fb, jnp.concatenate([br_ref[j], bi_ref[j]], axis=0)) * scale
    o_ref[...] = so_ref[...].astype(o_ref.dtype)


def _ct_real_b(br, bi, fb_re, scale):
    n_seq, n1, n2, d = br.shape
    tk1, tw = CT_ROWS, CT_COLS
    blk = pl.BlockSpec((None, tk1, n2, tw), lambda s, i, c: (s, i, 0, c))
    return pl.pallas_call(
        functools.partial(_ct_real_b_kernel, scale),
        grid=(n_seq, n1 // tk1, d // tw),
        in_specs=[blk, blk, pl.BlockSpec(fb_re.shape, lambda s, i, c: (0, 0))],
        out_specs=pl.BlockSpec((None, n2, tk1, tw), lambda s, i, c: (s, 0, i, c)),
        out_shape=jax.ShapeDtypeStruct((n_seq, n2, n1, d), BF16),
        scratch_shapes=[pltpu.VMEM((n2, tk1, tw), F32)],
        compiler_params=_params(("parallel", "parallel", "parallel"),
                                _vmem_limit(3 * _nbytes((tk1, n2, tw), BF16),
                                            resident=_nbytes((n2, tk1, tw), F32),
                                            temps=2 * _nbytes((n2, tk1, tw), F32))),
        name="ct_real_b",
    )(br, bi, fb_re)


def _cos_sin(num, den):
    ang = (num % den).astype(F32) * (2.0 * math.pi / den)
    return jnp.cos(ang), jnp.sin(ang)


def _ct_tables(n1, n2):
    i1 = jnp.arange(n1, dtype=jnp.int32)
    i2 = jnp.arange(n2, dtype=jnp.int32)
    c1, s1 = _cos_sin(i1[:, None] * i1[None, :], n1)
    c2, s2 = _cos_sin(i2[:, None] * i2[None, :], n2)
    cw, sw = _cos_sin(i2[:, None] * i1[None, :], n1 * n2)
    lanes = lambda t: jnp.broadcast_to(t[..., None], t.shape + (TW_LANES,))

    def stage_a_rows(t):
        t = t.reshape(n2 // CT_ROWS, CT_ROWS // SUB, SUB, n1)
        return lanes(jnp.swapaxes(t, 2, 3).reshape(n2 // CT_ROWS, CT_ROWS // SUB, n1 * SUB))

    return {"c1": c1, "s1": s1, "c2": c2, "s2": s2,
            "cw_a": stage_a_rows(cw), "sw_a": stage_a_rows(sw),
            "cw_b": lanes(cw.T), "sw_b": lanes(sw.T)}


def _filter_rows(feat_ref, t_ref, w1_ref, b1_ref, fr1_ref, w2_ref, b2_ref, fr2_ref, w3_ref,
                 decay_ref):
    x = jnp.sin(fr1_ref[...] * (_dot(feat_ref[...].astype(BF16), w1_ref[...].astype(BF16))
                                + b1_ref[...]))
    x = jnp.sin(fr2_ref[...] * (_dot(x.astype(BF16), w2_ref[...].astype(BF16)) + b2_ref[...]))
    h = _dot(x.astype(BF16), w3_ref[...].astype(BF16))
    return h * (jnp.exp(-t_ref[...] * jnp.exp(decay_ref[...])) + HY_SHIFT)


def _filter_stats_kernel(*refs):
    ss_ref = refs[-1]
    h = _filter_rows(*refs[:-1])

    @pl.when(pl.program_id(0) == 0)
    def _():
        ss_ref[...] = jnp.zeros_like(ss_ref)

    ss_ref[...] += jnp.sum(h * h, axis=0, keepdims=True)


def _filter_emit_kernel(*refs):
    ss_ref, a_ref, b_ref = refs[-3:]
    h = _filter_rows(*refs[:-3])
    n_dir = a_ref.shape[1]
    ss = ss_ref[...]
    norm = lax.rsqrt(ss[:, :n_dir] + ss[:, n_dir:] + 1e-12)
    fwd = h[:, :n_dir] * norm
    bwd = h[:, n_dir:] * norm
    row = lax.broadcasted_iota(jnp.int32, bwd.shape, 0) + pl.program_id(0) * bwd.shape[0]
    bwd = jnp.where(row == 0, 0.0, bwd)
    a_ref[...] = (fwd + bwd).astype(BF16)
    b_ref[...] = (fwd - bwd).astype(BF16)


def _hyena_filters(length, f_w1, f_b1, f_freq1, f_w2, f_b2, f_freq2, f_w3, log_decay):
    t = jnp.linspace(0.0, 1.0, length, dtype=F32)[:, None]
    t_idx = jnp.arange(length, dtype=F32)[:, None]
    bands = jnp.linspace(1e-4, HY_BANDS - 1, HY_BANDS, dtype=F32)
    w = 2.0 * math.pi * t_idx * bands / length
    feat = jnp.concatenate([t, jnp.cos(w), -jnp.sin(w)], axis=-1)
    emb_pad = 128
    feat = jnp.pad(feat, ((0, 0), (0, emb_pad - HY_EMB)))
    w1 = jnp.pad(f_w1, ((0, emb_pad - HY_EMB), (0, 0)))
    n_all = HY_DIRS * HY_ORDER * D_MODEL
    n_dir = HY_ORDER * D_MODEL
    tm = 256
    full = lambda shape: pl.BlockSpec(shape, lambda i: (0, 0))
    mlp_specs = [pl.BlockSpec((tm, emb_pad), lambda i: (i, 0)),
                 pl.BlockSpec((tm, 1), lambda i: (i, 0)),
                 full((emb_pad, HY_FW)), full((1, HY_FW)), full((1, HY_FW)),
                 full((HY_FW, HY_FW)), full((1, HY_FW)), full((1, HY_FW)),
                 full((HY_FW, n_all)), full((1, n_all))]
    mlp_args = (feat, t, w1, f_b1.reshape(1, HY_FW), f_freq1.reshape(1, HY_FW), f_w2,
                f_b2.reshape(1, HY_FW), f_freq2.reshape(1, HY_FW), f_w3, log_decay.reshape(1, n_all))
    vmem = _vmem_limit(_nbytes((tm, n_all), F32) + _nbytes((HY_FW, n_all), F32),
                       temps=4 * _nbytes((tm, n_all), F32))
    ss = pl.pallas_call(
        _filter_stats_kernel,
        grid=(length // tm,),
        in_specs=mlp_specs,
        out_specs=full((1, n_all)),
        out_shape=jax.ShapeDtypeStruct((1, n_all), F32),
        compiler_params=_params(("arbitrary",), vmem),
        name="filter_stats",
    )(*mlp_args)
    comb = jax.ShapeDtypeStruct((length, n_dir), BF16)
    return pl.pallas_call(
        _filter_emit_kernel,
        grid=(length // tm,),
        in_specs=mlp_specs + [full((1, n_all))],
        out_specs=[pl.BlockSpec((tm, n_dir), lambda i: (i, 0)), pl.BlockSpec((tm, n_dir), lambda i: (i, 0))],
        out_shape=[comb, comb],
        compiler_params=_params(("parallel",), vmem),
        name="filter_emit",
    )(*mlp_args, ss)


def _cis_product(row_hi, row_lo, period):
    def cis(phase):
        ang = (phase % period).astype(F32) * (2.0 * math.pi / period)
        return jnp.cos(ang)[:, :, None], jnp.sin(ang)[:, :, None]
    (c1, s1), (c0, s0) = cis(row_hi), cis(row_lo)
    c0, s0 = jnp.swapaxes(c0, 1, 2), jnp.swapaxes(s0, 1, 2)
    rows = row_hi.shape[0]
    return ((c1 * c0 - s1 * s0).reshape(rows, -1), (s1 * c0 + c1 * s0).reshape(rows, -1))


def _odd_dft_tables(length):
    split = 1 << (length.bit_length() // 2)
    r = jnp.arange(length, dtype=jnp.int32)[:, None]
    hi = jnp.arange(length // split, dtype=jnp.int32)[None, :] * split
    lo = jnp.arange(split, dtype=jnp.int32)[None, :]
    c, s = _cis_product((2 * r + 1) * hi, (2 * r + 1) * lo, 4 * length)
    ct, st = _cis_product(r * (2 * hi), r * (2 * lo + 1), 4 * length)
    return {"c": c.astype(BF16), "s": s.astype(BF16), "ct": ct.astype(BF16), "st": st.astype(BF16)}


def _dft_tables(length):
    split = 1 << (length.bit_length() // 2)
    r = jnp.arange(length, dtype=jnp.int32)[:, None]
    hi = jnp.arange(length // split, dtype=jnp.int32)[None, :] * split
    lo = jnp.arange(split, dtype=jnp.int32)[None, :]
    c, s = _cis_product(r * hi, r * lo, length)
    return c.astype(BF16), (-s).astype(BF16)


def _hyena_mix_dense(pc, filt_p, filt_m, skip, n_seq, length):
    tabs = _odd_dft_tables(length)
    kr, ks = _mm(tabs["c"], filt_p), _mm(tabs["s"], filt_m)
    yr, ys = _dft_fwd(tabs, pc, 0, kr, ks, 0, n_seq, length)
    z1 = _dft_inv(tabs, yr, ys, pc, 0, pc, 1, skip, 0, n_seq, length, F32)[None]
    yr, ys = _dft_fwd(tabs, z1, 0, kr, ks, 1, n_seq, length)
    return _dft_inv(tabs, yr, ys, z1, 0, pc, 2, skip, 1, n_seq, length, BF16)


HY_N1, HY_N2 = 64, 128


def _hyena_mix_pair(pc, filt_p, filt_m, skip):
    assert DEC_BATCH == 2 and HY_N1 * HY_N2 == 2 * DEC_SEQ and ROWS % DEC_SEQ == 0
    n1, n2, rows_in = HY_N1, HY_N2, HY_N1 // 2
    t = _ct_tables(n1, n2)
    c_in, s_in = t["c1"][:, :rows_in], t["s1"][:, :rows_in]
    fa = _kron_sub(jnp.block([[c_in, s_in], [-s_in, c_in]])).astype(BF16)
    fb = jnp.block([[t["c2"], t["s2"]], [-t["s2"], t["c2"]]]).astype(BF16)
    fbi = jnp.block([[t["c2"], -t["s2"]], [t["s2"], t["c2"]]]).astype(BF16)
    mirror = lambda shift: (-jnp.arange(n2) - shift) % n2
    fbp0, fbp1 = [jnp.concatenate([t["c2"][mirror(sh)], t["s2"][mirror(sh)]], axis=1).astype(BF16)
                  for sh in (0, 1)]
    c_out, s_out = t["c1"][:rows_in], t["s1"][:rows_in]
    fai = _kron_sub(jnp.block([[c_out, -s_out], [s_out, c_out]])).astype(BF16)
    n_filt = HY_ORDER * D_MODEL
    filt_view = lambda f: (f.reshape(rows_in, n2, n_filt), lambda s: ())
    ur, ui = _ct_stage_a([filt_view(filt_p), filt_view(filt_m)], 1, fa, t["cw_a"], t["sw_a"])

    pc = pc.reshape(3, ROWS // DEC_SEQ, rows_in, n2, D_MODEL)
    lat0 = ROWS_CTX // DEC_SEQ
    stage_a = (fa, t["cw_a"], t["sw_a"])
    br, bi = _ct_stage_a([(pc, lambda s: (0, lat0)), (pc, lambda s: (0, lat0 + 1))], 1, *stage_a)
    z, z0, br, bi = pc, lat0, br[0], bi[0]
    for order in range(HY_ORDER):
        vr, vi = _ct_mid(br, bi, ur[0], ui[0], order, fb, fbi, fbp0, fbp1, t["cw_b"], t["sw_b"])
        more = order + 1 < HY_ORDER
        outs = _ct_inv_a(vr, vi, fai, z, (0, z0), pc, (1 + order, lat0), skip, order,
                         F32 if more else BF16, chain=stage_a if more else None)
        z, z0 = outs[0][None], 0
        if more:
            br, bi = outs[1], outs[2]
    return z.reshape(ROWS_LAT, D_MODEL)


def _fnet_chan_kernel(h_ref, sh_ref, sc_ref, w_ref, p_ref, q_ref):
    u = _modulate(h_ref[...], sh_ref[...], sc_ref[...]).astype(BF16)
    w = w_ref[...]
    for g in range(FNET_GROUPS):
        cols = slice(g * FNET_CG, (g + 1) * FNET_CG)
        r = _dot(u[:, cols], w)
        p_ref[:, cols] = r[:, :FNET_CG].astype(BF16)
        q_ref[:, cols] = r[:, FNET_CG:].astype(BF16)


def _fnet_chan(h, mods, w_cs, tm=512):
    row = pl.BlockSpec((tm, D_MODEL), lambda i: (i, 0))
    out = jax.ShapeDtypeStruct((ROWS, D_MODEL), BF16)
    return pl.pallas_call(
        _fnet_chan_kernel,
        grid=(ROWS // tm,),
        in_specs=[row, _mod_spec(0, tm), _mod_spec(1, tm),
                  pl.BlockSpec((FNET_CG, 2 * FNET_CG), lambda i: (0, 0))],
        out_specs=[row, row],
        out_shape=[out, out],
        compiler_params=_params(("parallel",),
                                _vmem_limit(3 * _nbytes((tm, D_MODEL), F32),
                                            temps=2 * _nbytes((tm, D_MODEL), F32))),
        name="fnet_chan",
    )(h, mods, mods, w_cs)


def _fnet_pos_kernel(scale, c_ref, ns_ref, p_ref, q_ref, o_ref, acc_ref):
    k = pl.program_id(3)

    @pl.when(k == 0)
    def _():
        acc_ref[...] = jnp.zeros_like(acc_ref)

    acc_ref[...] += (_dot(c_ref[...], p_ref[...].astype(BF16))
                     + _dot(ns_ref[...], q_ref[...].astype(BF16)))

    @pl.when(k == pl.num_programs(3) - 1)
    def _():
        o_ref[...] = (acc_ref[...] * scale).astype(o_ref.dtype)


def _fnet_pos(c_tab, ns_tab, p, q, n_seq, length):
    tt, tn, tk = _seq_tiles(length)
    nb, nt, nk = D_MODEL // tn, length // tt, length // tk
    scale = (length * FNET_CG) ** -0.5
    return pl.pallas_call(
        functools.partial(_fnet_pos_kernel, scale),
        grid=(n_seq, nt, nb, nk),
        in_specs=[pl.BlockSpec((tt, tk), lambda s, t, n, k: (t, k)),
                  pl.BlockSpec((tt, tk), lambda s, t, n, k: (t, k)),
                  pl.BlockSpec((tk, tn), lambda s, t, n, k: (s * nk + k, n)),
                  pl.BlockSpec((tk, tn), lambda s, t, n, k: (s * nk + k, n))],
        out_specs=pl.BlockSpec((tt, tn), lambda s, t, n, k: (s * nt + t, n)),
        out_shape=jax.ShapeDtypeStruct((n_seq * length, D_MODEL), BF16),
        scratch_shapes=[pltpu.VMEM((tt, tn), F32)],
        compiler_params=_params(("parallel", "parallel", "parallel", "arbitrary"),
                                _vmem_limit(2 * _nbytes((tt, tk), BF16) + 2 * _nbytes((tk, tn), BF16)
                                            + _nbytes((tt, tn), BF16),
                                            resident=_nbytes((tt, tn), F32),
                                            temps=2 * _nbytes((tt, tn), F32))),
        name="fnet_pos",
    )(c_tab, ns_tab, p, q)


FN_N1, FN_N2 = 32, 128


def _fnet_pos_factored(p, q):
    assert FN_N1 * FN_N2 == DEC_SEQ and ROWS % DEC_SEQ == 0
    n1, n2 = FN_N1, FN_N2
    t = _ct_tables(n1, n2)
    fa = _kron_sub(jnp.block([[t["c1"], -t["s1"]], [-t["s1"], -t["c1"]]])).astype(BF16)
    fb_re = jnp.concatenate([t["c2"], t["s2"]], axis=1).astype(BF16)
    lat0 = ROWS_CTX // DEC_SEQ
    view = lambda x: x.reshape(ROWS // DEC_SEQ, n1, n2, D_MODEL)
    seq = lambda s: (lat0 + s,)
    br, bi = _ct_stage_a([(view(p), seq), (view(q), seq)], DEC_BATCH, fa, t["cw_a"], t["sw_a"])
    f = _ct_real_b(br, bi, fb_re, (DEC_SEQ * FNET_CG) ** -0.5)
    return f.reshape(ROWS_LAT, D_MODEL)


def _rope_table():
    rows = DEC_SEQ // GRID_W
    row = jnp.repeat(jnp.arange(rows), GRID_W).astype(F32)
    col = jnp.tile(jnp.arange(GRID_W), rows).astype(F32)
    inv = ROPE_THETA ** (-jnp.arange(0, AXIS_ROPE, 2, dtype=F32) / AXIS_ROPE)
    ang = jnp.concatenate([row[:, None] * inv, col[:, None] * inv], axis=-1)
    cos = jnp.repeat(jnp.cos(ang), 2, axis=-1)
    sin = jnp.repeat(jnp.sin(ang), 2, axis=-1)
    lat = jnp.tile(jnp.concatenate([cos, sin], axis=-1), (DEC_BATCH, 1))
    ctx = jnp.concatenate([jnp.ones((ROWS_CTX, QK_ROPE), F32), jnp.zeros((ROWS_CTX, QK_ROPE), F32)],
                          axis=-1)
    return jnp.concatenate([ctx, lat], axis=0)


def _pair_rotated(w):
    pairs = w.reshape(w.shape[:-1] + (QK_ROPE // 2, 2))
    return jnp.stack([-pairs[..., 1], pairs[..., 0]], axis=-1).reshape(w.shape)


def kernel(x_prompt, x_sample, c, cache_ckv, cache_krope, c_ctx, ada_w, ada_b, ln_g, ln_b, ffn_w_gate, ffn_w_up, ffn_w_down, mla_w_dq, mla_q_norm, mla_w_uq, mla_w_dkv, mla_kv_norm, mla_w_kr, mla_w_ukv, mla_w_o, hy_w_in, hy_b_in, hy_conv_w, hy_conv_b, hy_f_w1, hy_f_b1, hy_f_freq1, hy_f_w2, hy_f_b2, hy_f_freq2, hy_f_w3, hy_log_decay, hy_skip, hy_w_out, hy_b_out, fn_w_out, fn_b_out):
    assert x_prompt.shape == (BATCH, SEQ, D_MODEL) and x_sample.shape == (DEC_BATCH, DEC_SEQ, D_MODEL)
    assert ROWS_CTX % DEC_SEQ == 0 and SEQ == FNET_CG

    assert N_MIXERS > 0 and DEPTH > 1
    h = (x_prompt.reshape(ROWS_CTX, D_MODEL), x_sample.reshape(ROWS_LAT, D_MODEL))
    cond = jnp.concatenate([c_ctx[None, :], c, jnp.zeros((COND_PAD - N_COND, D_MODEL), F32)])
    mods_all = _modulation_vectors(cond, ada_w, ada_b)
    zero_bias = jnp.zeros((D_MODEL,), F32)
    ffn_w = (ffn_w_gate.astype(BF16), ffn_w_up.astype(BF16), ffn_w_down)
    rope_tab = None
    ckv_states, krope_states = [], []

    for i in range(DEPTH):
        kind, j = i % N_MIXERS, i // N_MIXERS
        mods = mods_all[i]
        if kind == 0:
            if rope_tab is None:
                rope_tab = _rope_table()
            w_kr2 = jnp.concatenate([mla_w_kr[j], _pair_rotated(mla_w_kr[j])], axis=-1).astype(BF16)
            wq = mla_w_uq[j].reshape(Q_RANK, MLA_HEADS, QK_NOPE + QK_ROPE)
            w_q = jnp.concatenate([wq, _pair_rotated(wq[..., QK_NOPE:])], axis=-1)
            w_q = w_q.reshape(Q_RANK, MLA_HEADS * HEAD_W).astype(BF16)
            w_ukv = mla_w_ukv[j].reshape(KV_RANK, MLA_HEADS, QK_NOPE + V_DIM)
            w_k = w_ukv[..., :QK_NOPE].reshape(KV_RANK, MLA_HEADS * QK_NOPE).astype(BF16)
            w_vt = w_ukv[..., QK_NOPE:].reshape(KV_RANK, MLA_HEADS * V_DIM).T.astype(BF16)
            cq, ckv, kr, kr2 = _mla_down(h, mods, mla_w_dq[j].astype(BF16), mla_w_dkv[j].astype(BF16),
                                         w_kr2, mla_q_norm[j], mla_kv_norm[j], rope_tab)
            ckv_states.append(ckv[:ROWS_CTX].reshape(BATCH, SEQ, KV_RANK))
            krope_states.append(kr[:ROWS_CTX].reshape(BATCH, SEQ, QK_ROPE))
            q = _q_up(cq, w_q, rope_tab)
            k_tok, vt_tok = _kv_expand(ckv, kr2, w_k, w_vt)
            kc = cache_krope[:, j].reshape(DEC_BATCH * PAST_LEN, QK_ROPE).astype(BF16)
            k_cache, vt_cache = _kv_expand(cache_ckv[:, j].reshape(DEC_BATCH * PAST_LEN, KV_RANK),
                                           jnp.concatenate([kc, kc], axis=-1), w_k, w_vt)
            o = _attention(q, k_tok, vt_tok, k_cache, vt_cache)
            h = _mm_postnorm(o, mla_w_o[j].astype(BF16), zero_bias, h, mods, 2, ln_g[i, 0], ln_b[i, 0])
        elif kind == 1:
            pc = _hyena_in(h, mods, hy_w_in[j].astype(BF16), hy_b_in[j], hy_conv_w[j], hy_conv_b[j])
            fp = (hy_f_w1[j], hy_f_b1[j], hy_f_freq1[j], hy_f_w2[j], hy_f_b2[j], hy_f_freq2[j],
                  hy_f_w3[j], hy_log_decay[j])
            skip = hy_skip[j].reshape(HY_ORDER, 1, D_MODEL)
            z_ctx = _hyena_mix_dense(pc, *_hyena_filters(SEQ, *fp), skip, BATCH, SEQ)
            z_lat = _hyena_mix_pair(pc, *_hyena_filters(DEC_SEQ, *fp), skip)
            h = _mm_postnorm((z_ctx, z_lat), hy_w_out[j].astype(BF16), hy_b_out[j],
                             h, mods, 2, ln_g[i, 0], ln_b[i, 0])
        else:
            c_ch, ns_ch = _dft_tables(FNET_CG)
            p, q = _fnet_chan(h, mods, jnp.concatenate([c_ch, -ns_ch], axis=-1))
            f_ctx = _fnet_pos(c_ch, ns_ch, p, q, BATCH, SEQ)
            f_lat = _fnet_pos_factored(p, q)
            h = _mm_postnorm((f_ctx, f_lat), fn_w_out[j].astype(BF16), fn_b_out[j],
                             h, mods, 2, ln_g[i, 0], ln_b[i, 0])
        h = _ffn(h, mods, *ffn_w, i, ln_g[i, 1], ln_b[i, 1], split_out=i == DEPTH - 1)

    y_prompt = h[0].reshape(BATCH, SEQ, D_MODEL)
    y_sample = h[1].reshape(DEC_BATCH, DEC_SEQ, D_MODEL)
    return (y_prompt, y_sample, jnp.stack(ckv_states, axis=1), jnp.stack(krope_states, axis=1))
```
